```python
import math, functools
import jax, jax.numpy as jnp
from jax import lax
import numpy as np

D_MODEL = 1024
BATCH = 32
SEQ = 2048
DEPTH = 4

CTX_LEN = 256
GRID_W = 64
N_MOD = 6
RMS_EPS = 1e-6
N_EVEN = (DEPTH + 1) // 2
N_ODD = DEPTH // 2

SSD_W = D_MODEL
SSD_HEAD_DIM = 64
SSD_HEADS = SSD_W // SSD_HEAD_DIM
SSD_GROUPS = 2
SSD_HPG = SSD_HEADS // SSD_GROUPS
SSD_STATE = 128
SSD_BC = SSD_GROUPS * SSD_STATE
SSD_XBC = SSD_W + 2 * SSD_BC
SSD_CONV = 4
SSD_CHUNK = 128
LRU_W = D_MODEL
LRU_BLOCKS = 8
LRU_BLOCK_W = LRU_W // LRU_BLOCKS
LRU_CONV = 4
LRU_C = 8.0
EV_SPLITS = (SSD_W, SSD_W + SSD_XBC, SSD_W + SSD_XBC + 2 * SSD_HEADS, SSD_W + SSD_XBC + 2 * SSD_HEADS + LRU_W)
EV_IN = SSD_W + SSD_XBC + 2 * SSD_HEADS + 2 * LRU_W
EV_MIX = SSD_W + LRU_W
HG_W = 3 * D_MODEL // 4
HG_DK = 128
HG_DV = 128
HG_HEADS = HG_W // HG_DK
HG_CHUNK = 32
S5_W = D_MODEL // 4
S5_GROUP_CH = 16
S5_GROUPS = S5_W // S5_GROUP_CH
S5_STATE = 64
OD_SPLITS = (HG_W, 2 * HG_W, 3 * HG_W, 4 * HG_W, 5 * HG_W)
OD_IN = 5 * HG_W + S5_W
OD_MIX = HG_W + S5_W
D_FF = 2816
FFN_CONV = 3

kernel_name = "hybrid_ssd_rglru_hgrn2_s5_diffusion_trunk"


def rms_norm(t, g):
    tf = t.astype(jnp.float32)
    tf = tf * lax.rsqrt(jnp.mean(jnp.square(tf), axis=-1, keepdims=True) + RMS_EPS)
    return (tf * g.astype(jnp.float32)).astype(t.dtype)


def modulate(t, shift, scale):
    return t * (1 + scale) + shift


def depthwise_conv1d(t, w, b):
    k = w.shape[0]
    y = lax.conv_general_dilated(t, w[:, None, :], (1,), [((k - 1) // 2, k // 2)],
                                 dimension_numbers=("NWC", "WIO", "NWC"),
                                 feature_group_count=t.shape[-1])
    return y + b


def depthwise_conv2d(t, w, b):
    y = lax.conv_general_dilated(t, w[:, :, None, :], (1, 1), [(1, 1), (1, 1)],
                                 dimension_numbers=("NHWC", "HWIO", "NHWC"),
                                 feature_group_count=t.shape[-1])
    return y + b


def to_chunks(t, size):
    b, n = t.shape[:2]
    return jnp.swapaxes(t.reshape((b, n // size, size) + t.shape[2:]), 0, 1)


def from_chunks(t):
    t = jnp.swapaxes(t, 0, 1)
    return t.reshape((t.shape[0], t.shape[1] * t.shape[2]) + t.shape[3:])


def run_direction(scan_fn, ctx_seqs, lat_seqs, h0, reverse):
    flip = (lambda t: jnp.flip(t, axis=1)) if reverse else (lambda t: t)
    y_c, h_c = scan_fn(*map(flip, ctx_seqs), h0)
    y_x, _ = scan_fn(*map(flip, lat_seqs), h_c)
    return flip(y_c), flip(y_x)


def ssd_scan(x, dt, bm, cm, h0, a):
    log_a = dt * a
    xdt = x * dt[..., None]
    mask = jnp.tril(jnp.ones((SSD_CHUNK, SSD_CHUNK), dtype=bool))

    def step(h, inp):
        xc, lac, bc, cc = inp
        cum = jnp.cumsum(lac, axis=1)
        seg = cum[:, :, None] - cum[:, None]
        decay = jnp.exp(jnp.where(mask[None, :, :, None, None], seg, -jnp.inf))
        cb = jnp.einsum("blgn,bsgn->blsg", cc, bc)
        y = jnp.einsum("blsgh,bsghp->blghp", cb[..., None] * decay, xc)
        y = y + jnp.einsum("blgn,bghpn->blghp", cc, h) * jnp.exp(cum)[..., None]
        to_end = jnp.exp(cum[:, -1:] - cum)
        h = h * jnp.exp(cum[:, -1])[..., None, None] + jnp.einsum(
            "bsgn,bsghp->bghpn", bc, xc * to_end[..., None])
        return h, y

    h, ys = lax.scan(step, h0, tuple(to_chunks(t, SSD_CHUNK) for t in (xdt, log_a, bm, cm)))
    return from_chunks(ys), h


def linear_recurrence(a, b, h0):
    b = b.at[:, 0].add(a[:, 0] * h0)

    def combine(l, r):
        return l[0] * r[0], r[0] * l[1] + r[1]

    return lax.associative_scan(combine, (a, b), axis=1)[1]


def rglru_scan(u, h0, w_a, b_a, w_i, b_i, lam):
    r = jax.nn.sigmoid(jnp.einsum("btnk,nkj->btnj", u, w_a) + b_a)
    i = jax.nn.sigmoid(jnp.einsum("btnk,nkj->btnj", u, w_i) + b_i)
    log_a = -LRU_C * jax.nn.softplus(-lam) * r
    a = jnp.exp(log_a)
    bx = jnp.sqrt(-jnp.expm1(2 * log_a)) * (i * u)
    h = linear_recurrence(a, bx, h0)
    return h, h[:, -1]


def hgrn_scan(q, k, log_f, v, s0):
    mask = jnp.tril(jnp.ones((HG_CHUNK, HG_CHUNK), dtype=bool))

    def step(s, inp):
        qc, kc, gc, vc = inp
        cum = jnp.cumsum(gc, axis=1)
        seg = cum[:, :, None] - cum[:, None]
        decay = jnp.exp(jnp.where(mask[None, :, :, None, None], seg, -jnp.inf))
        att = jnp.einsum("blhk,blshk->blsh", qc, decay * kc[:, None])
        y = jnp.einsum("blsh,bshv->blhv", att, vc)
        y = y + jnp.einsum("blhk,bhkv->blhv", qc * jnp.exp(cum), s)
        s = s * jnp.exp(cum[:, -1])[..., None] + jnp.einsum(
            "bshk,bshv->bhkv", kc * jnp.exp(cum[:, -1:] - cum), vc)
        return s, y

    s, ys = lax.scan(step, s0, tuple(to_chunks(t, HG_CHUNK) for t in (q, k, log_f, v)))
    return from_chunks(ys), s


def s5_scan(u, h0, lam_re, lam_im, log_step, b_re, b_im, c_re, c_im):
    step = jnp.exp(log_step)[:, None]
    mag = jnp.exp(lam_re * step)
    ar, ai = mag * jnp.cos(lam_im * step), mag * jnp.sin(lam_im * step)
    den = lam_re * lam_re + lam_im * lam_im
    zr = ((ar - 1) * lam_re + ai * lam_im) / den
    zi = (ai * lam_re - (ar - 1) * lam_im) / den
    bbr = zr[..., None] * b_re - zi[..., None] * b_im
    bbi = zr[..., None] * b_im + zi[..., None] * b_re
    xr = jnp.einsum("btgk,gpk->btgp", u, bbr)
    xi = jnp.einsum("btgk,gpk->btgp", u, bbi)
    h0r, h0i = h0
    xr = xr.at[:, 0].add(ar * h0r - ai * h0i)
    xi = xi.at[:, 0].add(ar * h0i + ai * h0r)
    a_r = jnp.broadcast_to(ar, xr.shape)
    a_i = jnp.broadcast_to(ai, xi.shape)

    def combine(l, r):
        lar, lai, lbr, lbi = l
        rar, rai, rbr, rbi = r
        return (lar * rar - lai * rai, lar * rai + lai * rar,
                rar * lbr - rai * lbi + rbr, rar * lbi + rai * lbr + rbi)

    _, _, hr, hi = lax.associative_scan(combine, (a_r, a_i, xr, xi), axis=1)
    y = jnp.einsum("btgp,gkp->btgk", hr, c_re) - jnp.einsum("btgp,gkp->btgk", hi, c_im)
    return y, (hr[:, -1], hi[:, -1])


def even_mixer(hx, hc, w_in, w_out, ssd_conv_w, ssd_conv_b, ssd_dt_bias, ssd_a_log, ssd_d, ssd_norm_g,
               lru_conv_w, lru_conv_b, lru_w_a, lru_b_a, lru_w_i, lru_b_i, lru_lam, need_ctx):
    ssd_a = -jnp.exp(ssd_a_log).reshape(2, SSD_GROUPS, SSD_HPG)

    def prepare(h):
        bsz, t = h.shape[:2]
        z, xbc, dt, gy, u = jnp.split(h @ w_in, EV_SPLITS, axis=-1)
        xbc = jax.nn.silu(depthwise_conv1d(xbc, ssd_conv_w, ssd_conv_b))
        xs, bm, cm = jnp.split(xbc, [SSD_W, SSD_W + SSD_BC], axis=-1)
        dt = jax.nn.softplus(dt.reshape(bsz, t, 2, SSD_HEADS) + ssd_dt_bias)
        u = depthwise_conv1d(u, lru_conv_w, lru_conv_b)
        return {"z": z, "gy": gy,
                "x": xs.reshape(bsz, t, SSD_GROUPS, SSD_HPG, SSD_HEAD_DIM),
                "b": bm.reshape(bsz, t, SSD_GROUPS, SSD_STATE),
                "c": cm.reshape(bsz, t, SSD_GROUPS, SSD_STATE),
                "dt": dt.reshape(bsz, t, 2, SSD_GROUPS, SSD_HPG),
                "u": u.reshape(bsz, t, LRU_BLOCKS, LRU_BLOCK_W)}

    pc, px = prepare(hc), prepare(hx)
    bsz = hx.shape[0]
    ssd_h0 = jnp.zeros((bsz, SSD_GROUPS, SSD_HPG, SSD_HEAD_DIM, SSD_STATE), hx.dtype)
    lru_h0 = jnp.zeros((bsz, LRU_BLOCKS, LRU_BLOCK_W), hx.dtype)
    ssd_dirs, lru_dirs = [], []
    for d, reverse in enumerate((False, True)):
        ssd_dirs.append(run_direction(
            functools.partial(ssd_scan, a=ssd_a[d]),
            (pc["x"], pc["dt"][:, :, d], pc["b"], pc["c"]),
            (px["x"], px["dt"][:, :, d], px["b"], px["c"]), ssd_h0, reverse))
        lru_dirs.append(run_direction(
            functools.partial(rglru_scan, w_a=lru_w_a[d], b_a=lru_b_a[d].reshape(LRU_BLOCKS, LRU_BLOCK_W),
                              w_i=lru_w_i[d], b_i=lru_b_i[d].reshape(LRU_BLOCKS, LRU_BLOCK_W),
                              lam=lru_lam[d].reshape(LRU_BLOCKS, LRU_BLOCK_W)),
            (pc["u"],), (px["u"],), lru_h0, reverse))

    def finish(p, ssd_y, lru_h):
        bsz_, t = p["z"].shape[:2]
        y = (ssd_y + ssd_d.reshape(SSD_GROUPS, SSD_HPG, 1) * p["x"]).reshape(bsz_, t, SSD_W)
        y = rms_norm(y * jax.nn.silu(p["z"]), ssd_norm_g)
        r = lru_h.reshape(bsz_, t, LRU_W) * jax.nn.gelu(p["gy"])
        return jnp.concatenate([y, r], axis=-1) @ w_out

    out_x = finish(px, ssd_dirs[0][1] + ssd_dirs[1][1], lru_dirs[0][1] + lru_dirs[1][1])
    out_c = finish(pc, ssd_dirs[0][0] + ssd_dirs[1][0], lru_dirs[0][0] + lru_dirs[1][0]) if need_ctx else None
    return out_x, out_c


def odd_mixer(hx, hc, lower_bound, w_in, w_out, hg_norm_g, s5_lam_re, s5_lam_im, s5_log_step,
              s5_b_re, s5_b_im, s5_c_re, s5_c_im, s5_d, s5_glu_w, s5_glu_b, need_ctx):
    lb = lower_bound.reshape(HG_HEADS, HG_DK)

    def prepare(h):
        bsz, t = h.shape[:2]
        q, f_fwd, f_bwd, v, g, u = jnp.split(h @ w_in, OD_SPLITS, axis=-1)
        gates = []
        for f in (f_fwd, f_bwd):
            f = f.reshape(bsz, t, HG_HEADS, HG_DK)
            log_f = jnp.log(lb + (1 - lb) * jax.nn.sigmoid(f))
            k = (1 - lb) * jax.nn.sigmoid(-f)
            gates.append((k, log_f))
        return {"q": jax.nn.silu(q.reshape(bsz, t, HG_HEADS, HG_DK)),
                "v": v.reshape(bsz, t, HG_HEADS, HG_DV), "g": g, "gates": gates,
                "u": u.reshape(bsz, t, S5_GROUPS, S5_GROUP_CH)}

    pc, px = prepare(hc), prepare(hx)
    bsz = hx.shape[0]
    hg_h0 = jnp.zeros((bsz, HG_HEADS, HG_DK, HG_DV), hx.dtype)
    s5_zero = jnp.zeros((bsz, S5_GROUPS, S5_STATE), hx.dtype)
    hg_dirs, s5_dirs = [], []
    for d, reverse in enumerate((False, True)):
        hg_dirs.append(run_direction(
            hgrn_scan, (pc["q"], pc["gates"][d][0], pc["gates"][d][1], pc["v"]),
            (px["q"], px["gates"][d][0], px["gates"][d][1], px["v"]), hg_h0, reverse))
        s5_dirs.append(run_direction(
            functools.partial(s5_scan, lam_re=s5_lam_re[d], lam_im=s5_lam_im[d], log_step=s5_log_step[d],
                              b_re=s5_b_re, b_im=s5_b_im, c_re=s5_c_re[d], c_im=s5_c_im[d]),
            (pc["u"],), (px["u"],), (s5_zero, s5_zero), reverse))

    def finish(p, o, y):
        bsz_, t = p["g"].shape[:2]
        o = rms_norm(o, hg_norm_g) * jax.nn.silu(p["g"].reshape(bsz_, t, HG_HEADS, HG_DV))
        y = y + s5_d.reshape(S5_GROUPS, S5_GROUP_CH) * p["u"]
        y = jax.nn.gelu(y.reshape(bsz_, t, S5_W))
        y = y * jax.nn.sigmoid(y @ s5_glu_w + s5_glu_b)
        return jnp.concatenate([o.reshape(bsz_, t, HG_W), y], axis=-1) @ w_out

    out_x = finish(px, hg_dirs[0][1] + hg_dirs[1][1], s5_dirs[0][1] + s5_dirs[1][1])
    out_c = finish(pc, hg_dirs[0][0] + hg_dirs[1][0], s5_dirs[0][0] + s5_dirs[1][0]) if need_ctx else None
    return out_x, out_c


def conv_ffn(h, rows, w_gate, w_up, conv_w, conv_b, w_down):
    a = h @ w_gate
    if rows is None:
        a = depthwise_conv1d(a, conv_w[1], conv_b)
    else:
        bsz, t, f = a.shape
        a = depthwise_conv2d(a.reshape(bsz, rows, GRID_W, f), conv_w, conv_b).reshape(bsz, t, f)
    return (jax.nn.silu(a) * (h @ w_up)) @ w_down


def _fwd_setup_inputs(seed: int = 0) -> dict:
    key = jax.random.key(seed)
    ks = iter(jax.random.split(key, 64))
    D = D_MODEL

    def nrm(shape, scale):
        return scale * jax.random.normal(next(ks), shape, jnp.float32)

    def uni(shape, lo, hi):
        return jax.random.uniform(next(ks), shape, jnp.float32, lo, hi)

    ssd_dt = jnp.exp(uni((N_EVEN, 2, SSD_HEADS), math.log(1e-3), math.log(1e-1)))
    lru_a = uni((N_EVEN, 2, LRU_W), 0.9, 0.999) ** (1.0 / LRU_C)
    n_idx = jnp.arange(S5_STATE, dtype=jnp.float32)
    return {
        "x": nrm((BATCH, SEQ, D), 1.0),
        "c": nrm((BATCH, D), 1.0),
        "ctx": nrm((BATCH, CTX_LEN, D), 1.0),
        "c_ctx": nrm((D,), 1.0),
        "w_mod": nrm((DEPTH, D, N_MOD * D), 0.5 * D ** -0.5),
        "b_mod": nrm((DEPTH, N_MOD * D), 0.02),
        "norm_mix_g": 1.0 + nrm((DEPTH, D), 0.02),
        "norm_ffn_g": 1.0 + nrm((DEPTH, D), 0.02),
        "final_norm_g": 1.0 + nrm((D,), 0.02),
        "ev_w_in": nrm((N_EVEN, D, EV_IN), D ** -0.5),
        "ev_w_out": nrm((N_EVEN, EV_MIX, D), EV_MIX ** -0.5),
        "ssd_conv_w": nrm((N_EVEN, SSD_CONV, SSD_XBC), SSD_CONV ** -0.5),
        "ssd_conv_b": nrm((N_EVEN, SSD_XBC), 0.02),
        "ssd_dt_bias": ssd_dt + jnp.log(-jnp.expm1(-ssd_dt)),
        "ssd_a_log": jnp.log(uni((N_EVEN, 2, SSD_HEADS), 1.0, 16.0)),
        "ssd_d": 1.0 + nrm((N_EVEN, SSD_HEADS), 0.1),
        "ssd_norm_g": 1.0 + nrm((N_EVEN, SSD_W), 0.02),
        "lru_conv_w": nrm((N_EVEN, LRU_CONV, LRU_W), LRU_CONV ** -0.5),
        "lru_conv_b": nrm((N_EVEN, LRU_W), 0.02),
        "lru_w_a": nrm((N_EVEN, 2, LRU_BLOCKS, LRU_BLOCK_W, LRU_BLOCK_W), LRU_BLOCK_W ** -0.5),
        "lru_b_a": nrm((N_EVEN, 2, LRU_W), 0.02),
        "lru_w_i": nrm((N_EVEN, 2, LRU_BLOCKS, LRU_BLOCK_W, LRU_BLOCK_W), LRU_BLOCK_W ** -0.5),
        "lru_b_i": nrm((N_EVEN, 2, LRU_W), 0.02),
        "lru_lam": jnp.log(lru_a) - jnp.log1p(-lru_a),
        "od_w_in": nrm((N_ODD, D, OD_IN), D ** -0.5),
        "od_w_out": nrm((N_ODD, OD_MIX, D), OD_MIX ** -0.5),
        "hg_lb_logits": nrm((DEPTH, HG_W), 0.1),
        "hg_norm_g": 1.0 + nrm((N_ODD, HG_HEADS, HG_DV), 0.02),
        "s5_lam_re": -0.5 + nrm((N_ODD, 2, S5_GROUPS, S5_STATE), 0.01),
        "s5_lam_im": math.pi * n_idx + nrm((N_ODD, 2, S5_GROUPS, S5_STATE), 0.01),
        "s5_log_step": uni((N_ODD, 2, S5_GROUPS), math.log(1e-3), math.log(1e-1)),
        "s5_b_re": nrm((N_ODD, S5_GROUPS, S5_STATE, S5_GROUP_CH), (2 * S5_GROUP_CH) ** -0.5),
        "s5_b_im": nrm((N_ODD, S5_GROUPS, S5_STATE, S5_GROUP_CH), (2 * S5_GROUP_CH) ** -0.5),
        "s5_c_re": nrm((N_ODD, 2, S5_GROUPS, S5_GROUP_CH, S5_STATE), 0.5),
        "s5_c_im": nrm((N_ODD, 2, S5_GROUPS, S5_GROUP_CH, S5_STATE), 0.5),
        "s5_d": nrm((N_ODD, S5_W), 1.0),
        "s5_glu_w": nrm((N_ODD, S5_W, S5_W), S5_W ** -0.5),
        "s5_glu_b": nrm((N_ODD, S5_W), 0.02),
        "ffn_w_gate": nrm((DEPTH, D, D_FF), D ** -0.5),
        "ffn_w_up": nrm((DEPTH, D, D_FF), D ** -0.5),
        "ffn_conv_w": nrm((DEPTH, FFN_CONV, FFN_CONV, D_FF), 1.0 / FFN_CONV),
        "ffn_conv_b": nrm((DEPTH, D_FF), 0.02),
        "ffn_w_down": nrm((DEPTH, D_FF, D), D_FF ** -0.5),
    }


def _fwd_reference(x, c, ctx, c_ctx, w_mod, b_mod, norm_mix_g, norm_ffn_g, final_norm_g,
              ev_w_in, ev_w_out, ssd_conv_w, ssd_conv_b, ssd_dt_bias, ssd_a_log, ssd_d, ssd_norm_g,
              lru_conv_w, lru_conv_b, lru_w_a, lru_b_a, lru_w_i, lru_b_i, lru_lam,
              od_w_in, od_w_out, hg_lb_logits, hg_norm_g,
              s5_lam_re, s5_lam_im, s5_log_step, s5_b_re, s5_b_im, s5_c_re, s5_c_im, s5_d,
              s5_glu_w, s5_glu_b,
              ffn_w_gate, ffn_w_up, ffn_conv_w, ffn_conv_b, ffn_w_down):
    rows = x.shape[1] // GRID_W
    p = jax.nn.softmax(hg_lb_logits.astype(jnp.float32), axis=0)
    lower_bounds = (jnp.cumsum(p, axis=0) - p[0]).astype(hg_lb_logits.dtype)
    s_c = jax.nn.silu(c)
    s_cc = jax.nn.silu(c_ctx)
    for layer in range(DEPTH):
        last = layer == DEPTH - 1
        j = layer // 2
        mod_x = jnp.split((s_c @ w_mod[layer] + b_mod[layer])[:, None, :], N_MOD, axis=-1)
        mod_c = jnp.split(s_cc @ w_mod[layer] + b_mod[layer], N_MOD, axis=-1)
        hx = modulate(rms_norm(x, norm_mix_g[layer]), mod_x[0], mod_x[1])
        hc = modulate(rms_norm(ctx, norm_mix_g[layer]), mod_c[0], mod_c[1])
        if layer % 2 == 0:
            ox, oc = even_mixer(hx, hc, ev_w_in[j], ev_w_out[j], ssd_conv_w[j], ssd_conv_b[j],
                                ssd_dt_bias[j], ssd_a_log[j], ssd_d[j], ssd_norm_g[j],
                                lru_conv_w[j], lru_conv_b[j], lru_w_a[j], lru_b_a[j],
                                lru_w_i[j], lru_b_i[j], lru_lam[j], not last)
        else:
            ox, oc = odd_mixer(hx, hc, lower_bounds[layer], od_w_in[j], od_w_out[j], hg_norm_g[j],
                               s5_lam_re[j], s5_lam_im[j], s5_log_step[j], s5_b_re[j], s5_b_im[j],
                               s5_c_re[j], s5_c_im[j], s5_d[j], s5_glu_w[j], s5_glu_b[j], not last)
        x = x + mod_x[2] * ox
        fx = modulate(rms_norm(x, norm_ffn_g[layer]), mod_x[3], mod_x[4])
        x = x + mod_x[5] * conv_ffn(fx, rows, ffn_w_gate[layer], ffn_w_up[layer],
                                    ffn_conv_w[layer], ffn_conv_b[layer], ffn_w_down[layer])
        if not last:
            ctx = ctx + mod_c[2] * oc
            fc = modulate(rms_norm(ctx, norm_ffn_g[layer]), mod_c[3], mod_c[4])
            ctx = ctx + mod_c[5] * conv_ffn(fc, None, ffn_w_gate[layer], ffn_w_up[layer],
                                            ffn_conv_w[layer], ffn_conv_b[layer], ffn_w_down[layer])
    return rms_norm(x, final_norm_g)


import jax as _jax
import jax.numpy as _jnp

TWIN_FORMAT = 'train_step'
FWD_PARAMS = ['x', 'c', 'ctx', 'c_ctx', 'w_mod', 'b_mod', 'norm_mix_g', 'norm_ffn_g', 'final_norm_g', 'ev_w_in', 'ev_w_out', 'ssd_conv_w', 'ssd_conv_b', 'ssd_dt_bias', 'ssd_a_log', 'ssd_d', 'ssd_norm_g', 'lru_conv_w', 'lru_conv_b', 'lru_w_a', 'lru_b_a', 'lru_w_i', 'lru_b_i', 'lru_lam', 'od_w_in', 'od_w_out', 'hg_lb_logits', 'hg_norm_g', 's5_lam_re', 's5_lam_im', 's5_log_step', 's5_b_re', 's5_b_im', 's5_c_re', 's5_c_im', 's5_d', 's5_glu_w', 's5_glu_b', 'ffn_w_gate', 'ffn_w_up', 'ffn_conv_w', 'ffn_conv_b', 'ffn_w_down']
TWIN_WEIGHTS = ['c_ctx', 'w_mod', 'b_mod', 'norm_mix_g', 'norm_ffn_g', 'final_norm_g', 'ev_w_in', 'ev_w_out', 'ssd_conv_w', 'ssd_conv_b', 'ssd_dt_bias', 'ssd_a_log', 'ssd_d', 'ssd_norm_g', 'lru_conv_w', 'lru_conv_b', 'lru_w_a', 'lru_b_a', 'lru_w_i', 'lru_b_i', 'lru_lam', 'od_w_in', 'od_w_out', 'hg_lb_logits', 'hg_norm_g', 's5_lam_re', 's5_lam_im', 's5_log_step', 's5_b_re', 's5_b_im', 's5_c_re', 's5_c_im', 's5_d', 's5_glu_w', 's5_glu_b', 'ffn_w_gate', 'ffn_w_up', 'ffn_conv_w', 'ffn_conv_b', 'ffn_w_down']
TWIN_DIFF_INPUT = 'x'
TWIN_INPUTS = ['x', 'c', 'ctx', 'c_ctx', 'w_mod', 'b_mod', 'norm_mix_g', 'norm_ffn_g', 'final_norm_g', 'ev_w_in', 'ev_w_out', 'ssd_conv_w', 'ssd_conv_b', 'ssd_dt_bias', 'ssd_a_log', 'ssd_d', 'ssd_norm_g', 'lru_conv_w', 'lru_conv_b', 'lru_w_a', 'lru_b_a', 'lru_w_i', 'lru_b_i', 'lru_lam', 'od_w_in', 'od_w_out', 'hg_lb_logits', 'hg_norm_g', 's5_lam_re', 's5_lam_im', 's5_log_step', 's5_b_re', 's5_b_im', 's5_c_re', 's5_c_im', 's5_d', 's5_glu_w', 's5_glu_b', 'ffn_w_gate', 'ffn_w_up', 'ffn_conv_w', 'ffn_conv_b', 'ffn_w_down', 'loss_target', 'm_c_ctx', 'm_w_mod', 'm_b_mod', 'm_norm_mix_g', 'm_norm_ffn_g', 'm_final_norm_g', 'm_ev_w_in', 'm_ev_w_out', 'm_ssd_conv_w', 'm_ssd_conv_b', 'm_ssd_dt_bias', 'm_ssd_a_log', 'm_ssd_d', 'm_ssd_norm_g', 'm_lru_conv_w', 'm_lru_conv_b', 'm_lru_w_a', 'm_lru_b_a', 'm_lru_w_i', 'm_lru_b_i', 'm_lru_lam', 'm_od_w_in', 'm_od_w_out', 'm_hg_lb_logits', 'm_hg_norm_g', 'm_s5_lam_re', 'm_s5_lam_im', 'm_s5_log_step', 'm_s5_b_re', 'm_s5_b_im', 'm_s5_c_re', 'm_s5_c_im', 'm_s5_d', 'm_s5_glu_w', 'm_s5_glu_b', 'm_ffn_w_gate', 'm_ffn_w_up', 'm_ffn_conv_w', 'm_ffn_conv_b', 'm_ffn_w_down', 'v_c_ctx', 'v_w_mod', 'v_b_mod', 'v_norm_mix_g', 'v_norm_ffn_g', 'v_final_norm_g', 'v_ev_w_in', 'v_ev_w_out', 'v_ssd_conv_w', 'v_ssd_conv_b', 'v_ssd_dt_bias', 'v_ssd_a_log', 'v_ssd_d', 'v_ssd_norm_g', 'v_lru_conv_w', 'v_lru_conv_b', 'v_lru_w_a', 'v_lru_b_a', 'v_lru_w_i', 'v_lru_b_i', 'v_lru_lam', 'v_od_w_in', 'v_od_w_out', 'v_hg_lb_logits', 'v_hg_norm_g', 'v_s5_lam_re', 'v_s5_lam_im', 'v_s5_log_step', 'v_s5_b_re', 'v_s5_b_im', 'v_s5_c_re', 'v_s5_c_im', 'v_s5_d', 'v_s5_glu_w', 'v_s5_glu_b', 'v_ffn_w_gate', 'v_ffn_w_up', 'v_ffn_conv_w', 'v_ffn_conv_b', 'v_ffn_w_down']
TWIN_OUTPUTS = ['loss', 'grad_x', 'grad_c_ctx', 'grad_w_mod', 'grad_b_mod', 'grad_norm_mix_g', 'grad_norm_ffn_g', 'grad_final_norm_g', 'grad_ev_w_in', 'grad_ev_w_out', 'grad_ssd_conv_w', 'grad_ssd_conv_b', 'grad_ssd_dt_bias', 'grad_ssd_a_log', 'grad_ssd_d', 'grad_ssd_norm_g', 'grad_lru_conv_w', 'grad_lru_conv_b', 'grad_lru_w_a', 'grad_lru_b_a', 'grad_lru_w_i', 'grad_lru_b_i', 'grad_lru_lam', 'grad_od_w_in', 'grad_od_w_out', 'grad_hg_lb_logits', 'grad_hg_norm_g', 'grad_s5_lam_re', 'grad_s5_lam_im', 'grad_s5_log_step', 'grad_s5_b_re', 'grad_s5_b_im', 'grad_s5_c_re', 'grad_s5_c_im', 'grad_s5_d', 'grad_s5_glu_w', 'grad_s5_glu_b', 'grad_ffn_w_gate', 'grad_ffn_w_up', 'grad_ffn_conv_w', 'grad_ffn_conv_b', 'grad_ffn_w_down', 'delta_c_ctx', 'delta_w_mod', 'delta_b_mod', 'delta_norm_mix_g', 'delta_norm_ffn_g', 'delta_final_norm_g', 'delta_ev_w_in', 'delta_ev_w_out', 'delta_ssd_conv_w', 'delta_ssd_conv_b', 'delta_ssd_dt_bias', 'delta_ssd_a_log', 'delta_ssd_d', 'delta_ssd_norm_g', 'delta_lru_conv_w', 'delta_lru_conv_b', 'delta_lru_w_a', 'delta_lru_b_a', 'delta_lru_w_i', 'delta_lru_b_i', 'delta_lru_lam', 'delta_od_w_in', 'delta_od_w_out', 'delta_hg_lb_logits', 'delta_hg_norm_g', 'delta_s5_lam_re', 'delta_s5_lam_im', 'delta_s5_log_step', 'delta_s5_b_re', 'delta_s5_b_im', 'delta_s5_c_re', 'delta_s5_c_im', 'delta_s5_d', 'delta_s5_glu_w', 'delta_s5_glu_b', 'delta_ffn_w_gate', 'delta_ffn_w_up', 'delta_ffn_conv_w', 'delta_ffn_conv_b', 'delta_ffn_w_down', 'new_m_c_ctx', 'new_m_w_mod', 'new_m_b_mod', 'new_m_norm_mix_g', 'new_m_norm_ffn_g', 'new_m_final_norm_g', 'new_m_ev_w_in', 'new_m_ev_w_out', 'new_m_ssd_conv_w', 'new_m_ssd_conv_b', 'new_m_ssd_dt_bias', 'new_m_ssd_a_log', 'new_m_ssd_d', 'new_m_ssd_norm_g', 'new_m_lru_conv_w', 'new_m_lru_conv_b', 'new_m_lru_w_a', 'new_m_lru_b_a', 'new_m_lru_w_i', 'new_m_lru_b_i', 'new_m_lru_lam', 'new_m_od_w_in', 'new_m_od_w_out', 'new_m_hg_lb_logits', 'new_m_hg_norm_g', 'new_m_s5_lam_re', 'new_m_s5_lam_im', 'new_m_s5_log_step', 'new_m_s5_b_re', 'new_m_s5_b_im', 'new_m_s5_c_re', 'new_m_s5_c_im', 'new_m_s5_d', 'new_m_s5_glu_w', 'new_m_s5_glu_b', 'new_m_ffn_w_gate', 'new_m_ffn_w_up', 'new_m_ffn_conv_w', 'new_m_ffn_conv_b', 'new_m_ffn_w_down', 'new_v_c_ctx', 'new_v_w_mod', 'new_v_b_mod', 'new_v_norm_mix_g', 'new_v_norm_ffn_g', 'new_v_final_norm_g', 'new_v_ev_w_in', 'new_v_ev_w_out', 'new_v_ssd_conv_w', 'new_v_ssd_conv_b', 'new_v_ssd_dt_bias', 'new_v_ssd_a_log', 'new_v_ssd_d', 'new_v_ssd_norm_g', 'new_v_lru_conv_w', 'new_v_lru_conv_b', 'new_v_lru_w_a', 'new_v_lru_b_a', 'new_v_lru_w_i', 'new_v_lru_b_i', 'new_v_lru_lam', 'new_v_od_w_in', 'new_v_od_w_out', 'new_v_hg_lb_logits', 'new_v_hg_norm_g', 'new_v_s5_lam_re', 'new_v_s5_lam_im', 'new_v_s5_log_step', 'new_v_s5_b_re', 'new_v_s5_b_im', 'new_v_s5_c_re', 'new_v_s5_c_im', 'new_v_s5_d', 'new_v_s5_glu_w', 'new_v_s5_glu_b', 'new_v_ffn_w_gate', 'new_v_ffn_w_up', 'new_v_ffn_conv_w', 'new_v_ffn_conv_b', 'new_v_ffn_w_down']
TWIN_LEAF_KINDS = {'loss': 'loss', 'grad_x': 'grad_x', 'grad_c_ctx': 'grad_w', 'grad_w_mod': 'grad_w', 'grad_b_mod': 'grad_w', 'grad_norm_mix_g': 'grad_w', 'grad_norm_ffn_g': 'grad_w', 'grad_final_norm_g': 'grad_w', 'grad_ev_w_in': 'grad_w', 'grad_ev_w_out': 'grad_w', 'grad_ssd_conv_w': 'grad_w', 'grad_ssd_conv_b': 'grad_w', 'grad_ssd_dt_bias': 'grad_w', 'grad_ssd_a_log': 'grad_w', 'grad_ssd_d': 'grad_w', 'grad_ssd_norm_g': 'grad_w', 'grad_lru_conv_w': 'grad_w', 'grad_lru_conv_b': 'grad_w', 'grad_lru_w_a': 'grad_w', 'grad_lru_b_a': 'grad_w', 'grad_lru_w_i': 'grad_w', 'grad_lru_b_i': 'grad_w', 'grad_lru_lam': 'grad_w', 'grad_od_w_in': 'grad_w', 'grad_od_w_out': 'grad_w', 'grad_hg_lb_logits': 'grad_w', 'grad_hg_norm_g': 'grad_w', 'grad_s5_lam_re': 'grad_w', 'grad_s5_lam_im': 'grad_w', 'grad_s5_log_step': 'grad_w', 'grad_s5_b_re': 'grad_w', 'grad_s5_b_im': 'grad_w', 'grad_s5_c_re': 'grad_w', 'grad_s5_c_im': 'grad_w', 'grad_s5_d': 'grad_w', 'grad_s5_glu_w': 'grad_w', 'grad_s5_glu_b': 'grad_w', 'grad_ffn_w_gate': 'grad_w', 'grad_ffn_w_up': 'grad_w', 'grad_ffn_conv_w': 'grad_w', 'grad_ffn_conv_b': 'grad_w', 'grad_ffn_w_down': 'grad_w', 'delta_c_ctx': 'delta_w', 'delta_w_mod': 'delta_w', 'delta_b_mod': 'delta_w', 'delta_norm_mix_g': 'delta_w', 'delta_norm_ffn_g': 'delta_w', 'delta_final_norm_g': 'delta_w', 'delta_ev_w_in': 'delta_w', 'delta_ev_w_out': 'delta_w', 'delta_ssd_conv_w': 'delta_w', 'delta_ssd_conv_b': 'delta_w', 'delta_ssd_dt_bias': 'delta_w', 'delta_ssd_a_log': 'delta_w', 'delta_ssd_d': 'delta_w', 'delta_ssd_norm_g': 'delta_w', 'delta_lru_conv_w': 'delta_w', 'delta_lru_conv_b': 'delta_w', 'delta_lru_w_a': 'delta_w', 'delta_lru_b_a': 'delta_w', 'delta_lru_w_i': 'delta_w', 'delta_lru_b_i': 'delta_w', 'delta_lru_lam': 'delta_w', 'delta_od_w_in': 'delta_w', 'delta_od_w_out': 'delta_w', 'delta_hg_lb_logits': 'delta_w', 'delta_hg_norm_g': 'delta_w', 'delta_s5_lam_re': 'delta_w', 'delta_s5_lam_im': 'delta_w', 'delta_s5_log_step': 'delta_w', 'delta_s5_b_re': 'delta_w', 'delta_s5_b_im': 'delta_w', 'delta_s5_c_re': 'delta_w', 'delta_s5_c_im': 'delta_w', 'delta_s5_d': 'delta_w', 'delta_s5_glu_w': 'delta_w', 'delta_s5_glu_b': 'delta_w', 'delta_ffn_w_gate': 'delta_w', 'delta_ffn_w_up': 'delta_w', 'delta_ffn_conv_w': 'delta_w', 'delta_ffn_conv_b': 'delta_w', 'delta_ffn_w_down': 'delta_w', 'new_m_c_ctx': 'new_m', 'new_m_w_mod': 'new_m', 'new_m_b_mod': 'new_m', 'new_m_norm_mix_g': 'new_m', 'new_m_norm_ffn_g': 'new_m', 'new_m_final_norm_g': 'new_m', 'new_m_ev_w_in': 'new_m', 'new_m_ev_w_out': 'new_m', 'new_m_ssd_conv_w': 'new_m', 'new_m_ssd_conv_b': 'new_m', 'new_m_ssd_dt_bias': 'new_m', 'new_m_ssd_a_log': 'new_m', 'new_m_ssd_d': 'new_m', 'new_m_ssd_norm_g': 'new_m', 'new_m_lru_conv_w': 'new_m', 'new_m_lru_conv_b': 'new_m', 'new_m_lru_w_a': 'new_m', 'new_m_lru_b_a': 'new_m', 'new_m_lru_w_i': 'new_m', 'new_m_lru_b_i': 'new_m', 'new_m_lru_lam': 'new_m', 'new_m_od_w_in': 'new_m', 'new_m_od_w_out': 'new_m', 'new_m_hg_lb_logits': 'new_m', 'new_m_hg_norm_g': 'new_m', 'new_m_s5_lam_re': 'new_m', 'new_m_s5_lam_im': 'new_m', 'new_m_s5_log_step': 'new_m', 'new_m_s5_b_re': 'new_m', 'new_m_s5_b_im': 'new_m', 'new_m_s5_c_re': 'new_m', 'new_m_s5_c_im': 'new_m', 'new_m_s5_d': 'new_m', 'new_m_s5_glu_w': 'new_m', 'new_m_s5_glu_b': 'new_m', 'new_m_ffn_w_gate': 'new_m', 'new_m_ffn_w_up': 'new_m', 'new_m_ffn_conv_w': 'new_m', 'new_m_ffn_conv_b': 'new_m', 'new_m_ffn_w_down': 'new_m', 'new_v_c_ctx': 'new_v', 'new_v_w_mod': 'new_v', 'new_v_b_mod': 'new_v', 'new_v_norm_mix_g': 'new_v', 'new_v_norm_ffn_g': 'new_v', 'new_v_final_norm_g': 'new_v', 'new_v_ev_w_in': 'new_v', 'new_v_ev_w_out': 'new_v', 'new_v_ssd_conv_w': 'new_v', 'new_v_ssd_conv_b': 'new_v', 'new_v_ssd_dt_bias': 'new_v', 'new_v_ssd_a_log': 'new_v', 'new_v_ssd_d': 'new_v', 'new_v_ssd_norm_g': 'new_v', 'new_v_lru_conv_w': 'new_v', 'new_v_lru_conv_b': 'new_v', 'new_v_lru_w_a': 'new_v', 'new_v_lru_b_a': 'new_v', 'new_v_lru_w_i': 'new_v', 'new_v_lru_b_i': 'new_v', 'new_v_lru_lam': 'new_v', 'new_v_od_w_in': 'new_v', 'new_v_od_w_out': 'new_v', 'new_v_hg_lb_logits': 'new_v', 'new_v_hg_norm_g': 'new_v', 'new_v_s5_lam_re': 'new_v', 'new_v_s5_lam_im': 'new_v', 'new_v_s5_log_step': 'new_v', 'new_v_s5_b_re': 'new_v', 'new_v_s5_b_im': 'new_v', 'new_v_s5_c_re': 'new_v', 'new_v_s5_c_im': 'new_v', 'new_v_s5_d': 'new_v', 'new_v_s5_glu_w': 'new_v', 'new_v_s5_glu_b': 'new_v', 'new_v_ffn_w_gate': 'new_v', 'new_v_ffn_w_up': 'new_v', 'new_v_ffn_conv_w': 'new_v', 'new_v_ffn_conv_b': 'new_v', 'new_v_ffn_w_down': 'new_v'}


def _forward(args):
    return _fwd_reference(*[args[k] for k in FWD_PARAMS])


def _output_shape():
    out = _jax.eval_shape(lambda: _forward(_fwd_setup_inputs(0)))
    return out.shape, out.dtype

N_MICROBATCH = 1
ADAM_LR = 0.001
ADAM_B1 = 0.9
ADAM_B2 = 0.999
ADAM_EPS = 1e-08
ADAM_WD = 0.01
ADAM_STEP = 10
PER_EXAMPLE_BATCH_AXIS = {'x': 0, 'c': 0, 'ctx': 0, 'loss_target': 0}
SHARED_INPUTS = []
_WEIGHT_DTYPES = {'c_ctx': _jnp.float32, 'w_mod': _jnp.float32, 'b_mod': _jnp.float32, 'norm_mix_g': _jnp.float32, 'norm_ffn_g': _jnp.float32, 'final_norm_g': _jnp.float32, 'ev_w_in': _jnp.float32, 'ev_w_out': _jnp.float32, 'ssd_conv_w': _jnp.float32, 'ssd_conv_b': _jnp.float32, 'ssd_dt_bias': _jnp.float32, 'ssd_a_log': _jnp.float32, 'ssd_d': _jnp.float32, 'ssd_norm_g': _jnp.float32, 'lru_conv_w': _jnp.float32, 'lru_conv_b': _jnp.float32, 'lru_w_a': _jnp.float32, 'lru_b_a': _jnp.float32, 'lru_w_i': _jnp.float32, 'lru_b_i': _jnp.float32, 'lru_lam': _jnp.float32, 'od_w_in': _jnp.float32, 'od_w_out': _jnp.float32, 'hg_lb_logits': _jnp.float32, 'hg_norm_g': _jnp.float32, 's5_lam_re': _jnp.float32, 's5_lam_im': _jnp.float32, 's5_log_step': _jnp.float32, 's5_b_re': _jnp.float32, 's5_b_im': _jnp.float32, 's5_c_re': _jnp.float32, 's5_c_im': _jnp.float32, 's5_d': _jnp.float32, 's5_glu_w': _jnp.float32, 's5_glu_b': _jnp.float32, 'ffn_w_gate': _jnp.float32, 'ffn_w_up': _jnp.float32, 'ffn_conv_w': _jnp.float32, 'ffn_conv_b': _jnp.float32, 'ffn_w_down': _jnp.float32}
MOMENT_SCALE = {'c_ctx': 6.142528e-02, 'w_mod': 1.734372e-01, 'b_mod': 2.863915e-01, 'norm_mix_g': 1.276585e-01, 'norm_ffn_g': 6.758139e-02, 'final_norm_g': 6.499683e+01, 'ev_w_in': 1.166191e-01, 'ev_w_out': 1.752776e-01, 'ssd_conv_w': 4.174165e-02, 'ssd_conv_b': 6.422399e-02, 'ssd_dt_bias': 1.104318e-01, 'ssd_a_log': 1.116933e-01, 'ssd_d': 1.707727e-01, 'ssd_norm_g': 5.173888e-02, 'lru_conv_w': 1.850371e-01, 'lru_conv_b': 4.836120e-01, 'lru_w_a': 9.601309e-03, 'lru_b_a': 1.412703e-02, 'lru_w_i': 1.955489e-02, 'lru_b_i': 3.559226e-02, 'lru_lam': 3.481769e-02, 'od_w_in': 2.807241e-02, 'od_w_out': 4.057109e-02, 'hg_lb_logits': 1.839687e-03, 'hg_norm_g': 4.430568e-02, 's5_lam_re': 1.274488e-02, 's5_lam_im': 1.136855e-02, 's5_log_step': 7.676380e+00, 's5_b_re': 1.244928e-02, 's5_b_im': 1.074295e-02, 's5_c_re': 2.886742e-03, 's5_c_im': 2.706299e-03, 's5_d': 2.978424e-02, 's5_glu_w': 1.210138e-02, 's5_glu_b': 1.600057e-02, 'ffn_w_gate': 3.018963e-02, 'ffn_w_up': 2.965698e-02, 'ffn_conv_w': 3.005678e-02, 'ffn_conv_b': 2.646097e-02, 'ffn_w_down': 4.932197e-02}


def _to_microbatches(a, axis):
    t = _jnp.moveaxis(a, axis, 0)
    t = t.reshape((N_MICROBATCH, t.shape[0] // N_MICROBATCH) + t.shape[1:])
    return _jnp.moveaxis(t, 1, axis + 1)


def setup_inputs(seed: int = 0) -> dict:
    inp = _fwd_setup_inputs(seed)
    key = _jax.random.fold_in(_jax.random.key(seed), 7919)
    shape, _ = _output_shape()
    out = dict(inp)
    out["loss_target"] = _jax.random.normal(_jax.random.fold_in(key, 0), shape, _jnp.float32)
    for i, name in enumerate(TWIN_WEIGHTS):
        w = inp[name].astype(_jnp.float32)
        if MOMENT_SCALE is None:
            s = _jnp.sqrt(_jnp.mean(_jnp.square(w)) + 1e-30)
        else:
            s = MOMENT_SCALE[name]
        km, kv = _jax.random.split(_jax.random.fold_in(key, i + 1))
        out[name] = w
        out["m_" + name] = s * _jax.random.normal(km, w.shape, _jnp.float32)
        out["v_" + name] = (s * s) * _jax.random.uniform(kv, w.shape, _jnp.float32, 0.5, 1.5)
    if N_MICROBATCH > 1:
        for name, axis in PER_EXAMPLE_BATCH_AXIS.items():
            out[name] = _to_microbatches(out[name], axis)
    return {'x': out['x'], 'c': out['c'], 'ctx': out['ctx'], 'c_ctx': out['c_ctx'], 'w_mod': out['w_mod'], 'b_mod': out['b_mod'], 'norm_mix_g': out['norm_mix_g'], 'norm_ffn_g': out['norm_ffn_g'], 'final_norm_g': out['final_norm_g'], 'ev_w_in': out['ev_w_in'], 'ev_w_out': out['ev_w_out'], 'ssd_conv_w': out['ssd_conv_w'], 'ssd_conv_b': out['ssd_conv_b'], 'ssd_dt_bias': out['ssd_dt_bias'], 'ssd_a_log': out['ssd_a_log'], 'ssd_d': out['ssd_d'], 'ssd_norm_g': out['ssd_norm_g'], 'lru_conv_w': out['lru_conv_w'], 'lru_conv_b': out['lru_conv_b'], 'lru_w_a': out['lru_w_a'], 'lru_b_a': out['lru_b_a'], 'lru_w_i': out['lru_w_i'], 'lru_b_i': out['lru_b_i'], 'lru_lam': out['lru_lam'], 'od_w_in': out['od_w_in'], 'od_w_out': out['od_w_out'], 'hg_lb_logits': out['hg_lb_logits'], 'hg_norm_g': out['hg_norm_g'], 's5_lam_re': out['s5_lam_re'], 's5_lam_im': out['s5_lam_im'], 's5_log_step': out['s5_log_step'], 's5_b_re': out['s5_b_re'], 's5_b_im': out['s5_b_im'], 's5_c_re': out['s5_c_re'], 's5_c_im': out['s5_c_im'], 's5_d': out['s5_d'], 's5_glu_w': out['s5_glu_w'], 's5_glu_b': out['s5_glu_b'], 'ffn_w_gate': out['ffn_w_gate'], 'ffn_w_up': out['ffn_w_up'], 'ffn_conv_w': out['ffn_conv_w'], 'ffn_conv_b': out['ffn_conv_b'], 'ffn_w_down': out['ffn_w_down'], 'loss_target': out['loss_target'], 'm_c_ctx': out['m_c_ctx'], 'm_w_mod': out['m_w_mod'], 'm_b_mod': out['m_b_mod'], 'm_norm_mix_g': out['m_norm_mix_g'], 'm_norm_ffn_g': out['m_norm_ffn_g'], 'm_final_norm_g': out['m_final_norm_g'], 'm_ev_w_in': out['m_ev_w_in'], 'm_ev_w_out': out['m_ev_w_out'], 'm_ssd_conv_w': out['m_ssd_conv_w'], 'm_ssd_conv_b': out['m_ssd_conv_b'], 'm_ssd_dt_bias': out['m_ssd_dt_bias'], 'm_ssd_a_log': out['m_ssd_a_log'], 'm_ssd_d': out['m_ssd_d'], 'm_ssd_norm_g': out['m_ssd_norm_g'], 'm_lru_conv_w': out['m_lru_conv_w'], 'm_lru_conv_b': out['m_lru_conv_b'], 'm_lru_w_a': out['m_lru_w_a'], 'm_lru_b_a': out['m_lru_b_a'], 'm_lru_w_i': out['m_lru_w_i'], 'm_lru_b_i': out['m_lru_b_i'], 'm_lru_lam': out['m_lru_lam'], 'm_od_w_in': out['m_od_w_in'], 'm_od_w_out': out['m_od_w_out'], 'm_hg_lb_logits': out['m_hg_lb_logits'], 'm_hg_norm_g': out['m_hg_norm_g'], 'm_s5_lam_re': out['m_s5_lam_re'], 'm_s5_lam_im': out['m_s5_lam_im'], 'm_s5_log_step': out['m_s5_log_step'], 'm_s5_b_re': out['m_s5_b_re'], 'm_s5_b_im': out['m_s5_b_im'], 'm_s5_c_re': out['m_s5_c_re'], 'm_s5_c_im': out['m_s5_c_im'], 'm_s5_d': out['m_s5_d'], 'm_s5_glu_w': out['m_s5_glu_w'], 'm_s5_glu_b': out['m_s5_glu_b'], 'm_ffn_w_gate': out['m_ffn_w_gate'], 'm_ffn_w_up': out['m_ffn_w_up'], 'm_ffn_conv_w': out['m_ffn_conv_w'], 'm_ffn_conv_b': out['m_ffn_conv_b'], 'm_ffn_w_down': out['m_ffn_w_down'], 'v_c_ctx': out['v_c_ctx'], 'v_w_mod': out['v_w_mod'], 'v_b_mod': out['v_b_mod'], 'v_norm_mix_g': out['v_norm_mix_g'], 'v_norm_ffn_g': out['v_norm_ffn_g'], 'v_final_norm_g': out['v_final_norm_g'], 'v_ev_w_in': out['v_ev_w_in'], 'v_ev_w_out': out['v_ev_w_out'], 'v_ssd_conv_w': out['v_ssd_conv_w'], 'v_ssd_conv_b': out['v_ssd_conv_b'], 'v_ssd_dt_bias': out['v_ssd_dt_bias'], 'v_ssd_a_log': out['v_ssd_a_log'], 'v_ssd_d': out['v_ssd_d'], 'v_ssd_norm_g': out['v_ssd_norm_g'], 'v_lru_conv_w': out['v_lru_conv_w'], 'v_lru_conv_b': out['v_lru_conv_b'], 'v_lru_w_a': out['v_lru_w_a'], 'v_lru_b_a': out['v_lru_b_a'], 'v_lru_w_i': out['v_lru_w_i'], 'v_lru_b_i': out['v_lru_b_i'], 'v_lru_lam': out['v_lru_lam'], 'v_od_w_in': out['v_od_w_in'], 'v_od_w_out': out['v_od_w_out'], 'v_hg_lb_logits': out['v_hg_lb_logits'], 'v_hg_norm_g': out['v_hg_norm_g'], 'v_s5_lam_re': out['v_s5_lam_re'], 'v_s5_lam_im': out['v_s5_lam_im'], 'v_s5_log_step': out['v_s5_log_step'], 'v_s5_b_re': out['v_s5_b_re'], 'v_s5_b_im': out['v_s5_b_im'], 'v_s5_c_re': out['v_s5_c_re'], 'v_s5_c_im': out['v_s5_c_im'], 'v_s5_d': out['v_s5_d'], 'v_s5_glu_w': out['v_s5_glu_w'], 'v_s5_glu_b': out['v_s5_glu_b'], 'v_ffn_w_gate': out['v_ffn_w_gate'], 'v_ffn_w_up': out['v_ffn_w_up'], 'v_ffn_conv_w': out['v_ffn_conv_w'], 'v_ffn_conv_b': out['v_ffn_conv_b'], 'v_ffn_w_down': out['v_ffn_w_down']}


def _loss(weights, diff, rest, loss_target):
    with _jax.named_scope("forward"):
        args = {**rest, TWIN_DIFF_INPUT: diff, **{k: w.astype(_WEIGHT_DTYPES[k]) for k, w in weights.items()}}
        y = _forward(args)
    with _jax.named_scope("loss_head"):
        err = _jnp.square(y.astype(_jnp.float32) - loss_target)
        return 0.5 * _jnp.sum(_jnp.mean(err, axis=-1)) if err.ndim else 0.5 * err


def _adamw(w, g, m, v):
    m = ADAM_B1 * m + (1.0 - ADAM_B1) * g
    v = ADAM_B2 * v + (1.0 - ADAM_B2) * _jnp.square(g)
    m_hat = m / (1.0 - ADAM_B1 ** ADAM_STEP)
    v_hat = v / (1.0 - ADAM_B2 ** ADAM_STEP)
    delta = -ADAM_LR * (m_hat / (_jnp.sqrt(v_hat) + ADAM_EPS) + ADAM_WD * w)
    return delta, m, v


def reference(x, c, ctx, c_ctx, w_mod, b_mod, norm_mix_g, norm_ffn_g, final_norm_g, ev_w_in, ev_w_out, ssd_conv_w, ssd_conv_b, ssd_dt_bias, ssd_a_log, ssd_d, ssd_norm_g, lru_conv_w, lru_conv_b, lru_w_a, lru_b_a, lru_w_i, lru_b_i, lru_lam, od_w_in, od_w_out, hg_lb_logits, hg_norm_g, s5_lam_re, s5_lam_im, s5_log_step, s5_b_re, s5_b_im, s5_c_re, s5_c_im, s5_d, s5_glu_w, s5_glu_b, ffn_w_gate, ffn_w_up, ffn_conv_w, ffn_conv_b, ffn_w_down, loss_target, m_c_ctx, m_w_mod, m_b_mod, m_norm_mix_g, m_norm_ffn_g, m_final_norm_g, m_ev_w_in, m_ev_w_out, m_ssd_conv_w, m_ssd_conv_b, m_ssd_dt_bias, m_ssd_a_log, m_ssd_d, m_ssd_norm_g, m_lru_conv_w, m_lru_conv_b, m_lru_w_a, m_lru_b_a, m_lru_w_i, m_lru_b_i, m_lru_lam, m_od_w_in, m_od_w_out, m_hg_lb_logits, m_hg_norm_g, m_s5_lam_re, m_s5_lam_im, m_s5_log_step, m_s5_b_re, m_s5_b_im, m_s5_c_re, m_s5_c_im, m_s5_d, m_s5_glu_w, m_s5_glu_b, m_ffn_w_gate, m_ffn_w_up, m_ffn_conv_w, m_ffn_conv_b, m_ffn_w_down, v_c_ctx, v_w_mod, v_b_mod, v_norm_mix_g, v_norm_ffn_g, v_final_norm_g, v_ev_w_in, v_ev_w_out, v_ssd_conv_w, v_ssd_conv_b, v_ssd_dt_bias, v_ssd_a_log, v_ssd_d, v_ssd_norm_g, v_lru_conv_w, v_lru_conv_b, v_lru_w_a, v_lru_b_a, v_lru_w_i, v_lru_b_i, v_lru_lam, v_od_w_in, v_od_w_out, v_hg_lb_logits, v_hg_norm_g, v_s5_lam_re, v_s5_lam_im, v_s5_log_step, v_s5_b_re, v_s5_b_im, v_s5_c_re, v_s5_c_im, v_s5_d, v_s5_glu_w, v_s5_glu_b, v_ffn_w_gate, v_ffn_w_up, v_ffn_conv_w, v_ffn_conv_b, v_ffn_w_down):
    given = dict(x=x, c=c, ctx=ctx, c_ctx=c_ctx, w_mod=w_mod, b_mod=b_mod, norm_mix_g=norm_mix_g, norm_ffn_g=norm_ffn_g, final_norm_g=final_norm_g, ev_w_in=ev_w_in, ev_w_out=ev_w_out, ssd_conv_w=ssd_conv_w, ssd_conv_b=ssd_conv_b, ssd_dt_bias=ssd_dt_bias, ssd_a_log=ssd_a_log, ssd_d=ssd_d, ssd_norm_g=ssd_norm_g, lru_conv_w=lru_conv_w, lru_conv_b=lru_conv_b, lru_w_a=lru_w_a, lru_b_a=lru_b_a, lru_w_i=lru_w_i, lru_b_i=lru_b_i, lru_lam=lru_lam, od_w_in=od_w_in, od_w_out=od_w_out, hg_lb_logits=hg_lb_logits, hg_norm_g=hg_norm_g, s5_lam_re=s5_lam_re, s5_lam_im=s5_lam_im, s5_log_step=s5_log_step, s5_b_re=s5_b_re, s5_b_im=s5_b_im, s5_c_re=s5_c_re, s5_c_im=s5_c_im, s5_d=s5_d, s5_glu_w=s5_glu_w, s5_glu_b=s5_glu_b, ffn_w_gate=ffn_w_gate, ffn_w_up=ffn_w_up, ffn_conv_w=ffn_conv_w, ffn_conv_b=ffn_conv_b, ffn_w_down=ffn_w_down, loss_target=loss_target, m_c_ctx=m_c_ctx, m_w_mod=m_w_mod, m_b_mod=m_b_mod, m_norm_mix_g=m_norm_mix_g, m_norm_ffn_g=m_norm_ffn_g, m_final_norm_g=m_final_norm_g, m_ev_w_in=m_ev_w_in, m_ev_w_out=m_ev_w_out, m_ssd_conv_w=m_ssd_conv_w, m_ssd_conv_b=m_ssd_conv_b, m_ssd_dt_bias=m_ssd_dt_bias, m_ssd_a_log=m_ssd_a_log, m_ssd_d=m_ssd_d, m_ssd_norm_g=m_ssd_norm_g, m_lru_conv_w=m_lru_conv_w, m_lru_conv_b=m_lru_conv_b, m_lru_w_a=m_lru_w_a, m_lru_b_a=m_lru_b_a, m_lru_w_i=m_lru_w_i, m_lru_b_i=m_lru_b_i, m_lru_lam=m_lru_lam, m_od_w_in=m_od_w_in, m_od_w_out=m_od_w_out, m_hg_lb_logits=m_hg_lb_logits, m_hg_norm_g=m_hg_norm_g, m_s5_lam_re=m_s5_lam_re, m_s5_lam_im=m_s5_lam_im, m_s5_log_step=m_s5_log_step, m_s5_b_re=m_s5_b_re, m_s5_b_im=m_s5_b_im, m_s5_c_re=m_s5_c_re, m_s5_c_im=m_s5_c_im, m_s5_d=m_s5_d, m_s5_glu_w=m_s5_glu_w, m_s5_glu_b=m_s5_glu_b, m_ffn_w_gate=m_ffn_w_gate, m_ffn_w_up=m_ffn_w_up, m_ffn_conv_w=m_ffn_conv_w, m_ffn_conv_b=m_ffn_conv_b, m_ffn_w_down=m_ffn_w_down, v_c_ctx=v_c_ctx, v_w_mod=v_w_mod, v_b_mod=v_b_mod, v_norm_mix_g=v_norm_mix_g, v_norm_ffn_g=v_norm_ffn_g, v_final_norm_g=v_final_norm_g, v_ev_w_in=v_ev_w_in, v_ev_w_out=v_ev_w_out, v_ssd_conv_w=v_ssd_conv_w, v_ssd_conv_b=v_ssd_conv_b, v_ssd_dt_bias=v_ssd_dt_bias, v_ssd_a_log=v_ssd_a_log, v_ssd_d=v_ssd_d, v_ssd_norm_g=v_ssd_norm_g, v_lru_conv_w=v_lru_conv_w, v_lru_conv_b=v_lru_conv_b, v_lru_w_a=v_lru_w_a, v_lru_b_a=v_lru_b_a, v_lru_w_i=v_lru_w_i, v_lru_b_i=v_lru_b_i, v_lru_lam=v_lru_lam, v_od_w_in=v_od_w_in, v_od_w_out=v_od_w_out, v_hg_lb_logits=v_hg_lb_logits, v_hg_norm_g=v_hg_norm_g, v_s5_lam_re=v_s5_lam_re, v_s5_lam_im=v_s5_lam_im, v_s5_log_step=v_s5_log_step, v_s5_b_re=v_s5_b_re, v_s5_b_im=v_s5_b_im, v_s5_c_re=v_s5_c_re, v_s5_c_im=v_s5_c_im, v_s5_d=v_s5_d, v_s5_glu_w=v_s5_glu_w, v_s5_glu_b=v_s5_glu_b, v_ffn_w_gate=v_ffn_w_gate, v_ffn_w_up=v_ffn_w_up, v_ffn_conv_w=v_ffn_conv_w, v_ffn_conv_b=v_ffn_conv_b, v_ffn_w_down=v_ffn_w_down)
    weights = {n: given[n] for n in TWIN_WEIGHTS}
    shared = {n: given[n] for n in SHARED_INPUTS}
    per_example = {n: given[n] for n in ['x', 'c', 'ctx']}
    grad_fn = _jax.value_and_grad(_loss, argnums=(0, 1))

    def one_microbatch(ex, loss_target):
        ex = dict(ex)
        diff = ex.pop(TWIN_DIFF_INPUT)
        return grad_fn(weights, diff, {**shared, **ex}, loss_target)

    if N_MICROBATCH == 1:
        loss, (grad_w, grad_x) = one_microbatch(per_example, given["loss_target"])
    else:
        def body(carry, xs):
            loss_sum, grad_sum = carry
            l_k, (gw_k, gx_k) = one_microbatch(xs[0], xs[1])
            with _jax.named_scope("update"):
                return (loss_sum + l_k, _jax.tree.map(_jnp.add, grad_sum, gw_k)), gx_k

        init = (_jnp.zeros((), _jnp.float32), _jax.tree.map(_jnp.zeros_like, weights))
        (loss, grad_w), grad_x = _jax.lax.scan(body, init, (per_example, given["loss_target"]))
    with _jax.named_scope("update"):
        delta_w, new_m, new_v = {}, {}, {}
        for n in TWIN_WEIGHTS:
            delta_w[n], new_m[n], new_v[n] = _adamw(weights[n], grad_w[n], given["m_" + n], given["v_" + n])
    return (loss, grad_x, *[grad_w[n] for n in TWIN_WEIGHTS], *[delta_w[n] for n in TWIN_WEIGHTS],
            *[new_m[n] for n in TWIN_WEIGHTS], *[new_v[n] for n in TWIN_WEIGHTS])
```

```python
import functools
import math

import jax
import jax.numpy as jnp
from jax import lax
from jax.experimental import pallas as pl
from jax.experimental.pallas import tpu as pltpu

F32 = jnp.float32
BF16 = jnp.bfloat16

D_MODEL = 1024
DEPTH = 4
CTX_LEN = 256
SEQ = 2048
S_TOT = CTX_LEN + SEQ
GRID_W = 64
N_MOD = 6
RMS_EPS = 1e-6
N_DEV = 8

SSD_HEADS = 16
SSD_HEAD_DIM = 64
SSD_GROUPS = 2
SSD_HPG = 8
SSD_STATE = 128
SSD_CHUNK = 128
LRU_BLOCKS = 8
LRU_BLOCK_W = 128
LRU_C = 8.0
HG_W = 768
HG_HEADS = 6
HG_DK = 128
HG_CHUNK = 32
S5_W = 256
S5_GROUPS = 16
S5_GROUP_CH = 16
S5_STATE = 64
D_FF = 2816

ADAM_LR = 0.001
ADAM_B1 = 0.9
ADAM_B2 = 0.999
ADAM_EPS = 1e-08
ADAM_WD = 0.01
ADAM_STEP = 10

TOK_BLOCK = CTX_LEN
SUBLANES = 8
VMEM_LIMIT_BYTES = 56 * 1024 * 1024
MM_A_BLOCK_BYTES = 8 * 1024 * 1024


def _cparams(sem=None):
    kw = dict(vmem_limit_bytes=VMEM_LIMIT_BYTES)
    if sem is not None:
        kw["dimension_semantics"] = sem
    return pltpu.CompilerParams(**kw)


def _pick(n, cands):
    for c in cands:
        if n % c == 0:
            return c
    return n


def mm(pairs, name, out_dtype=F32):
    m = pairs[0][0].shape[0]
    n = pairs[0][1].shape[1]
    kdims = [a.shape[1] for a, _ in pairs]
    tn = _pick(n, (512, 384, 256, 128))
    ktile = 1024 if (len(pairs) == 1 and kdims[0] > 4096 and kdims[0] % 1024 == 0) else None
    nk = kdims[0] // ktile if ktile else 1
    row_bytes = sum((ktile or a.shape[1]) * a.dtype.itemsize for a, _ in pairs)
    tm = _pick(m, tuple(c for c in (1024, 512, 256, 128, 64, 48, 40, 32, 16, 8) if c * row_bytes <= MM_A_BLOCK_BYTES))
    npairs = len(pairs)
    if nk > 1:
        assert out_dtype == F32

    def body(*refs):
        o_ref = refs[2 * npairs]
        acc = None
        for i in range(npairs):
            a = refs[2 * i][...].astype(BF16)
            w = refs[2 * i + 1][...].astype(BF16)
            p = jnp.dot(a, w, preferred_element_type=F32)
            acc = p if acc is None else acc + p
        if nk == 1:
            o_ref[...] = acc.astype(out_dtype)
        else:
            k = pl.program_id(2)

            @pl.when(k == 0)
            def _():
                o_ref[...] = acc

            @pl.when(k > 0)
            def _():
                o_ref[...] += acc

    in_specs = []
    args = []
    for a, w in pairs:
        kk = a.shape[1]
        assert w.shape == (kk, n) and a.shape[0] == m, (a.shape, w.shape)
        tk = ktile if ktile else kk
        in_specs.append(pl.BlockSpec((tm, tk), lambda i, j, k: (i, k)))
        in_specs.append(pl.BlockSpec((tk, tn), lambda i, j, k: (k, j)))
        args += [a, w]
    return pl.pallas_call(
        body,
        name=name,
        grid=(m // tm, n // tn, nk),
        in_specs=in_specs,
        out_specs=pl.BlockSpec((tm, tn), lambda i, j, k: (i, j)),
        out_shape=jax.ShapeDtypeStruct((m, n), out_dtype),
        compiler_params=_cparams(("parallel", "parallel", "arbitrary")),
    )(*args)


def _mod_index(b, t):
    return (2 * b + jnp.minimum(t, 1), 0, 0)


def tok_fwd(name, f, toks, mod, params, out_widths, out_dtypes):
    nb, s, _ = toks[0].shape
    nt, nm, npar = len(toks), int(mod is not None), len(params)

    def body(*refs):
        ins, outs = refs[: nt + nm + npar], refs[nt + nm + npar:]
        tv = [r[...].astype(F32) for r in ins[:nt]]
        mv = [ins[nt][k:k + 1, :] for k in range(N_MOD)] if nm else None
        pv = [r[...] for r in ins[nt + nm:]]
        for o, r in zip(outs, f(tv, mv, pv)):
            o[...] = r.astype(o.dtype)

    in_specs = [pl.BlockSpec((None, TOK_BLOCK, t.shape[2]), lambda b, t: (b, t, 0)) for t in toks]
    if nm:
        in_specs.append(pl.BlockSpec((None, N_MOD, mod.shape[2]), _mod_index))
    in_specs += [pl.BlockSpec(p.shape, lambda b, t, nd=p.ndim: (0,) * nd) for p in params]
    return pl.pallas_call(
        body,
        name=name,
        grid=(nb, s // TOK_BLOCK),
        in_specs=in_specs,
        out_specs=[pl.BlockSpec((None, TOK_BLOCK, w), lambda b, t: (b, t, 0)) for w in out_widths],
        out_shape=[jax.ShapeDtypeStruct((nb, s, w), dt) for w, dt in zip(out_widths, out_dtypes)],
        compiler_params=_cparams(("parallel", "parallel")),
    )(*toks, *([mod] if nm else []), *params)


def tok_bwd(name, f, toks, mod, params, cots, dtok_dtypes):
    nb, s, _ = toks[0].shape
    nt, nm, npar, nc = len(toks), int(mod is not None), len(params), len(cots)

    def body(*refs):
        n_in = nt + nm + npar + nc
        ins, outs = refs[:n_in], refs[n_in:]
        b, t = pl.program_id(0), pl.program_id(1)
        tv = [r[...].astype(F32) for r in ins[:nt]]
        mv = [ins[nt][k:k + 1, :] for k in range(N_MOD)] if nm else None
        pv = [r[...] for r in ins[nt + nm: nt + nm + npar]]
        cv = [r[...].astype(F32) for r in ins[nt + nm + npar:]]
        _, vjp = jax.vjp(f, tv, mv, pv)
        dtv, dmv, dpv = vjp(cv)
        for o, r in zip(outs[:nt], dtv):
            o[...] = r.astype(o.dtype)
        if nm:
            dm_ref = outs[nt]

            @pl.when(t <= 1)
            def _():
                for k in range(N_MOD):
                    dm_ref[k:k + 1, :] = dmv[k]

            @pl.when(t > 1)
            def _():
                for k in range(N_MOD):
                    dm_ref[k:k + 1, :] += dmv[k]

        first = jnp.logical_and(b == 0, t == 0)
        for o, r in zip(outs[nt + nm:], dpv):
            @pl.when(first)
            def _(o=o, r=r):
                o[...] = r

            @pl.when(jnp.logical_not(first))
            def _(o=o, r=r):
                o[...] += r

    tok_spec = lambda w: pl.BlockSpec((None, TOK_BLOCK, w), lambda b, t: (b, t, 0))
    in_specs = [tok_spec(t.shape[2]) for t in toks]
    if nm:
        in_specs.append(pl.BlockSpec((None, N_MOD, mod.shape[2]), _mod_index))
    in_specs += [pl.BlockSpec(p.shape, lambda b, t, nd=p.ndim: (0,) * nd) for p in params]
    in_specs += [tok_spec(c.shape[2]) for c in cots]
    out_specs = [tok_spec(t.shape[2]) for t in toks]
    out_shape = [jax.ShapeDtypeStruct(t.shape, dt) for t, dt in zip(toks, dtok_dtypes)]
    if nm:
        out_specs.append(pl.BlockSpec((None, N_MOD, mod.shape[2]), _mod_index))
        out_shape.append(jax.ShapeDtypeStruct(mod.shape, F32))
    out_specs += [pl.BlockSpec(p.shape, lambda b, t, nd=p.ndim: (0,) * nd) for p in params]
    out_shape += [jax.ShapeDtypeStruct(p.shape, F32) for p in params]
    res = pl.pallas_call(
        body,
        name=name,
        grid=(nb, s // TOK_BLOCK),
        in_specs=in_specs,
        out_specs=out_specs,
        out_shape=out_shape,
        compiler_params=_cparams(("arbitrary", "arbitrary")),
    )(*toks, *([mod] if nm else []), *params, *cots)
    return res[:nt], (res[nt] if nm else None), res[nt + nm:]


def _rms(x, g):
    return x * lax.rsqrt(jnp.mean(x * x, axis=-1, keepdims=True) + RMS_EPS) * g


def _silu(x):
    return x * jax.nn.sigmoid(x)


def f_norm(shift_row, scale_row):
    def f(tv, mv, pv):
        return [_rms(tv[0], pv[0]) * (1.0 + mv[scale_row]) + mv[shift_row]]
    return f


def f_resnorm(gate_row, shift_row, scale_row):
    def f(tv, mv, pv):
        s = tv[0] + mv[gate_row] * tv[1]
        return [s, _rms(s, pv[0]) * (1.0 + mv[scale_row]) + mv[shift_row]]
    return f


_ANY = pl.BlockSpec(memory_space=pl.ANY)
_MESH = pl.DeviceIdType.MESH


def _my_pos():
    return lax.axis_index("x"), lax.axis_index("y"), lax.axis_index("c")


def _slot_of(pos):
    return 4 * pos[0] + 2 * pos[1] + pos[2]


def all_gather(x, name):
    r, c = x.shape

    def body(x_ref, out_ref, send_sems, recv_sems, local_sem):
        px, py, pc = _my_pos()
        me, sibling = (px, py, pc), (px, py, 1 - pc)
        chips = [(1 - px, py), (px, 1 - py), (1 - px, 1 - py)]

        def copy(k, block, to, src=None):
            slot = out_ref.at[_slot_of(block)]
            return pltpu.make_async_remote_copy(
                src_ref=slot if src is None else src, dst_ref=slot,
                send_sem=send_sems.at[k], recv_sem=recv_sems.at[k],
                device_id=to, device_id_type=_MESH)

        mine = pltpu.make_async_copy(x_ref, out_ref.at[_slot_of(me)], local_sem)
        mine.start()
        first = [copy(0, me, sibling, src=x_ref)]
        first += [copy(1 + j, me, (*chip, pc), src=x_ref) for j, chip in enumerate(chips)]
        for cp in first:
            cp.start()
        passed = [copy(4 + j, (*chip, pc), sibling) for j, chip in enumerate(chips)]
        for j, chip in enumerate(chips):
            copy(1 + j, (*chip, pc), me).wait_recv()
            passed[j].start()
        copy(0, sibling, me).wait_recv()
        for j, chip in enumerate(chips):
            copy(4 + j, (*chip, 1 - pc), me).wait_recv()
        for cp in first + passed:
            cp.wait_send()
        mine.wait()

    return pl.pallas_call(
        body,
        name=name,
        out_shape=jax.ShapeDtypeStruct((N_DEV, r, c), x.dtype),
        in_specs=[_ANY],
        out_specs=_ANY,
        scratch_shapes=[pltpu.SemaphoreType.DMA((7,)), pltpu.SemaphoreType.DMA((7,)), pltpu.SemaphoreType.DMA],
    )(x)


def all_to_all(x, name):
    _, r, c = x.shape

    def body(x_ref, out_ref, send_sems, recv_sems, local_sem):
        px, py, pc = _my_pos()
        me = (px, py, pc)

        def flipped(k):
            kx, ky, kc = (k >> 2) & 1, (k >> 1) & 1, k & 1
            return (1 - px if kx else px, 1 - py if ky else py, 1 - pc if kc else pc)

        def copy(k):
            peer = flipped(k)
            return pltpu.make_async_remote_copy(
                src_ref=x_ref.at[_slot_of(peer)], dst_ref=out_ref.at[_slot_of(me)],
                send_sem=send_sems.at[k - 1], recv_sem=recv_sems.at[k - 1],
                device_id=peer, device_id_type=_MESH)

        mine = pltpu.make_async_copy(x_ref.at[_slot_of(me)], out_ref.at[_slot_of(me)], local_sem)
        mine.start()
        copies = [copy(k) for k in range(1, N_DEV)]
        for cp in copies:
            cp.start()
        for k in range(1, N_DEV):
            peer = flipped(k)
            pltpu.make_async_remote_copy(
                src_ref=x_ref.at[_slot_of(me)], dst_ref=out_ref.at[_slot_of(peer)],
                send_sem=send_sems.at[k - 1], recv_sem=recv_sems.at[k - 1],
                device_id=peer, device_id_type=_MESH).wait_recv()
        for cp in copies:
            cp.wait_send()
        mine.wait()

    return pl.pallas_call(
        body,
        name=name,
        out_shape=jax.ShapeDtypeStruct(x.shape, x.dtype),
        in_specs=[_ANY],
        out_specs=_ANY,
        scratch_shapes=[pltpu.SemaphoreType.DMA((7,)), pltpu.SemaphoreType.DMA((7,)), pltpu.SemaphoreType.DMA],
    )(x)


def sum_slots(x, name):
    n, r, c = x.shape
    tr = _pick(r, (512, 256, 128, 64, 32, 16, 8))

    def body(x_ref, o_ref):
        acc = x_ref[0].astype(F32)
        for i in range(1, n):
            acc = acc + x_ref[i].astype(F32)
        o_ref[...] = acc

    return pl.pallas_call(
        body,
        name=name,
        grid=(r // tr,),
        in_specs=[pl.BlockSpec((n, tr, c), lambda i: (0, i, 0))],
        out_specs=pl.BlockSpec((tr, c), lambda i: (i, 0)),
        out_shape=jax.ShapeDtypeStruct((r, c), F32),
        compiler_params=_cparams(("parallel",)),
    )(x)


CONV_CH_TILE = 256


def _shift_rows(x, off):
    n = x.shape[0]
    if off % n == 0:
        return x
    return pltpu.roll(x, (-off) % n, axis=0)


def _between(v, lo, hi):
    return jnp.where(v >= lo, 1.0, 0.0) * jnp.where(v < hi, 1.0, 0.0)


def taps_1d(ntaps, s_ctx, s_tot):
    def mask(off):
        def m(t):
            is_ctx = _between(t, 0, s_ctx)
            return is_ctx * _between(t + off, 0, s_ctx) + (1.0 - is_ctx) * _between(t + off, s_ctx, s_tot)
        return m
    return [(j - (ntaps - 1) // 2, mask(j - (ntaps - 1) // 2)) for j in range(ntaps)]


def taps_grid(s_ctx, s_tot, grid_w):
    assert s_ctx % grid_w == 0

    def mask(dr, dc):
        def m(t):
            is_ctx = _between(t, 0, s_ctx)
            lat = _between(t % grid_w + dc, 0, grid_w) * _between(t + grid_w * dr, s_ctx, s_tot)
            ctx = _between(t + dc, 0, s_ctx) if dr == 0 else 0.0
            return is_ctx * ctx + (1.0 - is_ctx) * lat
        return m
    return [(grid_w * dr + dc, mask(dr, dc)) for dr in (-1, 0, 1) for dc in (-1, 0, 1)]


def _conv_acc(x, w_ref, b_ref, taps, masks):
    acc = jnp.broadcast_to(b_ref[...], x.shape)
    for k, (off, _) in enumerate(taps):
        acc = acc + w_ref[k:k + 1, :] * (_shift_rows(x, off) * masks[k])
    return acc


def _tap_masks(taps, s):
    t = lax.broadcasted_iota(jnp.int32, (s, 1), 0)
    return [m(t) for _, m in taps]


def conv_fwd(name, x, w, b, taps, mode, mul=None, out_dtype=F32):
    nb, s, c = x.shape
    ct = _pick(c, (CONV_CH_TILE, 128))
    has_mul = mode == "silu_mul"

    def body(*refs):
        x_ref, w_ref, b_ref = refs[:3]
        o_ref = refs[-1]
        acc = _conv_acc(x_ref[...], w_ref, b_ref, taps, _tap_masks(taps, s))
        if mode == "none":
            out = acc
        else:
            out = _silu(acc)
            if has_mul:
                out = out * refs[3][...].astype(F32)
        o_ref[...] = out.astype(o_ref.dtype)

    blk = pl.BlockSpec((None, s, ct), lambda bb, j: (bb, 0, j))
    par = lambda k: pl.BlockSpec((k, ct), lambda bb, j: (0, j))
    return pl.pallas_call(
        body,
        name=name,
        grid=(nb, c // ct),
        in_specs=[blk, par(w.shape[0]), par(1)] + ([blk] if has_mul else []),
        out_specs=blk,
        out_shape=jax.ShapeDtypeStruct(x.shape, out_dtype),
        compiler_params=_cparams(("parallel", "parallel")),
    )(x, w, b, *([mul] if has_mul else []))


def conv_bwd(name, x, w, b, dout, taps, mode, mul=None):
    nb, s, c = x.shape
    ct = _pick(c, (CONV_CH_TILE, 128))
    has_mul = mode == "silu_mul"
    nk = w.shape[0]

    def body(*refs):
        x_ref, w_ref, b_ref, do_ref = refs[:4]
        n_in = 5 if has_mul else 4
        dx_ref, dw_ref, db_ref = refs[n_in:n_in + 3]
        bb = pl.program_id(1)
        masks = _tap_masks(taps, s)
        x = x_ref[...]
        dacc = do_ref[...].astype(F32)
        if mode != "none":
            acc = _conv_acc(x, w_ref, b_ref, taps, masks)
            sg = jax.nn.sigmoid(acc)
            if has_mul:
                refs[n_in + 3][...] = (dacc * (acc * sg)).astype(refs[n_in + 3].dtype)
                dacc = dacc * refs[4][...].astype(F32)
            dacc = dacc * (sg * (1.0 + acc * (1.0 - sg)))
        dx = jnp.zeros_like(x)
        dws = []
        for k, (off, _) in enumerate(taps):
            dm = dacc * masks[k]
            dx = dx + _shift_rows(w_ref[k:k + 1, :] * dm, -off)
            dws.append(jnp.sum(dm * _shift_rows(x, off), axis=0, keepdims=True))
        dx_ref[...] = dx
        db = jnp.sum(dacc, axis=0, keepdims=True)

        @pl.when(bb == 0)
        def _():
            for k in range(nk):
                dw_ref[k:k + 1, :] = dws[k]
            db_ref[...] = db

        @pl.when(bb > 0)
        def _():
            for k in range(nk):
                dw_ref[k:k + 1, :] += dws[k]
            db_ref[...] += db

    blk = pl.BlockSpec((None, s, ct), lambda j, bb: (bb, 0, j))
    par = lambda k: pl.BlockSpec((k, ct), lambda j, bb: (0, j))
    out_specs = [blk, par(nk), par(1)] + ([blk] if has_mul else [])
    out_shape = [jax.ShapeDtypeStruct(x.shape, F32), jax.ShapeDtypeStruct(w.shape, F32), jax.ShapeDtypeStruct(b.shape, F32)]
    if has_mul:
        out_shape.append(jax.ShapeDtypeStruct(x.shape, F32))
    return pl.pallas_call(
        body,
        name=name,
        grid=(c // ct, nb),
        in_specs=[blk, par(nk), par(1), blk] + ([blk] if has_mul else []),
        out_specs=out_specs,
        out_shape=out_shape,
        compiler_params=_cparams(("parallel", "arbitrary")),
    )(x, w, b, dout, *([mul] if has_mul else []))


def _scan_order(direction, adjoint, s_ctx, s_tot):
    nc, nt = s_ctx // SUBLANES, s_tot // SUBLANES
    if direction == 0:
        return ([(0, nt, 1)], False) if not adjoint else ([(nt - 1, nt, -1)], True)
    if not adjoint:
        return [(nc - 1, nc, -1), (nt - 1, nt - nc, -1)], True
    return [(nc, nt - nc, 1), (0, nc, 1)], False


def _last_row(h, descending):
    row = lax.broadcasted_iota(jnp.int32, h.shape, 0)
    pick = 0 if descending else SUBLANES - 1
    return jnp.sum(jnp.where(row == pick, h, 0.0), axis=0, keepdims=True)


def _prev_rows(h, carry, descending):
    row = lax.broadcasted_iota(jnp.int32, h.shape, 0)
    if descending:
        return jnp.where(row == SUBLANES - 1, carry, pltpu.roll(h, SUBLANES - 1, axis=0))
    return jnp.where(row == 0, carry, pltpu.roll(h, 1, axis=0))


def _scan_real(a_ref, x_ref, h_ref, hp_ref, order):
    ranges, descending = order
    width = a_ref.shape[1]
    row = lax.broadcasted_iota(jnp.int32, (SUBLANES, width), 0)

    def tile(i, carry):
        t0 = pl.multiple_of(i * SUBLANES, SUBLANES)
        a = a_ref[pl.ds(t0, SUBLANES), :]
        x = x_ref[pl.ds(t0, SUBLANES), :]
        for k in (1, 2, 4):
            sh = SUBLANES - k if descending else k
            keep = (row < SUBLANES - k) if descending else (row >= k)
            x = jnp.where(keep, a * pltpu.roll(x, sh, axis=0) + x, x)
            a = jnp.where(keep, a * pltpu.roll(a, sh, axis=0), a)
        h = a * carry + x
        if h_ref is not None:
            h_ref[pl.ds(t0, SUBLANES), :] = h
        if hp_ref is not None:
            hp_ref[pl.ds(t0, SUBLANES), :] = _prev_rows(h, carry, descending)
        return _last_row(h, descending)

    carry = jnp.zeros((1, width), F32)
    for first, count, step in ranges:
        carry = lax.fori_loop(0, count, lambda j, c, first=first, step=step: tile(first + step * j, c), carry)


def _cmul(ar, ai, br, bi):
    return ar * br - ai * bi, ar * bi + ai * br


def _scan_cplx(lr, li, xr_ref, xi_ref, hpr_ref, hpi_ref, order):
    ranges, descending = order
    width = xr_ref.shape[1]
    row = lax.broadcasted_iota(jnp.int32, (SUBLANES, width), 0)
    pw = [(lr, li)]
    for _ in range(SUBLANES - 1):
        pw.append(_cmul(pw[-1][0], pw[-1][1], lr, li))
    pr = jnp.zeros((SUBLANES, width), F32)
    pi = jnp.zeros((SUBLANES, width), F32)
    for r in range(SUBLANES):
        n = SUBLANES - 1 - r if descending else r
        pr = jnp.where(row == r, pw[n][0], pr)
        pi = jnp.where(row == r, pw[n][1], pi)

    def tile(i, carry):
        cr, ci = carry
        t0 = pl.multiple_of(i * SUBLANES, SUBLANES)
        xr = xr_ref[pl.ds(t0, SUBLANES), :]
        xi = xi_ref[pl.ds(t0, SUBLANES), :]
        for k in (1, 2, 4):
            sh = SUBLANES - k if descending else k
            keep = (row < SUBLANES - k) if descending else (row >= k)
            sr, si = _cmul(pw[k - 1][0], pw[k - 1][1], pltpu.roll(xr, sh, axis=0), pltpu.roll(xi, sh, axis=0))
            xr = jnp.where(keep, xr + sr, xr)
            xi = jnp.where(keep, xi + si, xi)
        hr, hi = _cmul(pr, pi, cr, ci)
        hr, hi = hr + xr, hi + xi
        xr_ref[pl.ds(t0, SUBLANES), :] = hr
        xi_ref[pl.ds(t0, SUBLANES), :] = hi
        if hpr_ref is not None:
            hpr_ref[pl.ds(t0, SUBLANES), :] = _prev_rows(hr, cr, descending)
            hpi_ref[pl.ds(t0, SUBLANES), :] = _prev_rows(hi, ci, descending)
        return _last_row(hr, descending), _last_row(hi, descending)

    carry = (jnp.zeros((1, width), F32), jnp.zeros((1, width), F32))
    for first, count, step in ranges:
        carry = lax.fori_loop(0, count, lambda j, c, first=first, step=step: tile(first + step * j, c), carry)


def _log1p_pos(y):
    return jnp.where(y < 0.01, y * (1.0 - y * (0.5 - y * (1.0 / 3.0 - 0.25 * y))), jnp.log(1.0 + y))


def _softplus(x):
    return jnp.maximum(x, 0.0) + _log1p_pos(jnp.exp(-jnp.abs(x)))


def _neg_expm1(z):
    series = -z * (1.0 + z * (0.5 + z * (1.0 / 6.0 + z * (1.0 / 24.0 + z * (1.0 / 120.0)))))
    return jnp.where(z > -0.1, series, 1.0 - jnp.exp(z))


def _lru_gates(u, w_a, b_a, w_i, b_i, lam):
    ub = u.astype(BF16)
    r = jax.nn.sigmoid(jnp.dot(ub, w_a.astype(BF16), preferred_element_type=F32) + b_a)
    i = jax.nn.sigmoid(jnp.dot(ub, w_i.astype(BF16), preferred_element_type=F32) + b_i)
    log_a = (-LRU_C) * _softplus(-lam) * r
    return jnp.exp(log_a), jnp.sqrt(_neg_expm1(2.0 * log_a)) * (i * u)


def _lru_specs(nblk, bw, order):
    w = pl.BlockSpec((2, None, bw, bw), lambda *g: (0, order(*g), 0, 0))
    v = pl.BlockSpec((2, None, 1, bw), lambda *g: (0, order(*g), 0, 0))
    return [w, v, w, v, v]


def lru_fwd(name, u, w_a, b_a, w_i, b_i, lam, s_ctx):
    nb, s, _ = u.shape
    nblk, bw = w_a.shape[1], w_a.shape[2]

    def body(u_ref, wa, ba, wi, bi, lm, o_ref, a_s, x_s, h_s):
        u_v = u_ref[...]
        for d in (0, 1):
            a, bx = _lru_gates(u_v, wa[d], ba[d], wi[d], bi[d], lm[d])
            a_s[...] = a
            x_s[...] = bx
            _scan_real(a_s, x_s, h_s, None, _scan_order(d, False, s_ctx, s))
            if d == 0:
                o_ref[...] = h_s[...]
            else:
                o_ref[...] += h_s[...]

    blk = pl.BlockSpec((None, s, bw), lambda b, n: (b, 0, n))
    return pl.pallas_call(
        body,
        name=name,
        grid=(nb, nblk),
        in_specs=[blk] + _lru_specs(nblk, bw, lambda b, n: n),
        out_specs=blk,
        out_shape=jax.ShapeDtypeStruct(u.shape, F32),
        scratch_shapes=[pltpu.VMEM((s, bw), F32)] * 3,
        compiler_params=_cparams(("parallel", "parallel")),
    )(u, w_a, b_a, w_i, b_i, lam)


def lru_bwd(name, u, w_a, b_a, w_i, b_i, lam, dh, s_ctx):
    nb, s, _ = u.shape
    nblk, bw = w_a.shape[1], w_a.shape[2]

    def body(u_ref, wa, ba, wi, bi, lm, dh_ref, du_ref, dwa, dba, dwi, dbi, dlm, a_s, x_s, h_s, hp_s, wp_s):
        b = pl.program_id(1)
        u_v = u_ref[...]
        dh_v = dh_ref[...]
        du = jnp.zeros_like(u_v)
        for d in (0, 1):
            (a, bx), vjp = jax.vjp(_lru_gates, u_v, wa[d], ba[d], wi[d], bi[d], lm[d])
            a_s[...] = a
            x_s[...] = bx
            _scan_real(a_s, x_s, None, hp_s, _scan_order(d, False, s_ctx, s))
            x_s[...] = a * dh_v
            _scan_real(a_s, x_s, None, wp_s, _scan_order(d, True, s_ctx, s))
            g = dh_v + wp_s[...]
            grads = vjp((g * hp_s[...], g))
            du = du + grads[0]
            for ref, val in zip((dwa, dba, dwi, dbi, dlm), grads[1:]):
                @pl.when(b == 0)
                def _(ref=ref, val=val):
                    ref[d] = val

                @pl.when(b > 0)
                def _(ref=ref, val=val):
                    ref[d] += val
        du_ref[...] = du

    blk = pl.BlockSpec((None, s, bw), lambda n, b: (b, 0, n))
    pspecs = _lru_specs(nblk, bw, lambda n, b: n)
    return pl.pallas_call(
        body,
        name=name,
        grid=(nblk, nb),
        in_specs=[blk] + pspecs + [blk],
        out_specs=[blk] + pspecs,
        out_shape=[jax.ShapeDtypeStruct(u.shape, F32)] + [jax.ShapeDtypeStruct(p.shape, F32) for p in (w_a, b_a, w_i, b_i, lam)],
        scratch_shapes=[pltpu.VMEM((s, bw), F32)] * 5,
        compiler_params=_cparams(("parallel", "arbitrary")),
    )(u, w_a, b_a, w_i, b_i, lam, dh)


S5_TILE_CH = 128
S5_TILE_STATES = S5_TILE_CH // S5_GROUP_CH * S5_STATE


def _dot_nt(a, b):
    return lax.dot_general(a, b, (((1,), (1,)), ((), ())), preferred_element_type=F32)


def _dot_tn(a, b):
    return lax.dot_general(a, b, (((0,), (0,)), ((), ())), preferred_element_type=F32)


def _s5_specs(order):
    lam = pl.BlockSpec((2, 1, S5_TILE_STATES), lambda *g: (0, 0, order(*g)))
    mat = pl.BlockSpec((2, None, S5_TILE_STATES, S5_TILE_CH), lambda *g: (0, order(*g), 0, 0))
    return [lam, lam, mat, mat, mat, mat]


def s5_fwd(name, u, lam_r, lam_i, bt_r, bt_i, ct_r, ct_i, s_ctx):
    nb, s, w = u.shape

    def body(u_ref, lr, li, btr, bti, ctr, cti, o_ref, xr_s, xi_s):
        ub = u_ref[...].astype(BF16)
        for d in (0, 1):
            xr_s[...] = _dot_nt(ub, btr[d].astype(BF16))
            xi_s[...] = _dot_nt(ub, bti[d].astype(BF16))
            _scan_cplx(lr[d], li[d], xr_s, xi_s, None, None, _scan_order(d, False, s_ctx, s))
            y = (jnp.dot(xr_s[...].astype(BF16), ctr[d].astype(BF16), preferred_element_type=F32)
                 - jnp.dot(xi_s[...].astype(BF16), cti[d].astype(BF16), preferred_element_type=F32))
            if d == 0:
                o_ref[...] = y
            else:
                o_ref[...] += y

    blk = pl.BlockSpec((None, s, S5_TILE_CH), lambda b, j: (b, 0, j))
    return pl.pallas_call(
        body,
        name=name,
        grid=(nb, w // S5_TILE_CH),
        in_specs=[blk] + _s5_specs(lambda b, j: j),
        out_specs=blk,
        out_shape=jax.ShapeDtypeStruct(u.shape, F32),
        scratch_shapes=[pltpu.VMEM((s, S5_TILE_STATES), F32)] * 2,
        compiler_params=_cparams(("parallel", "parallel")),
    )(u, lam_r, lam_i, bt_r, bt_i, ct_r, ct_i)


def s5_bwd(name, u, lam_r, lam_i, bt_r, bt_i, ct_r, ct_i, dy, s_ctx):
    nb, s, w = u.shape

    def body(u_ref, lr, li, btr, bti, ctr, cti, dy_ref, du_ref, dlr, dli, dbtr, dbti, dctr, dcti,
             hr_s, hi_s, hpr_s, hpi_s, gr_s, gi_s):
        b = pl.program_id(1)
        ub = u_ref[...].astype(BF16)
        dyb = dy_ref[...].astype(BF16)
        du = jnp.zeros((s, S5_TILE_CH), F32)
        for d in (0, 1):
            hr_s[...] = _dot_nt(ub, btr[d].astype(BF16))
            hi_s[...] = _dot_nt(ub, bti[d].astype(BF16))
            _scan_cplx(lr[d], li[d], hr_s, hi_s, hpr_s, hpi_s, _scan_order(d, False, s_ctx, s))
            d_ctr = _dot_tn(hr_s[...].astype(BF16), dyb)
            d_cti = -_dot_tn(hi_s[...].astype(BF16), dyb)
            gr_s[...] = _dot_nt(dyb, ctr[d].astype(BF16))
            gi_s[...] = -_dot_nt(dyb, cti[d].astype(BF16))
            _scan_cplx(lr[d], -li[d], gr_s, gi_s, None, None, _scan_order(d, True, s_ctx, s))
            gr, gi = gr_s[...], gi_s[...]
            hpr, hpi = hpr_s[...], hpi_s[...]
            d_lr = jnp.sum(gr * hpr + gi * hpi, axis=0, keepdims=True)
            d_li = jnp.sum(gi * hpr - gr * hpi, axis=0, keepdims=True)
            grb, gib = gr.astype(BF16), gi.astype(BF16)
            du = du + jnp.dot(grb, btr[d].astype(BF16), preferred_element_type=F32)
            du = du + jnp.dot(gib, bti[d].astype(BF16), preferred_element_type=F32)
            d_btr = _dot_tn(grb, ub)
            d_bti = _dot_tn(gib, ub)
            for ref, val in zip((dlr, dli, dbtr, dbti, dctr, dcti), (d_lr, d_li, d_btr, d_bti, d_ctr, d_cti)):
                @pl.when(b == 0)
                def _(ref=ref, val=val):
                    ref[d] = val

                @pl.when(b > 0)
                def _(ref=ref, val=val):
                    ref[d] += val
        du_ref[...] = du

    blk = pl.BlockSpec((None, s, S5_TILE_CH), lambda j, b: (b, 0, j))
    pspecs = _s5_specs(lambda j, b: j)
    params = (lam_r, lam_i, bt_r, bt_i, ct_r, ct_i)
    return pl.pallas_call(
        body,
        name=name,
        grid=(w // S5_TILE_CH, nb),
        in_specs=[blk] + pspecs + [blk],
        out_specs=[blk] + pspecs,
        out_shape=[jax.ShapeDtypeStruct(u.shape, F32)] + [jax.ShapeDtypeStruct(p.shape, F32) for p in params],
        scratch_shapes=[pltpu.VMEM((s, S5_TILE_STATES), F32)] * 6,
        compiler_params=_cparams(("parallel", "arbitrary")),
    )(u, lam_r, lam_i, bt_r, bt_i, ct_r, ct_i, dy)


def small_fwd(name, f, ins, out_shapes):
    n = len(ins)

    def body(*refs):
        for o, r in zip(refs[n:], f([r[...] for r in refs[:n]])):
            o[...] = r

    return pl.pallas_call(
        body, name=name,
        out_shape=[jax.ShapeDtypeStruct(s, F32) for s in out_shapes],
        compiler_params=_cparams(),
    )(*ins)


def small_bwd(name, f, ins, cots):
    n, nc = len(ins), len(cots)

    def body(*refs):
        _, vjp = jax.vjp(f, [r[...] for r in refs[:n]])
        (grads,) = vjp([r[...] for r in refs[n:n + nc]])
        for o, r in zip(refs[n + nc:], grads):
            o[...] = r

    return pl.pallas_call(
        body, name=name,
        out_shape=[jax.ShapeDtypeStruct(a.shape, F32) for a in ins],
        compiler_params=_cparams(),
    )(*ins, *cots)


def _row(x, r):
    return jnp.sum(jnp.where(lax.broadcasted_iota(jnp.int32, x.shape, 0) == r, x, 0.0), axis=0, keepdims=True)


def _col(x, c):
    return jnp.sum(jnp.where(lax.broadcasted_iota(jnp.int32, x.shape, 1) == c, x, 0.0), axis=1, keepdims=True)


def _chunk_at(i, reverse, ncc, nc):
    if not reverse:
        return i
    return jnp.where(i < ncc, ncc - 1 - i, nc - 1 - (i - ncc))


def _tri(n, reverse):
    li = lax.broadcasted_iota(jnp.int32, (n, n), 0)
    si = lax.broadcasted_iota(jnp.int32, (n, n), 1)
    return jnp.where((li <= si) if reverse else (li >= si), 1.0, 0.0)


_HI = lax.Precision.HIGHEST


def _ssd_chunk(xs, bm, cm, dtc, dtr, a_row, a_col, hs, reverse):
    n = bm.shape[0]
    last = 0 if reverse else n - 1
    tri = _tri(n, reverse)
    cum_c = jnp.dot(tri, dtc * -jnp.exp(a_row), precision=_HI, preferred_element_type=F32)
    cum_r = lax.dot_general(dtr * -jnp.exp(a_col), tri, (((1,), (1,)), ((), ())), precision=_HI, preferred_element_type=F32)
    tot_r = _row(cum_c, last)
    bmb, cmb = bm.astype(BF16), cm.astype(BF16)
    cb = _dot_nt(cmb, bmb)
    ys, hn = [], []
    for hd in range(len(xs)):
        cl = _col(cum_c, hd)
        tot = _col(tot_r, hd)
        decay = jnp.exp(jnp.where(tri > 0.0, cl - _row(cum_r, hd), -jnp.inf))
        xd = xs[hd] * _col(dtc, hd)
        y = jnp.dot((cb * decay).astype(BF16), xd.astype(BF16), preferred_element_type=F32)
        y = y + _dot_nt(cmb, hs[hd].astype(BF16)) * jnp.exp(cl)
        hnew = hs[hd] * jnp.exp(tot) + _dot_tn((xd * jnp.exp(tot - cl)).astype(BF16), bmb)
        ys.append(y)
        hn.append(hnew)
    return ys, hn


def _ssd_specs(reverse, ncc, nc, order):
    ch = lambda *g: _chunk_at(order(*g)[2], reverse, ncc, nc)
    b_ = lambda *g: order(*g)[0]
    g_ = lambda *g: order(*g)[1]
    xw = SSD_HPG * SSD_HEAD_DIM
    nxb = SSD_GROUPS * xw // SSD_STATE
    return [
        pl.BlockSpec((None, SSD_CHUNK, xw), lambda *g: (b_(*g), ch(*g), g_(*g))),
        pl.BlockSpec((None, SSD_CHUNK, SSD_STATE), lambda *g: (b_(*g), ch(*g), nxb + g_(*g))),
        pl.BlockSpec((None, SSD_CHUNK, SSD_STATE), lambda *g: (b_(*g), ch(*g), nxb + SSD_GROUPS + g_(*g))),
        pl.BlockSpec((None, None, SSD_CHUNK, SSD_HPG), lambda *g: (b_(*g), g_(*g), ch(*g), 0)),
        pl.BlockSpec((None, None, SSD_HPG, SSD_CHUNK), lambda *g: (b_(*g), g_(*g), 0, ch(*g))),
        pl.BlockSpec((None, 1, SSD_HPG), lambda *g: (g_(*g), 0, 0)),
        pl.BlockSpec((None, SSD_HPG, 1), lambda *g: (g_(*g), 0, 0)),
    ]


def ssd_fwd(name, xbc, dt_col, dt_row, a_row, a_col, reverse, s_ctx):
    nb, s, _ = xbc.shape
    nc, ncc = s // SSD_CHUNK, s_ctx // SSD_CHUNK
    xw, p = SSD_HPG * SSD_HEAD_DIM, SSD_HEAD_DIM

    def body(x_ref, bm_ref, cm_ref, dtc_ref, dtr_ref, ar_ref, ac_ref, y_ref, hst_ref, h_s):
        i = pl.program_id(2)

        @pl.when(i == 0)
        def _():
            h_s[...] = jnp.zeros_like(h_s)

        hst_ref[...] = h_s[...]
        xs = [x_ref[:, p * hd:p * (hd + 1)] for hd in range(SSD_HPG)]
        hs = [h_s[hd] for hd in range(SSD_HPG)]
        ys, hn = _ssd_chunk(xs, bm_ref[...], cm_ref[...], dtc_ref[...], dtr_ref[...], ar_ref[...], ac_ref[...], hs, reverse)
        for hd in range(SSD_HPG):
            y_ref[:, p * hd:p * (hd + 1)] = ys[hd]
            h_s[hd] = hn[hd]

    order = lambda b, g, i: (b, g, i)
    state = (SSD_HPG, SSD_HEAD_DIM, SSD_STATE)
    return pl.pallas_call(
        body,
        name=name,
        grid=(nb, SSD_GROUPS, nc),
        in_specs=_ssd_specs(reverse, ncc, nc, order),
        out_specs=[pl.BlockSpec((None, SSD_CHUNK, xw), lambda b, g, i: (b, _chunk_at(i, reverse, ncc, nc), g)),
                   pl.BlockSpec((None, None, None) + state, lambda b, g, i: (b, g, i, 0, 0, 0))],
        out_shape=[jax.ShapeDtypeStruct((nb, s, SSD_GROUPS * xw), F32),
                   jax.ShapeDtypeStruct((nb, SSD_GROUPS, nc) + state, F32)],
        scratch_shapes=[pltpu.VMEM(state, F32)],
        compiler_params=_cparams(("parallel", "parallel", "arbitrary")),
    )(xbc, xbc, xbc, dt_col, dt_row, a_row, a_col)


def ssd_bwd(name, xbc, dt_col, dt_row, a_row, a_col, hst, dy, reverse, s_ctx):
    nb, s, _ = xbc.shape
    nc, ncc = s // SSD_CHUNK, s_ctx // SSD_CHUNK
    xw, p = SSD_HPG * SSD_HEAD_DIM, SSD_HEAD_DIM

    def body(x_ref, bm_ref, cm_ref, dtc_ref, dtr_ref, ar_ref, ac_ref, hst_ref, dy_ref,
             dx_ref, dbm_ref, dcm_ref, ddtc_ref, ddtr_ref, dar_ref, dac_ref, dh_s):
        i = pl.program_id(2)

        @pl.when(i == 0)
        def _():
            dh_s[...] = jnp.zeros_like(dh_s)

        xs = [x_ref[:, p * hd:p * (hd + 1)] for hd in range(SSD_HPG)]
        hs = [hst_ref[hd] for hd in range(SSD_HPG)]
        f = functools.partial(_ssd_chunk, reverse=reverse)
        _, vjp = jax.vjp(f, xs, bm_ref[...], cm_ref[...], dtc_ref[...], dtr_ref[...], ar_ref[...], ac_ref[...], hs)
        dys = [dy_ref[:, p * hd:p * (hd + 1)] for hd in range(SSD_HPG)]
        dxs, dbm, dcm, ddtc, ddtr, dar, dac, dhs = vjp((dys, [dh_s[hd] for hd in range(SSD_HPG)]))
        for hd in range(SSD_HPG):
            dx_ref[:, p * hd:p * (hd + 1)] = dxs[hd]
            dh_s[hd] = dhs[hd]
        dbm_ref[...] = dbm
        dcm_ref[...] = dcm
        ddtc_ref[...] = ddtc
        ddtr_ref[...] = ddtr

        @pl.when(i == 0)
        def _():
            dar_ref[...] = dar
            dac_ref[...] = dac

        @pl.when(i > 0)
        def _():
            dar_ref[...] += dar
            dac_ref[...] += dac

    order = lambda b, g, i: (b, g, nc - 1 - i)
    ch = lambda b, g, i: _chunk_at(nc - 1 - i, reverse, ncc, nc)
    state = (SSD_HPG, SSD_HEAD_DIM, SSD_STATE)
    in_specs = _ssd_specs(reverse, ncc, nc, order) + [
        pl.BlockSpec((None, None, None) + state, lambda b, g, i: (b, g, nc - 1 - i, 0, 0, 0)),
        pl.BlockSpec((None, SSD_CHUNK, xw), lambda b, g, i: (b, ch(b, g, i), g))]
    gn = SSD_GROUPS * SSD_STATE
    out_specs = [
        pl.BlockSpec((None, SSD_CHUNK, xw), lambda b, g, i: (b, ch(b, g, i), g)),
        pl.BlockSpec((None, SSD_CHUNK, SSD_STATE), lambda b, g, i: (b, ch(b, g, i), g)),
        pl.BlockSpec((None, SSD_CHUNK, SSD_STATE), lambda b, g, i: (b, ch(b, g, i), g)),
        pl.BlockSpec((None, None, SSD_CHUNK, SSD_HPG), lambda b, g, i: (b, g, ch(b, g, i), 0)),
        pl.BlockSpec((None, None, SSD_HPG, SSD_CHUNK), lambda b, g, i: (b, g, 0, ch(b, g, i))),
        pl.BlockSpec((None, None, 1, SSD_HPG), lambda b, g, i: (b, g, 0, 0)),
        pl.BlockSpec((None, None, SSD_HPG, 1), lambda b, g, i: (b, g, 0, 0)),
    ]
    out_shape = [
        jax.ShapeDtypeStruct((nb, s, SSD_GROUPS * xw), F32),
        jax.ShapeDtypeStruct((nb, s, gn), F32),
        jax.ShapeDtypeStruct((nb, s, gn), F32),
        jax.ShapeDtypeStruct(dt_col.shape, F32),
        jax.ShapeDtypeStruct(dt_row.shape, F32),
        jax.ShapeDtypeStruct((nb, SSD_GROUPS, 1, SSD_HPG), F32),
        jax.ShapeDtypeStruct((nb, SSD_GROUPS, SSD_HPG, 1), F32),
    ]
    return pl.pallas_call(
        body,
        name=name,
        grid=(nb, SSD_GROUPS, nc),
        in_specs=in_specs,
        out_specs=out_specs,
        out_shape=out_shape,
        scratch_shapes=[pltpu.VMEM(state, F32)],
        compiler_params=_cparams(("parallel", "parallel", "arbitrary")),
    )(xbc, xbc, xbc, dt_col, dt_row, a_row, a_col, hst, dy)


def _hg_chunk(q, k, lf, v, st, reverse):
    n = q.shape[0]
    last = 0 if reverse else n - 1
    cum = jnp.dot(_tri(n, reverse), lf, precision=_HI, preferred_element_type=F32)
    rows = lax.broadcasted_iota(jnp.int32, q.shape, 0)
    y = jnp.zeros(v.shape, F32)
    for s in range(n):
        later = (rows <= s) if reverse else (rows >= s)
        e = jnp.exp(jnp.where(later, cum - _row(cum, s), -jnp.inf))
        att = jnp.sum(q * _row(k, s) * e, axis=1, keepdims=True)
        y = y + att * _row(v, s)
    y = y + _dot_nt((q * jnp.exp(cum)).astype(BF16), st.astype(BF16))
    tot = _row(cum, last)
    st_new = st * jnp.exp(tot) + _dot_tn(v.astype(BF16), (k * jnp.exp(tot - cum)).astype(BF16))
    return y, st_new


def _hg_super(s_ctx):
    return min(256, s_ctx)


def hg_fwd(name, q, k, lf, v, reverse, s_ctx):
    nb, s, w = q.shape
    nh, dk, sup = w // HG_DK, HG_DK, _hg_super(s_ctx)
    nsup, nsc, cps = s // sup, s_ctx // sup, sup // HG_CHUNK

    def body(q_ref, k_ref, lf_ref, v_ref, y_ref, hst_ref, st_s):
        i = pl.program_id(2)

        @pl.when(i == 0)
        def _():
            st_s[...] = jnp.zeros_like(st_s)

        def step(c, st):
            r0 = pl.multiple_of((cps - 1 - c if reverse else c) * HG_CHUNK, HG_CHUNK)
            rows = pl.ds(r0, HG_CHUNK)
            hst_ref[c] = st
            y, st_new = _hg_chunk(q_ref[rows, :], k_ref[rows, :], lf_ref[rows, :], v_ref[rows, :], st, reverse)
            y_ref[rows, :] = y
            return st_new

        st_s[...] = lax.fori_loop(0, cps, step, st_s[...])

    blk = pl.BlockSpec((None, sup, dk), lambda b, h, i: (b, _chunk_at(i, reverse, nsc, nsup), h))
    return pl.pallas_call(
        body,
        name=name,
        grid=(nb, nh, nsup),
        in_specs=[blk] * 4,
        out_specs=[blk, pl.BlockSpec((None, None, cps, dk, dk), lambda b, h, i: (b, h, i, 0, 0))],
        out_shape=[jax.ShapeDtypeStruct(q.shape, F32), jax.ShapeDtypeStruct((nb, nh, s // HG_CHUNK, dk, dk), F32)],
        scratch_shapes=[pltpu.VMEM((dk, dk), F32)],
        compiler_params=_cparams(("parallel", "parallel", "arbitrary")),
    )(q, k, lf, v)


def hg_bwd(name, q, k, lf, v, hst, dy, reverse, s_ctx):
    nb, s, w = q.shape
    nh, dk, sup = w // HG_DK, HG_DK, _hg_super(s_ctx)
    nsup, nsc, cps = s // sup, s_ctx // sup, sup // HG_CHUNK

    def body(q_ref, k_ref, lf_ref, v_ref, hst_ref, dy_ref, dq_ref, dk_ref, dlf_ref, dv_ref, dst_s):
        i = pl.program_id(2)

        @pl.when(i == 0)
        def _():
            dst_s[...] = jnp.zeros_like(dst_s)

        def step(cc, dst):
            c = cps - 1 - cc
            r0 = pl.multiple_of((cps - 1 - c if reverse else c) * HG_CHUNK, HG_CHUNK)
            rows = pl.ds(r0, HG_CHUNK)
            f = functools.partial(_hg_chunk, reverse=reverse)
            _, vjp = jax.vjp(f, q_ref[rows, :], k_ref[rows, :], lf_ref[rows, :], v_ref[rows, :], hst_ref[c])
            dq, dkk, dlf, dv, dst_prev = vjp((dy_ref[rows, :], dst))
            dq_ref[rows, :] = dq
            dk_ref[rows, :] = dkk
            dlf_ref[rows, :] = dlf
            dv_ref[rows, :] = dv
            return dst_prev

        dst_s[...] = lax.fori_loop(0, cps, step, dst_s[...])

    blk = pl.BlockSpec((None, sup, dk), lambda b, h, i: (b, _chunk_at(nsup - 1 - i, reverse, nsc, nsup), h))
    return pl.pallas_call(
        body,
        name=name,
        grid=(nb, nh, nsup),
        in_specs=[blk] * 4 + [pl.BlockSpec((None, None, cps, dk, dk), lambda b, h, i: (b, h, nsup - 1 - i, 0, 0)), blk],
        out_specs=[blk] * 4,
        out_shape=[jax.ShapeDtypeStruct(q.shape, F32)] * 4,
        scratch_shapes=[pltpu.VMEM((dk, dk), F32)],
        compiler_params=_cparams(("parallel", "parallel", "arbitrary")),
    )(q, k, lf, v, hst, dy)


def f_s5_discretize(ins):
    lam_re, lam_im, log_step, b_re, b_im = ins
    step = jnp.exp(log_step)
    mag = jnp.exp(lam_re * step)
    ar, ai = mag * jnp.cos(lam_im * step), mag * jnp.sin(lam_im * step)
    den = lam_re * lam_re + lam_im * lam_im
    zr = ((ar - 1.0) * lam_re + ai * lam_im) / den
    zi = (ai * lam_re - (ar - 1.0) * lam_im) / den
    return [ar, ai, zr * b_re - zi * b_im, zr * b_im + zi * b_re]


def s5_tiles_of(m):
    g, p, k = m.shape
    gt = S5_TILE_CH // k
    eye = jnp.eye(gt, dtype=m.dtype)
    t = m.reshape(g // gt, gt, p, 1, k) * eye[None, :, None, :, None]
    return t.reshape(g // gt, gt * p, gt * k)


def s5_groups_of(t, g, p, k):
    gt = S5_TILE_CH // k
    d = jnp.diagonal(t.reshape(g // gt, gt, p, gt, k), axis1=1, axis2=3)
    return jnp.transpose(d, (0, 3, 1, 2)).reshape(g, p, k)


def f_lower_bounds(ins):
    (logits,) = ins
    e = jnp.exp(logits - jnp.max(logits, axis=0, keepdims=True))
    p = e / jnp.sum(e, axis=0, keepdims=True)
    n = logits.shape[0]
    li = lax.broadcasted_iota(jnp.int32, (n, n), 0)
    si = lax.broadcasted_iota(jnp.int32, (n, n), 1)
    after_first = jnp.where(jnp.logical_and(si >= 1, si <= li), 1.0, 0.0)
    return [jnp.dot(after_first, p, precision=_HI, preferred_element_type=F32)]


def f_silu(ins):
    return [_silu(ins[0])]


def f_norm_keep(shift_row, scale_row):
    def f(tv, mv, pv):
        return [tv[0], _rms(tv[0], pv[0]) * (1.0 + mv[scale_row]) + mv[shift_row]]
    return f


def f_dt(tv, mv, pv):
    return [_softplus(tv[0] + pv[0])]


def f_even_finish(tv, mv, pv):
    y_f, y_b, xs, z, h_sum, gy = tv
    d_exp, g = pv
    y = _rms((y_f + y_b + d_exp * xs) * _silu(z), g)
    return [y, h_sum * jax.nn.gelu(gy)]


def f_odd_prep(tv, mv, pv):
    q, f_f, f_b = tv
    (lb,) = pv
    outs = [_silu(q)]
    for f in (f_f, f_b):
        outs.append((1.0 - lb) * jax.nn.sigmoid(-f))
        outs.append(jnp.log(lb + (1.0 - lb) * jax.nn.sigmoid(f)))
    return outs


def f_odd_finish(tv, mv, pv):
    o_f, o_b, g, s5y, u = tv
    norm_g, s5_d, glu_w, glu_b = pv
    o = o_f + o_b
    w = o.shape[1]
    hi = lax.broadcasted_iota(jnp.int32, (w, w), 0) // HG_DK
    hj = lax.broadcasted_iota(jnp.int32, (w, w), 1) // HG_DK
    head_mean = jnp.where(hi == hj, 1.0 / HG_DK, 0.0)
    ms = jnp.dot(o * o, head_mean, precision=_HI, preferred_element_type=F32)
    on = o * lax.rsqrt(ms + RMS_EPS) * norm_g * _silu(g)
    y = jax.nn.gelu(s5y + s5_d * u)
    gate = jax.nn.sigmoid(jnp.dot(y.astype(BF16), glu_w.astype(BF16), preferred_element_type=F32) + glu_b)
    return [on, y * gate]


def final_loss(name, s, br, mod, g, target, s_ctx):
    nb, st, d = s.shape
    tb = TOK_BLOCK
    assert s_ctx == tb

    def lossf(sv, bv, gate, gv, tv):
        y = _rms(sv + gate * bv, gv)
        err = jnp.square(y - tv)
        return 0.5 * jnp.sum(jnp.mean(err, axis=-1, keepdims=True), axis=0, keepdims=True)

    def body(s_ref, b_ref, m_ref, g_ref, t_ref, l_ref, ds_ref, db_ref, dm_ref, dg_ref):
        b, t = pl.program_id(0), pl.program_id(1)

        @pl.when(t == 0)
        def _():
            ds_ref[...] = jnp.zeros_like(ds_ref)
            db_ref[...] = jnp.zeros_like(db_ref)
            dm_ref[...] = jnp.zeros_like(dm_ref)
            l_ref[...] = jnp.zeros_like(l_ref)

        @pl.when(jnp.logical_and(b == 0, t == 0))
        def _():
            dg_ref[...] = jnp.zeros_like(dg_ref)

        @pl.when(t > 0)
        def _():
            gate = m_ref[N_MOD - 1:N_MOD, :]
            l, vjp = jax.vjp(lossf, s_ref[...], b_ref[...], gate, g_ref[...], t_ref[...])
            ds, db, dgate, dg, _ = vjp(jnp.ones((1, 1), F32))
            ds_ref[...] = ds
            db_ref[...] = db
            dg_ref[...] += dg
            l_ref[...] += jnp.broadcast_to(l, l_ref.shape)

            @pl.when(t == 1)
            def _():
                dm_ref[...] = jnp.zeros_like(dm_ref)
                dm_ref[N_MOD - 1:N_MOD, :] = dgate

            @pl.when(t > 1)
            def _():
                dm_ref[N_MOD - 1:N_MOD, :] += dgate

    tok = pl.BlockSpec((None, tb, d), lambda b, t: (b, t, 0))
    modspec = pl.BlockSpec((None, N_MOD, d), _mod_index)
    gspec = pl.BlockSpec((1, d), lambda b, t: (0, 0))
    return pl.pallas_call(
        body,
        name=name,
        grid=(nb, st // tb),
        in_specs=[tok, tok, modspec, gspec, pl.BlockSpec((None, tb, d), lambda b, t: (b, jnp.maximum(t - 1, 0), 0))],
        out_specs=[pl.BlockSpec((None, SUBLANES, 128), lambda b, t: (b, 0, 0)), tok, tok, modspec, gspec],
        out_shape=[jax.ShapeDtypeStruct((nb, SUBLANES, 128), F32), jax.ShapeDtypeStruct(s.shape, F32),
                   jax.ShapeDtypeStruct(s.shape, F32), jax.ShapeDtypeStruct(mod.shape, F32), jax.ShapeDtypeStruct(g.shape, F32)],
        compiler_params=_cparams(("arbitrary", "arbitrary")),
    )(s, br, mod, g, target)


def adamw(name, w, g, m, v):
    shape = w.shape
    cols = shape[-1] if w.ndim >= 2 else w.size
    rows = w.size // cols
    tr = _pick(rows, (512, 256, 128, 64, 32, 16, 8))

    def body(w_ref, g_ref, m_ref, v_ref, d_ref, nm_ref, nv_ref):
        gv = g_ref[...]
        nm = ADAM_B1 * m_ref[...] + (1.0 - ADAM_B1) * gv
        nv = ADAM_B2 * v_ref[...] + (1.0 - ADAM_B2) * jnp.square(gv)
        m_hat = nm / (1.0 - ADAM_B1 ** ADAM_STEP)
        v_hat = nv / (1.0 - ADAM_B2 ** ADAM_STEP)
        d_ref[...] = -ADAM_LR * (m_hat / (jnp.sqrt(v_hat) + ADAM_EPS) + ADAM_WD * w_ref[...])
        nm_ref[...] = nm
        nv_ref[...] = nv

    spec = pl.BlockSpec((tr, cols), lambda i: (i, 0))
    outs = pl.pallas_call(
        body,
        name=name,
        grid=(rows // tr,),
        in_specs=[spec] * 4,
        out_specs=[spec] * 3,
        out_shape=[jax.ShapeDtypeStruct((rows, cols), F32)] * 3,
        compiler_params=_cparams(("parallel",)),
    )(*(a.reshape(rows, cols) for a in (w, g, m, v)))
    return tuple(o.reshape(shape) for o in outs)


EV_COLS = {"z": (0, 1024), "xbc": (1024, 2560), "dt": (2560, 2592), "gy": (2592, 3616), "u": (3616, 4640)}
OD_COLS = {"q": (0, 768), "ff": (768, 1536), "fb": (1536, 2304), "v": (2304, 3072), "g": (3072, 3840), "u": (3840, 4096)}
EV_OUT_ROWS = ((0, 1024), (1024, 2048))
OD_OUT_ROWS = ((0, 768), (768, 1024))
LANES = 128


def _pad_to_lanes(w):
    n = w.shape[1]
    return w if n % LANES == 0 else jnp.pad(w, ((0, 0), (0, LANES - n % LANES)))


def _layer_weights(l, big):
    j = l // 2
    even = l % 2 == 0
    w_in = big["ev_w_in" if even else "od_w_in"][j]
    w_out = big["ev_w_out" if even else "od_w_out"][j]
    lw = {"in": {}, "out": []}
    for name, (a, b) in (EV_COLS if even else OD_COLS).items():
        w = _pad_to_lanes(w_in[:, a:b])
        lw["in"][name] = (w, w.T)
    for a, b in (EV_OUT_ROWS if even else OD_OUT_ROWS):
        lw["out"].append((w_out[a:b], w_out[a:b].T))
    for name in ("gate", "up", "down"):
        w = big["ffn_w_" + name][l]
        lw[name] = (w, w.T)
    return lw


def _rows2d(a):
    return a.reshape(-1, a.shape[-1])


def _mm3(a, w, name, out_dtype=F32):
    return mm([(_rows2d(a), w)], name, out_dtype).reshape(a.shape[:-1] + (w.shape[1],))


def _wgrad(a, d, name):
    return mm([(_rows2d(a).T, _rows2d(d))], name)


def _dgrad(pairs, name, shape3):
    return mm([(_rows2d(d), wt) for d, wt in pairs], name).reshape(shape3[:-1] + (pairs[0][1].shape[1],))


def _dir_dt(dt, d):
    nb, s, _ = dt.shape
    dd = dt[:, :, SSD_HEADS * d:SSD_HEADS * (d + 1)].reshape(nb, s, SSD_GROUPS, SSD_HPG)
    return jnp.transpose(dd, (0, 2, 1, 3)), jnp.transpose(dd, (0, 2, 3, 1))


def _s5_prepare(p, j, tag):
    g_, p_, k_ = S5_GROUPS, S5_STATE, S5_GROUP_CH
    col = lambda t: t.reshape(g_ * p_, 1)
    ins, outs = [], []
    for d in (0, 1):
        i_d = [col(p["s5_lam_re"][j, d]), col(p["s5_lam_im"][j, d]), col(jnp.repeat(p["s5_log_step"][j, d], p_)),
               p["s5_b_re"][j].reshape(g_ * p_, k_), p["s5_b_im"][j].reshape(g_ * p_, k_)]
        ins.append(i_d)
        outs.append(small_fwd(f"{tag}_disc{d}", f_s5_discretize, i_d, [(g_ * p_, 1)] * 2 + [(g_ * p_, k_)] * 2))
    lam_r = jnp.stack([o[0].reshape(1, g_ * p_) for o in outs])
    lam_i = jnp.stack([o[1].reshape(1, g_ * p_) for o in outs])
    bt_r = jnp.stack([s5_tiles_of(o[2].reshape(g_, p_, k_)) for o in outs])
    bt_i = jnp.stack([s5_tiles_of(o[3].reshape(g_, p_, k_)) for o in outs])
    ct_r = jnp.stack([s5_tiles_of(jnp.transpose(p["s5_c_re"][j, d], (0, 2, 1))) for d in (0, 1)])
    ct_i = jnp.stack([s5_tiles_of(jnp.transpose(p["s5_c_im"][j, d], (0, 2, 1))) for d in (0, 1)])
    return ins, (lam_r, lam_i, bt_r, bt_i, ct_r, ct_i)


def _s5_param_grads(ins, grads, tag):
    g_, p_, k_ = S5_GROUPS, S5_STATE, S5_GROUP_CH
    dlr, dli, dbtr, dbti, dctr, dcti = grads
    g_lre, g_lim, g_ls, g_bre, g_bim = [], [], [], 0.0, 0.0
    for d in (0, 1):
        cots = [dlr[d].reshape(g_ * p_, 1), dli[d].reshape(g_ * p_, 1),
                s5_groups_of(dbtr[d], g_, p_, k_).reshape(g_ * p_, k_), s5_groups_of(dbti[d], g_, p_, k_).reshape(g_ * p_, k_)]
        g = small_bwd(f"{tag}_disc_bwd{d}", f_s5_discretize, ins[d], cots)
        g_lre.append(g[0].reshape(g_, p_))
        g_lim.append(g[1].reshape(g_, p_))
        g_ls.append(g[2].reshape(g_, p_).sum(-1))
        g_bre = g_bre + g[3].reshape(g_, p_, k_)
        g_bim = g_bim + g[4].reshape(g_, p_, k_)
    g_cre = jnp.stack([jnp.transpose(s5_groups_of(dctr[d], g_, p_, k_), (0, 2, 1)) for d in (0, 1)])
    g_cim = jnp.stack([jnp.transpose(s5_groups_of(dcti[d], g_, p_, k_), (0, 2, 1)) for d in (0, 1)])
    return jnp.stack(g_lre), jnp.stack(g_lim), jnp.stack(g_ls), g_bre, g_bim, g_cre, g_cim


def _even_mixer_fwd(l, hn, p, lw, s_ctx):
    j = l // 2
    t1 = taps_1d(4, s_ctx, hn.shape[1])
    r = {"hn": hn}
    proj = {n: _mm3(hn, lw["in"][n][0], f"l{l}_proj_{n}") for n in EV_COLS}
    r["z"], r["xbc"], r["gy"], r["u"] = proj["z"], proj["xbc"], proj["gy"], proj["u"]
    r["dtp"] = proj["dt"][:, :, :2 * SSD_HEADS]
    r["xbc_c"] = conv_fwd(f"l{l}_ssd_conv", r["xbc"], p["ssd_conv_w"][j], p["ssd_conv_b"][j][None], t1, "silu")
    r["u_c"] = conv_fwd(f"l{l}_lru_conv", r["u"], p["lru_conv_w"][j], p["lru_conv_b"][j][None], t1, "none")
    r["dt_bias"] = p["ssd_dt_bias"][j].reshape(1, 2 * SSD_HEADS)
    (r["dt"],) = tok_fwd(f"l{l}_dt", f_dt, [r["dtp"]], None, [r["dt_bias"]], [2 * SSD_HEADS], [F32])
    r["ys"], r["hst"], r["dts"], r["alog"] = [], [], [], []
    for d in (0, 1):
        dtc, dtr = _dir_dt(r["dt"], d)
        al = p["ssd_a_log"][j, d].reshape(SSD_GROUPS, SSD_HPG)
        al_r, al_c = al[:, None, :], al[:, :, None]
        y, hst = ssd_fwd(f"l{l}_ssd_fwd{d}", r["xbc_c"], dtc, dtr, al_r, al_c, bool(d), s_ctx)
        r["ys"].append(y)
        r["hst"].append(hst)
        r["dts"].append((dtc, dtr))
        r["alog"].append((al_r, al_c))
    v4 = lambda t: t.reshape(2, LRU_BLOCKS, 1, LRU_BLOCK_W)
    r["lru_p"] = (p["lru_w_a"][j], v4(p["lru_b_a"][j]), p["lru_w_i"][j], v4(p["lru_b_i"][j]), v4(p["lru_lam"][j]))
    r["h_sum"] = lru_fwd(f"l{l}_lru_fwd", r["u_c"], *r["lru_p"], s_ctx)
    r["xs"] = r["xbc_c"][:, :, :SSD_HEADS * SSD_HEAD_DIM]
    r["fin_p"] = [jnp.repeat(p["ssd_d"][j], SSD_HEAD_DIM)[None], p["ssd_norm_g"][j][None]]
    r["fin_in"] = [r["ys"][0], r["ys"][1], r["xs"], r["z"], r["h_sum"], r["gy"]]
    r["o"] = tok_fwd(f"l{l}_even_finish", f_even_finish, r["fin_in"], None, r["fin_p"], [1024, 1024], [BF16, BF16])
    return r


def _even_mixer_bwd(l, r, dox, p, lw, s_ctx, grads):
    j = l // 2
    shape3 = dox.shape
    t1 = taps_1d(4, s_ctx, shape3[1])
    grads["ev_w_out"][j] = jnp.concatenate([_wgrad(o, dox, f"l{l}_dwout{i}") for i, o in enumerate(r["o"])], axis=0)
    do = [_dgrad([(dox, lw["out"][i][1])], f"l{l}_dout{i}", shape3) for i in range(2)]
    (dy, _, dxs, dz, dh_sum, dgy), _, (dd_exp, grads["ssd_norm_g"][j]) = tok_bwd(
        f"l{l}_even_finish_bwd", f_even_finish, r["fin_in"], None, r["fin_p"], do, [F32] * 6)
    grads["ssd_d"][j] = dd_exp.reshape(SSD_HEADS, SSD_HEAD_DIM).sum(-1)
    du_c, dwa, dba, dwi, dbi, dlam = lru_bwd(f"l{l}_lru_bwd", r["u_c"], *r["lru_p"], dh_sum, s_ctx)
    grads["lru_w_a"][j], grads["lru_w_i"][j] = dwa, dwi
    v2 = lambda t: t.reshape(2, LRU_BLOCKS * LRU_BLOCK_W)
    grads["lru_b_a"][j], grads["lru_b_i"][j], grads["lru_lam"][j] = v2(dba), v2(dbi), v2(dlam)
    dx_sum, dbm_sum, dcm_sum, ddts, dalog = dxs, 0.0, 0.0, [], []
    for d in (0, 1):
        dx, dbm, dcm, ddtc, ddtr, dar, dac = ssd_bwd(
            f"l{l}_ssd_bwd{d}", r["xbc_c"], *r["dts"][d], *r["alog"][d], r["hst"][d], dy, bool(d), s_ctx)
        dx_sum, dbm_sum, dcm_sum = dx_sum + dx, dbm_sum + dbm, dcm_sum + dcm
        ddts.append((jnp.transpose(ddtc, (0, 2, 1, 3)) + jnp.transpose(ddtr, (0, 3, 1, 2))).reshape(shape3[0], shape3[1], SSD_HEADS))
        dalog.append((dar.sum(0)[:, 0, :] + dac.sum(0)[:, :, 0]).reshape(SSD_HEADS))
    grads["ssd_a_log"][j] = jnp.stack(dalog)
    dxbc_c = jnp.concatenate([dx_sum, dbm_sum, dcm_sum], axis=-1)
    (ddtp,), _, (ddt_bias,) = tok_bwd(f"l{l}_dt_bwd", f_dt, [r["dtp"]], None, [r["dt_bias"]], [jnp.concatenate(ddts, axis=-1)], [F32])
    grads["ssd_dt_bias"][j] = ddt_bias.reshape(2, SSD_HEADS)
    dxbc, grads["ssd_conv_w"][j], dcb = conv_bwd(f"l{l}_ssd_conv_bwd", r["xbc"], p["ssd_conv_w"][j], p["ssd_conv_b"][j][None], dxbc_c, t1, "silu")
    du, grads["lru_conv_w"][j], dlb = conv_bwd(f"l{l}_lru_conv_bwd", r["u"], p["lru_conv_w"][j], p["lru_conv_b"][j][None], du_c, t1, "none")
    grads["ssd_conv_b"][j], grads["lru_conv_b"][j] = dcb[0], dlb[0]
    dproj = {"z": dz, "xbc": dxbc, "dt": _pad_to_lanes(_rows2d(ddtp)).reshape(shape3[:2] + (LANES,)), "gy": dgy, "u": du}
    grads["ev_w_in"][j] = jnp.concatenate(
        [_wgrad(r["hn"], dproj[n], f"l{l}_dwin_{n}")[:, :b - a] for n, (a, b) in EV_COLS.items()], axis=1)
    return _dgrad([(dproj[n], lw["in"][n][1]) for n in EV_COLS], f"l{l}_dhn", shape3)


def _odd_mixer_fwd(l, hn, p, lw, lb_row, s_ctx):
    j = l // 2
    r = {"hn": hn}
    proj = {n: _mm3(hn, lw["in"][n][0], f"l{l}_proj_{n}") for n in OD_COLS}
    r["v"], r["g"], r["u"] = proj["v"], proj["g"], proj["u"]
    r["prep_in"] = [proj["q"], proj["ff"], proj["fb"]]
    r["lb"] = lb_row
    r["prep"] = tok_fwd(f"l{l}_odd_prep", f_odd_prep, r["prep_in"], None, [lb_row], [HG_W] * 5, [F32] * 5)
    qs = r["prep"][0]
    r["os"], r["hst"] = [], []
    for d in (0, 1):
        o, hst = hg_fwd(f"l{l}_hg_fwd{d}", qs, r["prep"][1 + 2 * d], r["prep"][2 + 2 * d], r["v"], bool(d), s_ctx)
        r["os"].append(o)
        r["hst"].append(hst)
    r["s5_ins"], r["s5_p"] = _s5_prepare(p, j, f"l{l}_s5")
    r["s5y"] = s5_fwd(f"l{l}_s5_fwd", r["u"], *r["s5_p"], s_ctx)
    r["fin_p"] = [p["hg_norm_g"][j].reshape(1, HG_W), p["s5_d"][j][None], p["s5_glu_w"][j], p["s5_glu_b"][j][None]]
    r["fin_in"] = [r["os"][0], r["os"][1], r["g"], r["s5y"], r["u"]]
    r["o"] = tok_fwd(f"l{l}_odd_finish", f_odd_finish, r["fin_in"], None, r["fin_p"], [HG_W, S5_W], [BF16, BF16])
    return r


def _odd_mixer_bwd(l, r, dox, p, lw, s_ctx, grads):
    j = l // 2
    shape3 = dox.shape
    grads["od_w_out"][j] = jnp.concatenate([_wgrad(o, dox, f"l{l}_dwout{i}") for i, o in enumerate(r["o"])], axis=0)
    do = [_dgrad([(dox, lw["out"][i][1])], f"l{l}_dout{i}", shape3) for i in range(2)]
    (do_hg, _, dg, ds5y, du_fin), _, (dng, grads["s5_d"][j], grads["s5_glu_w"][j], dglu_b) = tok_bwd(
        f"l{l}_odd_finish_bwd", f_odd_finish, r["fin_in"], None, r["fin_p"], do, [F32] * 5)
    grads["hg_norm_g"][j] = dng.reshape(HG_HEADS, HG_DK)
    grads["s5_d"][j], grads["s5_glu_b"][j] = grads["s5_d"][j][0], dglu_b[0]
    s5g = s5_bwd(f"l{l}_s5_bwd", r["u"], *r["s5_p"], ds5y, s_ctx)
    du = s5g[0] + du_fin
    (grads["s5_lam_re"][j], grads["s5_lam_im"][j], grads["s5_log_step"][j], grads["s5_b_re"][j], grads["s5_b_im"][j],
     grads["s5_c_re"][j], grads["s5_c_im"][j]) = _s5_param_grads(r["s5_ins"], s5g[1:], f"l{l}_s5")
    qs = r["prep"][0]
    dqs, dv, dprep = 0.0, 0.0, [None] * 5
    for d in (0, 1):
        dq, dk, dlf, dvd = hg_bwd(f"l{l}_hg_bwd{d}", qs, r["prep"][1 + 2 * d], r["prep"][2 + 2 * d], r["v"], r["hst"][d], do_hg, bool(d), s_ctx)
        dqs, dv = dqs + dq, dv + dvd
        dprep[1 + 2 * d], dprep[2 + 2 * d] = dk, dlf
    dprep[0] = dqs
    (dq_, dff, dfb), _, (dlb,) = tok_bwd(f"l{l}_odd_prep_bwd", f_odd_prep, r["prep_in"], None, [r["lb"]], dprep, [F32] * 3)
    dproj = {"q": dq_, "ff": dff, "fb": dfb, "v": dv, "g": dg, "u": du}
    grads["od_w_in"][j] = jnp.concatenate([_wgrad(r["hn"], dproj[n], f"l{l}_dwin_{n}") for n in OD_COLS], axis=1)
    return _dgrad([(dproj[n], lw["in"][n][1]) for n in OD_COLS], f"l{l}_dhn", shape3), dlb


def _ffn_fwd(l, fn, p, lw, s_ctx):
    r = {"fn": fn}
    tg = taps_grid(s_ctx, fn.shape[1], GRID_W)
    r["a"] = _mm3(fn, lw["gate"][0], f"l{l}_ffn_gate")
    r["up"] = _mm3(fn, lw["up"][0], f"l{l}_ffn_up")
    r["cw"], r["cb"] = p["ffn_conv_w"][l].reshape(9, D_FF), p["ffn_conv_b"][l][None]
    r["act"] = conv_fwd(f"l{l}_ffn_conv", r["a"], r["cw"], r["cb"], tg, "silu_mul", mul=r["up"], out_dtype=BF16)
    return r, _mm3(r["act"], lw["down"][0], f"l{l}_ffn_down")


def _ffn_bwd(l, r, dfo, lw, s_ctx, grads):
    shape3 = dfo.shape
    tg = taps_grid(s_ctx, shape3[1], GRID_W)
    grads["ffn_w_down"][l] = _wgrad(r["act"], dfo, f"l{l}_dwdown")
    dact = _dgrad([(dfo, lw["down"][1])], f"l{l}_dact", shape3)
    da, dcw, dcb, dup = conv_bwd(f"l{l}_ffn_conv_bwd", r["a"], r["cw"], r["cb"], dact, tg, "silu_mul", mul=r["up"])
    grads["ffn_conv_w"][l], grads["ffn_conv_b"][l] = dcw.reshape(3, 3, D_FF), dcb[0]
    grads["ffn_w_gate"][l] = _wgrad(r["fn"], da, f"l{l}_dwgate")
    grads["ffn_w_up"][l] = _wgrad(r["fn"], dup, f"l{l}_dwup")
    return _dgrad([(da, lw["gate"][1]), (dup, lw["up"][1])], f"l{l}_dfn", shape3)


BIG_WEIGHTS = ("ev_w_in", "ev_w_out", "od_w_in", "od_w_out", "ffn_w_gate", "ffn_w_up", "ffn_w_down")
PER_LAYER = {"norm_mix_g": DEPTH, "norm_ffn_g": DEPTH, "ffn_w_gate": DEPTH, "ffn_w_up": DEPTH, "ffn_conv_w": DEPTH,
             "ffn_conv_b": DEPTH, "ffn_w_down": DEPTH}


def local_step(x, ctx, target, modtabs, p, big, s_ctx=CTX_LEN):
    d_model = x.shape[-1]
    s0 = jnp.concatenate([ctx, x], axis=1)
    lws = [_layer_weights(l, big) for l in range(DEPTH)]
    shapes = {n: v.shape for n, v in {**p, **big}.items()}
    grads = {n: [None] * PER_LAYER.get(n, DEPTH // 2) for n in shapes if n not in ("c_ctx", "w_mod", "b_mod", "final_norm_g", "hg_lb_logits")}
    (lbs,) = small_fwd("lower_bounds", f_lower_bounds, [p["hg_lb_logits"]], [p["hg_lb_logits"].shape])
    tab_a = [modtabs[0]] + [modtabs[l].at[:, N_MOD - 1].set(modtabs[l - 1][:, N_MOD - 1]) for l in range(1, DEPTH)]
    res = []
    s, br = s0, None
    for l in range(DEPTH):
        r = {}
        g_mix, g_ffn = p["norm_mix_g"][l][None], p["norm_ffn_g"][l][None]
        if l == 0:
            (hn,) = tok_fwd("l0_norm", f_norm(0, 1), [s], tab_a[0], [g_mix], [d_model], [BF16])
            r["a_in"] = [s]
        else:
            r["a_in"] = [s, br]
            s, hn = tok_fwd(f"l{l}_resnorm_a", f_resnorm(5, 0, 1), r["a_in"], tab_a[l], [g_mix], [d_model] * 2, [F32, BF16])
        if l % 2 == 0:
            r["mix"] = _even_mixer_fwd(l, hn, p, lws[l], s_ctx)
        else:
            r["mix"] = _odd_mixer_fwd(l, hn, p, lws[l], lbs[l:l + 1], s_ctx)
        ox = mm([(_rows2d(o), w) for o, (w, _) in zip(r["mix"]["o"], lws[l]["out"])], f"l{l}_mix_out").reshape(s.shape)
        r["b_in"] = [s, ox]
        s, fn = tok_fwd(f"l{l}_resnorm_b", f_resnorm(2, 3, 4), r["b_in"], modtabs[l], [g_ffn], [d_model] * 2, [F32, BF16])
        r["ffn"], br = _ffn_fwd(l, fn, p, lws[l], s_ctx)
        res.append(r)

    loss_blk, ds, dbr, dtab_f, dfinal_g = final_loss("final_loss", s, br, modtabs[DEPTH - 1], p["final_norm_g"][None], target, s_ctx)
    grads["final_norm_g"] = dfinal_g[0]
    dmod = [None] * DEPTH
    dtab_next = dtab_f
    dlb = jnp.zeros_like(lbs)
    for l in reversed(range(DEPTH)):
        r = res[l]
        g_mix, g_ffn = p["norm_mix_g"][l][None], p["norm_ffn_g"][l][None]
        dfn = _ffn_bwd(l, r["ffn"], dbr, lws[l], s_ctx, grads)
        (ds, dox), dtab_b, (grads["norm_ffn_g"][l],) = tok_bwd(
            f"l{l}_resnorm_b_bwd", f_resnorm(2, 3, 4), r["b_in"], modtabs[l], [g_ffn], [ds, dfn], [F32, F32])
        if l % 2 == 0:
            dhn = _even_mixer_bwd(l, r["mix"], dox, p, lws[l], s_ctx, grads)
        else:
            dhn, dlb_l = _odd_mixer_bwd(l, r["mix"], dox, p, lws[l], s_ctx, grads)
            dlb = dlb.at[l:l + 1].set(dlb_l)
        if l == 0:
            (ds,), dtab_a, (dg,) = tok_bwd("l0_norm_bwd", f_norm_keep(0, 1), r["a_in"], tab_a[0], [g_mix], [ds, dhn], [F32])
        else:
            (ds, dbr), dtab_a, (dg,) = tok_bwd(
                f"l{l}_resnorm_a_bwd", f_resnorm(5, 0, 1), r["a_in"], tab_a[l], [g_mix], [ds, dhn], [F32, F32])
        grads["norm_mix_g"][l] = dg
        dmod[l] = (dtab_a.at[:, N_MOD - 1].set(0.0) + dtab_b).at[:, N_MOD - 1].set(dtab_next[:, N_MOD - 1])
        dtab_next = dtab_a
    (grads["hg_lb_logits"],) = small_bwd("lower_bounds_bwd", f_lower_bounds, [p["hg_lb_logits"]], [dlb])
    out = {}
    for n, g in grads.items():
        if isinstance(g, list):
            g = jnp.stack([t.reshape(shapes[n][1:]) for t in g])
        out[n] = g.reshape(shapes[n])
    return loss_blk[:, 0, 0], ds[:, s_ctx:], dmod, out


WEIGHT_NAMES = (
    "c_ctx", "w_mod", "b_mod", "norm_mix_g", "norm_ffn_g", "final_norm_g", "ev_w_in", "ev_w_out", "ssd_conv_w",
    "ssd_conv_b", "ssd_dt_bias", "ssd_a_log", "ssd_d", "ssd_norm_g", "lru_conv_w", "lru_conv_b", "lru_w_a", "lru_b_a",
    "lru_w_i", "lru_b_i", "lru_lam", "od_w_in", "od_w_out", "hg_lb_logits", "hg_norm_g", "s5_lam_re", "s5_lam_im",
    "s5_log_step", "s5_b_re", "s5_b_im", "s5_c_re", "s5_c_im", "s5_d", "s5_glu_w", "s5_glu_b", "ffn_w_gate", "ffn_w_up",
    "ffn_conv_w", "ffn_conv_b", "ffn_w_down")
INPUT_NAMES = ("x", "c", "ctx") + WEIGHT_NAMES + ("loss_target",) + tuple("m_" + n for n in WEIGHT_NAMES) + tuple("v_" + n for n in WEIGHT_NAMES)
SHARD_AXIS = {"w_mod": 2, "ev_w_in": 2, "ev_w_out": 1, "ssd_conv_w": 2, "lru_conv_w": 2, "lru_b_a": 2, "lru_b_i": 2,
              "lru_lam": 2, "od_w_in": 2, "od_w_out": 1, "s5_d": 1, "s5_glu_w": 1, "s5_glu_b": 1, "ffn_w_gate": 2,
              "ffn_w_up": 2, "ffn_conv_w": 3, "ffn_w_down": 1}
SMALL_SHARDED = tuple(n for n in WEIGHT_NAMES if n in SHARD_AXIS and n not in BIG_WEIGHTS and n != "w_mod")
REPLICATED_LOCAL = tuple(n for n in WEIGHT_NAMES if n not in SHARD_AXIS and n not in ("c_ctx", "b_mod"))
PACK_WIDTH = 1024
MOD_ROWS = 48
CTX_ROW = 32


def _unshard(g8, axis):
    moved = jnp.moveaxis(g8, 0, axis)
    shp = moved.shape
    return moved.reshape(shp[:axis] + (shp[axis] * shp[axis + 1],) + shp[axis + 2:])


def _to_shards(full, axis):
    shp = full.shape
    return jnp.moveaxis(full.reshape(shp[:axis] + (N_DEV, shp[axis] // N_DEV) + shp[axis + 1:]), axis, 0)


def _pack(arrs, dtype, lead=()):
    flat = jnp.concatenate([a.astype(dtype).reshape(lead + (-1,)) for a in arrs], axis=-1)
    n = flat.shape[-1]
    unit = 16 * PACK_WIDTH
    padded = -(-n // unit) * unit
    flat = jnp.pad(flat, [(0, 0)] * len(lead) + [(0, padded - n)])
    return flat.reshape(lead + (padded // PACK_WIDTH, PACK_WIDTH))


def _unpack(packed, shapes, lead=()):
    flat = packed.reshape(lead + (-1,))
    out, off = [], 0
    for shp in shapes:
        n = math.prod(shp)
        out.append(flat[..., off:off + n].reshape(lead + tuple(shp)))
        off += n
    return out


def _my_block(full, axis, me):
    loc = full.shape[axis] // N_DEV
    return lax.dynamic_slice_in_dim(full, me * loc, loc, axis)


def kernel(*args):
    a = dict(zip(INPUT_NAMES, args))
    px, py, pc = _my_pos()
    me = 4 * px + 2 * py + pc
    nb = a["x"].shape[0]

    big8 = all_gather(_pack([a[n] for n in BIG_WEIGHTS], BF16), "gather_big_weights")
    big = {n: _unshard(g, SHARD_AXIS[n]) for n, g in zip(
        BIG_WEIGHTS, _unpack(big8, [a[n].shape for n in BIG_WEIGHTS], (N_DEV,)))}
    small_names = ("c",) + SMALL_SHARDED
    small8 = all_gather(_pack([a[n] for n in small_names], F32), "gather_small")
    small = dict(zip(small_names, _unpack(small8, [a[n].shape for n in small_names], (N_DEV,))))
    p = {n: a[n] for n in WEIGHT_NAMES if n not in SHARD_AXIS}
    for n in SMALL_SHARDED:
        p[n] = _unshard(small[n], SHARD_AXIS[n])
    c_all = small["c"].reshape(N_DEV * nb, D_MODEL)

    rows = jnp.concatenate([c_all, a["c_ctx"][None], jnp.zeros((MOD_ROWS - CTX_ROW - 1, D_MODEL), F32)], axis=0)
    (srows,) = small_fwd("mod_silu", f_silu, [rows], [rows.shape])
    wmod2d = jnp.transpose(a["w_mod"], (1, 0, 2)).reshape(D_MODEL, -1).astype(BF16)
    cols = a["w_mod"].shape[2]
    mod_loc = mm([(srows, wmod2d)], "mod_proj")
    mod8 = all_gather(mod_loc, "gather_mod").reshape(N_DEV, MOD_ROWS, DEPTH, cols)
    mod_all = jnp.transpose(mod8, (2, 1, 0, 3)).reshape(DEPTH, MOD_ROWS, N_DEV * cols) + a["b_mod"][:, None, :]
    modtabs = []
    for l in range(DEPTH):
        mine = lax.dynamic_slice_in_dim(mod_all[l], me * nb, nb, 0).reshape(nb, N_MOD, D_MODEL)
        ctx_row = jnp.broadcast_to(mod_all[l, CTX_ROW].reshape(1, N_MOD, D_MODEL), (nb, N_MOD, D_MODEL))
        modtabs.append(jnp.stack([ctx_row, mine], axis=1).reshape(2 * nb, N_MOD, D_MODEL))

    loss_b, grad_x, dmod, grads = local_step(a["x"], a["ctx"], a["loss_target"], modtabs, p, big)
    loss = lax.psum(jnp.sum(loss_b), ("x", "y", "c"))

    dm = jnp.stack([t.reshape(nb, 2, N_MOD * D_MODEL) for t in dmod])
    dloc = jnp.concatenate([dm[:, :, 1], jnp.sum(dm[:, :, 0], axis=1, keepdims=True),
                            jnp.zeros((DEPTH, SUBLANES - nb - 1, N_MOD * D_MODEL), F32)], axis=1)
    d8 = all_gather(dloc.reshape(DEPTH * SUBLANES, -1), "gather_dmod").reshape(N_DEV, DEPTH, SUBLANES, -1)
    d_rows = jnp.transpose(d8[:, :, :nb], (1, 0, 2, 3)).reshape(DEPTH, N_DEV * nb, -1)
    d_ctx = jnp.sum(d8[:, :, nb], axis=0)[:, None]
    d_full = jnp.concatenate([d_rows, d_ctx, jnp.zeros((DEPTH, MOD_ROWS - CTX_ROW - 1, N_MOD * D_MODEL), F32)], axis=1)
    grads["b_mod"] = jnp.sum(d_full, axis=1)
    d_cols = jnp.transpose(_my_block(d_full, 2, me), (1, 0, 2)).reshape(MOD_ROWS, DEPTH * cols)
    g_wmod = mm([(srows.T, d_cols)], "mod_dw")
    g_wmod_local = jnp.transpose(g_wmod.reshape(D_MODEL, DEPTH, cols), (1, 0, 2))
    d_srows_part = mm([(d_cols[CTX_ROW:CTX_ROW + SUBLANES], wmod2d.T)], "mod_dctx")[0]

    reduce_names = REPLICATED_LOCAL + SMALL_SHARDED
    part8 = all_gather(_pack([d_srows_part] + [grads[n] for n in reduce_names], F32), "gather_small_grads")
    total = sum_slots(part8, "sum_small_grads")
    totals = _unpack(total, [(D_MODEL,)] + [grads[n].shape for n in reduce_names])
    d_srows = jnp.zeros_like(rows).at[CTX_ROW].set(totals[0])
    (d_rows_in,) = small_bwd("mod_silu_bwd", f_silu, [rows], [d_srows])
    g_local = {"c_ctx": d_rows_in[CTX_ROW], "b_mod": grads["b_mod"], "w_mod": g_wmod_local}
    for n, t in zip(reduce_names, totals[1:]):
        g_local[n] = _my_block(t, SHARD_AXIS[n], me) if n in SHARD_AXIS else t

    sent = _pack([_to_shards(grads[n], SHARD_AXIS[n]) for n in BIG_WEIGHTS], BF16, (N_DEV,))
    got = all_to_all(sent.reshape(N_DEV, -1, PACK_WIDTH), "exchange_big_grads")
    summed = sum_slots(got, "sum_big_grads")
    for n, t in zip(BIG_WEIGHTS, _unpack(summed, [a[n].shape for n in BIG_WEIGHTS])):
        g_local[n] = t

    deltas, new_m, new_v = [], [], []
    for n in WEIGHT_NAMES:
        d, m, v = adamw("adamw_" + n, a[n], g_local[n], a["m_" + n], a["v_" + n])
        deltas.append(d)
        new_m.append(m)
        new_v.append(v)
    return (loss, grad_x, *[g_local[n] for n in WEIGHT_NAMES], *deltas, *new_m, *new_v)
```

```python
import functools
import math

import jax
import jax.numpy as jnp
from jax import lax
from jax.experimental import pallas as pl
from jax.experimental.pallas import tpu as pltpu

F32 = jnp.float32
BF16 = jnp.bfloat16

D_MODEL = 1024
DEPTH = 4
CTX_LEN = 256
SEQ = 2048
S_TOT = CTX_LEN + SEQ
GRID_W = 64
N_MOD = 6
RMS_EPS = 1e-6
N_DEV = 8

SSD_HEADS = 16
SSD_HEAD_DIM = 64
SSD_GROUPS = 2
SSD_HPG = 8
SSD_STATE = 128
SSD_CHUNK = 128
LRU_BLOCKS = 8
LRU_BLOCK_W = 128
LRU_C = 8.0
HG_W = 768
HG_HEADS = 6
HG_DK = 128
HG_CHUNK = 32
S5_W = 256
S5_GROUPS = 16
S5_GROUP_CH = 16
S5_STATE = 64
D_FF = 2816

ADAM_LR = 0.001
ADAM_B1 = 0.9
ADAM_B2 = 0.999
ADAM_EPS = 1e-08
ADAM_WD = 0.01
ADAM_STEP = 10

TOK_BLOCK = CTX_LEN
SUBLANES = 8
VMEM_LIMIT_BYTES = 56 * 1024 * 1024
MM_A_BLOCK_BYTES = 8 * 1024 * 1024


def _cparams(sem=None):
    kw = dict(vmem_limit_bytes=VMEM_LIMIT_BYTES)
    if sem is not None:
        kw["dimension_semantics"] = sem
    return pltpu.CompilerParams(**kw)


def _pick(n, cands):
    for c in cands:
        if n % c == 0:
            return c
    return n


def mm(pairs, name, out_dtype=F32):
    m = pairs[0][0].shape[0]
    n = pairs[0][1].shape[1]
    kdims = [a.shape[1] for a, _ in pairs]
    tn = _pick(n, (512, 384, 256, 128))
    ktile = 1024 if (len(pairs) == 1 and kdims[0] > 4096 and kdims[0] % 1024 == 0) else None
    nk = kdims[0] // ktile if ktile else 1
    row_bytes = sum((ktile or a.shape[1]) * a.dtype.itemsize for a, _ in pairs)
    tm = _pick(m, tuple(c for c in (1024, 512, 256, 128, 64, 48, 40, 32, 16, 8) if c * row_bytes <= MM_A_BLOCK_BYTES))
    npairs = len(pairs)
    if nk > 1:
        assert out_dtype == F32

    def body(*refs):
        o_ref = refs[2 * npairs]
        acc = None
        for i in range(npairs):
            a = refs[2 * i][...].astype(BF16)
            w = refs[2 * i + 1][...].astype(BF16)
            p = jnp.dot(a, w, preferred_element_type=F32)
            acc = p if acc is None else acc + p
        if nk == 1:
            o_ref[...] = acc.astype(out_dtype)
        else:
            k = pl.program_id(2)

            @pl.when(k == 0)
            def _():
                o_ref[...] = acc

            @pl.when(k > 0)
            def _():
                o_ref[...] += acc

    in_specs = []
    args = []
    for a, w in pairs:
        kk = a.shape[1]
        assert w.shape == (kk, n) and a.shape[0] == m, (a.shape, w.shape)
        tk = ktile if ktile else kk
        in_specs.append(pl.BlockSpec((tm, tk), lambda i, j, k: (i, k)))
        in_specs.append(pl.BlockSpec((tk, tn), lambda i, j, k: (k, j)))
        args += [a, w]
    return pl.pallas_call(
        body,
        name=name,
        grid=(m // tm, n // tn, nk),
        in_specs=in_specs,
        out_specs=pl.BlockSpec((tm, tn), lambda i, j, k: (i, j)),
        out_shape=jax.ShapeDtypeStruct((m, n), out_dtype),
        compiler_params=_cparams(("parallel", "parallel", "arbitrary")),
    )(*args)


def _mod_index(b, t):
    return (2 * b + jnp.minimum(t, 1), 0, 0)


def tok_fwd(name, f, toks, mod, params, out_widths, out_dtypes):
    nb, s, _ = toks[0].shape
    nt, nm, npar = len(toks), int(mod is not None), len(params)

    def body(*refs):
        ins, outs = refs[: nt + nm + npar], refs[nt + nm + npar:]
        tv = [r[...].astype(F32) for r in ins[:nt]]
        mv = [ins[nt][k:k + 1, :] for k in range(N_MOD)] if nm else None
        pv = [r[...] for r in ins[nt + nm:]]
        for o, r in zip(outs, f(tv, mv, pv)):
            o[...] = r.astype(o.dtype)

    in_specs = [pl.BlockSpec((None, TOK_BLOCK, t.shape[2]), lambda b, t: (b, t, 0)) for t in toks]
    if nm:
        in_specs.append(pl.BlockSpec((None, N_MOD, mod.shape[2]), _mod_index))
    in_specs += [pl.BlockSpec(p.shape, lambda b, t, nd=p.ndim: (0,) * nd) for p in params]
    return pl.pallas_call(
        body,
        name=name,
        grid=(nb, s // TOK_BLOCK),
        in_specs=in_specs,
        out_specs=[pl.BlockSpec((None, TOK_BLOCK, w), lambda b, t: (b, t, 0)) for w in out_widths],
        out_shape=[jax.ShapeDtypeStruct((nb, s, w), dt) for w, dt in zip(out_widths, out_dtypes)],
        compiler_params=_cparams(("parallel", "parallel")),
    )(*toks, *([mod] if nm else []), *params)


def tok_bwd(name, f, toks, mod, params, cots, dtok_dtypes):
    nb, s, _ = toks[0].shape
    nt, nm, npar, nc = len(toks), int(mod is not None), len(params), len(cots)

    def body(*refs):
        n_in = nt + nm + npar + nc
        ins, outs = refs[:n_in], refs[n_in:]
        b, t = pl.program_id(0), pl.program_id(1)
        tv = [r[...].astype(F32) for r in ins[:nt]]
        mv = [ins[nt][k:k + 1, :] for k in range(N_MOD)] if nm else None
        pv = [r[...] for r in ins[nt + nm: nt + nm + npar]]
        cv = [r[...].astype(F32) for r in ins[nt + nm + npar:]]
        _, vjp = jax.vjp(f, tv, mv, pv)
        dtv, dmv, dpv = vjp(cv)
        for o, r in zip(outs[:nt], dtv):
            o[...] = r.astype(o.dtype)
        if nm:
            dm_ref = outs[nt]

            @pl.when(t <= 1)
            def _():
                for k in range(N_MOD):
                    dm_ref[k:k + 1, :] = dmv[k]

            @pl.when(t > 1)
            def _():
                for k in range(N_MOD):
                    dm_ref[k:k + 1, :] += dmv[k]

        first = jnp.logical_and(b == 0, t == 0)
        for o, r in zip(outs[nt + nm:], dpv):
            @pl.when(first)
            def _(o=o, r=r):
                o[...] = r

            @pl.when(jnp.logical_not(first))
            def _(o=o, r=r):
                o[...] += r

    tok_spec = lambda w: pl.BlockSpec((None, TOK_BLOCK, w), lambda b, t: (b, t, 0))
    in_specs = [tok_spec(t.shape[2]) for t in toks]
    if nm:
        in_specs.append(pl.BlockSpec((None, N_MOD, mod.shape[2]), _mod_index))
    in_specs += [pl.BlockSpec(p.shape, lambda b, t, nd=p.ndim: (0,) * nd) for p in params]
    in_specs += [tok_spec(c.shape[2]) for c in cots]
    out_specs = [tok_spec(t.shape[2]) for t in toks]
    out_shape = [jax.ShapeDtypeStruct(t.shape, dt) for t, dt in zip(toks, dtok_dtypes)]
    if nm:
        out_specs.append(pl.BlockSpec((None, N_MOD, mod.shape[2]), _mod_index))
        out_shape.append(jax.ShapeDtypeStruct(mod.shape, F32))
    out_specs += [pl.BlockSpec(p.shape, lambda b, t, nd=p.ndim: (0,) * nd) for p in params]
    out_shape += [jax.ShapeDtypeStruct(p.shape, F32) for p in params]
    res = pl.pallas_call(
        body,
        name=name,
        grid=(nb, s // TOK_BLOCK),
        in_specs=in_specs,
        out_specs=out_specs,
        out_shape=out_shape,
        compiler_params=_cparams(("arbitrary", "arbitrary")),
    )(*toks, *([mod] if nm else []), *params, *cots)
    return res[:nt], (res[nt] if nm else None), res[nt + nm:]


def _rms(x, g):
    return x * lax.rsqrt(jnp.mean(x * x, axis=-1, keepdims=True) + RMS_EPS) * g


def _silu(x):
    return x * jax.nn.sigmoid(x)


def f_norm(shift_row, scale_row):
    def f(tv, mv, pv):
        return [_rms(tv[0], pv[0]) * (1.0 + mv[scale_row]) + mv[shift_row]]
    return f


def f_resnorm(gate_row, shift_row, scale_row):
    def f(tv, mv, pv):
        s = tv[0] + mv[gate_row] * tv[1]
        return [s, _rms(s, pv[0]) * (1.0 + mv[scale_row]) + mv[shift_row]]
    return f


_ANY = pl.BlockSpec(memory_space=pl.ANY)
_MESH = pl.DeviceIdType.MESH


def _my_pos():
    return lax.axis_index("x"), lax.axis_index("y"), lax.axis_index("c")


def _slot_of(pos):
    return 4 * pos[0] + 2 * pos[1] + pos[2]


def all_gather(xs, name):
    n = len(xs)

    def body(*refs):
        x_refs, out_refs = refs[:n], refs[n:2 * n]
        send_sems, recv_sems, local_sems = refs[2 * n:]
        px, py, pc = _my_pos()
        me, sibling = (px, py, pc), (px, py, 1 - pc)
        chips = [(1 - px, py), (px, 1 - py), (1 - px, 1 - py)]

        def copy(a, k, block, to, from_input=False):
            slot = out_refs[a].at[_slot_of(block)]
            return pltpu.make_async_remote_copy(
                src_ref=x_refs[a] if from_input else slot, dst_ref=slot,
                send_sem=send_sems.at[a, k], recv_sem=recv_sems.at[a, k],
                device_id=to, device_id_type=_MESH)

        mine = [pltpu.make_async_copy(x_refs[a], out_refs[a].at[_slot_of(me)], local_sems.at[a]) for a in range(n)]
        for cp in mine:
            cp.start()
        first = [copy(a, 0, me, sibling, True) for a in range(n)]
        first += [copy(a, 1 + j, me, (*chip, pc), True) for j, chip in enumerate(chips) for a in range(n)]
        for cp in first:
            cp.start()
        passed = []
        for j, chip in enumerate(chips):
            for a in range(n):
                copy(a, 1 + j, (*chip, pc), me).wait_recv()
                passed.append(copy(a, 4 + j, (*chip, pc), sibling))
                passed[-1].start()
        for a in range(n):
            copy(a, 0, sibling, me).wait_recv()
        for j, chip in enumerate(chips):
            for a in range(n):
                copy(a, 4 + j, (*chip, 1 - pc), me).wait_recv()
        for cp in first + passed:
            cp.wait_send()
        for cp in mine:
            cp.wait()

    return pl.pallas_call(
        body,
        name=name,
        out_shape=[jax.ShapeDtypeStruct((N_DEV,) + x.shape, x.dtype) for x in xs],
        in_specs=[_ANY] * n,
        out_specs=[_ANY] * n,
        scratch_shapes=[pltpu.SemaphoreType.DMA((n, 7)), pltpu.SemaphoreType.DMA((n, 7)), pltpu.SemaphoreType.DMA((n,))],
    )(*xs)


def all_to_all(xs, name):
    n = len(xs)

    def body(*refs):
        x_refs, out_refs = refs[:n], refs[n:2 * n]
        send_sems, recv_sems, local_sems = refs[2 * n:]
        px, py, pc = _my_pos()
        me = (px, py, pc)

        def flipped(k):
            kx, ky, kc = (k >> 2) & 1, (k >> 1) & 1, k & 1
            return (1 - px if kx else px, 1 - py if ky else py, 1 - pc if kc else pc)

        def copy(a, k):
            peer = flipped(k)
            return pltpu.make_async_remote_copy(
                src_ref=x_refs[a].at[_slot_of(peer)], dst_ref=out_refs[a].at[_slot_of(me)],
                send_sem=send_sems.at[a, k - 1], recv_sem=recv_sems.at[a, k - 1],
                device_id=peer, device_id_type=_MESH)

        def landing(a, k):
            peer = flipped(k)
            return pltpu.make_async_remote_copy(
                src_ref=x_refs[a].at[_slot_of(me)], dst_ref=out_refs[a].at[_slot_of(peer)],
                send_sem=send_sems.at[a, k - 1], recv_sem=recv_sems.at[a, k - 1],
                device_id=peer, device_id_type=_MESH)

        mine = [pltpu.make_async_copy(x_refs[a].at[_slot_of(me)], out_refs[a].at[_slot_of(me)], local_sems.at[a]) for a in range(n)]
        for cp in mine:
            cp.start()
        copies = [copy(a, k) for a in range(n) for k in range(1, N_DEV)]
        for cp in copies:
            cp.start()
        for a in range(n):
            for k in range(1, N_DEV):
                landing(a, k).wait_recv()
        for cp in copies:
            cp.wait_send()
        for cp in mine:
            cp.wait()

    return pl.pallas_call(
        body,
        name=name,
        out_shape=[jax.ShapeDtypeStruct(x.shape, x.dtype) for x in xs],
        in_specs=[_ANY] * n,
        out_specs=[_ANY] * n,
        scratch_shapes=[pltpu.SemaphoreType.DMA((n, 7)), pltpu.SemaphoreType.DMA((n, 7)), pltpu.SemaphoreType.DMA((n,))],
    )(*xs)


def sum_slots(x, name):
    n, r, c = x.shape
    tr = _pick(r, (512, 256, 128, 64, 32, 16, 8))

    def body(x_ref, o_ref):
        acc = x_ref[0].astype(F32)
        for i in range(1, n):
            acc = acc + x_ref[i].astype(F32)
        o_ref[...] = acc

    return pl.pallas_call(
        body,
        name=name,
        grid=(r // tr,),
        in_specs=[pl.BlockSpec((n, tr, c), lambda i: (0, i, 0))],
        out_specs=pl.BlockSpec((tr, c), lambda i: (i, 0)),
        out_shape=jax.ShapeDtypeStruct((r, c), F32),
        compiler_params=_cparams(("parallel",)),
    )(x)


CONV_CH_TILE = 256


def _shift_rows(x, off):
    n = x.shape[0]
    if off % n == 0:
        return x
    return pltpu.roll(x, (-off) % n, axis=0)


def _between(v, lo, hi):
    return jnp.where(v >= lo, 1.0, 0.0) * jnp.where(v < hi, 1.0, 0.0)


def taps_1d(ntaps, s_ctx, s_tot):
    def mask(off):
        def m(t):
            is_ctx = _between(t, 0, s_ctx)
            return is_ctx * _between(t + off, 0, s_ctx) + (1.0 - is_ctx) * _between(t + off, s_ctx, s_tot)
        return m
    return [(j - (ntaps - 1) // 2, mask(j - (ntaps - 1) // 2)) for j in range(ntaps)]


def taps_grid(s_ctx, s_tot, grid_w):
    assert s_ctx % grid_w == 0

    def mask(dr, dc):
        def m(t):
            is_ctx = _between(t, 0, s_ctx)
            lat = _between(t % grid_w + dc, 0, grid_w) * _between(t + grid_w * dr, s_ctx, s_tot)
            ctx = _between(t + dc, 0, s_ctx) if dr == 0 else 0.0
            return is_ctx * ctx + (1.0 - is_ctx) * lat
        return m
    return [(grid_w * dr + dc, mask(dr, dc)) for dr in (-1, 0, 1) for dc in (-1, 0, 1)]


def _conv_acc(x, w_ref, b_ref, taps, masks):
    acc = jnp.broadcast_to(b_ref[...], x.shape)
    for k, (off, _) in enumerate(taps):
        acc = acc + w_ref[k:k + 1, :] * (_shift_rows(x, off) * masks[k])
    return acc


def _tap_masks(taps, s):
    t = lax.broadcasted_iota(jnp.int32, (s, 1), 0)
    return [m(t) for _, m in taps]


def conv_fwd(name, x, w, b, taps, mode, mul=None, out_dtype=F32):
    nb, s, c = x.shape
    ct = _pick(c, (CONV_CH_TILE, 128))
    has_mul = mode == "silu_mul"

    def body(*refs):
        x_ref, w_ref, b_ref = refs[:3]
        o_ref = refs[-1]
        acc = _conv_acc(x_ref[...], w_ref, b_ref, taps, _tap_masks(taps, s))
        if mode == "none":
            out = acc
        else:
            out = _silu(acc)
            if has_mul:
                out = out * refs[3][...].astype(F32)
        o_ref[...] = out.astype(o_ref.dtype)

    blk = pl.BlockSpec((None, s, ct), lambda bb, j: (bb, 0, j))
    par = lambda k: pl.BlockSpec((k, ct), lambda bb, j: (0, j))
    return pl.pallas_call(
        body,
        name=name,
        grid=(nb, c // ct),
        in_specs=[blk, par(w.shape[0]), par(1)] + ([blk] if has_mul else []),
        out_specs=blk,
        out_shape=jax.ShapeDtypeStruct(x.shape, out_dtype),
        compiler_params=_cparams(("parallel", "parallel")),
    )(x, w, b, *([mul] if has_mul else []))


def conv_bwd(name, x, w, b, dout, taps, mode, mul=None, dx_dtype=F32):
    nb, s, c = x.shape
    ct = _pick(c, (CONV_CH_TILE, 128))
    has_mul = mode == "silu_mul"
    nk = w.shape[0]

    def body(*refs):
        x_ref, w_ref, b_ref, do_ref = refs[:4]
        n_in = 5 if has_mul else 4
        dx_ref, dw_ref, db_ref = refs[n_in:n_in + 3]
        bb = pl.program_id(1)
        masks = _tap_masks(taps, s)
        x = x_ref[...]
        dacc = do_ref[...].astype(F32)
        if mode != "none":
            acc = _conv_acc(x, w_ref, b_ref, taps, masks)
            sg = jax.nn.sigmoid(acc)
            if has_mul:
                refs[n_in + 3][...] = (dacc * (acc * sg)).astype(refs[n_in + 3].dtype)
                dacc = dacc * refs[4][...].astype(F32)
            dacc = dacc * (sg * (1.0 + acc * (1.0 - sg)))
        dx = jnp.zeros_like(x)
        dws = []
        for k, (off, _) in enumerate(taps):
            dm = dacc * masks[k]
            dx = dx + _shift_rows(w_ref[k:k + 1, :] * dm, -off)
            dws.append(jnp.sum(dm * _shift_rows(x, off), axis=0, keepdims=True))
        dx_ref[...] = dx.astype(dx_ref.dtype)
        db = jnp.sum(dacc, axis=0, keepdims=True)

        @pl.when(bb == 0)
        def _():
            for k in range(nk):
                dw_ref[k:k + 1, :] = dws[k]
            db_ref[...] = db

        @pl.when(bb > 0)
        def _():
            for k in range(nk):
                dw_ref[k:k + 1, :] += dws[k]
            db_ref[...] += db

    blk = pl.BlockSpec((None, s, ct), lambda j, bb: (bb, 0, j))
    par = lambda k: pl.BlockSpec((k, ct), lambda j, bb: (0, j))
    out_specs = [blk, par(nk), par(1)] + ([blk] if has_mul else [])
    out_shape = [jax.ShapeDtypeStruct(x.shape, dx_dtype), jax.ShapeDtypeStruct(w.shape, F32), jax.ShapeDtypeStruct(b.shape, F32)]
    if has_mul:
        out_shape.append(jax.ShapeDtypeStruct(x.shape, dx_dtype))
    return pl.pallas_call(
        body,
        name=name,
        grid=(c // ct, nb),
        in_specs=[blk, par(nk), par(1), blk] + ([blk] if has_mul else []),
        out_specs=out_specs,
        out_shape=out_shape,
        compiler_params=_cparams(("parallel", "arbitrary")),
    )(x, w, b, dout, *([mul] if has_mul else []))


def _scan_order(direction, adjoint, s_ctx, s_tot):
    nc, nt = s_ctx // SUBLANES, s_tot // SUBLANES
    if direction == 0:
        return ([(0, nt, 1)], False) if not adjoint else ([(nt - 1, nt, -1)], True)
    if not adjoint:
        return [(nc - 1, nc, -1), (nt - 1, nt - nc, -1)], True
    return [(nc, nt - nc, 1), (0, nc, 1)], False


def _last_row(h, descending):
    row = lax.broadcasted_iota(jnp.int32, h.shape, 0)
    pick = 0 if descending else SUBLANES - 1
    return jnp.sum(jnp.where(row == pick, h, 0.0), axis=0, keepdims=True)


def _prev_rows(h, carry, descending):
    row = lax.broadcasted_iota(jnp.int32, h.shape, 0)
    if descending:
        return jnp.where(row == SUBLANES - 1, carry, pltpu.roll(h, SUBLANES - 1, axis=0))
    return jnp.where(row == 0, carry, pltpu.roll(h, 1, axis=0))


def _scan_real(a_ref, x_ref, h_ref, hp_ref, order):
    ranges, descending = order
    width = a_ref.shape[1]
    row = lax.broadcasted_iota(jnp.int32, (SUBLANES, width), 0)

    def tile(i, carry):
        t0 = pl.multiple_of(i * SUBLANES, SUBLANES)
        a = a_ref[pl.ds(t0, SUBLANES), :]
        x = x_ref[pl.ds(t0, SUBLANES), :]
        for k in (1, 2, 4):
            sh = SUBLANES - k if descending else k
            keep = (row < SUBLANES - k) if descending else (row >= k)
            x = jnp.where(keep, a * pltpu.roll(x, sh, axis=0) + x, x)
            a = jnp.where(keep, a * pltpu.roll(a, sh, axis=0), a)
        h = a * carry + x
        if h_ref is not None:
            h_ref[pl.ds(t0, SUBLANES), :] = h
        if hp_ref is not None:
            hp_ref[pl.ds(t0, SUBLANES), :] = _prev_rows(h, carry, descending)
        return _last_row(h, descending)

    carry = jnp.zeros((1, width), F32)
    for first, count, step in ranges:
        carry = lax.fori_loop(0, count, lambda j, c, first=first, step=step: tile(first + step * j, c), carry)


def _cmul(ar, ai, br, bi):
    return ar * br - ai * bi, ar * bi + ai * br


def _scan_cplx(lr, li, xr_ref, xi_ref, hpr_ref, hpi_ref, order):
    ranges, descending = order
    width = xr_ref.shape[1]
    row = lax.broadcasted_iota(jnp.int32, (SUBLANES, width), 0)
    pw = [(lr, li)]
    for _ in range(SUBLANES - 1):
        pw.append(_cmul(pw[-1][0], pw[-1][1], lr, li))
    pr = jnp.zeros((SUBLANES, width), F32)
    pi = jnp.zeros((SUBLANES, width), F32)
    for r in range(SUBLANES):
        n = SUBLANES - 1 - r if descending else r
        pr = jnp.where(row == r, pw[n][0], pr)
        pi = jnp.where(row == r, pw[n][1], pi)

    def tile(i, carry):
        cr, ci = carry
        t0 = pl.multiple_of(i * SUBLANES, SUBLANES)
        xr = xr_ref[pl.ds(t0, SUBLANES), :]
        xi = xi_ref[pl.ds(t0, SUBLANES), :]
        for k in (1, 2, 4):
            sh = SUBLANES - k if descending else k
            keep = (row < SUBLANES - k) if descending else (row >= k)
            sr, si = _cmul(pw[k - 1][0], pw[k - 1][1], pltpu.roll(xr, sh, axis=0), pltpu.roll(xi, sh, axis=0))
            xr = jnp.where(keep, xr + sr, xr)
            xi = jnp.where(keep, xi + si, xi)
        hr, hi = _cmul(pr, pi, cr, ci)
        hr, hi = hr + xr, hi + xi
        xr_ref[pl.ds(t0, SUBLANES), :] = hr
        xi_ref[pl.ds(t0, SUBLANES), :] = hi
        if hpr_ref is not None:
            hpr_ref[pl.ds(t0, SUBLANES), :] = _prev_rows(hr, cr, descending)
            hpi_ref[pl.ds(t0, SUBLANES), :] = _prev_rows(hi, ci, descending)
        return _last_row(hr, descending), _last_row(hi, descending)

    carry = (jnp.zeros((1, width), F32), jnp.zeros((1, width), F32))
    for first, count, step in ranges:
        carry = lax.fori_loop(0, count, lambda j, c, first=first, step=step: tile(first + step * j, c), carry)


def _log1p_pos(y):
    return jnp.where(y < 0.01, y * (1.0 - y * (0.5 - y * (1.0 / 3.0 - 0.25 * y))), jnp.log(1.0 + y))


def _softplus(x):
    return jnp.maximum(x, 0.0) + _log1p_pos(jnp.exp(-jnp.abs(x)))


def _neg_expm1(z):
    series = -z * (1.0 + z * (0.5 + z * (1.0 / 6.0 + z * (1.0 / 24.0 + z * (1.0 / 120.0)))))
    return jnp.where(z > -0.1, series, 1.0 - jnp.exp(z))


def _lru_gates(u, w_a, b_a, w_i, b_i, lam):
    ub = u.astype(BF16)
    r = jax.nn.sigmoid(jnp.dot(ub, w_a.astype(BF16), preferred_element_type=F32) + b_a)
    i = jax.nn.sigmoid(jnp.dot(ub, w_i.astype(BF16), preferred_element_type=F32) + b_i)
    log_a = (-LRU_C) * _softplus(-lam) * r
    return jnp.exp(log_a), jnp.sqrt(_neg_expm1(2.0 * log_a)) * (i * u)


def _lru_specs(nblk, bw, order):
    w = pl.BlockSpec((2, None, bw, bw), lambda *g: (0, order(*g), 0, 0))
    v = pl.BlockSpec((2, None, 1, bw), lambda *g: (0, order(*g), 0, 0))
    return [w, v, w, v, v]


def lru_fwd(name, u, w_a, b_a, w_i, b_i, lam, s_ctx):
    nb, s, _ = u.shape
    nblk, bw = w_a.shape[1], w_a.shape[2]

    def body(u_ref, wa, ba, wi, bi, lm, o_ref, a_s, x_s, h_s):
        u_v = u_ref[...]
        for d in (0, 1):
            a, bx = _lru_gates(u_v, wa[d], ba[d], wi[d], bi[d], lm[d])
            a_s[...] = a
            x_s[...] = bx
            _scan_real(a_s, x_s, h_s, None, _scan_order(d, False, s_ctx, s))
            if d == 0:
                o_ref[...] = h_s[...]
            else:
                o_ref[...] += h_s[...]

    blk = pl.BlockSpec((None, s, bw), lambda b, n: (b, 0, n))
    return pl.pallas_call(
        body,
        name=name,
        grid=(nb, nblk),
        in_specs=[blk] + _lru_specs(nblk, bw, lambda b, n: n),
        out_specs=blk,
        out_shape=jax.ShapeDtypeStruct(u.shape, F32),
        scratch_shapes=[pltpu.VMEM((s, bw), F32)] * 3,
        compiler_params=_cparams(("parallel", "parallel")),
    )(u, w_a, b_a, w_i, b_i, lam)


def lru_bwd(name, u, w_a, b_a, w_i, b_i, lam, dh, s_ctx):
    nb, s, _ = u.shape
    nblk, bw = w_a.shape[1], w_a.shape[2]

    def body(u_ref, wa, ba, wi, bi, lm, dh_ref, du_ref, dwa, dba, dwi, dbi, dlm, a_s, x_s, h_s, hp_s, wp_s):
        b = pl.program_id(1)
        u_v = u_ref[...]
        dh_v = dh_ref[...]
        du = jnp.zeros_like(u_v)
        for d in (0, 1):
            (a, bx), vjp = jax.vjp(_lru_gates, u_v, wa[d], ba[d], wi[d], bi[d], lm[d])
            a_s[...] = a
            x_s[...] = bx
            _scan_real(a_s, x_s, None, hp_s, _scan_order(d, False, s_ctx, s))
            x_s[...] = a * dh_v
            _scan_real(a_s, x_s, None, wp_s, _scan_order(d, True, s_ctx, s))
            g = dh_v + wp_s[...]
            grads = vjp((g * hp_s[...], g))
            du = du + grads[0]
            for ref, val in zip((dwa, dba, dwi, dbi, dlm), grads[1:]):
                @pl.when(b == 0)
                def _(ref=ref, val=val):
                    ref[d] = val

                @pl.when(b > 0)
                def _(ref=ref, val=val):
                    ref[d] += val
        du_ref[...] = du

    blk = pl.BlockSpec((None, s, bw), lambda n, b: (b, 0, n))
    pspecs = _lru_specs(nblk, bw, lambda n, b: n)
    return pl.pallas_call(
        body,
        name=name,
        grid=(nblk, nb),
        in_specs=[blk] + pspecs + [blk],
        out_specs=[blk] + pspecs,
        out_shape=[jax.ShapeDtypeStruct(u.shape, F32)] + [jax.ShapeDtypeStruct(p.shape, F32) for p in (w_a, b_a, w_i, b_i, lam)],
        scratch_shapes=[pltpu.VMEM((s, bw), F32)] * 5,
        compiler_params=_cparams(("parallel", "arbitrary")),
    )(u, w_a, b_a, w_i, b_i, lam, dh)


S5_TILE_CH = 128
S5_TILE_STATES = S5_TILE_CH // S5_GROUP_CH * S5_STATE


def _dot_nt(a, b):
    return lax.dot_general(a, b, (((1,), (1,)), ((), ())), preferred_element_type=F32)


def _dot_tn(a, b):
    return lax.dot_general(a, b, (((0,), (0,)), ((), ())), preferred_element_type=F32)


def _s5_specs(order):
    lam = pl.BlockSpec((2, 1, S5_TILE_STATES), lambda *g: (0, 0, order(*g)))
    mat = pl.BlockSpec((2, None, S5_TILE_STATES, S5_TILE_CH), lambda *g: (0, order(*g), 0, 0))
    return [lam, lam, mat, mat, mat, mat]


def s5_fwd(name, u, lam_r, lam_i, bt_r, bt_i, ct_r, ct_i, s_ctx):
    nb, s, w = u.shape

    def body(u_ref, lr, li, btr, bti, ctr, cti, o_ref, xr_s, xi_s):
        ub = u_ref[...].astype(BF16)
        for d in (0, 1):
            xr_s[...] = _dot_nt(ub, btr[d].astype(BF16))
            xi_s[...] = _dot_nt(ub, bti[d].astype(BF16))
            _scan_cplx(lr[d], li[d], xr_s, xi_s, None, None, _scan_order(d, False, s_ctx, s))
            y = (jnp.dot(xr_s[...].astype(BF16), ctr[d].astype(BF16), preferred_element_type=F32)
                 - jnp.dot(xi_s[...].astype(BF16), cti[d].astype(BF16), preferred_element_type=F32))
            if d == 0:
                o_ref[...] = y
            else:
                o_ref[...] += y

    blk = pl.BlockSpec((None, s, S5_TILE_CH), lambda b, j: (b, 0, j))
    return pl.pallas_call(
        body,
        name=name,
        grid=(nb, w // S5_TILE_CH),
        in_specs=[blk] + _s5_specs(lambda b, j: j),
        out_specs=blk,
        out_shape=jax.ShapeDtypeStruct(u.shape, F32),
        scratch_shapes=[pltpu.VMEM((s, S5_TILE_STATES), F32)] * 2,
        compiler_params=_cparams(("parallel", "parallel")),
    )(u, lam_r, lam_i, bt_r, bt_i, ct_r, ct_i)


def s5_bwd(name, u, lam_r, lam_i, bt_r, bt_i, ct_r, ct_i, dy, s_ctx):
    nb, s, w = u.shape

    def body(u_ref, lr, li, btr, bti, ctr, cti, dy_ref, du_ref, dlr, dli, dbtr, dbti, dctr, dcti,
             hr_s, hi_s, hpr_s, hpi_s, gr_s, gi_s):
        b = pl.program_id(1)
        ub = u_ref[...].astype(BF16)
        dyb = dy_ref[...].astype(BF16)
        du = jnp.zeros((s, S5_TILE_CH), F32)
        for d in (0, 1):
            hr_s[...] = _dot_nt(ub, btr[d].astype(BF16))
            hi_s[...] = _dot_nt(ub, bti[d].astype(BF16))
            _scan_cplx(lr[d], li[d], hr_s, hi_s, hpr_s, hpi_s, _scan_order(d, False, s_ctx, s))
            d_ctr = _dot_tn(hr_s[...].astype(BF16), dyb)
            d_cti = -_dot_tn(hi_s[...].astype(BF16), dyb)
            gr_s[...] = _dot_nt(dyb, ctr[d].astype(BF16))
            gi_s[...] = -_dot_nt(dyb, cti[d].astype(BF16))
            _scan_cplx(lr[d], -li[d], gr_s, gi_s, None, None, _scan_order(d, True, s_ctx, s))
            gr, gi = gr_s[...], gi_s[...]
            hpr, hpi = hpr_s[...], hpi_s[...]
            d_lr = jnp.sum(gr * hpr + gi * hpi, axis=0, keepdims=True)
            d_li = jnp.sum(gi * hpr - gr * hpi, axis=0, keepdims=True)
            grb, gib = gr.astype(BF16), gi.astype(BF16)
            du = du + jnp.dot(grb, btr[d].astype(BF16), preferred_element_type=F32)
            du = du + jnp.dot(gib, bti[d].astype(BF16), preferred_element_type=F32)
            d_btr = _dot_tn(grb, ub)
            d_bti = _dot_tn(gib, ub)
            for ref, val in zip((dlr, dli, dbtr, dbti, dctr, dcti), (d_lr, d_li, d_btr, d_bti, d_ctr, d_cti)):
                @pl.when(b == 0)
                def _(ref=ref, val=val):
                    ref[d] = val

                @pl.when(b > 0)
                def _(ref=ref, val=val):
                    ref[d] += val
        du_ref[...] = du

    blk = pl.BlockSpec((None, s, S5_TILE_CH), lambda j, b: (b, 0, j))
    pspecs = _s5_specs(lambda j, b: j)
    params = (lam_r, lam_i, bt_r, bt_i, ct_r, ct_i)
    return pl.pallas_call(
        body,
        name=name,
        grid=(w // S5_TILE_CH, nb),
        in_specs=[blk] + pspecs + [blk],
        out_specs=[blk] + pspecs,
        out_shape=[jax.ShapeDtypeStruct(u.shape, F32)] + [jax.ShapeDtypeStruct(p.shape, F32) for p in params],
        scratch_shapes=[pltpu.VMEM((s, S5_TILE_STATES), F32)] * 6,
        compiler_params=_cparams(("parallel", "arbitrary")),
    )(u, lam_r, lam_i, bt_r, bt_i, ct_r, ct_i, dy)


def small_fwd(name, f, ins, out_shapes):
    n = len(ins)

    def body(*refs):
        for o, r in zip(refs[n:], f([r[...] for r in refs[:n]])):
            o[...] = r

    return pl.pallas_call(
        body, name=name,
        out_shape=[jax.ShapeDtypeStruct(s, F32) for s in out_shapes],
        compiler_params=_cparams(),
    )(*ins)


def small_bwd(name, f, ins, cots):
    n, nc = len(ins), len(cots)

    def body(*refs):
        _, vjp = jax.vjp(f, [r[...] for r in refs[:n]])
        (grads,) = vjp([r[...] for r in refs[n:n + nc]])
        for o, r in zip(refs[n + nc:], grads):
            o[...] = r

    return pl.pallas_call(
        body, name=name,
        out_shape=[jax.ShapeDtypeStruct(a.shape, F32) for a in ins],
        compiler_params=_cparams(),
    )(*ins, *cots)


def _row(x, r):
    return jnp.sum(jnp.where(lax.broadcasted_iota(jnp.int32, x.shape, 0) == r, x, 0.0), axis=0, keepdims=True)


def _col(x, c):
    return jnp.sum(jnp.where(lax.broadcasted_iota(jnp.int32, x.shape, 1) == c, x, 0.0), axis=1, keepdims=True)


def _chunk_at(i, reverse, ncc, nc):
    if not reverse:
        return i
    return jnp.where(i < ncc, ncc - 1 - i, nc - 1 - (i - ncc))


def _tri(n, reverse):
    li = lax.broadcasted_iota(jnp.int32, (n, n), 0)
    si = lax.broadcasted_iota(jnp.int32, (n, n), 1)
    return jnp.where((li <= si) if reverse else (li >= si), 1.0, 0.0)


_HI = lax.Precision.HIGHEST


def _ssd_chunk(xs, bm, cm, dtc, dtr, a_row, a_col, hs, reverse):
    n = bm.shape[0]
    last = 0 if reverse else n - 1
    tri = _tri(n, reverse)
    cum_c = jnp.dot(tri, dtc * -jnp.exp(a_row), precision=_HI, preferred_element_type=F32)
    cum_r = lax.dot_general(dtr * -jnp.exp(a_col), tri, (((1,), (1,)), ((), ())), precision=_HI, preferred_element_type=F32)
    tot_r = _row(cum_c, last)
    bmb, cmb = bm.astype(BF16), cm.astype(BF16)
    cb = _dot_nt(cmb, bmb)
    ys, hn = [], []
    for hd in range(len(xs)):
        cl = _col(cum_c, hd)
        tot = _col(tot_r, hd)
        decay = jnp.exp(jnp.where(tri > 0.0, cl - _row(cum_r, hd), -jnp.inf))
        xd = xs[hd] * _col(dtc, hd)
        y = jnp.dot((cb * decay).astype(BF16), xd.astype(BF16), preferred_element_type=F32)
        y = y + _dot_nt(cmb, hs[hd].astype(BF16)) * jnp.exp(cl)
        hnew = hs[hd] * jnp.exp(tot) + _dot_tn((xd * jnp.exp(tot - cl)).astype(BF16), bmb)
        ys.append(y)
        hn.append(hnew)
    return ys, hn


def _ssd_specs(reverse, ncc, nc, order):
    ch = lambda *g: _chunk_at(order(*g)[2], reverse, ncc, nc)
    b_ = lambda *g: order(*g)[0]
    g_ = lambda *g: order(*g)[1]
    xw = SSD_HPG * SSD_HEAD_DIM
    nxb = SSD_GROUPS * xw // SSD_STATE
    return [
        pl.BlockSpec((None, SSD_CHUNK, xw), lambda *g: (b_(*g), ch(*g), g_(*g))),
        pl.BlockSpec((None, SSD_CHUNK, SSD_STATE), lambda *g: (b_(*g), ch(*g), nxb + g_(*g))),
        pl.BlockSpec((None, SSD_CHUNK, SSD_STATE), lambda *g: (b_(*g), ch(*g), nxb + SSD_GROUPS + g_(*g))),
        pl.BlockSpec((None, None, SSD_CHUNK, SSD_HPG), lambda *g: (b_(*g), g_(*g), ch(*g), 0)),
        pl.BlockSpec((None, None, SSD_HPG, SSD_CHUNK), lambda *g: (b_(*g), g_(*g), 0, ch(*g))),
        pl.BlockSpec((None, 1, SSD_HPG), lambda *g: (g_(*g), 0, 0)),
        pl.BlockSpec((None, SSD_HPG, 1), lambda *g: (g_(*g), 0, 0)),
    ]


def ssd_fwd(name, xbc, dt_col, dt_row, a_row, a_col, reverse, s_ctx):
    nb, s, _ = xbc.shape
    nc, ncc = s // SSD_CHUNK, s_ctx // SSD_CHUNK
    xw, p = SSD_HPG * SSD_HEAD_DIM, SSD_HEAD_DIM

    def body(x_ref, bm_ref, cm_ref, dtc_ref, dtr_ref, ar_ref, ac_ref, y_ref, hst_ref, h_s):
        i = pl.program_id(2)

        @pl.when(i == 0)
        def _():
            h_s[...] = jnp.zeros_like(h_s)

        hst_ref[...] = h_s[...]
        xs = [x_ref[:, p * hd:p * (hd + 1)] for hd in range(SSD_HPG)]
        hs = [h_s[hd] for hd in range(SSD_HPG)]
        ys, hn = _ssd_chunk(xs, bm_ref[...], cm_ref[...], dtc_ref[...], dtr_ref[...], ar_ref[...], ac_ref[...], hs, reverse)
        for hd in range(SSD_HPG):
            y_ref[:, p * hd:p * (hd + 1)] = ys[hd]
            h_s[hd] = hn[hd]

    order = lambda b, g, i: (b, g, i)
    state = (SSD_HPG, SSD_HEAD_DIM, SSD_STATE)
    return pl.pallas_call(
        body,
        name=name,
        grid=(nb, SSD_GROUPS, nc),
        in_specs=_ssd_specs(reverse, ncc, nc, order),
        out_specs=[pl.BlockSpec((None, SSD_CHUNK, xw), lambda b, g, i: (b, _chunk_at(i, reverse, ncc, nc), g)),
                   pl.BlockSpec((None, None, None) + state, lambda b, g, i: (b, g, i, 0, 0, 0))],
        out_shape=[jax.ShapeDtypeStruct((nb, s, SSD_GROUPS * xw), F32),
                   jax.ShapeDtypeStruct((nb, SSD_GROUPS, nc) + state, F32)],
        scratch_shapes=[pltpu.VMEM(state, F32)],
        compiler_params=_cparams(("parallel", "parallel", "arbitrary")),
    )(xbc, xbc, xbc, dt_col, dt_row, a_row, a_col)


def ssd_bwd(name, xbc, dt_col, dt_row, a_row, a_col, hst, dy, reverse, s_ctx):
    nb, s, _ = xbc.shape
    nc, ncc = s // SSD_CHUNK, s_ctx // SSD_CHUNK
    xw, p = SSD_HPG * SSD_HEAD_DIM, SSD_HEAD_DIM

    def body(x_ref, bm_ref, cm_ref, dtc_ref, dtr_ref, ar_ref, ac_ref, hst_ref, dy_ref,
             dx_ref, dbm_ref, dcm_ref, ddtc_ref, ddtr_ref, dar_ref, dac_ref, dh_s):
        i = pl.program_id(2)

        @pl.when(i == 0)
        def _():
            dh_s[...] = jnp.zeros_like(dh_s)

        xs = [x_ref[:, p * hd:p * (hd + 1)] for hd in range(SSD_HPG)]
        hs = [hst_ref[hd] for hd in range(SSD_HPG)]
        f = functools.partial(_ssd_chunk, reverse=reverse)
        _, vjp = jax.vjp(f, xs, bm_ref[...], cm_ref[...], dtc_ref[...], dtr_ref[...], ar_ref[...], ac_ref[...], hs)
        dys = [dy_ref[:, p * hd:p * (hd + 1)] for hd in range(SSD_HPG)]
        dxs, dbm, dcm, ddtc, ddtr, dar, dac, dhs = vjp((dys, [dh_s[hd] for hd in range(SSD_HPG)]))
        for hd in range(SSD_HPG):
            dx_ref[:, p * hd:p * (hd + 1)] = dxs[hd]
            dh_s[hd] = dhs[hd]
        dbm_ref[...] = dbm
        dcm_ref[...] = dcm
        ddtc_ref[...] = ddtc
        ddtr_ref[...] = ddtr

        @pl.when(i == 0)
        def _():
            dar_ref[...] = dar
            dac_ref[...] = dac

        @pl.when(i > 0)
        def _():
            dar_ref[...] += dar
            dac_ref[...] += dac

    order = lambda b, g, i: (b, g, nc - 1 - i)
    ch = lambda b, g, i: _chunk_at(nc - 1 - i, reverse, ncc, nc)
    state = (SSD_HPG, SSD_HEAD_DIM, SSD_STATE)
    in_specs = _ssd_specs(reverse, ncc, nc, order) + [
        pl.BlockSpec((None, None, None) + state, lambda b, g, i: (b, g, nc - 1 - i, 0, 0, 0)),
        pl.BlockSpec((None, SSD_CHUNK, xw), lambda b, g, i: (b, ch(b, g, i), g))]
    gn = SSD_GROUPS * SSD_STATE
    out_specs = [
        pl.BlockSpec((None, SSD_CHUNK, xw), lambda b, g, i: (b, ch(b, g, i), g)),
        pl.BlockSpec((None, SSD_CHUNK, SSD_STATE), lambda b, g, i: (b, ch(b, g, i), g)),
        pl.BlockSpec((None, SSD_CHUNK, SSD_STATE), lambda b, g, i: (b, ch(b, g, i), g)),
        pl.BlockSpec((None, None, SSD_CHUNK, SSD_HPG), lambda b, g, i: (b, g, ch(b, g, i), 0)),
        pl.BlockSpec((None, None, SSD_HPG, SSD_CHUNK), lambda b, g, i: (b, g, 0, ch(b, g, i))),
        pl.BlockSpec((None, None, 1, SSD_HPG), lambda b, g, i: (b, g, 0, 0)),
        pl.BlockSpec((None, None, SSD_HPG, 1), lambda b, g, i: (b, g, 0, 0)),
    ]
    out_shape = [
        jax.ShapeDtypeStruct((nb, s, SSD_GROUPS * xw), F32),
        jax.ShapeDtypeStruct((nb, s, gn), F32),
        jax.ShapeDtypeStruct((nb, s, gn), F32),
        jax.ShapeDtypeStruct(dt_col.shape, F32),
        jax.ShapeDtypeStruct(dt_row.shape, F32),
        jax.ShapeDtypeStruct((nb, SSD_GROUPS, 1, SSD_HPG), F32),
        jax.ShapeDtypeStruct((nb, SSD_GROUPS, SSD_HPG, 1), F32),
    ]
    return pl.pallas_call(
        body,
        name=name,
        grid=(nb, SSD_GROUPS, nc),
        in_specs=in_specs,
        out_specs=out_specs,
        out_shape=out_shape,
        scratch_shapes=[pltpu.VMEM(state, F32)],
        compiler_params=_cparams(("parallel", "parallel", "arbitrary")),
    )(xbc, xbc, xbc, dt_col, dt_row, a_row, a_col, hst, dy)


def _hg_chunk(q, k, lf, v, st, reverse):
    n = q.shape[0]
    last = 0 if reverse else n - 1
    cum = jnp.dot(_tri(n, reverse), lf, precision=_HI, preferred_element_type=F32)
    rows = lax.broadcasted_iota(jnp.int32, q.shape, 0)
    y = jnp.zeros(v.shape, F32)
    for s in range(n):
        later = (rows <= s) if reverse else (rows >= s)
        e = jnp.exp(jnp.where(later, cum - _row(cum, s), -jnp.inf))
        att = jnp.sum(q * _row(k, s) * e, axis=1, keepdims=True)
        y = y + att * _row(v, s)
    y = y + _dot_nt((q * jnp.exp(cum)).astype(BF16), st.astype(BF16))
    tot = _row(cum, last)
    st_new = st * jnp.exp(tot) + _dot_tn(v.astype(BF16), (k * jnp.exp(tot - cum)).astype(BF16))
    return y, st_new


def _hg_super(s_ctx):
    return min(256, s_ctx)


def hg_fwd(name, q, k, lf, v, reverse, s_ctx):
    nb, s, w = q.shape
    nh, dk, sup = w // HG_DK, HG_DK, _hg_super(s_ctx)
    nsup, nsc, cps = s // sup, s_ctx // sup, sup // HG_CHUNK

    def body(q_ref, k_ref, lf_ref, v_ref, y_ref, hst_ref, st_s):
        i = pl.program_id(2)

        @pl.when(i == 0)
        def _():
            st_s[...] = jnp.zeros_like(st_s)

        def step(c, st):
            r0 = pl.multiple_of((cps - 1 - c if reverse else c) * HG_CHUNK, HG_CHUNK)
            rows = pl.ds(r0, HG_CHUNK)
            hst_ref[c] = st
            y, st_new = _hg_chunk(q_ref[rows, :], k_ref[rows, :], lf_ref[rows, :], v_ref[rows, :], st, reverse)
            y_ref[rows, :] = y
            return st_new

        st_s[...] = lax.fori_loop(0, cps, step, st_s[...])

    blk = pl.BlockSpec((None, sup, dk), lambda b, h, i: (b, _chunk_at(i, reverse, nsc, nsup), h))
    return pl.pallas_call(
        body,
        name=name,
        grid=(nb, nh, nsup),
        in_specs=[blk] * 4,
        out_specs=[blk, pl.BlockSpec((None, None, cps, dk, dk), lambda b, h, i: (b, h, i, 0, 0))],
        out_shape=[jax.ShapeDtypeStruct(q.shape, F32), jax.ShapeDtypeStruct((nb, nh, s // HG_CHUNK, dk, dk), F32)],
        scratch_shapes=[pltpu.VMEM((dk, dk), F32)],
        compiler_params=_cparams(("parallel", "parallel", "arbitrary")),
    )(q, k, lf, v)


def hg_bwd(name, q, k, lf, v, hst, dy, reverse, s_ctx):
    nb, s, w = q.shape
    nh, dk, sup = w // HG_DK, HG_DK, _hg_super(s_ctx)
    nsup, nsc, cps = s // sup, s_ctx // sup, sup // HG_CHUNK

    def body(q_ref, k_ref, lf_ref, v_ref, hst_ref, dy_ref, dq_ref, dk_ref, dlf_ref, dv_ref, dst_s):
        i = pl.program_id(2)

        @pl.when(i == 0)
        def _():
            dst_s[...] = jnp.zeros_like(dst_s)

        def step(cc, dst):
            c = cps - 1 - cc
            r0 = pl.multiple_of((cps - 1 - c if reverse else c) * HG_CHUNK, HG_CHUNK)
            rows = pl.ds(r0, HG_CHUNK)
            f = functools.partial(_hg_chunk, reverse=reverse)
            _, vjp = jax.vjp(f, q_ref[rows, :], k_ref[rows, :], lf_ref[rows, :], v_ref[rows, :], hst_ref[c])
            dq, dkk, dlf, dv, dst_prev = vjp((dy_ref[rows, :], dst))
            dq_ref[rows, :] = dq
            dk_ref[rows, :] = dkk
            dlf_ref[rows, :] = dlf
            dv_ref[rows, :] = dv
            return dst_prev

        dst_s[...] = lax.fori_loop(0, cps, step, dst_s[...])

    blk = pl.BlockSpec((None, sup, dk), lambda b, h, i: (b, _chunk_at(nsup - 1 - i, reverse, nsc, nsup), h))
    return pl.pallas_call(
        body,
        name=name,
        grid=(nb, nh, nsup),
        in_specs=[blk] * 4 + [pl.BlockSpec((None, None, cps, dk, dk), lambda b, h, i: (b, h, nsup - 1 - i, 0, 0)), blk],
        out_specs=[blk] * 4,
        out_shape=[jax.ShapeDtypeStruct(q.shape, F32)] * 4,
        scratch_shapes=[pltpu.VMEM((dk, dk), F32)],
        compiler_params=_cparams(("parallel", "parallel", "arbitrary")),
    )(q, k, lf, v, hst, dy)


def f_s5_discretize(ins):
    lam_re, lam_im, log_step, b_re, b_im = ins
    step = jnp.exp(log_step)
    mag = jnp.exp(lam_re * step)
    ar, ai = mag * jnp.cos(lam_im * step), mag * jnp.sin(lam_im * step)
    den = lam_re * lam_re + lam_im * lam_im
    zr = ((ar - 1.0) * lam_re + ai * lam_im) / den
    zi = (ai * lam_re - (ar - 1.0) * lam_im) / den
    return [ar, ai, zr * b_re - zi * b_im, zr * b_im + zi * b_re]


def s5_tiles_of(m):
    g, p, k = m.shape
    gt = S5_TILE_CH // k
    eye = jnp.eye(gt, dtype=m.dtype)
    t = m.reshape(g // gt, gt, p, 1, k) * eye[None, :, None, :, None]
    return t.reshape(g // gt, gt * p, gt * k)


def s5_groups_of(t, g, p, k):
    gt = S5_TILE_CH // k
    eye = jnp.eye(gt, dtype=t.dtype)
    return jnp.sum(t.reshape(g // gt, gt, p, gt, k) * eye[None, :, None, :, None], axis=3).reshape(g, p, k)


def f_lower_bounds(ins):
    (logits,) = ins
    e = jnp.exp(logits - jnp.max(logits, axis=0, keepdims=True))
    p = e / jnp.sum(e, axis=0, keepdims=True)
    n = logits.shape[0]
    li = lax.broadcasted_iota(jnp.int32, (n, n), 0)
    si = lax.broadcasted_iota(jnp.int32, (n, n), 1)
    after_first = jnp.where(jnp.logical_and(si >= 1, si <= li), 1.0, 0.0)
    return [jnp.dot(after_first, p, precision=_HI, preferred_element_type=F32)]


def f_silu(ins):
    return [_silu(ins[0])]


def f_norm_keep(shift_row, scale_row):
    def f(tv, mv, pv):
        return [tv[0], _rms(tv[0], pv[0]) * (1.0 + mv[scale_row]) + mv[shift_row]]
    return f


def f_dt(tv, mv, pv):
    return [_softplus(tv[0] + pv[0])]


def f_even_finish(tv, mv, pv):
    y_f, y_b, xs, z, h_sum, gy = tv
    d_exp, g = pv
    y = _rms((y_f + y_b + d_exp * xs) * _silu(z), g)
    return [y, h_sum * jax.nn.gelu(gy)]


def f_odd_prep(tv, mv, pv):
    q, f_f, f_b = tv
    (lb,) = pv
    outs = [_silu(q)]
    for f in (f_f, f_b):
        outs.append((1.0 - lb) * jax.nn.sigmoid(-f))
        outs.append(jnp.log(lb + (1.0 - lb) * jax.nn.sigmoid(f)))
    return outs


def f_odd_finish(tv, mv, pv):
    o_f, o_b, g, s5y, u = tv
    norm_g, s5_d, glu_w, glu_b = pv
    o = o_f + o_b
    w = o.shape[1]
    hi = lax.broadcasted_iota(jnp.int32, (w, w), 0) // HG_DK
    hj = lax.broadcasted_iota(jnp.int32, (w, w), 1) // HG_DK
    head_mean = jnp.where(hi == hj, 1.0 / HG_DK, 0.0)
    ms = jnp.dot(o * o, head_mean, precision=_HI, preferred_element_type=F32)
    on = o * lax.rsqrt(ms + RMS_EPS) * norm_g * _silu(g)
    y = jax.nn.gelu(s5y + s5_d * u)
    gate = jax.nn.sigmoid(jnp.dot(y.astype(BF16), glu_w.astype(BF16), preferred_element_type=F32) + glu_b)
    return [on, y * gate]


def final_loss(name, s, br, mod, g, target, s_ctx):
    nb, st, d = s.shape
    tb = TOK_BLOCK
    assert s_ctx == tb

    def lossf(sv, bv, gate, gv, tv):
        y = _rms(sv + gate * bv, gv)
        err = jnp.square(y - tv)
        return 0.5 * jnp.sum(jnp.mean(err, axis=-1, keepdims=True), axis=0, keepdims=True)

    def body(s_ref, b_ref, m_ref, g_ref, t_ref, l_ref, ds_ref, db_ref, dm_ref, dg_ref):
        b, t = pl.program_id(0), pl.program_id(1)

        @pl.when(t == 0)
        def _():
            ds_ref[...] = jnp.zeros_like(ds_ref)
            db_ref[...] = jnp.zeros_like(db_ref)
            dm_ref[...] = jnp.zeros_like(dm_ref)
            l_ref[...] = jnp.zeros_like(l_ref)

        @pl.when(jnp.logical_and(b == 0, t == 0))
        def _():
            dg_ref[...] = jnp.zeros_like(dg_ref)

        @pl.when(t > 0)
        def _():
            gate = m_ref[N_MOD - 1:N_MOD, :]
            l, vjp = jax.vjp(lossf, s_ref[...], b_ref[...], gate, g_ref[...], t_ref[...])
            ds, db, dgate, dg, _ = vjp(jnp.ones((1, 1), F32))
            ds_ref[...] = ds
            db_ref[...] = db
            dg_ref[...] += dg
            l_ref[...] += jnp.broadcast_to(l, l_ref.shape)

            @pl.when(t == 1)
            def _():
                dm_ref[...] = jnp.zeros_like(dm_ref)
                dm_ref[N_MOD - 1:N_MOD, :] = dgate

            @pl.when(t > 1)
            def _():
                dm_ref[N_MOD - 1:N_MOD, :] += dgate

    tok = pl.BlockSpec((None, tb, d), lambda b, t: (b, t, 0))
    modspec = pl.BlockSpec((None, N_MOD, d), _mod_index)
    gspec = pl.BlockSpec((1, d), lambda b, t: (0, 0))
    return pl.pallas_call(
        body,
        name=name,
        grid=(nb, st // tb),
        in_specs=[tok, tok, modspec, gspec, pl.BlockSpec((None, tb, d), lambda b, t: (b, jnp.maximum(t - 1, 0), 0))],
        out_specs=[pl.BlockSpec((None, SUBLANES, 128), lambda b, t: (b, 0, 0)), tok, tok, modspec, gspec],
        out_shape=[jax.ShapeDtypeStruct((nb, SUBLANES, 128), F32), jax.ShapeDtypeStruct(s.shape, F32),
                   jax.ShapeDtypeStruct(s.shape, F32), jax.ShapeDtypeStruct(mod.shape, F32), jax.ShapeDtypeStruct(g.shape, F32)],
        compiler_params=_cparams(("arbitrary", "arbitrary")),
    )(s, br, mod, g, target)


def adamw(name, w, g, m, v):
    shape = w.shape
    cols = shape[-1] if w.ndim >= 2 else w.size
    rows = w.size // cols
    tr = _pick(rows, (512, 256, 128, 64, 32, 16, 8))

    def body(w_ref, g_ref, m_ref, v_ref, d_ref, nm_ref, nv_ref):
        gv = g_ref[...]
        nm = ADAM_B1 * m_ref[...] + (1.0 - ADAM_B1) * gv
        nv = ADAM_B2 * v_ref[...] + (1.0 - ADAM_B2) * jnp.square(gv)
        m_hat = nm / (1.0 - ADAM_B1 ** ADAM_STEP)
        v_hat = nv / (1.0 - ADAM_B2 ** ADAM_STEP)
        d_ref[...] = -ADAM_LR * (m_hat / (jnp.sqrt(v_hat) + ADAM_EPS) + ADAM_WD * w_ref[...])
        nm_ref[...] = nm
        nv_ref[...] = nv

    spec = pl.BlockSpec((tr, cols), lambda i: (i, 0))
    outs = pl.pallas_call(
        body,
        name=name,
        grid=(rows // tr,),
        in_specs=[spec] * 4,
        out_specs=[spec] * 3,
        out_shape=[jax.ShapeDtypeStruct((rows, cols), F32)] * 3,
        compiler_params=_cparams(("parallel",)),
    )(*(a.reshape(rows, cols) for a in (w, g, m, v)))
    return tuple(o.reshape(shape) for o in outs)


EV_COLS = {"z": (0, 1024), "xbc": (1024, 2560), "dt": (2560, 2592), "gy": (2592, 3616), "u": (3616, 4640)}
OD_COLS = {"q": (0, 768), "ff": (768, 1536), "fb": (1536, 2304), "v": (2304, 3072), "g": (3072, 3840), "u": (3840, 4096)}
EV_OUT_ROWS = ((0, 1024), (1024, 2048))
OD_OUT_ROWS = ((0, 768), (768, 1024))
LANES = 128


def _pad_to_lanes(w):
    n = w.shape[1]
    return w if n % LANES == 0 else jnp.pad(w, ((0, 0), (0, LANES - n % LANES)))


def _layer_weights(l, big):
    j = l // 2
    even = l % 2 == 0
    w_in = big["ev_w_in" if even else "od_w_in"][j]
    w_out = big["ev_w_out" if even else "od_w_out"][j]
    lw = {"in": {}, "out": []}
    for name, (a, b) in (EV_COLS if even else OD_COLS).items():
        w = _pad_to_lanes(w_in[:, a:b])
        lw["in"][name] = (w, w.T)
    for a, b in (EV_OUT_ROWS if even else OD_OUT_ROWS):
        lw["out"].append((w_out[a:b], w_out[a:b].T))
    for name in ("gate", "up", "down"):
        w = big["ffn_w_" + name][l]
        lw[name] = (w, w.T)
    return lw


def _rows2d(a):
    return a.reshape(-1, a.shape[-1])


def _mm3(a, w, name, out_dtype=F32):
    return mm([(_rows2d(a), w)], name, out_dtype).reshape(a.shape[:-1] + (w.shape[1],))


def _wgrad(a, d, name):
    return mm([(_rows2d(a).T, _rows2d(d))], name)


def _dgrad(pairs, name, shape3):
    return mm([(_rows2d(d), wt) for d, wt in pairs], name).reshape(shape3[:-1] + (pairs[0][1].shape[1],))


def _dir_dt(dt, d):
    nb, s, _ = dt.shape
    dd = dt[:, :, SSD_HEADS * d:SSD_HEADS * (d + 1)].reshape(nb, s, SSD_GROUPS, SSD_HPG)
    return jnp.transpose(dd, (0, 2, 1, 3)), jnp.transpose(dd, (0, 2, 3, 1))


def _s5_prepare(p, j, tag):
    g_, p_, k_ = S5_GROUPS, S5_STATE, S5_GROUP_CH
    col = lambda t: t.reshape(g_ * p_, 1)
    ins, outs = [], []
    for d in (0, 1):
        i_d = [col(p["s5_lam_re"][j, d]), col(p["s5_lam_im"][j, d]), col(jnp.repeat(p["s5_log_step"][j, d], p_)),
               p["s5_b_re"][j].reshape(g_ * p_, k_), p["s5_b_im"][j].reshape(g_ * p_, k_)]
        ins.append(i_d)
        outs.append(small_fwd(f"{tag}_disc{d}", f_s5_discretize, i_d, [(g_ * p_, 1)] * 2 + [(g_ * p_, k_)] * 2))
    lam_r = jnp.stack([o[0].reshape(1, g_ * p_) for o in outs])
    lam_i = jnp.stack([o[1].reshape(1, g_ * p_) for o in outs])
    bt_r = jnp.stack([s5_tiles_of(o[2].reshape(g_, p_, k_)) for o in outs])
    bt_i = jnp.stack([s5_tiles_of(o[3].reshape(g_, p_, k_)) for o in outs])
    ct_r = jnp.stack([s5_tiles_of(jnp.transpose(p["s5_c_re"][j, d], (0, 2, 1))) for d in (0, 1)])
    ct_i = jnp.stack([s5_tiles_of(jnp.transpose(p["s5_c_im"][j, d], (0, 2, 1))) for d in (0, 1)])
    return ins, (lam_r, lam_i, bt_r, bt_i, ct_r, ct_i)


def _s5_param_grads(ins, grads, tag):
    g_, p_, k_ = S5_GROUPS, S5_STATE, S5_GROUP_CH
    dlr, dli, dbtr, dbti, dctr, dcti = grads
    g_lre, g_lim, g_ls, g_bre, g_bim = [], [], [], 0.0, 0.0
    for d in (0, 1):
        cots = [dlr[d].reshape(g_ * p_, 1), dli[d].reshape(g_ * p_, 1),
                s5_groups_of(dbtr[d], g_, p_, k_).reshape(g_ * p_, k_), s5_groups_of(dbti[d], g_, p_, k_).reshape(g_ * p_, k_)]
        g = small_bwd(f"{tag}_disc_bwd{d}", f_s5_discretize, ins[d], cots)
        g_lre.append(g[0].reshape(g_, p_))
        g_lim.append(g[1].reshape(g_, p_))
        g_ls.append(g[2].reshape(g_, p_).sum(-1))
        g_bre = g_bre + g[3].reshape(g_, p_, k_)
        g_bim = g_bim + g[4].reshape(g_, p_, k_)
    g_cre = jnp.stack([jnp.transpose(s5_groups_of(dctr[d], g_, p_, k_), (0, 2, 1)) for d in (0, 1)])
    g_cim = jnp.stack([jnp.transpose(s5_groups_of(dcti[d], g_, p_, k_), (0, 2, 1)) for d in (0, 1)])
    return jnp.stack(g_lre), jnp.stack(g_lim), jnp.stack(g_ls), g_bre, g_bim, g_cre, g_cim


def _even_mixer_fwd(l, hn, p, lw, s_ctx):
    j = l // 2
    t1 = taps_1d(4, s_ctx, hn.shape[1])
    r = {"hn": hn}
    proj = {n: _mm3(hn, lw["in"][n][0], f"l{l}_proj_{n}") for n in EV_COLS}
    r["z"], r["xbc"], r["gy"], r["u"] = proj["z"], proj["xbc"], proj["gy"], proj["u"]
    r["dtp"] = proj["dt"][:, :, :2 * SSD_HEADS]
    r["xbc_c"] = conv_fwd(f"l{l}_ssd_conv", r["xbc"], p["ssd_conv_w"][j], p["ssd_conv_b"][j][None], t1, "silu")
    r["u_c"] = conv_fwd(f"l{l}_lru_conv", r["u"], p["lru_conv_w"][j], p["lru_conv_b"][j][None], t1, "none")
    r["dt_bias"] = p["ssd_dt_bias"][j].reshape(1, 2 * SSD_HEADS)
    (r["dt"],) = tok_fwd(f"l{l}_dt", f_dt, [r["dtp"]], None, [r["dt_bias"]], [2 * SSD_HEADS], [F32])
    r["ys"], r["hst"], r["dts"], r["alog"] = [], [], [], []
    for d in (0, 1):
        dtc, dtr = _dir_dt(r["dt"], d)
        al = p["ssd_a_log"][j, d].reshape(SSD_GROUPS, SSD_HPG)
        al_r, al_c = al[:, None, :], al[:, :, None]
        y, hst = ssd_fwd(f"l{l}_ssd_fwd{d}", r["xbc_c"], dtc, dtr, al_r, al_c, bool(d), s_ctx)
        r["ys"].append(y)
        r["hst"].append(hst)
        r["dts"].append((dtc, dtr))
        r["alog"].append((al_r, al_c))
    v4 = lambda t: t.reshape(2, LRU_BLOCKS, 1, LRU_BLOCK_W)
    r["lru_p"] = (p["lru_w_a"][j], v4(p["lru_b_a"][j]), p["lru_w_i"][j], v4(p["lru_b_i"][j]), v4(p["lru_lam"][j]))
    r["h_sum"] = lru_fwd(f"l{l}_lru_fwd", r["u_c"], *r["lru_p"], s_ctx)
    r["xs"] = r["xbc_c"][:, :, :SSD_HEADS * SSD_HEAD_DIM]
    r["fin_p"] = [jnp.repeat(p["ssd_d"][j], SSD_HEAD_DIM)[None], p["ssd_norm_g"][j][None]]
    r["fin_in"] = [r["ys"][0], r["ys"][1], r["xs"], r["z"], r["h_sum"], r["gy"]]
    r["o"] = tok_fwd(f"l{l}_even_finish", f_even_finish, r["fin_in"], None, r["fin_p"], [1024, 1024], [BF16, BF16])
    return r


def _even_mixer_bwd(l, r, dox, p, lw, s_ctx, grads):
    j = l // 2
    shape3 = dox.shape
    t1 = taps_1d(4, s_ctx, shape3[1])
    grads["ev_w_out"][j] = jnp.concatenate([_wgrad(o, dox, f"l{l}_dwout{i}") for i, o in enumerate(r["o"])], axis=0)
    do = [_dgrad([(dox, lw["out"][i][1])], f"l{l}_dout{i}", shape3) for i in range(2)]
    (dy, _, dxs, dz, dh_sum, dgy), _, (dd_exp, grads["ssd_norm_g"][j]) = tok_bwd(
        f"l{l}_even_finish_bwd", f_even_finish, r["fin_in"], None, r["fin_p"], do, [F32, F32, F32, BF16, F32, BF16])
    grads["ssd_d"][j] = dd_exp.reshape(SSD_HEADS, SSD_HEAD_DIM).sum(-1)
    du_c, dwa, dba, dwi, dbi, dlam = lru_bwd(f"l{l}_lru_bwd", r["u_c"], *r["lru_p"], dh_sum, s_ctx)
    grads["lru_w_a"][j], grads["lru_w_i"][j] = dwa, dwi
    v2 = lambda t: t.reshape(2, LRU_BLOCKS * LRU_BLOCK_W)
    grads["lru_b_a"][j], grads["lru_b_i"][j], grads["lru_lam"][j] = v2(dba), v2(dbi), v2(dlam)
    dx_sum, dbm_sum, dcm_sum, ddts, dalog = dxs, 0.0, 0.0, [], []
    for d in (0, 1):
        dx, dbm, dcm, ddtc, ddtr, dar, dac = ssd_bwd(
            f"l{l}_ssd_bwd{d}", r["xbc_c"], *r["dts"][d], *r["alog"][d], r["hst"][d], dy, bool(d), s_ctx)
        dx_sum, dbm_sum, dcm_sum = dx_sum + dx, dbm_sum + dbm, dcm_sum + dcm
        ddts.append((jnp.transpose(ddtc, (0, 2, 1, 3)) + jnp.transpose(ddtr, (0, 3, 1, 2))).reshape(shape3[0], shape3[1], SSD_HEADS))
        dalog.append((dar.sum(0)[:, 0, :] + dac.sum(0)[:, :, 0]).reshape(SSD_HEADS))
    grads["ssd_a_log"][j] = jnp.stack(dalog)
    dxbc_c = jnp.concatenate([dx_sum, dbm_sum, dcm_sum], axis=-1)
    (ddtp,), _, (ddt_bias,) = tok_bwd(f"l{l}_dt_bwd", f_dt, [r["dtp"]], None, [r["dt_bias"]], [jnp.concatenate(ddts, axis=-1)], [F32])
    grads["ssd_dt_bias"][j] = ddt_bias.reshape(2, SSD_HEADS)
    dxbc, grads["ssd_conv_w"][j], dcb = conv_bwd(f"l{l}_ssd_conv_bwd", r["xbc"], p["ssd_conv_w"][j], p["ssd_conv_b"][j][None], dxbc_c, t1, "silu", dx_dtype=BF16)
    du, grads["lru_conv_w"][j], dlb = conv_bwd(f"l{l}_lru_conv_bwd", r["u"], p["lru_conv_w"][j], p["lru_conv_b"][j][None], du_c, t1, "none", dx_dtype=BF16)
    grads["ssd_conv_b"][j], grads["lru_conv_b"][j] = dcb[0], dlb[0]
    dproj = {"z": dz, "xbc": dxbc, "dt": _pad_to_lanes(_rows2d(ddtp)).reshape(shape3[:2] + (LANES,)), "gy": dgy, "u": du}
    grads["ev_w_in"][j] = jnp.concatenate(
        [_wgrad(r["hn"], dproj[n], f"l{l}_dwin_{n}")[:, :b - a] for n, (a, b) in EV_COLS.items()], axis=1)
    return _dgrad([(dproj[n], lw["in"][n][1]) for n in EV_COLS], f"l{l}_dhn", shape3)


def _odd_mixer_fwd(l, hn, p, lw, lb_row, s_ctx):
    j = l // 2
    r = {"hn": hn}
    proj = {n: _mm3(hn, lw["in"][n][0], f"l{l}_proj_{n}") for n in OD_COLS}
    r["v"], r["g"], r["u"] = proj["v"], proj["g"], proj["u"]
    r["prep_in"] = [proj["q"], proj["ff"], proj["fb"]]
    r["lb"] = lb_row
    r["prep"] = tok_fwd(f"l{l}_odd_prep", f_odd_prep, r["prep_in"], None, [lb_row], [HG_W] * 5, [F32] * 5)
    qs = r["prep"][0]
    r["os"], r["hst"] = [], []
    for d in (0, 1):
        o, hst = hg_fwd(f"l{l}_hg_fwd{d}", qs, r["prep"][1 + 2 * d], r["prep"][2 + 2 * d], r["v"], bool(d), s_ctx)
        r["os"].append(o)
        r["hst"].append(hst)
    r["s5_ins"], r["s5_p"] = _s5_prepare(p, j, f"l{l}_s5")
    r["s5y"] = s5_fwd(f"l{l}_s5_fwd", r["u"], *r["s5_p"], s_ctx)
    r["fin_p"] = [p["hg_norm_g"][j].reshape(1, HG_W), p["s5_d"][j][None], p["s5_glu_w"][j], p["s5_glu_b"][j][None]]
    r["fin_in"] = [r["os"][0], r["os"][1], r["g"], r["s5y"], r["u"]]
    r["o"] = tok_fwd(f"l{l}_odd_finish", f_odd_finish, r["fin_in"], None, r["fin_p"], [HG_W, S5_W], [BF16, BF16])
    return r


def _odd_mixer_bwd(l, r, dox, p, lw, s_ctx, grads):
    j = l // 2
    shape3 = dox.shape
    grads["od_w_out"][j] = jnp.concatenate([_wgrad(o, dox, f"l{l}_dwout{i}") for i, o in enumerate(r["o"])], axis=0)
    do = [_dgrad([(dox, lw["out"][i][1])], f"l{l}_dout{i}", shape3) for i in range(2)]
    (do_hg, _, dg, ds5y, du_fin), _, (dng, grads["s5_d"][j], grads["s5_glu_w"][j], dglu_b) = tok_bwd(
        f"l{l}_odd_finish_bwd", f_odd_finish, r["fin_in"], None, r["fin_p"], do, [F32, F32, BF16, F32, F32])
    grads["hg_norm_g"][j] = dng.reshape(HG_HEADS, HG_DK)
    grads["s5_d"][j], grads["s5_glu_b"][j] = grads["s5_d"][j][0], dglu_b[0]
    s5g = s5_bwd(f"l{l}_s5_bwd", r["u"], *r["s5_p"], ds5y, s_ctx)
    du = s5g[0] + du_fin
    (grads["s5_lam_re"][j], grads["s5_lam_im"][j], grads["s5_log_step"][j], grads["s5_b_re"][j], grads["s5_b_im"][j],
     grads["s5_c_re"][j], grads["s5_c_im"][j]) = _s5_param_grads(r["s5_ins"], s5g[1:], f"l{l}_s5")
    qs = r["prep"][0]
    dqs, dv, dprep = 0.0, 0.0, [None] * 5
    for d in (0, 1):
        dq, dk, dlf, dvd = hg_bwd(f"l{l}_hg_bwd{d}", qs, r["prep"][1 + 2 * d], r["prep"][2 + 2 * d], r["v"], r["hst"][d], do_hg, bool(d), s_ctx)
        dqs, dv = dqs + dq, dv + dvd
        dprep[1 + 2 * d], dprep[2 + 2 * d] = dk, dlf
    dprep[0] = dqs
    (dq_, dff, dfb), _, (dlb,) = tok_bwd(f"l{l}_odd_prep_bwd", f_odd_prep, r["prep_in"], None, [r["lb"]], dprep, [BF16] * 3)
    dproj = {"q": dq_, "ff": dff, "fb": dfb, "v": dv, "g": dg, "u": du}
    grads["od_w_in"][j] = jnp.concatenate([_wgrad(r["hn"], dproj[n], f"l{l}_dwin_{n}") for n in OD_COLS], axis=1)
    return _dgrad([(dproj[n], lw["in"][n][1]) for n in OD_COLS], f"l{l}_dhn", shape3), dlb


def _ffn_fwd(l, fn, p, lw, s_ctx):
    r = {"fn": fn}
    tg = taps_grid(s_ctx, fn.shape[1], GRID_W)
    r["a"] = _mm3(fn, lw["gate"][0], f"l{l}_ffn_gate")
    r["up"] = _mm3(fn, lw["up"][0], f"l{l}_ffn_up")
    r["cw"], r["cb"] = p["ffn_conv_w"][l].reshape(9, D_FF), p["ffn_conv_b"][l][None]
    r["act"] = conv_fwd(f"l{l}_ffn_conv", r["a"], r["cw"], r["cb"], tg, "silu_mul", mul=r["up"], out_dtype=BF16)
    return r, _mm3(r["act"], lw["down"][0], f"l{l}_ffn_down")


def _ffn_bwd(l, r, dfo, lw, s_ctx, grads):
    shape3 = dfo.shape
    tg = taps_grid(s_ctx, shape3[1], GRID_W)
    grads["ffn_w_down"][l] = _wgrad(r["act"], dfo, f"l{l}_dwdown")
    dact = _dgrad([(dfo, lw["down"][1])], f"l{l}_dact", shape3)
    da, dcw, dcb, dup = conv_bwd(f"l{l}_ffn_conv_bwd", r["a"], r["cw"], r["cb"], dact, tg, "silu_mul", mul=r["up"], dx_dtype=BF16)
    grads["ffn_conv_w"][l], grads["ffn_conv_b"][l] = dcw.reshape(3, 3, D_FF), dcb[0]
    grads["ffn_w_gate"][l] = _wgrad(r["fn"], da, f"l{l}_dwgate")
    grads["ffn_w_up"][l] = _wgrad(r["fn"], dup, f"l{l}_dwup")
    return _dgrad([(da, lw["gate"][1]), (dup, lw["up"][1])], f"l{l}_dfn", shape3)


BIG_WEIGHTS = ("ev_w_in", "ev_w_out", "od_w_in", "od_w_out", "ffn_w_gate", "ffn_w_up", "ffn_w_down")
PER_LAYER = {"norm_mix_g": DEPTH, "norm_ffn_g": DEPTH, "ffn_w_gate": DEPTH, "ffn_w_up": DEPTH, "ffn_conv_w": DEPTH,
             "ffn_conv_b": DEPTH, "ffn_w_down": DEPTH}


def local_step(x, ctx, target, modtabs, p, big, s_ctx=CTX_LEN):
    d_model = x.shape[-1]
    s0 = jnp.concatenate([ctx, x], axis=1)
    lws = [_layer_weights(l, big) for l in range(DEPTH)]
    shapes = {n: v.shape for n, v in {**p, **big}.items()}
    grads = {n: [None] * PER_LAYER.get(n, DEPTH // 2) for n in shapes if n not in ("c_ctx", "w_mod", "b_mod", "final_norm_g", "hg_lb_logits")}
    (lbs,) = small_fwd("lower_bounds", f_lower_bounds, [p["hg_lb_logits"]], [p["hg_lb_logits"].shape])
    tab_a = [modtabs[0]] + [modtabs[l].at[:, N_MOD - 1].set(modtabs[l - 1][:, N_MOD - 1]) for l in range(1, DEPTH)]
    res = []
    s, br = s0, None
    for l in range(DEPTH):
        r = {}
        g_mix, g_ffn = p["norm_mix_g"][l][None], p["norm_ffn_g"][l][None]
        if l == 0:
            (hn,) = tok_fwd("l0_norm", f_norm(0, 1), [s], tab_a[0], [g_mix], [d_model], [BF16])
            r["a_in"] = [s]
        else:
            r["a_in"] = [s, br]
            s, hn = tok_fwd(f"l{l}_resnorm_a", f_resnorm(5, 0, 1), r["a_in"], tab_a[l], [g_mix], [d_model] * 2, [F32, BF16])
        if l % 2 == 0:
            r["mix"] = _even_mixer_fwd(l, hn, p, lws[l], s_ctx)
        else:
            r["mix"] = _odd_mixer_fwd(l, hn, p, lws[l], lbs[l:l + 1], s_ctx)
        ox = mm([(_rows2d(o), w) for o, (w, _) in zip(r["mix"]["o"], lws[l]["out"])], f"l{l}_mix_out").reshape(s.shape)
        r["b_in"] = [s, ox]
        s, fn = tok_fwd(f"l{l}_resnorm_b", f_resnorm(2, 3, 4), r["b_in"], modtabs[l], [g_ffn], [d_model] * 2, [F32, BF16])
        r["ffn"], br = _ffn_fwd(l, fn, p, lws[l], s_ctx)
        res.append(r)

    loss_blk, ds, dbr, dtab_f, dfinal_g = final_loss("final_loss", s, br, modtabs[DEPTH - 1], p["final_norm_g"][None], target, s_ctx)
    grads["final_norm_g"] = dfinal_g[0]
    dmod = [None] * DEPTH
    dtab_next = dtab_f
    dlb = jnp.zeros_like(lbs)
    for l in reversed(range(DEPTH)):
        r = res[l]
        g_mix, g_ffn = p["norm_mix_g"][l][None], p["norm_ffn_g"][l][None]
        dfn = _ffn_bwd(l, r["ffn"], dbr, lws[l], s_ctx, grads)
        (ds, dox), dtab_b, (grads["norm_ffn_g"][l],) = tok_bwd(
            f"l{l}_resnorm_b_bwd", f_resnorm(2, 3, 4), r["b_in"], modtabs[l], [g_ffn], [ds, dfn], [F32, BF16])
        if l % 2 == 0:
            dhn = _even_mixer_bwd(l, r["mix"], dox, p, lws[l], s_ctx, grads)
        else:
            dhn, dlb_l = _odd_mixer_bwd(l, r["mix"], dox, p, lws[l], s_ctx, grads)
            dlb = dlb.at[l:l + 1].set(dlb_l)
        if l == 0:
            (ds,), dtab_a, (dg,) = tok_bwd("l0_norm_bwd", f_norm_keep(0, 1), r["a_in"], tab_a[0], [g_mix], [ds, dhn], [F32])
        else:
            (ds, dbr), dtab_a, (dg,) = tok_bwd(
                f"l{l}_resnorm_a_bwd", f_resnorm(5, 0, 1), r["a_in"], tab_a[l], [g_mix], [ds, dhn], [F32, F32])
        grads["norm_mix_g"][l] = dg
        dmod[l] = (dtab_a.at[:, N_MOD - 1].set(0.0) + dtab_b).at[:, N_MOD - 1].set(dtab_next[:, N_MOD - 1])
        dtab_next = dtab_a
    (grads["hg_lb_logits"],) = small_bwd("lower_bounds_bwd", f_lower_bounds, [p["hg_lb_logits"]], [dlb])
    out = {}
    for n, g in grads.items():
        if isinstance(g, list):
            g = jnp.stack([t.reshape(shapes[n][1:]) for t in g])
        out[n] = g.reshape(shapes[n])
    return loss_blk[:, 0, 0], ds[:, s_ctx:], dmod, out


WEIGHT_NAMES = (
    "c_ctx", "w_mod", "b_mod", "norm_mix_g", "norm_ffn_g", "final_norm_g", "ev_w_in", "ev_w_out", "ssd_conv_w",
    "ssd_conv_b", "ssd_dt_bias", "ssd_a_log", "ssd_d", "ssd_norm_g", "lru_conv_w", "lru_conv_b", "lru_w_a", "lru_b_a",
    "lru_w_i", "lru_b_i", "lru_lam", "od_w_in", "od_w_out", "hg_lb_logits", "hg_norm_g", "s5_lam_re", "s5_lam_im",
    "s5_log_step", "s5_b_re", "s5_b_im", "s5_c_re", "s5_c_im", "s5_d", "s5_glu_w", "s5_glu_b", "ffn_w_gate", "ffn_w_up",
    "ffn_conv_w", "ffn_conv_b", "ffn_w_down")
INPUT_NAMES = ("x", "c", "ctx") + WEIGHT_NAMES + ("loss_target",) + tuple("m_" + n for n in WEIGHT_NAMES) + tuple("v_" + n for n in WEIGHT_NAMES)
SHARD_AXIS = {"w_mod": 2, "ev_w_in": 2, "ev_w_out": 1, "ssd_conv_w": 2, "lru_conv_w": 2, "lru_b_a": 2, "lru_b_i": 2,
              "lru_lam": 2, "od_w_in": 2, "od_w_out": 1, "s5_d": 1, "s5_glu_w": 1, "s5_glu_b": 1, "ffn_w_gate": 2,
              "ffn_w_up": 2, "ffn_conv_w": 3, "ffn_w_down": 1}
SMALL_SHARDED = tuple(n for n in WEIGHT_NAMES if n in SHARD_AXIS and n not in BIG_WEIGHTS and n != "w_mod")
REPLICATED_LOCAL = tuple(n for n in WEIGHT_NAMES if n not in SHARD_AXIS and n not in ("c_ctx", "b_mod"))
PACK_WIDTH = 1024
MOD_ROWS = 48
CTX_ROW = 32


def _unshard(g8, axis):
    moved = jnp.moveaxis(g8, 0, axis)
    shp = moved.shape
    return moved.reshape(shp[:axis] + (shp[axis] * shp[axis + 1],) + shp[axis + 2:])


def _to_shards(full, axis):
    shp = full.shape
    return jnp.moveaxis(full.reshape(shp[:axis] + (N_DEV, shp[axis] // N_DEV) + shp[axis + 1:]), axis, 0)


def _pack(arrs, dtype, lead=()):
    flat = jnp.concatenate([a.astype(dtype).reshape(lead + (-1,)) for a in arrs], axis=-1)
    n = flat.shape[-1]
    unit = 16 * PACK_WIDTH
    padded = -(-n // unit) * unit
    flat = jnp.pad(flat, [(0, 0)] * len(lead) + [(0, padded - n)])
    return flat.reshape(lead + (padded // PACK_WIDTH, PACK_WIDTH))


def _unpack(packed, shapes, lead=()):
    flat = packed.reshape(lead + (-1,))
    out, off = [], 0
    for shp in shapes:
        n = math.prod(shp)
        out.append(flat[..., off:off + n].reshape(lead + tuple(shp)))
        off += n
    return out


def _my_block(full, axis, me):
    loc = full.shape[axis] // N_DEV
    return lax.dynamic_slice_in_dim(full, me * loc, loc, axis)


def kernel(*args):
    a = dict(zip(INPUT_NAMES, args))
    px, py, pc = _my_pos()
    me = 4 * px + 2 * py + pc
    nb = a["x"].shape[0]

    big8 = all_gather([a[n].astype(BF16) for n in BIG_WEIGHTS], "gather_big_weights")
    big = {n: _unshard(g, SHARD_AXIS[n]) for n, g in zip(BIG_WEIGHTS, big8)}
    small_names = ("c",) + SMALL_SHARDED
    (small8,) = all_gather([_pack([a[n] for n in small_names], F32)], "gather_small")
    small = dict(zip(small_names, _unpack(small8, [a[n].shape for n in small_names], (N_DEV,))))
    p = {n: a[n] for n in WEIGHT_NAMES if n not in SHARD_AXIS}
    for n in SMALL_SHARDED:
        p[n] = _unshard(small[n], SHARD_AXIS[n])
    c_all = small["c"].reshape(N_DEV * nb, D_MODEL)

    rows = jnp.concatenate([c_all, a["c_ctx"][None], jnp.zeros((MOD_ROWS - CTX_ROW - 1, D_MODEL), F32)], axis=0)
    (srows,) = small_fwd("mod_silu", f_silu, [rows], [rows.shape])
    wmod2d = jnp.transpose(a["w_mod"], (1, 0, 2)).reshape(D_MODEL, -1).astype(BF16)
    cols = a["w_mod"].shape[2]
    mod_loc = mm([(srows, wmod2d)], "mod_proj")
    mod8 = all_gather([mod_loc], "gather_mod")[0].reshape(N_DEV, MOD_ROWS, DEPTH, cols)
    mod_all = jnp.transpose(mod8, (2, 1, 0, 3)).reshape(DEPTH, MOD_ROWS, N_DEV * cols) + a["b_mod"][:, None, :]
    modtabs = []
    for l in range(DEPTH):
        mine = lax.dynamic_slice_in_dim(mod_all[l], me * nb, nb, 0).reshape(nb, N_MOD, D_MODEL)
        ctx_row = jnp.broadcast_to(mod_all[l, CTX_ROW].reshape(1, N_MOD, D_MODEL), (nb, N_MOD, D_MODEL))
        modtabs.append(jnp.stack([ctx_row, mine], axis=1).reshape(2 * nb, N_MOD, D_MODEL))

    loss_b, grad_x, dmod, grads = local_step(a["x"], a["ctx"], a["loss_target"], modtabs, p, big)
    loss = lax.psum(jnp.sum(loss_b), ("x", "y", "c"))

    dm = jnp.stack([t.reshape(nb, 2, N_MOD * D_MODEL) for t in dmod])
    dloc = jnp.concatenate([dm[:, :, 1], jnp.sum(dm[:, :, 0], axis=1, keepdims=True),
                            jnp.zeros((DEPTH, SUBLANES - nb - 1, N_MOD * D_MODEL), F32)], axis=1)
    d8 = all_gather([dloc.reshape(DEPTH * SUBLANES, -1)], "gather_dmod")[0].reshape(N_DEV, DEPTH, SUBLANES, -1)
    d_rows = jnp.transpose(d8[:, :, :nb], (1, 0, 2, 3)).reshape(DEPTH, N_DEV * nb, -1)
    d_ctx = jnp.sum(d8[:, :, nb], axis=0)[:, None]
    d_full = jnp.concatenate([d_rows, d_ctx, jnp.zeros((DEPTH, MOD_ROWS - CTX_ROW - 1, N_MOD * D_MODEL), F32)], axis=1)
    grads["b_mod"] = jnp.sum(d_full, axis=1)
    d_cols = jnp.transpose(_my_block(d_full, 2, me), (1, 0, 2)).reshape(MOD_ROWS, DEPTH * cols)
    g_wmod = mm([(srows.T, d_cols)], "mod_dw")
    g_wmod_local = jnp.transpose(g_wmod.reshape(D_MODEL, DEPTH, cols), (1, 0, 2))
    d_srows_part = mm([(d_cols[CTX_ROW:CTX_ROW + SUBLANES], wmod2d.T)], "mod_dctx")[0]

    reduce_names = REPLICATED_LOCAL + SMALL_SHARDED
    (part8,) = all_gather([_pack([d_srows_part] + [grads[n] for n in reduce_names], F32)], "gather_small_grads")
    total = sum_slots(part8, "sum_small_grads")
    totals = _unpack(total, [(D_MODEL,)] + [grads[n].shape for n in reduce_names])
    d_srows = jnp.zeros_like(rows).at[CTX_ROW].set(totals[0])
    (d_rows_in,) = small_bwd("mod_silu_bwd", f_silu, [rows], [d_srows])
    g_local = {"c_ctx": d_rows_in[CTX_ROW], "b_mod": grads["b_mod"], "w_mod": g_wmod_local}
    for n, t in zip(reduce_names, totals[1:]):
        g_local[n] = _my_block(t, SHARD_AXIS[n], me) if n in SHARD_AXIS else t

    got = all_to_all([_to_shards(grads[n], SHARD_AXIS[n]).astype(BF16) for n in BIG_WEIGHTS], "exchange_big_grads")
    for n, t in zip(BIG_WEIGHTS, got):
        g_local[n] = sum_slots(t.reshape(N_DEV, -1, t.shape[-1]), "sum_" + n).reshape(a[n].shape)

    deltas, new_m, new_v = [], [], []
    for n in WEIGHT_NAMES:
        d, m, v = adamw("adamw_" + n, a[n], g_local[n], a["m_" + n], a["v_" + n])
        deltas.append(d)
        new_m.append(m)
        new_v.append(v)
    return (loss, grad_x, *[g_local[n] for n in WEIGHT_NAMES], *deltas, *new_m, *new_v)
```

```python
import functools
import math

import jax
import jax.numpy as jnp
from jax import lax
from jax.experimental import pallas as pl
from jax.experimental.pallas import tpu as pltpu

F32 = jnp.float32
BF16 = jnp.bfloat16

D_MODEL = 1024
DEPTH = 4
CTX_LEN = 256
SEQ = 2048
S_TOT = CTX_LEN + SEQ
GRID_W = 64
N_MOD = 6
RMS_EPS = 1e-6
N_DEV = 8

SSD_HEADS = 16
SSD_HEAD_DIM = 64
SSD_GROUPS = 2
SSD_HPG = 8
SSD_STATE = 128
SSD_CHUNK = 128
SSD_W = SSD_HEADS * SSD_HEAD_DIM
LRU_BLOCKS = 8
LRU_BLOCK_W = 128
LRU_C = 8.0
HG_W = 768
HG_HEADS = 6
HG_DK = 128
HG_CHUNK = 32
S5_W = 256
S5_GROUPS = 16
S5_GROUP_CH = 16
S5_STATE = 64
D_FF = 2816

ADAM_LR = 0.001
ADAM_B1 = 0.9
ADAM_B2 = 0.999
ADAM_EPS = 1e-08
ADAM_WD = 0.01
ADAM_STEP = 10

TOK_BLOCK = CTX_LEN
SUBLANES = 8
VMEM_LIMIT_BYTES = 56 * 1024 * 1024
MM_A_BLOCK_BYTES = 8 * 1024 * 1024


def _cparams(sem=None):
    kw = dict(vmem_limit_bytes=VMEM_LIMIT_BYTES)
    if sem is not None:
        kw["dimension_semantics"] = sem
    return pltpu.CompilerParams(**kw)


def _pick(n, cands):
    for c in cands:
        if n % c == 0:
            return c
    return n


def mm(pairs, name, out_dtype=F32):
    m = pairs[0][0].shape[0]
    n = pairs[0][1].shape[1]
    kdims = [a.shape[1] for a, _ in pairs]
    tn = _pick(n, (512, 384, 256, 128))
    ktile = None
    if len(pairs) == 1 and kdims[0] > 4096:
        ktile = _pick(kdims[0], (2304, 2048, 1024))
    nk = kdims[0] // ktile if ktile else 1
    row_bytes = sum((ktile or a.shape[1]) * a.dtype.itemsize for a, _ in pairs)
    tm = _pick(m, tuple(c for c in (1024, 512, 256, 128, 64, 48, 40, 32, 16, 8) if c * row_bytes <= MM_A_BLOCK_BYTES))
    npairs = len(pairs)
    if nk > 1:
        assert out_dtype == F32

    def body(*refs):
        o_ref = refs[2 * npairs]
        acc = None
        for i in range(npairs):
            a = refs[2 * i][...].astype(BF16)
            w = refs[2 * i + 1][...].astype(BF16)
            p = jnp.dot(a, w, preferred_element_type=F32)
            acc = p if acc is None else acc + p
        if nk == 1:
            o_ref[...] = acc.astype(out_dtype)
        else:
            k = pl.program_id(2)

            @pl.when(k == 0)
            def _():
                o_ref[...] = acc

            @pl.when(k > 0)
            def _():
                o_ref[...] += acc

    in_specs = []
    args = []
    for a, w in pairs:
        kk = a.shape[1]
        assert w.shape == (kk, n) and a.shape[0] == m, (a.shape, w.shape)
        tk = ktile if ktile else kk
        in_specs.append(pl.BlockSpec((tm, tk), lambda i, j, k: (i, k)))
        in_specs.append(pl.BlockSpec((tk, tn), lambda i, j, k: (k, j)))
        args += [a, w]
    return pl.pallas_call(
        body,
        name=name,
        grid=(m // tm, n // tn, nk),
        in_specs=in_specs,
        out_specs=pl.BlockSpec((tm, tn), lambda i, j, k: (i, j)),
        out_shape=jax.ShapeDtypeStruct((m, n), out_dtype),
        compiler_params=_cparams(("parallel", "parallel", "arbitrary")),
    )(*args)


def _mod_index(b, t):
    return (2 * b + jnp.minimum(t, 1), 0, 0)


def tok_fwd(name, f, toks, mod, params, out_widths, out_dtypes):
    nb, s, _ = toks[0].shape
    nt, nm, npar = len(toks), int(mod is not None), len(params)

    def body(*refs):
        ins, outs = refs[: nt + nm + npar], refs[nt + nm + npar:]
        tv = [r[...].astype(F32) for r in ins[:nt]]
        mv = [ins[nt][k:k + 1, :] for k in range(N_MOD)] if nm else None
        pv = [r[...] for r in ins[nt + nm:]]
        for o, r in zip(outs, f(tv, mv, pv)):
            o[...] = r.astype(o.dtype)

    in_specs = [pl.BlockSpec((None, TOK_BLOCK, t.shape[2]), lambda b, t: (b, t, 0)) for t in toks]
    if nm:
        in_specs.append(pl.BlockSpec((None, N_MOD, mod.shape[2]), _mod_index))
    in_specs += [pl.BlockSpec(p.shape, lambda b, t, nd=p.ndim: (0,) * nd) for p in params]
    return pl.pallas_call(
        body,
        name=name,
        grid=(nb, s // TOK_BLOCK),
        in_specs=in_specs,
        out_specs=[pl.BlockSpec((None, TOK_BLOCK, w), lambda b, t: (b, t, 0)) for w in out_widths],
        out_shape=[jax.ShapeDtypeStruct((nb, s, w), dt) for w, dt in zip(out_widths, out_dtypes)],
        compiler_params=_cparams(("parallel", "parallel")),
    )(*toks, *([mod] if nm else []), *params)


def tok_bwd(name, f, toks, mod, params, cots, dtok_dtypes):
    nb, s, _ = toks[0].shape
    nt, nm, npar, nc = len(toks), int(mod is not None), len(params), len(cots)

    def body(*refs):
        n_in = nt + nm + npar + nc
        ins, outs = refs[:n_in], refs[n_in:]
        b, t = pl.program_id(0), pl.program_id(1)
        tv = [r[...].astype(F32) for r in ins[:nt]]
        mv = [ins[nt][k:k + 1, :] for k in range(N_MOD)] if nm else None
        pv = [r[...] for r in ins[nt + nm: nt + nm + npar]]
        cv = [r[...].astype(F32) for r in ins[nt + nm + npar:]]
        _, vjp = jax.vjp(f, tv, mv, pv)
        dtv, dmv, dpv = vjp(cv)
        for o, r in zip(outs[:nt], dtv):
            o[...] = r.astype(o.dtype)
        if nm:
            dm_ref = outs[nt]

            @pl.when(t <= 1)
            def _():
                for k in range(N_MOD):
                    dm_ref[k:k + 1, :] = dmv[k]

            @pl.when(t > 1)
            def _():
                for k in range(N_MOD):
                    dm_ref[k:k + 1, :] += dmv[k]

        first = jnp.logical_and(b == 0, t == 0)
        for o, r in zip(outs[nt + nm:], dpv):
            @pl.when(first)
            def _(o=o, r=r):
                o[...] = r

            @pl.when(jnp.logical_not(first))
            def _(o=o, r=r):
                o[...] += r

    tok_spec = lambda w: pl.BlockSpec((None, TOK_BLOCK, w), lambda b, t: (b, t, 0))
    in_specs = [tok_spec(t.shape[2]) for t in toks]
    if nm:
        in_specs.append(pl.BlockSpec((None, N_MOD, mod.shape[2]), _mod_index))
    in_specs += [pl.BlockSpec(p.shape, lambda b, t, nd=p.ndim: (0,) * nd) for p in params]
    in_specs += [tok_spec(c.shape[2]) for c in cots]
    out_specs = [tok_spec(t.shape[2]) for t in toks]
    out_shape = [jax.ShapeDtypeStruct(t.shape, dt) for t, dt in zip(toks, dtok_dtypes)]
    if nm:
        out_specs.append(pl.BlockSpec((None, N_MOD, mod.shape[2]), _mod_index))
        out_shape.append(jax.ShapeDtypeStruct(mod.shape, F32))
    out_specs += [pl.BlockSpec(p.shape, lambda b, t, nd=p.ndim: (0,) * nd) for p in params]
    out_shape += [jax.ShapeDtypeStruct(p.shape, F32) for p in params]
    res = pl.pallas_call(
        body,
        name=name,
        grid=(nb, s // TOK_BLOCK),
        in_specs=in_specs,
        out_specs=out_specs,
        out_shape=out_shape,
        compiler_params=_cparams(("arbitrary", "arbitrary")),
    )(*toks, *([mod] if nm else []), *params, *cots)
    return res[:nt], (res[nt] if nm else None), res[nt + nm:]


def _rms(x, g):
    return x * lax.rsqrt(jnp.mean(x * x, axis=-1, keepdims=True) + RMS_EPS) * g


def _silu(x):
    return x * jax.nn.sigmoid(x)


def f_norm(shift_row, scale_row):
    def f(tv, mv, pv):
        return [_rms(tv[0], pv[0]) * (1.0 + mv[scale_row]) + mv[shift_row]]
    return f


def f_resnorm(gate_row, shift_row, scale_row):
    def f(tv, mv, pv):
        s = tv[0] + mv[gate_row] * tv[1]
        return [s, _rms(s, pv[0]) * (1.0 + mv[scale_row]) + mv[shift_row]]
    return f


_ANY = pl.BlockSpec(memory_space=pl.ANY)
_MESH = pl.DeviceIdType.MESH


def _my_pos():
    return lax.axis_index("x"), lax.axis_index("y"), lax.axis_index("c")


def _slot_of(pos):
    return 4 * pos[0] + 2 * pos[1] + pos[2]


def all_gather(xs, name):
    n = len(xs)

    def body(*refs):
        x_refs, out_refs = refs[:n], refs[n:2 * n]
        send_sems, recv_sems, local_sems = refs[2 * n:]
        px, py, pc = _my_pos()
        me, sibling = (px, py, pc), (px, py, 1 - pc)
        chips = [(1 - px, py), (px, 1 - py), (1 - px, 1 - py)]

        def copy(a, k, block, to, from_input=False):
            slot = out_refs[a].at[_slot_of(block)]
            return pltpu.make_async_remote_copy(
                src_ref=x_refs[a] if from_input else slot, dst_ref=slot,
                send_sem=send_sems.at[a, k], recv_sem=recv_sems.at[a, k],
                device_id=to, device_id_type=_MESH)

        mine = [pltpu.make_async_copy(x_refs[a], out_refs[a].at[_slot_of(me)], local_sems.at[a]) for a in range(n)]
        for cp in mine:
            cp.start()
        first = [copy(a, 0, me, sibling, True) for a in range(n)]
        first += [copy(a, 1 + j, me, (*chip, pc), True) for j, chip in enumerate(chips) for a in range(n)]
        for cp in first:
            cp.start()
        passed = []
        for j, chip in enumerate(chips):
            for a in range(n):
                copy(a, 1 + j, (*chip, pc), me).wait_recv()
                passed.append(copy(a, 4 + j, (*chip, pc), sibling))
                passed[-1].start()
        for a in range(n):
            copy(a, 0, sibling, me).wait_recv()
        for j, chip in enumerate(chips):
            for a in range(n):
                copy(a, 4 + j, (*chip, 1 - pc), me).wait_recv()
        for cp in first + passed:
            cp.wait_send()
        for cp in mine:
            cp.wait()

    return pl.pallas_call(
        body,
        name=name,
        out_shape=[jax.ShapeDtypeStruct((N_DEV,) + x.shape, x.dtype) for x in xs],
        in_specs=[_ANY] * n,
        out_specs=[_ANY] * n,
        scratch_shapes=[pltpu.SemaphoreType.DMA((n, 7)), pltpu.SemaphoreType.DMA((n, 7)), pltpu.SemaphoreType.DMA((n,))],
    )(*xs)


def all_to_all(xs, name):
    n = len(xs)

    def body(*refs):
        x_refs, out_refs = refs[:n], refs[n:2 * n]
        send_sems, recv_sems, local_sems = refs[2 * n:]
        px, py, pc = _my_pos()
        me = (px, py, pc)

        def flipped(k):
            kx, ky, kc = (k >> 2) & 1, (k >> 1) & 1, k & 1
            return (1 - px if kx else px, 1 - py if ky else py, 1 - pc if kc else pc)

        def copy(a, k):
            peer = flipped(k)
            return pltpu.make_async_remote_copy(
                src_ref=x_refs[a].at[_slot_of(peer)], dst_ref=out_refs[a].at[_slot_of(me)],
                send_sem=send_sems.at[a, k - 1], recv_sem=recv_sems.at[a, k - 1],
                device_id=peer, device_id_type=_MESH)

        def landing(a, k):
            peer = flipped(k)
            return pltpu.make_async_remote_copy(
                src_ref=x_refs[a].at[_slot_of(me)], dst_ref=out_refs[a].at[_slot_of(peer)],
                send_sem=send_sems.at[a, k - 1], recv_sem=recv_sems.at[a, k - 1],
                device_id=peer, device_id_type=_MESH)

        mine = [pltpu.make_async_copy(x_refs[a].at[_slot_of(me)], out_refs[a].at[_slot_of(me)], local_sems.at[a]) for a in range(n)]
        for cp in mine:
            cp.start()
        copies = [copy(a, k) for a in range(n) for k in range(1, N_DEV)]
        for cp in copies:
            cp.start()
        for a in range(n):
            for k in range(1, N_DEV):
                landing(a, k).wait_recv()
        for cp in copies:
            cp.wait_send()
        for cp in mine:
            cp.wait()

    return pl.pallas_call(
        body,
        name=name,
        out_shape=[jax.ShapeDtypeStruct(x.shape, x.dtype) for x in xs],
        in_specs=[_ANY] * n,
        out_specs=[_ANY] * n,
        scratch_shapes=[pltpu.SemaphoreType.DMA((n, 7)), pltpu.SemaphoreType.DMA((n, 7)), pltpu.SemaphoreType.DMA((n,))],
    )(*xs)


def sum_slots(x, name):
    n, r, c = x.shape
    tr = _pick(r, (512, 256, 128, 64, 32, 16, 8))

    def body(x_ref, o_ref):
        acc = x_ref[0].astype(F32)
        for i in range(1, n):
            acc = acc + x_ref[i].astype(F32)
        o_ref[...] = acc

    return pl.pallas_call(
        body,
        name=name,
        grid=(r // tr,),
        in_specs=[pl.BlockSpec((n, tr, c), lambda i: (0, i, 0))],
        out_specs=pl.BlockSpec((tr, c), lambda i: (i, 0)),
        out_shape=jax.ShapeDtypeStruct((r, c), F32),
        compiler_params=_cparams(("parallel",)),
    )(x)


CONV_CH_TILE = 256


def _shift_rows(x, off):
    n = x.shape[0]
    if off % n == 0:
        return x
    return pltpu.roll(x, (-off) % n, axis=0)


def _between(v, lo, hi):
    return jnp.where(v >= lo, 1.0, 0.0) * jnp.where(v < hi, 1.0, 0.0)


def taps_1d(ntaps, s_ctx, s_tot):
    def mask(off):
        def m(t):
            is_ctx = _between(t, 0, s_ctx)
            return is_ctx * _between(t + off, 0, s_ctx) + (1.0 - is_ctx) * _between(t + off, s_ctx, s_tot)
        return m
    return [(j - (ntaps - 1) // 2, mask(j - (ntaps - 1) // 2)) for j in range(ntaps)]


def taps_grid(s_ctx, s_tot, grid_w):
    assert s_ctx % grid_w == 0

    def mask(dr, dc):
        def m(t):
            is_ctx = _between(t, 0, s_ctx)
            lat = _between(t % grid_w + dc, 0, grid_w) * _between(t + grid_w * dr, s_ctx, s_tot)
            ctx = _between(t + dc, 0, s_ctx) if dr == 0 else 0.0
            return is_ctx * ctx + (1.0 - is_ctx) * lat
        return m
    return [(grid_w * dr + dc, mask(dr, dc)) for dr in (-1, 0, 1) for dc in (-1, 0, 1)]


def _conv_acc(x, w_ref, b_ref, taps, masks):
    acc = jnp.broadcast_to(b_ref[...], x.shape)
    for k, (off, _) in enumerate(taps):
        acc = acc + w_ref[k:k + 1, :] * (_shift_rows(x, off) * masks[k])
    return acc


def _tap_masks(taps, s):
    t = lax.broadcasted_iota(jnp.int32, (s, 1), 0)
    return [m(t) for _, m in taps]


def conv_fwd(name, x, w, b, taps, mode, mul=None, out_dtype=F32):
    nb, s, c = x.shape
    ct = _pick(c, (CONV_CH_TILE, 128))
    has_mul = mode == "silu_mul"

    def body(*refs):
        x_ref, w_ref, b_ref = refs[:3]
        o_ref = refs[-1]
        acc = _conv_acc(x_ref[...], w_ref, b_ref, taps, _tap_masks(taps, s))
        if mode == "none":
            out = acc
        else:
            out = _silu(acc)
            if has_mul:
                out = out * refs[3][...].astype(F32)
        o_ref[...] = out.astype(o_ref.dtype)

    blk = pl.BlockSpec((None, s, ct), lambda bb, j: (bb, 0, j))
    par = lambda k: pl.BlockSpec((k, ct), lambda bb, j: (0, j))
    return pl.pallas_call(
        body,
        name=name,
        grid=(nb, c // ct),
        in_specs=[blk, par(w.shape[0]), par(1)] + ([blk] if has_mul else []),
        out_specs=blk,
        out_shape=jax.ShapeDtypeStruct(x.shape, out_dtype),
        compiler_params=_cparams(("parallel", "parallel")),
    )(x, w, b, *([mul] if has_mul else []))


def conv_bwd(name, x, w, b, dout, taps, mode, mul=None, dx_dtype=F32):
    nb, s, c = x.shape
    ct = _pick(c, (CONV_CH_TILE, 128))
    has_mul = mode == "silu_mul"
    nk = w.shape[0]

    def body(*refs):
        x_ref, w_ref, b_ref, do_ref = refs[:4]
        n_in = 5 if has_mul else 4
        dx_ref, dw_ref, db_ref = refs[n_in:n_in + 3]
        bb = pl.program_id(1)
        masks = _tap_masks(taps, s)
        x = x_ref[...]
        dacc = do_ref[...].astype(F32)
        if mode != "none":
            acc = _conv_acc(x, w_ref, b_ref, taps, masks)
            sg = jax.nn.sigmoid(acc)
            if has_mul:
                refs[n_in + 3][...] = (dacc * (acc * sg)).astype(refs[n_in + 3].dtype)
                dacc = dacc * refs[4][...].astype(F32)
            dacc = dacc * (sg * (1.0 + acc * (1.0 - sg)))
        dx = jnp.zeros_like(x)
        dws = []
        for k, (off, _) in enumerate(taps):
            dm = dacc * masks[k]
            dx = dx + _shift_rows(w_ref[k:k + 1, :] * dm, -off)
            dws.append(jnp.sum(dm * _shift_rows(x, off), axis=0, keepdims=True))
        dx_ref[...] = dx.astype(dx_ref.dtype)
        db = jnp.sum(dacc, axis=0, keepdims=True)

        @pl.when(bb == 0)
        def _():
            for k in range(nk):
                dw_ref[k:k + 1, :] = dws[k]
            db_ref[...] = db

        @pl.when(bb > 0)
        def _():
            for k in range(nk):
                dw_ref[k:k + 1, :] += dws[k]
            db_ref[...] += db

    blk = pl.BlockSpec((None, s, ct), lambda j, bb: (bb, 0, j))
    par = lambda k: pl.BlockSpec((k, ct), lambda j, bb: (0, j))
    out_specs = [blk, par(nk), par(1)] + ([blk] if has_mul else [])
    out_shape = [jax.ShapeDtypeStruct(x.shape, dx_dtype), jax.ShapeDtypeStruct(w.shape, F32), jax.ShapeDtypeStruct(b.shape, F32)]
    if has_mul:
        out_shape.append(jax.ShapeDtypeStruct(x.shape, dx_dtype))
    return pl.pallas_call(
        body,
        name=name,
        grid=(c // ct, nb),
        in_specs=[blk, par(nk), par(1), blk] + ([blk] if has_mul else []),
        out_specs=out_specs,
        out_shape=out_shape,
        compiler_params=_cparams(("parallel", "arbitrary")),
    )(x, w, b, dout, *([mul] if has_mul else []))


SCAN_UNROLL = 4


def _scan_order(direction, adjoint, s_ctx, s_tot):
    nc, nt = s_ctx // SUBLANES, s_tot // SUBLANES
    if direction == 0:
        return ([(0, nt, 1)], False) if not adjoint else ([(nt - 1, nt, -1)], True)
    if not adjoint:
        return [(nc - 1, nc, -1), (nt - 1, nt - nc, -1)], True
    return [(nc, nt - nc, 1), (0, nc, 1)], False


def _last_row(h, descending):
    row = lax.broadcasted_iota(jnp.int32, h.shape, 0)
    pick = 0 if descending else SUBLANES - 1
    return jnp.sum(jnp.where(row == pick, h, 0.0), axis=0, keepdims=True)


def _prev_rows(h, carry, descending):
    row = lax.broadcasted_iota(jnp.int32, h.shape, 0)
    if descending:
        return jnp.where(row == SUBLANES - 1, carry, pltpu.roll(h, SUBLANES - 1, axis=0))
    return jnp.where(row == 0, carry, pltpu.roll(h, 1, axis=0))


def _scan_real(a_ref, x_ref, h_ref, hp_ref, order):
    ranges, descending = order
    width = a_ref.shape[1]
    row = lax.broadcasted_iota(jnp.int32, (SUBLANES, width), 0)

    def tile(i, carry):
        t0 = pl.multiple_of(i * SUBLANES, SUBLANES)
        a = a_ref[pl.ds(t0, SUBLANES), :]
        x = x_ref[pl.ds(t0, SUBLANES), :]
        for k in (1, 2, 4):
            sh = SUBLANES - k if descending else k
            keep = (row < SUBLANES - k) if descending else (row >= k)
            x = jnp.where(keep, a * pltpu.roll(x, sh, axis=0) + x, x)
            a = jnp.where(keep, a * pltpu.roll(a, sh, axis=0), a)
        h = a * carry + x
        if h_ref is not None:
            h_ref[pl.ds(t0, SUBLANES), :] = h
        if hp_ref is not None:
            hp_ref[pl.ds(t0, SUBLANES), :] = _prev_rows(h, carry, descending)
        return _last_row(h, descending)

    carry = jnp.zeros((1, width), F32)
    for first, count, step in ranges:
        carry = lax.fori_loop(0, count, lambda j, c, first=first, step=step: tile(first + step * j, c), carry,
                              unroll=SCAN_UNROLL if count % SCAN_UNROLL == 0 else 1)


def _cmul(ar, ai, br, bi):
    return ar * br - ai * bi, ar * bi + ai * br


def _scan_cplx(lr, li, xr_ref, xi_ref, hpr_ref, hpi_ref, order):
    ranges, descending = order
    width = xr_ref.shape[1]
    row = lax.broadcasted_iota(jnp.int32, (SUBLANES, width), 0)
    pw = [(lr, li)]
    for _ in range(SUBLANES - 1):
        pw.append(_cmul(pw[-1][0], pw[-1][1], lr, li))
    pr = jnp.zeros((SUBLANES, width), F32)
    pi = jnp.zeros((SUBLANES, width), F32)
    for r in range(SUBLANES):
        n = SUBLANES - 1 - r if descending else r
        pr = jnp.where(row == r, pw[n][0], pr)
        pi = jnp.where(row == r, pw[n][1], pi)

    def tile(i, carry):
        cr, ci = carry
        t0 = pl.multiple_of(i * SUBLANES, SUBLANES)
        xr = xr_ref[pl.ds(t0, SUBLANES), :]
        xi = xi_ref[pl.ds(t0, SUBLANES), :]
        for k in (1, 2, 4):
            sh = SUBLANES - k if descending else k
            keep = (row < SUBLANES - k) if descending else (row >= k)
            sr, si = _cmul(pw[k - 1][0], pw[k - 1][1], pltpu.roll(xr, sh, axis=0), pltpu.roll(xi, sh, axis=0))
            xr = jnp.where(keep, xr + sr, xr)
            xi = jnp.where(keep, xi + si, xi)
        hr, hi = _cmul(pr, pi, cr, ci)
        hr, hi = hr + xr, hi + xi
        xr_ref[pl.ds(t0, SUBLANES), :] = hr
        xi_ref[pl.ds(t0, SUBLANES), :] = hi
        if hpr_ref is not None:
            hpr_ref[pl.ds(t0, SUBLANES), :] = _prev_rows(hr, cr, descending)
            hpi_ref[pl.ds(t0, SUBLANES), :] = _prev_rows(hi, ci, descending)
        return _last_row(hr, descending), _last_row(hi, descending)

    carry = (jnp.zeros((1, width), F32), jnp.zeros((1, width), F32))
    for first, count, step in ranges:
        carry = lax.fori_loop(0, count, lambda j, c, first=first, step=step: tile(first + step * j, c), carry,
                              unroll=SCAN_UNROLL if count % SCAN_UNROLL == 0 else 1)


def _log1p_pos(y):
    return jnp.where(y < 0.01, y * (1.0 - y * (0.5 - y * (1.0 / 3.0 - 0.25 * y))), jnp.log(1.0 + y))


def _softplus(x):
    return jnp.maximum(x, 0.0) + _log1p_pos(jnp.exp(-jnp.abs(x)))


def _neg_expm1(z):
    series = -z * (1.0 + z * (0.5 + z * (1.0 / 6.0 + z * (1.0 / 24.0 + z * (1.0 / 120.0)))))
    return jnp.where(z > -0.1, series, 1.0 - jnp.exp(z))


def _lru_gates(u, w_a, b_a, w_i, b_i, lam):
    ub = u.astype(BF16)
    r = jax.nn.sigmoid(jnp.dot(ub, w_a.astype(BF16), preferred_element_type=F32) + b_a)
    i = jax.nn.sigmoid(jnp.dot(ub, w_i.astype(BF16), preferred_element_type=F32) + b_i)
    log_a = (-LRU_C) * _softplus(-lam) * r
    return jnp.exp(log_a), jnp.sqrt(_neg_expm1(2.0 * log_a)) * (i * u)


LRU_PER_STEP = 4
LRU_PER_STEP_BWD = 2


def _lru_specs(per, bw, order):
    w = pl.BlockSpec((2, per, bw, bw), lambda *g: (0, order(*g), 0, 0))
    v = pl.BlockSpec((2, per, 1, bw), lambda *g: (0, order(*g), 0, 0))
    return [w, v, w, v, v]


def lru_fwd(name, u, w_a, b_a, w_i, b_i, lam, s_ctx):
    nb, s, _ = u.shape
    nblk, bw = w_a.shape[1], w_a.shape[2]
    per = min(LRU_PER_STEP, nblk)

    def body(u_ref, wa, ba, wi, bi, lm, o_ref, a_s, x_s, h_s):
        for d in (0, 1):
            for k in range(per):
                cols = slice(k * bw, (k + 1) * bw)
                a, bx = _lru_gates(u_ref[:, cols], wa[d, k], ba[d, k], wi[d, k], bi[d, k], lm[d, k])
                a_s[:, cols] = a
                x_s[:, cols] = bx
            _scan_real(a_s, x_s, h_s, None, _scan_order(d, False, s_ctx, s))
            if d == 0:
                o_ref[...] = h_s[...]
            else:
                o_ref[...] += h_s[...]

    blk = pl.BlockSpec((None, s, per * bw), lambda b, n: (b, 0, n))
    return pl.pallas_call(
        body,
        name=name,
        grid=(nb, nblk // per),
        in_specs=[blk] + _lru_specs(per, bw, lambda b, n: n),
        out_specs=blk,
        out_shape=jax.ShapeDtypeStruct(u.shape, F32),
        scratch_shapes=[pltpu.VMEM((s, per * bw), F32)] * 3,
        compiler_params=_cparams(("parallel", "parallel")),
    )(u, w_a, b_a, w_i, b_i, lam)


def lru_bwd(name, u, w_a, b_a, w_i, b_i, lam, dh, s_ctx):
    nb, s, _ = u.shape
    nblk, bw = w_a.shape[1], w_a.shape[2]
    per = min(LRU_PER_STEP_BWD, nblk)

    def body(u_ref, wa, ba, wi, bi, lm, dh_ref, du_ref, dwa, dba, dwi, dbi, dlm, a_s, x_s, hp_s, wp_s):
        b = pl.program_id(1)
        for d in (0, 1):
            for k in range(per):
                cols = slice(k * bw, (k + 1) * bw)
                a, bx = _lru_gates(u_ref[:, cols], wa[d, k], ba[d, k], wi[d, k], bi[d, k], lm[d, k])
                a_s[:, cols] = a
                x_s[:, cols] = bx
            _scan_real(a_s, x_s, None, hp_s, _scan_order(d, False, s_ctx, s))
            x_s[...] = a_s[...] * dh_ref[...]
            _scan_real(a_s, x_s, None, wp_s, _scan_order(d, True, s_ctx, s))
            for k in range(per):
                cols = slice(k * bw, (k + 1) * bw)
                g = dh_ref[:, cols] + wp_s[:, cols]
                _, vjp = jax.vjp(_lru_gates, u_ref[:, cols], wa[d, k], ba[d, k], wi[d, k], bi[d, k], lm[d, k])
                grads = vjp((g * hp_s[:, cols], g))
                if d == 0:
                    du_ref[:, cols] = grads[0]
                else:
                    du_ref[:, cols] += grads[0]
                for ref, val in zip((dwa, dba, dwi, dbi, dlm), grads[1:]):
                    @pl.when(b == 0)
                    def _(ref=ref, val=val, k=k):
                        ref[d, k] = val

                    @pl.when(b > 0)
                    def _(ref=ref, val=val, k=k):
                        ref[d, k] += val

    blk = pl.BlockSpec((None, s, per * bw), lambda n, b: (b, 0, n))
    pspecs = _lru_specs(per, bw, lambda n, b: n)
    return pl.pallas_call(
        body,
        name=name,
        grid=(nblk // per, nb),
        in_specs=[blk] + pspecs + [blk],
        out_specs=[blk] + pspecs,
        out_shape=[jax.ShapeDtypeStruct(u.shape, F32)] + [jax.ShapeDtypeStruct(p.shape, F32) for p in (w_a, b_a, w_i, b_i, lam)],
        scratch_shapes=[pltpu.VMEM((s, per * bw), F32)] * 4,
        compiler_params=_cparams(("parallel", "arbitrary")),
    )(u, w_a, b_a, w_i, b_i, lam, dh)


S5_TILE_CH = 128
S5_TILE_STATES = S5_TILE_CH // S5_GROUP_CH * S5_STATE


def _dot_nt(a, b):
    return lax.dot_general(a, b, (((1,), (1,)), ((), ())), preferred_element_type=F32)


def _dot_tn(a, b):
    return lax.dot_general(a, b, (((0,), (0,)), ((), ())), preferred_element_type=F32)


def _s5_specs(order):
    lam = pl.BlockSpec((2, 1, S5_TILE_STATES), lambda *g: (0, 0, order(*g)))
    mat = pl.BlockSpec((2, None, S5_TILE_STATES, S5_TILE_CH), lambda *g: (0, order(*g), 0, 0))
    return [lam, lam, mat, mat, mat, mat]


def s5_fwd(name, u, lam_r, lam_i, bt_r, bt_i, ct_r, ct_i, s_ctx):
    nb, s, w = u.shape

    def body(u_ref, lr, li, btr, bti, ctr, cti, o_ref, xr_s, xi_s):
        ub = u_ref[...].astype(BF16)
        for d in (0, 1):
            xr_s[...] = _dot_nt(ub, btr[d].astype(BF16))
            xi_s[...] = _dot_nt(ub, bti[d].astype(BF16))
            _scan_cplx(lr[d], li[d], xr_s, xi_s, None, None, _scan_order(d, False, s_ctx, s))
            y = (jnp.dot(xr_s[...].astype(BF16), ctr[d].astype(BF16), preferred_element_type=F32)
                 - jnp.dot(xi_s[...].astype(BF16), cti[d].astype(BF16), preferred_element_type=F32))
            if d == 0:
                o_ref[...] = y
            else:
                o_ref[...] += y

    blk = pl.BlockSpec((None, s, S5_TILE_CH), lambda b, j: (b, 0, j))
    return pl.pallas_call(
        body,
        name=name,
        grid=(nb, w // S5_TILE_CH),
        in_specs=[blk] + _s5_specs(lambda b, j: j),
        out_specs=blk,
        out_shape=jax.ShapeDtypeStruct(u.shape, F32),
        scratch_shapes=[pltpu.VMEM((s, S5_TILE_STATES), F32)] * 2,
        compiler_params=_cparams(("parallel", "parallel")),
    )(u, lam_r, lam_i, bt_r, bt_i, ct_r, ct_i)


def s5_bwd(name, u, lam_r, lam_i, bt_r, bt_i, ct_r, ct_i, dy, s_ctx):
    nb, s, w = u.shape

    def body(u_ref, lr, li, btr, bti, ctr, cti, dy_ref, du_ref, dlr, dli, dbtr, dbti, dctr, dcti,
             hr_s, hi_s, hpr_s, hpi_s, gr_s, gi_s):
        b = pl.program_id(1)
        ub = u_ref[...].astype(BF16)
        dyb = dy_ref[...].astype(BF16)
        du = jnp.zeros((s, S5_TILE_CH), F32)
        for d in (0, 1):
            hr_s[...] = _dot_nt(ub, btr[d].astype(BF16))
            hi_s[...] = _dot_nt(ub, bti[d].astype(BF16))
            _scan_cplx(lr[d], li[d], hr_s, hi_s, hpr_s, hpi_s, _scan_order(d, False, s_ctx, s))
            d_ctr = _dot_tn(hr_s[...].astype(BF16), dyb)
            d_cti = -_dot_tn(hi_s[...].astype(BF16), dyb)
            gr_s[...] = _dot_nt(dyb, ctr[d].astype(BF16))
            gi_s[...] = -_dot_nt(dyb, cti[d].astype(BF16))
            _scan_cplx(lr[d], -li[d], gr_s, gi_s, None, None, _scan_order(d, True, s_ctx, s))
            gr, gi = gr_s[...], gi_s[...]
            hpr, hpi = hpr_s[...], hpi_s[...]
            d_lr = jnp.sum(gr * hpr + gi * hpi, axis=0, keepdims=True)
            d_li = jnp.sum(gi * hpr - gr * hpi, axis=0, keepdims=True)
            grb, gib = gr.astype(BF16), gi.astype(BF16)
            du = du + jnp.dot(grb, btr[d].astype(BF16), preferred_element_type=F32)
            du = du + jnp.dot(gib, bti[d].astype(BF16), preferred_element_type=F32)
            d_btr = _dot_tn(grb, ub)
            d_bti = _dot_tn(gib, ub)
            for ref, val in zip((dlr, dli, dbtr, dbti, dctr, dcti), (d_lr, d_li, d_btr, d_bti, d_ctr, d_cti)):
                @pl.when(b == 0)
                def _(ref=ref, val=val):
                    ref[d] = val

                @pl.when(b > 0)
                def _(ref=ref, val=val):
                    ref[d] += val
        du_ref[...] = du

    blk = pl.BlockSpec((None, s, S5_TILE_CH), lambda j, b: (b, 0, j))
    pspecs = _s5_specs(lambda j, b: j)
    params = (lam_r, lam_i, bt_r, bt_i, ct_r, ct_i)
    return pl.pallas_call(
        body,
        name=name,
        grid=(w // S5_TILE_CH, nb),
        in_specs=[blk] + pspecs + [blk],
        out_specs=[blk] + pspecs,
        out_shape=[jax.ShapeDtypeStruct(u.shape, F32)] + [jax.ShapeDtypeStruct(p.shape, F32) for p in params],
        scratch_shapes=[pltpu.VMEM((s, S5_TILE_STATES), F32)] * 6,
        compiler_params=_cparams(("parallel", "arbitrary")),
    )(u, lam_r, lam_i, bt_r, bt_i, ct_r, ct_i, dy)


def small_fwd(name, f, ins, out_shapes):
    n = len(ins)

    def body(*refs):
        for o, r in zip(refs[n:], f([r[...] for r in refs[:n]])):
            o[...] = r

    return pl.pallas_call(
        body, name=name,
        out_shape=[jax.ShapeDtypeStruct(s, F32) for s in out_shapes],
        compiler_params=_cparams(),
    )(*ins)


def small_bwd(name, f, ins, cots):
    n, nc = len(ins), len(cots)

    def body(*refs):
        _, vjp = jax.vjp(f, [r[...] for r in refs[:n]])
        (grads,) = vjp([r[...] for r in refs[n:n + nc]])
        for o, r in zip(refs[n + nc:], grads):
            o[...] = r

    return pl.pallas_call(
        body, name=name,
        out_shape=[jax.ShapeDtypeStruct(a.shape, F32) for a in ins],
        compiler_params=_cparams(),
    )(*ins, *cots)


def _row(x, r):
    return jnp.sum(jnp.where(lax.broadcasted_iota(jnp.int32, x.shape, 0) == r, x, 0.0), axis=0, keepdims=True)


def _col(x, c):
    return jnp.sum(jnp.where(lax.broadcasted_iota(jnp.int32, x.shape, 1) == c, x, 0.0), axis=1, keepdims=True)


def _chunk_at(i, reverse, ncc, nc):
    if not reverse:
        return i
    return jnp.where(i < ncc, ncc - 1 - i, nc - 1 - (i - ncc))


def _tri(n, reverse):
    li = lax.broadcasted_iota(jnp.int32, (n, n), 0)
    si = lax.broadcasted_iota(jnp.int32, (n, n), 1)
    return jnp.where((li <= si) if reverse else (li >= si), 1.0, 0.0)


_HI = lax.Precision.HIGHEST


def _ssd_chunk(xs, bm, cm, dtc, dtr, a_row, a_col, hs, reverse):
    n = bm.shape[0]
    last = 0 if reverse else n - 1
    tri = _tri(n, reverse)
    cum_c = jnp.dot(tri, dtc * -jnp.exp(a_row), precision=_HI, preferred_element_type=F32)
    cum_r = lax.dot_general(dtr * -jnp.exp(a_col), tri, (((1,), (1,)), ((), ())), precision=_HI, preferred_element_type=F32)
    tot_r = _row(cum_c, last)
    bmb, cmb = bm.astype(BF16), cm.astype(BF16)
    cb = _dot_nt(cmb, bmb)
    ys, hn = [], []
    for hd in range(len(xs)):
        cl = _col(cum_c, hd)
        tot = _col(tot_r, hd)
        decay = jnp.exp(jnp.where(tri > 0.0, cl - _row(cum_r, hd), -jnp.inf))
        xd = xs[hd] * _col(dtc, hd)
        y = jnp.dot((cb * decay).astype(BF16), xd.astype(BF16), preferred_element_type=F32)
        y = y + _dot_nt(cmb, hs[hd].astype(BF16)) * jnp.exp(cl)
        hnew = hs[hd] * jnp.exp(tot) + _dot_tn((xd * jnp.exp(tot - cl)).astype(BF16), bmb)
        ys.append(y)
        hn.append(hnew)
    return ys, hn


def _ssd_specs(reverse, ncc, nc, order):
    ch = lambda *g: _chunk_at(order(*g)[1], reverse, ncc, nc)
    b_ = lambda *g: order(*g)[0]
    gn = SSD_GROUPS * SSD_STATE
    return [
        pl.BlockSpec((None, SSD_CHUNK, SSD_W), lambda *g: (b_(*g), ch(*g), 0)),
        pl.BlockSpec((None, SSD_CHUNK, gn), lambda *g: (b_(*g), ch(*g), SSD_W // gn)),
        pl.BlockSpec((None, SSD_CHUNK, gn), lambda *g: (b_(*g), ch(*g), SSD_W // gn + 1)),
        pl.BlockSpec((None, SSD_GROUPS, SSD_CHUNK, SSD_HPG), lambda *g: (b_(*g), 0, ch(*g), 0)),
        pl.BlockSpec((None, SSD_GROUPS, SSD_HPG, SSD_CHUNK), lambda *g: (b_(*g), 0, 0, ch(*g))),
        pl.BlockSpec((SSD_GROUPS, 1, SSD_HPG), lambda *g: (0, 0, 0)),
        pl.BlockSpec((SSD_GROUPS, SSD_HPG, 1), lambda *g: (0, 0, 0)),
    ]


def _ssd_group_inputs(g, x_ref, bm_ref, cm_ref, dtc_ref, dtr_ref, ar_ref, ac_ref):
    p, n = SSD_HEAD_DIM, SSD_STATE
    xs = [x_ref[:, p * (SSD_HPG * g + hd):p * (SSD_HPG * g + hd + 1)] for hd in range(SSD_HPG)]
    return xs, bm_ref[:, n * g:n * (g + 1)], cm_ref[:, n * g:n * (g + 1)], dtc_ref[g], dtr_ref[g], ar_ref[g], ac_ref[g]


def ssd_fwd(name, xbc, dt_col, dt_row, a_row, a_col, reverse, s_ctx):
    nb, s, _ = xbc.shape
    nc, ncc = s // SSD_CHUNK, s_ctx // SSD_CHUNK
    p = SSD_HEAD_DIM

    def body(x_ref, bm_ref, cm_ref, dtc_ref, dtr_ref, ar_ref, ac_ref, y_ref, hst_ref, h_s):
        i = pl.program_id(1)

        @pl.when(i == 0)
        def _():
            h_s[...] = jnp.zeros_like(h_s)

        hst_ref[...] = h_s[...]
        for g in range(SSD_GROUPS):
            xs, bm, cm, dtc, dtr, ar, ac = _ssd_group_inputs(g, x_ref, bm_ref, cm_ref, dtc_ref, dtr_ref, ar_ref, ac_ref)
            ys, hn = _ssd_chunk(xs, bm, cm, dtc, dtr, ar, ac, [h_s[g, hd] for hd in range(SSD_HPG)], reverse)
            for hd in range(SSD_HPG):
                y_ref[:, p * (SSD_HPG * g + hd):p * (SSD_HPG * g + hd + 1)] = ys[hd]
                h_s[g, hd] = hn[hd]

    state = (SSD_GROUPS, SSD_HPG, SSD_HEAD_DIM, SSD_STATE)
    return pl.pallas_call(
        body,
        name=name,
        grid=(nb, nc),
        in_specs=_ssd_specs(reverse, ncc, nc, lambda b, i: (b, i)),
        out_specs=[pl.BlockSpec((None, SSD_CHUNK, SSD_W), lambda b, i: (b, _chunk_at(i, reverse, ncc, nc), 0)),
                   pl.BlockSpec((None, None) + state, lambda b, i: (b, i, 0, 0, 0, 0))],
        out_shape=[jax.ShapeDtypeStruct((nb, s, SSD_W), F32), jax.ShapeDtypeStruct((nb, nc) + state, F32)],
        scratch_shapes=[pltpu.VMEM(state, F32)],
        compiler_params=_cparams(("parallel", "arbitrary")),
    )(xbc, xbc, xbc, dt_col, dt_row, a_row, a_col)


def ssd_bwd(name, xbc, dt_col, dt_row, a_row, a_col, hst, dy, reverse, s_ctx):
    nb, s, _ = xbc.shape
    nc, ncc = s // SSD_CHUNK, s_ctx // SSD_CHUNK
    p, n = SSD_HEAD_DIM, SSD_STATE

    def body(x_ref, bm_ref, cm_ref, dtc_ref, dtr_ref, ar_ref, ac_ref, hst_ref, dy_ref,
             dx_ref, dbm_ref, dcm_ref, ddtc_ref, ddtr_ref, dar_ref, dac_ref, dh_s):
        i = pl.program_id(1)

        @pl.when(i == 0)
        def _():
            dh_s[...] = jnp.zeros_like(dh_s)

        for g in range(SSD_GROUPS):
            xs, bm, cm, dtc, dtr, ar, ac = _ssd_group_inputs(g, x_ref, bm_ref, cm_ref, dtc_ref, dtr_ref, ar_ref, ac_ref)
            hs = [hst_ref[g, hd] for hd in range(SSD_HPG)]
            _, vjp = jax.vjp(functools.partial(_ssd_chunk, reverse=reverse), xs, bm, cm, dtc, dtr, ar, ac, hs)
            dys = [dy_ref[:, p * (SSD_HPG * g + hd):p * (SSD_HPG * g + hd + 1)] for hd in range(SSD_HPG)]
            dxs, dbm, dcm, ddtc, ddtr, dar, dac, dhs = vjp((dys, [dh_s[g, hd] for hd in range(SSD_HPG)]))
            for hd in range(SSD_HPG):
                dx_ref[:, p * (SSD_HPG * g + hd):p * (SSD_HPG * g + hd + 1)] = dxs[hd]
                dh_s[g, hd] = dhs[hd]
            dbm_ref[:, n * g:n * (g + 1)] = dbm
            dcm_ref[:, n * g:n * (g + 1)] = dcm
            ddtc_ref[g] = ddtc
            ddtr_ref[g] = ddtr

            @pl.when(i == 0)
            def _(g=g, dar=dar, dac=dac):
                dar_ref[g] = dar
                dac_ref[g] = dac

            @pl.when(i > 0)
            def _(g=g, dar=dar, dac=dac):
                dar_ref[g] += dar
                dac_ref[g] += dac

    ch = lambda b, i: _chunk_at(nc - 1 - i, reverse, ncc, nc)
    state = (SSD_GROUPS, SSD_HPG, SSD_HEAD_DIM, SSD_STATE)
    gn = SSD_GROUPS * SSD_STATE
    in_specs = _ssd_specs(reverse, ncc, nc, lambda b, i: (b, nc - 1 - i)) + [
        pl.BlockSpec((None, None) + state, lambda b, i: (b, nc - 1 - i, 0, 0, 0, 0)),
        pl.BlockSpec((None, SSD_CHUNK, SSD_W), lambda b, i: (b, ch(b, i), 0))]
    out_specs = [
        pl.BlockSpec((None, SSD_CHUNK, SSD_W), lambda b, i: (b, ch(b, i), 0)),
        pl.BlockSpec((None, SSD_CHUNK, gn), lambda b, i: (b, ch(b, i), 0)),
        pl.BlockSpec((None, SSD_CHUNK, gn), lambda b, i: (b, ch(b, i), 0)),
        pl.BlockSpec((None, SSD_GROUPS, SSD_CHUNK, SSD_HPG), lambda b, i: (b, 0, ch(b, i), 0)),
        pl.BlockSpec((None, SSD_GROUPS, SSD_HPG, SSD_CHUNK), lambda b, i: (b, 0, 0, ch(b, i))),
        pl.BlockSpec((None, SSD_GROUPS, 1, SSD_HPG), lambda b, i: (b, 0, 0, 0)),
        pl.BlockSpec((None, SSD_GROUPS, SSD_HPG, 1), lambda b, i: (b, 0, 0, 0)),
    ]
    out_shape = [
        jax.ShapeDtypeStruct((nb, s, SSD_W), F32),
        jax.ShapeDtypeStruct((nb, s, gn), F32),
        jax.ShapeDtypeStruct((nb, s, gn), F32),
        jax.ShapeDtypeStruct(dt_col.shape, F32),
        jax.ShapeDtypeStruct(dt_row.shape, F32),
        jax.ShapeDtypeStruct((nb, SSD_GROUPS, 1, SSD_HPG), F32),
        jax.ShapeDtypeStruct((nb, SSD_GROUPS, SSD_HPG, 1), F32),
    ]
    return pl.pallas_call(
        body,
        name=name,
        grid=(nb, nc),
        in_specs=in_specs,
        out_specs=out_specs,
        out_shape=out_shape,
        scratch_shapes=[pltpu.VMEM(state, F32)],
        compiler_params=_cparams(("parallel", "arbitrary")),
    )(xbc, xbc, xbc, dt_col, dt_row, a_row, a_col, hst, dy)


HG_TILES = HG_CHUNK // SUBLANES


def _hg_cum_tiles(x_t, reverse):
    row = lax.broadcasted_iota(jnp.int32, x_t[0].shape, 0)
    out = [None] * len(x_t)
    off = None
    for i in (reversed(range(len(x_t))) if reverse else range(len(x_t))):
        c = x_t[i]
        for k in (1, 2, 4):
            keep = (row < SUBLANES - k) if reverse else (row >= k)
            c = jnp.where(keep, c + pltpu.roll(c, SUBLANES - k if reverse else k, axis=0), c)
        out[i] = c if off is None else c + off
        off = _last_row(out[i], reverse)
    return out, off


def _hg_pairs(reverse):
    row = lax.broadcasted_iota(jnp.int32, (SUBLANES, HG_DK), 0)
    rots = []
    for r in range(SUBLANES):
        rots.append(((SUBLANES - r) % SUBLANES, row <= SUBLANES - 1 - r) if reverse else (r, row >= r))
    return [(j, [i for i in range(HG_TILES) if (i <= j if reverse else i >= j)], rots) for j in range(HG_TILES)]


def _rot(x, sh):
    return pltpu.roll(x, sh, axis=0) if sh else x


def _cat(tiles):
    return jnp.concatenate(tiles, axis=0)


def _hg_chunk_fwd(q_t, k_t, lf_t, v_t, st, reverse):
    cum_t, tot = _hg_cum_tiles(lf_t, reverse)
    y_t = [jnp.zeros(v_t[0].shape, F32) for _ in v_t]
    for j, l_tiles, rots in _hg_pairs(reverse):
        for sh, diag_ok in rots:
            k_j, c_j, v_j = _rot(k_t[j], sh), _rot(cum_t[j], sh), _rot(v_t[j], sh)
            for i in l_tiles:
                e = jnp.exp(cum_t[i] - c_j)
                if i == j:
                    e = jnp.where(diag_ok, e, 0.0)
                att = jnp.sum(q_t[i] * (k_j * e), axis=1, keepdims=True)
                y_t[i] = y_t[i] + att * v_j
    q, k, v, cum = _cat(q_t), _cat(k_t), _cat(v_t), _cat(cum_t)
    y_state = _dot_nt((q * jnp.exp(cum)).astype(BF16), st.astype(BF16))
    st_new = st * jnp.exp(tot) + _dot_tn(v.astype(BF16), (k * jnp.exp(tot - cum)).astype(BF16))
    return [y + y_state[SUBLANES * i:SUBLANES * (i + 1)] for i, y in enumerate(y_t)], st_new


def _hg_chunk_bwd(q_t, k_t, lf_t, v_t, st, dy_t, dst_new, reverse):
    nt = len(q_t)
    cum_t, tot = _hg_cum_tiles(lf_t, reverse)
    q, k, v, cum, dy = _cat(q_t), _cat(k_t), _cat(v_t), _cat(cum_t), _cat(dy_t)
    e_cum, e_tot, e_end = jnp.exp(cum), jnp.exp(tot), jnp.exp(tot - cum)
    qt, khat = q * e_cum, k * e_end
    dyb, dsb = dy.astype(BF16), dst_new.astype(BF16)
    dqt = jnp.dot(dyb, st.astype(BF16), preferred_element_type=F32)
    dst = dst_new * e_tot + _dot_tn(dyb, qt.astype(BF16))
    dv = _dot_nt(khat.astype(BF16), dsb)
    dkhat = jnp.dot(v.astype(BF16), dsb, preferred_element_type=F32)
    t1 = dkhat * khat
    dtot = jnp.sum(dst_new * st, axis=0, keepdims=True) * e_tot + jnp.sum(t1, axis=0, keepdims=True)
    rows = lax.broadcasted_iota(jnp.int32, cum.shape, 0)
    last = 0 if reverse else cum.shape[0] - 1
    dcum = dqt * qt - t1 + jnp.where(rows == last, dtot, 0.0)
    tiles = lambda a: [a[SUBLANES * i:SUBLANES * (i + 1)] for i in range(nt)]
    dq_t, dk_t, dv_t, dcum_t = tiles(dqt * e_cum), tiles(dkhat * e_end), tiles(dv), tiles(dcum)
    for j, l_tiles, rots in _hg_pairs(reverse):
        for sh, diag_ok in rots:
            k_j, c_j, v_j = _rot(k_t[j], sh), _rot(cum_t[j], sh), _rot(v_t[j], sh)
            acc_v = acc_k = acc_c = None
            for i in l_tiles:
                e = jnp.exp(cum_t[i] - c_j)
                if i == j:
                    e = jnp.where(diag_ok, e, 0.0)
                ke, qe = k_j * e, q_t[i] * e
                p = q_t[i] * ke
                att = jnp.sum(p, axis=1, keepdims=True)
                datt = jnp.sum(dy_t[i] * v_j, axis=1, keepdims=True)
                g = datt * p
                dq_t[i] = dq_t[i] + datt * ke
                dcum_t[i] = dcum_t[i] + g
                av, ak = att * dy_t[i], datt * qe
                acc_v, acc_k, acc_c = (av, ak, g) if acc_v is None else (acc_v + av, acc_k + ak, acc_c + g)
            back = (SUBLANES - sh) % SUBLANES
            dv_t[j] = dv_t[j] + _rot(acc_v, back)
            dk_t[j] = dk_t[j] + _rot(acc_k, back)
            dcum_t[j] = dcum_t[j] - _rot(acc_c, back)
    dlf_t, _ = _hg_cum_tiles(dcum_t, not reverse)
    return dq_t, dk_t, dlf_t, dv_t, dst


def _hg_super(s_ctx):
    return min(256, s_ctx)


def hg_fwd(name, q, k, lf, v, reverse, s_ctx):
    nb, s, w = q.shape
    nh, dk, sup = w // HG_DK, HG_DK, _hg_super(s_ctx)
    nsup, nsc, cps = s // sup, s_ctx // sup, sup // HG_CHUNK

    def body(q_ref, k_ref, lf_ref, v_ref, y_ref, hst_ref, st_s):
        i = pl.program_id(2)

        @pl.when(i == 0)
        def _():
            st_s[...] = jnp.zeros_like(st_s)

        def step(c, st):
            r0 = pl.multiple_of((cps - 1 - c if reverse else c) * HG_CHUNK, HG_CHUNK)
            tile = lambda ref: [ref[pl.ds(r0 + SUBLANES * i, SUBLANES), :] for i in range(HG_TILES)]
            hst_ref[c] = st
            y_t, st_new = _hg_chunk_fwd(tile(q_ref), tile(k_ref), tile(lf_ref), tile(v_ref), st, reverse)
            for i in range(HG_TILES):
                y_ref[pl.ds(r0 + SUBLANES * i, SUBLANES), :] = y_t[i]
            return st_new

        st_s[...] = lax.fori_loop(0, cps, step, st_s[...], unroll=2 if cps % 2 == 0 else 1)

    blk = pl.BlockSpec((None, sup, dk), lambda b, h, i: (b, _chunk_at(i, reverse, nsc, nsup), h))
    return pl.pallas_call(
        body,
        name=name,
        grid=(nb, nh, nsup),
        in_specs=[blk] * 4,
        out_specs=[blk, pl.BlockSpec((None, None, cps, dk, dk), lambda b, h, i: (b, h, i, 0, 0))],
        out_shape=[jax.ShapeDtypeStruct(q.shape, F32), jax.ShapeDtypeStruct((nb, nh, s // HG_CHUNK, dk, dk), F32)],
        scratch_shapes=[pltpu.VMEM((dk, dk), F32)],
        compiler_params=_cparams(("parallel", "parallel", "arbitrary")),
    )(q, k, lf, v)


def hg_bwd(name, q, k, lf, v, hst, dy, reverse, s_ctx):
    nb, s, w = q.shape
    nh, dk, sup = w // HG_DK, HG_DK, _hg_super(s_ctx)
    nsup, nsc, cps = s // sup, s_ctx // sup, sup // HG_CHUNK

    def body(q_ref, k_ref, lf_ref, v_ref, hst_ref, dy_ref, dq_ref, dk_ref, dlf_ref, dv_ref, dst_s):
        i = pl.program_id(2)

        @pl.when(i == 0)
        def _():
            dst_s[...] = jnp.zeros_like(dst_s)

        def step(cc, dst):
            c = cps - 1 - cc
            r0 = pl.multiple_of((cps - 1 - c if reverse else c) * HG_CHUNK, HG_CHUNK)
            tile = lambda ref: [ref[pl.ds(r0 + SUBLANES * i, SUBLANES), :] for i in range(HG_TILES)]
            dq_t, dk_t, dlf_t, dv_t, dst_prev = _hg_chunk_bwd(
                tile(q_ref), tile(k_ref), tile(lf_ref), tile(v_ref), hst_ref[c], tile(dy_ref), dst, reverse)
            for ref, val in zip((dq_ref, dk_ref, dlf_ref, dv_ref), (dq_t, dk_t, dlf_t, dv_t)):
                for i in range(HG_TILES):
                    ref[pl.ds(r0 + SUBLANES * i, SUBLANES), :] = val[i]
            return dst_prev

        dst_s[...] = lax.fori_loop(0, cps, step, dst_s[...], unroll=2 if cps % 2 == 0 else 1)

    blk = pl.BlockSpec((None, sup, dk), lambda b, h, i: (b, _chunk_at(nsup - 1 - i, reverse, nsc, nsup), h))
    return pl.pallas_call(
        body,
        name=name,
        grid=(nb, nh, nsup),
        in_specs=[blk] * 4 + [pl.BlockSpec((None, None, cps, dk, dk), lambda b, h, i: (b, h, nsup - 1 - i, 0, 0)), blk],
        out_specs=[blk] * 4,
        out_shape=[jax.ShapeDtypeStruct(q.shape, F32)] * 4,
        scratch_shapes=[pltpu.VMEM((dk, dk), F32)],
        compiler_params=_cparams(("parallel", "parallel", "arbitrary")),
    )(q, k, lf, v, hst, dy)


def f_s5_discretize(ins):
    lam_re, lam_im, log_step, b_re, b_im = ins
    step = jnp.exp(log_step)
    mag = jnp.exp(lam_re * step)
    ar, ai = mag * jnp.cos(lam_im * step), mag * jnp.sin(lam_im * step)
    den = lam_re * lam_re + lam_im * lam_im
    zr = ((ar - 1.0) * lam_re + ai * lam_im) / den
    zi = (ai * lam_re - (ar - 1.0) * lam_im) / den
    return [ar, ai, zr * b_re - zi * b_im, zr * b_im + zi * b_re]


def s5_tiles_of(m):
    g, p, k = m.shape
    gt = S5_TILE_CH // k
    eye = jnp.eye(gt, dtype=m.dtype)
    t = m.reshape(g // gt, gt, p, 1, k) * eye[None, :, None, :, None]
    return t.reshape(g // gt, gt * p, gt * k)


def s5_groups_of(t, g, p, k):
    gt = S5_TILE_CH // k
    eye = jnp.eye(gt, dtype=t.dtype)
    return jnp.sum(t.reshape(g // gt, gt, p, gt, k) * eye[None, :, None, :, None], axis=3).reshape(g, p, k)


def f_lower_bounds(ins):
    (logits,) = ins
    e = jnp.exp(logits - jnp.max(logits, axis=0, keepdims=True))
    p = e / jnp.sum(e, axis=0, keepdims=True)
    n = logits.shape[0]
    li = lax.broadcasted_iota(jnp.int32, (n, n), 0)
    si = lax.broadcasted_iota(jnp.int32, (n, n), 1)
    after_first = jnp.where(jnp.logical_and(si >= 1, si <= li), 1.0, 0.0)
    return [jnp.dot(after_first, p, precision=_HI, preferred_element_type=F32)]


def f_silu(ins):
    return [_silu(ins[0])]


def f_norm_keep(shift_row, scale_row):
    def f(tv, mv, pv):
        return [tv[0], _rms(tv[0], pv[0]) * (1.0 + mv[scale_row]) + mv[shift_row]]
    return f


def f_dt(tv, mv, pv):
    return [_softplus(tv[0] + pv[0])]


def f_even_finish(tv, mv, pv):
    y_f, y_b, xs, z, h_sum, gy = tv
    d_exp, g = pv
    y = _rms((y_f + y_b + d_exp * xs) * _silu(z), g)
    return [y, h_sum * jax.nn.gelu(gy)]


def f_odd_prep(tv, mv, pv):
    q, f_f, f_b = tv
    (lb,) = pv
    outs = [_silu(q)]
    for f in (f_f, f_b):
        outs.append((1.0 - lb) * jax.nn.sigmoid(-f))
        outs.append(jnp.log(lb + (1.0 - lb) * jax.nn.sigmoid(f)))
    return outs


def f_odd_finish(tv, mv, pv):
    o_f, o_b, g, s5y, u = tv
    norm_g, s5_d, glu_w, glu_b = pv
    o = o_f + o_b
    w = o.shape[1]
    hi = lax.broadcasted_iota(jnp.int32, (w, w), 0) // HG_DK
    hj = lax.broadcasted_iota(jnp.int32, (w, w), 1) // HG_DK
    head_mean = jnp.where(hi == hj, 1.0 / HG_DK, 0.0)
    ms = jnp.dot(o * o, head_mean, precision=_HI, preferred_element_type=F32)
    on = o * lax.rsqrt(ms + RMS_EPS) * norm_g * _silu(g)
    y = jax.nn.gelu(s5y + s5_d * u)
    gate = jax.nn.sigmoid(jnp.dot(y.astype(BF16), glu_w.astype(BF16), preferred_element_type=F32) + glu_b)
    return [on, y * gate]


def final_loss(name, s, br, mod, g, target, s_ctx):
    nb, st, d = s.shape
    tb = TOK_BLOCK
    assert s_ctx == tb

    def lossf(sv, bv, gate, gv, tv):
        y = _rms(sv + gate * bv, gv)
        err = jnp.square(y - tv)
        return 0.5 * jnp.sum(jnp.mean(err, axis=-1, keepdims=True), axis=0, keepdims=True)

    def body(s_ref, b_ref, m_ref, g_ref, t_ref, l_ref, ds_ref, db_ref, dm_ref, dg_ref):
        b, t = pl.program_id(0), pl.program_id(1)

        @pl.when(t == 0)
        def _():
            ds_ref[...] = jnp.zeros_like(ds_ref)
            db_ref[...] = jnp.zeros_like(db_ref)
            dm_ref[...] = jnp.zeros_like(dm_ref)
            l_ref[...] = jnp.zeros_like(l_ref)

        @pl.when(jnp.logical_and(b == 0, t == 0))
        def _():
            dg_ref[...] = jnp.zeros_like(dg_ref)

        @pl.when(t > 0)
        def _():
            gate = m_ref[N_MOD - 1:N_MOD, :]
            l, vjp = jax.vjp(lossf, s_ref[...], b_ref[...], gate, g_ref[...], t_ref[...])
            ds, db, dgate, dg, _ = vjp(jnp.ones((1, 1), F32))
            ds_ref[...] = ds
            db_ref[...] = db.astype(db_ref.dtype)
            dg_ref[...] += dg
            l_ref[...] += jnp.broadcast_to(l, l_ref.shape)

            @pl.when(t == 1)
            def _():
                dm_ref[...] = jnp.zeros_like(dm_ref)
                dm_ref[N_MOD - 1:N_MOD, :] = dgate

            @pl.when(t > 1)
            def _():
                dm_ref[N_MOD - 1:N_MOD, :] += dgate

    tok = pl.BlockSpec((None, tb, d), lambda b, t: (b, t, 0))
    modspec = pl.BlockSpec((None, N_MOD, d), _mod_index)
    gspec = pl.BlockSpec((1, d), lambda b, t: (0, 0))
    return pl.pallas_call(
        body,
        name=name,
        grid=(nb, st // tb),
        in_specs=[tok, tok, modspec, gspec, pl.BlockSpec((None, tb, d), lambda b, t: (b, jnp.maximum(t - 1, 0), 0))],
        out_specs=[pl.BlockSpec((None, SUBLANES, 128), lambda b, t: (b, 0, 0)), tok, tok, modspec, gspec],
        out_shape=[jax.ShapeDtypeStruct((nb, SUBLANES, 128), F32), jax.ShapeDtypeStruct(s.shape, F32),
                   jax.ShapeDtypeStruct(s.shape, BF16), jax.ShapeDtypeStruct(mod.shape, F32), jax.ShapeDtypeStruct(g.shape, F32)],
        compiler_params=_cparams(("arbitrary", "arbitrary")),
    )(s, br, mod, g, target)


def adamw(name, w, g, m, v):
    shape = w.shape
    cols = shape[-1] if w.ndim >= 2 else w.size
    rows = w.size // cols
    tr = _pick(rows, (512, 256, 128, 64, 32, 16, 8))

    def body(w_ref, g_ref, m_ref, v_ref, d_ref, nm_ref, nv_ref):
        gv = g_ref[...]
        nm = ADAM_B1 * m_ref[...] + (1.0 - ADAM_B1) * gv
        nv = ADAM_B2 * v_ref[...] + (1.0 - ADAM_B2) * jnp.square(gv)
        m_hat = nm / (1.0 - ADAM_B1 ** ADAM_STEP)
        v_hat = nv / (1.0 - ADAM_B2 ** ADAM_STEP)
        d_ref[...] = -ADAM_LR * (m_hat / (jnp.sqrt(v_hat) + ADAM_EPS) + ADAM_WD * w_ref[...])
        nm_ref[...] = nm
        nv_ref[...] = nv

    spec = pl.BlockSpec((tr, cols), lambda i: (i, 0))
    outs = pl.pallas_call(
        body,
        name=name,
        grid=(rows // tr,),
        in_specs=[spec] * 4,
        out_specs=[spec] * 3,
        out_shape=[jax.ShapeDtypeStruct((rows, cols), F32)] * 3,
        compiler_params=_cparams(("parallel",)),
    )(*(a.reshape(rows, cols) for a in (w, g, m, v)))
    return tuple(o.reshape(shape) for o in outs)


EV_COLS = {"z": (0, 1024), "xbc": (1024, 2560), "dt": (2560, 2592), "gy": (2592, 3616), "u": (3616, 4640)}
OD_COLS = {"q": (0, 768), "ff": (768, 1536), "fb": (1536, 2304), "v": (2304, 3072), "g": (3072, 3840), "u": (3840, 4096)}
EV_OUT_ROWS = ((0, 1024), (1024, 2048))
OD_OUT_ROWS = ((0, 768), (768, 1024))
LANES = 128


def _pad_to_lanes(w):
    n = w.shape[1]
    return w if n % LANES == 0 else jnp.pad(w, ((0, 0), (0, LANES - n % LANES)))


def _layer_weights(l, big):
    j = l // 2
    even = l % 2 == 0
    w_in = big["ev_w_in" if even else "od_w_in"][j]
    w_out = big["ev_w_out" if even else "od_w_out"][j]
    lw = {"in": {}, "out": []}
    for name, (a, b) in (EV_COLS if even else OD_COLS).items():
        w = _pad_to_lanes(w_in[:, a:b])
        lw["in"][name] = (w, w.T)
    for a, b in (EV_OUT_ROWS if even else OD_OUT_ROWS):
        lw["out"].append((w_out[a:b], w_out[a:b].T))
    for name in ("gate", "up", "down"):
        w = big["ffn_w_" + name][l]
        lw[name] = (w, w.T)
    return lw


def _rows2d(a):
    return a.reshape(-1, a.shape[-1])


def _mm3(a, w, name, out_dtype=F32):
    return mm([(_rows2d(a), w)], name, out_dtype).reshape(a.shape[:-1] + (w.shape[1],))


def _wgrad(a, d, name):
    return mm([(_rows2d(a).T, _rows2d(d))], name)


def _dgrad(pairs, name, shape3):
    return mm([(_rows2d(d), wt) for d, wt in pairs], name).reshape(shape3[:-1] + (pairs[0][1].shape[1],))


def _dir_dt(dt, d):
    nb, s, _ = dt.shape
    dd = dt[:, :, SSD_HEADS * d:SSD_HEADS * (d + 1)].reshape(nb, s, SSD_GROUPS, SSD_HPG)
    return jnp.transpose(dd, (0, 2, 1, 3)), jnp.transpose(dd, (0, 2, 3, 1))


def _s5_prepare(p, j, tag):
    g_, p_, k_ = S5_GROUPS, S5_STATE, S5_GROUP_CH
    col = lambda t: t.reshape(g_ * p_, 1)
    ins, outs = [], []
    for d in (0, 1):
        i_d = [col(p["s5_lam_re"][j, d]), col(p["s5_lam_im"][j, d]), col(jnp.repeat(p["s5_log_step"][j, d], p_)),
               p["s5_b_re"][j].reshape(g_ * p_, k_), p["s5_b_im"][j].reshape(g_ * p_, k_)]
        ins.append(i_d)
        outs.append(small_fwd(f"{tag}_disc{d}", f_s5_discretize, i_d, [(g_ * p_, 1)] * 2 + [(g_ * p_, k_)] * 2))
    lam_r = jnp.stack([o[0].reshape(1, g_ * p_) for o in outs])
    lam_i = jnp.stack([o[1].reshape(1, g_ * p_) for o in outs])
    bt_r = jnp.stack([s5_tiles_of(o[2].reshape(g_, p_, k_)) for o in outs])
    bt_i = jnp.stack([s5_tiles_of(o[3].reshape(g_, p_, k_)) for o in outs])
    ct_r = jnp.stack([s5_tiles_of(jnp.transpose(p["s5_c_re"][j, d], (0, 2, 1))) for d in (0, 1)])
    ct_i = jnp.stack([s5_tiles_of(jnp.transpose(p["s5_c_im"][j, d], (0, 2, 1))) for d in (0, 1)])
    return ins, (lam_r, lam_i, bt_r, bt_i, ct_r, ct_i)


def _s5_param_grads(ins, grads, tag):
    g_, p_, k_ = S5_GROUPS, S5_STATE, S5_GROUP_CH
    dlr, dli, dbtr, dbti, dctr, dcti = grads
    g_lre, g_lim, g_ls, g_bre, g_bim = [], [], [], 0.0, 0.0
    for d in (0, 1):
        cots = [dlr[d].reshape(g_ * p_, 1), dli[d].reshape(g_ * p_, 1),
                s5_groups_of(dbtr[d], g_, p_, k_).reshape(g_ * p_, k_), s5_groups_of(dbti[d], g_, p_, k_).reshape(g_ * p_, k_)]
        g = small_bwd(f"{tag}_disc_bwd{d}", f_s5_discretize, ins[d], cots)
        g_lre.append(g[0].reshape(g_, p_))
        g_lim.append(g[1].reshape(g_, p_))
        g_ls.append(g[2].reshape(g_, p_).sum(-1))
        g_bre = g_bre + g[3].reshape(g_, p_, k_)
        g_bim = g_bim + g[4].reshape(g_, p_, k_)
    g_cre = jnp.stack([jnp.transpose(s5_groups_of(dctr[d], g_, p_, k_), (0, 2, 1)) for d in (0, 1)])
    g_cim = jnp.stack([jnp.transpose(s5_groups_of(dcti[d], g_, p_, k_), (0, 2, 1)) for d in (0, 1)])
    return jnp.stack(g_lre), jnp.stack(g_lim), jnp.stack(g_ls), g_bre, g_bim, g_cre, g_cim


def _even_mixer_fwd(l, hn, p, lw, s_ctx):
    j = l // 2
    t1 = taps_1d(4, s_ctx, hn.shape[1])
    r = {"hn": hn}
    proj = {n: _mm3(hn, lw["in"][n][0], f"l{l}_proj_{n}") for n in EV_COLS}
    r["z"], r["xbc"], r["gy"], r["u"] = proj["z"], proj["xbc"], proj["gy"], proj["u"]
    r["dtp"] = proj["dt"][:, :, :2 * SSD_HEADS]
    r["xbc_c"] = conv_fwd(f"l{l}_ssd_conv", r["xbc"], p["ssd_conv_w"][j], p["ssd_conv_b"][j][None], t1, "silu")
    r["u_c"] = conv_fwd(f"l{l}_lru_conv", r["u"], p["lru_conv_w"][j], p["lru_conv_b"][j][None], t1, "none")
    r["dt_bias"] = p["ssd_dt_bias"][j].reshape(1, 2 * SSD_HEADS)
    (r["dt"],) = tok_fwd(f"l{l}_dt", f_dt, [r["dtp"]], None, [r["dt_bias"]], [2 * SSD_HEADS], [F32])
    r["ys"], r["hst"], r["dts"], r["alog"] = [], [], [], []
    for d in (0, 1):
        dtc, dtr = _dir_dt(r["dt"], d)
        al = p["ssd_a_log"][j, d].reshape(SSD_GROUPS, SSD_HPG)
        al_r, al_c = al[:, None, :], al[:, :, None]
        y, hst = ssd_fwd(f"l{l}_ssd_fwd{d}", r["xbc_c"], dtc, dtr, al_r, al_c, bool(d), s_ctx)
        r["ys"].append(y)
        r["hst"].append(hst)
        r["dts"].append((dtc, dtr))
        r["alog"].append((al_r, al_c))
    v4 = lambda t: t.reshape(2, LRU_BLOCKS, 1, LRU_BLOCK_W)
    r["lru_p"] = (p["lru_w_a"][j], v4(p["lru_b_a"][j]), p["lru_w_i"][j], v4(p["lru_b_i"][j]), v4(p["lru_lam"][j]))
    r["h_sum"] = lru_fwd(f"l{l}_lru_fwd", r["u_c"], *r["lru_p"], s_ctx)
    r["xs"] = r["xbc_c"][:, :, :SSD_HEADS * SSD_HEAD_DIM]
    r["fin_p"] = [jnp.repeat(p["ssd_d"][j], SSD_HEAD_DIM)[None], p["ssd_norm_g"][j][None]]
    r["fin_in"] = [r["ys"][0], r["ys"][1], r["xs"], r["z"], r["h_sum"], r["gy"]]
    r["o"] = tok_fwd(f"l{l}_even_finish", f_even_finish, r["fin_in"], None, r["fin_p"], [1024, 1024], [BF16, BF16])
    return r


def _even_mixer_bwd(l, r, dox, p, lw, s_ctx, grads):
    j = l // 2
    shape3 = dox.shape
    t1 = taps_1d(4, s_ctx, shape3[1])
    grads["ev_w_out"][j] = jnp.concatenate([_wgrad(o, dox, f"l{l}_dwout{i}") for i, o in enumerate(r["o"])], axis=0)
    do = [_dgrad([(dox, lw["out"][i][1])], f"l{l}_dout{i}", shape3) for i in range(2)]
    (dy, _, dxs, dz, dh_sum, dgy), _, (dd_exp, grads["ssd_norm_g"][j]) = tok_bwd(
        f"l{l}_even_finish_bwd", f_even_finish, r["fin_in"], None, r["fin_p"], do, [F32, F32, F32, BF16, F32, BF16])
    grads["ssd_d"][j] = dd_exp.reshape(SSD_HEADS, SSD_HEAD_DIM).sum(-1)
    du_c, dwa, dba, dwi, dbi, dlam = lru_bwd(f"l{l}_lru_bwd", r["u_c"], *r["lru_p"], dh_sum, s_ctx)
    grads["lru_w_a"][j], grads["lru_w_i"][j] = dwa, dwi
    v2 = lambda t: t.reshape(2, LRU_BLOCKS * LRU_BLOCK_W)
    grads["lru_b_a"][j], grads["lru_b_i"][j], grads["lru_lam"][j] = v2(dba), v2(dbi), v2(dlam)
    dx_sum, dbm_sum, dcm_sum, ddts, dalog = dxs, 0.0, 0.0, [], []
    for d in (0, 1):
        dx, dbm, dcm, ddtc, ddtr, dar, dac = ssd_bwd(
            f"l{l}_ssd_bwd{d}", r["xbc_c"], *r["dts"][d], *r["alog"][d], r["hst"][d], dy, bool(d), s_ctx)
        dx_sum, dbm_sum, dcm_sum = dx_sum + dx, dbm_sum + dbm, dcm_sum + dcm
        ddts.append((jnp.transpose(ddtc, (0, 2, 1, 3)) + jnp.transpose(ddtr, (0, 3, 1, 2))).reshape(shape3[0], shape3[1], SSD_HEADS))
        dalog.append((dar.sum(0)[:, 0, :] + dac.sum(0)[:, :, 0]).reshape(SSD_HEADS))
    grads["ssd_a_log"][j] = jnp.stack(dalog)
    dxbc_c = jnp.concatenate([dx_sum, dbm_sum, dcm_sum], axis=-1)
    (ddtp,), _, (ddt_bias,) = tok_bwd(f"l{l}_dt_bwd", f_dt, [r["dtp"]], None, [r["dt_bias"]], [jnp.concatenate(ddts, axis=-1)], [F32])
    grads["ssd_dt_bias"][j] = ddt_bias.reshape(2, SSD_HEADS)
    dxbc, grads["ssd_conv_w"][j], dcb = conv_bwd(f"l{l}_ssd_conv_bwd", r["xbc"], p["ssd_conv_w"][j], p["ssd_conv_b"][j][None], dxbc_c, t1, "silu", dx_dtype=BF16)
    du, grads["lru_conv_w"][j], dlb = conv_bwd(f"l{l}_lru_conv_bwd", r["u"], p["lru_conv_w"][j], p["lru_conv_b"][j][None], du_c, t1, "none", dx_dtype=BF16)
    grads["ssd_conv_b"][j], grads["lru_conv_b"][j] = dcb[0], dlb[0]
    dproj = {"z": dz, "xbc": dxbc, "dt": _pad_to_lanes(_rows2d(ddtp)).reshape(shape3[:2] + (LANES,)), "gy": dgy, "u": du}
    grads["ev_w_in"][j] = jnp.concatenate(
        [_wgrad(r["hn"], dproj[n], f"l{l}_dwin_{n}")[:, :b - a] for n, (a, b) in EV_COLS.items()], axis=1)
    return _dgrad([(dproj[n], lw["in"][n][1]) for n in EV_COLS], f"l{l}_dhn", shape3)


def _odd_mixer_fwd(l, hn, p, lw, lb_row, s_ctx):
    j = l // 2
    r = {"hn": hn}
    proj = {n: _mm3(hn, lw["in"][n][0], f"l{l}_proj_{n}") for n in OD_COLS}
    r["v"], r["g"], r["u"] = proj["v"], proj["g"], proj["u"]
    r["prep_in"] = [proj["q"], proj["ff"], proj["fb"]]
    r["lb"] = lb_row
    r["prep"] = tok_fwd(f"l{l}_odd_prep", f_odd_prep, r["prep_in"], None, [lb_row], [HG_W] * 5, [F32] * 5)
    qs = r["prep"][0]
    r["os"], r["hst"] = [], []
    for d in (0, 1):
        o, hst = hg_fwd(f"l{l}_hg_fwd{d}", qs, r["prep"][1 + 2 * d], r["prep"][2 + 2 * d], r["v"], bool(d), s_ctx)
        r["os"].append(o)
        r["hst"].append(hst)
    r["s5_ins"], r["s5_p"] = _s5_prepare(p, j, f"l{l}_s5")
    r["s5y"] = s5_fwd(f"l{l}_s5_fwd", r["u"], *r["s5_p"], s_ctx)
    r["fin_p"] = [p["hg_norm_g"][j].reshape(1, HG_W), p["s5_d"][j][None], p["s5_glu_w"][j], p["s5_glu_b"][j][None]]
    r["fin_in"] = [r["os"][0], r["os"][1], r["g"], r["s5y"], r["u"]]
    r["o"] = tok_fwd(f"l{l}_odd_finish", f_odd_finish, r["fin_in"], None, r["fin_p"], [HG_W, S5_W], [BF16, BF16])
    return r


def _odd_mixer_bwd(l, r, dox, p, lw, s_ctx, grads):
    j = l // 2
    shape3 = dox.shape
    grads["od_w_out"][j] = jnp.concatenate([_wgrad(o, dox, f"l{l}_dwout{i}") for i, o in enumerate(r["o"])], axis=0)
    do = [_dgrad([(dox, lw["out"][i][1])], f"l{l}_dout{i}", shape3) for i in range(2)]
    (do_hg, _, dg, ds5y, du_fin), _, (dng, grads["s5_d"][j], grads["s5_glu_w"][j], dglu_b) = tok_bwd(
        f"l{l}_odd_finish_bwd", f_odd_finish, r["fin_in"], None, r["fin_p"], do, [F32, F32, BF16, F32, F32])
    grads["hg_norm_g"][j] = dng.reshape(HG_HEADS, HG_DK)
    grads["s5_d"][j], grads["s5_glu_b"][j] = grads["s5_d"][j][0], dglu_b[0]
    s5g = s5_bwd(f"l{l}_s5_bwd", r["u"], *r["s5_p"], ds5y, s_ctx)
    du = s5g[0] + du_fin
    (grads["s5_lam_re"][j], grads["s5_lam_im"][j], grads["s5_log_step"][j], grads["s5_b_re"][j], grads["s5_b_im"][j],
     grads["s5_c_re"][j], grads["s5_c_im"][j]) = _s5_param_grads(r["s5_ins"], s5g[1:], f"l{l}_s5")
    qs = r["prep"][0]
    dqs, dv, dprep = 0.0, 0.0, [None] * 5
    for d in (0, 1):
        dq, dk, dlf, dvd = hg_bwd(f"l{l}_hg_bwd{d}", qs, r["prep"][1 + 2 * d], r["prep"][2 + 2 * d], r["v"], r["hst"][d], do_hg, bool(d), s_ctx)
        dqs, dv = dqs + dq, dv + dvd
        dprep[1 + 2 * d], dprep[2 + 2 * d] = dk, dlf
    dprep[0] = dqs
    (dq_, dff, dfb), _, (dlb,) = tok_bwd(f"l{l}_odd_prep_bwd", f_odd_prep, r["prep_in"], None, [r["lb"]], dprep, [BF16] * 3)
    dproj = {"q": dq_, "ff": dff, "fb": dfb, "v": dv, "g": dg, "u": du}
    grads["od_w_in"][j] = jnp.concatenate([_wgrad(r["hn"], dproj[n], f"l{l}_dwin_{n}") for n in OD_COLS], axis=1)
    return _dgrad([(dproj[n], lw["in"][n][1]) for n in OD_COLS], f"l{l}_dhn", shape3), dlb


def _ffn_fwd(l, fn, p, lw, s_ctx):
    r = {"fn": fn}
    tg = taps_grid(s_ctx, fn.shape[1], GRID_W)
    r["a"] = _mm3(fn, lw["gate"][0], f"l{l}_ffn_gate")
    r["up"] = _mm3(fn, lw["up"][0], f"l{l}_ffn_up")
    r["cw"], r["cb"] = p["ffn_conv_w"][l].reshape(9, D_FF), p["ffn_conv_b"][l][None]
    r["act"] = conv_fwd(f"l{l}_ffn_conv", r["a"], r["cw"], r["cb"], tg, "silu_mul", mul=r["up"], out_dtype=BF16)
    return r, _mm3(r["act"], lw["down"][0], f"l{l}_ffn_down")


def _ffn_bwd(l, r, dfo, lw, s_ctx, grads):
    shape3 = dfo.shape
    tg = taps_grid(s_ctx, shape3[1], GRID_W)
    grads["ffn_w_down"][l] = _wgrad(r["act"], dfo, f"l{l}_dwdown")
    dact = _dgrad([(dfo, lw["down"][1])], f"l{l}_dact", shape3)
    da, dcw, dcb, dup = conv_bwd(f"l{l}_ffn_conv_bwd", r["a"], r["cw"], r["cb"], dact, tg, "silu_mul", mul=r["up"], dx_dtype=BF16)
    grads["ffn_conv_w"][l], grads["ffn_conv_b"][l] = dcw.reshape(3, 3, D_FF), dcb[0]
    grads["ffn_w_gate"][l] = _wgrad(r["fn"], da, f"l{l}_dwgate")
    grads["ffn_w_up"][l] = _wgrad(r["fn"], dup, f"l{l}_dwup")
    return _dgrad([(da, lw["gate"][1]), (dup, lw["up"][1])], f"l{l}_dfn", shape3)


BIG_WEIGHTS = ("ev_w_in", "ev_w_out", "od_w_in", "od_w_out", "ffn_w_gate", "ffn_w_up", "ffn_w_down")
PER_LAYER = {"norm_mix_g": DEPTH, "norm_ffn_g": DEPTH, "ffn_w_gate": DEPTH, "ffn_w_up": DEPTH, "ffn_conv_w": DEPTH,
             "ffn_conv_b": DEPTH, "ffn_w_down": DEPTH}


def local_step(x, ctx, target, modtabs, p, big, s_ctx=CTX_LEN):
    d_model = x.shape[-1]
    s0 = jnp.concatenate([ctx, x], axis=1)
    lws = [_layer_weights(l, big) for l in range(DEPTH)]
    shapes = {n: v.shape for n, v in {**p, **big}.items()}
    grads = {n: [None] * PER_LAYER.get(n, DEPTH // 2) for n in shapes if n not in ("c_ctx", "w_mod", "b_mod", "final_norm_g", "hg_lb_logits")}
    (lbs,) = small_fwd("lower_bounds", f_lower_bounds, [p["hg_lb_logits"]], [p["hg_lb_logits"].shape])
    tab_a = [modtabs[0]] + [modtabs[l].at[:, N_MOD - 1].set(modtabs[l - 1][:, N_MOD - 1]) for l in range(1, DEPTH)]
    res = []
    s, br = s0, None
    for l in range(DEPTH):
        r = {}
        g_mix, g_ffn = p["norm_mix_g"][l][None], p["norm_ffn_g"][l][None]
        if l == 0:
            (hn,) = tok_fwd("l0_norm", f_norm(0, 1), [s], tab_a[0], [g_mix], [d_model], [BF16])
            r["a_in"] = [s]
        else:
            r["a_in"] = [s, br]
            s, hn = tok_fwd(f"l{l}_resnorm_a", f_resnorm(5, 0, 1), r["a_in"], tab_a[l], [g_mix], [d_model] * 2, [F32, BF16])
        if l % 2 == 0:
            r["mix"] = _even_mixer_fwd(l, hn, p, lws[l], s_ctx)
        else:
            r["mix"] = _odd_mixer_fwd(l, hn, p, lws[l], lbs[l:l + 1], s_ctx)
        ox = mm([(_rows2d(o), w) for o, (w, _) in zip(r["mix"]["o"], lws[l]["out"])], f"l{l}_mix_out").reshape(s.shape)
        r["b_in"] = [s, ox]
        s, fn = tok_fwd(f"l{l}_resnorm_b", f_resnorm(2, 3, 4), r["b_in"], modtabs[l], [g_ffn], [d_model] * 2, [F32, BF16])
        r["ffn"], br = _ffn_fwd(l, fn, p, lws[l], s_ctx)
        res.append(r)

    loss_blk, ds, dbr, dtab_f, dfinal_g = final_loss("final_loss", s, br, modtabs[DEPTH - 1], p["final_norm_g"][None], target, s_ctx)
    grads["final_norm_g"] = dfinal_g[0]
    dmod = [None] * DEPTH
    dtab_next = dtab_f
    dlb = jnp.zeros_like(lbs)
    for l in reversed(range(DEPTH)):
        r = res[l]
        g_mix, g_ffn = p["norm_mix_g"][l][None], p["norm_ffn_g"][l][None]
        dfn = _ffn_bwd(l, r["ffn"], dbr, lws[l], s_ctx, grads)
        (ds, dox), dtab_b, (grads["norm_ffn_g"][l],) = tok_bwd(
            f"l{l}_resnorm_b_bwd", f_resnorm(2, 3, 4), r["b_in"], modtabs[l], [g_ffn], [ds, dfn], [F32, BF16])
        if l % 2 == 0:
            dhn = _even_mixer_bwd(l, r["mix"], dox, p, lws[l], s_ctx, grads)
        else:
            dhn, dlb_l = _odd_mixer_bwd(l, r["mix"], dox, p, lws[l], s_ctx, grads)
            dlb = dlb.at[l:l + 1].set(dlb_l)
        if l == 0:
            (ds,), dtab_a, (dg,) = tok_bwd("l0_norm_bwd", f_norm_keep(0, 1), r["a_in"], tab_a[0], [g_mix], [ds, dhn], [F32])
        else:
            (ds, dbr), dtab_a, (dg,) = tok_bwd(
                f"l{l}_resnorm_a_bwd", f_resnorm(5, 0, 1), r["a_in"], tab_a[l], [g_mix], [ds, dhn], [F32, BF16])
        grads["norm_mix_g"][l] = dg
        dmod[l] = (dtab_a.at[:, N_MOD - 1].set(0.0) + dtab_b).at[:, N_MOD - 1].set(dtab_next[:, N_MOD - 1])
        dtab_next = dtab_a
    (grads["hg_lb_logits"],) = small_bwd("lower_bounds_bwd", f_lower_bounds, [p["hg_lb_logits"]], [dlb])
    out = {}
    for n, g in grads.items():
        if isinstance(g, list):
            g = jnp.stack([t.reshape(shapes[n][1:]) for t in g])
        out[n] = g.reshape(shapes[n])
    return loss_blk[:, 0, 0], ds[:, s_ctx:], dmod, out


WEIGHT_NAMES = (
    "c_ctx", "w_mod", "b_mod", "norm_mix_g", "norm_ffn_g", "final_norm_g", "ev_w_in", "ev_w_out", "ssd_conv_w",
    "ssd_conv_b", "ssd_dt_bias", "ssd_a_log", "ssd_d", "ssd_norm_g", "lru_conv_w", "lru_conv_b", "lru_w_a", "lru_b_a",
    "lru_w_i", "lru_b_i", "lru_lam", "od_w_in", "od_w_out", "hg_lb_logits", "hg_norm_g", "s5_lam_re", "s5_lam_im",
    "s5_log_step", "s5_b_re", "s5_b_im", "s5_c_re", "s5_c_im", "s5_d", "s5_glu_w", "s5_glu_b", "ffn_w_gate", "ffn_w_up",
    "ffn_conv_w", "ffn_conv_b", "ffn_w_down")
INPUT_NAMES = ("x", "c", "ctx") + WEIGHT_NAMES + ("loss_target",) + tuple("m_" + n for n in WEIGHT_NAMES) + tuple("v_" + n for n in WEIGHT_NAMES)
SHARD_AXIS = {"w_mod": 2, "ev_w_in": 2, "ev_w_out": 1, "ssd_conv_w": 2, "lru_conv_w": 2, "lru_b_a": 2, "lru_b_i": 2,
              "lru_lam": 2, "od_w_in": 2, "od_w_out": 1, "s5_d": 1, "s5_glu_w": 1, "s5_glu_b": 1, "ffn_w_gate": 2,
              "ffn_w_up": 2, "ffn_conv_w": 3, "ffn_w_down": 1}
SMALL_SHARDED = tuple(n for n in WEIGHT_NAMES if n in SHARD_AXIS and n not in BIG_WEIGHTS and n != "w_mod")
REPLICATED_LOCAL = tuple(n for n in WEIGHT_NAMES if n not in SHARD_AXIS and n not in ("c_ctx", "b_mod"))
PACK_WIDTH = 1024
MOD_ROWS = 48
CTX_ROW = 32


def _unshard(g8, axis):
    moved = jnp.moveaxis(g8, 0, axis)
    shp = moved.shape
    return moved.reshape(shp[:axis] + (shp[axis] * shp[axis + 1],) + shp[axis + 2:])


def _to_shards(full, axis):
    shp = full.shape
    return jnp.moveaxis(full.reshape(shp[:axis] + (N_DEV, shp[axis] // N_DEV) + shp[axis + 1:]), axis, 0)


def _pack(arrs, dtype, lead=()):
    flat = jnp.concatenate([a.astype(dtype).reshape(lead + (-1,)) for a in arrs], axis=-1)
    n = flat.shape[-1]
    unit = 16 * PACK_WIDTH
    padded = -(-n // unit) * unit
    flat = jnp.pad(flat, [(0, 0)] * len(lead) + [(0, padded - n)])
    return flat.reshape(lead + (padded // PACK_WIDTH, PACK_WIDTH))


def _unpack(packed, shapes, lead=()):
    flat = packed.reshape(lead + (-1,))
    out, off = [], 0
    for shp in shapes:
        n = math.prod(shp)
        out.append(flat[..., off:off + n].reshape(lead + tuple(shp)))
        off += n
    return out


def _my_block(full, axis, me):
    loc = full.shape[axis] // N_DEV
    return lax.dynamic_slice_in_dim(full, me * loc, loc, axis)


def kernel(*args):
    a = dict(zip(INPUT_NAMES, args))
    px, py, pc = _my_pos()
    me = 4 * px + 2 * py + pc
    nb = a["x"].shape[0]

    big8 = all_gather([a[n].astype(BF16) for n in BIG_WEIGHTS], "gather_big_weights")
    big = {n: _unshard(g, SHARD_AXIS[n]) for n, g in zip(BIG_WEIGHTS, big8)}
    small_names = ("c",) + SMALL_SHARDED
    (small8,) = all_gather([_pack([a[n] for n in small_names], F32)], "gather_small")
    small = dict(zip(small_names, _unpack(small8, [a[n].shape for n in small_names], (N_DEV,))))
    p = {n: a[n] for n in WEIGHT_NAMES if n not in SHARD_AXIS}
    for n in SMALL_SHARDED:
        p[n] = _unshard(small[n], SHARD_AXIS[n])
    c_all = small["c"].reshape(N_DEV * nb, D_MODEL)

    rows = jnp.concatenate([c_all, a["c_ctx"][None], jnp.zeros((MOD_ROWS - CTX_ROW - 1, D_MODEL), F32)], axis=0)
    (srows,) = small_fwd("mod_silu", f_silu, [rows], [rows.shape])
    wmod2d = jnp.transpose(a["w_mod"], (1, 0, 2)).reshape(D_MODEL, -1).astype(BF16)
    cols = a["w_mod"].shape[2]
    mod_loc = mm([(srows, wmod2d)], "mod_proj")
    mod8 = all_gather([mod_loc], "gather_mod")[0].reshape(N_DEV, MOD_ROWS, DEPTH, cols)
    mod_all = jnp.transpose(mod8, (2, 1, 0, 3)).reshape(DEPTH, MOD_ROWS, N_DEV * cols) + a["b_mod"][:, None, :]
    modtabs = []
    for l in range(DEPTH):
        mine = lax.dynamic_slice_in_dim(mod_all[l], me * nb, nb, 0).reshape(nb, N_MOD, D_MODEL)
        ctx_row = jnp.broadcast_to(mod_all[l, CTX_ROW].reshape(1, N_MOD, D_MODEL), (nb, N_MOD, D_MODEL))
        modtabs.append(jnp.stack([ctx_row, mine], axis=1).reshape(2 * nb, N_MOD, D_MODEL))

    loss_b, grad_x, dmod, grads = local_step(a["x"], a["ctx"], a["loss_target"], modtabs, p, big)
    loss = lax.psum(jnp.sum(loss_b), ("x", "y", "c"))

    dm = jnp.stack([t.reshape(nb, 2, N_MOD * D_MODEL) for t in dmod])
    dloc = jnp.concatenate([dm[:, :, 1], jnp.sum(dm[:, :, 0], axis=1, keepdims=True),
                            jnp.zeros((DEPTH, SUBLANES - nb - 1, N_MOD * D_MODEL), F32)], axis=1)
    d8 = all_gather([dloc.reshape(DEPTH * SUBLANES, -1)], "gather_dmod")[0].reshape(N_DEV, DEPTH, SUBLANES, -1)
    d_rows = jnp.transpose(d8[:, :, :nb], (1, 0, 2, 3)).reshape(DEPTH, N_DEV * nb, -1)
    d_ctx = jnp.sum(d8[:, :, nb], axis=0)[:, None]
    d_full = jnp.concatenate([d_rows, d_ctx, jnp.zeros((DEPTH, MOD_ROWS - CTX_ROW - 1, N_MOD * D_MODEL), F32)], axis=1)
    grads["b_mod"] = jnp.sum(d_full, axis=1)
    d_cols = jnp.transpose(_my_block(d_full, 2, me), (1, 0, 2)).reshape(MOD_ROWS, DEPTH * cols)
    g_wmod = mm([(srows.T, d_cols)], "mod_dw")
    g_wmod_local = jnp.transpose(g_wmod.reshape(D_MODEL, DEPTH, cols), (1, 0, 2))
    d_srows_part = mm([(d_cols[CTX_ROW:CTX_ROW + SUBLANES], wmod2d.T)], "mod_dctx")[0]

    reduce_names = REPLICATED_LOCAL + SMALL_SHARDED
    (part8,) = all_gather([_pack([d_srows_part] + [grads[n] for n in reduce_names], F32)], "gather_small_grads")
    total = sum_slots(part8, "sum_small_grads")
    totals = _unpack(total, [(D_MODEL,)] + [grads[n].shape for n in reduce_names])
    d_srows = jnp.zeros_like(rows).at[CTX_ROW].set(totals[0])
    (d_rows_in,) = small_bwd("mod_silu_bwd", f_silu, [rows], [d_srows])
    g_local = {"c_ctx": d_rows_in[CTX_ROW], "b_mod": grads["b_mod"], "w_mod": g_wmod_local}
    for n, t in zip(reduce_names, totals[1:]):
        g_local[n] = _my_block(t, SHARD_AXIS[n], me) if n in SHARD_AXIS else t

    got = all_to_all([_to_shards(grads[n], SHARD_AXIS[n]).astype(BF16) for n in BIG_WEIGHTS], "exchange_big_grads")
    for n, t in zip(BIG_WEIGHTS, got):
        g_local[n] = sum_slots(t.reshape(N_DEV, -1, t.shape[-1]), "sum_" + n).reshape(a[n].shape)

    deltas, new_m, new_v = [], [], []
    for n in WEIGHT_NAMES:
        d, m, v = adamw("adamw_" + n, a[n], g_local[n], a["m_" + n], a["v_" + n])
        deltas.append(d)
        new_m.append(m)
        new_v.append(v)
    return (loss, grad_x, *[g_local[n] for n in WEIGHT_NAMES], *deltas, *new_m, *new_v)
```

```python
import functools
import math

import jax
import jax.numpy as jnp
from jax import lax
from jax.experimental import pallas as pl
from jax.experimental.pallas import tpu as pltpu

F32 = jnp.float32
BF16 = jnp.bfloat16

D_MODEL = 1024
DEPTH = 4
CTX_LEN = 256
SEQ = 2048
S_TOT = CTX_LEN + SEQ
GRID_W = 64
N_MOD = 6
RMS_EPS = 1e-6
N_DEV = 8

SSD_HEADS = 16
SSD_HEAD_DIM = 64
SSD_GROUPS = 2
SSD_HPG = 8
SSD_STATE = 128
SSD_CHUNK = 128
SSD_W = SSD_HEADS * SSD_HEAD_DIM
LRU_BLOCKS = 8
LRU_BLOCK_W = 128
LRU_C = 8.0
HG_W = 768
HG_HEADS = 6
HG_DK = 128
HG_CHUNK = 32
S5_W = 256
S5_GROUPS = 16
S5_GROUP_CH = 16
S5_STATE = 64
D_FF = 2816

ADAM_LR = 0.001
ADAM_B1 = 0.9
ADAM_B2 = 0.999
ADAM_EPS = 1e-08
ADAM_WD = 0.01
ADAM_STEP = 10

TOK_BLOCK = CTX_LEN
SUBLANES = 8
VMEM_LIMIT_BYTES = 56 * 1024 * 1024
MM_BLOCK_BYTES = 8 * 1024 * 1024
MM_TILES = (1408, 1024, 768, 704, 512, 384, 352, 256, 128, 64, 48, 40, 32, 16, 8)
MM_ROW_TILES = (2304, 2048, 1152, 1024, 512, 256, 128, 64, 48, 32, 16, 8)
LANE_TILE = 128


def _cparams(sem=None):
    kw = dict(vmem_limit_bytes=VMEM_LIMIT_BYTES)
    if sem is not None:
        kw["dimension_semantics"] = sem
    return pltpu.CompilerParams(**kw)


def _pick(n, cands):
    for c in cands:
        if n % c == 0:
            return c
    return n


def mm(pairs, name, out_dtype=F32):
    m = pairs[0][0].shape[0]
    n = pairs[0][1].shape[1]
    kdims = [a.shape[1] for a, _ in pairs]
    ktile = None
    if len(pairs) == 1 and kdims[0] > 4096:
        ktile = _pick(kdims[0], (2304, 2048, 1024))
    nk = kdims[0] // ktile if ktile else 1
    col_bytes = sum((ktile or w.shape[0]) * w.dtype.itemsize for _, w in pairs)
    tn = _pick(n, tuple(c for c in MM_TILES if c % LANE_TILE == 0 and c * col_bytes <= MM_BLOCK_BYTES))
    row_bytes = max(sum((ktile or a.shape[1]) * a.dtype.itemsize for a, _ in pairs), tn * 4)
    tm = _pick(m, tuple(c for c in MM_TILES if c * row_bytes <= MM_BLOCK_BYTES))
    npairs = len(pairs)
    if nk > 1:
        assert out_dtype == F32

    def body(*refs):
        o_ref = refs[2 * npairs]
        acc = None
        for i in range(npairs):
            a = refs[2 * i][...].astype(BF16)
            w = refs[2 * i + 1][...].astype(BF16)
            p = jnp.dot(a, w, preferred_element_type=F32)
            acc = p if acc is None else acc + p
        if nk == 1:
            o_ref[...] = acc.astype(out_dtype)
        else:
            k = pl.program_id(2)

            @pl.when(k == 0)
            def _():
                o_ref[...] = acc

            @pl.when(k > 0)
            def _():
                o_ref[...] += acc

    in_specs = []
    args = []
    for a, w in pairs:
        kk = a.shape[1]
        assert w.shape == (kk, n) and a.shape[0] == m, (a.shape, w.shape)
        tk = ktile if ktile else kk
        in_specs.append(pl.BlockSpec((tm, tk), lambda i, j, k: (i, k)))
        in_specs.append(pl.BlockSpec((tk, tn), lambda i, j, k: (k, j)))
        args += [a, w]
    return pl.pallas_call(
        body,
        name=name,
        grid=(m // tm, n // tn, nk),
        in_specs=in_specs,
        out_specs=pl.BlockSpec((tm, tn), lambda i, j, k: (i, j)),
        out_shape=jax.ShapeDtypeStruct((m, n), out_dtype),
        compiler_params=_cparams(("parallel", "parallel", "arbitrary")),
    )(*args)


def mm_tn(a, d, name):
    r, k = a.shape
    n = d.shape[1]
    tr = _pick(r, MM_ROW_TILES)
    lane_ok = lambda c, full: c % LANE_TILE == 0 or c == full
    tk = _pick(k, tuple(c for c in MM_TILES if lane_ok(c, k) and c * tr * a.dtype.itemsize <= MM_BLOCK_BYTES))
    tn = _pick(n, tuple(c for c in MM_TILES if lane_ok(c, n) and c * tr * d.dtype.itemsize <= MM_BLOCK_BYTES
                        and c * tk * 4 <= MM_BLOCK_BYTES))

    def body(a_ref, d_ref, o_ref):
        acc = lax.dot_general(a_ref[...].astype(BF16), d_ref[...].astype(BF16), (((0,), (0,)), ((), ())),
                              preferred_element_type=F32)
        step = pl.program_id(2)

        @pl.when(step == 0)
        def _():
            o_ref[...] = acc

        @pl.when(step > 0)
        def _():
            o_ref[...] += acc

    return pl.pallas_call(
        body,
        name=name,
        grid=(k // tk, n // tn, r // tr),
        in_specs=[pl.BlockSpec((tr, tk), lambda i, j, s: (s, i)), pl.BlockSpec((tr, tn), lambda i, j, s: (s, j))],
        out_specs=pl.BlockSpec((tk, tn), lambda i, j, s: (i, j)),
        out_shape=jax.ShapeDtypeStruct((k, n), F32),
        compiler_params=_cparams(("parallel", "parallel", "arbitrary")),
    )(a, d)


def _mod_index(b, t):
    return (2 * b + jnp.minimum(t, 1), 0, 0)


def tok_fwd(name, f, toks, mod, params, out_widths, out_dtypes):
    nb, s, _ = toks[0].shape
    nt, nm, npar = len(toks), int(mod is not None), len(params)

    def body(*refs):
        ins, outs = refs[: nt + nm + npar], refs[nt + nm + npar:]
        tv = [r[...].astype(F32) for r in ins[:nt]]
        mv = [ins[nt][k:k + 1, :] for k in range(N_MOD)] if nm else None
        pv = [r[...] for r in ins[nt + nm:]]
        for o, r in zip(outs, f(tv, mv, pv)):
            o[...] = r.astype(o.dtype)

    in_specs = [pl.BlockSpec((None, TOK_BLOCK, t.shape[2]), lambda b, t: (b, t, 0)) for t in toks]
    if nm:
        in_specs.append(pl.BlockSpec((None, N_MOD, mod.shape[2]), _mod_index))
    in_specs += [pl.BlockSpec(p.shape, lambda b, t, nd=p.ndim: (0,) * nd) for p in params]
    return pl.pallas_call(
        body,
        name=name,
        grid=(nb, s // TOK_BLOCK),
        in_specs=in_specs,
        out_specs=[pl.BlockSpec((None, TOK_BLOCK, w), lambda b, t: (b, t, 0)) for w in out_widths],
        out_shape=[jax.ShapeDtypeStruct((nb, s, w), dt) for w, dt in zip(out_widths, out_dtypes)],
        compiler_params=_cparams(("parallel", "parallel")),
    )(*toks, *([mod] if nm else []), *params)


def tok_bwd(name, f, toks, mod, params, cots, dtok_dtypes):
    nb, s, _ = toks[0].shape
    nt, nm, npar, nc = len(toks), int(mod is not None), len(params), len(cots)

    def body(*refs):
        n_in = nt + nm + npar + nc
        ins, outs = refs[:n_in], refs[n_in:]
        b, t = pl.program_id(0), pl.program_id(1)
        tv = [r[...].astype(F32) for r in ins[:nt]]
        mv = [ins[nt][k:k + 1, :] for k in range(N_MOD)] if nm else None
        pv = [r[...] for r in ins[nt + nm: nt + nm + npar]]
        cv = [r[...].astype(F32) for r in ins[nt + nm + npar:]]
        _, vjp = jax.vjp(f, tv, mv, pv)
        dtv, dmv, dpv = vjp(cv)
        for o, r in zip(outs[:nt], dtv):
            o[...] = r.astype(o.dtype)
        if nm:
            dm_ref = outs[nt]

            @pl.when(t <= 1)
            def _():
                for k in range(N_MOD):
                    dm_ref[k:k + 1, :] = dmv[k]

            @pl.when(t > 1)
            def _():
                for k in range(N_MOD):
                    dm_ref[k:k + 1, :] += dmv[k]

        first = jnp.logical_and(b == 0, t == 0)
        for o, r in zip(outs[nt + nm:], dpv):
            @pl.when(first)
            def _(o=o, r=r):
                o[...] = r

            @pl.when(jnp.logical_not(first))
            def _(o=o, r=r):
                o[...] += r

    tok_spec = lambda w: pl.BlockSpec((None, TOK_BLOCK, w), lambda b, t: (b, t, 0))
    in_specs = [tok_spec(t.shape[2]) for t in toks]
    if nm:
        in_specs.append(pl.BlockSpec((None, N_MOD, mod.shape[2]), _mod_index))
    in_specs += [pl.BlockSpec(p.shape, lambda b, t, nd=p.ndim: (0,) * nd) for p in params]
    in_specs += [tok_spec(c.shape[2]) for c in cots]
    out_specs = [tok_spec(t.shape[2]) for t in toks]
    out_shape = [jax.ShapeDtypeStruct(t.shape, dt) for t, dt in zip(toks, dtok_dtypes)]
    if nm:
        out_specs.append(pl.BlockSpec((None, N_MOD, mod.shape[2]), _mod_index))
        out_shape.append(jax.ShapeDtypeStruct(mod.shape, F32))
    out_specs += [pl.BlockSpec(p.shape, lambda b, t, nd=p.ndim: (0,) * nd) for p in params]
    out_shape += [jax.ShapeDtypeStruct(p.shape, F32) for p in params]
    res = pl.pallas_call(
        body,
        name=name,
        grid=(nb, s // TOK_BLOCK),
        in_specs=in_specs,
        out_specs=out_specs,
        out_shape=out_shape,
        compiler_params=_cparams(("arbitrary", "arbitrary")),
    )(*toks, *([mod] if nm else []), *params, *cots)
    return res[:nt], (res[nt] if nm else None), res[nt + nm:]


def _rms(x, g):
    return x * lax.rsqrt(jnp.mean(x * x, axis=-1, keepdims=True) + RMS_EPS) * g


def _silu(x):
    return x * jax.nn.sigmoid(x)


def f_norm(shift_row, scale_row):
    def f(tv, mv, pv):
        return [_rms(tv[0], pv[0]) * (1.0 + mv[scale_row]) + mv[shift_row]]
    return f


def f_resnorm(gate_row, shift_row, scale_row):
    def f(tv, mv, pv):
        s = tv[0] + mv[gate_row] * tv[1]
        return [s, _rms(s, pv[0]) * (1.0 + mv[scale_row]) + mv[shift_row]]
    return f


_ANY = pl.BlockSpec(memory_space=pl.ANY)
_MESH = pl.DeviceIdType.MESH


def _my_pos():
    return lax.axis_index("x"), lax.axis_index("y"), lax.axis_index("c")


def _slot_of(pos):
    return 4 * pos[0] + 2 * pos[1] + pos[2]


def all_gather(xs, name):
    n = len(xs)

    def body(*refs):
        x_refs, out_refs = refs[:n], refs[n:2 * n]
        send_sems, recv_sems, local_sems = refs[2 * n:]
        px, py, pc = _my_pos()
        me, sibling = (px, py, pc), (px, py, 1 - pc)
        chips = [(1 - px, py), (px, 1 - py), (1 - px, 1 - py)]

        def copy(a, k, block, to, from_input=False):
            slot = out_refs[a].at[_slot_of(block)]
            return pltpu.make_async_remote_copy(
                src_ref=x_refs[a] if from_input else slot, dst_ref=slot,
                send_sem=send_sems.at[a, k], recv_sem=recv_sems.at[a, k],
                device_id=to, device_id_type=_MESH)

        mine = [pltpu.make_async_copy(x_refs[a], out_refs[a].at[_slot_of(me)], local_sems.at[a]) for a in range(n)]
        for cp in mine:
            cp.start()
        first = [copy(a, 0, me, sibling, True) for a in range(n)]
        first += [copy(a, 1 + j, me, (*chip, pc), True) for j, chip in enumerate(chips) for a in range(n)]
        for cp in first:
            cp.start()
        passed = []
        for j, chip in enumerate(chips):
            for a in range(n):
                copy(a, 1 + j, (*chip, pc), me).wait_recv()
                passed.append(copy(a, 4 + j, (*chip, pc), sibling))
                passed[-1].start()
        for a in range(n):
            copy(a, 0, sibling, me).wait_recv()
        for j, chip in enumerate(chips):
            for a in range(n):
                copy(a, 4 + j, (*chip, 1 - pc), me).wait_recv()
        for cp in first + passed:
            cp.wait_send()
        for cp in mine:
            cp.wait()

    return pl.pallas_call(
        body,
        name=name,
        out_shape=[jax.ShapeDtypeStruct((N_DEV,) + x.shape, x.dtype) for x in xs],
        in_specs=[_ANY] * n,
        out_specs=[_ANY] * n,
        scratch_shapes=[pltpu.SemaphoreType.DMA((n, 7)), pltpu.SemaphoreType.DMA((n, 7)), pltpu.SemaphoreType.DMA((n,))],
    )(*xs)


def all_to_all(xs, name):
    n = len(xs)

    def body(*refs):
        x_refs, out_refs = refs[:n], refs[n:2 * n]
        send_sems, recv_sems, local_sems = refs[2 * n:]
        px, py, pc = _my_pos()
        me = (px, py, pc)

        def flipped(k):
            kx, ky, kc = (k >> 2) & 1, (k >> 1) & 1, k & 1
            return (1 - px if kx else px, 1 - py if ky else py, 1 - pc if kc else pc)

        def copy(a, k):
            peer = flipped(k)
            return pltpu.make_async_remote_copy(
                src_ref=x_refs[a].at[_slot_of(peer)], dst_ref=out_refs[a].at[_slot_of(me)],
                send_sem=send_sems.at[a, k - 1], recv_sem=recv_sems.at[a, k - 1],
                device_id=peer, device_id_type=_MESH)

        def landing(a, k):
            peer = flipped(k)
            return pltpu.make_async_remote_copy(
                src_ref=x_refs[a].at[_slot_of(me)], dst_ref=out_refs[a].at[_slot_of(peer)],
                send_sem=send_sems.at[a, k - 1], recv_sem=recv_sems.at[a, k - 1],
                device_id=peer, device_id_type=_MESH)

        mine = [pltpu.make_async_copy(x_refs[a].at[_slot_of(me)], out_refs[a].at[_slot_of(me)], local_sems.at[a]) for a in range(n)]
        for cp in mine:
            cp.start()
        copies = [copy(a, k) for a in range(n) for k in range(1, N_DEV)]
        for cp in copies:
            cp.start()
        for a in range(n):
            for k in range(1, N_DEV):
                landing(a, k).wait_recv()
        for cp in copies:
            cp.wait_send()
        for cp in mine:
            cp.wait()

    return pl.pallas_call(
        body,
        name=name,
        out_shape=[jax.ShapeDtypeStruct(x.shape, x.dtype) for x in xs],
        in_specs=[_ANY] * n,
        out_specs=[_ANY] * n,
        scratch_shapes=[pltpu.SemaphoreType.DMA((n, 7)), pltpu.SemaphoreType.DMA((n, 7)), pltpu.SemaphoreType.DMA((n,))],
    )(*xs)


def sum_slots(x, name):
    n, r, c = x.shape
    tr = _pick(r, (512, 256, 128, 64, 32, 16, 8))

    def body(x_ref, o_ref):
        acc = x_ref[0].astype(F32)
        for i in range(1, n):
            acc = acc + x_ref[i].astype(F32)
        o_ref[...] = acc

    return pl.pallas_call(
        body,
        name=name,
        grid=(r // tr,),
        in_specs=[pl.BlockSpec((n, tr, c), lambda i: (0, i, 0))],
        out_specs=pl.BlockSpec((tr, c), lambda i: (i, 0)),
        out_shape=jax.ShapeDtypeStruct((r, c), F32),
        compiler_params=_cparams(("parallel",)),
    )(x)


CONV_CH_TILE = 256


def _shift_rows(x, off):
    n = x.shape[0]
    if off % n == 0:
        return x
    return pltpu.roll(x, (-off) % n, axis=0)


def _between(v, lo, hi):
    return jnp.where(v >= lo, 1.0, 0.0) * jnp.where(v < hi, 1.0, 0.0)


def taps_1d(ntaps, s_ctx, s_tot):
    def mask(off):
        def m(t):
            is_ctx = _between(t, 0, s_ctx)
            return is_ctx * _between(t + off, 0, s_ctx) + (1.0 - is_ctx) * _between(t + off, s_ctx, s_tot)
        return m
    return [(j - (ntaps - 1) // 2, mask(j - (ntaps - 1) // 2)) for j in range(ntaps)]


def taps_grid(s_ctx, s_tot, grid_w):
    assert s_ctx % grid_w == 0

    return ("grid", s_ctx, s_tot, grid_w)


def _grid_masks(taps, s):
    _, s_ctx, s_tot, grid_w = taps
    t = lax.broadcasted_iota(jnp.int32, (s, 1), 0)
    is_ctx = _between(t, 0, s_ctx)
    mcol = {dc: is_ctx * _between(t + dc, 0, s_ctx) + (1.0 - is_ctx) * _between(t % grid_w + dc, 0, grid_w) for dc in (-1, 1)}
    mrow = {dr: (1.0 - is_ctx) * _between(t + grid_w * dr, s_ctx, s_tot) for dr in (-1, 1)}
    return mcol, mrow


def _grid_cols(x, mcol):
    return {-1: _shift_rows(x, -1) * mcol[-1], 0: x, 1: _shift_rows(x, 1) * mcol[1]}


def _conv_acc(x, w_ref, b_ref, taps, s):
    acc = jnp.broadcast_to(b_ref[...], x.shape)
    if taps[0] == "grid":
        grid_w = taps[3]
        mcol, mrow = _grid_masks(taps, s)
        xc = _grid_cols(x, mcol)
        for a, dr in enumerate((-1, 0, 1)):
            r = sum(w_ref[3 * a + b:3 * a + b + 1, :] * xc[dc] for b, dc in enumerate((-1, 0, 1)))
            acc = acc + (r if dr == 0 else _shift_rows(r, grid_w * dr) * mrow[dr])
        return acc
    t = lax.broadcasted_iota(jnp.int32, (s, 1), 0)
    for k, (off, m) in enumerate(taps):
        acc = acc + w_ref[k:k + 1, :] * (_shift_rows(x, off) * m(t))
    return acc


def _conv_adjoint(x, dacc, w_ref, taps, s):
    if taps[0] == "grid":
        grid_w = taps[3]
        mcol, mrow = _grid_masks(taps, s)
        xc = _grid_cols(x, mcol)
        dxc = {dc: 0.0 for dc in (-1, 0, 1)}
        dws = []
        for a, dr in enumerate((-1, 0, 1)):
            d_r = dacc if dr == 0 else _shift_rows(dacc * mrow[dr], -grid_w * dr)
            for b, dc in enumerate((-1, 0, 1)):
                dxc[dc] = dxc[dc] + w_ref[3 * a + b:3 * a + b + 1, :] * d_r
                dws.append(jnp.sum(d_r * xc[dc], axis=0, keepdims=True))
        dx = dxc[0] + _shift_rows(dxc[-1] * mcol[-1], 1) + _shift_rows(dxc[1] * mcol[1], -1)
        return dx, dws
    t = lax.broadcasted_iota(jnp.int32, (s, 1), 0)
    dx = jnp.zeros_like(x)
    dws = []
    for k, (off, m) in enumerate(taps):
        dm = dacc * m(t)
        dx = dx + _shift_rows(w_ref[k:k + 1, :] * dm, -off)
        dws.append(jnp.sum(dm * _shift_rows(x, off), axis=0, keepdims=True))
    return dx, dws


def conv_fwd(name, x, w, b, taps, mode, mul=None, out_dtype=F32):
    nb, s, c = x.shape
    ct = _pick(c, (CONV_CH_TILE, 128))
    has_mul = mode == "silu_mul"

    def body(*refs):
        x_ref, w_ref, b_ref = refs[:3]
        o_ref = refs[-1]
        acc = _conv_acc(x_ref[...], w_ref, b_ref, taps, s)
        if mode == "none":
            out = acc
        else:
            out = _silu(acc)
            if has_mul:
                out = out * refs[3][...].astype(F32)
        o_ref[...] = out.astype(o_ref.dtype)

    blk = pl.BlockSpec((None, s, ct), lambda bb, j: (bb, 0, j))
    par = lambda k: pl.BlockSpec((k, ct), lambda bb, j: (0, j))
    return pl.pallas_call(
        body,
        name=name,
        grid=(nb, c // ct),
        in_specs=[blk, par(w.shape[0]), par(1)] + ([blk] if has_mul else []),
        out_specs=blk,
        out_shape=jax.ShapeDtypeStruct(x.shape, out_dtype),
        compiler_params=_cparams(("parallel", "parallel")),
    )(x, w, b, *([mul] if has_mul else []))


def conv_bwd(name, x, w, b, dout, taps, mode, mul=None, dx_dtype=F32):
    nb, s, c = x.shape
    ct = _pick(c, (CONV_CH_TILE, 128))
    has_mul = mode == "silu_mul"
    nk = w.shape[0]

    def body(*refs):
        x_ref, w_ref, b_ref, do_ref = refs[:4]
        n_in = 5 if has_mul else 4
        dx_ref, dw_ref, db_ref = refs[n_in:n_in + 3]
        bb = pl.program_id(1)
        x = x_ref[...]
        dacc = do_ref[...].astype(F32)
        if mode != "none":
            acc = _conv_acc(x, w_ref, b_ref, taps, s)
            sg = jax.nn.sigmoid(acc)
            if has_mul:
                refs[n_in + 3][...] = (dacc * (acc * sg)).astype(refs[n_in + 3].dtype)
                dacc = dacc * refs[4][...].astype(F32)
            dacc = dacc * (sg * (1.0 + acc * (1.0 - sg)))
        dx, dws = _conv_adjoint(x, dacc, w_ref, taps, s)
        dx_ref[...] = dx.astype(dx_ref.dtype)
        db = jnp.sum(dacc, axis=0, keepdims=True)

        @pl.when(bb == 0)
        def _():
            for k in range(nk):
                dw_ref[k:k + 1, :] = dws[k]
            db_ref[...] = db

        @pl.when(bb > 0)
        def _():
            for k in range(nk):
                dw_ref[k:k + 1, :] += dws[k]
            db_ref[...] += db

    blk = pl.BlockSpec((None, s, ct), lambda j, bb: (bb, 0, j))
    par = lambda k: pl.BlockSpec((k, ct), lambda j, bb: (0, j))
    out_specs = [blk, par(nk), par(1)] + ([blk] if has_mul else [])
    out_shape = [jax.ShapeDtypeStruct(x.shape, dx_dtype), jax.ShapeDtypeStruct(w.shape, F32), jax.ShapeDtypeStruct(b.shape, F32)]
    if has_mul:
        out_shape.append(jax.ShapeDtypeStruct(x.shape, dx_dtype))
    return pl.pallas_call(
        body,
        name=name,
        grid=(c // ct, nb),
        in_specs=[blk, par(nk), par(1), blk] + ([blk] if has_mul else []),
        out_specs=out_specs,
        out_shape=out_shape,
        compiler_params=_cparams(("parallel", "arbitrary")),
    )(x, w, b, dout, *([mul] if has_mul else []))


SCAN_UNROLL = 4


def _scan_order(direction, adjoint, s_ctx, s_tot):
    nc, nt = s_ctx // SUBLANES, s_tot // SUBLANES
    if direction == 0:
        return ([(0, nt, 1)], False) if not adjoint else ([(nt - 1, nt, -1)], True)
    if not adjoint:
        return [(nc - 1, nc, -1), (nt - 1, nt - nc, -1)], True
    return [(nc, nt - nc, 1), (0, nc, 1)], False


def _last_row(h, descending):
    row = lax.broadcasted_iota(jnp.int32, h.shape, 0)
    pick = 0 if descending else SUBLANES - 1
    return jnp.sum(jnp.where(row == pick, h, 0.0), axis=0, keepdims=True)


def _prev_rows(h, carry, descending):
    row = lax.broadcasted_iota(jnp.int32, h.shape, 0)
    if descending:
        return jnp.where(row == SUBLANES - 1, carry, pltpu.roll(h, SUBLANES - 1, axis=0))
    return jnp.where(row == 0, carry, pltpu.roll(h, 1, axis=0))


def _scan_real(a_ref, x_ref, h_ref, hp_ref, order):
    ranges, descending = order
    n_rows, width = a_ref.shape
    n_tiles = n_rows // SUBLANES
    row = lax.broadcasted_iota(jnp.int32, (SUBLANES, width), 0)
    unroll = lambda count: SCAN_UNROLL if count % SCAN_UNROLL == 0 else 1

    def run(ac_ref):
        def in_tile(i, _):
            t0 = pl.multiple_of(i * SUBLANES, SUBLANES)
            a = a_ref[pl.ds(t0, SUBLANES), :]
            x = x_ref[pl.ds(t0, SUBLANES), :]
            for k in (1, 2, 4):
                sh = SUBLANES - k if descending else k
                keep = (row < SUBLANES - k) if descending else (row >= k)
                x = jnp.where(keep, a * pltpu.roll(x, sh, axis=0) + x, x)
                a = jnp.where(keep, a * pltpu.roll(a, sh, axis=0), a)
            ac_ref[pl.ds(t0, SUBLANES), :] = a
            x_ref[pl.ds(t0, SUBLANES), :] = x
            return 0

        lax.fori_loop(0, n_tiles, in_tile, 0, unroll=unroll(n_tiles))

        def tile(i, carry):
            t0 = pl.multiple_of(i * SUBLANES, SUBLANES)
            a = ac_ref[pl.ds(t0, SUBLANES), :]
            x = x_ref[pl.ds(t0, SUBLANES), :]
            h = a * carry + x
            if h_ref is not None:
                h_ref[pl.ds(t0, SUBLANES), :] = h
            if hp_ref is not None:
                hp_ref[pl.ds(t0, SUBLANES), :] = _prev_rows(h, carry, descending)
            return _last_row(a, descending) * carry + _last_row(x, descending)

        carry = jnp.zeros((1, width), F32)
        for first, count, step in ranges:
            carry = lax.fori_loop(0, count, lambda j, c, first=first, step=step: tile(first + step * j, c), carry,
                                  unroll=unroll(count))

    pl.run_scoped(run, pltpu.VMEM((n_rows, width), F32))


def _cmul(ar, ai, br, bi):
    return ar * br - ai * bi, ar * bi + ai * br


def _scan_cplx(lr, li, xr_ref, xi_ref, hpr_ref, hpi_ref, order):
    ranges, descending = order
    width = xr_ref.shape[1]
    row = lax.broadcasted_iota(jnp.int32, (SUBLANES, width), 0)
    pw = [(lr, li)]
    for _ in range(SUBLANES - 1):
        pw.append(_cmul(pw[-1][0], pw[-1][1], lr, li))
    pr = jnp.zeros((SUBLANES, width), F32)
    pi = jnp.zeros((SUBLANES, width), F32)
    for r in range(SUBLANES):
        n = SUBLANES - 1 - r if descending else r
        pr = jnp.where(row == r, pw[n][0], pr)
        pi = jnp.where(row == r, pw[n][1], pi)

    n_tiles = xr_ref.shape[0] // SUBLANES
    unroll = lambda count: SCAN_UNROLL if count % SCAN_UNROLL == 0 else 1

    def in_tile(i, _):
        t0 = pl.multiple_of(i * SUBLANES, SUBLANES)
        xr = xr_ref[pl.ds(t0, SUBLANES), :]
        xi = xi_ref[pl.ds(t0, SUBLANES), :]
        for k in (1, 2, 4):
            sh = SUBLANES - k if descending else k
            keep = (row < SUBLANES - k) if descending else (row >= k)
            sr, si = _cmul(pw[k - 1][0], pw[k - 1][1], pltpu.roll(xr, sh, axis=0), pltpu.roll(xi, sh, axis=0))
            xr = jnp.where(keep, xr + sr, xr)
            xi = jnp.where(keep, xi + si, xi)
        xr_ref[pl.ds(t0, SUBLANES), :] = xr
        xi_ref[pl.ds(t0, SUBLANES), :] = xi
        return 0

    lax.fori_loop(0, n_tiles, in_tile, 0, unroll=unroll(n_tiles))
    lam8 = pw[SUBLANES - 1]

    def tile(i, carry):
        cr, ci = carry
        t0 = pl.multiple_of(i * SUBLANES, SUBLANES)
        xr = xr_ref[pl.ds(t0, SUBLANES), :]
        xi = xi_ref[pl.ds(t0, SUBLANES), :]
        hr, hi = _cmul(pr, pi, cr, ci)
        hr, hi = hr + xr, hi + xi
        xr_ref[pl.ds(t0, SUBLANES), :] = hr
        xi_ref[pl.ds(t0, SUBLANES), :] = hi
        if hpr_ref is not None:
            hpr_ref[pl.ds(t0, SUBLANES), :] = _prev_rows(hr, cr, descending)
            hpi_ref[pl.ds(t0, SUBLANES), :] = _prev_rows(hi, ci, descending)
        nr, ni = _cmul(lam8[0], lam8[1], cr, ci)
        return nr + _last_row(xr, descending), ni + _last_row(xi, descending)

    carry = (jnp.zeros((1, width), F32), jnp.zeros((1, width), F32))
    for first, count, step in ranges:
        carry = lax.fori_loop(0, count, lambda j, c, first=first, step=step: tile(first + step * j, c), carry,
                              unroll=unroll(count))


def _log1p_pos(y):
    return jnp.where(y < 0.01, y * (1.0 - y * (0.5 - y * (1.0 / 3.0 - 0.25 * y))), jnp.log(1.0 + y))


def _softplus(x):
    return jnp.maximum(x, 0.0) + _log1p_pos(jnp.exp(-jnp.abs(x)))


def _neg_expm1(z):
    series = -z * (1.0 + z * (0.5 + z * (1.0 / 6.0 + z * (1.0 / 24.0 + z * (1.0 / 120.0)))))
    return jnp.where(z > -0.1, series, 1.0 - jnp.exp(z))


def _lru_gates(u, w_a, b_a, w_i, b_i, lam):
    ub = u.astype(BF16)
    r = jax.nn.sigmoid(jnp.dot(ub, w_a.astype(BF16), preferred_element_type=F32) + b_a)
    i = jax.nn.sigmoid(jnp.dot(ub, w_i.astype(BF16), preferred_element_type=F32) + b_i)
    log_a = (-LRU_C) * _softplus(-lam) * r
    return jnp.exp(log_a), jnp.sqrt(_neg_expm1(2.0 * log_a)) * (i * u)


LRU_PER_STEP = 4
LRU_PER_STEP_BWD = 2


def _lru_specs(per, bw, order):
    w = pl.BlockSpec((2, per, bw, bw), lambda *g: (0, order(*g), 0, 0))
    v = pl.BlockSpec((2, per, 1, bw), lambda *g: (0, order(*g), 0, 0))
    return [w, v, w, v, v]


def lru_fwd(name, u, w_a, b_a, w_i, b_i, lam, s_ctx):
    nb, s, _ = u.shape
    nblk, bw = w_a.shape[1], w_a.shape[2]
    per = min(LRU_PER_STEP, nblk)

    def body(u_ref, wa, ba, wi, bi, lm, o_ref, a_s, x_s, h_s):
        for d in (0, 1):
            for k in range(per):
                cols = slice(k * bw, (k + 1) * bw)
                a, bx = _lru_gates(u_ref[:, cols], wa[d, k], ba[d, k], wi[d, k], bi[d, k], lm[d, k])
                a_s[:, cols] = a
                x_s[:, cols] = bx
            _scan_real(a_s, x_s, h_s, None, _scan_order(d, False, s_ctx, s))
            if d == 0:
                o_ref[...] = h_s[...]
            else:
                o_ref[...] += h_s[...]

    blk = pl.BlockSpec((None, s, per * bw), lambda b, n: (b, 0, n))
    return pl.pallas_call(
        body,
        name=name,
        grid=(nb, nblk // per),
        in_specs=[blk] + _lru_specs(per, bw, lambda b, n: n),
        out_specs=blk,
        out_shape=jax.ShapeDtypeStruct(u.shape, F32),
        scratch_shapes=[pltpu.VMEM((s, per * bw), F32)] * 3,
        compiler_params=_cparams(("parallel", "parallel")),
    )(u, w_a, b_a, w_i, b_i, lam)


def lru_bwd(name, u, w_a, b_a, w_i, b_i, lam, dh, s_ctx):
    nb, s, _ = u.shape
    nblk, bw = w_a.shape[1], w_a.shape[2]
    per = min(LRU_PER_STEP_BWD, nblk)

    def body(u_ref, wa, ba, wi, bi, lm, dh_ref, du_ref, dwa, dba, dwi, dbi, dlm, a_s, x_s, hp_s, wp_s):
        b = pl.program_id(1)
        for d in (0, 1):
            for k in range(per):
                cols = slice(k * bw, (k + 1) * bw)
                a, bx = _lru_gates(u_ref[:, cols], wa[d, k], ba[d, k], wi[d, k], bi[d, k], lm[d, k])
                a_s[:, cols] = a
                x_s[:, cols] = bx
            _scan_real(a_s, x_s, None, hp_s, _scan_order(d, False, s_ctx, s))
            x_s[...] = a_s[...] * dh_ref[...]
            _scan_real(a_s, x_s, None, wp_s, _scan_order(d, True, s_ctx, s))
            for k in range(per):
                cols = slice(k * bw, (k + 1) * bw)
                g = dh_ref[:, cols] + wp_s[:, cols]
                _, vjp = jax.vjp(_lru_gates, u_ref[:, cols], wa[d, k], ba[d, k], wi[d, k], bi[d, k], lm[d, k])
                grads = vjp((g * hp_s[:, cols], g))
                if d == 0:
                    du_ref[:, cols] = grads[0]
                else:
                    du_ref[:, cols] += grads[0]
                for ref, val in zip((dwa, dba, dwi, dbi, dlm), grads[1:]):
                    @pl.when(b == 0)
                    def _(ref=ref, val=val, k=k):
                        ref[d, k] = val

                    @pl.when(b > 0)
                    def _(ref=ref, val=val, k=k):
                        ref[d, k] += val

    blk = pl.BlockSpec((None, s, per * bw), lambda n, b: (b, 0, n))
    pspecs = _lru_specs(per, bw, lambda n, b: n)
    return pl.pallas_call(
        body,
        name=name,
        grid=(nblk // per, nb),
        in_specs=[blk] + pspecs + [blk],
        out_specs=[blk] + pspecs,
        out_shape=[jax.ShapeDtypeStruct(u.shape, F32)] + [jax.ShapeDtypeStruct(p.shape, F32) for p in (w_a, b_a, w_i, b_i, lam)],
        scratch_shapes=[pltpu.VMEM((s, per * bw), F32)] * 4,
        compiler_params=_cparams(("parallel", "arbitrary")),
    )(u, w_a, b_a, w_i, b_i, lam, dh)


S5_TILE_CH = 128
S5_TILE_STATES = S5_TILE_CH // S5_GROUP_CH * S5_STATE


def _dot_nt(a, b):
    return lax.dot_general(a, b, (((1,), (1,)), ((), ())), preferred_element_type=F32)


def _dot_tn(a, b):
    return lax.dot_general(a, b, (((0,), (0,)), ((), ())), preferred_element_type=F32)


def _s5_specs(order):
    lam = pl.BlockSpec((2, 1, S5_TILE_STATES), lambda *g: (0, 0, order(*g)))
    mat = pl.BlockSpec((2, None, S5_TILE_STATES, S5_TILE_CH), lambda *g: (0, order(*g), 0, 0))
    return [lam, lam, mat, mat, mat, mat]


def s5_fwd(name, u, lam_r, lam_i, bt_r, bt_i, ct_r, ct_i, s_ctx):
    nb, s, w = u.shape

    def body(u_ref, lr, li, btr, bti, ctr, cti, o_ref, xr_s, xi_s):
        ub = u_ref[...].astype(BF16)
        for d in (0, 1):
            xr_s[...] = _dot_nt(ub, btr[d].astype(BF16))
            xi_s[...] = _dot_nt(ub, bti[d].astype(BF16))
            _scan_cplx(lr[d], li[d], xr_s, xi_s, None, None, _scan_order(d, False, s_ctx, s))
            y = (jnp.dot(xr_s[...].astype(BF16), ctr[d].astype(BF16), preferred_element_type=F32)
                 - jnp.dot(xi_s[...].astype(BF16), cti[d].astype(BF16), preferred_element_type=F32))
            if d == 0:
                o_ref[...] = y
            else:
                o_ref[...] += y

    blk = pl.BlockSpec((None, s, S5_TILE_CH), lambda b, j: (b, 0, j))
    return pl.pallas_call(
        body,
        name=name,
        grid=(nb, w // S5_TILE_CH),
        in_specs=[blk] + _s5_specs(lambda b, j: j),
        out_specs=blk,
        out_shape=jax.ShapeDtypeStruct(u.shape, F32),
        scratch_shapes=[pltpu.VMEM((s, S5_TILE_STATES), F32)] * 2,
        compiler_params=_cparams(("parallel", "parallel")),
    )(u, lam_r, lam_i, bt_r, bt_i, ct_r, ct_i)


def s5_bwd(name, u, lam_r, lam_i, bt_r, bt_i, ct_r, ct_i, dy, s_ctx):
    nb, s, w = u.shape

    def body(u_ref, lr, li, btr, bti, ctr, cti, dy_ref, du_ref, dlr, dli, dbtr, dbti, dctr, dcti,
             hr_s, hi_s, hpr_s, hpi_s, gr_s, gi_s):
        b = pl.program_id(1)
        ub = u_ref[...].astype(BF16)
        dyb = dy_ref[...].astype(BF16)
        du = jnp.zeros((s, S5_TILE_CH), F32)
        for d in (0, 1):
            hr_s[...] = _dot_nt(ub, btr[d].astype(BF16))
            hi_s[...] = _dot_nt(ub, bti[d].astype(BF16))
            _scan_cplx(lr[d], li[d], hr_s, hi_s, hpr_s, hpi_s, _scan_order(d, False, s_ctx, s))
            d_ctr = _dot_tn(hr_s[...].astype(BF16), dyb)
            d_cti = -_dot_tn(hi_s[...].astype(BF16), dyb)
            gr_s[...] = _dot_nt(dyb, ctr[d].astype(BF16))
            gi_s[...] = -_dot_nt(dyb, cti[d].astype(BF16))
            _scan_cplx(lr[d], -li[d], gr_s, gi_s, None, None, _scan_order(d, True, s_ctx, s))
            gr, gi = gr_s[...], gi_s[...]
            hpr, hpi = hpr_s[...], hpi_s[...]
            d_lr = jnp.sum(gr * hpr + gi * hpi, axis=0, keepdims=True)
            d_li = jnp.sum(gi * hpr - gr * hpi, axis=0, keepdims=True)
            grb, gib = gr.astype(BF16), gi.astype(BF16)
            du = du + jnp.dot(grb, btr[d].astype(BF16), preferred_element_type=F32)
            du = du + jnp.dot(gib, bti[d].astype(BF16), preferred_element_type=F32)
            d_btr = _dot_tn(grb, ub)
            d_bti = _dot_tn(gib, ub)
            for ref, val in zip((dlr, dli, dbtr, dbti, dctr, dcti), (d_lr, d_li, d_btr, d_bti, d_ctr, d_cti)):
                @pl.when(b == 0)
                def _(ref=ref, val=val):
                    ref[d] = val

                @pl.when(b > 0)
                def _(ref=ref, val=val):
                    ref[d] += val
        du_ref[...] = du

    blk = pl.BlockSpec((None, s, S5_TILE_CH), lambda j, b: (b, 0, j))
    pspecs = _s5_specs(lambda j, b: j)
    params = (lam_r, lam_i, bt_r, bt_i, ct_r, ct_i)
    return pl.pallas_call(
        body,
        name=name,
        grid=(w // S5_TILE_CH, nb),
        in_specs=[blk] + pspecs + [blk],
        out_specs=[blk] + pspecs,
        out_shape=[jax.ShapeDtypeStruct(u.shape, F32)] + [jax.ShapeDtypeStruct(p.shape, F32) for p in params],
        scratch_shapes=[pltpu.VMEM((s, S5_TILE_STATES), F32)] * 6,
        compiler_params=_cparams(("parallel", "arbitrary")),
    )(u, lam_r, lam_i, bt_r, bt_i, ct_r, ct_i, dy)


def small_fwd(name, f, ins, out_shapes):
    n = len(ins)

    def body(*refs):
        for o, r in zip(refs[n:], f([r[...] for r in refs[:n]])):
            o[...] = r

    return pl.pallas_call(
        body, name=name,
        out_shape=[jax.ShapeDtypeStruct(s, F32) for s in out_shapes],
        compiler_params=_cparams(),
    )(*ins)


def small_bwd(name, f, ins, cots):
    n, nc = len(ins), len(cots)

    def body(*refs):
        _, vjp = jax.vjp(f, [r[...] for r in refs[:n]])
        (grads,) = vjp([r[...] for r in refs[n:n + nc]])
        for o, r in zip(refs[n + nc:], grads):
            o[...] = r

    return pl.pallas_call(
        body, name=name,
        out_shape=[jax.ShapeDtypeStruct(a.shape, F32) for a in ins],
        compiler_params=_cparams(),
    )(*ins, *cots)


def _row(x, r):
    return jnp.sum(jnp.where(lax.broadcasted_iota(jnp.int32, x.shape, 0) == r, x, 0.0), axis=0, keepdims=True)


def _col(x, c):
    return jnp.sum(jnp.where(lax.broadcasted_iota(jnp.int32, x.shape, 1) == c, x, 0.0), axis=1, keepdims=True)


def _chunk_at(i, reverse, ncc, nc):
    if not reverse:
        return i
    return jnp.where(i < ncc, ncc - 1 - i, nc - 1 - (i - ncc))


def _tri(n, reverse):
    li = lax.broadcasted_iota(jnp.int32, (n, n), 0)
    si = lax.broadcasted_iota(jnp.int32, (n, n), 1)
    return jnp.where((li <= si) if reverse else (li >= si), 1.0, 0.0)


_HI = lax.Precision.HIGHEST


def _ssd_chunk(xs, bm, cm, dtc, dtr, a_row, a_col, hs, reverse):
    n = bm.shape[0]
    last = 0 if reverse else n - 1
    tri = _tri(n, reverse)
    cum_c = jnp.dot(tri, dtc * -jnp.exp(a_row), precision=_HI, preferred_element_type=F32)
    cum_r = lax.dot_general(dtr * -jnp.exp(a_col), tri, (((1,), (1,)), ((), ())), precision=_HI, preferred_element_type=F32)
    tot_r = _row(cum_c, last)
    bmb, cmb = bm.astype(BF16), cm.astype(BF16)
    cb = _dot_nt(cmb, bmb)
    ys, hn = [], []
    for hd in range(len(xs)):
        cl = _col(cum_c, hd)
        tot = _col(tot_r, hd)
        decay = jnp.exp(jnp.where(tri > 0.0, cl - _row(cum_r, hd), -jnp.inf))
        xd = xs[hd] * _col(dtc, hd)
        y = jnp.dot((cb * decay).astype(BF16), xd.astype(BF16), preferred_element_type=F32)
        y = y + _dot_nt(cmb, hs[hd].astype(BF16)) * jnp.exp(cl)
        hnew = hs[hd] * jnp.exp(tot) + _dot_tn((xd * jnp.exp(tot - cl)).astype(BF16), bmb)
        ys.append(y)
        hn.append(hnew)
    return ys, hn


def _ssd_specs(reverse, ncc, nc, order):
    ch = lambda *g: _chunk_at(order(*g)[1], reverse, ncc, nc)
    b_ = lambda *g: order(*g)[0]
    gn = SSD_GROUPS * SSD_STATE
    return [
        pl.BlockSpec((None, SSD_CHUNK, SSD_W), lambda *g: (b_(*g), ch(*g), 0)),
        pl.BlockSpec((None, SSD_CHUNK, gn), lambda *g: (b_(*g), ch(*g), SSD_W // gn)),
        pl.BlockSpec((None, SSD_CHUNK, gn), lambda *g: (b_(*g), ch(*g), SSD_W // gn + 1)),
        pl.BlockSpec((None, SSD_GROUPS, SSD_CHUNK, SSD_HPG), lambda *g: (b_(*g), 0, ch(*g), 0)),
        pl.BlockSpec((None, SSD_GROUPS, SSD_HPG, SSD_CHUNK), lambda *g: (b_(*g), 0, 0, ch(*g))),
        pl.BlockSpec((SSD_GROUPS, 1, SSD_HPG), lambda *g: (0, 0, 0)),
        pl.BlockSpec((SSD_GROUPS, SSD_HPG, 1), lambda *g: (0, 0, 0)),
    ]


def _ssd_group_inputs(g, x_ref, bm_ref, cm_ref, dtc_ref, dtr_ref, ar_ref, ac_ref):
    p, n = SSD_HEAD_DIM, SSD_STATE
    xs = [x_ref[:, p * (SSD_HPG * g + hd):p * (SSD_HPG * g + hd + 1)] for hd in range(SSD_HPG)]
    return xs, bm_ref[:, n * g:n * (g + 1)], cm_ref[:, n * g:n * (g + 1)], dtc_ref[g], dtr_ref[g], ar_ref[g], ac_ref[g]


def ssd_fwd(name, xbc, dt_col, dt_row, a_row, a_col, reverse, s_ctx):
    nb, s, _ = xbc.shape
    nc, ncc = s // SSD_CHUNK, s_ctx // SSD_CHUNK
    p = SSD_HEAD_DIM

    def body(x_ref, bm_ref, cm_ref, dtc_ref, dtr_ref, ar_ref, ac_ref, y_ref, hst_ref, h_s):
        i = pl.program_id(1)

        @pl.when(i == 0)
        def _():
            h_s[...] = jnp.zeros_like(h_s)

        hst_ref[...] = h_s[...]
        for g in range(SSD_GROUPS):
            xs, bm, cm, dtc, dtr, ar, ac = _ssd_group_inputs(g, x_ref, bm_ref, cm_ref, dtc_ref, dtr_ref, ar_ref, ac_ref)
            ys, hn = _ssd_chunk(xs, bm, cm, dtc, dtr, ar, ac, [h_s[g, hd] for hd in range(SSD_HPG)], reverse)
            for hd in range(SSD_HPG):
                y_ref[:, p * (SSD_HPG * g + hd):p * (SSD_HPG * g + hd + 1)] = ys[hd]
                h_s[g, hd] = hn[hd]

    state = (SSD_GROUPS, SSD_HPG, SSD_HEAD_DIM, SSD_STATE)
    return pl.pallas_call(
        body,
        name=name,
        grid=(nb, nc),
        in_specs=_ssd_specs(reverse, ncc, nc, lambda b, i: (b, i)),
        out_specs=[pl.BlockSpec((None, SSD_CHUNK, SSD_W), lambda b, i: (b, _chunk_at(i, reverse, ncc, nc), 0)),
                   pl.BlockSpec((None, None) + state, lambda b, i: (b, i, 0, 0, 0, 0))],
        out_shape=[jax.ShapeDtypeStruct((nb, s, SSD_W), F32), jax.ShapeDtypeStruct((nb, nc) + state, F32)],
        scratch_shapes=[pltpu.VMEM(state, F32)],
        compiler_params=_cparams(("parallel", "arbitrary")),
    )(xbc, xbc, xbc, dt_col, dt_row, a_row, a_col)


def ssd_bwd(name, xbc, dt_col, dt_row, a_row, a_col, hst, dy, reverse, s_ctx):
    nb, s, _ = xbc.shape
    nc, ncc = s // SSD_CHUNK, s_ctx // SSD_CHUNK
    p, n = SSD_HEAD_DIM, SSD_STATE

    def body(x_ref, bm_ref, cm_ref, dtc_ref, dtr_ref, ar_ref, ac_ref, hst_ref, dy_ref,
             dx_ref, dbm_ref, dcm_ref, ddtc_ref, ddtr_ref, dar_ref, dac_ref, dh_s):
        i = pl.program_id(1)

        @pl.when(i == 0)
        def _():
            dh_s[...] = jnp.zeros_like(dh_s)

        for g in range(SSD_GROUPS):
            xs, bm, cm, dtc, dtr, ar, ac = _ssd_group_inputs(g, x_ref, bm_ref, cm_ref, dtc_ref, dtr_ref, ar_ref, ac_ref)
            hs = [hst_ref[g, hd] for hd in range(SSD_HPG)]
            _, vjp = jax.vjp(functools.partial(_ssd_chunk, reverse=reverse), xs, bm, cm, dtc, dtr, ar, ac, hs)
            dys = [dy_ref[:, p * (SSD_HPG * g + hd):p * (SSD_HPG * g + hd + 1)] for hd in range(SSD_HPG)]
            dxs, dbm, dcm, ddtc, ddtr, dar, dac, dhs = vjp((dys, [dh_s[g, hd] for hd in range(SSD_HPG)]))
            for hd in range(SSD_HPG):
                dx_ref[:, p * (SSD_HPG * g + hd):p * (SSD_HPG * g + hd + 1)] = dxs[hd]
                dh_s[g, hd] = dhs[hd]
            dbm_ref[:, n * g:n * (g + 1)] = dbm
            dcm_ref[:, n * g:n * (g + 1)] = dcm
            ddtc_ref[g] = ddtc
            ddtr_ref[g] = ddtr

            @pl.when(i == 0)
            def _(g=g, dar=dar, dac=dac):
                dar_ref[g] = dar
                dac_ref[g] = dac

            @pl.when(i > 0)
            def _(g=g, dar=dar, dac=dac):
                dar_ref[g] += dar
                dac_ref[g] += dac

    ch = lambda b, i: _chunk_at(nc - 1 - i, reverse, ncc, nc)
    state = (SSD_GROUPS, SSD_HPG, SSD_HEAD_DIM, SSD_STATE)
    gn = SSD_GROUPS * SSD_STATE
    in_specs = _ssd_specs(reverse, ncc, nc, lambda b, i: (b, nc - 1 - i)) + [
        pl.BlockSpec((None, None) + state, lambda b, i: (b, nc - 1 - i, 0, 0, 0, 0)),
        pl.BlockSpec((None, SSD_CHUNK, SSD_W), lambda b, i: (b, ch(b, i), 0))]
    out_specs = [
        pl.BlockSpec((None, SSD_CHUNK, SSD_W), lambda b, i: (b, ch(b, i), 0)),
        pl.BlockSpec((None, SSD_CHUNK, gn), lambda b, i: (b, ch(b, i), 0)),
        pl.BlockSpec((None, SSD_CHUNK, gn), lambda b, i: (b, ch(b, i), 0)),
        pl.BlockSpec((None, SSD_GROUPS, SSD_CHUNK, SSD_HPG), lambda b, i: (b, 0, ch(b, i), 0)),
        pl.BlockSpec((None, SSD_GROUPS, SSD_HPG, SSD_CHUNK), lambda b, i: (b, 0, 0, ch(b, i))),
        pl.BlockSpec((None, SSD_GROUPS, 1, SSD_HPG), lambda b, i: (b, 0, 0, 0)),
        pl.BlockSpec((None, SSD_GROUPS, SSD_HPG, 1), lambda b, i: (b, 0, 0, 0)),
    ]
    out_shape = [
        jax.ShapeDtypeStruct((nb, s, SSD_W), F32),
        jax.ShapeDtypeStruct((nb, s, gn), F32),
        jax.ShapeDtypeStruct((nb, s, gn), F32),
        jax.ShapeDtypeStruct(dt_col.shape, F32),
        jax.ShapeDtypeStruct(dt_row.shape, F32),
        jax.ShapeDtypeStruct((nb, SSD_GROUPS, 1, SSD_HPG), F32),
        jax.ShapeDtypeStruct((nb, SSD_GROUPS, SSD_HPG, 1), F32),
    ]
    return pl.pallas_call(
        body,
        name=name,
        grid=(nb, nc),
        in_specs=in_specs,
        out_specs=out_specs,
        out_shape=out_shape,
        scratch_shapes=[pltpu.VMEM(state, F32)],
        compiler_params=_cparams(("parallel", "arbitrary")),
    )(xbc, xbc, xbc, dt_col, dt_row, a_row, a_col, hst, dy)


HG_TILES = HG_CHUNK // SUBLANES


def _hg_cum_tiles(x_t, reverse):
    row = lax.broadcasted_iota(jnp.int32, x_t[0].shape, 0)
    out = [None] * len(x_t)
    off = None
    for i in (reversed(range(len(x_t))) if reverse else range(len(x_t))):
        c = x_t[i]
        for k in (1, 2, 4):
            keep = (row < SUBLANES - k) if reverse else (row >= k)
            c = jnp.where(keep, c + pltpu.roll(c, SUBLANES - k if reverse else k, axis=0), c)
        out[i] = c if off is None else c + off
        off = _last_row(out[i], reverse)
    return out, off


def _hg_pairs(reverse):
    row = lax.broadcasted_iota(jnp.int32, (SUBLANES, HG_DK), 0)
    rots = []
    for r in range(SUBLANES):
        rots.append(((SUBLANES - r) % SUBLANES, row <= SUBLANES - 1 - r) if reverse else (r, row >= r))
    return [(j, [i for i in range(HG_TILES) if (i <= j if reverse else i >= j)], rots) for j in range(HG_TILES)]


def _rot(x, sh):
    return pltpu.roll(x, sh, axis=0) if sh else x


def _cat(tiles):
    return jnp.concatenate(tiles, axis=0)


def _hg_chunk_fwd(q_t, k_t, lf_t, v_t, st, reverse):
    cum_t, tot = _hg_cum_tiles(lf_t, reverse)
    y_t = [jnp.zeros(v_t[0].shape, F32) for _ in v_t]
    for j, l_tiles, rots in _hg_pairs(reverse):
        for sh, diag_ok in rots:
            k_j, c_j, v_j = _rot(k_t[j], sh), _rot(cum_t[j], sh), _rot(v_t[j], sh)
            for i in l_tiles:
                e = jnp.exp(cum_t[i] - c_j)
                if i == j:
                    e = jnp.where(diag_ok, e, 0.0)
                att = jnp.sum(q_t[i] * (k_j * e), axis=1, keepdims=True)
                y_t[i] = y_t[i] + att * v_j
    q, k, v, cum = _cat(q_t), _cat(k_t), _cat(v_t), _cat(cum_t)
    y_state = _dot_nt((q * jnp.exp(cum)).astype(BF16), st.astype(BF16))
    st_new = st * jnp.exp(tot) + _dot_tn(v.astype(BF16), (k * jnp.exp(tot - cum)).astype(BF16))
    return [y + y_state[SUBLANES * i:SUBLANES * (i + 1)] for i, y in enumerate(y_t)], st_new


def _hg_chunk_bwd(q_t, k_t, lf_t, v_t, st, dy_t, dst_new, reverse):
    nt = len(q_t)
    cum_t, tot = _hg_cum_tiles(lf_t, reverse)
    q, k, v, cum, dy = _cat(q_t), _cat(k_t), _cat(v_t), _cat(cum_t), _cat(dy_t)
    e_cum, e_tot, e_end = jnp.exp(cum), jnp.exp(tot), jnp.exp(tot - cum)
    qt, khat = q * e_cum, k * e_end
    dyb, dsb = dy.astype(BF16), dst_new.astype(BF16)
    dqt = jnp.dot(dyb, st.astype(BF16), preferred_element_type=F32)
    dst = dst_new * e_tot + _dot_tn(dyb, qt.astype(BF16))
    dv = _dot_nt(khat.astype(BF16), dsb)
    dkhat = jnp.dot(v.astype(BF16), dsb, preferred_element_type=F32)
    t1 = dkhat * khat
    dtot = jnp.sum(dst_new * st, axis=0, keepdims=True) * e_tot + jnp.sum(t1, axis=0, keepdims=True)
    rows = lax.broadcasted_iota(jnp.int32, cum.shape, 0)
    last = 0 if reverse else cum.shape[0] - 1
    dcum = dqt * qt - t1 + jnp.where(rows == last, dtot, 0.0)
    tiles = lambda a: [a[SUBLANES * i:SUBLANES * (i + 1)] for i in range(nt)]
    dq_t, dk_t, dv_t, dcum_t = tiles(dqt * e_cum), tiles(dkhat * e_end), tiles(dv), tiles(dcum)
    for j, l_tiles, rots in _hg_pairs(reverse):
        for sh, diag_ok in rots:
            k_j, c_j, v_j = _rot(k_t[j], sh), _rot(cum_t[j], sh), _rot(v_t[j], sh)
            acc_v = acc_k = acc_c = None
            for i in l_tiles:
                e = jnp.exp(cum_t[i] - c_j)
                if i == j:
                    e = jnp.where(diag_ok, e, 0.0)
                ke, qe = k_j * e, q_t[i] * e
                p = q_t[i] * ke
                att = jnp.sum(p, axis=1, keepdims=True)
                datt = jnp.sum(dy_t[i] * v_j, axis=1, keepdims=True)
                g = datt * p
                dq_t[i] = dq_t[i] + datt * ke
                dcum_t[i] = dcum_t[i] + g
                av, ak = att * dy_t[i], datt * qe
                acc_v, acc_k, acc_c = (av, ak, g) if acc_v is None else (acc_v + av, acc_k + ak, acc_c + g)
            back = (SUBLANES - sh) % SUBLANES
            dv_t[j] = dv_t[j] + _rot(acc_v, back)
            dk_t[j] = dk_t[j] + _rot(acc_k, back)
            dcum_t[j] = dcum_t[j] - _rot(acc_c, back)
    dlf_t, _ = _hg_cum_tiles(dcum_t, not reverse)
    return dq_t, dk_t, dlf_t, dv_t, dst


def _hg_super(s_ctx):
    return min(256, s_ctx)


def hg_fwd(name, q, k, lf, v, reverse, s_ctx):
    nb, s, w = q.shape
    nh, dk, sup = w // HG_DK, HG_DK, _hg_super(s_ctx)
    nsup, nsc, cps = s // sup, s_ctx // sup, sup // HG_CHUNK

    def body(q_ref, k_ref, lf_ref, v_ref, y_ref, hst_ref, st_s):
        i = pl.program_id(2)

        @pl.when(i == 0)
        def _():
            st_s[...] = jnp.zeros_like(st_s)

        def step(c, st):
            r0 = pl.multiple_of((cps - 1 - c if reverse else c) * HG_CHUNK, HG_CHUNK)
            tile = lambda ref: [ref[pl.ds(r0 + SUBLANES * i, SUBLANES), :] for i in range(HG_TILES)]
            hst_ref[c] = st
            y_t, st_new = _hg_chunk_fwd(tile(q_ref), tile(k_ref), tile(lf_ref), tile(v_ref), st, reverse)
            for i in range(HG_TILES):
                y_ref[pl.ds(r0 + SUBLANES * i, SUBLANES), :] = y_t[i]
            return st_new

        st_s[...] = lax.fori_loop(0, cps, step, st_s[...], unroll=2 if cps % 2 == 0 else 1)

    blk = pl.BlockSpec((None, sup, dk), lambda b, h, i: (b, _chunk_at(i, reverse, nsc, nsup), h))
    return pl.pallas_call(
        body,
        name=name,
        grid=(nb, nh, nsup),
        in_specs=[blk] * 4,
        out_specs=[blk, pl.BlockSpec((None, None, cps, dk, dk), lambda b, h, i: (b, h, i, 0, 0))],
        out_shape=[jax.ShapeDtypeStruct(q.shape, F32), jax.ShapeDtypeStruct((nb, nh, s // HG_CHUNK, dk, dk), F32)],
        scratch_shapes=[pltpu.VMEM((dk, dk), F32)],
        compiler_params=_cparams(("parallel", "parallel", "arbitrary")),
    )(q, k, lf, v)


def hg_bwd(name, q, k, lf, v, hst, dy, reverse, s_ctx):
    nb, s, w = q.shape
    nh, dk, sup = w // HG_DK, HG_DK, _hg_super(s_ctx)
    nsup, nsc, cps = s // sup, s_ctx // sup, sup // HG_CHUNK

    def body(q_ref, k_ref, lf_ref, v_ref, hst_ref, dy_ref, dq_ref, dk_ref, dlf_ref, dv_ref, dst_s):
        i = pl.program_id(2)

        @pl.when(i == 0)
        def _():
            dst_s[...] = jnp.zeros_like(dst_s)

        def step(cc, dst):
            c = cps - 1 - cc
            r0 = pl.multiple_of((cps - 1 - c if reverse else c) * HG_CHUNK, HG_CHUNK)
            tile = lambda ref: [ref[pl.ds(r0 + SUBLANES * i, SUBLANES), :] for i in range(HG_TILES)]
            dq_t, dk_t, dlf_t, dv_t, dst_prev = _hg_chunk_bwd(
                tile(q_ref), tile(k_ref), tile(lf_ref), tile(v_ref), hst_ref[c], tile(dy_ref), dst, reverse)
            for ref, val in zip((dq_ref, dk_ref, dlf_ref, dv_ref), (dq_t, dk_t, dlf_t, dv_t)):
                for i in range(HG_TILES):
                    ref[pl.ds(r0 + SUBLANES * i, SUBLANES), :] = val[i]
            return dst_prev

        dst_s[...] = lax.fori_loop(0, cps, step, dst_s[...], unroll=2 if cps % 2 == 0 else 1)

    blk = pl.BlockSpec((None, sup, dk), lambda b, h, i: (b, _chunk_at(nsup - 1 - i, reverse, nsc, nsup), h))
    return pl.pallas_call(
        body,
        name=name,
        grid=(nb, nh, nsup),
        in_specs=[blk] * 4 + [pl.BlockSpec((None, None, cps, dk, dk), lambda b, h, i: (b, h, nsup - 1 - i, 0, 0)), blk],
        out_specs=[blk] * 4,
        out_shape=[jax.ShapeDtypeStruct(q.shape, F32)] * 4,
        scratch_shapes=[pltpu.VMEM((dk, dk), F32)],
        compiler_params=_cparams(("parallel", "parallel", "arbitrary")),
    )(q, k, lf, v, hst, dy)


def f_s5_discretize(ins):
    lam_re, lam_im, log_step, b_re, b_im = ins
    step = jnp.exp(log_step)
    mag = jnp.exp(lam_re * step)
    ar, ai = mag * jnp.cos(lam_im * step), mag * jnp.sin(lam_im * step)
    den = lam_re * lam_re + lam_im * lam_im
    zr = ((ar - 1.0) * lam_re + ai * lam_im) / den
    zi = (ai * lam_re - (ar - 1.0) * lam_im) / den
    return [ar, ai, zr * b_re - zi * b_im, zr * b_im + zi * b_re]


def s5_tiles_of(m):
    g, p, k = m.shape
    gt = S5_TILE_CH // k
    eye = jnp.eye(gt, dtype=m.dtype)
    t = m.reshape(g // gt, gt, p, 1, k) * eye[None, :, None, :, None]
    return t.reshape(g // gt, gt * p, gt * k)


def s5_groups_of(t, g, p, k):
    gt = S5_TILE_CH // k
    eye = jnp.eye(gt, dtype=t.dtype)
    return jnp.sum(t.reshape(g // gt, gt, p, gt, k) * eye[None, :, None, :, None], axis=3).reshape(g, p, k)


def f_lower_bounds(ins):
    (logits,) = ins
    e = jnp.exp(logits - jnp.max(logits, axis=0, keepdims=True))
    p = e / jnp.sum(e, axis=0, keepdims=True)
    n = logits.shape[0]
    li = lax.broadcasted_iota(jnp.int32, (n, n), 0)
    si = lax.broadcasted_iota(jnp.int32, (n, n), 1)
    after_first = jnp.where(jnp.logical_and(si >= 1, si <= li), 1.0, 0.0)
    return [jnp.dot(after_first, p, precision=_HI, preferred_element_type=F32)]


def f_silu(ins):
    return [_silu(ins[0])]


def f_norm_keep(shift_row, scale_row):
    def f(tv, mv, pv):
        return [tv[0], _rms(tv[0], pv[0]) * (1.0 + mv[scale_row]) + mv[shift_row]]
    return f


def f_dt(tv, mv, pv):
    return [_softplus(tv[0] + pv[0])]


def f_even_finish(tv, mv, pv):
    y_f, y_b, xs, z, h_sum, gy = tv
    d_exp, g = pv
    y = _rms((y_f + y_b + d_exp * xs) * _silu(z), g)
    return [y, h_sum * jax.nn.gelu(gy)]


def f_odd_prep(tv, mv, pv):
    q, f_f, f_b = tv
    (lb,) = pv
    outs = [_silu(q)]
    for f in (f_f, f_b):
        outs.append((1.0 - lb) * jax.nn.sigmoid(-f))
        outs.append(jnp.log(lb + (1.0 - lb) * jax.nn.sigmoid(f)))
    return outs


def f_odd_finish(tv, mv, pv):
    o_f, o_b, g, s5y, u = tv
    norm_g, s5_d, glu_w, glu_b = pv
    o = o_f + o_b
    w = o.shape[1]
    hi = lax.broadcasted_iota(jnp.int32, (w, w), 0) // HG_DK
    hj = lax.broadcasted_iota(jnp.int32, (w, w), 1) // HG_DK
    head_mean = jnp.where(hi == hj, 1.0 / HG_DK, 0.0)
    ms = jnp.dot(o * o, head_mean, precision=_HI, preferred_element_type=F32)
    on = o * lax.rsqrt(ms + RMS_EPS) * norm_g * _silu(g)
    y = jax.nn.gelu(s5y + s5_d * u)
    gate = jax.nn.sigmoid(jnp.dot(y.astype(BF16), glu_w.astype(BF16), preferred_element_type=F32) + glu_b)
    return [on, y * gate]


def final_loss(name, s, br, mod, g, target, s_ctx):
    nb, st, d = s.shape
    tb = TOK_BLOCK
    assert s_ctx == tb

    def lossf(sv, bv, gate, gv, tv):
        y = _rms(sv + gate * bv, gv)
        err = jnp.square(y - tv)
        return 0.5 * jnp.sum(jnp.mean(err, axis=-1, keepdims=True), axis=0, keepdims=True)

    def body(s_ref, b_ref, m_ref, g_ref, t_ref, l_ref, ds_ref, db_ref, dm_ref, dg_ref):
        b, t = pl.program_id(0), pl.program_id(1)

        @pl.when(t == 0)
        def _():
            ds_ref[...] = jnp.zeros_like(ds_ref)
            db_ref[...] = jnp.zeros_like(db_ref)
            dm_ref[...] = jnp.zeros_like(dm_ref)
            l_ref[...] = jnp.zeros_like(l_ref)

        @pl.when(jnp.logical_and(b == 0, t == 0))
        def _():
            dg_ref[...] = jnp.zeros_like(dg_ref)

        @pl.when(t > 0)
        def _():
            gate = m_ref[N_MOD - 1:N_MOD, :]
            l, vjp = jax.vjp(lossf, s_ref[...], b_ref[...], gate, g_ref[...], t_ref[...])
            ds, db, dgate, dg, _ = vjp(jnp.ones((1, 1), F32))
            ds_ref[...] = ds
            db_ref[...] = db.astype(db_ref.dtype)
            dg_ref[...] += dg
            l_ref[...] += jnp.broadcast_to(l, l_ref.shape)

            @pl.when(t == 1)
            def _():
                dm_ref[...] = jnp.zeros_like(dm_ref)
                dm_ref[N_MOD - 1:N_MOD, :] = dgate

            @pl.when(t > 1)
            def _():
                dm_ref[N_MOD - 1:N_MOD, :] += dgate

    tok = pl.BlockSpec((None, tb, d), lambda b, t: (b, t, 0))
    modspec = pl.BlockSpec((None, N_MOD, d), _mod_index)
    gspec = pl.BlockSpec((1, d), lambda b, t: (0, 0))
    return pl.pallas_call(
        body,
        name=name,
        grid=(nb, st // tb),
        in_specs=[tok, tok, modspec, gspec, pl.BlockSpec((None, tb, d), lambda b, t: (b, jnp.maximum(t - 1, 0), 0))],
        out_specs=[pl.BlockSpec((None, SUBLANES, 128), lambda b, t: (b, 0, 0)), tok, tok, modspec, gspec],
        out_shape=[jax.ShapeDtypeStruct((nb, SUBLANES, 128), F32), jax.ShapeDtypeStruct(s.shape, F32),
                   jax.ShapeDtypeStruct(s.shape, BF16), jax.ShapeDtypeStruct(mod.shape, F32), jax.ShapeDtypeStruct(g.shape, F32)],
        compiler_params=_cparams(("arbitrary", "arbitrary")),
    )(s, br, mod, g, target)


def adamw(name, w, g, m, v):
    shape = w.shape
    cols = shape[-1] if w.ndim >= 2 else w.size
    rows = w.size // cols
    tr = _pick(rows, (512, 256, 128, 64, 32, 16, 8))

    def body(w_ref, g_ref, m_ref, v_ref, d_ref, nm_ref, nv_ref):
        gv = g_ref[...]
        nm = ADAM_B1 * m_ref[...] + (1.0 - ADAM_B1) * gv
        nv = ADAM_B2 * v_ref[...] + (1.0 - ADAM_B2) * jnp.square(gv)
        m_hat = nm / (1.0 - ADAM_B1 ** ADAM_STEP)
        v_hat = nv / (1.0 - ADAM_B2 ** ADAM_STEP)
        d_ref[...] = -ADAM_LR * (m_hat / (jnp.sqrt(v_hat) + ADAM_EPS) + ADAM_WD * w_ref[...])
        nm_ref[...] = nm
        nv_ref[...] = nv

    spec = pl.BlockSpec((tr, cols), lambda i: (i, 0))
    outs = pl.pallas_call(
        body,
        name=name,
        grid=(rows // tr,),
        in_specs=[spec] * 4,
        out_specs=[spec] * 3,
        out_shape=[jax.ShapeDtypeStruct((rows, cols), F32)] * 3,
        compiler_params=_cparams(("parallel",)),
    )(*(a.reshape(rows, cols) for a in (w, g, m, v)))
    return tuple(o.reshape(shape) for o in outs)


EV_COLS = {"z": (0, 1024), "xbc": (1024, 2560), "dt": (2560, 2592), "gy": (2592, 3616), "u": (3616, 4640)}
OD_COLS = {"q": (0, 768), "ff": (768, 1536), "fb": (1536, 2304), "v": (2304, 3072), "g": (3072, 3840), "u": (3840, 4096)}
EV_OUT_ROWS = ((0, 1024), (1024, 2048))
OD_OUT_ROWS = ((0, 768), (768, 1024))
LANES = 128


def _pad_to_lanes(w):
    n = w.shape[1]
    return w if n % LANES == 0 else jnp.pad(w, ((0, 0), (0, LANES - n % LANES)))


def _layer_weights(l, big):
    j = l // 2
    even = l % 2 == 0
    w_in = big["ev_w_in" if even else "od_w_in"][j]
    w_out = big["ev_w_out" if even else "od_w_out"][j]
    lw = {"in": {}, "out": []}
    for name, (a, b) in (EV_COLS if even else OD_COLS).items():
        w = _pad_to_lanes(w_in[:, a:b])
        lw["in"][name] = (w, w.T)
    for a, b in (EV_OUT_ROWS if even else OD_OUT_ROWS):
        lw["out"].append((w_out[a:b], w_out[a:b].T))
    for name in ("gate", "up", "down"):
        w = big["ffn_w_" + name][l]
        lw[name] = (w, w.T)
    return lw


def _rows2d(a):
    return a.reshape(-1, a.shape[-1])


def _mm3(a, w, name, out_dtype=F32):
    return mm([(_rows2d(a), w)], name, out_dtype).reshape(a.shape[:-1] + (w.shape[1],))


def _wgrad(a, d, name):
    return mm_tn(_rows2d(a), _rows2d(d), name)


def _dgrad(pairs, name, shape3):
    return mm([(_rows2d(d), wt) for d, wt in pairs], name).reshape(shape3[:-1] + (pairs[0][1].shape[1],))


def _dir_dt(dt, d):
    nb, s, _ = dt.shape
    dd = dt[:, :, SSD_HEADS * d:SSD_HEADS * (d + 1)].reshape(nb, s, SSD_GROUPS, SSD_HPG)
    return jnp.transpose(dd, (0, 2, 1, 3)), jnp.transpose(dd, (0, 2, 3, 1))


def _s5_prepare(p, j, tag):
    g_, p_, k_ = S5_GROUPS, S5_STATE, S5_GROUP_CH
    col = lambda t: t.reshape(g_ * p_, 1)
    ins, outs = [], []
    for d in (0, 1):
        i_d = [col(p["s5_lam_re"][j, d]), col(p["s5_lam_im"][j, d]), col(jnp.repeat(p["s5_log_step"][j, d], p_)),
               p["s5_b_re"][j].reshape(g_ * p_, k_), p["s5_b_im"][j].reshape(g_ * p_, k_)]
        ins.append(i_d)
        outs.append(small_fwd(f"{tag}_disc{d}", f_s5_discretize, i_d, [(g_ * p_, 1)] * 2 + [(g_ * p_, k_)] * 2))
    lam_r = jnp.stack([o[0].reshape(1, g_ * p_) for o in outs])
    lam_i = jnp.stack([o[1].reshape(1, g_ * p_) for o in outs])
    bt_r = jnp.stack([s5_tiles_of(o[2].reshape(g_, p_, k_)) for o in outs])
    bt_i = jnp.stack([s5_tiles_of(o[3].reshape(g_, p_, k_)) for o in outs])
    ct_r = jnp.stack([s5_tiles_of(jnp.transpose(p["s5_c_re"][j, d], (0, 2, 1))) for d in (0, 1)])
    ct_i = jnp.stack([s5_tiles_of(jnp.transpose(p["s5_c_im"][j, d], (0, 2, 1))) for d in (0, 1)])
    return ins, (lam_r, lam_i, bt_r, bt_i, ct_r, ct_i)


def _s5_param_grads(ins, grads, tag):
    g_, p_, k_ = S5_GROUPS, S5_STATE, S5_GROUP_CH
    dlr, dli, dbtr, dbti, dctr, dcti = grads
    g_lre, g_lim, g_ls, g_bre, g_bim = [], [], [], 0.0, 0.0
    for d in (0, 1):
        cots = [dlr[d].reshape(g_ * p_, 1), dli[d].reshape(g_ * p_, 1),
                s5_groups_of(dbtr[d], g_, p_, k_).reshape(g_ * p_, k_), s5_groups_of(dbti[d], g_, p_, k_).reshape(g_ * p_, k_)]
        g = small_bwd(f"{tag}_disc_bwd{d}", f_s5_discretize, ins[d], cots)
        g_lre.append(g[0].reshape(g_, p_))
        g_lim.append(g[1].reshape(g_, p_))
        g_ls.append(g[2].reshape(g_, p_).sum(-1))
        g_bre = g_bre + g[3].reshape(g_, p_, k_)
        g_bim = g_bim + g[4].reshape(g_, p_, k_)
    g_cre = jnp.stack([jnp.transpose(s5_groups_of(dctr[d], g_, p_, k_), (0, 2, 1)) for d in (0, 1)])
    g_cim = jnp.stack([jnp.transpose(s5_groups_of(dcti[d], g_, p_, k_), (0, 2, 1)) for d in (0, 1)])
    return jnp.stack(g_lre), jnp.stack(g_lim), jnp.stack(g_ls), g_bre, g_bim, g_cre, g_cim


def _even_mixer_fwd(l, hn, p, lw, s_ctx):
    j = l // 2
    t1 = taps_1d(4, s_ctx, hn.shape[1])
    r = {"hn": hn}
    proj = {n: _mm3(hn, lw["in"][n][0], f"l{l}_proj_{n}") for n in EV_COLS}
    r["z"], r["xbc"], r["gy"], r["u"] = proj["z"], proj["xbc"], proj["gy"], proj["u"]
    r["dtp"] = proj["dt"][:, :, :2 * SSD_HEADS]
    r["xbc_c"] = conv_fwd(f"l{l}_ssd_conv", r["xbc"], p["ssd_conv_w"][j], p["ssd_conv_b"][j][None], t1, "silu")
    r["u_c"] = conv_fwd(f"l{l}_lru_conv", r["u"], p["lru_conv_w"][j], p["lru_conv_b"][j][None], t1, "none")
    r["dt_bias"] = p["ssd_dt_bias"][j].reshape(1, 2 * SSD_HEADS)
    (r["dt"],) = tok_fwd(f"l{l}_dt", f_dt, [r["dtp"]], None, [r["dt_bias"]], [2 * SSD_HEADS], [F32])
    r["ys"], r["hst"], r["dts"], r["alog"] = [], [], [], []
    for d in (0, 1):
        dtc, dtr = _dir_dt(r["dt"], d)
        al = p["ssd_a_log"][j, d].reshape(SSD_GROUPS, SSD_HPG)
        al_r, al_c = al[:, None, :], al[:, :, None]
        y, hst = ssd_fwd(f"l{l}_ssd_fwd{d}", r["xbc_c"], dtc, dtr, al_r, al_c, bool(d), s_ctx)
        r["ys"].append(y)
        r["hst"].append(hst)
        r["dts"].append((dtc, dtr))
        r["alog"].append((al_r, al_c))
    v4 = lambda t: t.reshape(2, LRU_BLOCKS, 1, LRU_BLOCK_W)
    r["lru_p"] = (p["lru_w_a"][j], v4(p["lru_b_a"][j]), p["lru_w_i"][j], v4(p["lru_b_i"][j]), v4(p["lru_lam"][j]))
    r["h_sum"] = lru_fwd(f"l{l}_lru_fwd", r["u_c"], *r["lru_p"], s_ctx)
    r["xs"] = r["xbc_c"][:, :, :SSD_HEADS * SSD_HEAD_DIM]
    r["fin_p"] = [jnp.repeat(p["ssd_d"][j], SSD_HEAD_DIM)[None], p["ssd_norm_g"][j][None]]
    r["fin_in"] = [r["ys"][0], r["ys"][1], r["xs"], r["z"], r["h_sum"], r["gy"]]
    r["o"] = tok_fwd(f"l{l}_even_finish", f_even_finish, r["fin_in"], None, r["fin_p"], [1024, 1024], [BF16, BF16])
    return r


def _even_mixer_bwd(l, r, dox, p, lw, s_ctx, grads):
    j = l // 2
    shape3 = dox.shape
    t1 = taps_1d(4, s_ctx, shape3[1])
    grads["ev_w_out"][j] = jnp.concatenate([_wgrad(o, dox, f"l{l}_dwout{i}") for i, o in enumerate(r["o"])], axis=0)
    do = [_dgrad([(dox, lw["out"][i][1])], f"l{l}_dout{i}", shape3) for i in range(2)]
    (dy, _, dxs, dz, dh_sum, dgy), _, (dd_exp, grads["ssd_norm_g"][j]) = tok_bwd(
        f"l{l}_even_finish_bwd", f_even_finish, r["fin_in"], None, r["fin_p"], do, [F32, F32, F32, BF16, F32, BF16])
    grads["ssd_d"][j] = dd_exp.reshape(SSD_HEADS, SSD_HEAD_DIM).sum(-1)
    du_c, dwa, dba, dwi, dbi, dlam = lru_bwd(f"l{l}_lru_bwd", r["u_c"], *r["lru_p"], dh_sum, s_ctx)
    grads["lru_w_a"][j], grads["lru_w_i"][j] = dwa, dwi
    v2 = lambda t: t.reshape(2, LRU_BLOCKS * LRU_BLOCK_W)
    grads["lru_b_a"][j], grads["lru_b_i"][j], grads["lru_lam"][j] = v2(dba), v2(dbi), v2(dlam)
    dx_sum, dbm_sum, dcm_sum, ddts, dalog = dxs, 0.0, 0.0, [], []
    for d in (0, 1):
        dx, dbm, dcm, ddtc, ddtr, dar, dac = ssd_bwd(
            f"l{l}_ssd_bwd{d}", r["xbc_c"], *r["dts"][d], *r["alog"][d], r["hst"][d], dy, bool(d), s_ctx)
        dx_sum, dbm_sum, dcm_sum = dx_sum + dx, dbm_sum + dbm, dcm_sum + dcm
        ddts.append((jnp.transpose(ddtc, (0, 2, 1, 3)) + jnp.transpose(ddtr, (0, 3, 1, 2))).reshape(shape3[0], shape3[1], SSD_HEADS))
        dalog.append((dar.sum(0)[:, 0, :] + dac.sum(0)[:, :, 0]).reshape(SSD_HEADS))
    grads["ssd_a_log"][j] = jnp.stack(dalog)
    dxbc_c = jnp.concatenate([dx_sum, dbm_sum, dcm_sum], axis=-1)
    (ddtp,), _, (ddt_bias,) = tok_bwd(f"l{l}_dt_bwd", f_dt, [r["dtp"]], None, [r["dt_bias"]], [jnp.concatenate(ddts, axis=-1)], [F32])
    grads["ssd_dt_bias"][j] = ddt_bias.reshape(2, SSD_HEADS)
    dxbc, grads["ssd_conv_w"][j], dcb = conv_bwd(f"l{l}_ssd_conv_bwd", r["xbc"], p["ssd_conv_w"][j], p["ssd_conv_b"][j][None], dxbc_c, t1, "silu", dx_dtype=BF16)
    du, grads["lru_conv_w"][j], dlb = conv_bwd(f"l{l}_lru_conv_bwd", r["u"], p["lru_conv_w"][j], p["lru_conv_b"][j][None], du_c, t1, "none", dx_dtype=BF16)
    grads["ssd_conv_b"][j], grads["lru_conv_b"][j] = dcb[0], dlb[0]
    dproj = {"z": dz, "xbc": dxbc, "dt": _pad_to_lanes(_rows2d(ddtp)).reshape(shape3[:2] + (LANES,)), "gy": dgy, "u": du}
    grads["ev_w_in"][j] = jnp.concatenate(
        [_wgrad(r["hn"], dproj[n], f"l{l}_dwin_{n}")[:, :b - a] for n, (a, b) in EV_COLS.items()], axis=1)
    return _dgrad([(dproj[n], lw["in"][n][1]) for n in EV_COLS], f"l{l}_dhn", shape3)


def _odd_mixer_fwd(l, hn, p, lw, lb_row, s_ctx):
    j = l // 2
    r = {"hn": hn}
    proj = {n: _mm3(hn, lw["in"][n][0], f"l{l}_proj_{n}") for n in OD_COLS}
    r["v"], r["g"], r["u"] = proj["v"], proj["g"], proj["u"]
    r["prep_in"] = [proj["q"], proj["ff"], proj["fb"]]
    r["lb"] = lb_row
    r["prep"] = tok_fwd(f"l{l}_odd_prep", f_odd_prep, r["prep_in"], None, [lb_row], [HG_W] * 5, [F32] * 5)
    qs = r["prep"][0]
    r["os"], r["hst"] = [], []
    for d in (0, 1):
        o, hst = hg_fwd(f"l{l}_hg_fwd{d}", qs, r["prep"][1 + 2 * d], r["prep"][2 + 2 * d], r["v"], bool(d), s_ctx)
        r["os"].append(o)
        r["hst"].append(hst)
    r["s5_ins"], r["s5_p"] = _s5_prepare(p, j, f"l{l}_s5")
    r["s5y"] = s5_fwd(f"l{l}_s5_fwd", r["u"], *r["s5_p"], s_ctx)
    r["fin_p"] = [p["hg_norm_g"][j].reshape(1, HG_W), p["s5_d"][j][None], p["s5_glu_w"][j], p["s5_glu_b"][j][None]]
    r["fin_in"] = [r["os"][0], r["os"][1], r["g"], r["s5y"], r["u"]]
    r["o"] = tok_fwd(f"l{l}_odd_finish", f_odd_finish, r["fin_in"], None, r["fin_p"], [HG_W, S5_W], [BF16, BF16])
    return r


def _odd_mixer_bwd(l, r, dox, p, lw, s_ctx, grads):
    j = l // 2
    shape3 = dox.shape
    grads["od_w_out"][j] = jnp.concatenate([_wgrad(o, dox, f"l{l}_dwout{i}") for i, o in enumerate(r["o"])], axis=0)
    do = [_dgrad([(dox, lw["out"][i][1])], f"l{l}_dout{i}", shape3) for i in range(2)]
    (do_hg, _, dg, ds5y, du_fin), _, (dng, grads["s5_d"][j], grads["s5_glu_w"][j], dglu_b) = tok_bwd(
        f"l{l}_odd_finish_bwd", f_odd_finish, r["fin_in"], None, r["fin_p"], do, [F32, F32, BF16, F32, F32])
    grads["hg_norm_g"][j] = dng.reshape(HG_HEADS, HG_DK)
    grads["s5_d"][j], grads["s5_glu_b"][j] = grads["s5_d"][j][0], dglu_b[0]
    s5g = s5_bwd(f"l{l}_s5_bwd", r["u"], *r["s5_p"], ds5y, s_ctx)
    du = s5g[0] + du_fin
    (grads["s5_lam_re"][j], grads["s5_lam_im"][j], grads["s5_log_step"][j], grads["s5_b_re"][j], grads["s5_b_im"][j],
     grads["s5_c_re"][j], grads["s5_c_im"][j]) = _s5_param_grads(r["s5_ins"], s5g[1:], f"l{l}_s5")
    qs = r["prep"][0]
    dqs, dv, dprep = 0.0, 0.0, [None] * 5
    for d in (0, 1):
        dq, dk, dlf, dvd = hg_bwd(f"l{l}_hg_bwd{d}", qs, r["prep"][1 + 2 * d], r["prep"][2 + 2 * d], r["v"], r["hst"][d], do_hg, bool(d), s_ctx)
        dqs, dv = dqs + dq, dv + dvd
        dprep[1 + 2 * d], dprep[2 + 2 * d] = dk, dlf
    dprep[0] = dqs
    (dq_, dff, dfb), _, (dlb,) = tok_bwd(f"l{l}_odd_prep_bwd", f_odd_prep, r["prep_in"], None, [r["lb"]], dprep, [BF16] * 3)
    dproj = {"q": dq_, "ff": dff, "fb": dfb, "v": dv, "g": dg, "u": du}
    grads["od_w_in"][j] = jnp.concatenate([_wgrad(r["hn"], dproj[n], f"l{l}_dwin_{n}") for n in OD_COLS], axis=1)
    return _dgrad([(dproj[n], lw["in"][n][1]) for n in OD_COLS], f"l{l}_dhn", shape3), dlb


def _ffn_fwd(l, fn, p, lw, s_ctx):
    r = {"fn": fn}
    tg = taps_grid(s_ctx, fn.shape[1], GRID_W)
    r["a"] = _mm3(fn, lw["gate"][0], f"l{l}_ffn_gate")
    r["up"] = _mm3(fn, lw["up"][0], f"l{l}_ffn_up")
    r["cw"], r["cb"] = p["ffn_conv_w"][l].reshape(9, D_FF), p["ffn_conv_b"][l][None]
    r["act"] = conv_fwd(f"l{l}_ffn_conv", r["a"], r["cw"], r["cb"], tg, "silu_mul", mul=r["up"], out_dtype=BF16)
    return r, _mm3(r["act"], lw["down"][0], f"l{l}_ffn_down")


def _ffn_bwd(l, r, dfo, lw, s_ctx, grads):
    shape3 = dfo.shape
    tg = taps_grid(s_ctx, shape3[1], GRID_W)
    grads["ffn_w_down"][l] = _wgrad(r["act"], dfo, f"l{l}_dwdown")
    dact = _dgrad([(dfo, lw["down"][1])], f"l{l}_dact", shape3)
    da, dcw, dcb, dup = conv_bwd(f"l{l}_ffn_conv_bwd", r["a"], r["cw"], r["cb"], dact, tg, "silu_mul", mul=r["up"], dx_dtype=BF16)
    grads["ffn_conv_w"][l], grads["ffn_conv_b"][l] = dcw.reshape(3, 3, D_FF), dcb[0]
    grads["ffn_w_gate"][l] = _wgrad(r["fn"], da, f"l{l}_dwgate")
    grads["ffn_w_up"][l] = _wgrad(r["fn"], dup, f"l{l}_dwup")
    return _dgrad([(da, lw["gate"][1]), (dup, lw["up"][1])], f"l{l}_dfn", shape3)


BIG_WEIGHTS = ("ev_w_in", "ev_w_out", "od_w_in", "od_w_out", "ffn_w_gate", "ffn_w_up", "ffn_w_down")
PER_LAYER = {"norm_mix_g": DEPTH, "norm_ffn_g": DEPTH, "ffn_w_gate": DEPTH, "ffn_w_up": DEPTH, "ffn_conv_w": DEPTH,
             "ffn_conv_b": DEPTH, "ffn_w_down": DEPTH}


def local_step(x, ctx, target, modtabs, p, big, s_ctx=CTX_LEN):
    d_model = x.shape[-1]
    s0 = jnp.concatenate([ctx, x], axis=1)
    lws = [_layer_weights(l, big) for l in range(DEPTH)]
    shapes = {n: v.shape for n, v in {**p, **big}.items()}
    grads = {n: [None] * PER_LAYER.get(n, DEPTH // 2) for n in shapes if n not in ("c_ctx", "w_mod", "b_mod", "final_norm_g", "hg_lb_logits")}
    (lbs,) = small_fwd("lower_bounds", f_lower_bounds, [p["hg_lb_logits"]], [p["hg_lb_logits"].shape])
    tab_a = [modtabs[0]] + [modtabs[l].at[:, N_MOD - 1].set(modtabs[l - 1][:, N_MOD - 1]) for l in range(1, DEPTH)]
    res = []
    s, br = s0, None
    for l in range(DEPTH):
        r = {}
        g_mix, g_ffn = p["norm_mix_g"][l][None], p["norm_ffn_g"][l][None]
        if l == 0:
            (hn,) = tok_fwd("l0_norm", f_norm(0, 1), [s], tab_a[0], [g_mix], [d_model], [BF16])
            r["a_in"] = [s]
        else:
            r["a_in"] = [s, br]
            s, hn = tok_fwd(f"l{l}_resnorm_a", f_resnorm(5, 0, 1), r["a_in"], tab_a[l], [g_mix], [d_model] * 2, [F32, BF16])
        if l % 2 == 0:
            r["mix"] = _even_mixer_fwd(l, hn, p, lws[l], s_ctx)
        else:
            r["mix"] = _odd_mixer_fwd(l, hn, p, lws[l], lbs[l:l + 1], s_ctx)
        ox = mm([(_rows2d(o), w) for o, (w, _) in zip(r["mix"]["o"], lws[l]["out"])], f"l{l}_mix_out").reshape(s.shape)
        r["b_in"] = [s, ox]
        s, fn = tok_fwd(f"l{l}_resnorm_b", f_resnorm(2, 3, 4), r["b_in"], modtabs[l], [g_ffn], [d_model] * 2, [F32, BF16])
        r["ffn"], br = _ffn_fwd(l, fn, p, lws[l], s_ctx)
        res.append(r)

    loss_blk, ds, dbr, dtab_f, dfinal_g = final_loss("final_loss", s, br, modtabs[DEPTH - 1], p["final_norm_g"][None], target, s_ctx)
    grads["final_norm_g"] = dfinal_g[0]
    dmod = [None] * DEPTH
    dtab_next = dtab_f
    dlb = jnp.zeros_like(lbs)
    for l in reversed(range(DEPTH)):
        r = res[l]
        g_mix, g_ffn = p["norm_mix_g"][l][None], p["norm_ffn_g"][l][None]
        dfn = _ffn_bwd(l, r["ffn"], dbr, lws[l], s_ctx, grads)
        (ds, dox), dtab_b, (grads["norm_ffn_g"][l],) = tok_bwd(
            f"l{l}_resnorm_b_bwd", f_resnorm(2, 3, 4), r["b_in"], modtabs[l], [g_ffn], [ds, dfn], [F32, BF16])
        if l % 2 == 0:
            dhn = _even_mixer_bwd(l, r["mix"], dox, p, lws[l], s_ctx, grads)
        else:
            dhn, dlb_l = _odd_mixer_bwd(l, r["mix"], dox, p, lws[l], s_ctx, grads)
            dlb = dlb.at[l:l + 1].set(dlb_l)
        if l == 0:
            (ds,), dtab_a, (dg,) = tok_bwd("l0_norm_bwd", f_norm_keep(0, 1), r["a_in"], tab_a[0], [g_mix], [ds, dhn], [F32])
        else:
            (ds, dbr), dtab_a, (dg,) = tok_bwd(
                f"l{l}_resnorm_a_bwd", f_resnorm(5, 0, 1), r["a_in"], tab_a[l], [g_mix], [ds, dhn], [F32, BF16])
        grads["norm_mix_g"][l] = dg
        dmod[l] = (dtab_a.at[:, N_MOD - 1].set(0.0) + dtab_b).at[:, N_MOD - 1].set(dtab_next[:, N_MOD - 1])
        dtab_next = dtab_a
    (grads["hg_lb_logits"],) = small_bwd("lower_bounds_bwd", f_lower_bounds, [p["hg_lb_logits"]], [dlb])
    out = {}
    for n, g in grads.items():
        if isinstance(g, list):
            g = jnp.stack([t.reshape(shapes[n][1:]) for t in g])
        out[n] = g.reshape(shapes[n])
    return loss_blk[:, 0, 0], ds[:, s_ctx:], dmod, out


WEIGHT_NAMES = (
    "c_ctx", "w_mod", "b_mod", "norm_mix_g", "norm_ffn_g", "final_norm_g", "ev_w_in", "ev_w_out", "ssd_conv_w",
    "ssd_conv_b", "ssd_dt_bias", "ssd_a_log", "ssd_d", "ssd_norm_g", "lru_conv_w", "lru_conv_b", "lru_w_a", "lru_b_a",
    "lru_w_i", "lru_b_i", "lru_lam", "od_w_in", "od_w_out", "hg_lb_logits", "hg_norm_g", "s5_lam_re", "s5_lam_im",
    "s5_log_step", "s5_b_re", "s5_b_im", "s5_c_re", "s5_c_im", "s5_d", "s5_glu_w", "s5_glu_b", "ffn_w_gate", "ffn_w_up",
    "ffn_conv_w", "ffn_conv_b", "ffn_w_down")
INPUT_NAMES = ("x", "c", "ctx") + WEIGHT_NAMES + ("loss_target",) + tuple("m_" + n for n in WEIGHT_NAMES) + tuple("v_" + n for n in WEIGHT_NAMES)
SHARD_AXIS = {"w_mod": 2, "ev_w_in": 2, "ev_w_out": 1, "ssd_conv_w": 2, "lru_conv_w": 2, "lru_b_a": 2, "lru_b_i": 2,
              "lru_lam": 2, "od_w_in": 2, "od_w_out": 1, "s5_d": 1, "s5_glu_w": 1, "s5_glu_b": 1, "ffn_w_gate": 2,
              "ffn_w_up": 2, "ffn_conv_w": 3, "ffn_w_down": 1}
SMALL_SHARDED = tuple(n for n in WEIGHT_NAMES if n in SHARD_AXIS and n not in BIG_WEIGHTS and n != "w_mod")
REPLICATED_LOCAL = tuple(n for n in WEIGHT_NAMES if n not in SHARD_AXIS and n not in ("c_ctx", "b_mod"))
PACK_WIDTH = 1024
MOD_ROWS = 48
CTX_ROW = 32


def _unshard(g8, axis):
    moved = jnp.moveaxis(g8, 0, axis)
    shp = moved.shape
    return moved.reshape(shp[:axis] + (shp[axis] * shp[axis + 1],) + shp[axis + 2:])


def _to_shards(full, axis):
    shp = full.shape
    return jnp.moveaxis(full.reshape(shp[:axis] + (N_DEV, shp[axis] // N_DEV) + shp[axis + 1:]), axis, 0)


def _pack(arrs, dtype, lead=()):
    flat = jnp.concatenate([a.astype(dtype).reshape(lead + (-1,)) for a in arrs], axis=-1)
    n = flat.shape[-1]
    unit = 16 * PACK_WIDTH
    padded = -(-n // unit) * unit
    flat = jnp.pad(flat, [(0, 0)] * len(lead) + [(0, padded - n)])
    return flat.reshape(lead + (padded // PACK_WIDTH, PACK_WIDTH))


def _unpack(packed, shapes, lead=()):
    flat = packed.reshape(lead + (-1,))
    out, off = [], 0
    for shp in shapes:
        n = math.prod(shp)
        out.append(flat[..., off:off + n].reshape(lead + tuple(shp)))
        off += n
    return out


def _my_block(full, axis, me):
    loc = full.shape[axis] // N_DEV
    return lax.dynamic_slice_in_dim(full, me * loc, loc, axis)


def kernel(*args):
    a = dict(zip(INPUT_NAMES, args))
    px, py, pc = _my_pos()
    me = 4 * px + 2 * py + pc
    nb = a["x"].shape[0]

    big8 = all_gather([a[n].astype(BF16) for n in BIG_WEIGHTS], "gather_big_weights")
    big = {n: _unshard(g, SHARD_AXIS[n]) for n, g in zip(BIG_WEIGHTS, big8)}
    small_names = ("c",) + SMALL_SHARDED
    (small8,) = all_gather([_pack([a[n] for n in small_names], F32)], "gather_small")
    small = dict(zip(small_names, _unpack(small8, [a[n].shape for n in small_names], (N_DEV,))))
    p = {n: a[n] for n in WEIGHT_NAMES if n not in SHARD_AXIS}
    for n in SMALL_SHARDED:
        p[n] = _unshard(small[n], SHARD_AXIS[n])
    c_all = small["c"].reshape(N_DEV * nb, D_MODEL)

    rows = jnp.concatenate([c_all, a["c_ctx"][None], jnp.zeros((MOD_ROWS - CTX_ROW - 1, D_MODEL), F32)], axis=0)
    (srows,) = small_fwd("mod_silu", f_silu, [rows], [rows.shape])
    wmod2d = jnp.transpose(a["w_mod"], (1, 0, 2)).reshape(D_MODEL, -1).astype(BF16)
    cols = a["w_mod"].shape[2]
    mod_loc = mm([(srows, wmod2d)], "mod_proj")
    mod8 = all_gather([mod_loc], "gather_mod")[0].reshape(N_DEV, MOD_ROWS, DEPTH, cols)
    mod_all = jnp.transpose(mod8, (2, 1, 0, 3)).reshape(DEPTH, MOD_ROWS, N_DEV * cols) + a["b_mod"][:, None, :]
    modtabs = []
    for l in range(DEPTH):
        mine = lax.dynamic_slice_in_dim(mod_all[l], me * nb, nb, 0).reshape(nb, N_MOD, D_MODEL)
        ctx_row = jnp.broadcast_to(mod_all[l, CTX_ROW].reshape(1, N_MOD, D_MODEL), (nb, N_MOD, D_MODEL))
        modtabs.append(jnp.stack([ctx_row, mine], axis=1).reshape(2 * nb, N_MOD, D_MODEL))

    loss_b, grad_x, dmod, grads = local_step(a["x"], a["ctx"], a["loss_target"], modtabs, p, big)
    loss = lax.psum(jnp.sum(loss_b), ("x", "y", "c"))

    dm = jnp.stack([t.reshape(nb, 2, N_MOD * D_MODEL) for t in dmod])
    dloc = jnp.concatenate([dm[:, :, 1], jnp.sum(dm[:, :, 0], axis=1, keepdims=True),
                            jnp.zeros((DEPTH, SUBLANES - nb - 1, N_MOD * D_MODEL), F32)], axis=1)
    d8 = all_gather([dloc.reshape(DEPTH * SUBLANES, -1)], "gather_dmod")[0].reshape(N_DEV, DEPTH, SUBLANES, -1)
    d_rows = jnp.transpose(d8[:, :, :nb], (1, 0, 2, 3)).reshape(DEPTH, N_DEV * nb, -1)
    d_ctx = jnp.sum(d8[:, :, nb], axis=0)[:, None]
    d_full = jnp.concatenate([d_rows, d_ctx, jnp.zeros((DEPTH, MOD_ROWS - CTX_ROW - 1, N_MOD * D_MODEL), F32)], axis=1)
    grads["b_mod"] = jnp.sum(d_full, axis=1)
    d_cols = jnp.transpose(_my_block(d_full, 2, me), (1, 0, 2)).reshape(MOD_ROWS, DEPTH * cols)
    g_wmod = mm([(srows.T, d_cols)], "mod_dw")
    g_wmod_local = jnp.transpose(g_wmod.reshape(D_MODEL, DEPTH, cols), (1, 0, 2))
    d_srows_part = mm([(d_cols[CTX_ROW:CTX_ROW + SUBLANES], wmod2d.T)], "mod_dctx")[0]

    reduce_names = REPLICATED_LOCAL + SMALL_SHARDED
    (part8,) = all_gather([_pack([d_srows_part] + [grads[n] for n in reduce_names], F32)], "gather_small_grads")
    total = sum_slots(part8, "sum_small_grads")
    totals = _unpack(total, [(D_MODEL,)] + [grads[n].shape for n in reduce_names])
    d_srows = jnp.zeros_like(rows).at[CTX_ROW].set(totals[0])
    (d_rows_in,) = small_bwd("mod_silu_bwd", f_silu, [rows], [d_srows])
    g_local = {"c_ctx": d_rows_in[CTX_ROW], "b_mod": grads["b_mod"], "w_mod": g_wmod_local}
    for n, t in zip(reduce_names, totals[1:]):
        g_local[n] = _my_block(t, SHARD_AXIS[n], me) if n in SHARD_AXIS else t

    got = all_to_all([_to_shards(grads[n], SHARD_AXIS[n]).astype(BF16) for n in BIG_WEIGHTS], "exchange_big_grads")
    for n, t in zip(BIG_WEIGHTS, got):
        g_local[n] = sum_slots(t.reshape(N_DEV, -1, t.shape[-1]), "sum_" + n).reshape(a[n].shape)

    deltas, new_m, new_v = [], [], []
    for n in WEIGHT_NAMES:
        d, m, v = adamw("adamw_" + n, a[n], g_local[n], a["m_" + n], a["v_" + n])
        deltas.append(d)
        new_m.append(m)
        new_v.append(v)
    return (loss, grad_x, *[g_local[n] for n in WEIGHT_NAMES], *deltas, *new_m, *new_v)
```

```python
import functools
import math

import jax
import jax.numpy as jnp
from jax import lax
from jax.experimental import pallas as pl
from jax.experimental.pallas import tpu as pltpu

F32 = jnp.float32
BF16 = jnp.bfloat16

D_MODEL = 1024
DEPTH = 4
CTX_LEN = 256
SEQ = 2048
S_TOT = CTX_LEN + SEQ
GRID_W = 64
N_MOD = 6
RMS_EPS = 1e-6
N_DEV = 8

SSD_HEADS = 16
SSD_HEAD_DIM = 64
SSD_GROUPS = 2
SSD_HPG = 8
SSD_STATE = 128
SSD_CHUNK = 128
SSD_W = SSD_HEADS * SSD_HEAD_DIM
LRU_BLOCKS = 8
LRU_BLOCK_W = 128
LRU_C = 8.0
HG_W = 768
HG_HEADS = 6
HG_DK = 128
HG_CHUNK = 32
S5_W = 256
S5_GROUPS = 16
S5_GROUP_CH = 16
S5_STATE = 64
D_FF = 2816

ADAM_LR = 0.001
ADAM_B1 = 0.9
ADAM_B2 = 0.999
ADAM_EPS = 1e-08
ADAM_WD = 0.01
ADAM_STEP = 10

TOK_BLOCK = CTX_LEN
SUBLANES = 8
VMEM_LIMIT_BYTES = 56 * 1024 * 1024
MM_BLOCK_BYTES = 8 * 1024 * 1024
MM_TILES = (1408, 1024, 768, 704, 512, 384, 352, 256, 128, 64, 48, 40, 32, 16, 8)
MM_ROW_TILES = (2304, 2048, 1152, 1024, 512, 256, 128, 64, 48, 32, 16, 8)
LANE_TILE = 128


def _cparams(sem=None):
    kw = dict(vmem_limit_bytes=VMEM_LIMIT_BYTES)
    if sem is not None:
        kw["dimension_semantics"] = sem
    return pltpu.CompilerParams(**kw)


def _pick(n, cands):
    for c in cands:
        if n % c == 0:
            return c
    return n


def mm(pairs, name, out_dtype=F32):
    m = pairs[0][0].shape[0]
    n = pairs[0][1].shape[1]
    kdims = [a.shape[1] for a, _ in pairs]
    ktile = None
    if len(pairs) == 1 and kdims[0] > 4096:
        ktile = _pick(kdims[0], (2304, 2048, 1024))
    nk = kdims[0] // ktile if ktile else 1
    col_bytes = sum((ktile or w.shape[0]) * w.dtype.itemsize for _, w in pairs)
    tn = _pick(n, tuple(c for c in MM_TILES if c % LANE_TILE == 0 and c * col_bytes <= MM_BLOCK_BYTES))
    row_bytes = max(sum((ktile or a.shape[1]) * a.dtype.itemsize for a, _ in pairs), tn * 4)
    tm = _pick(m, tuple(c for c in MM_TILES if c * row_bytes <= MM_BLOCK_BYTES))
    npairs = len(pairs)
    if nk > 1:
        assert out_dtype == F32

    def body(*refs):
        o_ref = refs[2 * npairs]
        acc = None
        for i in range(npairs):
            a = refs[2 * i][...].astype(BF16)
            w = refs[2 * i + 1][...].astype(BF16)
            p = jnp.dot(a, w, preferred_element_type=F32)
            acc = p if acc is None else acc + p
        if nk == 1:
            o_ref[...] = acc.astype(out_dtype)
        else:
            k = pl.program_id(2)

            @pl.when(k == 0)
            def _():
                o_ref[...] = acc

            @pl.when(k > 0)
            def _():
                o_ref[...] += acc

    in_specs = []
    args = []
    for a, w in pairs:
        kk = a.shape[1]
        assert w.shape == (kk, n) and a.shape[0] == m, (a.shape, w.shape)
        tk = ktile if ktile else kk
        in_specs.append(pl.BlockSpec((tm, tk), lambda i, j, k: (i, k)))
        in_specs.append(pl.BlockSpec((tk, tn), lambda i, j, k: (k, j)))
        args += [a, w]
    return pl.pallas_call(
        body,
        name=name,
        grid=(m // tm, n // tn, nk),
        in_specs=in_specs,
        out_specs=pl.BlockSpec((tm, tn), lambda i, j, k: (i, j)),
        out_shape=jax.ShapeDtypeStruct((m, n), out_dtype),
        compiler_params=_cparams(("parallel", "parallel", "arbitrary")),
    )(*args)


def mm_tn(a, d, name):
    r, k = a.shape
    n = d.shape[1]
    tr = _pick(r, MM_ROW_TILES)
    lane_ok = lambda c, full: c % LANE_TILE == 0 or c == full
    tk = _pick(k, tuple(c for c in MM_TILES if lane_ok(c, k) and c * tr * a.dtype.itemsize <= MM_BLOCK_BYTES))
    tn = _pick(n, tuple(c for c in MM_TILES if lane_ok(c, n) and c * tr * d.dtype.itemsize <= MM_BLOCK_BYTES
                        and c * tk * 4 <= MM_BLOCK_BYTES))

    def body(a_ref, d_ref, o_ref):
        acc = lax.dot_general(a_ref[...].astype(BF16), d_ref[...].astype(BF16), (((0,), (0,)), ((), ())),
                              preferred_element_type=F32)
        step = pl.program_id(2)

        @pl.when(step == 0)
        def _():
            o_ref[...] = acc

        @pl.when(step > 0)
        def _():
            o_ref[...] += acc

    return pl.pallas_call(
        body,
        name=name,
        grid=(k // tk, n // tn, r // tr),
        in_specs=[pl.BlockSpec((tr, tk), lambda i, j, s: (s, i)), pl.BlockSpec((tr, tn), lambda i, j, s: (s, j))],
        out_specs=pl.BlockSpec((tk, tn), lambda i, j, s: (i, j)),
        out_shape=jax.ShapeDtypeStruct((k, n), F32),
        compiler_params=_cparams(("parallel", "parallel", "arbitrary")),
    )(a, d)


def _mod_index(b, t):
    return (2 * b + jnp.minimum(t, 1), 0, 0)


def tok_fwd(name, f, toks, mod, params, out_widths, out_dtypes):
    nb, s, _ = toks[0].shape
    nt, nm, npar = len(toks), int(mod is not None), len(params)

    def body(*refs):
        ins, outs = refs[: nt + nm + npar], refs[nt + nm + npar:]
        tv = [r[...].astype(F32) for r in ins[:nt]]
        mv = [ins[nt][k:k + 1, :] for k in range(N_MOD)] if nm else None
        pv = [r[...] for r in ins[nt + nm:]]
        for o, r in zip(outs, f(tv, mv, pv)):
            o[...] = r.astype(o.dtype)

    in_specs = [pl.BlockSpec((None, TOK_BLOCK, t.shape[2]), lambda b, t: (b, t, 0)) for t in toks]
    if nm:
        in_specs.append(pl.BlockSpec((None, N_MOD, mod.shape[2]), _mod_index))
    in_specs += [pl.BlockSpec(p.shape, lambda b, t, nd=p.ndim: (0,) * nd) for p in params]
    return pl.pallas_call(
        body,
        name=name,
        grid=(nb, s // TOK_BLOCK),
        in_specs=in_specs,
        out_specs=[pl.BlockSpec((None, TOK_BLOCK, w), lambda b, t: (b, t, 0)) for w in out_widths],
        out_shape=[jax.ShapeDtypeStruct((nb, s, w), dt) for w, dt in zip(out_widths, out_dtypes)],
        compiler_params=_cparams(("parallel", "parallel")),
    )(*toks, *([mod] if nm else []), *params)


def tok_bwd(name, f, toks, mod, params, cots, dtok_dtypes):
    nb, s, _ = toks[0].shape
    nt, nm, npar, nc = len(toks), int(mod is not None), len(params), len(cots)

    def body(*refs):
        n_in = nt + nm + npar + nc
        ins, outs = refs[:n_in], refs[n_in:]
        b, t = pl.program_id(0), pl.program_id(1)
        tv = [r[...].astype(F32) for r in ins[:nt]]
        mv = [ins[nt][k:k + 1, :] for k in range(N_MOD)] if nm else None
        pv = [r[...] for r in ins[nt + nm: nt + nm + npar]]
        cv = [r[...].astype(F32) for r in ins[nt + nm + npar:]]
        _, vjp = jax.vjp(f, tv, mv, pv)
        dtv, dmv, dpv = vjp(cv)
        for o, r in zip(outs[:nt], dtv):
            o[...] = r.astype(o.dtype)
        if nm:
            dm_ref = outs[nt]

            @pl.when(t <= 1)
            def _():
                for k in range(N_MOD):
                    dm_ref[k:k + 1, :] = dmv[k]

            @pl.when(t > 1)
            def _():
                for k in range(N_MOD):
                    dm_ref[k:k + 1, :] += dmv[k]

        first = jnp.logical_and(b == 0, t == 0)
        for o, r in zip(outs[nt + nm:], dpv):
            @pl.when(first)
            def _(o=o, r=r):
                o[...] = r

            @pl.when(jnp.logical_not(first))
            def _(o=o, r=r):
                o[...] += r

    tok_spec = lambda w: pl.BlockSpec((None, TOK_BLOCK, w), lambda b, t: (b, t, 0))
    in_specs = [tok_spec(t.shape[2]) for t in toks]
    if nm:
        in_specs.append(pl.BlockSpec((None, N_MOD, mod.shape[2]), _mod_index))
    in_specs += [pl.BlockSpec(p.shape, lambda b, t, nd=p.ndim: (0,) * nd) for p in params]
    in_specs += [tok_spec(c.shape[2]) for c in cots]
    out_specs = [tok_spec(t.shape[2]) for t in toks]
    out_shape = [jax.ShapeDtypeStruct(t.shape, dt) for t, dt in zip(toks, dtok_dtypes)]
    if nm:
        out_specs.append(pl.BlockSpec((None, N_MOD, mod.shape[2]), _mod_index))
        out_shape.append(jax.ShapeDtypeStruct(mod.shape, F32))
    out_specs += [pl.BlockSpec(p.shape, lambda b, t, nd=p.ndim: (0,) * nd) for p in params]
    out_shape += [jax.ShapeDtypeStruct(p.shape, F32) for p in params]
    res = pl.pallas_call(
        body,
        name=name,
        grid=(nb, s // TOK_BLOCK),
        in_specs=in_specs,
        out_specs=out_specs,
        out_shape=out_shape,
        compiler_params=_cparams(("arbitrary", "arbitrary")),
    )(*toks, *([mod] if nm else []), *params, *cots)
    return res[:nt], (res[nt] if nm else None), res[nt + nm:]


def _rms(x, g):
    return x * lax.rsqrt(jnp.mean(x * x, axis=-1, keepdims=True) + RMS_EPS) * g


def _silu(x):
    return x * jax.nn.sigmoid(x)


def f_norm(shift_row, scale_row):
    def f(tv, mv, pv):
        return [_rms(tv[0], pv[0]) * (1.0 + mv[scale_row]) + mv[shift_row]]
    return f


def f_resnorm(gate_row, shift_row, scale_row):
    def f(tv, mv, pv):
        s = tv[0] + mv[gate_row] * tv[1]
        return [s, _rms(s, pv[0]) * (1.0 + mv[scale_row]) + mv[shift_row]]
    return f


_ANY = pl.BlockSpec(memory_space=pl.ANY)
_MESH = pl.DeviceIdType.MESH


def _my_pos():
    return lax.axis_index("x"), lax.axis_index("y"), lax.axis_index("c")


def _slot_of(pos):
    return 4 * pos[0] + 2 * pos[1] + pos[2]


def all_gather(xs, name):
    n = len(xs)

    def body(*refs):
        x_refs, out_refs = refs[:n], refs[n:2 * n]
        send_sems, recv_sems, local_sems = refs[2 * n:]
        px, py, pc = _my_pos()
        me, sibling = (px, py, pc), (px, py, 1 - pc)
        chips = [(1 - px, py), (px, 1 - py), (1 - px, 1 - py)]

        def copy(a, k, block, to, from_input=False):
            slot = out_refs[a].at[_slot_of(block)]
            return pltpu.make_async_remote_copy(
                src_ref=x_refs[a] if from_input else slot, dst_ref=slot,
                send_sem=send_sems.at[a, k], recv_sem=recv_sems.at[a, k],
                device_id=to, device_id_type=_MESH)

        mine = [pltpu.make_async_copy(x_refs[a], out_refs[a].at[_slot_of(me)], local_sems.at[a]) for a in range(n)]
        for cp in mine:
            cp.start()
        first = [copy(a, 0, me, sibling, True) for a in range(n)]
        first += [copy(a, 1 + j, me, (*chip, pc), True) for j, chip in enumerate(chips) for a in range(n)]
        for cp in first:
            cp.start()
        passed = []
        for j, chip in enumerate(chips):
            for a in range(n):
                copy(a, 1 + j, (*chip, pc), me).wait_recv()
                passed.append(copy(a, 4 + j, (*chip, pc), sibling))
                passed[-1].start()
        for a in range(n):
            copy(a, 0, sibling, me).wait_recv()
        for j, chip in enumerate(chips):
            for a in range(n):
                copy(a, 4 + j, (*chip, 1 - pc), me).wait_recv()
        for cp in first + passed:
            cp.wait_send()
        for cp in mine:
            cp.wait()

    return pl.pallas_call(
        body,
        name=name,
        out_shape=[jax.ShapeDtypeStruct((N_DEV,) + x.shape, x.dtype) for x in xs],
        in_specs=[_ANY] * n,
        out_specs=[_ANY] * n,
        scratch_shapes=[pltpu.SemaphoreType.DMA((n, 7)), pltpu.SemaphoreType.DMA((n, 7)), pltpu.SemaphoreType.DMA((n,))],
    )(*xs)


def all_to_all(xs, name):
    n = len(xs)

    def body(*refs):
        x_refs, out_refs = refs[:n], refs[n:2 * n]
        send_sems, recv_sems, local_sems = refs[2 * n:]
        px, py, pc = _my_pos()
        me = (px, py, pc)

        def flipped(k):
            kx, ky, kc = (k >> 2) & 1, (k >> 1) & 1, k & 1
            return (1 - px if kx else px, 1 - py if ky else py, 1 - pc if kc else pc)

        def copy(a, k):
            peer = flipped(k)
            return pltpu.make_async_remote_copy(
                src_ref=x_refs[a].at[_slot_of(peer)], dst_ref=out_refs[a].at[_slot_of(me)],
                send_sem=send_sems.at[a, k - 1], recv_sem=recv_sems.at[a, k - 1],
                device_id=peer, device_id_type=_MESH)

        def landing(a, k):
            peer = flipped(k)
            return pltpu.make_async_remote_copy(
                src_ref=x_refs[a].at[_slot_of(me)], dst_ref=out_refs[a].at[_slot_of(peer)],
                send_sem=send_sems.at[a, k - 1], recv_sem=recv_sems.at[a, k - 1],
                device_id=peer, device_id_type=_MESH)

        mine = [pltpu.make_async_copy(x_refs[a].at[_slot_of(me)], out_refs[a].at[_slot_of(me)], local_sems.at[a]) for a in range(n)]
        for cp in mine:
            cp.start()
        copies = [copy(a, k) for a in range(n) for k in range(1, N_DEV)]
        for cp in copies:
            cp.start()
        for a in range(n):
            for k in range(1, N_DEV):
                landing(a, k).wait_recv()
        for cp in copies:
            cp.wait_send()
        for cp in mine:
            cp.wait()

    return pl.pallas_call(
        body,
        name=name,
        out_shape=[jax.ShapeDtypeStruct(x.shape, x.dtype) for x in xs],
        in_specs=[_ANY] * n,
        out_specs=[_ANY] * n,
        scratch_shapes=[pltpu.SemaphoreType.DMA((n, 7)), pltpu.SemaphoreType.DMA((n, 7)), pltpu.SemaphoreType.DMA((n,))],
    )(*xs)


N_CHIPS = 4


def sibling_swap(xs, name):
    n = len(xs)

    def body(*refs):
        x_refs, out_refs = refs[:n], refs[n:2 * n]
        send_sems, recv_sems = refs[2 * n:]
        px, py, pc = _my_pos()
        copies = [pltpu.make_async_remote_copy(
            src_ref=x_refs[a].at[:, 1 - pc], dst_ref=out_refs[a],
            send_sem=send_sems.at[a], recv_sem=recv_sems.at[a],
            device_id=(px, py, 1 - pc), device_id_type=_MESH) for a in range(n)]
        for cp in copies:
            cp.start()
        for cp in copies:
            cp.wait()

    return pl.pallas_call(
        body,
        name=name,
        out_shape=[jax.ShapeDtypeStruct(x.shape[:1] + x.shape[2:], x.dtype) for x in xs],
        in_specs=[_ANY] * n,
        out_specs=[_ANY] * n,
        scratch_shapes=[pltpu.SemaphoreType.DMA((n,)), pltpu.SemaphoreType.DMA((n,))],
    )(*xs)


def chip_exchange(xs, name):
    n = len(xs)

    def body(*refs):
        x_refs, out_refs = refs[:n], refs[n:2 * n]
        send_sems, recv_sems, local_sems = refs[2 * n:]
        px, py, pc = _my_pos()
        my_chip = 2 * px + py

        def peer(k):
            return (1 - px if k & 2 else px, 1 - py if k & 1 else py)

        def copy(a, k, landing):
            qx, qy = peer(k)
            src, dst = (my_chip, 2 * qx + qy) if landing else (2 * qx + qy, my_chip)
            return pltpu.make_async_remote_copy(
                src_ref=x_refs[a].at[src], dst_ref=out_refs[a].at[dst],
                send_sem=send_sems.at[a, k - 1], recv_sem=recv_sems.at[a, k - 1],
                device_id=(qx, qy, pc), device_id_type=_MESH)

        mine = [pltpu.make_async_copy(x_refs[a].at[my_chip], out_refs[a].at[my_chip], local_sems.at[a]) for a in range(n)]
        for cp in mine:
            cp.start()
        copies = [copy(a, k, False) for a in range(n) for k in range(1, N_CHIPS)]
        for cp in copies:
            cp.start()
        for a in range(n):
            for k in range(1, N_CHIPS):
                copy(a, k, True).wait_recv()
        for cp in copies:
            cp.wait_send()
        for cp in mine:
            cp.wait()

    return pl.pallas_call(
        body,
        name=name,
        out_shape=[jax.ShapeDtypeStruct(x.shape, x.dtype) for x in xs],
        in_specs=[_ANY] * n,
        out_specs=[_ANY] * n,
        scratch_shapes=[pltpu.SemaphoreType.DMA((n, N_CHIPS - 1)), pltpu.SemaphoreType.DMA((n, N_CHIPS - 1)),
                        pltpu.SemaphoreType.DMA((n,))],
    )(*xs)


def pair_sum(own, got, name):
    nch, _, r, c = own.shape
    tr = _pick(r, (512, 256, 128, 64, 32, 16))

    def body(own_ref, got_ref, o_ref):
        pc = lax.axis_index("c")
        o_ref[...] = (own_ref[pc].astype(F32) + got_ref[...].astype(F32)).astype(o_ref.dtype)

    return pl.pallas_call(
        body,
        name=name,
        grid=(nch, r // tr),
        in_specs=[pl.BlockSpec((None, 2, tr, c), lambda i, j: (i, 0, j, 0)), pl.BlockSpec((None, tr, c), lambda i, j: (i, j, 0))],
        out_specs=pl.BlockSpec((None, tr, c), lambda i, j: (i, j, 0)),
        out_shape=jax.ShapeDtypeStruct((nch, r, c), own.dtype),
        compiler_params=_cparams(("parallel", "parallel")),
    )(own, got)


def sum_slots(x, name):
    n, r, c = x.shape
    tr = _pick(r, (512, 256, 128, 64, 32, 16, 8))

    def body(x_ref, o_ref):
        acc = x_ref[0].astype(F32)
        for i in range(1, n):
            acc = acc + x_ref[i].astype(F32)
        o_ref[...] = acc

    return pl.pallas_call(
        body,
        name=name,
        grid=(r // tr,),
        in_specs=[pl.BlockSpec((n, tr, c), lambda i: (0, i, 0))],
        out_specs=pl.BlockSpec((tr, c), lambda i: (i, 0)),
        out_shape=jax.ShapeDtypeStruct((r, c), F32),
        compiler_params=_cparams(("parallel",)),
    )(x)


CONV_CH_TILE = 256


def _shift_rows(x, off):
    n = x.shape[0]
    if off % n == 0:
        return x
    return pltpu.roll(x, (-off) % n, axis=0)


def _between(v, lo, hi):
    return jnp.where(v >= lo, 1.0, 0.0) * jnp.where(v < hi, 1.0, 0.0)


def taps_1d(ntaps, s_ctx, s_tot):
    def mask(off):
        def m(t):
            is_ctx = _between(t, 0, s_ctx)
            return is_ctx * _between(t + off, 0, s_ctx) + (1.0 - is_ctx) * _between(t + off, s_ctx, s_tot)
        return m
    return [(j - (ntaps - 1) // 2, mask(j - (ntaps - 1) // 2)) for j in range(ntaps)]


def taps_grid(s_ctx, s_tot, grid_w):
    assert s_ctx % grid_w == 0

    return ("grid", s_ctx, s_tot, grid_w)


def _grid_masks(taps, s):
    _, s_ctx, s_tot, grid_w = taps
    t = lax.broadcasted_iota(jnp.int32, (s, 1), 0)
    is_ctx = _between(t, 0, s_ctx)
    mcol = {dc: is_ctx * _between(t + dc, 0, s_ctx) + (1.0 - is_ctx) * _between(t % grid_w + dc, 0, grid_w) for dc in (-1, 1)}
    mrow = {dr: (1.0 - is_ctx) * _between(t + grid_w * dr, s_ctx, s_tot) for dr in (-1, 1)}
    return mcol, mrow


def _grid_cols(x, mcol):
    return {-1: _shift_rows(x, -1) * mcol[-1], 0: x, 1: _shift_rows(x, 1) * mcol[1]}


def _conv_acc(x, w_ref, b_ref, taps, s):
    acc = jnp.broadcast_to(b_ref[...], x.shape)
    if taps[0] == "grid":
        grid_w = taps[3]
        mcol, mrow = _grid_masks(taps, s)
        xc = _grid_cols(x, mcol)
        for a, dr in enumerate((-1, 0, 1)):
            r = sum(w_ref[3 * a + b:3 * a + b + 1, :] * xc[dc] for b, dc in enumerate((-1, 0, 1)))
            acc = acc + (r if dr == 0 else _shift_rows(r, grid_w * dr) * mrow[dr])
        return acc
    t = lax.broadcasted_iota(jnp.int32, (s, 1), 0)
    for k, (off, m) in enumerate(taps):
        acc = acc + w_ref[k:k + 1, :] * (_shift_rows(x, off) * m(t))
    return acc


def _conv_adjoint(x, dacc, w_ref, taps, s):
    if taps[0] == "grid":
        grid_w = taps[3]
        mcol, mrow = _grid_masks(taps, s)
        xc = _grid_cols(x, mcol)
        dxc = {dc: 0.0 for dc in (-1, 0, 1)}
        dws = []
        for a, dr in enumerate((-1, 0, 1)):
            d_r = dacc if dr == 0 else _shift_rows(dacc * mrow[dr], -grid_w * dr)
            for b, dc in enumerate((-1, 0, 1)):
                dxc[dc] = dxc[dc] + w_ref[3 * a + b:3 * a + b + 1, :] * d_r
                dws.append(jnp.sum(d_r * xc[dc], axis=0, keepdims=True))
        dx = dxc[0] + _shift_rows(dxc[-1] * mcol[-1], 1) + _shift_rows(dxc[1] * mcol[1], -1)
        return dx, dws
    t = lax.broadcasted_iota(jnp.int32, (s, 1), 0)
    dx = jnp.zeros_like(x)
    dws = []
    for k, (off, m) in enumerate(taps):
        dm = dacc * m(t)
        dx = dx + _shift_rows(w_ref[k:k + 1, :] * dm, -off)
        dws.append(jnp.sum(dm * _shift_rows(x, off), axis=0, keepdims=True))
    return dx, dws


def conv_fwd(name, x, w, b, taps, mode, mul=None, out_dtype=F32):
    nb, s, c = x.shape
    ct = _pick(c, (CONV_CH_TILE, 128))
    has_mul = mode == "silu_mul"

    def body(*refs):
        x_ref, w_ref, b_ref = refs[:3]
        o_ref = refs[-1]
        acc = _conv_acc(x_ref[...], w_ref, b_ref, taps, s)
        if mode == "none":
            out = acc
        else:
            out = _silu(acc)
            if has_mul:
                out = out * refs[3][...].astype(F32)
        o_ref[...] = out.astype(o_ref.dtype)

    blk = pl.BlockSpec((None, s, ct), lambda bb, j: (bb, 0, j))
    par = lambda k: pl.BlockSpec((k, ct), lambda bb, j: (0, j))
    return pl.pallas_call(
        body,
        name=name,
        grid=(nb, c // ct),
        in_specs=[blk, par(w.shape[0]), par(1)] + ([blk] if has_mul else []),
        out_specs=blk,
        out_shape=jax.ShapeDtypeStruct(x.shape, out_dtype),
        compiler_params=_cparams(("parallel", "parallel")),
    )(x, w, b, *([mul] if has_mul else []))


def conv_bwd(name, x, w, b, dout, taps, mode, mul=None, dx_dtype=F32):
    nb, s, c = x.shape
    ct = _pick(c, (CONV_CH_TILE, 128))
    has_mul = mode == "silu_mul"
    nk = w.shape[0]

    def body(*refs):
        x_ref, w_ref, b_ref, do_ref = refs[:4]
        n_in = 5 if has_mul else 4
        dx_ref, dw_ref, db_ref = refs[n_in:n_in + 3]
        bb = pl.program_id(1)
        x = x_ref[...]
        dacc = do_ref[...].astype(F32)
        if mode != "none":
            acc = _conv_acc(x, w_ref, b_ref, taps, s)
            sg = jax.nn.sigmoid(acc)
            if has_mul:
                refs[n_in + 3][...] = (dacc * (acc * sg)).astype(refs[n_in + 3].dtype)
                dacc = dacc * refs[4][...].astype(F32)
            dacc = dacc * (sg * (1.0 + acc * (1.0 - sg)))
        dx, dws = _conv_adjoint(x, dacc, w_ref, taps, s)
        dx_ref[...] = dx.astype(dx_ref.dtype)
        db = jnp.sum(dacc, axis=0, keepdims=True)

        @pl.when(bb == 0)
        def _():
            for k in range(nk):
                dw_ref[k:k + 1, :] = dws[k]
            db_ref[...] = db

        @pl.when(bb > 0)
        def _():
            for k in range(nk):
                dw_ref[k:k + 1, :] += dws[k]
            db_ref[...] += db

    blk = pl.BlockSpec((None, s, ct), lambda j, bb: (bb, 0, j))
    par = lambda k: pl.BlockSpec((k, ct), lambda j, bb: (0, j))
    out_specs = [blk, par(nk), par(1)] + ([blk] if has_mul else [])
    out_shape = [jax.ShapeDtypeStruct(x.shape, dx_dtype), jax.ShapeDtypeStruct(w.shape, F32), jax.ShapeDtypeStruct(b.shape, F32)]
    if has_mul:
        out_shape.append(jax.ShapeDtypeStruct(x.shape, dx_dtype))
    return pl.pallas_call(
        body,
        name=name,
        grid=(c // ct, nb),
        in_specs=[blk, par(nk), par(1), blk] + ([blk] if has_mul else []),
        out_specs=out_specs,
        out_shape=out_shape,
        compiler_params=_cparams(("parallel", "arbitrary")),
    )(x, w, b, dout, *([mul] if has_mul else []))


SCAN_UNROLL = 4


def _scan_order(direction, adjoint, s_ctx, s_tot):
    nc, nt = s_ctx // SUBLANES, s_tot // SUBLANES
    if direction == 0:
        return ([(0, nt, 1)], False) if not adjoint else ([(nt - 1, nt, -1)], True)
    if not adjoint:
        return [(nc - 1, nc, -1), (nt - 1, nt - nc, -1)], True
    return [(nc, nt - nc, 1), (0, nc, 1)], False


def _last_row(h, descending):
    row = lax.broadcasted_iota(jnp.int32, h.shape, 0)
    pick = 0 if descending else SUBLANES - 1
    return jnp.sum(jnp.where(row == pick, h, 0.0), axis=0, keepdims=True)


def _prev_rows(h, carry, descending):
    row = lax.broadcasted_iota(jnp.int32, h.shape, 0)
    if descending:
        return jnp.where(row == SUBLANES - 1, carry, pltpu.roll(h, SUBLANES - 1, axis=0))
    return jnp.where(row == 0, carry, pltpu.roll(h, 1, axis=0))


def _scan_real(a_ref, x_ref, h_ref, hp_ref, order):
    ranges, descending = order
    n_rows, width = a_ref.shape
    n_tiles = n_rows // SUBLANES
    row = lax.broadcasted_iota(jnp.int32, (SUBLANES, width), 0)
    unroll = lambda count: SCAN_UNROLL if count % SCAN_UNROLL == 0 else 1

    def run(ac_ref):
        def in_tile(i, _):
            t0 = pl.multiple_of(i * SUBLANES, SUBLANES)
            a = a_ref[pl.ds(t0, SUBLANES), :]
            x = x_ref[pl.ds(t0, SUBLANES), :]
            for k in (1, 2, 4):
                sh = SUBLANES - k if descending else k
                keep = (row < SUBLANES - k) if descending else (row >= k)
                x = jnp.where(keep, a * pltpu.roll(x, sh, axis=0) + x, x)
                a = jnp.where(keep, a * pltpu.roll(a, sh, axis=0), a)
            ac_ref[pl.ds(t0, SUBLANES), :] = a
            x_ref[pl.ds(t0, SUBLANES), :] = x
            return 0

        lax.fori_loop(0, n_tiles, in_tile, 0, unroll=unroll(n_tiles))

        def tile(i, carry):
            t0 = pl.multiple_of(i * SUBLANES, SUBLANES)
            a = ac_ref[pl.ds(t0, SUBLANES), :]
            x = x_ref[pl.ds(t0, SUBLANES), :]
            h = a * carry + x
            if h_ref is not None:
                h_ref[pl.ds(t0, SUBLANES), :] = h
            if hp_ref is not None:
                hp_ref[pl.ds(t0, SUBLANES), :] = _prev_rows(h, carry, descending)
            return _last_row(a, descending) * carry + _last_row(x, descending)

        carry = jnp.zeros((1, width), F32)
        for first, count, step in ranges:
            carry = lax.fori_loop(0, count, lambda j, c, first=first, step=step: tile(first + step * j, c), carry,
                                  unroll=unroll(count))

    pl.run_scoped(run, pltpu.VMEM((n_rows, width), F32))


def _cmul(ar, ai, br, bi):
    return ar * br - ai * bi, ar * bi + ai * br


def _scan_cplx(lr, li, xr_ref, xi_ref, hpr_ref, hpi_ref, order):
    ranges, descending = order
    width = xr_ref.shape[1]
    row = lax.broadcasted_iota(jnp.int32, (SUBLANES, width), 0)
    pw = [(lr, li)]
    for _ in range(SUBLANES - 1):
        pw.append(_cmul(pw[-1][0], pw[-1][1], lr, li))
    pr = jnp.zeros((SUBLANES, width), F32)
    pi = jnp.zeros((SUBLANES, width), F32)
    for r in range(SUBLANES):
        n = SUBLANES - 1 - r if descending else r
        pr = jnp.where(row == r, pw[n][0], pr)
        pi = jnp.where(row == r, pw[n][1], pi)

    n_tiles = xr_ref.shape[0] // SUBLANES
    unroll = lambda count: SCAN_UNROLL if count % SCAN_UNROLL == 0 else 1

    def in_tile(i, _):
        t0 = pl.multiple_of(i * SUBLANES, SUBLANES)
        xr = xr_ref[pl.ds(t0, SUBLANES), :]
        xi = xi_ref[pl.ds(t0, SUBLANES), :]
        for k in (1, 2, 4):
            sh = SUBLANES - k if descending else k
            keep = (row < SUBLANES - k) if descending else (row >= k)
            sr, si = _cmul(pw[k - 1][0], pw[k - 1][1], pltpu.roll(xr, sh, axis=0), pltpu.roll(xi, sh, axis=0))
            xr = jnp.where(keep, xr + sr, xr)
            xi = jnp.where(keep, xi + si, xi)
        xr_ref[pl.ds(t0, SUBLANES), :] = xr
        xi_ref[pl.ds(t0, SUBLANES), :] = xi
        return 0

    lax.fori_loop(0, n_tiles, in_tile, 0, unroll=unroll(n_tiles))
    lam8 = pw[SUBLANES - 1]

    def tile(i, carry):
        cr, ci = carry
        t0 = pl.multiple_of(i * SUBLANES, SUBLANES)
        xr = xr_ref[pl.ds(t0, SUBLANES), :]
        xi = xi_ref[pl.ds(t0, SUBLANES), :]
        hr, hi = _cmul(pr, pi, cr, ci)
        hr, hi = hr + xr, hi + xi
        xr_ref[pl.ds(t0, SUBLANES), :] = hr
        xi_ref[pl.ds(t0, SUBLANES), :] = hi
        if hpr_ref is not None:
            hpr_ref[pl.ds(t0, SUBLANES), :] = _prev_rows(hr, cr, descending)
            hpi_ref[pl.ds(t0, SUBLANES), :] = _prev_rows(hi, ci, descending)
        nr, ni = _cmul(lam8[0], lam8[1], cr, ci)
        return nr + _last_row(xr, descending), ni + _last_row(xi, descending)

    carry = (jnp.zeros((1, width), F32), jnp.zeros((1, width), F32))
    for first, count, step in ranges:
        carry = lax.fori_loop(0, count, lambda j, c, first=first, step=step: tile(first + step * j, c), carry,
                              unroll=unroll(count))


def _log1p_pos(y):
    return jnp.where(y < 0.01, y * (1.0 - y * (0.5 - y * (1.0 / 3.0 - 0.25 * y))), jnp.log(1.0 + y))


def _softplus(x):
    return jnp.maximum(x, 0.0) + _log1p_pos(jnp.exp(-jnp.abs(x)))


def _neg_expm1(z):
    series = -z * (1.0 + z * (0.5 + z * (1.0 / 6.0 + z * (1.0 / 24.0 + z * (1.0 / 120.0)))))
    return jnp.where(z > -0.1, series, 1.0 - jnp.exp(z))


def _lru_gates(u, w_a, b_a, w_i, b_i, lam):
    ub = u.astype(BF16)
    r = jax.nn.sigmoid(jnp.dot(ub, w_a.astype(BF16), preferred_element_type=F32) + b_a)
    i = jax.nn.sigmoid(jnp.dot(ub, w_i.astype(BF16), preferred_element_type=F32) + b_i)
    log_a = (-LRU_C) * _softplus(-lam) * r
    return jnp.exp(log_a), jnp.sqrt(_neg_expm1(2.0 * log_a)) * (i * u)


LRU_PER_STEP = 4
LRU_PER_STEP_BWD = 2


def _lru_specs(per, bw, order):
    w = pl.BlockSpec((2, per, bw, bw), lambda *g: (0, order(*g), 0, 0))
    v = pl.BlockSpec((2, per, 1, bw), lambda *g: (0, order(*g), 0, 0))
    return [w, v, w, v, v]


def lru_fwd(name, u, w_a, b_a, w_i, b_i, lam, s_ctx):
    nb, s, _ = u.shape
    nblk, bw = w_a.shape[1], w_a.shape[2]
    per = min(LRU_PER_STEP, nblk)

    def body(u_ref, wa, ba, wi, bi, lm, o_ref, a_s, x_s, h_s):
        for d in (0, 1):
            for k in range(per):
                cols = slice(k * bw, (k + 1) * bw)
                a, bx = _lru_gates(u_ref[:, cols], wa[d, k], ba[d, k], wi[d, k], bi[d, k], lm[d, k])
                a_s[:, cols] = a
                x_s[:, cols] = bx
            _scan_real(a_s, x_s, h_s, None, _scan_order(d, False, s_ctx, s))
            if d == 0:
                o_ref[...] = h_s[...]
            else:
                o_ref[...] += h_s[...]

    blk = pl.BlockSpec((None, s, per * bw), lambda b, n: (b, 0, n))
    return pl.pallas_call(
        body,
        name=name,
        grid=(nb, nblk // per),
        in_specs=[blk] + _lru_specs(per, bw, lambda b, n: n),
        out_specs=blk,
        out_shape=jax.ShapeDtypeStruct(u.shape, F32),
        scratch_shapes=[pltpu.VMEM((s, per * bw), F32)] * 3,
        compiler_params=_cparams(("parallel", "parallel")),
    )(u, w_a, b_a, w_i, b_i, lam)


def lru_bwd(name, u, w_a, b_a, w_i, b_i, lam, dh, s_ctx):
    nb, s, _ = u.shape
    nblk, bw = w_a.shape[1], w_a.shape[2]
    per = min(LRU_PER_STEP_BWD, nblk)

    def body(u_ref, wa, ba, wi, bi, lm, dh_ref, du_ref, dwa, dba, dwi, dbi, dlm, a_s, x_s, hp_s, wp_s):
        b = pl.program_id(1)
        for d in (0, 1):
            for k in range(per):
                cols = slice(k * bw, (k + 1) * bw)
                a, bx = _lru_gates(u_ref[:, cols], wa[d, k], ba[d, k], wi[d, k], bi[d, k], lm[d, k])
                a_s[:, cols] = a
                x_s[:, cols] = bx
            _scan_real(a_s, x_s, None, hp_s, _scan_order(d, False, s_ctx, s))
            x_s[...] = a_s[...] * dh_ref[...]
            _scan_real(a_s, x_s, None, wp_s, _scan_order(d, True, s_ctx, s))
            for k in range(per):
                cols = slice(k * bw, (k + 1) * bw)
                g = dh_ref[:, cols] + wp_s[:, cols]
                _, vjp = jax.vjp(_lru_gates, u_ref[:, cols], wa[d, k], ba[d, k], wi[d, k], bi[d, k], lm[d, k])
                grads = vjp((g * hp_s[:, cols], g))
                if d == 0:
                    du_ref[:, cols] = grads[0]
                else:
                    du_ref[:, cols] += grads[0]
                for ref, val in zip((dwa, dba, dwi, dbi, dlm), grads[1:]):
                    @pl.when(b == 0)
                    def _(ref=ref, val=val, k=k):
                        ref[d, k] = val

                    @pl.when(b > 0)
                    def _(ref=ref, val=val, k=k):
                        ref[d, k] += val

    blk = pl.BlockSpec((None, s, per * bw), lambda n, b: (b, 0, n))
    pspecs = _lru_specs(per, bw, lambda n, b: n)
    return pl.pallas_call(
        body,
        name=name,
        grid=(nblk // per, nb),
        in_specs=[blk] + pspecs + [blk],
        out_specs=[blk] + pspecs,
        out_shape=[jax.ShapeDtypeStruct(u.shape, F32)] + [jax.ShapeDtypeStruct(p.shape, F32) for p in (w_a, b_a, w_i, b_i, lam)],
        scratch_shapes=[pltpu.VMEM((s, per * bw), F32)] * 4,
        compiler_params=_cparams(("parallel", "arbitrary")),
    )(u, w_a, b_a, w_i, b_i, lam, dh)


S5_TILE_CH = 128
S5_TILE_STATES = S5_TILE_CH // S5_GROUP_CH * S5_STATE


def _dot_nt(a, b):
    return lax.dot_general(a, b, (((1,), (1,)), ((), ())), preferred_element_type=F32)


def _dot_tn(a, b):
    return lax.dot_general(a, b, (((0,), (0,)), ((), ())), preferred_element_type=F32)


def _s5_specs(order):
    lam = pl.BlockSpec((2, 1, S5_TILE_STATES), lambda *g: (0, 0, order(*g)))
    mat = pl.BlockSpec((2, None, S5_TILE_STATES, S5_TILE_CH), lambda *g: (0, order(*g), 0, 0))
    return [lam, lam, mat, mat, mat, mat]


def s5_fwd(name, u, lam_r, lam_i, bt_r, bt_i, ct_r, ct_i, s_ctx):
    nb, s, w = u.shape

    def body(u_ref, lr, li, btr, bti, ctr, cti, o_ref, xr_s, xi_s):
        ub = u_ref[...].astype(BF16)
        for d in (0, 1):
            xr_s[...] = _dot_nt(ub, btr[d].astype(BF16))
            xi_s[...] = _dot_nt(ub, bti[d].astype(BF16))
            _scan_cplx(lr[d], li[d], xr_s, xi_s, None, None, _scan_order(d, False, s_ctx, s))
            y = (jnp.dot(xr_s[...].astype(BF16), ctr[d].astype(BF16), preferred_element_type=F32)
                 - jnp.dot(xi_s[...].astype(BF16), cti[d].astype(BF16), preferred_element_type=F32))
            if d == 0:
                o_ref[...] = y
            else:
                o_ref[...] += y

    blk = pl.BlockSpec((None, s, S5_TILE_CH), lambda b, j: (b, 0, j))
    return pl.pallas_call(
        body,
        name=name,
        grid=(nb, w // S5_TILE_CH),
        in_specs=[blk] + _s5_specs(lambda b, j: j),
        out_specs=blk,
        out_shape=jax.ShapeDtypeStruct(u.shape, F32),
        scratch_shapes=[pltpu.VMEM((s, S5_TILE_STATES), F32)] * 2,
        compiler_params=_cparams(("parallel", "parallel")),
    )(u, lam_r, lam_i, bt_r, bt_i, ct_r, ct_i)


def s5_bwd(name, u, lam_r, lam_i, bt_r, bt_i, ct_r, ct_i, dy, s_ctx):
    nb, s, w = u.shape

    def body(u_ref, lr, li, btr, bti, ctr, cti, dy_ref, du_ref, dlr, dli, dbtr, dbti, dctr, dcti,
             hr_s, hi_s, hpr_s, hpi_s, gr_s, gi_s):
        b = pl.program_id(1)
        ub = u_ref[...].astype(BF16)
        dyb = dy_ref[...].astype(BF16)
        du = jnp.zeros((s, S5_TILE_CH), F32)
        for d in (0, 1):
            hr_s[...] = _dot_nt(ub, btr[d].astype(BF16))
            hi_s[...] = _dot_nt(ub, bti[d].astype(BF16))
            _scan_cplx(lr[d], li[d], hr_s, hi_s, hpr_s, hpi_s, _scan_order(d, False, s_ctx, s))
            d_ctr = _dot_tn(hr_s[...].astype(BF16), dyb)
            d_cti = -_dot_tn(hi_s[...].astype(BF16), dyb)
            gr_s[...] = _dot_nt(dyb, ctr[d].astype(BF16))
            gi_s[...] = -_dot_nt(dyb, cti[d].astype(BF16))
            _scan_cplx(lr[d], -li[d], gr_s, gi_s, None, None, _scan_order(d, True, s_ctx, s))
            gr, gi = gr_s[...], gi_s[...]
            hpr, hpi = hpr_s[...], hpi_s[...]
            d_lr = jnp.sum(gr * hpr + gi * hpi, axis=0, keepdims=True)
            d_li = jnp.sum(gi * hpr - gr * hpi, axis=0, keepdims=True)
            grb, gib = gr.astype(BF16), gi.astype(BF16)
            du = du + jnp.dot(grb, btr[d].astype(BF16), preferred_element_type=F32)
            du = du + jnp.dot(gib, bti[d].astype(BF16), preferred_element_type=F32)
            d_btr = _dot_tn(grb, ub)
            d_bti = _dot_tn(gib, ub)
            for ref, val in zip((dlr, dli, dbtr, dbti, dctr, dcti), (d_lr, d_li, d_btr, d_bti, d_ctr, d_cti)):
                @pl.when(b == 0)
                def _(ref=ref, val=val):
                    ref[d] = val

                @pl.when(b > 0)
                def _(ref=ref, val=val):
                    ref[d] += val
        du_ref[...] = du

    blk = pl.BlockSpec((None, s, S5_TILE_CH), lambda j, b: (b, 0, j))
    pspecs = _s5_specs(lambda j, b: j)
    params = (lam_r, lam_i, bt_r, bt_i, ct_r, ct_i)
    return pl.pallas_call(
        body,
        name=name,
        grid=(w // S5_TILE_CH, nb),
        in_specs=[blk] + pspecs + [blk],
        out_specs=[blk] + pspecs,
        out_shape=[jax.ShapeDtypeStruct(u.shape, F32)] + [jax.ShapeDtypeStruct(p.shape, F32) for p in params],
        scratch_shapes=[pltpu.VMEM((s, S5_TILE_STATES), F32)] * 6,
        compiler_params=_cparams(("parallel", "arbitrary")),
    )(u, lam_r, lam_i, bt_r, bt_i, ct_r, ct_i, dy)


def small_fwd(name, f, ins, out_shapes):
    n = len(ins)

    def body(*refs):
        for o, r in zip(refs[n:], f([r[...] for r in refs[:n]])):
            o[...] = r

    return pl.pallas_call(
        body, name=name,
        out_shape=[jax.ShapeDtypeStruct(s, F32) for s in out_shapes],
        compiler_params=_cparams(),
    )(*ins)


def small_bwd(name, f, ins, cots):
    n, nc = len(ins), len(cots)

    def body(*refs):
        _, vjp = jax.vjp(f, [r[...] for r in refs[:n]])
        (grads,) = vjp([r[...] for r in refs[n:n + nc]])
        for o, r in zip(refs[n + nc:], grads):
            o[...] = r

    return pl.pallas_call(
        body, name=name,
        out_shape=[jax.ShapeDtypeStruct(a.shape, F32) for a in ins],
        compiler_params=_cparams(),
    )(*ins, *cots)


def _row(x, r):
    return jnp.sum(jnp.where(lax.broadcasted_iota(jnp.int32, x.shape, 0) == r, x, 0.0), axis=0, keepdims=True)


def _col(x, c):
    return jnp.sum(jnp.where(lax.broadcasted_iota(jnp.int32, x.shape, 1) == c, x, 0.0), axis=1, keepdims=True)


def _chunk_at(i, reverse, ncc, nc):
    if not reverse:
        return i
    return jnp.where(i < ncc, ncc - 1 - i, nc - 1 - (i - ncc))


def _tri(n, reverse):
    li = lax.broadcasted_iota(jnp.int32, (n, n), 0)
    si = lax.broadcasted_iota(jnp.int32, (n, n), 1)
    return jnp.where((li <= si) if reverse else (li >= si), 1.0, 0.0)


_HI = lax.Precision.HIGHEST


def _ssd_chunk(xs, bm, cm, dtc, dtr, a_row, a_col, hs, reverse):
    n = bm.shape[0]
    last = 0 if reverse else n - 1
    tri = _tri(n, reverse)
    cum_c = jnp.dot(tri, dtc * -jnp.exp(a_row), precision=_HI, preferred_element_type=F32)
    cum_r = lax.dot_general(dtr * -jnp.exp(a_col), tri, (((1,), (1,)), ((), ())), precision=_HI, preferred_element_type=F32)
    tot_r = _row(cum_c, last)
    bmb, cmb = bm.astype(BF16), cm.astype(BF16)
    cb = _dot_nt(cmb, bmb)
    ys, hn = [], []
    for hd in range(len(xs)):
        cl = _col(cum_c, hd)
        tot = _col(tot_r, hd)
        decay = jnp.exp(jnp.where(tri > 0.0, cl - _row(cum_r, hd), -jnp.inf))
        xd = xs[hd] * _col(dtc, hd)
        y = jnp.dot((cb * decay).astype(BF16), xd.astype(BF16), preferred_element_type=F32)
        y = y + _dot_nt(cmb, hs[hd].astype(BF16)) * jnp.exp(cl)
        hnew = hs[hd] * jnp.exp(tot) + _dot_tn((xd * jnp.exp(tot - cl)).astype(BF16), bmb)
        ys.append(y)
        hn.append(hnew)
    return ys, hn


def _ssd_specs(reverse, ncc, nc, order):
    ch = lambda *g: _chunk_at(order(*g)[1], reverse, ncc, nc)
    b_ = lambda *g: order(*g)[0]
    gn = SSD_GROUPS * SSD_STATE
    return [
        pl.BlockSpec((None, SSD_CHUNK, SSD_W), lambda *g: (b_(*g), ch(*g), 0)),
        pl.BlockSpec((None, SSD_CHUNK, gn), lambda *g: (b_(*g), ch(*g), SSD_W // gn)),
        pl.BlockSpec((None, SSD_CHUNK, gn), lambda *g: (b_(*g), ch(*g), SSD_W // gn + 1)),
        pl.BlockSpec((None, SSD_GROUPS, SSD_CHUNK, SSD_HPG), lambda *g: (b_(*g), 0, ch(*g), 0)),
        pl.BlockSpec((None, SSD_GROUPS, SSD_HPG, SSD_CHUNK), lambda *g: (b_(*g), 0, 0, ch(*g))),
        pl.BlockSpec((SSD_GROUPS, 1, SSD_HPG), lambda *g: (0, 0, 0)),
        pl.BlockSpec((SSD_GROUPS, SSD_HPG, 1), lambda *g: (0, 0, 0)),
    ]


def _ssd_group_inputs(g, x_ref, bm_ref, cm_ref, dtc_ref, dtr_ref, ar_ref, ac_ref):
    p, n = SSD_HEAD_DIM, SSD_STATE
    xs = [x_ref[:, p * (SSD_HPG * g + hd):p * (SSD_HPG * g + hd + 1)] for hd in range(SSD_HPG)]
    return xs, bm_ref[:, n * g:n * (g + 1)], cm_ref[:, n * g:n * (g + 1)], dtc_ref[g], dtr_ref[g], ar_ref[g], ac_ref[g]


def ssd_fwd(name, xbc, dt_col, dt_row, a_row, a_col, reverse, s_ctx):
    nb, s, _ = xbc.shape
    nc, ncc = s // SSD_CHUNK, s_ctx // SSD_CHUNK
    p = SSD_HEAD_DIM

    def body(x_ref, bm_ref, cm_ref, dtc_ref, dtr_ref, ar_ref, ac_ref, y_ref, hst_ref, h_s):
        i = pl.program_id(1)

        @pl.when(i == 0)
        def _():
            h_s[...] = jnp.zeros_like(h_s)

        hst_ref[...] = h_s[...]
        for g in range(SSD_GROUPS):
            xs, bm, cm, dtc, dtr, ar, ac = _ssd_group_inputs(g, x_ref, bm_ref, cm_ref, dtc_ref, dtr_ref, ar_ref, ac_ref)
            ys, hn = _ssd_chunk(xs, bm, cm, dtc, dtr, ar, ac, [h_s[g, hd] for hd in range(SSD_HPG)], reverse)
            for hd in range(SSD_HPG):
                y_ref[:, p * (SSD_HPG * g + hd):p * (SSD_HPG * g + hd + 1)] = ys[hd]
                h_s[g, hd] = hn[hd]

    state = (SSD_GROUPS, SSD_HPG, SSD_HEAD_DIM, SSD_STATE)
    return pl.pallas_call(
        body,
        name=name,
        grid=(nb, nc),
        in_specs=_ssd_specs(reverse, ncc, nc, lambda b, i: (b, i)),
        out_specs=[pl.BlockSpec((None, SSD_CHUNK, SSD_W), lambda b, i: (b, _chunk_at(i, reverse, ncc, nc), 0)),
                   pl.BlockSpec((None, None) + state, lambda b, i: (b, i, 0, 0, 0, 0))],
        out_shape=[jax.ShapeDtypeStruct((nb, s, SSD_W), F32), jax.ShapeDtypeStruct((nb, nc) + state, F32)],
        scratch_shapes=[pltpu.VMEM(state, F32)],
        compiler_params=_cparams(("parallel", "arbitrary")),
    )(xbc, xbc, xbc, dt_col, dt_row, a_row, a_col)


def ssd_bwd(name, xbc, dt_col, dt_row, a_row, a_col, hst, dy, reverse, s_ctx):
    nb, s, _ = xbc.shape
    nc, ncc = s // SSD_CHUNK, s_ctx // SSD_CHUNK
    p, n = SSD_HEAD_DIM, SSD_STATE

    def body(x_ref, bm_ref, cm_ref, dtc_ref, dtr_ref, ar_ref, ac_ref, hst_ref, dy_ref,
             dx_ref, dbm_ref, dcm_ref, ddtc_ref, ddtr_ref, dar_ref, dac_ref, dh_s):
        i = pl.program_id(1)

        @pl.when(i == 0)
        def _():
            dh_s[...] = jnp.zeros_like(dh_s)

        for g in range(SSD_GROUPS):
            xs, bm, cm, dtc, dtr, ar, ac = _ssd_group_inputs(g, x_ref, bm_ref, cm_ref, dtc_ref, dtr_ref, ar_ref, ac_ref)
            hs = [hst_ref[g, hd] for hd in range(SSD_HPG)]
            _, vjp = jax.vjp(functools.partial(_ssd_chunk, reverse=reverse), xs, bm, cm, dtc, dtr, ar, ac, hs)
            dys = [dy_ref[:, p * (SSD_HPG * g + hd):p * (SSD_HPG * g + hd + 1)] for hd in range(SSD_HPG)]
            dxs, dbm, dcm, ddtc, ddtr, dar, dac, dhs = vjp((dys, [dh_s[g, hd] for hd in range(SSD_HPG)]))
            for hd in range(SSD_HPG):
                dx_ref[:, p * (SSD_HPG * g + hd):p * (SSD_HPG * g + hd + 1)] = dxs[hd]
                dh_s[g, hd] = dhs[hd]
            dbm_ref[:, n * g:n * (g + 1)] = dbm
            dcm_ref[:, n * g:n * (g + 1)] = dcm
            ddtc_ref[g] = ddtc
            ddtr_ref[g] = ddtr

            @pl.when(i == 0)
            def _(g=g, dar=dar, dac=dac):
                dar_ref[g] = dar
                dac_ref[g] = dac

            @pl.when(i > 0)
            def _(g=g, dar=dar, dac=dac):
                dar_ref[g] += dar
                dac_ref[g] += dac

    ch = lambda b, i: _chunk_at(nc - 1 - i, reverse, ncc, nc)
    state = (SSD_GROUPS, SSD_HPG, SSD_HEAD_DIM, SSD_STATE)
    gn = SSD_GROUPS * SSD_STATE
    in_specs = _ssd_specs(reverse, ncc, nc, lambda b, i: (b, nc - 1 - i)) + [
        pl.BlockSpec((None, None) + state, lambda b, i: (b, nc - 1 - i, 0, 0, 0, 0)),
        pl.BlockSpec((None, SSD_CHUNK, SSD_W), lambda b, i: (b, ch(b, i), 0))]
    out_specs = [
        pl.BlockSpec((None, SSD_CHUNK, SSD_W), lambda b, i: (b, ch(b, i), 0)),
        pl.BlockSpec((None, SSD_CHUNK, gn), lambda b, i: (b, ch(b, i), 0)),
        pl.BlockSpec((None, SSD_CHUNK, gn), lambda b, i: (b, ch(b, i), 0)),
        pl.BlockSpec((None, SSD_GROUPS, SSD_CHUNK, SSD_HPG), lambda b, i: (b, 0, ch(b, i), 0)),
        pl.BlockSpec((None, SSD_GROUPS, SSD_HPG, SSD_CHUNK), lambda b, i: (b, 0, 0, ch(b, i))),
        pl.BlockSpec((None, SSD_GROUPS, 1, SSD_HPG), lambda b, i: (b, 0, 0, 0)),
        pl.BlockSpec((None, SSD_GROUPS, SSD_HPG, 1), lambda b, i: (b, 0, 0, 0)),
    ]
    out_shape = [
        jax.ShapeDtypeStruct((nb, s, SSD_W), F32),
        jax.ShapeDtypeStruct((nb, s, gn), F32),
        jax.ShapeDtypeStruct((nb, s, gn), F32),
        jax.ShapeDtypeStruct(dt_col.shape, F32),
        jax.ShapeDtypeStruct(dt_row.shape, F32),
        jax.ShapeDtypeStruct((nb, SSD_GROUPS, 1, SSD_HPG), F32),
        jax.ShapeDtypeStruct((nb, SSD_GROUPS, SSD_HPG, 1), F32),
    ]
    return pl.pallas_call(
        body,
        name=name,
        grid=(nb, nc),
        in_specs=in_specs,
        out_specs=out_specs,
        out_shape=out_shape,
        scratch_shapes=[pltpu.VMEM(state, F32)],
        compiler_params=_cparams(("parallel", "arbitrary")),
    )(xbc, xbc, xbc, dt_col, dt_row, a_row, a_col, hst, dy)


HG_TILES = HG_CHUNK // SUBLANES


def _hg_cum_tiles(x_t, reverse):
    row = lax.broadcasted_iota(jnp.int32, x_t[0].shape, 0)
    out = [None] * len(x_t)
    off = None
    for i in (reversed(range(len(x_t))) if reverse else range(len(x_t))):
        c = x_t[i]
        for k in (1, 2, 4):
            keep = (row < SUBLANES - k) if reverse else (row >= k)
            c = jnp.where(keep, c + pltpu.roll(c, SUBLANES - k if reverse else k, axis=0), c)
        out[i] = c if off is None else c + off
        off = _last_row(out[i], reverse)
    return out, off


def _hg_pairs(reverse):
    row = lax.broadcasted_iota(jnp.int32, (SUBLANES, HG_DK), 0)
    rots = []
    for r in range(SUBLANES):
        rots.append(((SUBLANES - r) % SUBLANES, row <= SUBLANES - 1 - r) if reverse else (r, row >= r))
    return [(j, [i for i in range(HG_TILES) if (i <= j if reverse else i >= j)], rots) for j in range(HG_TILES)]


def _rot(x, sh):
    return pltpu.roll(x, sh, axis=0) if sh else x


def _cat(tiles):
    return jnp.concatenate(tiles, axis=0)


def _hg_chunk_fwd(q_t, k_t, lf_t, v_t, st, reverse):
    cum_t, tot = _hg_cum_tiles(lf_t, reverse)
    y_t = [jnp.zeros(v_t[0].shape, F32) for _ in v_t]
    for j, l_tiles, rots in _hg_pairs(reverse):
        for sh, diag_ok in rots:
            k_j, c_j, v_j = _rot(k_t[j], sh), _rot(cum_t[j], sh), _rot(v_t[j], sh)
            for i in l_tiles:
                e = jnp.exp(cum_t[i] - c_j)
                if i == j:
                    e = jnp.where(diag_ok, e, 0.0)
                att = jnp.sum(q_t[i] * (k_j * e), axis=1, keepdims=True)
                y_t[i] = y_t[i] + att * v_j
    q, k, v, cum = _cat(q_t), _cat(k_t), _cat(v_t), _cat(cum_t)
    y_state = _dot_nt((q * jnp.exp(cum)).astype(BF16), st.astype(BF16))
    st_new = st * jnp.exp(tot) + _dot_tn(v.astype(BF16), (k * jnp.exp(tot - cum)).astype(BF16))
    return [y + y_state[SUBLANES * i:SUBLANES * (i + 1)] for i, y in enumerate(y_t)], st_new


def _hg_chunk_bwd(q_t, k_t, lf_t, v_t, st, dy_t, dst_new, reverse):
    nt = len(q_t)
    cum_t, tot = _hg_cum_tiles(lf_t, reverse)
    q, k, v, cum, dy = _cat(q_t), _cat(k_t), _cat(v_t), _cat(cum_t), _cat(dy_t)
    e_cum, e_tot, e_end = jnp.exp(cum), jnp.exp(tot), jnp.exp(tot - cum)
    qt, khat = q * e_cum, k * e_end
    dyb, dsb = dy.astype(BF16), dst_new.astype(BF16)
    dqt = jnp.dot(dyb, st.astype(BF16), preferred_element_type=F32)
    dst = dst_new * e_tot + _dot_tn(dyb, qt.astype(BF16))
    dv = _dot_nt(khat.astype(BF16), dsb)
    dkhat = jnp.dot(v.astype(BF16), dsb, preferred_element_type=F32)
    t1 = dkhat * khat
    dtot = jnp.sum(dst_new * st, axis=0, keepdims=True) * e_tot + jnp.sum(t1, axis=0, keepdims=True)
    rows = lax.broadcasted_iota(jnp.int32, cum.shape, 0)
    last = 0 if reverse else cum.shape[0] - 1
    dcum = dqt * qt - t1 + jnp.where(rows == last, dtot, 0.0)
    tiles = lambda a: [a[SUBLANES * i:SUBLANES * (i + 1)] for i in range(nt)]
    dq_t, dk_t, dv_t, dcum_t = tiles(dqt * e_cum), tiles(dkhat * e_end), tiles(dv), tiles(dcum)
    for j, l_tiles, rots in _hg_pairs(reverse):
        for sh, diag_ok in rots:
            k_j, c_j, v_j = _rot(k_t[j], sh), _rot(cum_t[j], sh), _rot(v_t[j], sh)
            acc_v = acc_k = acc_c = None
            for i in l_tiles:
                e = jnp.exp(cum_t[i] - c_j)
                if i == j:
                    e = jnp.where(diag_ok, e, 0.0)
                ke, qe = k_j * e, q_t[i] * e
                p = q_t[i] * ke
                att = jnp.sum(p, axis=1, keepdims=True)
                datt = jnp.sum(dy_t[i] * v_j, axis=1, keepdims=True)
                g = datt * p
                dq_t[i] = dq_t[i] + datt * ke
                dcum_t[i] = dcum_t[i] + g
                av, ak = att * dy_t[i], datt * qe
                acc_v, acc_k, acc_c = (av, ak, g) if acc_v is None else (acc_v + av, acc_k + ak, acc_c + g)
            back = (SUBLANES - sh) % SUBLANES
            dv_t[j] = dv_t[j] + _rot(acc_v, back)
            dk_t[j] = dk_t[j] + _rot(acc_k, back)
            dcum_t[j] = dcum_t[j] - _rot(acc_c, back)
    dlf_t, _ = _hg_cum_tiles(dcum_t, not reverse)
    return dq_t, dk_t, dlf_t, dv_t, dst


def _hg_super(s_ctx):
    return min(256, s_ctx)


def hg_fwd(name, q, k, lf, v, reverse, s_ctx):
    nb, s, w = q.shape
    nh, dk, sup = w // HG_DK, HG_DK, _hg_super(s_ctx)
    nsup, nsc, cps = s // sup, s_ctx // sup, sup // HG_CHUNK

    def body(q_ref, k_ref, lf_ref, v_ref, y_ref, hst_ref, st_s):
        i = pl.program_id(2)

        @pl.when(i == 0)
        def _():
            st_s[...] = jnp.zeros_like(st_s)

        def step(c, st):
            r0 = pl.multiple_of((cps - 1 - c if reverse else c) * HG_CHUNK, HG_CHUNK)
            tile = lambda ref: [ref[pl.ds(r0 + SUBLANES * i, SUBLANES), :] for i in range(HG_TILES)]
            hst_ref[c] = st
            y_t, st_new = _hg_chunk_fwd(tile(q_ref), tile(k_ref), tile(lf_ref), tile(v_ref), st, reverse)
            for i in range(HG_TILES):
                y_ref[pl.ds(r0 + SUBLANES * i, SUBLANES), :] = y_t[i]
            return st_new

        st_s[...] = lax.fori_loop(0, cps, step, st_s[...], unroll=2 if cps % 2 == 0 else 1)

    blk = pl.BlockSpec((None, sup, dk), lambda b, h, i: (b, _chunk_at(i, reverse, nsc, nsup), h))
    return pl.pallas_call(
        body,
        name=name,
        grid=(nb, nh, nsup),
        in_specs=[blk] * 4,
        out_specs=[blk, pl.BlockSpec((None, None, cps, dk, dk), lambda b, h, i: (b, h, i, 0, 0))],
        out_shape=[jax.ShapeDtypeStruct(q.shape, F32), jax.ShapeDtypeStruct((nb, nh, s // HG_CHUNK, dk, dk), F32)],
        scratch_shapes=[pltpu.VMEM((dk, dk), F32)],
        compiler_params=_cparams(("parallel", "parallel", "arbitrary")),
    )(q, k, lf, v)


def hg_bwd(name, q, k, lf, v, hst, dy, reverse, s_ctx):
    nb, s, w = q.shape
    nh, dk, sup = w // HG_DK, HG_DK, _hg_super(s_ctx)
    nsup, nsc, cps = s // sup, s_ctx // sup, sup // HG_CHUNK

    def body(q_ref, k_ref, lf_ref, v_ref, hst_ref, dy_ref, dq_ref, dk_ref, dlf_ref, dv_ref, dst_s):
        i = pl.program_id(2)

        @pl.when(i == 0)
        def _():
            dst_s[...] = jnp.zeros_like(dst_s)

        def step(cc, dst):
            c = cps - 1 - cc
            r0 = pl.multiple_of((cps - 1 - c if reverse else c) * HG_CHUNK, HG_CHUNK)
            tile = lambda ref: [ref[pl.ds(r0 + SUBLANES * i, SUBLANES), :] for i in range(HG_TILES)]
            dq_t, dk_t, dlf_t, dv_t, dst_prev = _hg_chunk_bwd(
                tile(q_ref), tile(k_ref), tile(lf_ref), tile(v_ref), hst_ref[c], tile(dy_ref), dst, reverse)
            for ref, val in zip((dq_ref, dk_ref, dlf_ref, dv_ref), (dq_t, dk_t, dlf_t, dv_t)):
                for i in range(HG_TILES):
                    ref[pl.ds(r0 + SUBLANES * i, SUBLANES), :] = val[i]
            return dst_prev

        dst_s[...] = lax.fori_loop(0, cps, step, dst_s[...], unroll=2 if cps % 2 == 0 else 1)

    blk = pl.BlockSpec((None, sup, dk), lambda b, h, i: (b, _chunk_at(nsup - 1 - i, reverse, nsc, nsup), h))
    return pl.pallas_call(
        body,
        name=name,
        grid=(nb, nh, nsup),
        in_specs=[blk] * 4 + [pl.BlockSpec((None, None, cps, dk, dk), lambda b, h, i: (b, h, nsup - 1 - i, 0, 0)), blk],
        out_specs=[blk] * 4,
        out_shape=[jax.ShapeDtypeStruct(q.shape, F32)] * 4,
        scratch_shapes=[pltpu.VMEM((dk, dk), F32)],
        compiler_params=_cparams(("parallel", "parallel", "arbitrary")),
    )(q, k, lf, v, hst, dy)


def f_s5_discretize(ins):
    lam_re, lam_im, log_step, b_re, b_im = ins
    step = jnp.exp(log_step)
    mag = jnp.exp(lam_re * step)
    ar, ai = mag * jnp.cos(lam_im * step), mag * jnp.sin(lam_im * step)
    den = lam_re * lam_re + lam_im * lam_im
    zr = ((ar - 1.0) * lam_re + ai * lam_im) / den
    zi = (ai * lam_re - (ar - 1.0) * lam_im) / den
    return [ar, ai, zr * b_re - zi * b_im, zr * b_im + zi * b_re]


def s5_tiles_of(m):
    g, p, k = m.shape
    gt = S5_TILE_CH // k
    eye = jnp.eye(gt, dtype=m.dtype)
    t = m.reshape(g // gt, gt, p, 1, k) * eye[None, :, None, :, None]
    return t.reshape(g // gt, gt * p, gt * k)


def s5_groups_of(t, g, p, k):
    gt = S5_TILE_CH // k
    eye = jnp.eye(gt, dtype=t.dtype)
    return jnp.sum(t.reshape(g // gt, gt, p, gt, k) * eye[None, :, None, :, None], axis=3).reshape(g, p, k)


def f_lower_bounds(ins):
    (logits,) = ins
    e = jnp.exp(logits - jnp.max(logits, axis=0, keepdims=True))
    p = e / jnp.sum(e, axis=0, keepdims=True)
    n = logits.shape[0]
    li = lax.broadcasted_iota(jnp.int32, (n, n), 0)
    si = lax.broadcasted_iota(jnp.int32, (n, n), 1)
    after_first = jnp.where(jnp.logical_and(si >= 1, si <= li), 1.0, 0.0)
    return [jnp.dot(after_first, p, precision=_HI, preferred_element_type=F32)]


def f_silu(ins):
    return [_silu(ins[0])]


def f_norm_keep(shift_row, scale_row):
    def f(tv, mv, pv):
        return [tv[0], _rms(tv[0], pv[0]) * (1.0 + mv[scale_row]) + mv[shift_row]]
    return f


def f_dt(tv, mv, pv):
    return [_softplus(tv[0] + pv[0])]


def f_even_finish(tv, mv, pv):
    y_f, y_b, xs, z, h_sum, gy = tv
    d_exp, g = pv
    y = _rms((y_f + y_b + d_exp * xs) * _silu(z), g)
    return [y, h_sum * jax.nn.gelu(gy)]


def f_odd_prep(tv, mv, pv):
    q, f_f, f_b = tv
    (lb,) = pv
    outs = [_silu(q)]
    for f in (f_f, f_b):
        outs.append((1.0 - lb) * jax.nn.sigmoid(-f))
        outs.append(jnp.log(lb + (1.0 - lb) * jax.nn.sigmoid(f)))
    return outs


def f_odd_finish(tv, mv, pv):
    o_f, o_b, g, s5y, u = tv
    norm_g, s5_d, glu_w, glu_b = pv
    o = o_f + o_b
    w = o.shape[1]
    hi = lax.broadcasted_iota(jnp.int32, (w, w), 0) // HG_DK
    hj = lax.broadcasted_iota(jnp.int32, (w, w), 1) // HG_DK
    head_mean = jnp.where(hi == hj, 1.0 / HG_DK, 0.0)
    ms = jnp.dot(o * o, head_mean, precision=_HI, preferred_element_type=F32)
    on = o * lax.rsqrt(ms + RMS_EPS) * norm_g * _silu(g)
    y = jax.nn.gelu(s5y + s5_d * u)
    gate = jax.nn.sigmoid(jnp.dot(y.astype(BF16), glu_w.astype(BF16), preferred_element_type=F32) + glu_b)
    return [on, y * gate]


def final_loss(name, s, br, mod, g, target, s_ctx):
    nb, st, d = s.shape
    tb = TOK_BLOCK
    assert s_ctx == tb

    def lossf(sv, bv, gate, gv, tv):
        y = _rms(sv + gate * bv, gv)
        err = jnp.square(y - tv)
        return 0.5 * jnp.sum(jnp.mean(err, axis=-1, keepdims=True), axis=0, keepdims=True)

    def body(s_ref, b_ref, m_ref, g_ref, t_ref, l_ref, ds_ref, db_ref, dm_ref, dg_ref):
        b, t = pl.program_id(0), pl.program_id(1)

        @pl.when(t == 0)
        def _():
            ds_ref[...] = jnp.zeros_like(ds_ref)
            db_ref[...] = jnp.zeros_like(db_ref)
            dm_ref[...] = jnp.zeros_like(dm_ref)
            l_ref[...] = jnp.zeros_like(l_ref)

        @pl.when(jnp.logical_and(b == 0, t == 0))
        def _():
            dg_ref[...] = jnp.zeros_like(dg_ref)

        @pl.when(t > 0)
        def _():
            gate = m_ref[N_MOD - 1:N_MOD, :]
            l, vjp = jax.vjp(lossf, s_ref[...], b_ref[...], gate, g_ref[...], t_ref[...])
            ds, db, dgate, dg, _ = vjp(jnp.ones((1, 1), F32))
            ds_ref[...] = ds
            db_ref[...] = db.astype(db_ref.dtype)
            dg_ref[...] += dg
            l_ref[...] += jnp.broadcast_to(l, l_ref.shape)

            @pl.when(t == 1)
            def _():
                dm_ref[...] = jnp.zeros_like(dm_ref)
                dm_ref[N_MOD - 1:N_MOD, :] = dgate

            @pl.when(t > 1)
            def _():
                dm_ref[N_MOD - 1:N_MOD, :] += dgate

    tok = pl.BlockSpec((None, tb, d), lambda b, t: (b, t, 0))
    modspec = pl.BlockSpec((None, N_MOD, d), _mod_index)
    gspec = pl.BlockSpec((1, d), lambda b, t: (0, 0))
    return pl.pallas_call(
        body,
        name=name,
        grid=(nb, st // tb),
        in_specs=[tok, tok, modspec, gspec, pl.BlockSpec((None, tb, d), lambda b, t: (b, jnp.maximum(t - 1, 0), 0))],
        out_specs=[pl.BlockSpec((None, SUBLANES, 128), lambda b, t: (b, 0, 0)), tok, tok, modspec, gspec],
        out_shape=[jax.ShapeDtypeStruct((nb, SUBLANES, 128), F32), jax.ShapeDtypeStruct(s.shape, F32),
                   jax.ShapeDtypeStruct(s.shape, BF16), jax.ShapeDtypeStruct(mod.shape, F32), jax.ShapeDtypeStruct(g.shape, F32)],
        compiler_params=_cparams(("arbitrary", "arbitrary")),
    )(s, br, mod, g, target)


def adamw(name, w, g, m, v):
    shape = w.shape
    cols = shape[-1] if w.ndim >= 2 else w.size
    rows = w.size // cols
    tr = _pick(rows, (512, 256, 128, 64, 32, 16, 8))

    def body(w_ref, g_ref, m_ref, v_ref, d_ref, nm_ref, nv_ref):
        gv = g_ref[...]
        nm = ADAM_B1 * m_ref[...] + (1.0 - ADAM_B1) * gv
        nv = ADAM_B2 * v_ref[...] + (1.0 - ADAM_B2) * jnp.square(gv)
        m_hat = nm / (1.0 - ADAM_B1 ** ADAM_STEP)
        v_hat = nv / (1.0 - ADAM_B2 ** ADAM_STEP)
        d_ref[...] = -ADAM_LR * (m_hat / (jnp.sqrt(v_hat) + ADAM_EPS) + ADAM_WD * w_ref[...])
        nm_ref[...] = nm
        nv_ref[...] = nv

    spec = pl.BlockSpec((tr, cols), lambda i: (i, 0))
    outs = pl.pallas_call(
        body,
        name=name,
        grid=(rows // tr,),
        in_specs=[spec] * 4,
        out_specs=[spec] * 3,
        out_shape=[jax.ShapeDtypeStruct((rows, cols), F32)] * 3,
        compiler_params=_cparams(("parallel",)),
    )(*(a.reshape(rows, cols) for a in (w, g, m, v)))
    return tuple(o.reshape(shape) for o in outs)


EV_COLS = {"z": (0, 1024), "xbc": (1024, 2560), "dt": (2560, 2592), "gy": (2592, 3616), "u": (3616, 4640)}
OD_COLS = {"q": (0, 768), "ff": (768, 1536), "fb": (1536, 2304), "v": (2304, 3072), "g": (3072, 3840), "u": (3840, 4096)}
EV_OUT_ROWS = ((0, 1024), (1024, 2048))
OD_OUT_ROWS = ((0, 768), (768, 1024))
LANES = 128


def _pad_to_lanes(w):
    n = w.shape[1]
    return w if n % LANES == 0 else jnp.pad(w, ((0, 0), (0, LANES - n % LANES)))


def _layer_weights(l, big):
    j = l // 2
    even = l % 2 == 0
    w_in = big["ev_w_in" if even else "od_w_in"][j]
    w_out = big["ev_w_out" if even else "od_w_out"][j]
    lw = {"in": {}, "out": []}
    for name, (a, b) in (EV_COLS if even else OD_COLS).items():
        w = _pad_to_lanes(w_in[:, a:b])
        lw["in"][name] = (w, w.T)
    for a, b in (EV_OUT_ROWS if even else OD_OUT_ROWS):
        lw["out"].append((w_out[a:b], w_out[a:b].T))
    for name in ("gate", "up", "down"):
        w = big["ffn_w_" + name][l]
        lw[name] = (w, w.T)
    return lw


def _rows2d(a):
    return a.reshape(-1, a.shape[-1])


def _mm3(a, w, name, out_dtype=F32):
    return mm([(_rows2d(a), w)], name, out_dtype).reshape(a.shape[:-1] + (w.shape[1],))


def _wgrad(a, d, name):
    return mm_tn(_rows2d(a), _rows2d(d), name)


def _dgrad(pairs, name, shape3):
    return mm([(_rows2d(d), wt) for d, wt in pairs], name).reshape(shape3[:-1] + (pairs[0][1].shape[1],))


def _dir_dt(dt, d):
    nb, s, _ = dt.shape
    dd = dt[:, :, SSD_HEADS * d:SSD_HEADS * (d + 1)].reshape(nb, s, SSD_GROUPS, SSD_HPG)
    return jnp.transpose(dd, (0, 2, 1, 3)), jnp.transpose(dd, (0, 2, 3, 1))


def _s5_prepare(p, j, tag):
    g_, p_, k_ = S5_GROUPS, S5_STATE, S5_GROUP_CH
    col = lambda t: t.reshape(g_ * p_, 1)
    ins, outs = [], []
    for d in (0, 1):
        i_d = [col(p["s5_lam_re"][j, d]), col(p["s5_lam_im"][j, d]), col(jnp.repeat(p["s5_log_step"][j, d], p_)),
               p["s5_b_re"][j].reshape(g_ * p_, k_), p["s5_b_im"][j].reshape(g_ * p_, k_)]
        ins.append(i_d)
        outs.append(small_fwd(f"{tag}_disc{d}", f_s5_discretize, i_d, [(g_ * p_, 1)] * 2 + [(g_ * p_, k_)] * 2))
    lam_r = jnp.stack([o[0].reshape(1, g_ * p_) for o in outs])
    lam_i = jnp.stack([o[1].reshape(1, g_ * p_) for o in outs])
    bt_r = jnp.stack([s5_tiles_of(o[2].reshape(g_, p_, k_)) for o in outs])
    bt_i = jnp.stack([s5_tiles_of(o[3].reshape(g_, p_, k_)) for o in outs])
    ct_r = jnp.stack([s5_tiles_of(jnp.transpose(p["s5_c_re"][j, d], (0, 2, 1))) for d in (0, 1)])
    ct_i = jnp.stack([s5_tiles_of(jnp.transpose(p["s5_c_im"][j, d], (0, 2, 1))) for d in (0, 1)])
    return ins, (lam_r, lam_i, bt_r, bt_i, ct_r, ct_i)


def _s5_param_grads(ins, grads, tag):
    g_, p_, k_ = S5_GROUPS, S5_STATE, S5_GROUP_CH
    dlr, dli, dbtr, dbti, dctr, dcti = grads
    g_lre, g_lim, g_ls, g_bre, g_bim = [], [], [], 0.0, 0.0
    for d in (0, 1):
        cots = [dlr[d].reshape(g_ * p_, 1), dli[d].reshape(g_ * p_, 1),
                s5_groups_of(dbtr[d], g_, p_, k_).reshape(g_ * p_, k_), s5_groups_of(dbti[d], g_, p_, k_).reshape(g_ * p_, k_)]
        g = small_bwd(f"{tag}_disc_bwd{d}", f_s5_discretize, ins[d], cots)
        g_lre.append(g[0].reshape(g_, p_))
        g_lim.append(g[1].reshape(g_, p_))
        g_ls.append(g[2].reshape(g_, p_).sum(-1))
        g_bre = g_bre + g[3].reshape(g_, p_, k_)
        g_bim = g_bim + g[4].reshape(g_, p_, k_)
    g_cre = jnp.stack([jnp.transpose(s5_groups_of(dctr[d], g_, p_, k_), (0, 2, 1)) for d in (0, 1)])
    g_cim = jnp.stack([jnp.transpose(s5_groups_of(dcti[d], g_, p_, k_), (0, 2, 1)) for d in (0, 1)])
    return jnp.stack(g_lre), jnp.stack(g_lim), jnp.stack(g_ls), g_bre, g_bim, g_cre, g_cim


def _even_mixer_fwd(l, hn, p, lw, s_ctx):
    j = l // 2
    t1 = taps_1d(4, s_ctx, hn.shape[1])
    r = {"hn": hn}
    proj = {n: _mm3(hn, lw["in"][n][0], f"l{l}_proj_{n}") for n in EV_COLS}
    r["z"], r["xbc"], r["gy"], r["u"] = proj["z"], proj["xbc"], proj["gy"], proj["u"]
    r["dtp"] = proj["dt"][:, :, :2 * SSD_HEADS]
    r["xbc_c"] = conv_fwd(f"l{l}_ssd_conv", r["xbc"], p["ssd_conv_w"][j], p["ssd_conv_b"][j][None], t1, "silu")
    r["u_c"] = conv_fwd(f"l{l}_lru_conv", r["u"], p["lru_conv_w"][j], p["lru_conv_b"][j][None], t1, "none")
    r["dt_bias"] = p["ssd_dt_bias"][j].reshape(1, 2 * SSD_HEADS)
    (r["dt"],) = tok_fwd(f"l{l}_dt", f_dt, [r["dtp"]], None, [r["dt_bias"]], [2 * SSD_HEADS], [F32])
    r["ys"], r["hst"], r["dts"], r["alog"] = [], [], [], []
    for d in (0, 1):
        dtc, dtr = _dir_dt(r["dt"], d)
        al = p["ssd_a_log"][j, d].reshape(SSD_GROUPS, SSD_HPG)
        al_r, al_c = al[:, None, :], al[:, :, None]
        y, hst = ssd_fwd(f"l{l}_ssd_fwd{d}", r["xbc_c"], dtc, dtr, al_r, al_c, bool(d), s_ctx)
        r["ys"].append(y)
        r["hst"].append(hst)
        r["dts"].append((dtc, dtr))
        r["alog"].append((al_r, al_c))
    v4 = lambda t: t.reshape(2, LRU_BLOCKS, 1, LRU_BLOCK_W)
    r["lru_p"] = (p["lru_w_a"][j], v4(p["lru_b_a"][j]), p["lru_w_i"][j], v4(p["lru_b_i"][j]), v4(p["lru_lam"][j]))
    r["h_sum"] = lru_fwd(f"l{l}_lru_fwd", r["u_c"], *r["lru_p"], s_ctx)
    r["xs"] = r["xbc_c"][:, :, :SSD_HEADS * SSD_HEAD_DIM]
    r["fin_p"] = [jnp.repeat(p["ssd_d"][j], SSD_HEAD_DIM)[None], p["ssd_norm_g"][j][None]]
    r["fin_in"] = [r["ys"][0], r["ys"][1], r["xs"], r["z"], r["h_sum"], r["gy"]]
    r["o"] = tok_fwd(f"l{l}_even_finish", f_even_finish, r["fin_in"], None, r["fin_p"], [1024, 1024], [BF16, BF16])
    return r


def _even_mixer_bwd(l, r, dox, p, lw, s_ctx, grads):
    j = l // 2
    shape3 = dox.shape
    t1 = taps_1d(4, s_ctx, shape3[1])
    grads["ev_w_out"][j] = jnp.concatenate([_wgrad(o, dox, f"l{l}_dwout{i}") for i, o in enumerate(r["o"])], axis=0)
    do = [_dgrad([(dox, lw["out"][i][1])], f"l{l}_dout{i}", shape3) for i in range(2)]
    (dy, _, dxs, dz, dh_sum, dgy), _, (dd_exp, grads["ssd_norm_g"][j]) = tok_bwd(
        f"l{l}_even_finish_bwd", f_even_finish, r["fin_in"], None, r["fin_p"], do, [F32, F32, F32, BF16, F32, BF16])
    grads["ssd_d"][j] = dd_exp.reshape(SSD_HEADS, SSD_HEAD_DIM).sum(-1)
    du_c, dwa, dba, dwi, dbi, dlam = lru_bwd(f"l{l}_lru_bwd", r["u_c"], *r["lru_p"], dh_sum, s_ctx)
    grads["lru_w_a"][j], grads["lru_w_i"][j] = dwa, dwi
    v2 = lambda t: t.reshape(2, LRU_BLOCKS * LRU_BLOCK_W)
    grads["lru_b_a"][j], grads["lru_b_i"][j], grads["lru_lam"][j] = v2(dba), v2(dbi), v2(dlam)
    dx_sum, dbm_sum, dcm_sum, ddts, dalog = dxs, 0.0, 0.0, [], []
    for d in (0, 1):
        dx, dbm, dcm, ddtc, ddtr, dar, dac = ssd_bwd(
            f"l{l}_ssd_bwd{d}", r["xbc_c"], *r["dts"][d], *r["alog"][d], r["hst"][d], dy, bool(d), s_ctx)
        dx_sum, dbm_sum, dcm_sum = dx_sum + dx, dbm_sum + dbm, dcm_sum + dcm
        ddts.append((jnp.transpose(ddtc, (0, 2, 1, 3)) + jnp.transpose(ddtr, (0, 3, 1, 2))).reshape(shape3[0], shape3[1], SSD_HEADS))
        dalog.append((dar.sum(0)[:, 0, :] + dac.sum(0)[:, :, 0]).reshape(SSD_HEADS))
    grads["ssd_a_log"][j] = jnp.stack(dalog)
    dxbc_c = jnp.concatenate([dx_sum, dbm_sum, dcm_sum], axis=-1)
    (ddtp,), _, (ddt_bias,) = tok_bwd(f"l{l}_dt_bwd", f_dt, [r["dtp"]], None, [r["dt_bias"]], [jnp.concatenate(ddts, axis=-1)], [F32])
    grads["ssd_dt_bias"][j] = ddt_bias.reshape(2, SSD_HEADS)
    dxbc, grads["ssd_conv_w"][j], dcb = conv_bwd(f"l{l}_ssd_conv_bwd", r["xbc"], p["ssd_conv_w"][j], p["ssd_conv_b"][j][None], dxbc_c, t1, "silu", dx_dtype=BF16)
    du, grads["lru_conv_w"][j], dlb = conv_bwd(f"l{l}_lru_conv_bwd", r["u"], p["lru_conv_w"][j], p["lru_conv_b"][j][None], du_c, t1, "none", dx_dtype=BF16)
    grads["ssd_conv_b"][j], grads["lru_conv_b"][j] = dcb[0], dlb[0]
    dproj = {"z": dz, "xbc": dxbc, "dt": _pad_to_lanes(_rows2d(ddtp)).reshape(shape3[:2] + (LANES,)), "gy": dgy, "u": du}
    grads["ev_w_in"][j] = jnp.concatenate(
        [_wgrad(r["hn"], dproj[n], f"l{l}_dwin_{n}")[:, :b - a] for n, (a, b) in EV_COLS.items()], axis=1)
    return _dgrad([(dproj[n], lw["in"][n][1]) for n in EV_COLS], f"l{l}_dhn", shape3)


def _odd_mixer_fwd(l, hn, p, lw, lb_row, s_ctx):
    j = l // 2
    r = {"hn": hn}
    proj = {n: _mm3(hn, lw["in"][n][0], f"l{l}_proj_{n}") for n in OD_COLS}
    r["v"], r["g"], r["u"] = proj["v"], proj["g"], proj["u"]
    r["prep_in"] = [proj["q"], proj["ff"], proj["fb"]]
    r["lb"] = lb_row
    r["prep"] = tok_fwd(f"l{l}_odd_prep", f_odd_prep, r["prep_in"], None, [lb_row], [HG_W] * 5, [F32] * 5)
    qs = r["prep"][0]
    r["os"], r["hst"] = [], []
    for d in (0, 1):
        o, hst = hg_fwd(f"l{l}_hg_fwd{d}", qs, r["prep"][1 + 2 * d], r["prep"][2 + 2 * d], r["v"], bool(d), s_ctx)
        r["os"].append(o)
        r["hst"].append(hst)
    r["s5_ins"], r["s5_p"] = _s5_prepare(p, j, f"l{l}_s5")
    r["s5y"] = s5_fwd(f"l{l}_s5_fwd", r["u"], *r["s5_p"], s_ctx)
    r["fin_p"] = [p["hg_norm_g"][j].reshape(1, HG_W), p["s5_d"][j][None], p["s5_glu_w"][j], p["s5_glu_b"][j][None]]
    r["fin_in"] = [r["os"][0], r["os"][1], r["g"], r["s5y"], r["u"]]
    r["o"] = tok_fwd(f"l{l}_odd_finish", f_odd_finish, r["fin_in"], None, r["fin_p"], [HG_W, S5_W], [BF16, BF16])
    return r


def _odd_mixer_bwd(l, r, dox, p, lw, s_ctx, grads):
    j = l // 2
    shape3 = dox.shape
    grads["od_w_out"][j] = jnp.concatenate([_wgrad(o, dox, f"l{l}_dwout{i}") for i, o in enumerate(r["o"])], axis=0)
    do = [_dgrad([(dox, lw["out"][i][1])], f"l{l}_dout{i}", shape3) for i in range(2)]
    (do_hg, _, dg, ds5y, du_fin), _, (dng, grads["s5_d"][j], grads["s5_glu_w"][j], dglu_b) = tok_bwd(
        f"l{l}_odd_finish_bwd", f_odd_finish, r["fin_in"], None, r["fin_p"], do, [F32, F32, BF16, F32, F32])
    grads["hg_norm_g"][j] = dng.reshape(HG_HEADS, HG_DK)
    grads["s5_d"][j], grads["s5_glu_b"][j] = grads["s5_d"][j][0], dglu_b[0]
    s5g = s5_bwd(f"l{l}_s5_bwd", r["u"], *r["s5_p"], ds5y, s_ctx)
    du = s5g[0] + du_fin
    (grads["s5_lam_re"][j], grads["s5_lam_im"][j], grads["s5_log_step"][j], grads["s5_b_re"][j], grads["s5_b_im"][j],
     grads["s5_c_re"][j], grads["s5_c_im"][j]) = _s5_param_grads(r["s5_ins"], s5g[1:], f"l{l}_s5")
    qs = r["prep"][0]
    dqs, dv, dprep = 0.0, 0.0, [None] * 5
    for d in (0, 1):
        dq, dk, dlf, dvd = hg_bwd(f"l{l}_hg_bwd{d}", qs, r["prep"][1 + 2 * d], r["prep"][2 + 2 * d], r["v"], r["hst"][d], do_hg, bool(d), s_ctx)
        dqs, dv = dqs + dq, dv + dvd
        dprep[1 + 2 * d], dprep[2 + 2 * d] = dk, dlf
    dprep[0] = dqs
    (dq_, dff, dfb), _, (dlb,) = tok_bwd(f"l{l}_odd_prep_bwd", f_odd_prep, r["prep_in"], None, [r["lb"]], dprep, [BF16] * 3)
    dproj = {"q": dq_, "ff": dff, "fb": dfb, "v": dv, "g": dg, "u": du}
    grads["od_w_in"][j] = jnp.concatenate([_wgrad(r["hn"], dproj[n], f"l{l}_dwin_{n}") for n in OD_COLS], axis=1)
    return _dgrad([(dproj[n], lw["in"][n][1]) for n in OD_COLS], f"l{l}_dhn", shape3), dlb


def _ffn_fwd(l, fn, p, lw, s_ctx):
    r = {"fn": fn}
    tg = taps_grid(s_ctx, fn.shape[1], GRID_W)
    r["a"] = _mm3(fn, lw["gate"][0], f"l{l}_ffn_gate")
    r["up"] = _mm3(fn, lw["up"][0], f"l{l}_ffn_up")
    r["cw"], r["cb"] = p["ffn_conv_w"][l].reshape(9, D_FF), p["ffn_conv_b"][l][None]
    r["act"] = conv_fwd(f"l{l}_ffn_conv", r["a"], r["cw"], r["cb"], tg, "silu_mul", mul=r["up"], out_dtype=BF16)
    return r, _mm3(r["act"], lw["down"][0], f"l{l}_ffn_down")


def _ffn_bwd(l, r, dfo, lw, s_ctx, grads):
    shape3 = dfo.shape
    tg = taps_grid(s_ctx, shape3[1], GRID_W)
    grads["ffn_w_down"][l] = _wgrad(r["act"], dfo, f"l{l}_dwdown")
    dact = _dgrad([(dfo, lw["down"][1])], f"l{l}_dact", shape3)
    da, dcw, dcb, dup = conv_bwd(f"l{l}_ffn_conv_bwd", r["a"], r["cw"], r["cb"], dact, tg, "silu_mul", mul=r["up"], dx_dtype=BF16)
    grads["ffn_conv_w"][l], grads["ffn_conv_b"][l] = dcw.reshape(3, 3, D_FF), dcb[0]
    grads["ffn_w_gate"][l] = _wgrad(r["fn"], da, f"l{l}_dwgate")
    grads["ffn_w_up"][l] = _wgrad(r["fn"], dup, f"l{l}_dwup")
    return _dgrad([(da, lw["gate"][1]), (dup, lw["up"][1])], f"l{l}_dfn", shape3)


BIG_WEIGHTS = ("ev_w_in", "ev_w_out", "od_w_in", "od_w_out", "ffn_w_gate", "ffn_w_up", "ffn_w_down")
PER_LAYER = {"norm_mix_g": DEPTH, "norm_ffn_g": DEPTH, "ffn_w_gate": DEPTH, "ffn_w_up": DEPTH, "ffn_conv_w": DEPTH,
             "ffn_conv_b": DEPTH, "ffn_w_down": DEPTH}


def local_step(x, ctx, target, modtabs, p, big, s_ctx=CTX_LEN):
    d_model = x.shape[-1]
    s0 = jnp.concatenate([ctx, x], axis=1)
    lws = [_layer_weights(l, big) for l in range(DEPTH)]
    shapes = {n: v.shape for n, v in {**p, **big}.items()}
    grads = {n: [None] * PER_LAYER.get(n, DEPTH // 2) for n in shapes if n not in ("c_ctx", "w_mod", "b_mod", "final_norm_g", "hg_lb_logits")}
    (lbs,) = small_fwd("lower_bounds", f_lower_bounds, [p["hg_lb_logits"]], [p["hg_lb_logits"].shape])
    tab_a = [modtabs[0]] + [modtabs[l].at[:, N_MOD - 1].set(modtabs[l - 1][:, N_MOD - 1]) for l in range(1, DEPTH)]
    res = []
    s, br = s0, None
    for l in range(DEPTH):
        r = {}
        g_mix, g_ffn = p["norm_mix_g"][l][None], p["norm_ffn_g"][l][None]
        if l == 0:
            (hn,) = tok_fwd("l0_norm", f_norm(0, 1), [s], tab_a[0], [g_mix], [d_model], [BF16])
            r["a_in"] = [s]
        else:
            r["a_in"] = [s, br]
            s, hn = tok_fwd(f"l{l}_resnorm_a", f_resnorm(5, 0, 1), r["a_in"], tab_a[l], [g_mix], [d_model] * 2, [F32, BF16])
        if l % 2 == 0:
            r["mix"] = _even_mixer_fwd(l, hn, p, lws[l], s_ctx)
        else:
            r["mix"] = _odd_mixer_fwd(l, hn, p, lws[l], lbs[l:l + 1], s_ctx)
        ox = mm([(_rows2d(o), w) for o, (w, _) in zip(r["mix"]["o"], lws[l]["out"])], f"l{l}_mix_out").reshape(s.shape)
        r["b_in"] = [s, ox]
        s, fn = tok_fwd(f"l{l}_resnorm_b", f_resnorm(2, 3, 4), r["b_in"], modtabs[l], [g_ffn], [d_model] * 2, [F32, BF16])
        r["ffn"], br = _ffn_fwd(l, fn, p, lws[l], s_ctx)
        res.append(r)

    loss_blk, ds, dbr, dtab_f, dfinal_g = final_loss("final_loss", s, br, modtabs[DEPTH - 1], p["final_norm_g"][None], target, s_ctx)
    grads["final_norm_g"] = dfinal_g[0]
    dmod = [None] * DEPTH
    dtab_next = dtab_f
    dlb = jnp.zeros_like(lbs)
    for l in reversed(range(DEPTH)):
        r = res[l]
        g_mix, g_ffn = p["norm_mix_g"][l][None], p["norm_ffn_g"][l][None]
        dfn = _ffn_bwd(l, r["ffn"], dbr, lws[l], s_ctx, grads)
        (ds, dox), dtab_b, (grads["norm_ffn_g"][l],) = tok_bwd(
            f"l{l}_resnorm_b_bwd", f_resnorm(2, 3, 4), r["b_in"], modtabs[l], [g_ffn], [ds, dfn], [F32, BF16])
        if l % 2 == 0:
            dhn = _even_mixer_bwd(l, r["mix"], dox, p, lws[l], s_ctx, grads)
        else:
            dhn, dlb_l = _odd_mixer_bwd(l, r["mix"], dox, p, lws[l], s_ctx, grads)
            dlb = dlb.at[l:l + 1].set(dlb_l)
        if l == 0:
            (ds,), dtab_a, (dg,) = tok_bwd("l0_norm_bwd", f_norm_keep(0, 1), r["a_in"], tab_a[0], [g_mix], [ds, dhn], [F32])
        else:
            (ds, dbr), dtab_a, (dg,) = tok_bwd(
                f"l{l}_resnorm_a_bwd", f_resnorm(5, 0, 1), r["a_in"], tab_a[l], [g_mix], [ds, dhn], [F32, BF16])
        grads["norm_mix_g"][l] = dg
        dmod[l] = (dtab_a.at[:, N_MOD - 1].set(0.0) + dtab_b).at[:, N_MOD - 1].set(dtab_next[:, N_MOD - 1])
        dtab_next = dtab_a
    (grads["hg_lb_logits"],) = small_bwd("lower_bounds_bwd", f_lower_bounds, [p["hg_lb_logits"]], [dlb])
    out = {}
    for n, g in grads.items():
        if isinstance(g, list):
            g = jnp.stack([t.reshape(shapes[n][1:]) for t in g])
        out[n] = g.reshape(shapes[n])
    return loss_blk[:, 0, 0], ds[:, s_ctx:], dmod, out


WEIGHT_NAMES = (
    "c_ctx", "w_mod", "b_mod", "norm_mix_g", "norm_ffn_g", "final_norm_g", "ev_w_in", "ev_w_out", "ssd_conv_w",
    "ssd_conv_b", "ssd_dt_bias", "ssd_a_log", "ssd_d", "ssd_norm_g", "lru_conv_w", "lru_conv_b", "lru_w_a", "lru_b_a",
    "lru_w_i", "lru_b_i", "lru_lam", "od_w_in", "od_w_out", "hg_lb_logits", "hg_norm_g", "s5_lam_re", "s5_lam_im",
    "s5_log_step", "s5_b_re", "s5_b_im", "s5_c_re", "s5_c_im", "s5_d", "s5_glu_w", "s5_glu_b", "ffn_w_gate", "ffn_w_up",
    "ffn_conv_w", "ffn_conv_b", "ffn_w_down")
INPUT_NAMES = ("x", "c", "ctx") + WEIGHT_NAMES + ("loss_target",) + tuple("m_" + n for n in WEIGHT_NAMES) + tuple("v_" + n for n in WEIGHT_NAMES)
SHARD_AXIS = {"w_mod": 2, "ev_w_in": 2, "ev_w_out": 1, "ssd_conv_w": 2, "lru_conv_w": 2, "lru_b_a": 2, "lru_b_i": 2,
              "lru_lam": 2, "od_w_in": 2, "od_w_out": 1, "s5_d": 1, "s5_glu_w": 1, "s5_glu_b": 1, "ffn_w_gate": 2,
              "ffn_w_up": 2, "ffn_conv_w": 3, "ffn_w_down": 1}
SMALL_SHARDED = tuple(n for n in WEIGHT_NAMES if n in SHARD_AXIS and n not in BIG_WEIGHTS and n != "w_mod")
REPLICATED_LOCAL = tuple(n for n in WEIGHT_NAMES if n not in SHARD_AXIS and n not in ("c_ctx", "b_mod"))
PACK_WIDTH = 1024
MOD_ROWS = 48
CTX_ROW = 32


def _unshard(g8, axis):
    moved = jnp.moveaxis(g8, 0, axis)
    shp = moved.shape
    return moved.reshape(shp[:axis] + (shp[axis] * shp[axis + 1],) + shp[axis + 2:])


def _to_shards(full, axis):
    shp = full.shape
    return jnp.moveaxis(full.reshape(shp[:axis] + (N_DEV, shp[axis] // N_DEV) + shp[axis + 1:]), axis, 0)


def _pack(arrs, dtype, lead=()):
    flat = jnp.concatenate([a.astype(dtype).reshape(lead + (-1,)) for a in arrs], axis=-1)
    n = flat.shape[-1]
    unit = 16 * PACK_WIDTH
    padded = -(-n // unit) * unit
    flat = jnp.pad(flat, [(0, 0)] * len(lead) + [(0, padded - n)])
    return flat.reshape(lead + (padded // PACK_WIDTH, PACK_WIDTH))


def _unpack(packed, shapes, lead=()):
    flat = packed.reshape(lead + (-1,))
    out, off = [], 0
    for shp in shapes:
        n = math.prod(shp)
        out.append(flat[..., off:off + n].reshape(lead + tuple(shp)))
        off += n
    return out


def _my_block(full, axis, me):
    loc = full.shape[axis] // N_DEV
    return lax.dynamic_slice_in_dim(full, me * loc, loc, axis)


def kernel(*args):
    a = dict(zip(INPUT_NAMES, args))
    px, py, pc = _my_pos()
    me = 4 * px + 2 * py + pc
    nb = a["x"].shape[0]

    small_names = ("c",) + SMALL_SHARDED
    *big8, small8 = all_gather([a[n].astype(BF16) for n in BIG_WEIGHTS] + [_pack([a[n] for n in small_names], F32)],
                               "gather_weights")
    big = {n: _unshard(g, SHARD_AXIS[n]) for n, g in zip(BIG_WEIGHTS, big8)}
    small = dict(zip(small_names, _unpack(small8, [a[n].shape for n in small_names], (N_DEV,))))
    p = {n: a[n] for n in WEIGHT_NAMES if n not in SHARD_AXIS}
    for n in SMALL_SHARDED:
        p[n] = _unshard(small[n], SHARD_AXIS[n])
    c_all = small["c"].reshape(N_DEV * nb, D_MODEL)

    rows = jnp.concatenate([c_all, a["c_ctx"][None], jnp.zeros((MOD_ROWS - CTX_ROW - 1, D_MODEL), F32)], axis=0)
    (srows,) = small_fwd("mod_silu", f_silu, [rows], [rows.shape])
    wmod2d = jnp.transpose(a["w_mod"], (1, 0, 2)).reshape(D_MODEL, -1).astype(BF16)
    cols = a["w_mod"].shape[2]
    mod_loc = mm([(srows, wmod2d)], "mod_proj")
    mod8 = all_gather([mod_loc], "gather_mod")[0].reshape(N_DEV, MOD_ROWS, DEPTH, cols)
    mod_all = jnp.transpose(mod8, (2, 1, 0, 3)).reshape(DEPTH, MOD_ROWS, N_DEV * cols) + a["b_mod"][:, None, :]
    modtabs = []
    for l in range(DEPTH):
        mine = lax.dynamic_slice_in_dim(mod_all[l], me * nb, nb, 0).reshape(nb, N_MOD, D_MODEL)
        ctx_row = jnp.broadcast_to(mod_all[l, CTX_ROW].reshape(1, N_MOD, D_MODEL), (nb, N_MOD, D_MODEL))
        modtabs.append(jnp.stack([ctx_row, mine], axis=1).reshape(2 * nb, N_MOD, D_MODEL))

    loss_b, grad_x, dmod, grads = local_step(a["x"], a["ctx"], a["loss_target"], modtabs, p, big)

    dm = jnp.stack([t.reshape(nb, 2, N_MOD * D_MODEL) for t in dmod])
    dloc = jnp.concatenate([dm[:, :, 1], jnp.sum(dm[:, :, 0], axis=1, keepdims=True),
                            jnp.zeros((DEPTH, SUBLANES - nb - 1, N_MOD * D_MODEL), F32)], axis=1)
    d8 = all_gather([dloc.reshape(DEPTH * SUBLANES, -1)], "gather_dmod")[0].reshape(N_DEV, DEPTH, SUBLANES, -1)
    d_rows = jnp.transpose(d8[:, :, :nb], (1, 0, 2, 3)).reshape(DEPTH, N_DEV * nb, -1)
    d_ctx = jnp.sum(d8[:, :, nb], axis=0)[:, None]
    d_full = jnp.concatenate([d_rows, d_ctx, jnp.zeros((DEPTH, MOD_ROWS - CTX_ROW - 1, N_MOD * D_MODEL), F32)], axis=1)
    grads["b_mod"] = jnp.sum(d_full, axis=1)
    d_cols = jnp.transpose(_my_block(d_full, 2, me), (1, 0, 2)).reshape(MOD_ROWS, DEPTH * cols)
    g_wmod = mm([(srows.T, d_cols)], "mod_dw")
    g_wmod_local = jnp.transpose(g_wmod.reshape(D_MODEL, DEPTH, cols), (1, 0, 2))
    d_srows_part = mm([(d_cols[CTX_ROW:CTX_ROW + SUBLANES], wmod2d.T)], "mod_dctx")[0]

    reduce_names = REPLICATED_LOCAL + SMALL_SHARDED
    parts = [jnp.sum(loss_b).reshape(1), d_srows_part] + [grads[n] for n in reduce_names]
    (part8,) = all_gather([_pack(parts, F32)], "gather_small_grads")
    total = sum_slots(part8, "sum_small_grads")
    totals = _unpack(total, [(1,), (D_MODEL,)] + [grads[n].shape for n in reduce_names])
    loss = totals[0][0]
    d_srows = jnp.zeros_like(rows).at[CTX_ROW].set(totals[1])
    (d_rows_in,) = small_bwd("mod_silu_bwd", f_silu, [rows], [d_srows])
    g_local = {"c_ctx": d_rows_in[CTX_ROW], "b_mod": grads["b_mod"], "w_mod": g_wmod_local}
    for n, t in zip(reduce_names, totals[2:]):
        g_local[n] = _my_block(t, SHARD_AXIS[n], me) if n in SHARD_AXIS else t

    own = [_to_shards(grads[n], SHARD_AXIS[n]).astype(BF16).reshape(N_CHIPS, 2, -1, a[n].shape[-1]) for n in BIG_WEIGHTS]
    from_sibling = sibling_swap(own, "exchange_sibling_grads")
    chip_sums = [pair_sum(o, s, "pair_sum_" + n) for n, o, s in zip(BIG_WEIGHTS, own, from_sibling)]
    got = chip_exchange(chip_sums, "exchange_chip_grads")
    for n, t in zip(BIG_WEIGHTS, got):
        g_local[n] = sum_slots(t, "sum_" + n).reshape(a[n].shape)

    deltas, new_m, new_v = [], [], []
    for n in WEIGHT_NAMES:
        d, m, v = adamw("adamw_" + n, a[n], g_local[n], a["m_" + n], a["v_" + n])
        deltas.append(d)
        new_m.append(m)
        new_v.append(v)
    return (loss, grad_x, *[g_local[n] for n in WEIGHT_NAMES], *deltas, *new_m, *new_v)
```

```python
import functools
import math

import jax
import jax.numpy as jnp
from jax import lax
from jax.experimental import pallas as pl
from jax.experimental.pallas import tpu as pltpu

F32 = jnp.float32
BF16 = jnp.bfloat16

D_MODEL = 1024
DEPTH = 4
CTX_LEN = 256
SEQ = 2048
S_TOT = CTX_LEN + SEQ
GRID_W = 64
N_MOD = 6
RMS_EPS = 1e-6
N_DEV = 8

SSD_HEADS = 16
SSD_HEAD_DIM = 64
SSD_GROUPS = 2
SSD_HPG = 8
SSD_STATE = 128
SSD_CHUNK = 128
SSD_W = SSD_HEADS * SSD_HEAD_DIM
LRU_BLOCKS = 8
LRU_BLOCK_W = 128
LRU_C = 8.0
HG_W = 768
HG_HEADS = 6
HG_DK = 128
HG_CHUNK = 32
S5_W = 256
S5_GROUPS = 16
S5_GROUP_CH = 16
S5_STATE = 64
D_FF = 2816

ADAM_LR = 0.001
ADAM_B1 = 0.9
ADAM_B2 = 0.999
ADAM_EPS = 1e-08
ADAM_WD = 0.01
ADAM_STEP = 10

TOK_BLOCK = CTX_LEN
SUBLANES = 8
VMEM_LIMIT_BYTES = 56 * 1024 * 1024
MM_BLOCK_BYTES = 8 * 1024 * 1024
MM_TILES = (1408, 1024, 768, 704, 512, 384, 352, 256, 128, 64, 48, 40, 32, 16, 8)
MM_ROW_TILES = (2304, 2048, 1152, 1024, 512, 256, 128, 64, 48, 32, 16, 8)
LANE_TILE = 128


def _cparams(sem=None):
    kw = dict(vmem_limit_bytes=VMEM_LIMIT_BYTES)
    if sem is not None:
        kw["dimension_semantics"] = sem
    return pltpu.CompilerParams(**kw)


def _pick(n, cands):
    for c in cands:
        if n % c == 0:
            return c
    return n


def mm(pairs, name, out_dtype=F32):
    m = pairs[0][0].shape[0]
    n = pairs[0][1].shape[1]
    kdims = [a.shape[1] for a, _ in pairs]
    ktile = None
    if len(pairs) == 1 and kdims[0] > 4096:
        ktile = _pick(kdims[0], (2304, 2048, 1024))
    nk = kdims[0] // ktile if ktile else 1
    col_bytes = sum((ktile or w.shape[0]) * w.dtype.itemsize for _, w in pairs)
    tn = _pick(n, tuple(c for c in MM_TILES if c % LANE_TILE == 0 and c * col_bytes <= MM_BLOCK_BYTES))
    row_bytes = max(sum((ktile or a.shape[1]) * a.dtype.itemsize for a, _ in pairs), tn * 4)
    tm = _pick(m, tuple(c for c in MM_TILES if c * row_bytes <= MM_BLOCK_BYTES))
    npairs = len(pairs)
    if nk > 1:
        assert out_dtype == F32

    def body(*refs):
        o_ref = refs[2 * npairs]
        acc = None
        for i in range(npairs):
            a = refs[2 * i][...].astype(BF16)
            w = refs[2 * i + 1][...].astype(BF16)
            p = jnp.dot(a, w, preferred_element_type=F32)
            acc = p if acc is None else acc + p
        if nk == 1:
            o_ref[...] = acc.astype(out_dtype)
        else:
            k = pl.program_id(2)

            @pl.when(k == 0)
            def _():
                o_ref[...] = acc

            @pl.when(k > 0)
            def _():
                o_ref[...] += acc

    in_specs = []
    args = []
    for a, w in pairs:
        kk = a.shape[1]
        assert w.shape == (kk, n) and a.shape[0] == m, (a.shape, w.shape)
        tk = ktile if ktile else kk
        in_specs.append(pl.BlockSpec((tm, tk), lambda i, j, k: (i, k)))
        in_specs.append(pl.BlockSpec((tk, tn), lambda i, j, k: (k, j)))
        args += [a, w]
    return pl.pallas_call(
        body,
        name=name,
        grid=(m // tm, n // tn, nk),
        in_specs=in_specs,
        out_specs=pl.BlockSpec((tm, tn), lambda i, j, k: (i, j)),
        out_shape=jax.ShapeDtypeStruct((m, n), out_dtype),
        compiler_params=_cparams(("parallel", "parallel", "arbitrary")),
    )(*args)


def mm_tn(a, d, name):
    r, k = a.shape
    n = d.shape[1]
    tr = _pick(r, MM_ROW_TILES)
    lane_ok = lambda c, full: c % LANE_TILE == 0 or c == full
    tk = _pick(k, tuple(c for c in MM_TILES if lane_ok(c, k) and c * tr * a.dtype.itemsize <= MM_BLOCK_BYTES))
    tn = _pick(n, tuple(c for c in MM_TILES if lane_ok(c, n) and c * tr * d.dtype.itemsize <= MM_BLOCK_BYTES
                        and c * tk * 4 <= MM_BLOCK_BYTES))

    def body(a_ref, d_ref, o_ref):
        acc = lax.dot_general(a_ref[...].astype(BF16), d_ref[...].astype(BF16), (((0,), (0,)), ((), ())),
                              preferred_element_type=F32)
        step = pl.program_id(2)

        @pl.when(step == 0)
        def _():
            o_ref[...] = acc

        @pl.when(step > 0)
        def _():
            o_ref[...] += acc

    return pl.pallas_call(
        body,
        name=name,
        grid=(k // tk, n // tn, r // tr),
        in_specs=[pl.BlockSpec((tr, tk), lambda i, j, s: (s, i)), pl.BlockSpec((tr, tn), lambda i, j, s: (s, j))],
        out_specs=pl.BlockSpec((tk, tn), lambda i, j, s: (i, j)),
        out_shape=jax.ShapeDtypeStruct((k, n), F32),
        compiler_params=_cparams(("parallel", "parallel", "arbitrary")),
    )(a, d)


def _mod_index(b, t):
    return (2 * b + jnp.minimum(t, 1), 0, 0)


def tok_fwd(name, f, toks, mod, params, out_widths, out_dtypes):
    nb, s, _ = toks[0].shape
    nt, nm, npar = len(toks), int(mod is not None), len(params)

    def body(*refs):
        ins, outs = refs[: nt + nm + npar], refs[nt + nm + npar:]
        tv = [r[...].astype(F32) for r in ins[:nt]]
        mv = [ins[nt][k:k + 1, :] for k in range(N_MOD)] if nm else None
        pv = [r[...] for r in ins[nt + nm:]]
        for o, r in zip(outs, f(tv, mv, pv)):
            o[...] = r.astype(o.dtype)

    in_specs = [pl.BlockSpec((None, TOK_BLOCK, t.shape[2]), lambda b, t: (b, t, 0)) for t in toks]
    if nm:
        in_specs.append(pl.BlockSpec((None, N_MOD, mod.shape[2]), _mod_index))
    in_specs += [pl.BlockSpec(p.shape, lambda b, t, nd=p.ndim: (0,) * nd) for p in params]
    return pl.pallas_call(
        body,
        name=name,
        grid=(nb, s // TOK_BLOCK),
        in_specs=in_specs,
        out_specs=[pl.BlockSpec((None, TOK_BLOCK, w), lambda b, t: (b, t, 0)) for w in out_widths],
        out_shape=[jax.ShapeDtypeStruct((nb, s, w), dt) for w, dt in zip(out_widths, out_dtypes)],
        compiler_params=_cparams(("parallel", "parallel")),
    )(*toks, *([mod] if nm else []), *params)


def tok_bwd(name, f, toks, mod, params, cots, dtok_dtypes):
    nb, s, _ = toks[0].shape
    nt, nm, npar, nc = len(toks), int(mod is not None), len(params), len(cots)

    def body(*refs):
        n_in = nt + nm + npar + nc
        ins, outs = refs[:n_in], refs[n_in:]
        b, t = pl.program_id(0), pl.program_id(1)
        tv = [r[...].astype(F32) for r in ins[:nt]]
        mv = [ins[nt][k:k + 1, :] for k in range(N_MOD)] if nm else None
        pv = [r[...] for r in ins[nt + nm: nt + nm + npar]]
        cv = [r[...].astype(F32) for r in ins[nt + nm + npar:]]
        _, vjp = jax.vjp(f, tv, mv, pv)
        dtv, dmv, dpv = vjp(cv)
        for o, r in zip(outs[:nt], dtv):
            o[...] = r.astype(o.dtype)
        if nm:
            dm_ref = outs[nt]

            @pl.when(t <= 1)
            def _():
                for k in range(N_MOD):
                    dm_ref[k:k + 1, :] = dmv[k]

            @pl.when(t > 1)
            def _():
                for k in range(N_MOD):
                    dm_ref[k:k + 1, :] += dmv[k]

        first = jnp.logical_and(b == 0, t == 0)
        for o, r in zip(outs[nt + nm:], dpv):
            @pl.when(first)
            def _(o=o, r=r):
                o[...] = r

            @pl.when(jnp.logical_not(first))
            def _(o=o, r=r):
                o[...] += r

    tok_spec = lambda w: pl.BlockSpec((None, TOK_BLOCK, w), lambda b, t: (b, t, 0))
    in_specs = [tok_spec(t.shape[2]) for t in toks]
    if nm:
        in_specs.append(pl.BlockSpec((None, N_MOD, mod.shape[2]), _mod_index))
    in_specs += [pl.BlockSpec(p.shape, lambda b, t, nd=p.ndim: (0,) * nd) for p in params]
    in_specs += [tok_spec(c.shape[2]) for c in cots]
    out_specs = [tok_spec(t.shape[2]) for t in toks]
    out_shape = [jax.ShapeDtypeStruct(t.shape, dt) for t, dt in zip(toks, dtok_dtypes)]
    if nm:
        out_specs.append(pl.BlockSpec((None, N_MOD, mod.shape[2]), _mod_index))
        out_shape.append(jax.ShapeDtypeStruct(mod.shape, F32))
    out_specs += [pl.BlockSpec(p.shape, lambda b, t, nd=p.ndim: (0,) * nd) for p in params]
    out_shape += [jax.ShapeDtypeStruct(p.shape, F32) for p in params]
    res = pl.pallas_call(
        body,
        name=name,
        grid=(nb, s // TOK_BLOCK),
        in_specs=in_specs,
        out_specs=out_specs,
        out_shape=out_shape,
        compiler_params=_cparams(("arbitrary", "arbitrary")),
    )(*toks, *([mod] if nm else []), *params, *cots)
    return res[:nt], (res[nt] if nm else None), res[nt + nm:]


def _rms(x, g):
    return x * lax.rsqrt(jnp.mean(x * x, axis=-1, keepdims=True) + RMS_EPS) * g


def _silu(x):
    return x * jax.nn.sigmoid(x)


def f_norm(shift_row, scale_row):
    def f(tv, mv, pv):
        return [_rms(tv[0], pv[0]) * (1.0 + mv[scale_row]) + mv[shift_row]]
    return f


def f_resnorm(gate_row, shift_row, scale_row):
    def f(tv, mv, pv):
        s = tv[0] + mv[gate_row] * tv[1]
        return [s, _rms(s, pv[0]) * (1.0 + mv[scale_row]) + mv[shift_row]]
    return f


_ANY = pl.BlockSpec(memory_space=pl.ANY)
_MESH = pl.DeviceIdType.MESH


def _my_pos():
    return lax.axis_index("x"), lax.axis_index("y"), lax.axis_index("c")


def _slot_of(pos):
    return 4 * pos[0] + 2 * pos[1] + pos[2]


def all_gather(xs, name):
    n = len(xs)

    def body(*refs):
        x_refs, out_refs = refs[:n], refs[n:2 * n]
        send_sems, recv_sems, local_sems = refs[2 * n:]
        px, py, pc = _my_pos()
        me, sibling = (px, py, pc), (px, py, 1 - pc)
        chips = [(1 - px, py), (px, 1 - py), (1 - px, 1 - py)]

        def copy(a, k, block, to, from_input=False):
            slot = out_refs[a].at[_slot_of(block)]
            return pltpu.make_async_remote_copy(
                src_ref=x_refs[a] if from_input else slot, dst_ref=slot,
                send_sem=send_sems.at[a, k], recv_sem=recv_sems.at[a, k],
                device_id=to, device_id_type=_MESH)

        mine = [pltpu.make_async_copy(x_refs[a], out_refs[a].at[_slot_of(me)], local_sems.at[a]) for a in range(n)]
        for cp in mine:
            cp.start()
        first = [copy(a, 0, me, sibling, True) for a in range(n)]
        first += [copy(a, 1 + j, me, (*chip, pc), True) for j, chip in enumerate(chips) for a in range(n)]
        for cp in first:
            cp.start()
        passed = []
        for j, chip in enumerate(chips):
            for a in range(n):
                copy(a, 1 + j, (*chip, pc), me).wait_recv()
                passed.append(copy(a, 4 + j, (*chip, pc), sibling))
                passed[-1].start()
        for a in range(n):
            copy(a, 0, sibling, me).wait_recv()
        for j, chip in enumerate(chips):
            for a in range(n):
                copy(a, 4 + j, (*chip, 1 - pc), me).wait_recv()
        for cp in first + passed:
            cp.wait_send()
        for cp in mine:
            cp.wait()

    return pl.pallas_call(
        body,
        name=name,
        out_shape=[jax.ShapeDtypeStruct((N_DEV,) + x.shape, x.dtype) for x in xs],
        in_specs=[_ANY] * n,
        out_specs=[_ANY] * n,
        scratch_shapes=[pltpu.SemaphoreType.DMA((n, 7)), pltpu.SemaphoreType.DMA((n, 7)), pltpu.SemaphoreType.DMA((n,))],
    )(*xs)


def all_to_all(xs, name):
    n = len(xs)

    def body(*refs):
        x_refs, out_refs = refs[:n], refs[n:2 * n]
        send_sems, recv_sems, local_sems = refs[2 * n:]
        px, py, pc = _my_pos()
        me = (px, py, pc)

        def flipped(k):
            kx, ky, kc = (k >> 2) & 1, (k >> 1) & 1, k & 1
            return (1 - px if kx else px, 1 - py if ky else py, 1 - pc if kc else pc)

        def copy(a, k):
            peer = flipped(k)
            return pltpu.make_async_remote_copy(
                src_ref=x_refs[a].at[_slot_of(peer)], dst_ref=out_refs[a].at[_slot_of(me)],
                send_sem=send_sems.at[a, k - 1], recv_sem=recv_sems.at[a, k - 1],
                device_id=peer, device_id_type=_MESH)

        def landing(a, k):
            peer = flipped(k)
            return pltpu.make_async_remote_copy(
                src_ref=x_refs[a].at[_slot_of(me)], dst_ref=out_refs[a].at[_slot_of(peer)],
                send_sem=send_sems.at[a, k - 1], recv_sem=recv_sems.at[a, k - 1],
                device_id=peer, device_id_type=_MESH)

        mine = [pltpu.make_async_copy(x_refs[a].at[_slot_of(me)], out_refs[a].at[_slot_of(me)], local_sems.at[a]) for a in range(n)]
        for cp in mine:
            cp.start()
        copies = [copy(a, k) for a in range(n) for k in range(1, N_DEV)]
        for cp in copies:
            cp.start()
        for a in range(n):
            for k in range(1, N_DEV):
                landing(a, k).wait_recv()
        for cp in copies:
            cp.wait_send()
        for cp in mine:
            cp.wait()

    return pl.pallas_call(
        body,
        name=name,
        out_shape=[jax.ShapeDtypeStruct(x.shape, x.dtype) for x in xs],
        in_specs=[_ANY] * n,
        out_specs=[_ANY] * n,
        scratch_shapes=[pltpu.SemaphoreType.DMA((n, 7)), pltpu.SemaphoreType.DMA((n, 7)), pltpu.SemaphoreType.DMA((n,))],
    )(*xs)


N_CHIPS = 4


def sibling_swap(xs, name):
    n = len(xs)

    def body(*refs):
        x_refs, out_refs = refs[:n], refs[n:2 * n]
        send_sems, recv_sems = refs[2 * n:]
        px, py, pc = _my_pos()
        copies = [pltpu.make_async_remote_copy(
            src_ref=x_refs[a].at[:, 1 - pc], dst_ref=out_refs[a],
            send_sem=send_sems.at[a], recv_sem=recv_sems.at[a],
            device_id=(px, py, 1 - pc), device_id_type=_MESH) for a in range(n)]
        for cp in copies:
            cp.start()
        for cp in copies:
            cp.wait()

    return pl.pallas_call(
        body,
        name=name,
        out_shape=[jax.ShapeDtypeStruct(x.shape[:1] + x.shape[2:], x.dtype) for x in xs],
        in_specs=[_ANY] * n,
        out_specs=[_ANY] * n,
        scratch_shapes=[pltpu.SemaphoreType.DMA((n,)), pltpu.SemaphoreType.DMA((n,))],
    )(*xs)


def chip_exchange(xs, name):
    n = len(xs)

    def body(*refs):
        x_refs, out_refs = refs[:n], refs[n:2 * n]
        send_sems, recv_sems, local_sems = refs[2 * n:]
        px, py, pc = _my_pos()
        my_chip = 2 * px + py

        def peer(k):
            return (1 - px if k & 2 else px, 1 - py if k & 1 else py)

        def copy(a, k, landing):
            qx, qy = peer(k)
            src, dst = (my_chip, 2 * qx + qy) if landing else (2 * qx + qy, my_chip)
            return pltpu.make_async_remote_copy(
                src_ref=x_refs[a].at[src], dst_ref=out_refs[a].at[dst],
                send_sem=send_sems.at[a, k - 1], recv_sem=recv_sems.at[a, k - 1],
                device_id=(qx, qy, pc), device_id_type=_MESH)

        mine = [pltpu.make_async_copy(x_refs[a].at[my_chip], out_refs[a].at[my_chip], local_sems.at[a]) for a in range(n)]
        for cp in mine:
            cp.start()
        copies = [copy(a, k, False) for a in range(n) for k in range(1, N_CHIPS)]
        for cp in copies:
            cp.start()
        for a in range(n):
            for k in range(1, N_CHIPS):
                copy(a, k, True).wait_recv()
        for cp in copies:
            cp.wait_send()
        for cp in mine:
            cp.wait()

    return pl.pallas_call(
        body,
        name=name,
        out_shape=[jax.ShapeDtypeStruct(x.shape, x.dtype) for x in xs],
        in_specs=[_ANY] * n,
        out_specs=[_ANY] * n,
        scratch_shapes=[pltpu.SemaphoreType.DMA((n, N_CHIPS - 1)), pltpu.SemaphoreType.DMA((n, N_CHIPS - 1)),
                        pltpu.SemaphoreType.DMA((n,))],
    )(*xs)


def pair_sum(own, got, name):
    nch, _, r, c = own.shape
    tr = _pick(r, (512, 256, 128, 64, 32, 16))

    def body(own_ref, got_ref, o_ref):
        pc = lax.axis_index("c")
        o_ref[...] = (own_ref[pc].astype(F32) + got_ref[...].astype(F32)).astype(o_ref.dtype)

    return pl.pallas_call(
        body,
        name=name,
        grid=(nch, r // tr),
        in_specs=[pl.BlockSpec((None, 2, tr, c), lambda i, j: (i, 0, j, 0)), pl.BlockSpec((None, tr, c), lambda i, j: (i, j, 0))],
        out_specs=pl.BlockSpec((None, tr, c), lambda i, j: (i, j, 0)),
        out_shape=jax.ShapeDtypeStruct((nch, r, c), own.dtype),
        compiler_params=_cparams(("parallel", "parallel")),
    )(own, got)


def sum_slots(x, name):
    n, r, c = x.shape
    tr = _pick(r, (512, 256, 128, 64, 32, 16, 8))

    def body(x_ref, o_ref):
        acc = x_ref[0].astype(F32)
        for i in range(1, n):
            acc = acc + x_ref[i].astype(F32)
        o_ref[...] = acc

    return pl.pallas_call(
        body,
        name=name,
        grid=(r // tr,),
        in_specs=[pl.BlockSpec((n, tr, c), lambda i: (0, i, 0))],
        out_specs=pl.BlockSpec((tr, c), lambda i: (i, 0)),
        out_shape=jax.ShapeDtypeStruct((r, c), F32),
        compiler_params=_cparams(("parallel",)),
    )(x)


CONV_CH_TILE = 256


def _shift_rows(x, off):
    n = x.shape[0]
    if off % n == 0:
        return x
    return pltpu.roll(x, (-off) % n, axis=0)


def _between(v, lo, hi):
    return jnp.where(v >= lo, 1.0, 0.0) * jnp.where(v < hi, 1.0, 0.0)


def taps_1d(ntaps, s_ctx, s_tot):
    def mask(off):
        def m(t):
            is_ctx = _between(t, 0, s_ctx)
            return is_ctx * _between(t + off, 0, s_ctx) + (1.0 - is_ctx) * _between(t + off, s_ctx, s_tot)
        return m
    return [(j - (ntaps - 1) // 2, mask(j - (ntaps - 1) // 2)) for j in range(ntaps)]


def taps_grid(s_ctx, s_tot, grid_w):
    assert s_ctx % grid_w == 0

    return ("grid", s_ctx, s_tot, grid_w)


def _grid_masks(taps, s):
    _, s_ctx, s_tot, grid_w = taps
    t = lax.broadcasted_iota(jnp.int32, (s, 1), 0)
    is_ctx = _between(t, 0, s_ctx)
    mcol = {dc: is_ctx * _between(t + dc, 0, s_ctx) + (1.0 - is_ctx) * _between(t % grid_w + dc, 0, grid_w) for dc in (-1, 1)}
    mrow = {dr: (1.0 - is_ctx) * _between(t + grid_w * dr, s_ctx, s_tot) for dr in (-1, 1)}
    return mcol, mrow


def _grid_cols(x, mcol):
    return {-1: _shift_rows(x, -1) * mcol[-1], 0: x, 1: _shift_rows(x, 1) * mcol[1]}


def _conv_acc(x, w_ref, b_ref, taps, s):
    acc = jnp.broadcast_to(b_ref[...], x.shape)
    if taps[0] == "grid":
        grid_w = taps[3]
        mcol, mrow = _grid_masks(taps, s)
        xc = _grid_cols(x, mcol)
        for a, dr in enumerate((-1, 0, 1)):
            r = sum(w_ref[3 * a + b:3 * a + b + 1, :] * xc[dc] for b, dc in enumerate((-1, 0, 1)))
            acc = acc + (r if dr == 0 else _shift_rows(r, grid_w * dr) * mrow[dr])
        return acc
    t = lax.broadcasted_iota(jnp.int32, (s, 1), 0)
    for k, (off, m) in enumerate(taps):
        acc = acc + w_ref[k:k + 1, :] * (_shift_rows(x, off) * m(t))
    return acc


def _conv_adjoint(x, dacc, w_ref, taps, s):
    if taps[0] == "grid":
        grid_w = taps[3]
        mcol, mrow = _grid_masks(taps, s)
        xc = _grid_cols(x, mcol)
        dxc = {dc: 0.0 for dc in (-1, 0, 1)}
        dws = []
        for a, dr in enumerate((-1, 0, 1)):
            d_r = dacc if dr == 0 else _shift_rows(dacc * mrow[dr], -grid_w * dr)
            for b, dc in enumerate((-1, 0, 1)):
                dxc[dc] = dxc[dc] + w_ref[3 * a + b:3 * a + b + 1, :] * d_r
                dws.append(jnp.sum(d_r * xc[dc], axis=0, keepdims=True))
        dx = dxc[0] + _shift_rows(dxc[-1] * mcol[-1], 1) + _shift_rows(dxc[1] * mcol[1], -1)
        return dx, dws
    t = lax.broadcasted_iota(jnp.int32, (s, 1), 0)
    dx = jnp.zeros_like(x)
    dws = []
    for k, (off, m) in enumerate(taps):
        dm = dacc * m(t)
        dx = dx + _shift_rows(w_ref[k:k + 1, :] * dm, -off)
        dws.append(jnp.sum(dm * _shift_rows(x, off), axis=0, keepdims=True))
    return dx, dws


def conv_fwd(name, x, w, b, taps, mode, mul=None, out_dtype=F32):
    nb, s, c = x.shape
    ct = _pick(c, (CONV_CH_TILE, 128))
    has_mul = mode == "silu_mul"

    def body(*refs):
        x_ref, w_ref, b_ref = refs[:3]
        o_ref = refs[-1]
        acc = _conv_acc(x_ref[...], w_ref, b_ref, taps, s)
        if mode == "none":
            out = acc
        else:
            out = _silu(acc)
            if has_mul:
                out = out * refs[3][...].astype(F32)
        o_ref[...] = out.astype(o_ref.dtype)

    blk = pl.BlockSpec((None, s, ct), lambda bb, j: (bb, 0, j))
    par = lambda k: pl.BlockSpec((k, ct), lambda bb, j: (0, j))
    return pl.pallas_call(
        body,
        name=name,
        grid=(nb, c // ct),
        in_specs=[blk, par(w.shape[0]), par(1)] + ([blk] if has_mul else []),
        out_specs=blk,
        out_shape=jax.ShapeDtypeStruct(x.shape, out_dtype),
        compiler_params=_cparams(("parallel", "parallel")),
    )(x, w, b, *([mul] if has_mul else []))


def conv_bwd(name, x, w, b, dout, taps, mode, mul=None, dx_dtype=F32):
    nb, s, c = x.shape
    ct = _pick(c, (CONV_CH_TILE, 128))
    has_mul = mode == "silu_mul"
    nk = w.shape[0]

    def body(*refs):
        x_ref, w_ref, b_ref, do_ref = refs[:4]
        n_in = 5 if has_mul else 4
        dx_ref, dw_ref, db_ref = refs[n_in:n_in + 3]
        bb = pl.program_id(1)
        x = x_ref[...]
        dacc = do_ref[...].astype(F32)
        if mode != "none":
            acc = _conv_acc(x, w_ref, b_ref, taps, s)
            sg = jax.nn.sigmoid(acc)
            if has_mul:
                refs[n_in + 3][...] = (dacc * (acc * sg)).astype(refs[n_in + 3].dtype)
                dacc = dacc * refs[4][...].astype(F32)
            dacc = dacc * (sg * (1.0 + acc * (1.0 - sg)))
        dx, dws = _conv_adjoint(x, dacc, w_ref, taps, s)
        dx_ref[...] = dx.astype(dx_ref.dtype)
        db = jnp.sum(dacc, axis=0, keepdims=True)

        @pl.when(bb == 0)
        def _():
            for k in range(nk):
                dw_ref[k:k + 1, :] = dws[k]
            db_ref[...] = db

        @pl.when(bb > 0)
        def _():
            for k in range(nk):
                dw_ref[k:k + 1, :] += dws[k]
            db_ref[...] += db

    blk = pl.BlockSpec((None, s, ct), lambda j, bb: (bb, 0, j))
    par = lambda k: pl.BlockSpec((k, ct), lambda j, bb: (0, j))
    out_specs = [blk, par(nk), par(1)] + ([blk] if has_mul else [])
    out_shape = [jax.ShapeDtypeStruct(x.shape, dx_dtype), jax.ShapeDtypeStruct(w.shape, F32), jax.ShapeDtypeStruct(b.shape, F32)]
    if has_mul:
        out_shape.append(jax.ShapeDtypeStruct(x.shape, dx_dtype))
    return pl.pallas_call(
        body,
        name=name,
        grid=(c // ct, nb),
        in_specs=[blk, par(nk), par(1), blk] + ([blk] if has_mul else []),
        out_specs=out_specs,
        out_shape=out_shape,
        compiler_params=_cparams(("parallel", "arbitrary")),
    )(x, w, b, dout, *([mul] if has_mul else []))


SCAN_UNROLL = 4


def _scan_order(direction, adjoint, s_ctx, s_tot):
    nc, nt = s_ctx // SUBLANES, s_tot // SUBLANES
    if direction == 0:
        return ([(0, nt, 1)], False) if not adjoint else ([(nt - 1, nt, -1)], True)
    if not adjoint:
        return [(nc - 1, nc, -1), (nt - 1, nt - nc, -1)], True
    return [(nc, nt - nc, 1), (0, nc, 1)], False


def _last_row(h, descending):
    row = lax.broadcasted_iota(jnp.int32, h.shape, 0)
    pick = 0 if descending else SUBLANES - 1
    return jnp.sum(jnp.where(row == pick, h, 0.0), axis=0, keepdims=True)


def _prev_rows(h, carry, descending):
    row = lax.broadcasted_iota(jnp.int32, h.shape, 0)
    if descending:
        return jnp.where(row == SUBLANES - 1, carry, pltpu.roll(h, SUBLANES - 1, axis=0))
    return jnp.where(row == 0, carry, pltpu.roll(h, 1, axis=0))


def _scan_real(a_ref, x_ref, h_ref, hp_ref, order):
    ranges, descending = order
    n_rows, width = a_ref.shape
    n_tiles = n_rows // SUBLANES
    row = lax.broadcasted_iota(jnp.int32, (SUBLANES, width), 0)
    unroll = lambda count: SCAN_UNROLL if count % SCAN_UNROLL == 0 else 1

    def run(ac_ref):
        def in_tile(i, _):
            t0 = pl.multiple_of(i * SUBLANES, SUBLANES)
            a = a_ref[pl.ds(t0, SUBLANES), :]
            x = x_ref[pl.ds(t0, SUBLANES), :]
            for k in (1, 2, 4):
                sh = SUBLANES - k if descending else k
                keep = (row < SUBLANES - k) if descending else (row >= k)
                x = jnp.where(keep, a * pltpu.roll(x, sh, axis=0) + x, x)
                a = jnp.where(keep, a * pltpu.roll(a, sh, axis=0), a)
            ac_ref[pl.ds(t0, SUBLANES), :] = a
            x_ref[pl.ds(t0, SUBLANES), :] = x
            return 0

        lax.fori_loop(0, n_tiles, in_tile, 0, unroll=unroll(n_tiles))

        def tile(i, carry):
            t0 = pl.multiple_of(i * SUBLANES, SUBLANES)
            a = ac_ref[pl.ds(t0, SUBLANES), :]
            x = x_ref[pl.ds(t0, SUBLANES), :]
            h = a * carry + x
            if h_ref is not None:
                h_ref[pl.ds(t0, SUBLANES), :] = h
            if hp_ref is not None:
                hp_ref[pl.ds(t0, SUBLANES), :] = _prev_rows(h, carry, descending)
            return _last_row(a, descending) * carry + _last_row(x, descending)

        carry = jnp.zeros((1, width), F32)
        for first, count, step in ranges:
            carry = lax.fori_loop(0, count, lambda j, c, first=first, step=step: tile(first + step * j, c), carry,
                                  unroll=unroll(count))

    pl.run_scoped(run, pltpu.VMEM((n_rows, width), F32))


def _cmul(ar, ai, br, bi):
    return ar * br - ai * bi, ar * bi + ai * br


def _scan_cplx(lr, li, xr_ref, xi_ref, hpr_ref, hpi_ref, order):
    ranges, descending = order
    width = xr_ref.shape[1]
    row = lax.broadcasted_iota(jnp.int32, (SUBLANES, width), 0)
    pw = [(lr, li)]
    for _ in range(SUBLANES - 1):
        pw.append(_cmul(pw[-1][0], pw[-1][1], lr, li))
    pr = jnp.zeros((SUBLANES, width), F32)
    pi = jnp.zeros((SUBLANES, width), F32)
    for r in range(SUBLANES):
        n = SUBLANES - 1 - r if descending else r
        pr = jnp.where(row == r, pw[n][0], pr)
        pi = jnp.where(row == r, pw[n][1], pi)

    n_tiles = xr_ref.shape[0] // SUBLANES
    unroll = lambda count: SCAN_UNROLL if count % SCAN_UNROLL == 0 else 1

    def in_tile(i, _):
        t0 = pl.multiple_of(i * SUBLANES, SUBLANES)
        xr = xr_ref[pl.ds(t0, SUBLANES), :]
        xi = xi_ref[pl.ds(t0, SUBLANES), :]
        for k in (1, 2, 4):
            sh = SUBLANES - k if descending else k
            keep = (row < SUBLANES - k) if descending else (row >= k)
            sr, si = _cmul(pw[k - 1][0], pw[k - 1][1], pltpu.roll(xr, sh, axis=0), pltpu.roll(xi, sh, axis=0))
            xr = jnp.where(keep, xr + sr, xr)
            xi = jnp.where(keep, xi + si, xi)
        xr_ref[pl.ds(t0, SUBLANES), :] = xr
        xi_ref[pl.ds(t0, SUBLANES), :] = xi
        return 0

    lax.fori_loop(0, n_tiles, in_tile, 0, unroll=unroll(n_tiles))
    lam8 = pw[SUBLANES - 1]

    def tile(i, carry):
        cr, ci = carry
        t0 = pl.multiple_of(i * SUBLANES, SUBLANES)
        xr = xr_ref[pl.ds(t0, SUBLANES), :]
        xi = xi_ref[pl.ds(t0, SUBLANES), :]
        hr, hi = _cmul(pr, pi, cr, ci)
        hr, hi = hr + xr, hi + xi
        xr_ref[pl.ds(t0, SUBLANES), :] = hr
        xi_ref[pl.ds(t0, SUBLANES), :] = hi
        if hpr_ref is not None:
            hpr_ref[pl.ds(t0, SUBLANES), :] = _prev_rows(hr, cr, descending)
            hpi_ref[pl.ds(t0, SUBLANES), :] = _prev_rows(hi, ci, descending)
        nr, ni = _cmul(lam8[0], lam8[1], cr, ci)
        return nr + _last_row(xr, descending), ni + _last_row(xi, descending)

    carry = (jnp.zeros((1, width), F32), jnp.zeros((1, width), F32))
    for first, count, step in ranges:
        carry = lax.fori_loop(0, count, lambda j, c, first=first, step=step: tile(first + step * j, c), carry,
                              unroll=unroll(count))


def _log1p_pos(y):
    return jnp.where(y < 0.01, y * (1.0 - y * (0.5 - y * (1.0 / 3.0 - 0.25 * y))), jnp.log(1.0 + y))


def _softplus(x):
    return jnp.maximum(x, 0.0) + _log1p_pos(jnp.exp(-jnp.abs(x)))


def _neg_expm1(z):
    series = -z * (1.0 + z * (0.5 + z * (1.0 / 6.0 + z * (1.0 / 24.0 + z * (1.0 / 120.0)))))
    return jnp.where(z > -0.1, series, 1.0 - jnp.exp(z))


def _lru_gates(u, w_a, b_a, w_i, b_i, lam):
    ub = u.astype(BF16)
    r = jax.nn.sigmoid(jnp.dot(ub, w_a.astype(BF16), preferred_element_type=F32) + b_a)
    i = jax.nn.sigmoid(jnp.dot(ub, w_i.astype(BF16), preferred_element_type=F32) + b_i)
    log_a = (-LRU_C) * _softplus(-lam) * r
    return jnp.exp(log_a), jnp.sqrt(_neg_expm1(2.0 * log_a)) * (i * u)


LRU_PER_STEP = 4
LRU_PER_STEP_BWD = 2


def _lru_specs(per, bw, order):
    w = pl.BlockSpec((2, per, bw, bw), lambda *g: (0, order(*g), 0, 0))
    v = pl.BlockSpec((2, per, 1, bw), lambda *g: (0, order(*g), 0, 0))
    return [w, v, w, v, v]


def lru_fwd(name, u, w_a, b_a, w_i, b_i, lam, s_ctx):
    nb, s, _ = u.shape
    nblk, bw = w_a.shape[1], w_a.shape[2]
    per = min(LRU_PER_STEP, nblk)

    def body(u_ref, wa, ba, wi, bi, lm, o_ref, a_s, x_s, h_s):
        for d in (0, 1):
            for k in range(per):
                cols = slice(k * bw, (k + 1) * bw)
                a, bx = _lru_gates(u_ref[:, cols], wa[d, k], ba[d, k], wi[d, k], bi[d, k], lm[d, k])
                a_s[:, cols] = a
                x_s[:, cols] = bx
            _scan_real(a_s, x_s, h_s, None, _scan_order(d, False, s_ctx, s))
            if d == 0:
                o_ref[...] = h_s[...]
            else:
                o_ref[...] += h_s[...]

    blk = pl.BlockSpec((None, s, per * bw), lambda b, n: (b, 0, n))
    return pl.pallas_call(
        body,
        name=name,
        grid=(nb, nblk // per),
        in_specs=[blk] + _lru_specs(per, bw, lambda b, n: n),
        out_specs=blk,
        out_shape=jax.ShapeDtypeStruct(u.shape, F32),
        scratch_shapes=[pltpu.VMEM((s, per * bw), F32)] * 3,
        compiler_params=_cparams(("parallel", "parallel")),
    )(u, w_a, b_a, w_i, b_i, lam)


def lru_bwd(name, u, w_a, b_a, w_i, b_i, lam, dh, s_ctx):
    nb, s, _ = u.shape
    nblk, bw = w_a.shape[1], w_a.shape[2]
    per = min(LRU_PER_STEP_BWD, nblk)

    def body(u_ref, wa, ba, wi, bi, lm, dh_ref, du_ref, dwa, dba, dwi, dbi, dlm, a_s, x_s, hp_s, wp_s):
        b = pl.program_id(1)
        for d in (0, 1):
            for k in range(per):
                cols = slice(k * bw, (k + 1) * bw)
                a, bx = _lru_gates(u_ref[:, cols], wa[d, k], ba[d, k], wi[d, k], bi[d, k], lm[d, k])
                a_s[:, cols] = a
                x_s[:, cols] = bx
            _scan_real(a_s, x_s, None, hp_s, _scan_order(d, False, s_ctx, s))
            x_s[...] = a_s[...] * dh_ref[...]
            _scan_real(a_s, x_s, None, wp_s, _scan_order(d, True, s_ctx, s))
            for k in range(per):
                cols = slice(k * bw, (k + 1) * bw)
                g = dh_ref[:, cols] + wp_s[:, cols]
                _, vjp = jax.vjp(_lru_gates, u_ref[:, cols], wa[d, k], ba[d, k], wi[d, k], bi[d, k], lm[d, k])
                grads = vjp((g * hp_s[:, cols], g))
                if d == 0:
                    du_ref[:, cols] = grads[0]
                else:
                    du_ref[:, cols] += grads[0]
                for ref, val in zip((dwa, dba, dwi, dbi, dlm), grads[1:]):
                    @pl.when(b == 0)
                    def _(ref=ref, val=val, k=k):
                        ref[d, k] = val

                    @pl.when(b > 0)
                    def _(ref=ref, val=val, k=k):
                        ref[d, k] += val

    blk = pl.BlockSpec((None, s, per * bw), lambda n, b: (b, 0, n))
    pspecs = _lru_specs(per, bw, lambda n, b: n)
    return pl.pallas_call(
        body,
        name=name,
        grid=(nblk // per, nb),
        in_specs=[blk] + pspecs + [blk],
        out_specs=[blk] + pspecs,
        out_shape=[jax.ShapeDtypeStruct(u.shape, F32)] + [jax.ShapeDtypeStruct(p.shape, F32) for p in (w_a, b_a, w_i, b_i, lam)],
        scratch_shapes=[pltpu.VMEM((s, per * bw), F32)] * 4,
        compiler_params=_cparams(("parallel", "arbitrary")),
    )(u, w_a, b_a, w_i, b_i, lam, dh)


S5_TILE_CH = 128
S5_TILE_STATES = S5_TILE_CH // S5_GROUP_CH * S5_STATE


def _dot_nt(a, b):
    return lax.dot_general(a, b, (((1,), (1,)), ((), ())), preferred_element_type=F32)


def _dot_tn(a, b):
    return lax.dot_general(a, b, (((0,), (0,)), ((), ())), preferred_element_type=F32)


def _s5_specs(order):
    lam = pl.BlockSpec((2, 1, S5_TILE_STATES), lambda *g: (0, 0, order(*g)))
    mat = pl.BlockSpec((2, None, S5_TILE_STATES, S5_TILE_CH), lambda *g: (0, order(*g), 0, 0))
    return [lam, lam, mat, mat, mat, mat]


def s5_fwd(name, u, lam_r, lam_i, bt_r, bt_i, ct_r, ct_i, s_ctx):
    nb, s, w = u.shape

    def body(u_ref, lr, li, btr, bti, ctr, cti, o_ref, xr_s, xi_s):
        ub = u_ref[...].astype(BF16)
        for d in (0, 1):
            xr_s[...] = _dot_nt(ub, btr[d].astype(BF16))
            xi_s[...] = _dot_nt(ub, bti[d].astype(BF16))
            _scan_cplx(lr[d], li[d], xr_s, xi_s, None, None, _scan_order(d, False, s_ctx, s))
            y = (jnp.dot(xr_s[...].astype(BF16), ctr[d].astype(BF16), preferred_element_type=F32)
                 - jnp.dot(xi_s[...].astype(BF16), cti[d].astype(BF16), preferred_element_type=F32))
            if d == 0:
                o_ref[...] = y
            else:
                o_ref[...] += y

    blk = pl.BlockSpec((None, s, S5_TILE_CH), lambda b, j: (b, 0, j))
    return pl.pallas_call(
        body,
        name=name,
        grid=(nb, w // S5_TILE_CH),
        in_specs=[blk] + _s5_specs(lambda b, j: j),
        out_specs=blk,
        out_shape=jax.ShapeDtypeStruct(u.shape, F32),
        scratch_shapes=[pltpu.VMEM((s, S5_TILE_STATES), F32)] * 2,
        compiler_params=_cparams(("parallel", "parallel")),
    )(u, lam_r, lam_i, bt_r, bt_i, ct_r, ct_i)


def s5_bwd(name, u, lam_r, lam_i, bt_r, bt_i, ct_r, ct_i, dy, s_ctx):
    nb, s, w = u.shape

    def body(u_ref, lr, li, btr, bti, ctr, cti, dy_ref, du_ref, dlr, dli, dbtr, dbti, dctr, dcti,
             hr_s, hi_s, hpr_s, hpi_s, gr_s, gi_s):
        b = pl.program_id(1)
        ub = u_ref[...].astype(BF16)
        dyb = dy_ref[...].astype(BF16)
        du = jnp.zeros((s, S5_TILE_CH), F32)
        for d in (0, 1):
            hr_s[...] = _dot_nt(ub, btr[d].astype(BF16))
            hi_s[...] = _dot_nt(ub, bti[d].astype(BF16))
            _scan_cplx(lr[d], li[d], hr_s, hi_s, hpr_s, hpi_s, _scan_order(d, False, s_ctx, s))
            d_ctr = _dot_tn(hr_s[...].astype(BF16), dyb)
            d_cti = -_dot_tn(hi_s[...].astype(BF16), dyb)
            gr_s[...] = _dot_nt(dyb, ctr[d].astype(BF16))
            gi_s[...] = -_dot_nt(dyb, cti[d].astype(BF16))
            _scan_cplx(lr[d], -li[d], gr_s, gi_s, None, None, _scan_order(d, True, s_ctx, s))
            gr, gi = gr_s[...], gi_s[...]
            hpr, hpi = hpr_s[...], hpi_s[...]
            d_lr = jnp.sum(gr * hpr + gi * hpi, axis=0, keepdims=True)
            d_li = jnp.sum(gi * hpr - gr * hpi, axis=0, keepdims=True)
            grb, gib = gr.astype(BF16), gi.astype(BF16)
            du = du + jnp.dot(grb, btr[d].astype(BF16), preferred_element_type=F32)
            du = du + jnp.dot(gib, bti[d].astype(BF16), preferred_element_type=F32)
            d_btr = _dot_tn(grb, ub)
            d_bti = _dot_tn(gib, ub)
            for ref, val in zip((dlr, dli, dbtr, dbti, dctr, dcti), (d_lr, d_li, d_btr, d_bti, d_ctr, d_cti)):
                @pl.when(b == 0)
                def _(ref=ref, val=val):
                    ref[d] = val

                @pl.when(b > 0)
                def _(ref=ref, val=val):
                    ref[d] += val
        du_ref[...] = du

    blk = pl.BlockSpec((None, s, S5_TILE_CH), lambda j, b: (b, 0, j))
    pspecs = _s5_specs(lambda j, b: j)
    params = (lam_r, lam_i, bt_r, bt_i, ct_r, ct_i)
    return pl.pallas_call(
        body,
        name=name,
        grid=(w // S5_TILE_CH, nb),
        in_specs=[blk] + pspecs + [blk],
        out_specs=[blk] + pspecs,
        out_shape=[jax.ShapeDtypeStruct(u.shape, F32)] + [jax.ShapeDtypeStruct(p.shape, F32) for p in params],
        scratch_shapes=[pltpu.VMEM((s, S5_TILE_STATES), F32)] * 6,
        compiler_params=_cparams(("parallel", "arbitrary")),
    )(u, lam_r, lam_i, bt_r, bt_i, ct_r, ct_i, dy)


def small_fwd(name, f, ins, out_shapes):
    n = len(ins)

    def body(*refs):
        for o, r in zip(refs[n:], f([r[...] for r in refs[:n]])):
            o[...] = r

    return pl.pallas_call(
        body, name=name,
        out_shape=[jax.ShapeDtypeStruct(s, F32) for s in out_shapes],
        compiler_params=_cparams(),
    )(*ins)


def small_bwd(name, f, ins, cots):
    n, nc = len(ins), len(cots)

    def body(*refs):
        _, vjp = jax.vjp(f, [r[...] for r in refs[:n]])
        (grads,) = vjp([r[...] for r in refs[n:n + nc]])
        for o, r in zip(refs[n + nc:], grads):
            o[...] = r

    return pl.pallas_call(
        body, name=name,
        out_shape=[jax.ShapeDtypeStruct(a.shape, F32) for a in ins],
        compiler_params=_cparams(),
    )(*ins, *cots)


def _row(x, r):
    return jnp.sum(jnp.where(lax.broadcasted_iota(jnp.int32, x.shape, 0) == r, x, 0.0), axis=0, keepdims=True)


def _col(x, c):
    return jnp.sum(jnp.where(lax.broadcasted_iota(jnp.int32, x.shape, 1) == c, x, 0.0), axis=1, keepdims=True)


def _chunk_at(i, reverse, ncc, nc):
    if not reverse:
        return i
    return jnp.where(i < ncc, ncc - 1 - i, nc - 1 - (i - ncc))


def _tri(n, reverse):
    li = lax.broadcasted_iota(jnp.int32, (n, n), 0)
    si = lax.broadcasted_iota(jnp.int32, (n, n), 1)
    return jnp.where((li <= si) if reverse else (li >= si), 1.0, 0.0)


_HI = lax.Precision.HIGHEST


def _ssd_chunk(xs, bm, cm, dtc, dtr, a_row, a_col, hs, reverse):
    n = bm.shape[0]
    last = 0 if reverse else n - 1
    tri = _tri(n, reverse)
    cum_c = jnp.dot(tri, dtc * -jnp.exp(a_row), precision=_HI, preferred_element_type=F32)
    cum_r = lax.dot_general(dtr * -jnp.exp(a_col), tri, (((1,), (1,)), ((), ())), precision=_HI, preferred_element_type=F32)
    tot_r = _row(cum_c, last)
    bmb = bm.astype(BF16)
    cb = _dot_nt(cm.astype(BF16), bmb)
    ys, hn = [], []
    for hd in range(len(xs)):
        cl = _col(cum_c, hd)
        tot = _col(tot_r, hd)
        decay = jnp.exp(jnp.where(tri > 0.0, cl - _row(cum_r, hd), -jnp.inf))
        xd = xs[hd] * _col(dtc, hd)
        lhs = jnp.concatenate([(cb * decay).astype(BF16), (cm * jnp.exp(cl)).astype(BF16)], axis=1)
        rhs = jnp.concatenate([xd.astype(BF16), hs[hd].astype(BF16)], axis=0)
        ys.append(jnp.dot(lhs, rhs, preferred_element_type=F32))
        hn.append(hs[hd] * jnp.exp(tot) + _dot_tn(bmb, (xd * jnp.exp(tot - cl)).astype(BF16)))
    return ys, hn


def _ssd_specs(reverse, ncc, nc, order):
    ch = lambda *g: _chunk_at(order(*g)[1], reverse, ncc, nc)
    b_ = lambda *g: order(*g)[0]
    gn = SSD_GROUPS * SSD_STATE
    return [
        pl.BlockSpec((None, SSD_CHUNK, SSD_W), lambda *g: (b_(*g), ch(*g), 0)),
        pl.BlockSpec((None, SSD_CHUNK, gn), lambda *g: (b_(*g), ch(*g), SSD_W // gn)),
        pl.BlockSpec((None, SSD_CHUNK, gn), lambda *g: (b_(*g), ch(*g), SSD_W // gn + 1)),
        pl.BlockSpec((None, SSD_GROUPS, SSD_CHUNK, SSD_HPG), lambda *g: (b_(*g), 0, ch(*g), 0)),
        pl.BlockSpec((None, SSD_GROUPS, SSD_HPG, SSD_CHUNK), lambda *g: (b_(*g), 0, 0, ch(*g))),
        pl.BlockSpec((SSD_GROUPS, 1, SSD_HPG), lambda *g: (0, 0, 0)),
        pl.BlockSpec((SSD_GROUPS, SSD_HPG, 1), lambda *g: (0, 0, 0)),
    ]


def _ssd_group_inputs(g, x_ref, bm_ref, cm_ref, dtc_ref, dtr_ref, ar_ref, ac_ref):
    p, n = SSD_HEAD_DIM, SSD_STATE
    xs = [x_ref[:, p * (SSD_HPG * g + hd):p * (SSD_HPG * g + hd + 1)] for hd in range(SSD_HPG)]
    return xs, bm_ref[:, n * g:n * (g + 1)], cm_ref[:, n * g:n * (g + 1)], dtc_ref[g], dtr_ref[g], ar_ref[g], ac_ref[g]


def ssd_fwd(name, xbc, dt_col, dt_row, a_row, a_col, reverse, s_ctx):
    nb, s, _ = xbc.shape
    nc, ncc = s // SSD_CHUNK, s_ctx // SSD_CHUNK
    p = SSD_HEAD_DIM

    def body(x_ref, bm_ref, cm_ref, dtc_ref, dtr_ref, ar_ref, ac_ref, y_ref, hst_ref, h_s):
        i = pl.program_id(1)

        @pl.when(i == 0)
        def _():
            h_s[...] = jnp.zeros_like(h_s)

        hst_ref[...] = h_s[...]
        for g in range(SSD_GROUPS):
            xs, bm, cm, dtc, dtr, ar, ac = _ssd_group_inputs(g, x_ref, bm_ref, cm_ref, dtc_ref, dtr_ref, ar_ref, ac_ref)
            ys, hn = _ssd_chunk(xs, bm, cm, dtc, dtr, ar, ac, [h_s[g, hd] for hd in range(SSD_HPG)], reverse)
            for hd in range(SSD_HPG):
                y_ref[:, p * (SSD_HPG * g + hd):p * (SSD_HPG * g + hd + 1)] = ys[hd]
                h_s[g, hd] = hn[hd]

    state = (SSD_GROUPS, SSD_HPG, SSD_STATE, SSD_HEAD_DIM)
    return pl.pallas_call(
        body,
        name=name,
        grid=(nb, nc),
        in_specs=_ssd_specs(reverse, ncc, nc, lambda b, i: (b, i)),
        out_specs=[pl.BlockSpec((None, SSD_CHUNK, SSD_W), lambda b, i: (b, _chunk_at(i, reverse, ncc, nc), 0)),
                   pl.BlockSpec((None, None) + state, lambda b, i: (b, i, 0, 0, 0, 0))],
        out_shape=[jax.ShapeDtypeStruct((nb, s, SSD_W), F32), jax.ShapeDtypeStruct((nb, nc) + state, F32)],
        scratch_shapes=[pltpu.VMEM(state, F32)],
        compiler_params=_cparams(("parallel", "arbitrary")),
    )(xbc, xbc, xbc, dt_col, dt_row, a_row, a_col)


def ssd_bwd(name, xbc, dt_col, dt_row, a_row, a_col, hst, dy, reverse, s_ctx):
    nb, s, _ = xbc.shape
    nc, ncc = s // SSD_CHUNK, s_ctx // SSD_CHUNK
    p, n = SSD_HEAD_DIM, SSD_STATE

    def body(x_ref, bm_ref, cm_ref, dtc_ref, dtr_ref, ar_ref, ac_ref, hst_ref, dy_ref,
             dx_ref, dbm_ref, dcm_ref, ddtc_ref, ddtr_ref, dar_ref, dac_ref, dh_s):
        i = pl.program_id(1)

        @pl.when(i == 0)
        def _():
            dh_s[...] = jnp.zeros_like(dh_s)

        for g in range(SSD_GROUPS):
            xs, bm, cm, dtc, dtr, ar, ac = _ssd_group_inputs(g, x_ref, bm_ref, cm_ref, dtc_ref, dtr_ref, ar_ref, ac_ref)
            hs = [hst_ref[g, hd] for hd in range(SSD_HPG)]
            _, vjp = jax.vjp(functools.partial(_ssd_chunk, reverse=reverse), xs, bm, cm, dtc, dtr, ar, ac, hs)
            dys = [dy_ref[:, p * (SSD_HPG * g + hd):p * (SSD_HPG * g + hd + 1)] for hd in range(SSD_HPG)]
            dxs, dbm, dcm, ddtc, ddtr, dar, dac, dhs = vjp((dys, [dh_s[g, hd] for hd in range(SSD_HPG)]))
            for hd in range(SSD_HPG):
                dx_ref[:, p * (SSD_HPG * g + hd):p * (SSD_HPG * g + hd + 1)] = dxs[hd]
                dh_s[g, hd] = dhs[hd]
            dbm_ref[:, n * g:n * (g + 1)] = dbm
            dcm_ref[:, n * g:n * (g + 1)] = dcm
            ddtc_ref[g] = ddtc
            ddtr_ref[g] = ddtr

            @pl.when(i == 0)
            def _(g=g, dar=dar, dac=dac):
                dar_ref[g] = dar
                dac_ref[g] = dac

            @pl.when(i > 0)
            def _(g=g, dar=dar, dac=dac):
                dar_ref[g] += dar
                dac_ref[g] += dac

    ch = lambda b, i: _chunk_at(nc - 1 - i, reverse, ncc, nc)
    state = (SSD_GROUPS, SSD_HPG, SSD_STATE, SSD_HEAD_DIM)
    gn = SSD_GROUPS * SSD_STATE
    in_specs = _ssd_specs(reverse, ncc, nc, lambda b, i: (b, nc - 1 - i)) + [
        pl.BlockSpec((None, None) + state, lambda b, i: (b, nc - 1 - i, 0, 0, 0, 0)),
        pl.BlockSpec((None, SSD_CHUNK, SSD_W), lambda b, i: (b, ch(b, i), 0))]
    out_specs = [
        pl.BlockSpec((None, SSD_CHUNK, SSD_W), lambda b, i: (b, ch(b, i), 0)),
        pl.BlockSpec((None, SSD_CHUNK, gn), lambda b, i: (b, ch(b, i), 0)),
        pl.BlockSpec((None, SSD_CHUNK, gn), lambda b, i: (b, ch(b, i), 0)),
        pl.BlockSpec((None, SSD_GROUPS, SSD_CHUNK, SSD_HPG), lambda b, i: (b, 0, ch(b, i), 0)),
        pl.BlockSpec((None, SSD_GROUPS, SSD_HPG, SSD_CHUNK), lambda b, i: (b, 0, 0, ch(b, i))),
        pl.BlockSpec((None, SSD_GROUPS, 1, SSD_HPG), lambda b, i: (b, 0, 0, 0)),
        pl.BlockSpec((None, SSD_GROUPS, SSD_HPG, 1), lambda b, i: (b, 0, 0, 0)),
    ]
    out_shape = [
        jax.ShapeDtypeStruct((nb, s, SSD_W), F32),
        jax.ShapeDtypeStruct((nb, s, gn), F32),
        jax.ShapeDtypeStruct((nb, s, gn), F32),
        jax.ShapeDtypeStruct(dt_col.shape, F32),
        jax.ShapeDtypeStruct(dt_row.shape, F32),
        jax.ShapeDtypeStruct((nb, SSD_GROUPS, 1, SSD_HPG), F32),
        jax.ShapeDtypeStruct((nb, SSD_GROUPS, SSD_HPG, 1), F32),
    ]
    return pl.pallas_call(
        body,
        name=name,
        grid=(nb, nc),
        in_specs=in_specs,
        out_specs=out_specs,
        out_shape=out_shape,
        scratch_shapes=[pltpu.VMEM(state, F32)],
        compiler_params=_cparams(("parallel", "arbitrary")),
    )(xbc, xbc, xbc, dt_col, dt_row, a_row, a_col, hst, dy)


HG_TILES = HG_CHUNK // SUBLANES


def _hg_cum_tiles(x_t, reverse):
    row = lax.broadcasted_iota(jnp.int32, x_t[0].shape, 0)
    out = [None] * len(x_t)
    off = None
    for i in (reversed(range(len(x_t))) if reverse else range(len(x_t))):
        c = x_t[i]
        for k in (1, 2, 4):
            keep = (row < SUBLANES - k) if reverse else (row >= k)
            c = jnp.where(keep, c + pltpu.roll(c, SUBLANES - k if reverse else k, axis=0), c)
        out[i] = c if off is None else c + off
        off = _last_row(out[i], reverse)
    return out, off


def _hg_pairs(reverse):
    row = lax.broadcasted_iota(jnp.int32, (SUBLANES, HG_DK), 0)
    rots = []
    for r in range(SUBLANES):
        rots.append(((SUBLANES - r) % SUBLANES, row <= SUBLANES - 1 - r) if reverse else (r, row >= r))
    return [(j, [i for i in range(HG_TILES) if (i <= j if reverse else i >= j)], rots) for j in range(HG_TILES)]


def _rot(x, sh):
    return pltpu.roll(x, sh, axis=0) if sh else x


def _cat(tiles):
    return jnp.concatenate(tiles, axis=0)


def _hg_chunk_fwd(q_t, k_t, lf_t, v_t, st, reverse):
    cum_t, tot = _hg_cum_tiles(lf_t, reverse)
    y_t = [jnp.zeros(v_t[0].shape, F32) for _ in v_t]
    for j, l_tiles, rots in _hg_pairs(reverse):
        for sh, diag_ok in rots:
            k_j, c_j, v_j = _rot(k_t[j], sh), _rot(cum_t[j], sh), _rot(v_t[j], sh)
            for i in l_tiles:
                e = jnp.exp(cum_t[i] - c_j)
                if i == j:
                    e = jnp.where(diag_ok, e, 0.0)
                att = jnp.sum(q_t[i] * (k_j * e), axis=1, keepdims=True)
                y_t[i] = y_t[i] + att * v_j
    q, k, v, cum = _cat(q_t), _cat(k_t), _cat(v_t), _cat(cum_t)
    y_state = _dot_nt((q * jnp.exp(cum)).astype(BF16), st.astype(BF16))
    st_new = st * jnp.exp(tot) + _dot_tn(v.astype(BF16), (k * jnp.exp(tot - cum)).astype(BF16))
    return [y + y_state[SUBLANES * i:SUBLANES * (i + 1)] for i, y in enumerate(y_t)], st_new


def _hg_chunk_bwd(q_t, k_t, lf_t, v_t, st, dy_t, dst_new, reverse):
    nt = len(q_t)
    cum_t, tot = _hg_cum_tiles(lf_t, reverse)
    q, k, v, cum, dy = _cat(q_t), _cat(k_t), _cat(v_t), _cat(cum_t), _cat(dy_t)
    e_cum, e_tot, e_end = jnp.exp(cum), jnp.exp(tot), jnp.exp(tot - cum)
    qt, khat = q * e_cum, k * e_end
    dyb, dsb = dy.astype(BF16), dst_new.astype(BF16)
    dqt = jnp.dot(dyb, st.astype(BF16), preferred_element_type=F32)
    dst = dst_new * e_tot + _dot_tn(dyb, qt.astype(BF16))
    dv = _dot_nt(khat.astype(BF16), dsb)
    dkhat = jnp.dot(v.astype(BF16), dsb, preferred_element_type=F32)
    t1 = dkhat * khat
    dtot = jnp.sum(dst_new * st, axis=0, keepdims=True) * e_tot + jnp.sum(t1, axis=0, keepdims=True)
    rows = lax.broadcasted_iota(jnp.int32, cum.shape, 0)
    last = 0 if reverse else cum.shape[0] - 1
    dcum = dqt * qt - t1 + jnp.where(rows == last, dtot, 0.0)
    tiles = lambda a: [a[SUBLANES * i:SUBLANES * (i + 1)] for i in range(nt)]
    dq_t, dk_t, dv_t, dcum_t = tiles(dqt * e_cum), tiles(dkhat * e_end), tiles(dv), tiles(dcum)
    for j, l_tiles, rots in _hg_pairs(reverse):
        for sh, diag_ok in rots:
            k_j, c_j, v_j = _rot(k_t[j], sh), _rot(cum_t[j], sh), _rot(v_t[j], sh)
            acc_v = acc_k = acc_c = None
            for i in l_tiles:
                e = jnp.exp(cum_t[i] - c_j)
                if i == j:
                    e = jnp.where(diag_ok, e, 0.0)
                ke, qe = k_j * e, q_t[i] * e
                p = q_t[i] * ke
                att = jnp.sum(p, axis=1, keepdims=True)
                datt = jnp.sum(dy_t[i] * v_j, axis=1, keepdims=True)
                g = datt * p
                dq_t[i] = dq_t[i] + datt * ke
                dcum_t[i] = dcum_t[i] + g
                av, ak = att * dy_t[i], datt * qe
                acc_v, acc_k, acc_c = (av, ak, g) if acc_v is None else (acc_v + av, acc_k + ak, acc_c + g)
            back = (SUBLANES - sh) % SUBLANES
            dv_t[j] = dv_t[j] + _rot(acc_v, back)
            dk_t[j] = dk_t[j] + _rot(acc_k, back)
            dcum_t[j] = dcum_t[j] - _rot(acc_c, back)
    dlf_t, _ = _hg_cum_tiles(dcum_t, not reverse)
    return dq_t, dk_t, dlf_t, dv_t, dst


def _hg_super(s_ctx):
    return min(256, s_ctx)


def hg_fwd(name, q, k, lf, v, reverse, s_ctx):
    nb, s, w = q.shape
    nh, dk, sup = w // HG_DK, HG_DK, _hg_super(s_ctx)
    nsup, nsc, cps = s // sup, s_ctx // sup, sup // HG_CHUNK

    def body(q_ref, k_ref, lf_ref, v_ref, y_ref, hst_ref, st_s):
        i = pl.program_id(2)

        @pl.when(i == 0)
        def _():
            st_s[...] = jnp.zeros_like(st_s)

        def step(c, st):
            r0 = pl.multiple_of((cps - 1 - c if reverse else c) * HG_CHUNK, HG_CHUNK)
            tile = lambda ref: [ref[pl.ds(r0 + SUBLANES * i, SUBLANES), :] for i in range(HG_TILES)]
            hst_ref[c] = st
            y_t, st_new = _hg_chunk_fwd(tile(q_ref), tile(k_ref), tile(lf_ref), tile(v_ref), st, reverse)
            for i in range(HG_TILES):
                y_ref[pl.ds(r0 + SUBLANES * i, SUBLANES), :] = y_t[i]
            return st_new

        st_s[...] = lax.fori_loop(0, cps, step, st_s[...], unroll=2 if cps % 2 == 0 else 1)

    blk = pl.BlockSpec((None, sup, dk), lambda b, h, i: (b, _chunk_at(i, reverse, nsc, nsup), h))
    return pl.pallas_call(
        body,
        name=name,
        grid=(nb, nh, nsup),
        in_specs=[blk] * 4,
        out_specs=[blk, pl.BlockSpec((None, None, cps, dk, dk), lambda b, h, i: (b, h, i, 0, 0))],
        out_shape=[jax.ShapeDtypeStruct(q.shape, F32), jax.ShapeDtypeStruct((nb, nh, s // HG_CHUNK, dk, dk), F32)],
        scratch_shapes=[pltpu.VMEM((dk, dk), F32)],
        compiler_params=_cparams(("parallel", "parallel", "arbitrary")),
    )(q, k, lf, v)


def hg_bwd(name, q, k, lf, v, hst, dy, reverse, s_ctx):
    nb, s, w = q.shape
    nh, dk, sup = w // HG_DK, HG_DK, _hg_super(s_ctx)
    nsup, nsc, cps = s // sup, s_ctx // sup, sup // HG_CHUNK

    def body(q_ref, k_ref, lf_ref, v_ref, hst_ref, dy_ref, dq_ref, dk_ref, dlf_ref, dv_ref, dst_s):
        i = pl.program_id(2)

        @pl.when(i == 0)
        def _():
            dst_s[...] = jnp.zeros_like(dst_s)

        def step(cc, dst):
            c = cps - 1 - cc
            r0 = pl.multiple_of((cps - 1 - c if reverse else c) * HG_CHUNK, HG_CHUNK)
            tile = lambda ref: [ref[pl.ds(r0 + SUBLANES * i, SUBLANES), :] for i in range(HG_TILES)]
            dq_t, dk_t, dlf_t, dv_t, dst_prev = _hg_chunk_bwd(
                tile(q_ref), tile(k_ref), tile(lf_ref), tile(v_ref), hst_ref[c], tile(dy_ref), dst, reverse)
            for ref, val in zip((dq_ref, dk_ref, dlf_ref, dv_ref), (dq_t, dk_t, dlf_t, dv_t)):
                for i in range(HG_TILES):
                    ref[pl.ds(r0 + SUBLANES * i, SUBLANES), :] = val[i]
            return dst_prev

        dst_s[...] = lax.fori_loop(0, cps, step, dst_s[...], unroll=2 if cps % 2 == 0 else 1)

    blk = pl.BlockSpec((None, sup, dk), lambda b, h, i: (b, _chunk_at(nsup - 1 - i, reverse, nsc, nsup), h))
    return pl.pallas_call(
        body,
        name=name,
        grid=(nb, nh, nsup),
        in_specs=[blk] * 4 + [pl.BlockSpec((None, None, cps, dk, dk), lambda b, h, i: (b, h, nsup - 1 - i, 0, 0)), blk],
        out_specs=[blk] * 4,
        out_shape=[jax.ShapeDtypeStruct(q.shape, F32)] * 4,
        scratch_shapes=[pltpu.VMEM((dk, dk), F32)],
        compiler_params=_cparams(("parallel", "parallel", "arbitrary")),
    )(q, k, lf, v, hst, dy)


def f_s5_discretize(ins):
    lam_re, lam_im, log_step, b_re, b_im = ins
    step = jnp.exp(log_step)
    mag = jnp.exp(lam_re * step)
    ar, ai = mag * jnp.cos(lam_im * step), mag * jnp.sin(lam_im * step)
    den = lam_re * lam_re + lam_im * lam_im
    zr = ((ar - 1.0) * lam_re + ai * lam_im) / den
    zi = (ai * lam_re - (ar - 1.0) * lam_im) / den
    return [ar, ai, zr * b_re - zi * b_im, zr * b_im + zi * b_re]


def s5_tiles_of(m):
    g, p, k = m.shape
    gt = S5_TILE_CH // k
    eye = jnp.eye(gt, dtype=m.dtype)
    t = m.reshape(g // gt, gt, p, 1, k) * eye[None, :, None, :, None]
    return t.reshape(g // gt, gt * p, gt * k)


def s5_groups_of(t, g, p, k):
    gt = S5_TILE_CH // k
    eye = jnp.eye(gt, dtype=t.dtype)
    return jnp.sum(t.reshape(g // gt, gt, p, gt, k) * eye[None, :, None, :, None], axis=3).reshape(g, p, k)


def f_lower_bounds(ins):
    (logits,) = ins
    e = jnp.exp(logits - jnp.max(logits, axis=0, keepdims=True))
    p = e / jnp.sum(e, axis=0, keepdims=True)
    n = logits.shape[0]
    li = lax.broadcasted_iota(jnp.int32, (n, n), 0)
    si = lax.broadcasted_iota(jnp.int32, (n, n), 1)
    after_first = jnp.where(jnp.logical_and(si >= 1, si <= li), 1.0, 0.0)
    return [jnp.dot(after_first, p, precision=_HI, preferred_element_type=F32)]


def f_silu(ins):
    return [_silu(ins[0])]


def f_norm_keep(shift_row, scale_row):
    def f(tv, mv, pv):
        return [tv[0], _rms(tv[0], pv[0]) * (1.0 + mv[scale_row]) + mv[shift_row]]
    return f


def f_dt(tv, mv, pv):
    return [_softplus(tv[0] + pv[0])]


def f_even_finish(tv, mv, pv):
    y_f, y_b, xs, z, h_sum, gy = tv
    d_exp, g = pv
    y = _rms((y_f + y_b + d_exp * xs) * _silu(z), g)
    return [y, h_sum * jax.nn.gelu(gy)]


def f_odd_prep(tv, mv, pv):
    q, f_f, f_b = tv
    (lb,) = pv
    outs = [_silu(q)]
    for f in (f_f, f_b):
        outs.append((1.0 - lb) * jax.nn.sigmoid(-f))
        outs.append(jnp.log(lb + (1.0 - lb) * jax.nn.sigmoid(f)))
    return outs


def f_odd_finish(tv, mv, pv):
    o_f, o_b, g, s5y, u = tv
    norm_g, s5_d, glu_w, glu_b = pv
    o = o_f + o_b
    w = o.shape[1]
    hi = lax.broadcasted_iota(jnp.int32, (w, w), 0) // HG_DK
    hj = lax.broadcasted_iota(jnp.int32, (w, w), 1) // HG_DK
    head_mean = jnp.where(hi == hj, 1.0 / HG_DK, 0.0)
    ms = jnp.dot(o * o, head_mean, precision=_HI, preferred_element_type=F32)
    on = o * lax.rsqrt(ms + RMS_EPS) * norm_g * _silu(g)
    y = jax.nn.gelu(s5y + s5_d * u)
    gate = jax.nn.sigmoid(jnp.dot(y.astype(BF16), glu_w.astype(BF16), preferred_element_type=F32) + glu_b)
    return [on, y * gate]


def final_loss(name, s, br, mod, g, target, s_ctx):
    nb, st, d = s.shape
    tb = TOK_BLOCK
    assert s_ctx == tb

    def lossf(sv, bv, gate, gv, tv):
        y = _rms(sv + gate * bv, gv)
        err = jnp.square(y - tv)
        return 0.5 * jnp.sum(jnp.mean(err, axis=-1, keepdims=True), axis=0, keepdims=True)

    def body(s_ref, b_ref, m_ref, g_ref, t_ref, l_ref, ds_ref, db_ref, dm_ref, dg_ref):
        b, t = pl.program_id(0), pl.program_id(1)

        @pl.when(t == 0)
        def _():
            ds_ref[...] = jnp.zeros_like(ds_ref)
            db_ref[...] = jnp.zeros_like(db_ref)
            dm_ref[...] = jnp.zeros_like(dm_ref)
            l_ref[...] = jnp.zeros_like(l_ref)

        @pl.when(jnp.logical_and(b == 0, t == 0))
        def _():
            dg_ref[...] = jnp.zeros_like(dg_ref)

        @pl.when(t > 0)
        def _():
            gate = m_ref[N_MOD - 1:N_MOD, :]
            l, vjp = jax.vjp(lossf, s_ref[...], b_ref[...], gate, g_ref[...], t_ref[...])
            ds, db, dgate, dg, _ = vjp(jnp.ones((1, 1), F32))
            ds_ref[...] = ds
            db_ref[...] = db.astype(db_ref.dtype)
            dg_ref[...] += dg
            l_ref[...] += jnp.broadcast_to(l, l_ref.shape)

            @pl.when(t == 1)
            def _():
                dm_ref[...] = jnp.zeros_like(dm_ref)
                dm_ref[N_MOD - 1:N_MOD, :] = dgate

            @pl.when(t > 1)
            def _():
                dm_ref[N_MOD - 1:N_MOD, :] += dgate

    tok = pl.BlockSpec((None, tb, d), lambda b, t: (b, t, 0))
    modspec = pl.BlockSpec((None, N_MOD, d), _mod_index)
    gspec = pl.BlockSpec((1, d), lambda b, t: (0, 0))
    return pl.pallas_call(
        body,
        name=name,
        grid=(nb, st // tb),
        in_specs=[tok, tok, modspec, gspec, pl.BlockSpec((None, tb, d), lambda b, t: (b, jnp.maximum(t - 1, 0), 0))],
        out_specs=[pl.BlockSpec((None, SUBLANES, 128), lambda b, t: (b, 0, 0)), tok, tok, modspec, gspec],
        out_shape=[jax.ShapeDtypeStruct((nb, SUBLANES, 128), F32), jax.ShapeDtypeStruct(s.shape, F32),
                   jax.ShapeDtypeStruct(s.shape, BF16), jax.ShapeDtypeStruct(mod.shape, F32), jax.ShapeDtypeStruct(g.shape, F32)],
        compiler_params=_cparams(("arbitrary", "arbitrary")),
    )(s, br, mod, g, target)


def adamw(name, w, g, m, v):
    shape = w.shape
    cols = shape[-1] if w.ndim >= 2 else w.size
    rows = w.size // cols
    tr = _pick(rows, (512, 256, 128, 64, 32, 16, 8))

    def body(w_ref, g_ref, m_ref, v_ref, d_ref, nm_ref, nv_ref):
        gv = g_ref[...]
        nm = ADAM_B1 * m_ref[...] + (1.0 - ADAM_B1) * gv
        nv = ADAM_B2 * v_ref[...] + (1.0 - ADAM_B2) * jnp.square(gv)
        m_hat = nm / (1.0 - ADAM_B1 ** ADAM_STEP)
        v_hat = nv / (1.0 - ADAM_B2 ** ADAM_STEP)
        d_ref[...] = -ADAM_LR * (m_hat / (jnp.sqrt(v_hat) + ADAM_EPS) + ADAM_WD * w_ref[...])
        nm_ref[...] = nm
        nv_ref[...] = nv

    spec = pl.BlockSpec((tr, cols), lambda i: (i, 0))
    outs = pl.pallas_call(
        body,
        name=name,
        grid=(rows // tr,),
        in_specs=[spec] * 4,
        out_specs=[spec] * 3,
        out_shape=[jax.ShapeDtypeStruct((rows, cols), F32)] * 3,
        compiler_params=_cparams(("parallel",)),
    )(*(a.reshape(rows, cols) for a in (w, g, m, v)))
    return tuple(o.reshape(shape) for o in outs)


EV_COLS = {"z": (0, 1024), "xbc": (1024, 2560), "dt": (2560, 2592), "gy": (2592, 3616), "u": (3616, 4640)}
OD_COLS = {"q": (0, 768), "ff": (768, 1536), "fb": (1536, 2304), "v": (2304, 3072), "g": (3072, 3840), "u": (3840, 4096)}
EV_OUT_ROWS = ((0, 1024), (1024, 2048))
OD_OUT_ROWS = ((0, 768), (768, 1024))
LANES = 128


def _pad_to_lanes(w):
    n = w.shape[1]
    return w if n % LANES == 0 else jnp.pad(w, ((0, 0), (0, LANES - n % LANES)))


def _layer_weights(l, big):
    j = l // 2
    even = l % 2 == 0
    w_in = big["ev_w_in" if even else "od_w_in"][j]
    w_out = big["ev_w_out" if even else "od_w_out"][j]
    lw = {"in": {}, "out": []}
    for name, (a, b) in (EV_COLS if even else OD_COLS).items():
        w = _pad_to_lanes(w_in[:, a:b])
        lw["in"][name] = (w, w.T)
    for a, b in (EV_OUT_ROWS if even else OD_OUT_ROWS):
        lw["out"].append((w_out[a:b], w_out[a:b].T))
    for name in ("gate", "up", "down"):
        w = big["ffn_w_" + name][l]
        lw[name] = (w, w.T)
    return lw


def _rows2d(a):
    return a.reshape(-1, a.shape[-1])


def _mm3(a, w, name, out_dtype=F32):
    return mm([(_rows2d(a), w)], name, out_dtype).reshape(a.shape[:-1] + (w.shape[1],))


def _wgrad(a, d, name):
    return mm_tn(_rows2d(a), _rows2d(d), name)


def _dgrad(pairs, name, shape3):
    return mm([(_rows2d(d), wt) for d, wt in pairs], name).reshape(shape3[:-1] + (pairs[0][1].shape[1],))


def _dir_dt(dt, d):
    nb, s, _ = dt.shape
    dd = dt[:, :, SSD_HEADS * d:SSD_HEADS * (d + 1)].reshape(nb, s, SSD_GROUPS, SSD_HPG)
    return jnp.transpose(dd, (0, 2, 1, 3)), jnp.transpose(dd, (0, 2, 3, 1))


def _s5_prepare(p, j, tag):
    g_, p_, k_ = S5_GROUPS, S5_STATE, S5_GROUP_CH
    col = lambda t: t.reshape(g_ * p_, 1)
    ins, outs = [], []
    for d in (0, 1):
        i_d = [col(p["s5_lam_re"][j, d]), col(p["s5_lam_im"][j, d]), col(jnp.repeat(p["s5_log_step"][j, d], p_)),
               p["s5_b_re"][j].reshape(g_ * p_, k_), p["s5_b_im"][j].reshape(g_ * p_, k_)]
        ins.append(i_d)
        outs.append(small_fwd(f"{tag}_disc{d}", f_s5_discretize, i_d, [(g_ * p_, 1)] * 2 + [(g_ * p_, k_)] * 2))
    lam_r = jnp.stack([o[0].reshape(1, g_ * p_) for o in outs])
    lam_i = jnp.stack([o[1].reshape(1, g_ * p_) for o in outs])
    bt_r = jnp.stack([s5_tiles_of(o[2].reshape(g_, p_, k_)) for o in outs])
    bt_i = jnp.stack([s5_tiles_of(o[3].reshape(g_, p_, k_)) for o in outs])
    ct_r = jnp.stack([s5_tiles_of(jnp.transpose(p["s5_c_re"][j, d], (0, 2, 1))) for d in (0, 1)])
    ct_i = jnp.stack([s5_tiles_of(jnp.transpose(p["s5_c_im"][j, d], (0, 2, 1))) for d in (0, 1)])
    return ins, (lam_r, lam_i, bt_r, bt_i, ct_r, ct_i)


def _s5_param_grads(ins, grads, tag):
    g_, p_, k_ = S5_GROUPS, S5_STATE, S5_GROUP_CH
    dlr, dli, dbtr, dbti, dctr, dcti = grads
    g_lre, g_lim, g_ls, g_bre, g_bim = [], [], [], 0.0, 0.0
    for d in (0, 1):
        cots = [dlr[d].reshape(g_ * p_, 1), dli[d].reshape(g_ * p_, 1),
                s5_groups_of(dbtr[d], g_, p_, k_).reshape(g_ * p_, k_), s5_groups_of(dbti[d], g_, p_, k_).reshape(g_ * p_, k_)]
        g = small_bwd(f"{tag}_disc_bwd{d}", f_s5_discretize, ins[d], cots)
        g_lre.append(g[0].reshape(g_, p_))
        g_lim.append(g[1].reshape(g_, p_))
        g_ls.append(g[2].reshape(g_, p_).sum(-1))
        g_bre = g_bre + g[3].reshape(g_, p_, k_)
        g_bim = g_bim + g[4].reshape(g_, p_, k_)
    g_cre = jnp.stack([jnp.transpose(s5_groups_of(dctr[d], g_, p_, k_), (0, 2, 1)) for d in (0, 1)])
    g_cim = jnp.stack([jnp.transpose(s5_groups_of(dcti[d], g_, p_, k_), (0, 2, 1)) for d in (0, 1)])
    return jnp.stack(g_lre), jnp.stack(g_lim), jnp.stack(g_ls), g_bre, g_bim, g_cre, g_cim


def _even_mixer_fwd(l, hn, p, lw, s_ctx):
    j = l // 2
    t1 = taps_1d(4, s_ctx, hn.shape[1])
    r = {"hn": hn}
    proj = {n: _mm3(hn, lw["in"][n][0], f"l{l}_proj_{n}") for n in EV_COLS}
    r["z"], r["xbc"], r["gy"], r["u"] = proj["z"], proj["xbc"], proj["gy"], proj["u"]
    r["dtp"] = proj["dt"][:, :, :2 * SSD_HEADS]
    r["xbc_c"] = conv_fwd(f"l{l}_ssd_conv", r["xbc"], p["ssd_conv_w"][j], p["ssd_conv_b"][j][None], t1, "silu")
    r["u_c"] = conv_fwd(f"l{l}_lru_conv", r["u"], p["lru_conv_w"][j], p["lru_conv_b"][j][None], t1, "none")
    r["dt_bias"] = p["ssd_dt_bias"][j].reshape(1, 2 * SSD_HEADS)
    (r["dt"],) = tok_fwd(f"l{l}_dt", f_dt, [r["dtp"]], None, [r["dt_bias"]], [2 * SSD_HEADS], [F32])
    r["ys"], r["hst"], r["dts"], r["alog"] = [], [], [], []
    for d in (0, 1):
        dtc, dtr = _dir_dt(r["dt"], d)
        al = p["ssd_a_log"][j, d].reshape(SSD_GROUPS, SSD_HPG)
        al_r, al_c = al[:, None, :], al[:, :, None]
        y, hst = ssd_fwd(f"l{l}_ssd_fwd{d}", r["xbc_c"], dtc, dtr, al_r, al_c, bool(d), s_ctx)
        r["ys"].append(y)
        r["hst"].append(hst)
        r["dts"].append((dtc, dtr))
        r["alog"].append((al_r, al_c))
    v4 = lambda t: t.reshape(2, LRU_BLOCKS, 1, LRU_BLOCK_W)
    r["lru_p"] = (p["lru_w_a"][j], v4(p["lru_b_a"][j]), p["lru_w_i"][j], v4(p["lru_b_i"][j]), v4(p["lru_lam"][j]))
    r["h_sum"] = lru_fwd(f"l{l}_lru_fwd", r["u_c"], *r["lru_p"], s_ctx)
    r["xs"] = r["xbc_c"][:, :, :SSD_HEADS * SSD_HEAD_DIM]
    r["fin_p"] = [jnp.repeat(p["ssd_d"][j], SSD_HEAD_DIM)[None], p["ssd_norm_g"][j][None]]
    r["fin_in"] = [r["ys"][0], r["ys"][1], r["xs"], r["z"], r["h_sum"], r["gy"]]
    r["o"] = tok_fwd(f"l{l}_even_finish", f_even_finish, r["fin_in"], None, r["fin_p"], [1024, 1024], [BF16, BF16])
    return r


def _even_mixer_bwd(l, r, dox, p, lw, s_ctx, grads):
    j = l // 2
    shape3 = dox.shape
    t1 = taps_1d(4, s_ctx, shape3[1])
    grads["ev_w_out"][j] = jnp.concatenate([_wgrad(o, dox, f"l{l}_dwout{i}") for i, o in enumerate(r["o"])], axis=0)
    do = [_dgrad([(dox, lw["out"][i][1])], f"l{l}_dout{i}", shape3) for i in range(2)]
    (dy, _, dxs, dz, dh_sum, dgy), _, (dd_exp, grads["ssd_norm_g"][j]) = tok_bwd(
        f"l{l}_even_finish_bwd", f_even_finish, r["fin_in"], None, r["fin_p"], do, [F32, F32, F32, BF16, F32, BF16])
    grads["ssd_d"][j] = dd_exp.reshape(SSD_HEADS, SSD_HEAD_DIM).sum(-1)
    du_c, dwa, dba, dwi, dbi, dlam = lru_bwd(f"l{l}_lru_bwd", r["u_c"], *r["lru_p"], dh_sum, s_ctx)
    grads["lru_w_a"][j], grads["lru_w_i"][j] = dwa, dwi
    v2 = lambda t: t.reshape(2, LRU_BLOCKS * LRU_BLOCK_W)
    grads["lru_b_a"][j], grads["lru_b_i"][j], grads["lru_lam"][j] = v2(dba), v2(dbi), v2(dlam)
    dx_sum, dbm_sum, dcm_sum, ddts, dalog = dxs, 0.0, 0.0, [], []
    for d in (0, 1):
        dx, dbm, dcm, ddtc, ddtr, dar, dac = ssd_bwd(
            f"l{l}_ssd_bwd{d}", r["xbc_c"], *r["dts"][d], *r["alog"][d], r["hst"][d], dy, bool(d), s_ctx)
        dx_sum, dbm_sum, dcm_sum = dx_sum + dx, dbm_sum + dbm, dcm_sum + dcm
        ddts.append((jnp.transpose(ddtc, (0, 2, 1, 3)) + jnp.transpose(ddtr, (0, 3, 1, 2))).reshape(shape3[0], shape3[1], SSD_HEADS))
        dalog.append((dar.sum(0)[:, 0, :] + dac.sum(0)[:, :, 0]).reshape(SSD_HEADS))
    grads["ssd_a_log"][j] = jnp.stack(dalog)
    dxbc_c = jnp.concatenate([dx_sum, dbm_sum, dcm_sum], axis=-1)
    (ddtp,), _, (ddt_bias,) = tok_bwd(f"l{l}_dt_bwd", f_dt, [r["dtp"]], None, [r["dt_bias"]], [jnp.concatenate(ddts, axis=-1)], [F32])
    grads["ssd_dt_bias"][j] = ddt_bias.reshape(2, SSD_HEADS)
    dxbc, grads["ssd_conv_w"][j], dcb = conv_bwd(f"l{l}_ssd_conv_bwd", r["xbc"], p["ssd_conv_w"][j], p["ssd_conv_b"][j][None], dxbc_c, t1, "silu", dx_dtype=BF16)
    du, grads["lru_conv_w"][j], dlb = conv_bwd(f"l{l}_lru_conv_bwd", r["u"], p["lru_conv_w"][j], p["lru_conv_b"][j][None], du_c, t1, "none", dx_dtype=BF16)
    grads["ssd_conv_b"][j], grads["lru_conv_b"][j] = dcb[0], dlb[0]
    dproj = {"z": dz, "xbc": dxbc, "dt": _pad_to_lanes(_rows2d(ddtp)).reshape(shape3[:2] + (LANES,)), "gy": dgy, "u": du}
    grads["ev_w_in"][j] = jnp.concatenate(
        [_wgrad(r["hn"], dproj[n], f"l{l}_dwin_{n}")[:, :b - a] for n, (a, b) in EV_COLS.items()], axis=1)
    return _dgrad([(dproj[n], lw["in"][n][1]) for n in EV_COLS], f"l{l}_dhn", shape3)


def _odd_mixer_fwd(l, hn, p, lw, lb_row, s_ctx):
    j = l // 2
    r = {"hn": hn}
    proj = {n: _mm3(hn, lw["in"][n][0], f"l{l}_proj_{n}") for n in OD_COLS}
    r["v"], r["g"], r["u"] = proj["v"], proj["g"], proj["u"]
    r["prep_in"] = [proj["q"], proj["ff"], proj["fb"]]
    r["lb"] = lb_row
    r["prep"] = tok_fwd(f"l{l}_odd_prep", f_odd_prep, r["prep_in"], None, [lb_row], [HG_W] * 5, [F32] * 5)
    qs = r["prep"][0]
    r["os"], r["hst"] = [], []
    for d in (0, 1):
        o, hst = hg_fwd(f"l{l}_hg_fwd{d}", qs, r["prep"][1 + 2 * d], r["prep"][2 + 2 * d], r["v"], bool(d), s_ctx)
        r["os"].append(o)
        r["hst"].append(hst)
    r["s5_ins"], r["s5_p"] = _s5_prepare(p, j, f"l{l}_s5")
    r["s5y"] = s5_fwd(f"l{l}_s5_fwd", r["u"], *r["s5_p"], s_ctx)
    r["fin_p"] = [p["hg_norm_g"][j].reshape(1, HG_W), p["s5_d"][j][None], p["s5_glu_w"][j], p["s5_glu_b"][j][None]]
    r["fin_in"] = [r["os"][0], r["os"][1], r["g"], r["s5y"], r["u"]]
    r["o"] = tok_fwd(f"l{l}_odd_finish", f_odd_finish, r["fin_in"], None, r["fin_p"], [HG_W, S5_W], [BF16, BF16])
    return r


def _odd_mixer_bwd(l, r, dox, p, lw, s_ctx, grads):
    j = l // 2
    shape3 = dox.shape
    grads["od_w_out"][j] = jnp.concatenate([_wgrad(o, dox, f"l{l}_dwout{i}") for i, o in enumerate(r["o"])], axis=0)
    do = [_dgrad([(dox, lw["out"][i][1])], f"l{l}_dout{i}", shape3) for i in range(2)]
    (do_hg, _, dg, ds5y, du_fin), _, (dng, grads["s5_d"][j], grads["s5_glu_w"][j], dglu_b) = tok_bwd(
        f"l{l}_odd_finish_bwd", f_odd_finish, r["fin_in"], None, r["fin_p"], do, [F32, F32, BF16, F32, F32])
    grads["hg_norm_g"][j] = dng.reshape(HG_HEADS, HG_DK)
    grads["s5_d"][j], grads["s5_glu_b"][j] = grads["s5_d"][j][0], dglu_b[0]
    s5g = s5_bwd(f"l{l}_s5_bwd", r["u"], *r["s5_p"], ds5y, s_ctx)
    du = s5g[0] + du_fin
    (grads["s5_lam_re"][j], grads["s5_lam_im"][j], grads["s5_log_step"][j], grads["s5_b_re"][j], grads["s5_b_im"][j],
     grads["s5_c_re"][j], grads["s5_c_im"][j]) = _s5_param_grads(r["s5_ins"], s5g[1:], f"l{l}_s5")
    qs = r["prep"][0]
    dqs, dv, dprep = 0.0, 0.0, [None] * 5
    for d in (0, 1):
        dq, dk, dlf, dvd = hg_bwd(f"l{l}_hg_bwd{d}", qs, r["prep"][1 + 2 * d], r["prep"][2 + 2 * d], r["v"], r["hst"][d], do_hg, bool(d), s_ctx)
        dqs, dv = dqs + dq, dv + dvd
        dprep[1 + 2 * d], dprep[2 + 2 * d] = dk, dlf
    dprep[0] = dqs
    (dq_, dff, dfb), _, (dlb,) = tok_bwd(f"l{l}_odd_prep_bwd", f_odd_prep, r["prep_in"], None, [r["lb"]], dprep, [BF16] * 3)
    dproj = {"q": dq_, "ff": dff, "fb": dfb, "v": dv, "g": dg, "u": du}
    grads["od_w_in"][j] = jnp.concatenate([_wgrad(r["hn"], dproj[n], f"l{l}_dwin_{n}") for n in OD_COLS], axis=1)
    return _dgrad([(dproj[n], lw["in"][n][1]) for n in OD_COLS], f"l{l}_dhn", shape3), dlb


def _ffn_fwd(l, fn, p, lw, s_ctx):
    r = {"fn": fn}
    tg = taps_grid(s_ctx, fn.shape[1], GRID_W)
    r["a"] = _mm3(fn, lw["gate"][0], f"l{l}_ffn_gate")
    r["up"] = _mm3(fn, lw["up"][0], f"l{l}_ffn_up")
    r["cw"], r["cb"] = p["ffn_conv_w"][l].reshape(9, D_FF), p["ffn_conv_b"][l][None]
    r["act"] = conv_fwd(f"l{l}_ffn_conv", r["a"], r["cw"], r["cb"], tg, "silu_mul", mul=r["up"], out_dtype=BF16)
    return r, _mm3(r["act"], lw["down"][0], f"l{l}_ffn_down")


def _ffn_bwd(l, r, dfo, lw, s_ctx, grads):
    shape3 = dfo.shape
    tg = taps_grid(s_ctx, shape3[1], GRID_W)
    grads["ffn_w_down"][l] = _wgrad(r["act"], dfo, f"l{l}_dwdown")
    dact = _dgrad([(dfo, lw["down"][1])], f"l{l}_dact", shape3)
    da, dcw, dcb, dup = conv_bwd(f"l{l}_ffn_conv_bwd", r["a"], r["cw"], r["cb"], dact, tg, "silu_mul", mul=r["up"], dx_dtype=BF16)
    grads["ffn_conv_w"][l], grads["ffn_conv_b"][l] = dcw.reshape(3, 3, D_FF), dcb[0]
    grads["ffn_w_gate"][l] = _wgrad(r["fn"], da, f"l{l}_dwgate")
    grads["ffn_w_up"][l] = _wgrad(r["fn"], dup, f"l{l}_dwup")
    return _dgrad([(da, lw["gate"][1]), (dup, lw["up"][1])], f"l{l}_dfn", shape3)


BIG_WEIGHTS = ("ev_w_in", "ev_w_out", "od_w_in", "od_w_out", "ffn_w_gate", "ffn_w_up", "ffn_w_down")
PER_LAYER = {"norm_mix_g": DEPTH, "norm_ffn_g": DEPTH, "ffn_w_gate": DEPTH, "ffn_w_up": DEPTH, "ffn_conv_w": DEPTH,
             "ffn_conv_b": DEPTH, "ffn_w_down": DEPTH}


def local_step(x, ctx, target, modtabs, p, big, s_ctx=CTX_LEN):
    d_model = x.shape[-1]
    s0 = jnp.concatenate([ctx, x], axis=1)
    lws = [_layer_weights(l, big) for l in range(DEPTH)]
    shapes = {n: v.shape for n, v in {**p, **big}.items()}
    grads = {n: [None] * PER_LAYER.get(n, DEPTH // 2) for n in shapes if n not in ("c_ctx", "w_mod", "b_mod", "final_norm_g", "hg_lb_logits")}
    (lbs,) = small_fwd("lower_bounds", f_lower_bounds, [p["hg_lb_logits"]], [p["hg_lb_logits"].shape])
    tab_a = [modtabs[0]] + [modtabs[l].at[:, N_MOD - 1].set(modtabs[l - 1][:, N_MOD - 1]) for l in range(1, DEPTH)]
    res = []
    s, br = s0, None
    for l in range(DEPTH):
        r = {}
        g_mix, g_ffn = p["norm_mix_g"][l][None], p["norm_ffn_g"][l][None]
        if l == 0:
            (hn,) = tok_fwd("l0_norm", f_norm(0, 1), [s], tab_a[0], [g_mix], [d_model], [BF16])
            r["a_in"] = [s]
        else:
            r["a_in"] = [s, br]
            s, hn = tok_fwd(f"l{l}_resnorm_a", f_resnorm(5, 0, 1), r["a_in"], tab_a[l], [g_mix], [d_model] * 2, [F32, BF16])
        if l % 2 == 0:
            r["mix"] = _even_mixer_fwd(l, hn, p, lws[l], s_ctx)
        else:
            r["mix"] = _odd_mixer_fwd(l, hn, p, lws[l], lbs[l:l + 1], s_ctx)
        ox = mm([(_rows2d(o), w) for o, (w, _) in zip(r["mix"]["o"], lws[l]["out"])], f"l{l}_mix_out").reshape(s.shape)
        r["b_in"] = [s, ox]
        s, fn = tok_fwd(f"l{l}_resnorm_b", f_resnorm(2, 3, 4), r["b_in"], modtabs[l], [g_ffn], [d_model] * 2, [F32, BF16])
        r["ffn"], br = _ffn_fwd(l, fn, p, lws[l], s_ctx)
        res.append(r)

    loss_blk, ds, dbr, dtab_f, dfinal_g = final_loss("final_loss", s, br, modtabs[DEPTH - 1], p["final_norm_g"][None], target, s_ctx)
    grads["final_norm_g"] = dfinal_g[0]
    dmod = [None] * DEPTH
    dtab_next = dtab_f
    dlb = jnp.zeros_like(lbs)
    for l in reversed(range(DEPTH)):
        r = res[l]
        g_mix, g_ffn = p["norm_mix_g"][l][None], p["norm_ffn_g"][l][None]
        dfn = _ffn_bwd(l, r["ffn"], dbr, lws[l], s_ctx, grads)
        (ds, dox), dtab_b, (grads["norm_ffn_g"][l],) = tok_bwd(
            f"l{l}_resnorm_b_bwd", f_resnorm(2, 3, 4), r["b_in"], modtabs[l], [g_ffn], [ds, dfn], [F32, BF16])
        if l % 2 == 0:
            dhn = _even_mixer_bwd(l, r["mix"], dox, p, lws[l], s_ctx, grads)
        else:
            dhn, dlb_l = _odd_mixer_bwd(l, r["mix"], dox, p, lws[l], s_ctx, grads)
            dlb = dlb.at[l:l + 1].set(dlb_l)
        if l == 0:
            (ds,), dtab_a, (dg,) = tok_bwd("l0_norm_bwd", f_norm_keep(0, 1), r["a_in"], tab_a[0], [g_mix], [ds, dhn], [F32])
        else:
            (ds, dbr), dtab_a, (dg,) = tok_bwd(
                f"l{l}_resnorm_a_bwd", f_resnorm(5, 0, 1), r["a_in"], tab_a[l], [g_mix], [ds, dhn], [F32, BF16])
        grads["norm_mix_g"][l] = dg
        dmod[l] = (dtab_a.at[:, N_MOD - 1].set(0.0) + dtab_b).at[:, N_MOD - 1].set(dtab_next[:, N_MOD - 1])
        dtab_next = dtab_a
    (grads["hg_lb_logits"],) = small_bwd("lower_bounds_bwd", f_lower_bounds, [p["hg_lb_logits"]], [dlb])
    out = {}
    for n, g in grads.items():
        if isinstance(g, list):
            g = jnp.stack([t.reshape(shapes[n][1:]) for t in g])
        out[n] = g.reshape(shapes[n])
    return loss_blk[:, 0, 0], ds[:, s_ctx:], dmod, out


WEIGHT_NAMES = (
    "c_ctx", "w_mod", "b_mod", "norm_mix_g", "norm_ffn_g", "final_norm_g", "ev_w_in", "ev_w_out", "ssd_conv_w",
    "ssd_conv_b", "ssd_dt_bias", "ssd_a_log", "ssd_d", "ssd_norm_g", "lru_conv_w", "lru_conv_b", "lru_w_a", "lru_b_a",
    "lru_w_i", "lru_b_i", "lru_lam", "od_w_in", "od_w_out", "hg_lb_logits", "hg_norm_g", "s5_lam_re", "s5_lam_im",
    "s5_log_step", "s5_b_re", "s5_b_im", "s5_c_re", "s5_c_im", "s5_d", "s5_glu_w", "s5_glu_b", "ffn_w_gate", "ffn_w_up",
    "ffn_conv_w", "ffn_conv_b", "ffn_w_down")
INPUT_NAMES = ("x", "c", "ctx") + WEIGHT_NAMES + ("loss_target",) + tuple("m_" + n for n in WEIGHT_NAMES) + tuple("v_" + n for n in WEIGHT_NAMES)
SHARD_AXIS = {"w_mod": 2, "ev_w_in": 2, "ev_w_out": 1, "ssd_conv_w": 2, "lru_conv_w": 2, "lru_b_a": 2, "lru_b_i": 2,
              "lru_lam": 2, "od_w_in": 2, "od_w_out": 1, "s5_d": 1, "s5_glu_w": 1, "s5_glu_b": 1, "ffn_w_gate": 2,
              "ffn_w_up": 2, "ffn_conv_w": 3, "ffn_w_down": 1}
SMALL_SHARDED = tuple(n for n in WEIGHT_NAMES if n in SHARD_AXIS and n not in BIG_WEIGHTS and n != "w_mod")
REPLICATED_LOCAL = tuple(n for n in WEIGHT_NAMES if n not in SHARD_AXIS and n not in ("c_ctx", "b_mod"))
PACK_WIDTH = 1024
MOD_ROWS = 48
CTX_ROW = 32


def _unshard(g8, axis):
    moved = jnp.moveaxis(g8, 0, axis)
    shp = moved.shape
    return moved.reshape(shp[:axis] + (shp[axis] * shp[axis + 1],) + shp[axis + 2:])


def _to_shards(full, axis):
    shp = full.shape
    return jnp.moveaxis(full.reshape(shp[:axis] + (N_DEV, shp[axis] // N_DEV) + shp[axis + 1:]), axis, 0)


def _pack(arrs, dtype, lead=(), row_mult=16):
    flat = jnp.concatenate([a.astype(dtype).reshape(lead + (-1,)) for a in arrs], axis=-1)
    n = flat.shape[-1]
    unit = row_mult * PACK_WIDTH
    padded = -(-n // unit) * unit
    flat = jnp.pad(flat, [(0, 0)] * len(lead) + [(0, padded - n)])
    return flat.reshape(lead + (padded // PACK_WIDTH, PACK_WIDTH))


def _unpack(packed, shapes, lead=()):
    flat = packed.reshape(lead + (-1,))
    out, off = [], 0
    for shp in shapes:
        n = math.prod(shp)
        out.append(flat[..., off:off + n].reshape(lead + tuple(shp)))
        off += n
    return out


def _my_block(full, axis, me):
    loc = full.shape[axis] // N_DEV
    return lax.dynamic_slice_in_dim(full, me * loc, loc, axis)


def kernel(*args):
    a = dict(zip(INPUT_NAMES, args))
    px, py, pc = _my_pos()
    me = 4 * px + 2 * py + pc
    nb = a["x"].shape[0]

    small_names = ("c",) + SMALL_SHARDED
    *big8, small8 = all_gather([a[n].astype(BF16) for n in BIG_WEIGHTS] + [_pack([a[n] for n in small_names], F32)],
                               "gather_weights")
    big = {n: _unshard(g, SHARD_AXIS[n]) for n, g in zip(BIG_WEIGHTS, big8)}
    small = dict(zip(small_names, _unpack(small8, [a[n].shape for n in small_names], (N_DEV,))))
    p = {n: a[n] for n in WEIGHT_NAMES if n not in SHARD_AXIS}
    for n in SMALL_SHARDED:
        p[n] = _unshard(small[n], SHARD_AXIS[n])
    c_all = small["c"].reshape(N_DEV * nb, D_MODEL)

    rows = jnp.concatenate([c_all, a["c_ctx"][None], jnp.zeros((MOD_ROWS - CTX_ROW - 1, D_MODEL), F32)], axis=0)
    (srows,) = small_fwd("mod_silu", f_silu, [rows], [rows.shape])
    wmod2d = jnp.transpose(a["w_mod"], (1, 0, 2)).reshape(D_MODEL, -1).astype(BF16)
    cols = a["w_mod"].shape[2]
    mod_loc = mm([(srows, wmod2d)], "mod_proj")
    mod8 = all_gather([mod_loc], "gather_mod")[0].reshape(N_DEV, MOD_ROWS, DEPTH, cols)
    mod_all = jnp.transpose(mod8, (2, 1, 0, 3)).reshape(DEPTH, MOD_ROWS, N_DEV * cols) + a["b_mod"][:, None, :]
    modtabs = []
    for l in range(DEPTH):
        mine = lax.dynamic_slice_in_dim(mod_all[l], me * nb, nb, 0).reshape(nb, N_MOD, D_MODEL)
        ctx_row = jnp.broadcast_to(mod_all[l, CTX_ROW].reshape(1, N_MOD, D_MODEL), (nb, N_MOD, D_MODEL))
        modtabs.append(jnp.stack([ctx_row, mine], axis=1).reshape(2 * nb, N_MOD, D_MODEL))

    loss_b, grad_x, dmod, grads = local_step(a["x"], a["ctx"], a["loss_target"], modtabs, p, big)

    dm = jnp.stack([t.reshape(nb, 2, N_MOD * D_MODEL) for t in dmod])
    dloc = jnp.concatenate([dm[:, :, 1], jnp.sum(dm[:, :, 0], axis=1, keepdims=True),
                            jnp.zeros((DEPTH, SUBLANES - nb - 1, N_MOD * D_MODEL), F32)], axis=1)
    d8 = all_gather([dloc.reshape(DEPTH * SUBLANES, -1)], "gather_dmod")[0].reshape(N_DEV, DEPTH, SUBLANES, -1)
    d_rows = jnp.transpose(d8[:, :, :nb], (1, 0, 2, 3)).reshape(DEPTH, N_DEV * nb, -1)
    d_ctx = jnp.sum(d8[:, :, nb], axis=0)[:, None]
    d_full = jnp.concatenate([d_rows, d_ctx, jnp.zeros((DEPTH, MOD_ROWS - CTX_ROW - 1, N_MOD * D_MODEL), F32)], axis=1)
    grads["b_mod"] = jnp.sum(d_full, axis=1)
    d_cols = jnp.transpose(_my_block(d_full, 2, me), (1, 0, 2)).reshape(MOD_ROWS, DEPTH * cols)
    g_wmod = mm([(srows.T, d_cols)], "mod_dw")
    g_wmod_local = jnp.transpose(g_wmod.reshape(D_MODEL, DEPTH, cols), (1, 0, 2))
    d_srows_part = mm([(d_cols[CTX_ROW:CTX_ROW + SUBLANES], wmod2d.T)], "mod_dctx")[0]

    reduce_names = REPLICATED_LOCAL + SMALL_SHARDED
    parts = [jnp.sum(loss_b).reshape(1), d_srows_part] + [grads[n] for n in reduce_names]
    packed = _pack(parts, F32, row_mult=SUBLANES * N_DEV)
    own = [_to_shards(grads[n], SHARD_AXIS[n]).astype(BF16).reshape(N_CHIPS, 2, -1, a[n].shape[-1]) for n in BIG_WEIGHTS]
    own.append(packed.reshape(N_CHIPS, 2, -1, PACK_WIDTH))
    names = BIG_WEIGHTS + ("small",)
    from_sibling = sibling_swap(own, "exchange_sibling_grads")
    chip_sums = [pair_sum(o, s, "pair_sum_" + n) for n, o, s in zip(names, own, from_sibling)]
    got = chip_exchange(chip_sums, "exchange_chip_grads")
    eighths = [sum_slots(t, "sum_" + n) for n, t in zip(names, got)]
    (small8,) = all_gather([eighths[-1]], "gather_small_sums")
    totals = _unpack(small8, [(1,), (D_MODEL,)] + [grads[n].shape for n in reduce_names])
    loss = totals[0][0]
    d_srows = jnp.zeros_like(rows).at[CTX_ROW].set(totals[1])
    (d_rows_in,) = small_bwd("mod_silu_bwd", f_silu, [rows], [d_srows])
    g_local = {"c_ctx": d_rows_in[CTX_ROW], "b_mod": grads["b_mod"], "w_mod": g_wmod_local}
    for n, t in zip(reduce_names, totals[2:]):
        g_local[n] = _my_block(t, SHARD_AXIS[n], me) if n in SHARD_AXIS else t
    for n, t in zip(BIG_WEIGHTS, eighths):
        g_local[n] = t.reshape(a[n].shape)

    deltas, new_m, new_v = [], [], []
    for n in WEIGHT_NAMES:
        d, m, v = adamw("adamw_" + n, a[n], g_local[n], a["m_" + n], a["v_" + n])
        deltas.append(d)
        new_m.append(m)
        new_v.append(v)
    return (loss, grad_x, *[g_local[n] for n in WEIGHT_NAMES], *deltas, *new_m, *new_v)
```

```python
import functools
import math

import jax
import jax.numpy as jnp
from jax import lax
from jax.experimental import pallas as pl
from jax.experimental.pallas import tpu as pltpu

F32 = jnp.float32
BF16 = jnp.bfloat16

D_MODEL = 1024
DEPTH = 4
CTX_LEN = 256
SEQ = 2048
S_TOT = CTX_LEN + SEQ
GRID_W = 64
N_MOD = 6
RMS_EPS = 1e-6
N_DEV = 8

SSD_HEADS = 16
SSD_HEAD_DIM = 64
SSD_GROUPS = 2
SSD_HPG = 8
SSD_STATE = 128
SSD_CHUNK = 128
SSD_W = SSD_HEADS * SSD_HEAD_DIM
LRU_BLOCKS = 8
LRU_BLOCK_W = 128
LRU_C = 8.0
HG_W = 768
HG_HEADS = 6
HG_DK = 128
HG_CHUNK = 32
S5_W = 256
S5_GROUPS = 16
S5_GROUP_CH = 16
S5_STATE = 64
D_FF = 2816

ADAM_LR = 0.001
ADAM_B1 = 0.9
ADAM_B2 = 0.999
ADAM_EPS = 1e-08
ADAM_WD = 0.01
ADAM_STEP = 10

TOK_BLOCK = CTX_LEN
SUBLANES = 8
VMEM_LIMIT_BYTES = 56 * 1024 * 1024
MM_BLOCK_BYTES = 8 * 1024 * 1024
MM_TILES = (1408, 1024, 768, 704, 512, 384, 352, 256, 128, 64, 48, 40, 32, 16, 8)
MM_ROW_TILES = (2304, 2048, 1152, 1024, 512, 256, 128, 64, 48, 32, 16, 8)
LANE_TILE = 128


def _cparams(sem=None):
    kw = dict(vmem_limit_bytes=VMEM_LIMIT_BYTES)
    if sem is not None:
        kw["dimension_semantics"] = sem
    return pltpu.CompilerParams(**kw)


def _pick(n, cands):
    for c in cands:
        if n % c == 0:
            return c
    return n


def mm(pairs, name, out_dtype=F32):
    m = pairs[0][0].shape[0]
    n = pairs[0][1].shape[1]
    kdims = [a.shape[1] for a, _ in pairs]
    ktile = None
    if len(pairs) == 1 and kdims[0] > 4096:
        ktile = _pick(kdims[0], (2304, 2048, 1024))
    nk = kdims[0] // ktile if ktile else 1
    col_bytes = sum((ktile or w.shape[0]) * w.dtype.itemsize for _, w in pairs)
    tn = _pick(n, tuple(c for c in MM_TILES if c % LANE_TILE == 0 and c * col_bytes <= MM_BLOCK_BYTES))
    row_bytes = max(sum((ktile or a.shape[1]) * a.dtype.itemsize for a, _ in pairs), tn * 4)
    tm = _pick(m, tuple(c for c in MM_TILES if c * row_bytes <= MM_BLOCK_BYTES))
    npairs = len(pairs)
    if nk > 1:
        assert out_dtype == F32

    def body(*refs):
        o_ref = refs[2 * npairs]
        acc = None
        for i in range(npairs):
            a = refs[2 * i][...].astype(BF16)
            w = refs[2 * i + 1][...].astype(BF16)
            p = jnp.dot(a, w, preferred_element_type=F32)
            acc = p if acc is None else acc + p
        if nk == 1:
            o_ref[...] = acc.astype(out_dtype)
        else:
            k = pl.program_id(2)

            @pl.when(k == 0)
            def _():
                o_ref[...] = acc

            @pl.when(k > 0)
            def _():
                o_ref[...] += acc

    in_specs = []
    args = []
    for a, w in pairs:
        kk = a.shape[1]
        assert w.shape == (kk, n) and a.shape[0] == m, (a.shape, w.shape)
        tk = ktile if ktile else kk
        in_specs.append(pl.BlockSpec((tm, tk), lambda i, j, k: (i, k)))
        in_specs.append(pl.BlockSpec((tk, tn), lambda i, j, k: (k, j)))
        args += [a, w]
    return pl.pallas_call(
        body,
        name=name,
        grid=(m // tm, n // tn, nk),
        in_specs=in_specs,
        out_specs=pl.BlockSpec((tm, tn), lambda i, j, k: (i, j)),
        out_shape=jax.ShapeDtypeStruct((m, n), out_dtype),
        compiler_params=_cparams(("parallel", "parallel", "arbitrary")),
    )(*args)


def mm_tn(a, d, name):
    r, k = a.shape
    n = d.shape[1]
    tr = _pick(r, MM_ROW_TILES)
    lane_ok = lambda c, full: c % LANE_TILE == 0 or c == full
    tk = _pick(k, tuple(c for c in MM_TILES if lane_ok(c, k) and c * tr * a.dtype.itemsize <= MM_BLOCK_BYTES))
    tn = _pick(n, tuple(c for c in MM_TILES if lane_ok(c, n) and c * tr * d.dtype.itemsize <= MM_BLOCK_BYTES
                        and c * tk * 4 <= MM_BLOCK_BYTES))

    def body(a_ref, d_ref, o_ref):
        acc = lax.dot_general(a_ref[...].astype(BF16), d_ref[...].astype(BF16), (((0,), (0,)), ((), ())),
                              preferred_element_type=F32)
        step = pl.program_id(2)

        @pl.when(step == 0)
        def _():
            o_ref[...] = acc

        @pl.when(step > 0)
        def _():
            o_ref[...] += acc

    return pl.pallas_call(
        body,
        name=name,
        grid=(k // tk, n // tn, r // tr),
        in_specs=[pl.BlockSpec((tr, tk), lambda i, j, s: (s, i)), pl.BlockSpec((tr, tn), lambda i, j, s: (s, j))],
        out_specs=pl.BlockSpec((tk, tn), lambda i, j, s: (i, j)),
        out_shape=jax.ShapeDtypeStruct((k, n), F32),
        compiler_params=_cparams(("parallel", "parallel", "arbitrary")),
    )(a, d)


def _mod_index(b, t):
    return (2 * b + jnp.minimum(t, 1), 0, 0)


def tok_fwd(name, f, toks, mod, params, out_widths, out_dtypes):
    nb, s, _ = toks[0].shape
    nt, nm, npar = len(toks), int(mod is not None), len(params)

    def body(*refs):
        ins, outs = refs[: nt + nm + npar], refs[nt + nm + npar:]
        tv = [r[...].astype(F32) for r in ins[:nt]]
        mv = [ins[nt][k:k + 1, :] for k in range(N_MOD)] if nm else None
        pv = [r[...] for r in ins[nt + nm:]]
        for o, r in zip(outs, f(tv, mv, pv)):
            o[...] = r.astype(o.dtype)

    in_specs = [pl.BlockSpec((None, TOK_BLOCK, t.shape[2]), lambda b, t: (b, t, 0)) for t in toks]
    if nm:
        in_specs.append(pl.BlockSpec((None, N_MOD, mod.shape[2]), _mod_index))
    in_specs += [pl.BlockSpec(p.shape, lambda b, t, nd=p.ndim: (0,) * nd) for p in params]
    return pl.pallas_call(
        body,
        name=name,
        grid=(nb, s // TOK_BLOCK),
        in_specs=in_specs,
        out_specs=[pl.BlockSpec((None, TOK_BLOCK, w), lambda b, t: (b, t, 0)) for w in out_widths],
        out_shape=[jax.ShapeDtypeStruct((nb, s, w), dt) for w, dt in zip(out_widths, out_dtypes)],
        compiler_params=_cparams(("parallel", "parallel")),
    )(*toks, *([mod] if nm else []), *params)


def tok_bwd(name, f, toks, mod, params, cots, dtok_dtypes):
    nb, s, _ = toks[0].shape
    nt, nm, npar, nc = len(toks), int(mod is not None), len(params), len(cots)

    def body(*refs):
        n_in = nt + nm + npar + nc
        ins, outs = refs[:n_in], refs[n_in:]
        b, t = pl.program_id(0), pl.program_id(1)
        tv = [r[...].astype(F32) for r in ins[:nt]]
        mv = [ins[nt][k:k + 1, :] for k in range(N_MOD)] if nm else None
        pv = [r[...] for r in ins[nt + nm: nt + nm + npar]]
        cv = [r[...].astype(F32) for r in ins[nt + nm + npar:]]
        _, vjp = jax.vjp(f, tv, mv, pv)
        dtv, dmv, dpv = vjp(cv)
        for o, r in zip(outs[:nt], dtv):
            o[...] = r.astype(o.dtype)
        if nm:
            dm_ref = outs[nt]

            @pl.when(t <= 1)
            def _():
                for k in range(N_MOD):
                    dm_ref[k:k + 1, :] = dmv[k]

            @pl.when(t > 1)
            def _():
                for k in range(N_MOD):
                    dm_ref[k:k + 1, :] += dmv[k]

        first = jnp.logical_and(b == 0, t == 0)
        for o, r in zip(outs[nt + nm:], dpv):
            @pl.when(first)
            def _(o=o, r=r):
                o[...] = r

            @pl.when(jnp.logical_not(first))
            def _(o=o, r=r):
                o[...] += r

    tok_spec = lambda w: pl.BlockSpec((None, TOK_BLOCK, w), lambda b, t: (b, t, 0))
    in_specs = [tok_spec(t.shape[2]) for t in toks]
    if nm:
        in_specs.append(pl.BlockSpec((None, N_MOD, mod.shape[2]), _mod_index))
    in_specs += [pl.BlockSpec(p.shape, lambda b, t, nd=p.ndim: (0,) * nd) for p in params]
    in_specs += [tok_spec(c.shape[2]) for c in cots]
    out_specs = [tok_spec(t.shape[2]) for t in toks]
    out_shape = [jax.ShapeDtypeStruct(t.shape, dt) for t, dt in zip(toks, dtok_dtypes)]
    if nm:
        out_specs.append(pl.BlockSpec((None, N_MOD, mod.shape[2]), _mod_index))
        out_shape.append(jax.ShapeDtypeStruct(mod.shape, F32))
    out_specs += [pl.BlockSpec(p.shape, lambda b, t, nd=p.ndim: (0,) * nd) for p in params]
    out_shape += [jax.ShapeDtypeStruct(p.shape, F32) for p in params]
    res = pl.pallas_call(
        body,
        name=name,
        grid=(nb, s // TOK_BLOCK),
        in_specs=in_specs,
        out_specs=out_specs,
        out_shape=out_shape,
        compiler_params=_cparams(("arbitrary", "arbitrary")),
    )(*toks, *([mod] if nm else []), *params, *cots)
    return res[:nt], (res[nt] if nm else None), res[nt + nm:]


def _rms(x, g):
    return x * lax.rsqrt(jnp.mean(x * x, axis=-1, keepdims=True) + RMS_EPS) * g


def _silu(x):
    return x * jax.nn.sigmoid(x)


def f_norm(shift_row, scale_row):
    def f(tv, mv, pv):
        return [_rms(tv[0], pv[0]) * (1.0 + mv[scale_row]) + mv[shift_row]]
    return f


def f_resnorm(gate_row, shift_row, scale_row):
    def f(tv, mv, pv):
        s = tv[0] + mv[gate_row] * tv[1]
        return [s, _rms(s, pv[0]) * (1.0 + mv[scale_row]) + mv[shift_row]]
    return f


_ANY = pl.BlockSpec(memory_space=pl.ANY)
_MESH = pl.DeviceIdType.MESH


def _my_pos():
    return lax.axis_index("x"), lax.axis_index("y"), lax.axis_index("c")


def _slot_of(pos):
    return 4 * pos[0] + 2 * pos[1] + pos[2]


def all_gather(xs, name):
    n = len(xs)

    def body(*refs):
        x_refs, out_refs = refs[:n], refs[n:2 * n]
        send_sems, recv_sems, local_sems = refs[2 * n:]
        px, py, pc = _my_pos()
        me, sibling = (px, py, pc), (px, py, 1 - pc)
        chips = [(1 - px, py), (px, 1 - py), (1 - px, 1 - py)]

        def copy(a, k, block, to, from_input=False):
            slot = out_refs[a].at[_slot_of(block)]
            return pltpu.make_async_remote_copy(
                src_ref=x_refs[a] if from_input else slot, dst_ref=slot,
                send_sem=send_sems.at[a, k], recv_sem=recv_sems.at[a, k],
                device_id=to, device_id_type=_MESH)

        mine = [pltpu.make_async_copy(x_refs[a], out_refs[a].at[_slot_of(me)], local_sems.at[a]) for a in range(n)]
        for cp in mine:
            cp.start()
        first = [copy(a, 0, me, sibling, True) for a in range(n)]
        first += [copy(a, 1 + j, me, (*chip, pc), True) for j, chip in enumerate(chips) for a in range(n)]
        for cp in first:
            cp.start()
        passed = []
        for j, chip in enumerate(chips):
            for a in range(n):
                copy(a, 1 + j, (*chip, pc), me).wait_recv()
                passed.append(copy(a, 4 + j, (*chip, pc), sibling))
                passed[-1].start()
        for a in range(n):
            copy(a, 0, sibling, me).wait_recv()
        for j, chip in enumerate(chips):
            for a in range(n):
                copy(a, 4 + j, (*chip, 1 - pc), me).wait_recv()
        for cp in first + passed:
            cp.wait_send()
        for cp in mine:
            cp.wait()

    return pl.pallas_call(
        body,
        name=name,
        out_shape=[jax.ShapeDtypeStruct((N_DEV,) + x.shape, x.dtype) for x in xs],
        in_specs=[_ANY] * n,
        out_specs=[_ANY] * n,
        scratch_shapes=[pltpu.SemaphoreType.DMA((n, 7)), pltpu.SemaphoreType.DMA((n, 7)), pltpu.SemaphoreType.DMA((n,))],
    )(*xs)


def all_to_all(xs, name):
    n = len(xs)

    def body(*refs):
        x_refs, out_refs = refs[:n], refs[n:2 * n]
        send_sems, recv_sems, local_sems = refs[2 * n:]
        px, py, pc = _my_pos()
        me = (px, py, pc)

        def flipped(k):
            kx, ky, kc = (k >> 2) & 1, (k >> 1) & 1, k & 1
            return (1 - px if kx else px, 1 - py if ky else py, 1 - pc if kc else pc)

        def copy(a, k):
            peer = flipped(k)
            return pltpu.make_async_remote_copy(
                src_ref=x_refs[a].at[_slot_of(peer)], dst_ref=out_refs[a].at[_slot_of(me)],
                send_sem=send_sems.at[a, k - 1], recv_sem=recv_sems.at[a, k - 1],
                device_id=peer, device_id_type=_MESH)

        def landing(a, k):
            peer = flipped(k)
            return pltpu.make_async_remote_copy(
                src_ref=x_refs[a].at[_slot_of(me)], dst_ref=out_refs[a].at[_slot_of(peer)],
                send_sem=send_sems.at[a, k - 1], recv_sem=recv_sems.at[a, k - 1],
                device_id=peer, device_id_type=_MESH)

        mine = [pltpu.make_async_copy(x_refs[a].at[_slot_of(me)], out_refs[a].at[_slot_of(me)], local_sems.at[a]) for a in range(n)]
        for cp in mine:
            cp.start()
        copies = [copy(a, k) for a in range(n) for k in range(1, N_DEV)]
        for cp in copies:
            cp.start()
        for a in range(n):
            for k in range(1, N_DEV):
                landing(a, k).wait_recv()
        for cp in copies:
            cp.wait_send()
        for cp in mine:
            cp.wait()

    return pl.pallas_call(
        body,
        name=name,
        out_shape=[jax.ShapeDtypeStruct(x.shape, x.dtype) for x in xs],
        in_specs=[_ANY] * n,
        out_specs=[_ANY] * n,
        scratch_shapes=[pltpu.SemaphoreType.DMA((n, 7)), pltpu.SemaphoreType.DMA((n, 7)), pltpu.SemaphoreType.DMA((n,))],
    )(*xs)


N_CHIPS = 4


def sibling_swap(xs, name):
    n = len(xs)

    def body(*refs):
        x_refs, out_refs = refs[:n], refs[n:2 * n]
        send_sems, recv_sems = refs[2 * n:]
        px, py, pc = _my_pos()
        copies = [pltpu.make_async_remote_copy(
            src_ref=x_refs[a].at[:, 1 - pc], dst_ref=out_refs[a],
            send_sem=send_sems.at[a], recv_sem=recv_sems.at[a],
            device_id=(px, py, 1 - pc), device_id_type=_MESH) for a in range(n)]
        for cp in copies:
            cp.start()
        for cp in copies:
            cp.wait()

    return pl.pallas_call(
        body,
        name=name,
        out_shape=[jax.ShapeDtypeStruct(x.shape[:1] + x.shape[2:], x.dtype) for x in xs],
        in_specs=[_ANY] * n,
        out_specs=[_ANY] * n,
        scratch_shapes=[pltpu.SemaphoreType.DMA((n,)), pltpu.SemaphoreType.DMA((n,))],
    )(*xs)


def chip_exchange(xs, name):
    n = len(xs)

    def body(*refs):
        x_refs, out_refs = refs[:n], refs[n:2 * n]
        send_sems, recv_sems, local_sems = refs[2 * n:]
        px, py, pc = _my_pos()
        my_chip = 2 * px + py

        def peer(k):
            return (1 - px if k & 2 else px, 1 - py if k & 1 else py)

        def copy(a, k, landing):
            qx, qy = peer(k)
            src, dst = (my_chip, 2 * qx + qy) if landing else (2 * qx + qy, my_chip)
            return pltpu.make_async_remote_copy(
                src_ref=x_refs[a].at[src], dst_ref=out_refs[a].at[dst],
                send_sem=send_sems.at[a, k - 1], recv_sem=recv_sems.at[a, k - 1],
                device_id=(qx, qy, pc), device_id_type=_MESH)

        mine = [pltpu.make_async_copy(x_refs[a].at[my_chip], out_refs[a].at[my_chip], local_sems.at[a]) for a in range(n)]
        for cp in mine:
            cp.start()
        copies = [copy(a, k, False) for a in range(n) for k in range(1, N_CHIPS)]
        for cp in copies:
            cp.start()
        for a in range(n):
            for k in range(1, N_CHIPS):
                copy(a, k, True).wait_recv()
        for cp in copies:
            cp.wait_send()
        for cp in mine:
            cp.wait()

    return pl.pallas_call(
        body,
        name=name,
        out_shape=[jax.ShapeDtypeStruct(x.shape, x.dtype) for x in xs],
        in_specs=[_ANY] * n,
        out_specs=[_ANY] * n,
        scratch_shapes=[pltpu.SemaphoreType.DMA((n, N_CHIPS - 1)), pltpu.SemaphoreType.DMA((n, N_CHIPS - 1)),
                        pltpu.SemaphoreType.DMA((n,))],
    )(*xs)


def pair_sum(own, got, name):
    nch, _, r, c = own.shape
    tr = _pick(r, (512, 256, 128, 64, 32, 16))

    def body(own_ref, got_ref, o_ref):
        pc = lax.axis_index("c")
        o_ref[...] = (own_ref[pc].astype(F32) + got_ref[...].astype(F32)).astype(o_ref.dtype)

    return pl.pallas_call(
        body,
        name=name,
        grid=(nch, r // tr),
        in_specs=[pl.BlockSpec((None, 2, tr, c), lambda i, j: (i, 0, j, 0)), pl.BlockSpec((None, tr, c), lambda i, j: (i, j, 0))],
        out_specs=pl.BlockSpec((None, tr, c), lambda i, j: (i, j, 0)),
        out_shape=jax.ShapeDtypeStruct((nch, r, c), own.dtype),
        compiler_params=_cparams(("parallel", "parallel")),
    )(own, got)


def sum_slots(x, name):
    n, r, c = x.shape
    tr = _pick(r, (512, 256, 128, 64, 32, 16, 8))

    def body(x_ref, o_ref):
        acc = x_ref[0].astype(F32)
        for i in range(1, n):
            acc = acc + x_ref[i].astype(F32)
        o_ref[...] = acc

    return pl.pallas_call(
        body,
        name=name,
        grid=(r // tr,),
        in_specs=[pl.BlockSpec((n, tr, c), lambda i: (0, i, 0))],
        out_specs=pl.BlockSpec((tr, c), lambda i: (i, 0)),
        out_shape=jax.ShapeDtypeStruct((r, c), F32),
        compiler_params=_cparams(("parallel",)),
    )(x)


CONV_CH_TILE = 256


def _shift_rows(x, off):
    n = x.shape[0]
    if off % n == 0:
        return x
    return pltpu.roll(x, (-off) % n, axis=0)


def _between(v, lo, hi):
    return jnp.where(v >= lo, 1.0, 0.0) * jnp.where(v < hi, 1.0, 0.0)


def taps_1d(ntaps, s_ctx, s_tot):
    def mask(off):
        def m(t):
            is_ctx = _between(t, 0, s_ctx)
            return is_ctx * _between(t + off, 0, s_ctx) + (1.0 - is_ctx) * _between(t + off, s_ctx, s_tot)
        return m
    return [(j - (ntaps - 1) // 2, mask(j - (ntaps - 1) // 2)) for j in range(ntaps)]


def taps_grid(s_ctx, s_tot, grid_w):
    assert s_ctx % grid_w == 0

    return ("grid", s_ctx, s_tot, grid_w)


def _grid_masks(taps, s):
    _, s_ctx, s_tot, grid_w = taps
    t = lax.broadcasted_iota(jnp.int32, (s, 1), 0)
    is_ctx = _between(t, 0, s_ctx)
    mcol = {dc: is_ctx * _between(t + dc, 0, s_ctx) + (1.0 - is_ctx) * _between(t % grid_w + dc, 0, grid_w) for dc in (-1, 1)}
    mrow = {dr: (1.0 - is_ctx) * _between(t + grid_w * dr, s_ctx, s_tot) for dr in (-1, 1)}
    return mcol, mrow


def _grid_cols(x, mcol):
    return {-1: _shift_rows(x, -1) * mcol[-1], 0: x, 1: _shift_rows(x, 1) * mcol[1]}


def _conv_acc(x, w_ref, b_ref, taps, s):
    acc = jnp.broadcast_to(b_ref[...], x.shape)
    if taps[0] == "grid":
        grid_w = taps[3]
        mcol, mrow = _grid_masks(taps, s)
        xc = _grid_cols(x, mcol)
        for a, dr in enumerate((-1, 0, 1)):
            r = sum(w_ref[3 * a + b:3 * a + b + 1, :] * xc[dc] for b, dc in enumerate((-1, 0, 1)))
            acc = acc + (r if dr == 0 else _shift_rows(r, grid_w * dr) * mrow[dr])
        return acc
    t = lax.broadcasted_iota(jnp.int32, (s, 1), 0)
    for k, (off, m) in enumerate(taps):
        acc = acc + w_ref[k:k + 1, :] * (_shift_rows(x, off) * m(t))
    return acc


def _conv_adjoint(x, dacc, w_ref, taps, s):
    if taps[0] == "grid":
        grid_w = taps[3]
        mcol, mrow = _grid_masks(taps, s)
        xc = _grid_cols(x, mcol)
        dxc = {dc: 0.0 for dc in (-1, 0, 1)}
        dws = []
        for a, dr in enumerate((-1, 0, 1)):
            d_r = dacc if dr == 0 else _shift_rows(dacc * mrow[dr], -grid_w * dr)
            for b, dc in enumerate((-1, 0, 1)):
                dxc[dc] = dxc[dc] + w_ref[3 * a + b:3 * a + b + 1, :] * d_r
                dws.append(jnp.sum(d_r * xc[dc], axis=0, keepdims=True))
        dx = dxc[0] + _shift_rows(dxc[-1] * mcol[-1], 1) + _shift_rows(dxc[1] * mcol[1], -1)
        return dx, dws
    t = lax.broadcasted_iota(jnp.int32, (s, 1), 0)
    dx = jnp.zeros_like(x)
    dws = []
    for k, (off, m) in enumerate(taps):
        dm = dacc * m(t)
        dx = dx + _shift_rows(w_ref[k:k + 1, :] * dm, -off)
        dws.append(jnp.sum(dm * _shift_rows(x, off), axis=0, keepdims=True))
    return dx, dws


def conv_fwd(name, x, w, b, taps, mode, mul=None, out_dtype=F32):
    nb, s, c = x.shape
    ct = _pick(c, (CONV_CH_TILE, 128))
    has_mul = mode == "silu_mul"

    def body(*refs):
        x_ref, w_ref, b_ref = refs[:3]
        o_ref = refs[-1]
        acc = _conv_acc(x_ref[...], w_ref, b_ref, taps, s)
        if mode == "none":
            out = acc
        else:
            out = _silu(acc)
            if has_mul:
                out = out * refs[3][...].astype(F32)
        o_ref[...] = out.astype(o_ref.dtype)

    blk = pl.BlockSpec((None, s, ct), lambda bb, j: (bb, 0, j))
    par = lambda k: pl.BlockSpec((k, ct), lambda bb, j: (0, j))
    return pl.pallas_call(
        body,
        name=name,
        grid=(nb, c // ct),
        in_specs=[blk, par(w.shape[0]), par(1)] + ([blk] if has_mul else []),
        out_specs=blk,
        out_shape=jax.ShapeDtypeStruct(x.shape, out_dtype),
        compiler_params=_cparams(("parallel", "parallel")),
    )(x, w, b, *([mul] if has_mul else []))


def conv_bwd(name, x, w, b, dout, taps, mode, mul=None, dx_dtype=F32):
    nb, s, c = x.shape
    ct = _pick(c, (CONV_CH_TILE, 128))
    has_mul = mode == "silu_mul"
    nk = w.shape[0]

    def body(*refs):
        x_ref, w_ref, b_ref, do_ref = refs[:4]
        n_in = 5 if has_mul else 4
        dx_ref, dw_ref, db_ref = refs[n_in:n_in + 3]
        bb = pl.program_id(1)
        x = x_ref[...]
        dacc = do_ref[...].astype(F32)
        if mode != "none":
            acc = _conv_acc(x, w_ref, b_ref, taps, s)
            sg = jax.nn.sigmoid(acc)
            if has_mul:
                refs[n_in + 3][...] = (dacc * (acc * sg)).astype(refs[n_in + 3].dtype)
                dacc = dacc * refs[4][...].astype(F32)
            dacc = dacc * (sg * (1.0 + acc * (1.0 - sg)))
        dx, dws = _conv_adjoint(x, dacc, w_ref, taps, s)
        dx_ref[...] = dx.astype(dx_ref.dtype)
        db = jnp.sum(dacc, axis=0, keepdims=True)

        @pl.when(bb == 0)
        def _():
            for k in range(nk):
                dw_ref[k:k + 1, :] = dws[k]
            db_ref[...] = db

        @pl.when(bb > 0)
        def _():
            for k in range(nk):
                dw_ref[k:k + 1, :] += dws[k]
            db_ref[...] += db

    blk = pl.BlockSpec((None, s, ct), lambda j, bb: (bb, 0, j))
    par = lambda k: pl.BlockSpec((k, ct), lambda j, bb: (0, j))
    out_specs = [blk, par(nk), par(1)] + ([blk] if has_mul else [])
    out_shape = [jax.ShapeDtypeStruct(x.shape, dx_dtype), jax.ShapeDtypeStruct(w.shape, F32), jax.ShapeDtypeStruct(b.shape, F32)]
    if has_mul:
        out_shape.append(jax.ShapeDtypeStruct(x.shape, dx_dtype))
    return pl.pallas_call(
        body,
        name=name,
        grid=(c // ct, nb),
        in_specs=[blk, par(nk), par(1), blk] + ([blk] if has_mul else []),
        out_specs=out_specs,
        out_shape=out_shape,
        compiler_params=_cparams(("parallel", "arbitrary")),
    )(x, w, b, dout, *([mul] if has_mul else []))


SCAN_UNROLL = 4


def _scan_order(direction, adjoint, s_ctx, s_tot):
    nc, nt = s_ctx // SUBLANES, s_tot // SUBLANES
    if direction == 0:
        return ([(0, nt, 1)], False) if not adjoint else ([(nt - 1, nt, -1)], True)
    if not adjoint:
        return [(nc - 1, nc, -1), (nt - 1, nt - nc, -1)], True
    return [(nc, nt - nc, 1), (0, nc, 1)], False


def _last_row(h, descending):
    row = lax.broadcasted_iota(jnp.int32, h.shape, 0)
    pick = 0 if descending else SUBLANES - 1
    return jnp.sum(jnp.where(row == pick, h, 0.0), axis=0, keepdims=True)


def _prev_rows(h, carry, descending):
    row = lax.broadcasted_iota(jnp.int32, h.shape, 0)
    if descending:
        return jnp.where(row == SUBLANES - 1, carry, pltpu.roll(h, SUBLANES - 1, axis=0))
    return jnp.where(row == 0, carry, pltpu.roll(h, 1, axis=0))


def _scan_real(a_ref, x_ref, h_ref, hp_ref, order):
    ranges, descending = order
    n_rows, width = a_ref.shape
    n_tiles = n_rows // SUBLANES
    row = lax.broadcasted_iota(jnp.int32, (SUBLANES, width), 0)
    unroll = lambda count: SCAN_UNROLL if count % SCAN_UNROLL == 0 else 1

    def run(ac_ref):
        def in_tile(i, _):
            t0 = pl.multiple_of(i * SUBLANES, SUBLANES)
            a = a_ref[pl.ds(t0, SUBLANES), :]
            x = x_ref[pl.ds(t0, SUBLANES), :]
            for k in (1, 2, 4):
                sh = SUBLANES - k if descending else k
                keep = (row < SUBLANES - k) if descending else (row >= k)
                x = jnp.where(keep, a * pltpu.roll(x, sh, axis=0) + x, x)
                a = jnp.where(keep, a * pltpu.roll(a, sh, axis=0), a)
            ac_ref[pl.ds(t0, SUBLANES), :] = a
            x_ref[pl.ds(t0, SUBLANES), :] = x
            return 0

        lax.fori_loop(0, n_tiles, in_tile, 0, unroll=unroll(n_tiles))

        def tile(i, carry):
            t0 = pl.multiple_of(i * SUBLANES, SUBLANES)
            a = ac_ref[pl.ds(t0, SUBLANES), :]
            x = x_ref[pl.ds(t0, SUBLANES), :]
            h = a * carry + x
            if h_ref is not None:
                h_ref[pl.ds(t0, SUBLANES), :] = h
            if hp_ref is not None:
                hp_ref[pl.ds(t0, SUBLANES), :] = _prev_rows(h, carry, descending)
            return _last_row(a, descending) * carry + _last_row(x, descending)

        carry = jnp.zeros((1, width), F32)
        for first, count, step in ranges:
            carry = lax.fori_loop(0, count, lambda j, c, first=first, step=step: tile(first + step * j, c), carry,
                                  unroll=unroll(count))

    pl.run_scoped(run, pltpu.VMEM((n_rows, width), F32))


def _cmul(ar, ai, br, bi):
    return ar * br - ai * bi, ar * bi + ai * br


def _scan_cplx(lr, li, xr_ref, xi_ref, hpr_ref, hpi_ref, order):
    ranges, descending = order
    width = xr_ref.shape[1]
    row = lax.broadcasted_iota(jnp.int32, (SUBLANES, width), 0)
    pw = [(lr, li)]
    for _ in range(SUBLANES - 1):
        pw.append(_cmul(pw[-1][0], pw[-1][1], lr, li))
    pr = jnp.zeros((SUBLANES, width), F32)
    pi = jnp.zeros((SUBLANES, width), F32)
    for r in range(SUBLANES):
        n = SUBLANES - 1 - r if descending else r
        pr = jnp.where(row == r, pw[n][0], pr)
        pi = jnp.where(row == r, pw[n][1], pi)

    n_tiles = xr_ref.shape[0] // SUBLANES
    unroll = lambda count: SCAN_UNROLL if count % SCAN_UNROLL == 0 else 1

    def in_tile(i, _):
        t0 = pl.multiple_of(i * SUBLANES, SUBLANES)
        xr = xr_ref[pl.ds(t0, SUBLANES), :]
        xi = xi_ref[pl.ds(t0, SUBLANES), :]
        for k in (1, 2, 4):
            sh = SUBLANES - k if descending else k
            keep = (row < SUBLANES - k) if descending else (row >= k)
            sr, si = _cmul(pw[k - 1][0], pw[k - 1][1], pltpu.roll(xr, sh, axis=0), pltpu.roll(xi, sh, axis=0))
            xr = jnp.where(keep, xr + sr, xr)
            xi = jnp.where(keep, xi + si, xi)
        xr_ref[pl.ds(t0, SUBLANES), :] = xr
        xi_ref[pl.ds(t0, SUBLANES), :] = xi
        return 0

    lax.fori_loop(0, n_tiles, in_tile, 0, unroll=unroll(n_tiles))
    lam8 = pw[SUBLANES - 1]

    def tile(i, carry):
        cr, ci = carry
        t0 = pl.multiple_of(i * SUBLANES, SUBLANES)
        xr = xr_ref[pl.ds(t0, SUBLANES), :]
        xi = xi_ref[pl.ds(t0, SUBLANES), :]
        hr, hi = _cmul(pr, pi, cr, ci)
        hr, hi = hr + xr, hi + xi
        xr_ref[pl.ds(t0, SUBLANES), :] = hr
        xi_ref[pl.ds(t0, SUBLANES), :] = hi
        if hpr_ref is not None:
            hpr_ref[pl.ds(t0, SUBLANES), :] = _prev_rows(hr, cr, descending)
            hpi_ref[pl.ds(t0, SUBLANES), :] = _prev_rows(hi, ci, descending)
        nr, ni = _cmul(lam8[0], lam8[1], cr, ci)
        return nr + _last_row(xr, descending), ni + _last_row(xi, descending)

    carry = (jnp.zeros((1, width), F32), jnp.zeros((1, width), F32))
    for first, count, step in ranges:
        carry = lax.fori_loop(0, count, lambda j, c, first=first, step=step: tile(first + step * j, c), carry,
                              unroll=unroll(count))


def _log1p_pos(y):
    return jnp.where(y < 0.01, y * (1.0 - y * (0.5 - y * (1.0 / 3.0 - 0.25 * y))), jnp.log(1.0 + y))


def _softplus(x):
    return jnp.maximum(x, 0.0) + _log1p_pos(jnp.exp(-jnp.abs(x)))


def _neg_expm1(z):
    series = -z * (1.0 + z * (0.5 + z * (1.0 / 6.0 + z * (1.0 / 24.0 + z * (1.0 / 120.0)))))
    return jnp.where(z > -0.1, series, 1.0 - jnp.exp(z))


def _lru_gates(u, w_a, b_a, w_i, b_i, lam):
    ub = u.astype(BF16)
    r = jax.nn.sigmoid(jnp.dot(ub, w_a.astype(BF16), preferred_element_type=F32) + b_a)
    i = jax.nn.sigmoid(jnp.dot(ub, w_i.astype(BF16), preferred_element_type=F32) + b_i)
    log_a = (-LRU_C) * _softplus(-lam) * r
    return jnp.exp(log_a), jnp.sqrt(_neg_expm1(2.0 * log_a)) * (i * u)


LRU_PER_STEP = 4
LRU_PER_STEP_BWD = 2


def _lru_specs(per, bw, order):
    w = pl.BlockSpec((2, per, bw, bw), lambda *g: (0, order(*g), 0, 0))
    v = pl.BlockSpec((2, per, 1, bw), lambda *g: (0, order(*g), 0, 0))
    return [w, v, w, v, v]


def lru_fwd(name, u, w_a, b_a, w_i, b_i, lam, s_ctx):
    nb, s, _ = u.shape
    nblk, bw = w_a.shape[1], w_a.shape[2]
    per = min(LRU_PER_STEP, nblk)

    def body(u_ref, wa, ba, wi, bi, lm, o_ref, a_s, x_s, h_s):
        for d in (0, 1):
            for k in range(per):
                cols = slice(k * bw, (k + 1) * bw)
                a, bx = _lru_gates(u_ref[:, cols], wa[d, k], ba[d, k], wi[d, k], bi[d, k], lm[d, k])
                a_s[:, cols] = a
                x_s[:, cols] = bx
            _scan_real(a_s, x_s, h_s, None, _scan_order(d, False, s_ctx, s))
            if d == 0:
                o_ref[...] = h_s[...]
            else:
                o_ref[...] += h_s[...]

    blk = pl.BlockSpec((None, s, per * bw), lambda b, n: (b, 0, n))
    return pl.pallas_call(
        body,
        name=name,
        grid=(nb, nblk // per),
        in_specs=[blk] + _lru_specs(per, bw, lambda b, n: n),
        out_specs=blk,
        out_shape=jax.ShapeDtypeStruct(u.shape, F32),
        scratch_shapes=[pltpu.VMEM((s, per * bw), F32)] * 3,
        compiler_params=_cparams(("parallel", "parallel")),
    )(u, w_a, b_a, w_i, b_i, lam)


def lru_bwd(name, u, w_a, b_a, w_i, b_i, lam, dh, s_ctx):
    nb, s, _ = u.shape
    nblk, bw = w_a.shape[1], w_a.shape[2]
    per = min(LRU_PER_STEP_BWD, nblk)

    def body(u_ref, wa, ba, wi, bi, lm, dh_ref, du_ref, dwa, dba, dwi, dbi, dlm, a_s, x_s, hp_s, wp_s):
        b = pl.program_id(1)
        for d in (0, 1):
            for k in range(per):
                cols = slice(k * bw, (k + 1) * bw)
                a, bx = _lru_gates(u_ref[:, cols], wa[d, k], ba[d, k], wi[d, k], bi[d, k], lm[d, k])
                a_s[:, cols] = a
                x_s[:, cols] = bx
            _scan_real(a_s, x_s, None, hp_s, _scan_order(d, False, s_ctx, s))
            x_s[...] = a_s[...] * dh_ref[...]
            _scan_real(a_s, x_s, None, wp_s, _scan_order(d, True, s_ctx, s))
            for k in range(per):
                cols = slice(k * bw, (k + 1) * bw)
                g = dh_ref[:, cols] + wp_s[:, cols]
                _, vjp = jax.vjp(_lru_gates, u_ref[:, cols], wa[d, k], ba[d, k], wi[d, k], bi[d, k], lm[d, k])
                grads = vjp((g * hp_s[:, cols], g))
                if d == 0:
                    du_ref[:, cols] = grads[0]
                else:
                    du_ref[:, cols] += grads[0]
                for ref, val in zip((dwa, dba, dwi, dbi, dlm), grads[1:]):
                    @pl.when(b == 0)
                    def _(ref=ref, val=val, k=k):
                        ref[d, k] = val

                    @pl.when(b > 0)
                    def _(ref=ref, val=val, k=k):
                        ref[d, k] += val

    blk = pl.BlockSpec((None, s, per * bw), lambda n, b: (b, 0, n))
    pspecs = _lru_specs(per, bw, lambda n, b: n)
    return pl.pallas_call(
        body,
        name=name,
        grid=(nblk // per, nb),
        in_specs=[blk] + pspecs + [blk],
        out_specs=[blk] + pspecs,
        out_shape=[jax.ShapeDtypeStruct(u.shape, F32)] + [jax.ShapeDtypeStruct(p.shape, F32) for p in (w_a, b_a, w_i, b_i, lam)],
        scratch_shapes=[pltpu.VMEM((s, per * bw), F32)] * 4,
        compiler_params=_cparams(("parallel", "arbitrary")),
    )(u, w_a, b_a, w_i, b_i, lam, dh)


S5_TILE_CH = 128
S5_TILE_STATES = S5_TILE_CH // S5_GROUP_CH * S5_STATE


def _dot_nt(a, b):
    return lax.dot_general(a, b, (((1,), (1,)), ((), ())), preferred_element_type=F32)


def _dot_tn(a, b):
    return lax.dot_general(a, b, (((0,), (0,)), ((), ())), preferred_element_type=F32)


def _s5_specs(order):
    lam = pl.BlockSpec((2, 1, S5_TILE_STATES), lambda *g: (0, 0, order(*g)))
    mat = pl.BlockSpec((2, None, S5_TILE_STATES, S5_TILE_CH), lambda *g: (0, order(*g), 0, 0))
    return [lam, lam, mat, mat, mat, mat]


def s5_fwd(name, u, lam_r, lam_i, bt_r, bt_i, ct_r, ct_i, s_ctx):
    nb, s, w = u.shape

    def body(u_ref, lr, li, btr, bti, ctr, cti, o_ref, xr_s, xi_s):
        ub = u_ref[...].astype(BF16)
        for d in (0, 1):
            xr_s[...] = _dot_nt(ub, btr[d].astype(BF16))
            xi_s[...] = _dot_nt(ub, bti[d].astype(BF16))
            _scan_cplx(lr[d], li[d], xr_s, xi_s, None, None, _scan_order(d, False, s_ctx, s))
            y = (jnp.dot(xr_s[...].astype(BF16), ctr[d].astype(BF16), preferred_element_type=F32)
                 - jnp.dot(xi_s[...].astype(BF16), cti[d].astype(BF16), preferred_element_type=F32))
            if d == 0:
                o_ref[...] = y
            else:
                o_ref[...] += y

    blk = pl.BlockSpec((None, s, S5_TILE_CH), lambda b, j: (b, 0, j))
    return pl.pallas_call(
        body,
        name=name,
        grid=(nb, w // S5_TILE_CH),
        in_specs=[blk] + _s5_specs(lambda b, j: j),
        out_specs=blk,
        out_shape=jax.ShapeDtypeStruct(u.shape, F32),
        scratch_shapes=[pltpu.VMEM((s, S5_TILE_STATES), F32)] * 2,
        compiler_params=_cparams(("parallel", "parallel")),
    )(u, lam_r, lam_i, bt_r, bt_i, ct_r, ct_i)


def s5_bwd(name, u, lam_r, lam_i, bt_r, bt_i, ct_r, ct_i, dy, s_ctx):
    nb, s, w = u.shape

    def body(u_ref, lr, li, btr, bti, ctr, cti, dy_ref, du_ref, dlr, dli, dbtr, dbti, dctr, dcti,
             hr_s, hi_s, hpr_s, hpi_s, gr_s, gi_s):
        b = pl.program_id(1)
        ub = u_ref[...].astype(BF16)
        dyb = dy_ref[...].astype(BF16)
        du = jnp.zeros((s, S5_TILE_CH), F32)
        for d in (0, 1):
            hr_s[...] = _dot_nt(ub, btr[d].astype(BF16))
            hi_s[...] = _dot_nt(ub, bti[d].astype(BF16))
            _scan_cplx(lr[d], li[d], hr_s, hi_s, hpr_s, hpi_s, _scan_order(d, False, s_ctx, s))
            d_ctr = _dot_tn(hr_s[...].astype(BF16), dyb)
            d_cti = -_dot_tn(hi_s[...].astype(BF16), dyb)
            gr_s[...] = _dot_nt(dyb, ctr[d].astype(BF16))
            gi_s[...] = -_dot_nt(dyb, cti[d].astype(BF16))
            _scan_cplx(lr[d], -li[d], gr_s, gi_s, None, None, _scan_order(d, True, s_ctx, s))
            gr, gi = gr_s[...], gi_s[...]
            hpr, hpi = hpr_s[...], hpi_s[...]
            d_lr = jnp.sum(gr * hpr + gi * hpi, axis=0, keepdims=True)
            d_li = jnp.sum(gi * hpr - gr * hpi, axis=0, keepdims=True)
            grb, gib = gr.astype(BF16), gi.astype(BF16)
            du = du + jnp.dot(grb, btr[d].astype(BF16), preferred_element_type=F32)
            du = du + jnp.dot(gib, bti[d].astype(BF16), preferred_element_type=F32)
            d_btr = _dot_tn(grb, ub)
            d_bti = _dot_tn(gib, ub)
            for ref, val in zip((dlr, dli, dbtr, dbti, dctr, dcti), (d_lr, d_li, d_btr, d_bti, d_ctr, d_cti)):
                @pl.when(b == 0)
                def _(ref=ref, val=val):
                    ref[d] = val

                @pl.when(b > 0)
                def _(ref=ref, val=val):
                    ref[d] += val
        du_ref[...] = du

    blk = pl.BlockSpec((None, s, S5_TILE_CH), lambda j, b: (b, 0, j))
    pspecs = _s5_specs(lambda j, b: j)
    params = (lam_r, lam_i, bt_r, bt_i, ct_r, ct_i)
    return pl.pallas_call(
        body,
        name=name,
        grid=(w // S5_TILE_CH, nb),
        in_specs=[blk] + pspecs + [blk],
        out_specs=[blk] + pspecs,
        out_shape=[jax.ShapeDtypeStruct(u.shape, F32)] + [jax.ShapeDtypeStruct(p.shape, F32) for p in params],
        scratch_shapes=[pltpu.VMEM((s, S5_TILE_STATES), F32)] * 6,
        compiler_params=_cparams(("parallel", "arbitrary")),
    )(u, lam_r, lam_i, bt_r, bt_i, ct_r, ct_i, dy)


def small_fwd(name, f, ins, out_shapes):
    n = len(ins)

    def body(*refs):
        for o, r in zip(refs[n:], f([r[...] for r in refs[:n]])):
            o[...] = r

    return pl.pallas_call(
        body, name=name,
        out_shape=[jax.ShapeDtypeStruct(s, F32) for s in out_shapes],
        compiler_params=_cparams(),
    )(*ins)


def small_bwd(name, f, ins, cots):
    n, nc = len(ins), len(cots)

    def body(*refs):
        _, vjp = jax.vjp(f, [r[...] for r in refs[:n]])
        (grads,) = vjp([r[...] for r in refs[n:n + nc]])
        for o, r in zip(refs[n + nc:], grads):
            o[...] = r

    return pl.pallas_call(
        body, name=name,
        out_shape=[jax.ShapeDtypeStruct(a.shape, F32) for a in ins],
        compiler_params=_cparams(),
    )(*ins, *cots)


def _row(x, r):
    return jnp.sum(jnp.where(lax.broadcasted_iota(jnp.int32, x.shape, 0) == r, x, 0.0), axis=0, keepdims=True)


def _col(x, c):
    return jnp.sum(jnp.where(lax.broadcasted_iota(jnp.int32, x.shape, 1) == c, x, 0.0), axis=1, keepdims=True)


def _chunk_at(i, reverse, ncc, nc):
    if not reverse:
        return i
    return jnp.where(i < ncc, ncc - 1 - i, nc - 1 - (i - ncc))


def _tri(n, reverse):
    li = lax.broadcasted_iota(jnp.int32, (n, n), 0)
    si = lax.broadcasted_iota(jnp.int32, (n, n), 1)
    return jnp.where((li <= si) if reverse else (li >= si), 1.0, 0.0)


_HI = lax.Precision.HIGHEST


def _ssd_chunk(xs, bm, cm, dtc, dtr, a_row, a_col, hs, reverse):
    n = bm.shape[0]
    last = 0 if reverse else n - 1
    tri = _tri(n, reverse)
    cum_c = jnp.dot(tri, dtc * -jnp.exp(a_row), precision=_HI, preferred_element_type=F32)
    cum_r = lax.dot_general(dtr * -jnp.exp(a_col), tri, (((1,), (1,)), ((), ())), precision=_HI, preferred_element_type=F32)
    tot_r = _row(cum_c, last)
    bmb, cmb = bm.astype(BF16), cm.astype(BF16)
    cb = _dot_nt(cmb, bmb)
    ys, hn = [], []
    for hd in range(len(xs)):
        cl = _col(cum_c, hd)
        tot = _col(tot_r, hd)
        decay = jnp.exp(jnp.where(tri > 0.0, cl - _row(cum_r, hd), -jnp.inf))
        xd = xs[hd] * _col(dtc, hd)
        y = jnp.dot((cb * decay).astype(BF16), xd.astype(BF16), preferred_element_type=F32)
        y = y + _dot_nt(cmb, hs[hd].astype(BF16)) * jnp.exp(cl)
        hnew = hs[hd] * jnp.exp(tot) + _dot_tn((xd * jnp.exp(tot - cl)).astype(BF16), bmb)
        ys.append(y)
        hn.append(hnew)
    return ys, hn


def _ssd_specs(reverse, ncc, nc, order):
    ch = lambda *g: _chunk_at(order(*g)[1], reverse, ncc, nc)
    b_ = lambda *g: order(*g)[0]
    gn = SSD_GROUPS * SSD_STATE
    return [
        pl.BlockSpec((None, SSD_CHUNK, SSD_W), lambda *g: (b_(*g), ch(*g), 0)),
        pl.BlockSpec((None, SSD_CHUNK, gn), lambda *g: (b_(*g), ch(*g), SSD_W // gn)),
        pl.BlockSpec((None, SSD_CHUNK, gn), lambda *g: (b_(*g), ch(*g), SSD_W // gn + 1)),
        pl.BlockSpec((None, SSD_GROUPS, SSD_CHUNK, SSD_HPG), lambda *g: (b_(*g), 0, ch(*g), 0)),
        pl.BlockSpec((None, SSD_GROUPS, SSD_HPG, SSD_CHUNK), lambda *g: (b_(*g), 0, 0, ch(*g))),
        pl.BlockSpec((SSD_GROUPS, 1, SSD_HPG), lambda *g: (0, 0, 0)),
        pl.BlockSpec((SSD_GROUPS, SSD_HPG, 1), lambda *g: (0, 0, 0)),
    ]


def _ssd_group_inputs(g, x_ref, bm_ref, cm_ref, dtc_ref, dtr_ref, ar_ref, ac_ref):
    p, n = SSD_HEAD_DIM, SSD_STATE
    xs = [x_ref[:, p * (SSD_HPG * g + hd):p * (SSD_HPG * g + hd + 1)] for hd in range(SSD_HPG)]
    return xs, bm_ref[:, n * g:n * (g + 1)], cm_ref[:, n * g:n * (g + 1)], dtc_ref[g], dtr_ref[g], ar_ref[g], ac_ref[g]


def ssd_fwd(name, xbc, dt_col, dt_row, a_row, a_col, reverse, s_ctx):
    nb, s, _ = xbc.shape
    nc, ncc = s // SSD_CHUNK, s_ctx // SSD_CHUNK
    p = SSD_HEAD_DIM

    def body(x_ref, bm_ref, cm_ref, dtc_ref, dtr_ref, ar_ref, ac_ref, y_ref, hst_ref, h_s):
        i = pl.program_id(1)

        @pl.when(i == 0)
        def _():
            h_s[...] = jnp.zeros_like(h_s)

        hst_ref[...] = h_s[...]
        for g in range(SSD_GROUPS):
            xs, bm, cm, dtc, dtr, ar, ac = _ssd_group_inputs(g, x_ref, bm_ref, cm_ref, dtc_ref, dtr_ref, ar_ref, ac_ref)
            ys, hn = _ssd_chunk(xs, bm, cm, dtc, dtr, ar, ac, [h_s[g, hd] for hd in range(SSD_HPG)], reverse)
            for hd in range(SSD_HPG):
                y_ref[:, p * (SSD_HPG * g + hd):p * (SSD_HPG * g + hd + 1)] = ys[hd]
                h_s[g, hd] = hn[hd]

    state = (SSD_GROUPS, SSD_HPG, SSD_HEAD_DIM, SSD_STATE)
    return pl.pallas_call(
        body,
        name=name,
        grid=(nb, nc),
        in_specs=_ssd_specs(reverse, ncc, nc, lambda b, i: (b, i)),
        out_specs=[pl.BlockSpec((None, SSD_CHUNK, SSD_W), lambda b, i: (b, _chunk_at(i, reverse, ncc, nc), 0)),
                   pl.BlockSpec((None, None) + state, lambda b, i: (b, i, 0, 0, 0, 0))],
        out_shape=[jax.ShapeDtypeStruct((nb, s, SSD_W), F32), jax.ShapeDtypeStruct((nb, nc) + state, F32)],
        scratch_shapes=[pltpu.VMEM(state, F32)],
        compiler_params=_cparams(("parallel", "arbitrary")),
    )(xbc, xbc, xbc, dt_col, dt_row, a_row, a_col)


def ssd_bwd(name, xbc, dt_col, dt_row, a_row, a_col, hst, dy, reverse, s_ctx):
    nb, s, _ = xbc.shape
    nc, ncc = s // SSD_CHUNK, s_ctx // SSD_CHUNK
    p, n = SSD_HEAD_DIM, SSD_STATE

    def body(x_ref, bm_ref, cm_ref, dtc_ref, dtr_ref, ar_ref, ac_ref, hst_ref, dy_ref,
             dx_ref, dbm_ref, dcm_ref, ddtc_ref, ddtr_ref, dar_ref, dac_ref, dh_s):
        i = pl.program_id(1)

        @pl.when(i == 0)
        def _():
            dh_s[...] = jnp.zeros_like(dh_s)

        for g in range(SSD_GROUPS):
            xs, bm, cm, dtc, dtr, ar, ac = _ssd_group_inputs(g, x_ref, bm_ref, cm_ref, dtc_ref, dtr_ref, ar_ref, ac_ref)
            hs = [hst_ref[g, hd] for hd in range(SSD_HPG)]
            _, vjp = jax.vjp(functools.partial(_ssd_chunk, reverse=reverse), xs, bm, cm, dtc, dtr, ar, ac, hs)
            dys = [dy_ref[:, p * (SSD_HPG * g + hd):p * (SSD_HPG * g + hd + 1)] for hd in range(SSD_HPG)]
            dxs, dbm, dcm, ddtc, ddtr, dar, dac, dhs = vjp((dys, [dh_s[g, hd] for hd in range(SSD_HPG)]))
            for hd in range(SSD_HPG):
                dx_ref[:, p * (SSD_HPG * g + hd):p * (SSD_HPG * g + hd + 1)] = dxs[hd]
                dh_s[g, hd] = dhs[hd]
            dbm_ref[:, n * g:n * (g + 1)] = dbm
            dcm_ref[:, n * g:n * (g + 1)] = dcm
            ddtc_ref[g] = ddtc
            ddtr_ref[g] = ddtr

            @pl.when(i == 0)
            def _(g=g, dar=dar, dac=dac):
                dar_ref[g] = dar
                dac_ref[g] = dac

            @pl.when(i > 0)
            def _(g=g, dar=dar, dac=dac):
                dar_ref[g] += dar
                dac_ref[g] += dac

    ch = lambda b, i: _chunk_at(nc - 1 - i, reverse, ncc, nc)
    state = (SSD_GROUPS, SSD_HPG, SSD_HEAD_DIM, SSD_STATE)
    gn = SSD_GROUPS * SSD_STATE
    in_specs = _ssd_specs(reverse, ncc, nc, lambda b, i: (b, nc - 1 - i)) + [
        pl.BlockSpec((None, None) + state, lambda b, i: (b, nc - 1 - i, 0, 0, 0, 0)),
        pl.BlockSpec((None, SSD_CHUNK, SSD_W), lambda b, i: (b, ch(b, i), 0))]
    out_specs = [
        pl.BlockSpec((None, SSD_CHUNK, SSD_W), lambda b, i: (b, ch(b, i), 0)),
        pl.BlockSpec((None, SSD_CHUNK, gn), lambda b, i: (b, ch(b, i), 0)),
        pl.BlockSpec((None, SSD_CHUNK, gn), lambda b, i: (b, ch(b, i), 0)),
        pl.BlockSpec((None, SSD_GROUPS, SSD_CHUNK, SSD_HPG), lambda b, i: (b, 0, ch(b, i), 0)),
        pl.BlockSpec((None, SSD_GROUPS, SSD_HPG, SSD_CHUNK), lambda b, i: (b, 0, 0, ch(b, i))),
        pl.BlockSpec((None, SSD_GROUPS, 1, SSD_HPG), lambda b, i: (b, 0, 0, 0)),
        pl.BlockSpec((None, SSD_GROUPS, SSD_HPG, 1), lambda b, i: (b, 0, 0, 0)),
    ]
    out_shape = [
        jax.ShapeDtypeStruct((nb, s, SSD_W), F32),
        jax.ShapeDtypeStruct((nb, s, gn), F32),
        jax.ShapeDtypeStruct((nb, s, gn), F32),
        jax.ShapeDtypeStruct(dt_col.shape, F32),
        jax.ShapeDtypeStruct(dt_row.shape, F32),
        jax.ShapeDtypeStruct((nb, SSD_GROUPS, 1, SSD_HPG), F32),
        jax.ShapeDtypeStruct((nb, SSD_GROUPS, SSD_HPG, 1), F32),
    ]
    return pl.pallas_call(
        body,
        name=name,
        grid=(nb, nc),
        in_specs=in_specs,
        out_specs=out_specs,
        out_shape=out_shape,
        scratch_shapes=[pltpu.VMEM(state, F32)],
        compiler_params=_cparams(("parallel", "arbitrary")),
    )(xbc, xbc, xbc, dt_col, dt_row, a_row, a_col, hst, dy)


HG_TILES = HG_CHUNK // SUBLANES


def _hg_cum_tiles(x_t, reverse):
    row = lax.broadcasted_iota(jnp.int32, x_t[0].shape, 0)
    out = [None] * len(x_t)
    off = None
    for i in (reversed(range(len(x_t))) if reverse else range(len(x_t))):
        c = x_t[i]
        for k in (1, 2, 4):
            keep = (row < SUBLANES - k) if reverse else (row >= k)
            c = jnp.where(keep, c + pltpu.roll(c, SUBLANES - k if reverse else k, axis=0), c)
        out[i] = c if off is None else c + off
        off = _last_row(out[i], reverse)
    return out, off


def _hg_pairs(reverse):
    row = lax.broadcasted_iota(jnp.int32, (SUBLANES, HG_DK), 0)
    rots = []
    for r in range(SUBLANES):
        rots.append(((SUBLANES - r) % SUBLANES, row <= SUBLANES - 1 - r) if reverse else (r, row >= r))
    return [(j, [i for i in range(HG_TILES) if (i <= j if reverse else i >= j)], rots) for j in range(HG_TILES)]


def _rot(x, sh):
    return pltpu.roll(x, sh, axis=0) if sh else x


def _cat(tiles):
    return jnp.concatenate(tiles, axis=0)


def _hg_chunk_fwd(q_t, k_t, lf_t, v_t, st, reverse):
    cum_t, tot = _hg_cum_tiles(lf_t, reverse)
    y_t = [jnp.zeros(v_t[0].shape, F32) for _ in v_t]
    for j, l_tiles, rots in _hg_pairs(reverse):
        for sh, diag_ok in rots:
            k_j, c_j, v_j = _rot(k_t[j], sh), _rot(cum_t[j], sh), _rot(v_t[j], sh)
            for i in l_tiles:
                e = jnp.exp(cum_t[i] - c_j)
                if i == j:
                    e = jnp.where(diag_ok, e, 0.0)
                att = jnp.sum(q_t[i] * (k_j * e), axis=1, keepdims=True)
                y_t[i] = y_t[i] + att * v_j
    q, k, v, cum = _cat(q_t), _cat(k_t), _cat(v_t), _cat(cum_t)
    y_state = _dot_nt((q * jnp.exp(cum)).astype(BF16), st.astype(BF16))
    st_new = st * jnp.exp(tot) + _dot_tn(v.astype(BF16), (k * jnp.exp(tot - cum)).astype(BF16))
    return [y + y_state[SUBLANES * i:SUBLANES * (i + 1)] for i, y in enumerate(y_t)], st_new


def _hg_chunk_bwd(q_t, k_t, lf_t, v_t, st, dy_t, dst_new, reverse):
    nt = len(q_t)
    cum_t, tot = _hg_cum_tiles(lf_t, reverse)
    q, k, v, cum, dy = _cat(q_t), _cat(k_t), _cat(v_t), _cat(cum_t), _cat(dy_t)
    e_cum, e_tot, e_end = jnp.exp(cum), jnp.exp(tot), jnp.exp(tot - cum)
    qt, khat = q * e_cum, k * e_end
    dyb, dsb = dy.astype(BF16), dst_new.astype(BF16)
    dqt = jnp.dot(dyb, st.astype(BF16), preferred_element_type=F32)
    dst = dst_new * e_tot + _dot_tn(dyb, qt.astype(BF16))
    dv = _dot_nt(khat.astype(BF16), dsb)
    dkhat = jnp.dot(v.astype(BF16), dsb, preferred_element_type=F32)
    t1 = dkhat * khat
    dtot = jnp.sum(dst_new * st, axis=0, keepdims=True) * e_tot + jnp.sum(t1, axis=0, keepdims=True)
    rows = lax.broadcasted_iota(jnp.int32, cum.shape, 0)
    last = 0 if reverse else cum.shape[0] - 1
    dcum = dqt * qt - t1 + jnp.where(rows == last, dtot, 0.0)
    tiles = lambda a: [a[SUBLANES * i:SUBLANES * (i + 1)] for i in range(nt)]
    dq_t, dk_t, dv_t, dcum_t = tiles(dqt * e_cum), tiles(dkhat * e_end), tiles(dv), tiles(dcum)
    for j, l_tiles, rots in _hg_pairs(reverse):
        for sh, diag_ok in rots:
            k_j, c_j, v_j = _rot(k_t[j], sh), _rot(cum_t[j], sh), _rot(v_t[j], sh)
            acc_v = acc_k = acc_c = None
            for i in l_tiles:
                e = jnp.exp(cum_t[i] - c_j)
                if i == j:
                    e = jnp.where(diag_ok, e, 0.0)
                ke, qe = k_j * e, q_t[i] * e
                p = q_t[i] * ke
                att = jnp.sum(p, axis=1, keepdims=True)
                datt = jnp.sum(dy_t[i] * v_j, axis=1, keepdims=True)
                g = datt * p
                dq_t[i] = dq_t[i] + datt * ke
                dcum_t[i] = dcum_t[i] + g
                av, ak = att * dy_t[i], datt * qe
                acc_v, acc_k, acc_c = (av, ak, g) if acc_v is None else (acc_v + av, acc_k + ak, acc_c + g)
            back = (SUBLANES - sh) % SUBLANES
            dv_t[j] = dv_t[j] + _rot(acc_v, back)
            dk_t[j] = dk_t[j] + _rot(acc_k, back)
            dcum_t[j] = dcum_t[j] - _rot(acc_c, back)
    dlf_t, _ = _hg_cum_tiles(dcum_t, not reverse)
    return dq_t, dk_t, dlf_t, dv_t, dst


def _hg_super(s_ctx):
    return min(256, s_ctx)


def hg_fwd(name, q, k, lf, v, reverse, s_ctx):
    nb, s, w = q.shape
    nh, dk, sup = w // HG_DK, HG_DK, _hg_super(s_ctx)
    nsup, nsc, cps = s // sup, s_ctx // sup, sup // HG_CHUNK

    def body(q_ref, k_ref, lf_ref, v_ref, y_ref, hst_ref, st_s):
        i = pl.program_id(2)

        @pl.when(i == 0)
        def _():
            st_s[...] = jnp.zeros_like(st_s)

        def step(c, st):
            r0 = pl.multiple_of((cps - 1 - c if reverse else c) * HG_CHUNK, HG_CHUNK)
            tile = lambda ref: [ref[pl.ds(r0 + SUBLANES * i, SUBLANES), :] for i in range(HG_TILES)]
            hst_ref[c] = st
            y_t, st_new = _hg_chunk_fwd(tile(q_ref), tile(k_ref), tile(lf_ref), tile(v_ref), st, reverse)
            for i in range(HG_TILES):
                y_ref[pl.ds(r0 + SUBLANES * i, SUBLANES), :] = y_t[i]
            return st_new

        st_s[...] = lax.fori_loop(0, cps, step, st_s[...], unroll=2 if cps % 2 == 0 else 1)

    blk = pl.BlockSpec((None, sup, dk), lambda b, h, i: (b, _chunk_at(i, reverse, nsc, nsup), h))
    return pl.pallas_call(
        body,
        name=name,
        grid=(nb, nh, nsup),
        in_specs=[blk] * 4,
        out_specs=[blk, pl.BlockSpec((None, None, cps, dk, dk), lambda b, h, i: (b, h, i, 0, 0))],
        out_shape=[jax.ShapeDtypeStruct(q.shape, F32), jax.ShapeDtypeStruct((nb, nh, s // HG_CHUNK, dk, dk), F32)],
        scratch_shapes=[pltpu.VMEM((dk, dk), F32)],
        compiler_params=_cparams(("parallel", "parallel", "arbitrary")),
    )(q, k, lf, v)


def hg_bwd(name, q, k, lf, v, hst, dy, reverse, s_ctx):
    nb, s, w = q.shape
    nh, dk, sup = w // HG_DK, HG_DK, _hg_super(s_ctx)
    nsup, nsc, cps = s // sup, s_ctx // sup, sup // HG_CHUNK

    def body(q_ref, k_ref, lf_ref, v_ref, hst_ref, dy_ref, dq_ref, dk_ref, dlf_ref, dv_ref, dst_s):
        i = pl.program_id(2)

        @pl.when(i == 0)
        def _():
            dst_s[...] = jnp.zeros_like(dst_s)

        def step(cc, dst):
            c = cps - 1 - cc
            r0 = pl.multiple_of((cps - 1 - c if reverse else c) * HG_CHUNK, HG_CHUNK)
            tile = lambda ref: [ref[pl.ds(r0 + SUBLANES * i, SUBLANES), :] for i in range(HG_TILES)]
            dq_t, dk_t, dlf_t, dv_t, dst_prev = _hg_chunk_bwd(
                tile(q_ref), tile(k_ref), tile(lf_ref), tile(v_ref), hst_ref[c], tile(dy_ref), dst, reverse)
            for ref, val in zip((dq_ref, dk_ref, dlf_ref, dv_ref), (dq_t, dk_t, dlf_t, dv_t)):
                for i in range(HG_TILES):
                    ref[pl.ds(r0 + SUBLANES * i, SUBLANES), :] = val[i]
            return dst_prev

        dst_s[...] = lax.fori_loop(0, cps, step, dst_s[...], unroll=2 if cps % 2 == 0 else 1)

    blk = pl.BlockSpec((None, sup, dk), lambda b, h, i: (b, _chunk_at(nsup - 1 - i, reverse, nsc, nsup), h))
    return pl.pallas_call(
        body,
        name=name,
        grid=(nb, nh, nsup),
        in_specs=[blk] * 4 + [pl.BlockSpec((None, None, cps, dk, dk), lambda b, h, i: (b, h, nsup - 1 - i, 0, 0)), blk],
        out_specs=[blk] * 4,
        out_shape=[jax.ShapeDtypeStruct(q.shape, F32)] * 4,
        scratch_shapes=[pltpu.VMEM((dk, dk), F32)],
        compiler_params=_cparams(("parallel", "parallel", "arbitrary")),
    )(q, k, lf, v, hst, dy)


def f_s5_discretize(ins):
    lam_re, lam_im, log_step, b_re, b_im = ins
    step = jnp.exp(log_step)
    mag = jnp.exp(lam_re * step)
    ar, ai = mag * jnp.cos(lam_im * step), mag * jnp.sin(lam_im * step)
    den = lam_re * lam_re + lam_im * lam_im
    zr = ((ar - 1.0) * lam_re + ai * lam_im) / den
    zi = (ai * lam_re - (ar - 1.0) * lam_im) / den
    return [ar, ai, zr * b_re - zi * b_im, zr * b_im + zi * b_re]


def s5_tiles_of(m):
    g, p, k = m.shape
    gt = S5_TILE_CH // k
    eye = jnp.eye(gt, dtype=m.dtype)
    t = m.reshape(g // gt, gt, p, 1, k) * eye[None, :, None, :, None]
    return t.reshape(g // gt, gt * p, gt * k)


def s5_groups_of(t, g, p, k):
    gt = S5_TILE_CH // k
    eye = jnp.eye(gt, dtype=t.dtype)
    return jnp.sum(t.reshape(g // gt, gt, p, gt, k) * eye[None, :, None, :, None], axis=3).reshape(g, p, k)


def f_lower_bounds(ins):
    (logits,) = ins
    e = jnp.exp(logits - jnp.max(logits, axis=0, keepdims=True))
    p = e / jnp.sum(e, axis=0, keepdims=True)
    n = logits.shape[0]
    li = lax.broadcasted_iota(jnp.int32, (n, n), 0)
    si = lax.broadcasted_iota(jnp.int32, (n, n), 1)
    after_first = jnp.where(jnp.logical_and(si >= 1, si <= li), 1.0, 0.0)
    return [jnp.dot(after_first, p, precision=_HI, preferred_element_type=F32)]


def f_silu(ins):
    return [_silu(ins[0])]


def f_norm_keep(shift_row, scale_row):
    def f(tv, mv, pv):
        return [tv[0], _rms(tv[0], pv[0]) * (1.0 + mv[scale_row]) + mv[shift_row]]
    return f


def f_dt(tv, mv, pv):
    return [_softplus(tv[0] + pv[0])]


def f_even_finish(tv, mv, pv):
    y_f, y_b, xs, z, h_sum, gy = tv
    d_exp, g = pv
    y = _rms((y_f + y_b + d_exp * xs) * _silu(z), g)
    return [y, h_sum * jax.nn.gelu(gy)]


def f_odd_prep(tv, mv, pv):
    q, f_f, f_b = tv
    (lb,) = pv
    outs = [_silu(q)]
    for f in (f_f, f_b):
        outs.append((1.0 - lb) * jax.nn.sigmoid(-f))
        outs.append(jnp.log(lb + (1.0 - lb) * jax.nn.sigmoid(f)))
    return outs


def f_odd_finish(tv, mv, pv):
    o_f, o_b, g, s5y, u = tv
    norm_g, s5_d, glu_w, glu_b = pv
    o = o_f + o_b
    w = o.shape[1]
    hi = lax.broadcasted_iota(jnp.int32, (w, w), 0) // HG_DK
    hj = lax.broadcasted_iota(jnp.int32, (w, w), 1) // HG_DK
    head_mean = jnp.where(hi == hj, 1.0 / HG_DK, 0.0)
    ms = jnp.dot(o * o, head_mean, precision=_HI, preferred_element_type=F32)
    on = o * lax.rsqrt(ms + RMS_EPS) * norm_g * _silu(g)
    y = jax.nn.gelu(s5y + s5_d * u)
    gate = jax.nn.sigmoid(jnp.dot(y.astype(BF16), glu_w.astype(BF16), preferred_element_type=F32) + glu_b)
    return [on, y * gate]


def final_loss(name, s, br, mod, g, target, s_ctx):
    nb, st, d = s.shape
    tb = TOK_BLOCK
    assert s_ctx == tb

    def lossf(sv, bv, gate, gv, tv):
        y = _rms(sv + gate * bv, gv)
        err = jnp.square(y - tv)
        return 0.5 * jnp.sum(jnp.mean(err, axis=-1, keepdims=True), axis=0, keepdims=True)

    def body(s_ref, b_ref, m_ref, g_ref, t_ref, l_ref, ds_ref, db_ref, dm_ref, dg_ref):
        b, t = pl.program_id(0), pl.program_id(1)

        @pl.when(t == 0)
        def _():
            ds_ref[...] = jnp.zeros_like(ds_ref)
            db_ref[...] = jnp.zeros_like(db_ref)
            dm_ref[...] = jnp.zeros_like(dm_ref)
            l_ref[...] = jnp.zeros_like(l_ref)

        @pl.when(jnp.logical_and(b == 0, t == 0))
        def _():
            dg_ref[...] = jnp.zeros_like(dg_ref)

        @pl.when(t > 0)
        def _():
            gate = m_ref[N_MOD - 1:N_MOD, :]
            l, vjp = jax.vjp(lossf, s_ref[...], b_ref[...], gate, g_ref[...], t_ref[...])
            ds, db, dgate, dg, _ = vjp(jnp.ones((1, 1), F32))
            ds_ref[...] = ds
            db_ref[...] = db.astype(db_ref.dtype)
            dg_ref[...] += dg
            l_ref[...] += jnp.broadcast_to(l, l_ref.shape)

            @pl.when(t == 1)
            def _():
                dm_ref[...] = jnp.zeros_like(dm_ref)
                dm_ref[N_MOD - 1:N_MOD, :] = dgate

            @pl.when(t > 1)
            def _():
                dm_ref[N_MOD - 1:N_MOD, :] += dgate

    tok = pl.BlockSpec((None, tb, d), lambda b, t: (b, t, 0))
    modspec = pl.BlockSpec((None, N_MOD, d), _mod_index)
    gspec = pl.BlockSpec((1, d), lambda b, t: (0, 0))
    return pl.pallas_call(
        body,
        name=name,
        grid=(nb, st // tb),
        in_specs=[tok, tok, modspec, gspec, pl.BlockSpec((None, tb, d), lambda b, t: (b, jnp.maximum(t - 1, 0), 0))],
        out_specs=[pl.BlockSpec((None, SUBLANES, 128), lambda b, t: (b, 0, 0)), tok, tok, modspec, gspec],
        out_shape=[jax.ShapeDtypeStruct((nb, SUBLANES, 128), F32), jax.ShapeDtypeStruct(s.shape, F32),
                   jax.ShapeDtypeStruct(s.shape, BF16), jax.ShapeDtypeStruct(mod.shape, F32), jax.ShapeDtypeStruct(g.shape, F32)],
        compiler_params=_cparams(("arbitrary", "arbitrary")),
    )(s, br, mod, g, target)


def adamw(name, w, g, m, v):
    shape = w.shape
    cols = shape[-1] if w.ndim >= 2 else w.size
    rows = w.size // cols
    tr = _pick(rows, (512, 256, 128, 64, 32, 16, 8))

    def body(w_ref, g_ref, m_ref, v_ref, d_ref, nm_ref, nv_ref):
        gv = g_ref[...]
        nm = ADAM_B1 * m_ref[...] + (1.0 - ADAM_B1) * gv
        nv = ADAM_B2 * v_ref[...] + (1.0 - ADAM_B2) * jnp.square(gv)
        m_hat = nm / (1.0 - ADAM_B1 ** ADAM_STEP)
        v_hat = nv / (1.0 - ADAM_B2 ** ADAM_STEP)
        d_ref[...] = -ADAM_LR * (m_hat / (jnp.sqrt(v_hat) + ADAM_EPS) + ADAM_WD * w_ref[...])
        nm_ref[...] = nm
        nv_ref[...] = nv

    spec = pl.BlockSpec((tr, cols), lambda i: (i, 0))
    outs = pl.pallas_call(
        body,
        name=name,
        grid=(rows // tr,),
        in_specs=[spec] * 4,
        out_specs=[spec] * 3,
        out_shape=[jax.ShapeDtypeStruct((rows, cols), F32)] * 3,
        compiler_params=_cparams(("parallel",)),
    )(*(a.reshape(rows, cols) for a in (w, g, m, v)))
    return tuple(o.reshape(shape) for o in outs)


EV_COLS = {"z": (0, 1024), "xbc": (1024, 2560), "dt": (2560, 2592), "gy": (2592, 3616), "u": (3616, 4640)}
OD_COLS = {"q": (0, 768), "ff": (768, 1536), "fb": (1536, 2304), "v": (2304, 3072), "g": (3072, 3840), "u": (3840, 4096)}
EV_OUT_ROWS = ((0, 1024), (1024, 2048))
OD_OUT_ROWS = ((0, 768), (768, 1024))
LANES = 128


def _pad_to_lanes(w):
    n = w.shape[1]
    return w if n % LANES == 0 else jnp.pad(w, ((0, 0), (0, LANES - n % LANES)))


def _layer_weights(l, big):
    j = l // 2
    even = l % 2 == 0
    w_in = big["ev_w_in" if even else "od_w_in"][j]
    w_out = big["ev_w_out" if even else "od_w_out"][j]
    lw = {"in": {}, "out": []}
    for name, (a, b) in (EV_COLS if even else OD_COLS).items():
        w = _pad_to_lanes(w_in[:, a:b])
        lw["in"][name] = (w, w.T)
    for a, b in (EV_OUT_ROWS if even else OD_OUT_ROWS):
        lw["out"].append((w_out[a:b], w_out[a:b].T))
    for name in ("gate", "up", "down"):
        w = big["ffn_w_" + name][l]
        lw[name] = (w, w.T)
    return lw


def _rows2d(a):
    return a.reshape(-1, a.shape[-1])


def _mm3(a, w, name, out_dtype=F32):
    return mm([(_rows2d(a), w)], name, out_dtype).reshape(a.shape[:-1] + (w.shape[1],))


def _wgrad(a, d, name):
    return mm_tn(_rows2d(a), _rows2d(d), name)


def _dgrad(pairs, name, shape3):
    return mm([(_rows2d(d), wt) for d, wt in pairs], name).reshape(shape3[:-1] + (pairs[0][1].shape[1],))


def _dir_dt(dt, d):
    nb, s, _ = dt.shape
    dd = dt[:, :, SSD_HEADS * d:SSD_HEADS * (d + 1)].reshape(nb, s, SSD_GROUPS, SSD_HPG)
    return jnp.transpose(dd, (0, 2, 1, 3)), jnp.transpose(dd, (0, 2, 3, 1))


def _s5_prepare(p, j, tag):
    g_, p_, k_ = S5_GROUPS, S5_STATE, S5_GROUP_CH
    col = lambda t: t.reshape(g_ * p_, 1)
    ins, outs = [], []
    for d in (0, 1):
        i_d = [col(p["s5_lam_re"][j, d]), col(p["s5_lam_im"][j, d]), col(jnp.repeat(p["s5_log_step"][j, d], p_)),
               p["s5_b_re"][j].reshape(g_ * p_, k_), p["s5_b_im"][j].reshape(g_ * p_, k_)]
        ins.append(i_d)
        outs.append(small_fwd(f"{tag}_disc{d}", f_s5_discretize, i_d, [(g_ * p_, 1)] * 2 + [(g_ * p_, k_)] * 2))
    lam_r = jnp.stack([o[0].reshape(1, g_ * p_) for o in outs])
    lam_i = jnp.stack([o[1].reshape(1, g_ * p_) for o in outs])
    bt_r = jnp.stack([s5_tiles_of(o[2].reshape(g_, p_, k_)) for o in outs])
    bt_i = jnp.stack([s5_tiles_of(o[3].reshape(g_, p_, k_)) for o in outs])
    ct_r = jnp.stack([s5_tiles_of(jnp.transpose(p["s5_c_re"][j, d], (0, 2, 1))) for d in (0, 1)])
    ct_i = jnp.stack([s5_tiles_of(jnp.transpose(p["s5_c_im"][j, d], (0, 2, 1))) for d in (0, 1)])
    return ins, (lam_r, lam_i, bt_r, bt_i, ct_r, ct_i)


def _s5_param_grads(ins, grads, tag):
    g_, p_, k_ = S5_GROUPS, S5_STATE, S5_GROUP_CH
    dlr, dli, dbtr, dbti, dctr, dcti = grads
    g_lre, g_lim, g_ls, g_bre, g_bim = [], [], [], 0.0, 0.0
    for d in (0, 1):
        cots = [dlr[d].reshape(g_ * p_, 1), dli[d].reshape(g_ * p_, 1),
                s5_groups_of(dbtr[d], g_, p_, k_).reshape(g_ * p_, k_), s5_groups_of(dbti[d], g_, p_, k_).reshape(g_ * p_, k_)]
        g = small_bwd(f"{tag}_disc_bwd{d}", f_s5_discretize, ins[d], cots)
        g_lre.append(g[0].reshape(g_, p_))
        g_lim.append(g[1].reshape(g_, p_))
        g_ls.append(g[2].reshape(g_, p_).sum(-1))
        g_bre = g_bre + g[3].reshape(g_, p_, k_)
        g_bim = g_bim + g[4].reshape(g_, p_, k_)
    g_cre = jnp.stack([jnp.transpose(s5_groups_of(dctr[d], g_, p_, k_), (0, 2, 1)) for d in (0, 1)])
    g_cim = jnp.stack([jnp.transpose(s5_groups_of(dcti[d], g_, p_, k_), (0, 2, 1)) for d in (0, 1)])
    return jnp.stack(g_lre), jnp.stack(g_lim), jnp.stack(g_ls), g_bre, g_bim, g_cre, g_cim


def _even_mixer_fwd(l, hn, p, lw, s_ctx):
    j = l // 2
    t1 = taps_1d(4, s_ctx, hn.shape[1])
    r = {"hn": hn}
    proj = {n: _mm3(hn, lw["in"][n][0], f"l{l}_proj_{n}") for n in EV_COLS}
    r["z"], r["xbc"], r["gy"], r["u"] = proj["z"], proj["xbc"], proj["gy"], proj["u"]
    r["dtp"] = proj["dt"][:, :, :2 * SSD_HEADS]
    r["xbc_c"] = conv_fwd(f"l{l}_ssd_conv", r["xbc"], p["ssd_conv_w"][j], p["ssd_conv_b"][j][None], t1, "silu")
    r["u_c"] = conv_fwd(f"l{l}_lru_conv", r["u"], p["lru_conv_w"][j], p["lru_conv_b"][j][None], t1, "none")
    r["dt_bias"] = p["ssd_dt_bias"][j].reshape(1, 2 * SSD_HEADS)
    (r["dt"],) = tok_fwd(f"l{l}_dt", f_dt, [r["dtp"]], None, [r["dt_bias"]], [2 * SSD_HEADS], [F32])
    r["ys"], r["hst"], r["dts"], r["alog"] = [], [], [], []
    for d in (0, 1):
        dtc, dtr = _dir_dt(r["dt"], d)
        al = p["ssd_a_log"][j, d].reshape(SSD_GROUPS, SSD_HPG)
        al_r, al_c = al[:, None, :], al[:, :, None]
        y, hst = ssd_fwd(f"l{l}_ssd_fwd{d}", r["xbc_c"], dtc, dtr, al_r, al_c, bool(d), s_ctx)
        r["ys"].append(y)
        r["hst"].append(hst)
        r["dts"].append((dtc, dtr))
        r["alog"].append((al_r, al_c))
    v4 = lambda t: t.reshape(2, LRU_BLOCKS, 1, LRU_BLOCK_W)
    r["lru_p"] = (p["lru_w_a"][j], v4(p["lru_b_a"][j]), p["lru_w_i"][j], v4(p["lru_b_i"][j]), v4(p["lru_lam"][j]))
    r["h_sum"] = lru_fwd(f"l{l}_lru_fwd", r["u_c"], *r["lru_p"], s_ctx)
    r["xs"] = r["xbc_c"][:, :, :SSD_HEADS * SSD_HEAD_DIM]
    r["fin_p"] = [jnp.repeat(p["ssd_d"][j], SSD_HEAD_DIM)[None], p["ssd_norm_g"][j][None]]
    r["fin_in"] = [r["ys"][0], r["ys"][1], r["xs"], r["z"], r["h_sum"], r["gy"]]
    r["o"] = tok_fwd(f"l{l}_even_finish", f_even_finish, r["fin_in"], None, r["fin_p"], [1024, 1024], [BF16, BF16])
    return r


def _even_mixer_bwd(l, r, dox, p, lw, s_ctx, grads):
    j = l // 2
    shape3 = dox.shape
    t1 = taps_1d(4, s_ctx, shape3[1])
    grads["ev_w_out"][j] = jnp.concatenate([_wgrad(o, dox, f"l{l}_dwout{i}") for i, o in enumerate(r["o"])], axis=0)
    do = [_dgrad([(dox, lw["out"][i][1])], f"l{l}_dout{i}", shape3) for i in range(2)]
    (dy, _, dxs, dz, dh_sum, dgy), _, (dd_exp, grads["ssd_norm_g"][j]) = tok_bwd(
        f"l{l}_even_finish_bwd", f_even_finish, r["fin_in"], None, r["fin_p"], do, [F32, F32, F32, BF16, F32, BF16])
    grads["ssd_d"][j] = dd_exp.reshape(SSD_HEADS, SSD_HEAD_DIM).sum(-1)
    du_c, dwa, dba, dwi, dbi, dlam = lru_bwd(f"l{l}_lru_bwd", r["u_c"], *r["lru_p"], dh_sum, s_ctx)
    grads["lru_w_a"][j], grads["lru_w_i"][j] = dwa, dwi
    v2 = lambda t: t.reshape(2, LRU_BLOCKS * LRU_BLOCK_W)
    grads["lru_b_a"][j], grads["lru_b_i"][j], grads["lru_lam"][j] = v2(dba), v2(dbi), v2(dlam)
    dx_sum, dbm_sum, dcm_sum, ddts, dalog = dxs, 0.0, 0.0, [], []
    for d in (0, 1):
        dx, dbm, dcm, ddtc, ddtr, dar, dac = ssd_bwd(
            f"l{l}_ssd_bwd{d}", r["xbc_c"], *r["dts"][d], *r["alog"][d], r["hst"][d], dy, bool(d), s_ctx)
        dx_sum, dbm_sum, dcm_sum = dx_sum + dx, dbm_sum + dbm, dcm_sum + dcm
        ddts.append((jnp.transpose(ddtc, (0, 2, 1, 3)) + jnp.transpose(ddtr, (0, 3, 1, 2))).reshape(shape3[0], shape3[1], SSD_HEADS))
        dalog.append((dar.sum(0)[:, 0, :] + dac.sum(0)[:, :, 0]).reshape(SSD_HEADS))
    grads["ssd_a_log"][j] = jnp.stack(dalog)
    dxbc_c = jnp.concatenate([dx_sum, dbm_sum, dcm_sum], axis=-1)
    (ddtp,), _, (ddt_bias,) = tok_bwd(f"l{l}_dt_bwd", f_dt, [r["dtp"]], None, [r["dt_bias"]], [jnp.concatenate(ddts, axis=-1)], [F32])
    grads["ssd_dt_bias"][j] = ddt_bias.reshape(2, SSD_HEADS)
    dxbc, grads["ssd_conv_w"][j], dcb = conv_bwd(f"l{l}_ssd_conv_bwd", r["xbc"], p["ssd_conv_w"][j], p["ssd_conv_b"][j][None], dxbc_c, t1, "silu", dx_dtype=BF16)
    du, grads["lru_conv_w"][j], dlb = conv_bwd(f"l{l}_lru_conv_bwd", r["u"], p["lru_conv_w"][j], p["lru_conv_b"][j][None], du_c, t1, "none", dx_dtype=BF16)
    grads["ssd_conv_b"][j], grads["lru_conv_b"][j] = dcb[0], dlb[0]
    dproj = {"z": dz, "xbc": dxbc, "dt": _pad_to_lanes(_rows2d(ddtp)).reshape(shape3[:2] + (LANES,)), "gy": dgy, "u": du}
    grads["ev_w_in"][j] = jnp.concatenate(
        [_wgrad(r["hn"], dproj[n], f"l{l}_dwin_{n}")[:, :b - a] for n, (a, b) in EV_COLS.items()], axis=1)
    return _dgrad([(dproj[n], lw["in"][n][1]) for n in EV_COLS], f"l{l}_dhn", shape3)


def _odd_mixer_fwd(l, hn, p, lw, lb_row, s_ctx):
    j = l // 2
    r = {"hn": hn}
    proj = {n: _mm3(hn, lw["in"][n][0], f"l{l}_proj_{n}") for n in OD_COLS}
    r["v"], r["g"], r["u"] = proj["v"], proj["g"], proj["u"]
    r["prep_in"] = [proj["q"], proj["ff"], proj["fb"]]
    r["lb"] = lb_row
    r["prep"] = tok_fwd(f"l{l}_odd_prep", f_odd_prep, r["prep_in"], None, [lb_row], [HG_W] * 5, [F32] * 5)
    qs = r["prep"][0]
    r["os"], r["hst"] = [], []
    for d in (0, 1):
        o, hst = hg_fwd(f"l{l}_hg_fwd{d}", qs, r["prep"][1 + 2 * d], r["prep"][2 + 2 * d], r["v"], bool(d), s_ctx)
        r["os"].append(o)
        r["hst"].append(hst)
    r["s5_ins"], r["s5_p"] = _s5_prepare(p, j, f"l{l}_s5")
    r["s5y"] = s5_fwd(f"l{l}_s5_fwd", r["u"], *r["s5_p"], s_ctx)
    r["fin_p"] = [p["hg_norm_g"][j].reshape(1, HG_W), p["s5_d"][j][None], p["s5_glu_w"][j], p["s5_glu_b"][j][None]]
    r["fin_in"] = [r["os"][0], r["os"][1], r["g"], r["s5y"], r["u"]]
    r["o"] = tok_fwd(f"l{l}_odd_finish", f_odd_finish, r["fin_in"], None, r["fin_p"], [HG_W, S5_W], [BF16, BF16])
    return r


def _odd_mixer_bwd(l, r, dox, p, lw, s_ctx, grads):
    j = l // 2
    shape3 = dox.shape
    grads["od_w_out"][j] = jnp.concatenate([_wgrad(o, dox, f"l{l}_dwout{i}") for i, o in enumerate(r["o"])], axis=0)
    do = [_dgrad([(dox, lw["out"][i][1])], f"l{l}_dout{i}", shape3) for i in range(2)]
    (do_hg, _, dg, ds5y, du_fin), _, (dng, grads["s5_d"][j], grads["s5_glu_w"][j], dglu_b) = tok_bwd(
        f"l{l}_odd_finish_bwd", f_odd_finish, r["fin_in"], None, r["fin_p"], do, [F32, F32, BF16, F32, F32])
    grads["hg_norm_g"][j] = dng.reshape(HG_HEADS, HG_DK)
    grads["s5_d"][j], grads["s5_glu_b"][j] = grads["s5_d"][j][0], dglu_b[0]
    s5g = s5_bwd(f"l{l}_s5_bwd", r["u"], *r["s5_p"], ds5y, s_ctx)
    du = s5g[0] + du_fin
    (grads["s5_lam_re"][j], grads["s5_lam_im"][j], grads["s5_log_step"][j], grads["s5_b_re"][j], grads["s5_b_im"][j],
     grads["s5_c_re"][j], grads["s5_c_im"][j]) = _s5_param_grads(r["s5_ins"], s5g[1:], f"l{l}_s5")
    qs = r["prep"][0]
    dqs, dv, dprep = 0.0, 0.0, [None] * 5
    for d in (0, 1):
        dq, dk, dlf, dvd = hg_bwd(f"l{l}_hg_bwd{d}", qs, r["prep"][1 + 2 * d], r["prep"][2 + 2 * d], r["v"], r["hst"][d], do_hg, bool(d), s_ctx)
        dqs, dv = dqs + dq, dv + dvd
        dprep[1 + 2 * d], dprep[2 + 2 * d] = dk, dlf
    dprep[0] = dqs
    (dq_, dff, dfb), _, (dlb,) = tok_bwd(f"l{l}_odd_prep_bwd", f_odd_prep, r["prep_in"], None, [r["lb"]], dprep, [BF16] * 3)
    dproj = {"q": dq_, "ff": dff, "fb": dfb, "v": dv, "g": dg, "u": du}
    grads["od_w_in"][j] = jnp.concatenate([_wgrad(r["hn"], dproj[n], f"l{l}_dwin_{n}") for n in OD_COLS], axis=1)
    return _dgrad([(dproj[n], lw["in"][n][1]) for n in OD_COLS], f"l{l}_dhn", shape3), dlb


def _ffn_fwd(l, fn, p, lw, s_ctx):
    r = {"fn": fn}
    tg = taps_grid(s_ctx, fn.shape[1], GRID_W)
    r["a"] = _mm3(fn, lw["gate"][0], f"l{l}_ffn_gate")
    r["up"] = _mm3(fn, lw["up"][0], f"l{l}_ffn_up")
    r["cw"], r["cb"] = p["ffn_conv_w"][l].reshape(9, D_FF), p["ffn_conv_b"][l][None]
    r["act"] = conv_fwd(f"l{l}_ffn_conv", r["a"], r["cw"], r["cb"], tg, "silu_mul", mul=r["up"], out_dtype=BF16)
    return r, _mm3(r["act"], lw["down"][0], f"l{l}_ffn_down")


def _ffn_bwd(l, r, dfo, lw, s_ctx, grads):
    shape3 = dfo.shape
    tg = taps_grid(s_ctx, shape3[1], GRID_W)
    grads["ffn_w_down"][l] = _wgrad(r["act"], dfo, f"l{l}_dwdown")
    dact = _dgrad([(dfo, lw["down"][1])], f"l{l}_dact", shape3)
    da, dcw, dcb, dup = conv_bwd(f"l{l}_ffn_conv_bwd", r["a"], r["cw"], r["cb"], dact, tg, "silu_mul", mul=r["up"], dx_dtype=BF16)
    grads["ffn_conv_w"][l], grads["ffn_conv_b"][l] = dcw.reshape(3, 3, D_FF), dcb[0]
    grads["ffn_w_gate"][l] = _wgrad(r["fn"], da, f"l{l}_dwgate")
    grads["ffn_w_up"][l] = _wgrad(r["fn"], dup, f"l{l}_dwup")
    return _dgrad([(da, lw["gate"][1]), (dup, lw["up"][1])], f"l{l}_dfn", shape3)


BIG_WEIGHTS = ("ev_w_in", "ev_w_out", "od_w_in", "od_w_out", "ffn_w_gate", "ffn_w_up", "ffn_w_down")
PER_LAYER = {"norm_mix_g": DEPTH, "norm_ffn_g": DEPTH, "ffn_w_gate": DEPTH, "ffn_w_up": DEPTH, "ffn_conv_w": DEPTH,
             "ffn_conv_b": DEPTH, "ffn_w_down": DEPTH}


def local_step(x, ctx, target, modtabs, p, big, s_ctx=CTX_LEN):
    d_model = x.shape[-1]
    s0 = jnp.concatenate([ctx, x], axis=1)
    lws = [_layer_weights(l, big) for l in range(DEPTH)]
    shapes = {n: v.shape for n, v in {**p, **big}.items()}
    grads = {n: [None] * PER_LAYER.get(n, DEPTH // 2) for n in shapes if n not in ("c_ctx", "w_mod", "b_mod", "final_norm_g", "hg_lb_logits")}
    (lbs,) = small_fwd("lower_bounds", f_lower_bounds, [p["hg_lb_logits"]], [p["hg_lb_logits"].shape])
    tab_a = [modtabs[0]] + [modtabs[l].at[:, N_MOD - 1].set(modtabs[l - 1][:, N_MOD - 1]) for l in range(1, DEPTH)]
    res = []
    s, br = s0, None
    for l in range(DEPTH):
        r = {}
        g_mix, g_ffn = p["norm_mix_g"][l][None], p["norm_ffn_g"][l][None]
        if l == 0:
            (hn,) = tok_fwd("l0_norm", f_norm(0, 1), [s], tab_a[0], [g_mix], [d_model], [BF16])
            r["a_in"] = [s]
        else:
            r["a_in"] = [s, br]
            s, hn = tok_fwd(f"l{l}_resnorm_a", f_resnorm(5, 0, 1), r["a_in"], tab_a[l], [g_mix], [d_model] * 2, [F32, BF16])
        if l % 2 == 0:
            r["mix"] = _even_mixer_fwd(l, hn, p, lws[l], s_ctx)
        else:
            r["mix"] = _odd_mixer_fwd(l, hn, p, lws[l], lbs[l:l + 1], s_ctx)
        ox = mm([(_rows2d(o), w) for o, (w, _) in zip(r["mix"]["o"], lws[l]["out"])], f"l{l}_mix_out").reshape(s.shape)
        r["b_in"] = [s, ox]
        s, fn = tok_fwd(f"l{l}_resnorm_b", f_resnorm(2, 3, 4), r["b_in"], modtabs[l], [g_ffn], [d_model] * 2, [F32, BF16])
        r["ffn"], br = _ffn_fwd(l, fn, p, lws[l], s_ctx)
        res.append(r)

    loss_blk, ds, dbr, dtab_f, dfinal_g = final_loss("final_loss", s, br, modtabs[DEPTH - 1], p["final_norm_g"][None], target, s_ctx)
    grads["final_norm_g"] = dfinal_g[0]
    dmod = [None] * DEPTH
    dtab_next = dtab_f
    dlb = jnp.zeros_like(lbs)
    for l in reversed(range(DEPTH)):
        r = res[l]
        g_mix, g_ffn = p["norm_mix_g"][l][None], p["norm_ffn_g"][l][None]
        dfn = _ffn_bwd(l, r["ffn"], dbr, lws[l], s_ctx, grads)
        (ds, dox), dtab_b, (grads["norm_ffn_g"][l],) = tok_bwd(
            f"l{l}_resnorm_b_bwd", f_resnorm(2, 3, 4), r["b_in"], modtabs[l], [g_ffn], [ds, dfn], [F32, BF16])
        if l % 2 == 0:
            dhn = _even_mixer_bwd(l, r["mix"], dox, p, lws[l], s_ctx, grads)
        else:
            dhn, dlb_l = _odd_mixer_bwd(l, r["mix"], dox, p, lws[l], s_ctx, grads)
            dlb = dlb.at[l:l + 1].set(dlb_l)
        if l == 0:
            (ds,), dtab_a, (dg,) = tok_bwd("l0_norm_bwd", f_norm_keep(0, 1), r["a_in"], tab_a[0], [g_mix], [ds, dhn], [F32])
        else:
            (ds, dbr), dtab_a, (dg,) = tok_bwd(
                f"l{l}_resnorm_a_bwd", f_resnorm(5, 0, 1), r["a_in"], tab_a[l], [g_mix], [ds, dhn], [F32, BF16])
        grads["norm_mix_g"][l] = dg
        dmod[l] = (dtab_a.at[:, N_MOD - 1].set(0.0) + dtab_b).at[:, N_MOD - 1].set(dtab_next[:, N_MOD - 1])
        dtab_next = dtab_a
    (grads["hg_lb_logits"],) = small_bwd("lower_bounds_bwd", f_lower_bounds, [p["hg_lb_logits"]], [dlb])
    out = {}
    for n, g in grads.items():
        if isinstance(g, list):
            g = jnp.stack([t.reshape(shapes[n][1:]) for t in g])
        out[n] = g.reshape(shapes[n])
    return loss_blk[:, 0, 0], ds[:, s_ctx:], dmod, out


WEIGHT_NAMES = (
    "c_ctx", "w_mod", "b_mod", "norm_mix_g", "norm_ffn_g", "final_norm_g", "ev_w_in", "ev_w_out", "ssd_conv_w",
    "ssd_conv_b", "ssd_dt_bias", "ssd_a_log", "ssd_d", "ssd_norm_g", "lru_conv_w", "lru_conv_b", "lru_w_a", "lru_b_a",
    "lru_w_i", "lru_b_i", "lru_lam", "od_w_in", "od_w_out", "hg_lb_logits", "hg_norm_g", "s5_lam_re", "s5_lam_im",
    "s5_log_step", "s5_b_re", "s5_b_im", "s5_c_re", "s5_c_im", "s5_d", "s5_glu_w", "s5_glu_b", "ffn_w_gate", "ffn_w_up",
    "ffn_conv_w", "ffn_conv_b", "ffn_w_down")
INPUT_NAMES = ("x", "c", "ctx") + WEIGHT_NAMES + ("loss_target",) + tuple("m_" + n for n in WEIGHT_NAMES) + tuple("v_" + n for n in WEIGHT_NAMES)
SHARD_AXIS = {"w_mod": 2, "ev_w_in": 2, "ev_w_out": 1, "ssd_conv_w": 2, "lru_conv_w": 2, "lru_b_a": 2, "lru_b_i": 2,
              "lru_lam": 2, "od_w_in": 2, "od_w_out": 1, "s5_d": 1, "s5_glu_w": 1, "s5_glu_b": 1, "ffn_w_gate": 2,
              "ffn_w_up": 2, "ffn_conv_w": 3, "ffn_w_down": 1}
SMALL_SHARDED = tuple(n for n in WEIGHT_NAMES if n in SHARD_AXIS and n not in BIG_WEIGHTS and n != "w_mod")
REPLICATED_LOCAL = tuple(n for n in WEIGHT_NAMES if n not in SHARD_AXIS and n not in ("c_ctx", "b_mod"))
PACK_WIDTH = 1024
MOD_ROWS = 48
CTX_ROW = 32


def _unshard(g8, axis):
    moved = jnp.moveaxis(g8, 0, axis)
    shp = moved.shape
    return moved.reshape(shp[:axis] + (shp[axis] * shp[axis + 1],) + shp[axis + 2:])


def _to_shards(full, axis):
    shp = full.shape
    return jnp.moveaxis(full.reshape(shp[:axis] + (N_DEV, shp[axis] // N_DEV) + shp[axis + 1:]), axis, 0)


def _pack(arrs, dtype, lead=(), row_mult=16):
    flat = jnp.concatenate([a.astype(dtype).reshape(lead + (-1,)) for a in arrs], axis=-1)
    n = flat.shape[-1]
    unit = row_mult * PACK_WIDTH
    padded = -(-n // unit) * unit
    flat = jnp.pad(flat, [(0, 0)] * len(lead) + [(0, padded - n)])
    return flat.reshape(lead + (padded // PACK_WIDTH, PACK_WIDTH))


def _unpack(packed, shapes, lead=()):
    flat = packed.reshape(lead + (-1,))
    out, off = [], 0
    for shp in shapes:
        n = math.prod(shp)
        out.append(flat[..., off:off + n].reshape(lead + tuple(shp)))
        off += n
    return out


def _my_block(full, axis, me):
    loc = full.shape[axis] // N_DEV
    return lax.dynamic_slice_in_dim(full, me * loc, loc, axis)


def kernel(*args):
    a = dict(zip(INPUT_NAMES, args))
    px, py, pc = _my_pos()
    me = 4 * px + 2 * py + pc
    nb = a["x"].shape[0]

    small_names = ("c",) + SMALL_SHARDED
    *big8, small8 = all_gather([a[n].astype(BF16) for n in BIG_WEIGHTS] + [_pack([a[n] for n in small_names], F32)],
                               "gather_weights")
    big = {n: _unshard(g, SHARD_AXIS[n]) for n, g in zip(BIG_WEIGHTS, big8)}
    small = dict(zip(small_names, _unpack(small8, [a[n].shape for n in small_names], (N_DEV,))))
    p = {n: a[n] for n in WEIGHT_NAMES if n not in SHARD_AXIS}
    for n in SMALL_SHARDED:
        p[n] = _unshard(small[n], SHARD_AXIS[n])
    c_all = small["c"].reshape(N_DEV * nb, D_MODEL)

    rows = jnp.concatenate([c_all, a["c_ctx"][None], jnp.zeros((MOD_ROWS - CTX_ROW - 1, D_MODEL), F32)], axis=0)
    (srows,) = small_fwd("mod_silu", f_silu, [rows], [rows.shape])
    wmod2d = jnp.transpose(a["w_mod"], (1, 0, 2)).reshape(D_MODEL, -1).astype(BF16)
    cols = a["w_mod"].shape[2]
    mod_loc = mm([(srows, wmod2d)], "mod_proj")
    mod8 = all_gather([mod_loc], "gather_mod")[0].reshape(N_DEV, MOD_ROWS, DEPTH, cols)
    mod_all = jnp.transpose(mod8, (2, 1, 0, 3)).reshape(DEPTH, MOD_ROWS, N_DEV * cols) + a["b_mod"][:, None, :]
    modtabs = []
    for l in range(DEPTH):
        mine = lax.dynamic_slice_in_dim(mod_all[l], me * nb, nb, 0).reshape(nb, N_MOD, D_MODEL)
        ctx_row = jnp.broadcast_to(mod_all[l, CTX_ROW].reshape(1, N_MOD, D_MODEL), (nb, N_MOD, D_MODEL))
        modtabs.append(jnp.stack([ctx_row, mine], axis=1).reshape(2 * nb, N_MOD, D_MODEL))

    loss_b, grad_x, dmod, grads = local_step(a["x"], a["ctx"], a["loss_target"], modtabs, p, big)

    dm = jnp.stack([t.reshape(nb, 2, N_MOD * D_MODEL) for t in dmod])
    dloc = jnp.concatenate([dm[:, :, 1], jnp.sum(dm[:, :, 0], axis=1, keepdims=True),
                            jnp.zeros((DEPTH, SUBLANES - nb - 1, N_MOD * D_MODEL), F32)], axis=1)
    d8 = all_gather([dloc.reshape(DEPTH * SUBLANES, -1)], "gather_dmod")[0].reshape(N_DEV, DEPTH, SUBLANES, -1)
    d_rows = jnp.transpose(d8[:, :, :nb], (1, 0, 2, 3)).reshape(DEPTH, N_DEV * nb, -1)
    d_ctx = jnp.sum(d8[:, :, nb], axis=0)[:, None]
    d_full = jnp.concatenate([d_rows, d_ctx, jnp.zeros((DEPTH, MOD_ROWS - CTX_ROW - 1, N_MOD * D_MODEL), F32)], axis=1)
    grads["b_mod"] = jnp.sum(d_full, axis=1)
    d_cols = jnp.transpose(_my_block(d_full, 2, me), (1, 0, 2)).reshape(MOD_ROWS, DEPTH * cols)
    g_wmod = mm([(srows.T, d_cols)], "mod_dw")
    g_wmod_local = jnp.transpose(g_wmod.reshape(D_MODEL, DEPTH, cols), (1, 0, 2))
    d_srows_part = mm([(d_cols[CTX_ROW:CTX_ROW + SUBLANES], wmod2d.T)], "mod_dctx")[0]

    reduce_names = REPLICATED_LOCAL + SMALL_SHARDED
    parts = [jnp.sum(loss_b).reshape(1), d_srows_part] + [grads[n] for n in reduce_names]
    packed = _pack(parts, F32, row_mult=SUBLANES * N_DEV)
    own = [_to_shards(grads[n], SHARD_AXIS[n]).astype(BF16).reshape(N_CHIPS, 2, -1, a[n].shape[-1]) for n in BIG_WEIGHTS]
    own.append(packed.reshape(N_CHIPS, 2, -1, PACK_WIDTH))
    names = BIG_WEIGHTS + ("small",)
    from_sibling = sibling_swap(own, "exchange_sibling_grads")
    chip_sums = [pair_sum(o, s, "pair_sum_" + n) for n, o, s in zip(names, own, from_sibling)]
    got = chip_exchange(chip_sums, "exchange_chip_grads")
    eighths = [sum_slots(t, "sum_" + n) for n, t in zip(names, got)]
    (small8,) = all_gather([eighths[-1]], "gather_small_sums")
    totals = _unpack(small8, [(1,), (D_MODEL,)] + [grads[n].shape for n in reduce_names])
    loss = totals[0][0]
    d_srows = jnp.zeros_like(rows).at[CTX_ROW].set(totals[1])
    (d_rows_in,) = small_bwd("mod_silu_bwd", f_silu, [rows], [d_srows])
    g_local = {"c_ctx": d_rows_in[CTX_ROW], "b_mod": grads["b_mod"], "w_mod": g_wmod_local}
    for n, t in zip(reduce_names, totals[2:]):
        g_local[n] = _my_block(t, SHARD_AXIS[n], me) if n in SHARD_AXIS else t
    for n, t in zip(BIG_WEIGHTS, eighths):
        g_local[n] = t.reshape(a[n].shape)

    deltas, new_m, new_v = [], [], []
    for n in WEIGHT_NAMES:
        d, m, v = adamw("adamw_" + n, a[n], g_local[n], a["m_" + n], a["v_" + n])
        deltas.append(d)
        new_m.append(m)
        new_v.append(v)
    return (loss, grad_x, *[g_local[n] for n in WEIGHT_NAMES], *deltas, *new_m, *new_v)
```

```python
import functools
import math

import jax
import jax.numpy as jnp
from jax import lax
from jax.experimental import pallas as pl
from jax.experimental.pallas import tpu as pltpu

F32 = jnp.float32
BF16 = jnp.bfloat16

D_MODEL = 1024
DEPTH = 4
CTX_LEN = 256
SEQ = 2048
S_TOT = CTX_LEN + SEQ
GRID_W = 64
N_MOD = 6
RMS_EPS = 1e-6
N_DEV = 8

SSD_HEADS = 16
SSD_HEAD_DIM = 64
SSD_GROUPS = 2
SSD_HPG = 8
SSD_STATE = 128
SSD_CHUNK = 128
SSD_W = SSD_HEADS * SSD_HEAD_DIM
LRU_BLOCKS = 8
LRU_BLOCK_W = 128
LRU_C = 8.0
HG_W = 768
HG_HEADS = 6
HG_DK = 128
HG_CHUNK = 32
S5_W = 256
S5_GROUPS = 16
S5_GROUP_CH = 16
S5_STATE = 64
D_FF = 2816

ADAM_LR = 0.001
ADAM_B1 = 0.9
ADAM_B2 = 0.999
ADAM_EPS = 1e-08
ADAM_WD = 0.01
ADAM_STEP = 10

TOK_BLOCK = CTX_LEN
SUBLANES = 8
VMEM_LIMIT_BYTES = 56 * 1024 * 1024
MM_BLOCK_BYTES = 8 * 1024 * 1024
MM_TILES = (1408, 1024, 768, 704, 512, 384, 352, 256, 128, 64, 48, 40, 32, 16, 8)
MM_ROW_TILES = (2304, 2048, 1152, 1024, 512, 256, 128, 64, 48, 32, 16, 8)
LANE_TILE = 128


def _cparams(sem=None):
    kw = dict(vmem_limit_bytes=VMEM_LIMIT_BYTES)
    if sem is not None:
        kw["dimension_semantics"] = sem
    return pltpu.CompilerParams(**kw)


def _pick(n, cands):
    for c in cands:
        if n % c == 0:
            return c
    return n


def mm(pairs, name, out_dtype=F32):
    m = pairs[0][0].shape[0]
    n = pairs[0][1].shape[1]
    kdims = [a.shape[1] for a, _ in pairs]
    ktile = None
    if len(pairs) == 1 and kdims[0] > 4096:
        ktile = _pick(kdims[0], (2304, 2048, 1024))
    nk = kdims[0] // ktile if ktile else 1
    col_bytes = sum((ktile or w.shape[0]) * w.dtype.itemsize for _, w in pairs)
    tn = _pick(n, tuple(c for c in MM_TILES if c % LANE_TILE == 0 and c * col_bytes <= MM_BLOCK_BYTES))
    row_bytes = max(sum((ktile or a.shape[1]) * a.dtype.itemsize for a, _ in pairs), tn * 4)
    tm = _pick(m, tuple(c for c in MM_TILES if c * row_bytes <= MM_BLOCK_BYTES))
    npairs = len(pairs)
    if nk > 1:
        assert out_dtype == F32

    def body(*refs):
        o_ref = refs[2 * npairs]
        acc = None
        for i in range(npairs):
            a = refs[2 * i][...].astype(BF16)
            w = refs[2 * i + 1][...].astype(BF16)
            p = jnp.dot(a, w, preferred_element_type=F32)
            acc = p if acc is None else acc + p
        if nk == 1:
            o_ref[...] = acc.astype(out_dtype)
        else:
            k = pl.program_id(2)

            @pl.when(k == 0)
            def _():
                o_ref[...] = acc

            @pl.when(k > 0)
            def _():
                o_ref[...] += acc

    in_specs = []
    args = []
    for a, w in pairs:
        kk = a.shape[1]
        assert w.shape == (kk, n) and a.shape[0] == m, (a.shape, w.shape)
        tk = ktile if ktile else kk
        in_specs.append(pl.BlockSpec((tm, tk), lambda i, j, k: (i, k)))
        in_specs.append(pl.BlockSpec((tk, tn), lambda i, j, k: (k, j)))
        args += [a, w]
    return pl.pallas_call(
        body,
        name=name,
        grid=(m // tm, n // tn, nk),
        in_specs=in_specs,
        out_specs=pl.BlockSpec((tm, tn), lambda i, j, k: (i, j)),
        out_shape=jax.ShapeDtypeStruct((m, n), out_dtype),
        compiler_params=_cparams(("parallel", "parallel", "arbitrary")),
    )(*args)


def mm_tn(a, d, name):
    r, k = a.shape
    n = d.shape[1]
    tr = _pick(r, MM_ROW_TILES)
    lane_ok = lambda c, full: c % LANE_TILE == 0 or c == full
    tk = _pick(k, tuple(c for c in MM_TILES if lane_ok(c, k) and c * tr * a.dtype.itemsize <= MM_BLOCK_BYTES))
    tn = _pick(n, tuple(c for c in MM_TILES if lane_ok(c, n) and c * tr * d.dtype.itemsize <= MM_BLOCK_BYTES
                        and c * tk * 4 <= MM_BLOCK_BYTES))

    def body(a_ref, d_ref, o_ref):
        acc = lax.dot_general(a_ref[...].astype(BF16), d_ref[...].astype(BF16), (((0,), (0,)), ((), ())),
                              preferred_element_type=F32)
        step = pl.program_id(2)

        @pl.when(step == 0)
        def _():
            o_ref[...] = acc

        @pl.when(step > 0)
        def _():
            o_ref[...] += acc

    return pl.pallas_call(
        body,
        name=name,
        grid=(k // tk, n // tn, r // tr),
        in_specs=[pl.BlockSpec((tr, tk), lambda i, j, s: (s, i)), pl.BlockSpec((tr, tn), lambda i, j, s: (s, j))],
        out_specs=pl.BlockSpec((tk, tn), lambda i, j, s: (i, j)),
        out_shape=jax.ShapeDtypeStruct((k, n), F32),
        compiler_params=_cparams(("parallel", "parallel", "arbitrary")),
    )(a, d)


def _mod_index(b, t):
    return (2 * b + jnp.minimum(t, 1), 0, 0)


def tok_fwd(name, f, toks, mod, params, out_widths, out_dtypes):
    nb, s, _ = toks[0].shape
    nt, nm, npar = len(toks), int(mod is not None), len(params)

    def body(*refs):
        ins, outs = refs[: nt + nm + npar], refs[nt + nm + npar:]
        tv = [r[...].astype(F32) for r in ins[:nt]]
        mv = [ins[nt][k:k + 1, :] for k in range(N_MOD)] if nm else None
        pv = [r[...] for r in ins[nt + nm:]]
        for o, r in zip(outs, f(tv, mv, pv)):
            o[...] = r.astype(o.dtype)

    in_specs = [pl.BlockSpec((None, TOK_BLOCK, t.shape[2]), lambda b, t: (b, t, 0)) for t in toks]
    if nm:
        in_specs.append(pl.BlockSpec((None, N_MOD, mod.shape[2]), _mod_index))
    in_specs += [pl.BlockSpec(p.shape, lambda b, t, nd=p.ndim: (0,) * nd) for p in params]
    return pl.pallas_call(
        body,
        name=name,
        grid=(nb, s // TOK_BLOCK),
        in_specs=in_specs,
        out_specs=[pl.BlockSpec((None, TOK_BLOCK, w), lambda b, t: (b, t, 0)) for w in out_widths],
        out_shape=[jax.ShapeDtypeStruct((nb, s, w), dt) for w, dt in zip(out_widths, out_dtypes)],
        compiler_params=_cparams(("parallel", "parallel")),
    )(*toks, *([mod] if nm else []), *params)


def tok_bwd(name, f, toks, mod, params, cots, dtok_dtypes):
    nb, s, _ = toks[0].shape
    nt, nm, npar, nc = len(toks), int(mod is not None), len(params), len(cots)

    def body(*refs):
        n_in = nt + nm + npar + nc
        ins, outs = refs[:n_in], refs[n_in:]
        b, t = pl.program_id(0), pl.program_id(1)
        tv = [r[...].astype(F32) for r in ins[:nt]]
        mv = [ins[nt][k:k + 1, :] for k in range(N_MOD)] if nm else None
        pv = [r[...] for r in ins[nt + nm: nt + nm + npar]]
        cv = [r[...].astype(F32) for r in ins[nt + nm + npar:]]
        _, vjp = jax.vjp(f, tv, mv, pv)
        dtv, dmv, dpv = vjp(cv)
        for o, r in zip(outs[:nt], dtv):
            o[...] = r.astype(o.dtype)
        if nm:
            dm_ref = outs[nt]

            @pl.when(t <= 1)
            def _():
                for k in range(N_MOD):
                    dm_ref[k:k + 1, :] = dmv[k]

            @pl.when(t > 1)
            def _():
                for k in range(N_MOD):
                    dm_ref[k:k + 1, :] += dmv[k]

        first = jnp.logical_and(b == 0, t == 0)
        for o, r in zip(outs[nt + nm:], dpv):
            @pl.when(first)
            def _(o=o, r=r):
                o[...] = r

            @pl.when(jnp.logical_not(first))
            def _(o=o, r=r):
                o[...] += r

    tok_spec = lambda w: pl.BlockSpec((None, TOK_BLOCK, w), lambda b, t: (b, t, 0))
    in_specs = [tok_spec(t.shape[2]) for t in toks]
    if nm:
        in_specs.append(pl.BlockSpec((None, N_MOD, mod.shape[2]), _mod_index))
    in_specs += [pl.BlockSpec(p.shape, lambda b, t, nd=p.ndim: (0,) * nd) for p in params]
    in_specs += [tok_spec(c.shape[2]) for c in cots]
    out_specs = [tok_spec(t.shape[2]) for t in toks]
    out_shape = [jax.ShapeDtypeStruct(t.shape, dt) for t, dt in zip(toks, dtok_dtypes)]
    if nm:
        out_specs.append(pl.BlockSpec((None, N_MOD, mod.shape[2]), _mod_index))
        out_shape.append(jax.ShapeDtypeStruct(mod.shape, F32))
    out_specs += [pl.BlockSpec(p.shape, lambda b, t, nd=p.ndim: (0,) * nd) for p in params]
    out_shape += [jax.ShapeDtypeStruct(p.shape, F32) for p in params]
    res = pl.pallas_call(
        body,
        name=name,
        grid=(nb, s // TOK_BLOCK),
        in_specs=in_specs,
        out_specs=out_specs,
        out_shape=out_shape,
        compiler_params=_cparams(("arbitrary", "arbitrary")),
    )(*toks, *([mod] if nm else []), *params, *cots)
    return res[:nt], (res[nt] if nm else None), res[nt + nm:]


def _rms(x, g):
    return x * lax.rsqrt(jnp.mean(x * x, axis=-1, keepdims=True) + RMS_EPS) * g


def _silu(x):
    return x * jax.nn.sigmoid(x)


def f_norm(shift_row, scale_row):
    def f(tv, mv, pv):
        return [_rms(tv[0], pv[0]) * (1.0 + mv[scale_row]) + mv[shift_row]]
    return f


def f_resnorm(gate_row, shift_row, scale_row):
    def f(tv, mv, pv):
        s = tv[0] + mv[gate_row] * tv[1]
        return [s, _rms(s, pv[0]) * (1.0 + mv[scale_row]) + mv[shift_row]]
    return f


_ANY = pl.BlockSpec(memory_space=pl.ANY)
_MESH = pl.DeviceIdType.MESH


def _my_pos():
    return lax.axis_index("x"), lax.axis_index("y"), lax.axis_index("c")


def _slot_of(pos):
    return 4 * pos[0] + 2 * pos[1] + pos[2]


def all_gather(xs, name):
    n = len(xs)

    def body(*refs):
        x_refs, out_refs = refs[:n], refs[n:2 * n]
        send_sems, recv_sems, local_sems = refs[2 * n:]
        px, py, pc = _my_pos()
        me, sibling = (px, py, pc), (px, py, 1 - pc)
        chips = [(1 - px, py), (px, 1 - py), (1 - px, 1 - py)]

        def copy(a, k, block, to, from_input=False):
            slot = out_refs[a].at[_slot_of(block)]
            return pltpu.make_async_remote_copy(
                src_ref=x_refs[a] if from_input else slot, dst_ref=slot,
                send_sem=send_sems.at[a, k], recv_sem=recv_sems.at[a, k],
                device_id=to, device_id_type=_MESH)

        mine = [pltpu.make_async_copy(x_refs[a], out_refs[a].at[_slot_of(me)], local_sems.at[a]) for a in range(n)]
        for cp in mine:
            cp.start()
        first = [copy(a, 0, me, sibling, True) for a in range(n)]
        first += [copy(a, 1 + j, me, (*chip, pc), True) for j, chip in enumerate(chips) for a in range(n)]
        for cp in first:
            cp.start()
        passed = []
        for j, chip in enumerate(chips):
            for a in range(n):
                copy(a, 1 + j, (*chip, pc), me).wait_recv()
                passed.append(copy(a, 4 + j, (*chip, pc), sibling))
                passed[-1].start()
        for a in range(n):
            copy(a, 0, sibling, me).wait_recv()
        for j, chip in enumerate(chips):
            for a in range(n):
                copy(a, 4 + j, (*chip, 1 - pc), me).wait_recv()
        for cp in first + passed:
            cp.wait_send()
        for cp in mine:
            cp.wait()

    return pl.pallas_call(
        body,
        name=name,
        out_shape=[jax.ShapeDtypeStruct((N_DEV,) + x.shape, x.dtype) for x in xs],
        in_specs=[_ANY] * n,
        out_specs=[_ANY] * n,
        scratch_shapes=[pltpu.SemaphoreType.DMA((n, 7)), pltpu.SemaphoreType.DMA((n, 7)), pltpu.SemaphoreType.DMA((n,))],
    )(*xs)


def all_to_all(xs, name):
    n = len(xs)

    def body(*refs):
        x_refs, out_refs = refs[:n], refs[n:2 * n]
        send_sems, recv_sems, local_sems = refs[2 * n:]
        px, py, pc = _my_pos()
        me = (px, py, pc)

        def flipped(k):
            kx, ky, kc = (k >> 2) & 1, (k >> 1) & 1, k & 1
            return (1 - px if kx else px, 1 - py if ky else py, 1 - pc if kc else pc)

        def copy(a, k):
            peer = flipped(k)
            return pltpu.make_async_remote_copy(
                src_ref=x_refs[a].at[_slot_of(peer)], dst_ref=out_refs[a].at[_slot_of(me)],
                send_sem=send_sems.at[a, k - 1], recv_sem=recv_sems.at[a, k - 1],
                device_id=peer, device_id_type=_MESH)

        def landing(a, k):
            peer = flipped(k)
            return pltpu.make_async_remote_copy(
                src_ref=x_refs[a].at[_slot_of(me)], dst_ref=out_refs[a].at[_slot_of(peer)],
                send_sem=send_sems.at[a, k - 1], recv_sem=recv_sems.at[a, k - 1],
                device_id=peer, device_id_type=_MESH)

        mine = [pltpu.make_async_copy(x_refs[a].at[_slot_of(me)], out_refs[a].at[_slot_of(me)], local_sems.at[a]) for a in range(n)]
        for cp in mine:
            cp.start()
        copies = [copy(a, k) for a in range(n) for k in range(1, N_DEV)]
        for cp in copies:
            cp.start()
        for a in range(n):
            for k in range(1, N_DEV):
                landing(a, k).wait_recv()
        for cp in copies:
            cp.wait_send()
        for cp in mine:
            cp.wait()

    return pl.pallas_call(
        body,
        name=name,
        out_shape=[jax.ShapeDtypeStruct(x.shape, x.dtype) for x in xs],
        in_specs=[_ANY] * n,
        out_specs=[_ANY] * n,
        scratch_shapes=[pltpu.SemaphoreType.DMA((n, 7)), pltpu.SemaphoreType.DMA((n, 7)), pltpu.SemaphoreType.DMA((n,))],
    )(*xs)


N_CHIPS = 4


def sibling_swap(xs, name):
    n = len(xs)

    def body(*refs):
        x_refs, out_refs = refs[:n], refs[n:2 * n]
        send_sems, recv_sems = refs[2 * n:]
        px, py, pc = _my_pos()
        copies = [pltpu.make_async_remote_copy(
            src_ref=x_refs[a].at[:, 1 - pc], dst_ref=out_refs[a],
            send_sem=send_sems.at[a], recv_sem=recv_sems.at[a],
            device_id=(px, py, 1 - pc), device_id_type=_MESH) for a in range(n)]
        for cp in copies:
            cp.start()
        for cp in copies:
            cp.wait()

    return pl.pallas_call(
        body,
        name=name,
        out_shape=[jax.ShapeDtypeStruct(x.shape[:1] + x.shape[2:], x.dtype) for x in xs],
        in_specs=[_ANY] * n,
        out_specs=[_ANY] * n,
        scratch_shapes=[pltpu.SemaphoreType.DMA((n,)), pltpu.SemaphoreType.DMA((n,))],
    )(*xs)


def chip_exchange(xs, name):
    n = len(xs)

    def body(*refs):
        x_refs, out_refs = refs[:n], refs[n:2 * n]
        send_sems, recv_sems, local_sems = refs[2 * n:]
        px, py, pc = _my_pos()
        my_chip = 2 * px + py

        def peer(k):
            return (1 - px if k & 2 else px, 1 - py if k & 1 else py)

        def copy(a, k, landing):
            qx, qy = peer(k)
            src, dst = (my_chip, 2 * qx + qy) if landing else (2 * qx + qy, my_chip)
            return pltpu.make_async_remote_copy(
                src_ref=x_refs[a].at[src], dst_ref=out_refs[a].at[dst],
                send_sem=send_sems.at[a, k - 1], recv_sem=recv_sems.at[a, k - 1],
                device_id=(qx, qy, pc), device_id_type=_MESH)

        mine = [pltpu.make_async_copy(x_refs[a].at[my_chip], out_refs[a].at[my_chip], local_sems.at[a]) for a in range(n)]
        for cp in mine:
            cp.start()
        copies = [copy(a, k, False) for a in range(n) for k in range(1, N_CHIPS)]
        for cp in copies:
            cp.start()
        for a in range(n):
            for k in range(1, N_CHIPS):
                copy(a, k, True).wait_recv()
        for cp in copies:
            cp.wait_send()
        for cp in mine:
            cp.wait()

    return pl.pallas_call(
        body,
        name=name,
        out_shape=[jax.ShapeDtypeStruct(x.shape, x.dtype) for x in xs],
        in_specs=[_ANY] * n,
        out_specs=[_ANY] * n,
        scratch_shapes=[pltpu.SemaphoreType.DMA((n, N_CHIPS - 1)), pltpu.SemaphoreType.DMA((n, N_CHIPS - 1)),
                        pltpu.SemaphoreType.DMA((n,))],
    )(*xs)


def pair_sum(own, got, name):
    nch, _, r, c = own.shape
    tr = _pick(r, (512, 256, 128, 64, 32, 16))

    def body(own_ref, got_ref, o_ref):
        pc = lax.axis_index("c")
        o_ref[...] = (own_ref[pc].astype(F32) + got_ref[...].astype(F32)).astype(o_ref.dtype)

    return pl.pallas_call(
        body,
        name=name,
        grid=(nch, r // tr),
        in_specs=[pl.BlockSpec((None, 2, tr, c), lambda i, j: (i, 0, j, 0)), pl.BlockSpec((None, tr, c), lambda i, j: (i, j, 0))],
        out_specs=pl.BlockSpec((None, tr, c), lambda i, j: (i, j, 0)),
        out_shape=jax.ShapeDtypeStruct((nch, r, c), own.dtype),
        compiler_params=_cparams(("parallel", "parallel")),
    )(own, got)


def sum_slots(x, name):
    n, r, c = x.shape
    tr = _pick(r, (512, 256, 128, 64, 32, 16, 8))

    def body(x_ref, o_ref):
        acc = x_ref[0].astype(F32)
        for i in range(1, n):
            acc = acc + x_ref[i].astype(F32)
        o_ref[...] = acc

    return pl.pallas_call(
        body,
        name=name,
        grid=(r // tr,),
        in_specs=[pl.BlockSpec((n, tr, c), lambda i: (0, i, 0))],
        out_specs=pl.BlockSpec((tr, c), lambda i: (i, 0)),
        out_shape=jax.ShapeDtypeStruct((r, c), F32),
        compiler_params=_cparams(("parallel",)),
    )(x)


CONV_CH_TILE = 256


def _shift_rows(x, off):
    n = x.shape[0]
    if off % n == 0:
        return x
    return pltpu.roll(x, (-off) % n, axis=0)


def _between(v, lo, hi):
    return jnp.where(v >= lo, 1.0, 0.0) * jnp.where(v < hi, 1.0, 0.0)


def taps_1d(ntaps, s_ctx, s_tot):
    def mask(off):
        def m(t):
            is_ctx = _between(t, 0, s_ctx)
            return is_ctx * _between(t + off, 0, s_ctx) + (1.0 - is_ctx) * _between(t + off, s_ctx, s_tot)
        return m
    return [(j - (ntaps - 1) // 2, mask(j - (ntaps - 1) // 2)) for j in range(ntaps)]


def taps_grid(s_ctx, s_tot, grid_w):
    assert s_ctx % grid_w == 0

    return ("grid", s_ctx, s_tot, grid_w)


def _grid_masks(taps, s):
    _, s_ctx, s_tot, grid_w = taps
    t = lax.broadcasted_iota(jnp.int32, (s, 1), 0)
    is_ctx = _between(t, 0, s_ctx)
    mcol = {dc: is_ctx * _between(t + dc, 0, s_ctx) + (1.0 - is_ctx) * _between(t % grid_w + dc, 0, grid_w) for dc in (-1, 1)}
    mrow = {dr: (1.0 - is_ctx) * _between(t + grid_w * dr, s_ctx, s_tot) for dr in (-1, 1)}
    return mcol, mrow


def _grid_cols(x, mcol):
    return {-1: _shift_rows(x, -1) * mcol[-1], 0: x, 1: _shift_rows(x, 1) * mcol[1]}


def _conv_acc(x, w_ref, b_ref, taps, s):
    acc = jnp.broadcast_to(b_ref[...], x.shape)
    if taps[0] == "grid":
        grid_w = taps[3]
        mcol, mrow = _grid_masks(taps, s)
        xc = _grid_cols(x, mcol)
        for a, dr in enumerate((-1, 0, 1)):
            r = sum(w_ref[3 * a + b:3 * a + b + 1, :] * xc[dc] for b, dc in enumerate((-1, 0, 1)))
            acc = acc + (r if dr == 0 else _shift_rows(r, grid_w * dr) * mrow[dr])
        return acc
    t = lax.broadcasted_iota(jnp.int32, (s, 1), 0)
    for k, (off, m) in enumerate(taps):
        acc = acc + w_ref[k:k + 1, :] * (_shift_rows(x, off) * m(t))
    return acc


def _conv_adjoint(x, dacc, w_ref, taps, s):
    if taps[0] == "grid":
        grid_w = taps[3]
        mcol, mrow = _grid_masks(taps, s)
        xc = _grid_cols(x, mcol)
        dxc = {dc: 0.0 for dc in (-1, 0, 1)}
        dws = []
        for a, dr in enumerate((-1, 0, 1)):
            d_r = dacc if dr == 0 else _shift_rows(dacc * mrow[dr], -grid_w * dr)
            for b, dc in enumerate((-1, 0, 1)):
                dxc[dc] = dxc[dc] + w_ref[3 * a + b:3 * a + b + 1, :] * d_r
                dws.append(jnp.sum(d_r * xc[dc], axis=0, keepdims=True))
        dx = dxc[0] + _shift_rows(dxc[-1] * mcol[-1], 1) + _shift_rows(dxc[1] * mcol[1], -1)
        return dx, dws
    t = lax.broadcasted_iota(jnp.int32, (s, 1), 0)
    dx = jnp.zeros_like(x)
    dws = []
    for k, (off, m) in enumerate(taps):
        dm = dacc * m(t)
        dx = dx + _shift_rows(w_ref[k:k + 1, :] * dm, -off)
        dws.append(jnp.sum(dm * _shift_rows(x, off), axis=0, keepdims=True))
    return dx, dws


def conv_fwd(name, x, w, b, taps, mode, mul=None, out_dtype=F32):
    nb, s, c = x.shape
    ct = _pick(c, (CONV_CH_TILE, 128))
    has_mul = mode == "silu_mul"

    def body(*refs):
        x_ref, w_ref, b_ref = refs[:3]
        o_ref = refs[-1]
        acc = _conv_acc(x_ref[...], w_ref, b_ref, taps, s)
        if mode == "none":
            out = acc
        else:
            out = _silu(acc)
            if has_mul:
                out = out * refs[3][...].astype(F32)
        o_ref[...] = out.astype(o_ref.dtype)

    blk = pl.BlockSpec((None, s, ct), lambda bb, j: (bb, 0, j))
    par = lambda k: pl.BlockSpec((k, ct), lambda bb, j: (0, j))
    return pl.pallas_call(
        body,
        name=name,
        grid=(nb, c // ct),
        in_specs=[blk, par(w.shape[0]), par(1)] + ([blk] if has_mul else []),
        out_specs=blk,
        out_shape=jax.ShapeDtypeStruct(x.shape, out_dtype),
        compiler_params=_cparams(("parallel", "parallel")),
    )(x, w, b, *([mul] if has_mul else []))


def conv_bwd(name, x, w, b, dout, taps, mode, mul=None, dx_dtype=F32):
    nb, s, c = x.shape
    ct = _pick(c, (CONV_CH_TILE, 128))
    has_mul = mode == "silu_mul"
    nk = w.shape[0]

    def body(*refs):
        x_ref, w_ref, b_ref, do_ref = refs[:4]
        n_in = 5 if has_mul else 4
        dx_ref, dw_ref, db_ref = refs[n_in:n_in + 3]
        bb = pl.program_id(1)
        x = x_ref[...]
        dacc = do_ref[...].astype(F32)
        if mode != "none":
            acc = _conv_acc(x, w_ref, b_ref, taps, s)
            sg = jax.nn.sigmoid(acc)
            if has_mul:
                refs[n_in + 3][...] = (dacc * (acc * sg)).astype(refs[n_in + 3].dtype)
                dacc = dacc * refs[4][...].astype(F32)
            dacc = dacc * (sg * (1.0 + acc * (1.0 - sg)))
        dx, dws = _conv_adjoint(x, dacc, w_ref, taps, s)
        dx_ref[...] = dx.astype(dx_ref.dtype)
        db = jnp.sum(dacc, axis=0, keepdims=True)

        @pl.when(bb == 0)
        def _():
            for k in range(nk):
                dw_ref[k:k + 1, :] = dws[k]
            db_ref[...] = db

        @pl.when(bb > 0)
        def _():
            for k in range(nk):
                dw_ref[k:k + 1, :] += dws[k]
            db_ref[...] += db

    blk = pl.BlockSpec((None, s, ct), lambda j, bb: (bb, 0, j))
    par = lambda k: pl.BlockSpec((k, ct), lambda j, bb: (0, j))
    out_specs = [blk, par(nk), par(1)] + ([blk] if has_mul else [])
    out_shape = [jax.ShapeDtypeStruct(x.shape, dx_dtype), jax.ShapeDtypeStruct(w.shape, F32), jax.ShapeDtypeStruct(b.shape, F32)]
    if has_mul:
        out_shape.append(jax.ShapeDtypeStruct(x.shape, dx_dtype))
    return pl.pallas_call(
        body,
        name=name,
        grid=(c // ct, nb),
        in_specs=[blk, par(nk), par(1), blk] + ([blk] if has_mul else []),
        out_specs=out_specs,
        out_shape=out_shape,
        compiler_params=_cparams(("parallel", "arbitrary")),
    )(x, w, b, dout, *([mul] if has_mul else []))


SCAN_UNROLL = 4


def _scan_order(direction, adjoint, s_ctx, s_tot):
    nc, nt = s_ctx // SUBLANES, s_tot // SUBLANES
    if direction == 0:
        return ([(0, nt, 1)], False) if not adjoint else ([(nt - 1, nt, -1)], True)
    if not adjoint:
        return [(nc - 1, nc, -1), (nt - 1, nt - nc, -1)], True
    return [(nc, nt - nc, 1), (0, nc, 1)], False


def _last_row(h, descending):
    row = lax.broadcasted_iota(jnp.int32, h.shape, 0)
    pick = 0 if descending else SUBLANES - 1
    return jnp.sum(jnp.where(row == pick, h, 0.0), axis=0, keepdims=True)


def _prev_rows(h, carry, descending):
    row = lax.broadcasted_iota(jnp.int32, h.shape, 0)
    if descending:
        return jnp.where(row == SUBLANES - 1, carry, pltpu.roll(h, SUBLANES - 1, axis=0))
    return jnp.where(row == 0, carry, pltpu.roll(h, 1, axis=0))


def _scan_real(a_ref, x_ref, h_ref, hp_ref, order):
    ranges, descending = order
    n_rows, width = a_ref.shape
    n_tiles = n_rows // SUBLANES
    row = lax.broadcasted_iota(jnp.int32, (SUBLANES, width), 0)
    unroll = lambda count: SCAN_UNROLL if count % SCAN_UNROLL == 0 else 1

    def run(ac_ref):
        def in_tile(i, _):
            t0 = pl.multiple_of(i * SUBLANES, SUBLANES)
            a = a_ref[pl.ds(t0, SUBLANES), :]
            x = x_ref[pl.ds(t0, SUBLANES), :]
            for k in (1, 2, 4):
                sh = SUBLANES - k if descending else k
                keep = (row < SUBLANES - k) if descending else (row >= k)
                x = jnp.where(keep, a * pltpu.roll(x, sh, axis=0) + x, x)
                a = jnp.where(keep, a * pltpu.roll(a, sh, axis=0), a)
            ac_ref[pl.ds(t0, SUBLANES), :] = a
            x_ref[pl.ds(t0, SUBLANES), :] = x
            return 0

        lax.fori_loop(0, n_tiles, in_tile, 0, unroll=unroll(n_tiles))

        def tile(i, carry):
            t0 = pl.multiple_of(i * SUBLANES, SUBLANES)
            a = ac_ref[pl.ds(t0, SUBLANES), :]
            x = x_ref[pl.ds(t0, SUBLANES), :]
            h = a * carry + x
            if h_ref is not None:
                h_ref[pl.ds(t0, SUBLANES), :] = h
            if hp_ref is not None:
                hp_ref[pl.ds(t0, SUBLANES), :] = _prev_rows(h, carry, descending)
            return _last_row(a, descending) * carry + _last_row(x, descending)

        carry = jnp.zeros((1, width), F32)
        for first, count, step in ranges:
            carry = lax.fori_loop(0, count, lambda j, c, first=first, step=step: tile(first + step * j, c), carry,
                                  unroll=unroll(count))

    pl.run_scoped(run, pltpu.VMEM((n_rows, width), F32))


def _cmul(ar, ai, br, bi):
    return ar * br - ai * bi, ar * bi + ai * br


def _scan_cplx(lr, li, xr_ref, xi_ref, hpr_ref, hpi_ref, order):
    ranges, descending = order
    width = xr_ref.shape[1]
    row = lax.broadcasted_iota(jnp.int32, (SUBLANES, width), 0)
    pw = [(lr, li)]
    for _ in range(SUBLANES - 1):
        pw.append(_cmul(pw[-1][0], pw[-1][1], lr, li))
    pr = jnp.zeros((SUBLANES, width), F32)
    pi = jnp.zeros((SUBLANES, width), F32)
    for r in range(SUBLANES):
        n = SUBLANES - 1 - r if descending else r
        pr = jnp.where(row == r, pw[n][0], pr)
        pi = jnp.where(row == r, pw[n][1], pi)

    n_tiles = xr_ref.shape[0] // SUBLANES
    unroll = lambda count: SCAN_UNROLL if count % SCAN_UNROLL == 0 else 1

    def in_tile(i, _):
        t0 = pl.multiple_of(i * SUBLANES, SUBLANES)
        xr = xr_ref[pl.ds(t0, SUBLANES), :]
        xi = xi_ref[pl.ds(t0, SUBLANES), :]
        for k in (1, 2, 4):
            sh = SUBLANES - k if descending else k
            keep = (row < SUBLANES - k) if descending else (row >= k)
            sr, si = _cmul(pw[k - 1][0], pw[k - 1][1], pltpu.roll(xr, sh, axis=0), pltpu.roll(xi, sh, axis=0))
            xr = jnp.where(keep, xr + sr, xr)
            xi = jnp.where(keep, xi + si, xi)
        xr_ref[pl.ds(t0, SUBLANES), :] = xr
        xi_ref[pl.ds(t0, SUBLANES), :] = xi
        return 0

    lax.fori_loop(0, n_tiles, in_tile, 0, unroll=unroll(n_tiles))
    lam8 = pw[SUBLANES - 1]

    def tile(i, carry):
        cr, ci = carry
        t0 = pl.multiple_of(i * SUBLANES, SUBLANES)
        xr = xr_ref[pl.ds(t0, SUBLANES), :]
        xi = xi_ref[pl.ds(t0, SUBLANES), :]
        hr, hi = _cmul(pr, pi, cr, ci)
        hr, hi = hr + xr, hi + xi
        xr_ref[pl.ds(t0, SUBLANES), :] = hr
        xi_ref[pl.ds(t0, SUBLANES), :] = hi
        if hpr_ref is not None:
            hpr_ref[pl.ds(t0, SUBLANES), :] = _prev_rows(hr, cr, descending)
            hpi_ref[pl.ds(t0, SUBLANES), :] = _prev_rows(hi, ci, descending)
        nr, ni = _cmul(lam8[0], lam8[1], cr, ci)
        return nr + _last_row(xr, descending), ni + _last_row(xi, descending)

    carry = (jnp.zeros((1, width), F32), jnp.zeros((1, width), F32))
    for first, count, step in ranges:
        carry = lax.fori_loop(0, count, lambda j, c, first=first, step=step: tile(first + step * j, c), carry,
                              unroll=unroll(count))


def _log1p_pos(y):
    return jnp.where(y < 0.01, y * (1.0 - y * (0.5 - y * (1.0 / 3.0 - 0.25 * y))), jnp.log(1.0 + y))


def _softplus(x):
    return jnp.maximum(x, 0.0) + _log1p_pos(jnp.exp(-jnp.abs(x)))


def _neg_expm1(z):
    series = -z * (1.0 + z * (0.5 + z * (1.0 / 6.0 + z * (1.0 / 24.0 + z * (1.0 / 120.0)))))
    return jnp.where(z > -0.1, series, 1.0 - jnp.exp(z))


def _lru_gates(u, w_a, b_a, w_i, b_i, lam):
    ub = u.astype(BF16)
    r = jax.nn.sigmoid(jnp.dot(ub, w_a.astype(BF16), preferred_element_type=F32) + b_a)
    i = jax.nn.sigmoid(jnp.dot(ub, w_i.astype(BF16), preferred_element_type=F32) + b_i)
    log_a = (-LRU_C) * _softplus(-lam) * r
    return jnp.exp(log_a), jnp.sqrt(_neg_expm1(2.0 * log_a)) * (i * u)


LRU_PER_STEP = 4
LRU_PER_STEP_BWD = 2


def _lru_specs(per, bw, order):
    w = pl.BlockSpec((2, per, bw, bw), lambda *g: (0, order(*g), 0, 0))
    v = pl.BlockSpec((2, per, 1, bw), lambda *g: (0, order(*g), 0, 0))
    return [w, v, w, v, v]


def lru_fwd(name, u, w_a, b_a, w_i, b_i, lam, s_ctx):
    nb, s, _ = u.shape
    nblk, bw = w_a.shape[1], w_a.shape[2]
    per = min(LRU_PER_STEP, nblk)

    def body(u_ref, wa, ba, wi, bi, lm, o_ref, a_s, x_s, h_s):
        for d in (0, 1):
            for k in range(per):
                cols = slice(k * bw, (k + 1) * bw)
                a, bx = _lru_gates(u_ref[:, cols], wa[d, k], ba[d, k], wi[d, k], bi[d, k], lm[d, k])
                a_s[:, cols] = a
                x_s[:, cols] = bx
            _scan_real(a_s, x_s, h_s, None, _scan_order(d, False, s_ctx, s))
            if d == 0:
                o_ref[...] = h_s[...]
            else:
                o_ref[...] += h_s[...]

    blk = pl.BlockSpec((None, s, per * bw), lambda b, n: (b, 0, n))
    return pl.pallas_call(
        body,
        name=name,
        grid=(nb, nblk // per),
        in_specs=[blk] + _lru_specs(per, bw, lambda b, n: n),
        out_specs=blk,
        out_shape=jax.ShapeDtypeStruct(u.shape, F32),
        scratch_shapes=[pltpu.VMEM((s, per * bw), F32)] * 3,
        compiler_params=_cparams(("parallel", "parallel")),
    )(u, w_a, b_a, w_i, b_i, lam)


def lru_bwd(name, u, w_a, b_a, w_i, b_i, lam, dh, s_ctx):
    nb, s, _ = u.shape
    nblk, bw = w_a.shape[1], w_a.shape[2]
    per = min(LRU_PER_STEP_BWD, nblk)

    def body(u_ref, wa, ba, wi, bi, lm, dh_ref, du_ref, dwa, dba, dwi, dbi, dlm, a_s, x_s, hp_s, wp_s):
        b = pl.program_id(1)
        for d in (0, 1):
            for k in range(per):
                cols = slice(k * bw, (k + 1) * bw)
                a, bx = _lru_gates(u_ref[:, cols], wa[d, k], ba[d, k], wi[d, k], bi[d, k], lm[d, k])
                a_s[:, cols] = a
                x_s[:, cols] = bx
            _scan_real(a_s, x_s, None, hp_s, _scan_order(d, False, s_ctx, s))
            x_s[...] = a_s[...] * dh_ref[...]
            _scan_real(a_s, x_s, None, wp_s, _scan_order(d, True, s_ctx, s))
            for k in range(per):
                cols = slice(k * bw, (k + 1) * bw)
                g = dh_ref[:, cols] + wp_s[:, cols]
                _, vjp = jax.vjp(_lru_gates, u_ref[:, cols], wa[d, k], ba[d, k], wi[d, k], bi[d, k], lm[d, k])
                grads = vjp((g * hp_s[:, cols], g))
                if d == 0:
                    du_ref[:, cols] = grads[0]
                else:
                    du_ref[:, cols] += grads[0]
                for ref, val in zip((dwa, dba, dwi, dbi, dlm), grads[1:]):
                    @pl.when(b == 0)
                    def _(ref=ref, val=val, k=k):
                        ref[d, k] = val

                    @pl.when(b > 0)
                    def _(ref=ref, val=val, k=k):
                        ref[d, k] += val

    blk = pl.BlockSpec((None, s, per * bw), lambda n, b: (b, 0, n))
    pspecs = _lru_specs(per, bw, lambda n, b: n)
    return pl.pallas_call(
        body,
        name=name,
        grid=(nblk // per, nb),
        in_specs=[blk] + pspecs + [blk],
        out_specs=[blk] + pspecs,
        out_shape=[jax.ShapeDtypeStruct(u.shape, F32)] + [jax.ShapeDtypeStruct(p.shape, F32) for p in (w_a, b_a, w_i, b_i, lam)],
        scratch_shapes=[pltpu.VMEM((s, per * bw), F32)] * 4,
        compiler_params=_cparams(("parallel", "arbitrary")),
    )(u, w_a, b_a, w_i, b_i, lam, dh)


S5_TILE_CH = 128
S5_TILE_STATES = S5_TILE_CH // S5_GROUP_CH * S5_STATE


def _dot_nt(a, b):
    return lax.dot_general(a, b, (((1,), (1,)), ((), ())), preferred_element_type=F32)


def _dot_tn(a, b):
    return lax.dot_general(a, b, (((0,), (0,)), ((), ())), preferred_element_type=F32)


def _s5_specs(order):
    lam = pl.BlockSpec((2, 1, S5_TILE_STATES), lambda *g: (0, 0, order(*g)))
    mat = pl.BlockSpec((2, None, S5_TILE_STATES, S5_TILE_CH), lambda *g: (0, order(*g), 0, 0))
    return [lam, lam, mat, mat, mat, mat]


def s5_fwd(name, u, lam_r, lam_i, bt_r, bt_i, ct_r, ct_i, s_ctx):
    nb, s, w = u.shape

    def body(u_ref, lr, li, btr, bti, ctr, cti, o_ref, xr_s, xi_s):
        ub = u_ref[...].astype(BF16)
        for d in (0, 1):
            xr_s[...] = _dot_nt(ub, btr[d].astype(BF16))
            xi_s[...] = _dot_nt(ub, bti[d].astype(BF16))
            _scan_cplx(lr[d], li[d], xr_s, xi_s, None, None, _scan_order(d, False, s_ctx, s))
            y = (jnp.dot(xr_s[...].astype(BF16), ctr[d].astype(BF16), preferred_element_type=F32)
                 - jnp.dot(xi_s[...].astype(BF16), cti[d].astype(BF16), preferred_element_type=F32))
            if d == 0:
                o_ref[...] = y
            else:
                o_ref[...] += y

    blk = pl.BlockSpec((None, s, S5_TILE_CH), lambda b, j: (b, 0, j))
    return pl.pallas_call(
        body,
        name=name,
        grid=(nb, w // S5_TILE_CH),
        in_specs=[blk] + _s5_specs(lambda b, j: j),
        out_specs=blk,
        out_shape=jax.ShapeDtypeStruct(u.shape, F32),
        scratch_shapes=[pltpu.VMEM((s, S5_TILE_STATES), F32)] * 2,
        compiler_params=_cparams(("parallel", "parallel")),
    )(u, lam_r, lam_i, bt_r, bt_i, ct_r, ct_i)


def s5_bwd(name, u, lam_r, lam_i, bt_r, bt_i, ct_r, ct_i, dy, s_ctx):
    nb, s, w = u.shape

    def body(u_ref, lr, li, btr, bti, ctr, cti, dy_ref, du_ref, dlr, dli, dbtr, dbti, dctr, dcti,
             hr_s, hi_s, hpr_s, hpi_s, gr_s, gi_s):
        b = pl.program_id(1)
        ub = u_ref[...].astype(BF16)
        dyb = dy_ref[...].astype(BF16)
        du = jnp.zeros((s, S5_TILE_CH), F32)
        for d in (0, 1):
            hr_s[...] = _dot_nt(ub, btr[d].astype(BF16))
            hi_s[...] = _dot_nt(ub, bti[d].astype(BF16))
            _scan_cplx(lr[d], li[d], hr_s, hi_s, hpr_s, hpi_s, _scan_order(d, False, s_ctx, s))
            d_ctr = _dot_tn(hr_s[...].astype(BF16), dyb)
            d_cti = -_dot_tn(hi_s[...].astype(BF16), dyb)
            gr_s[...] = _dot_nt(dyb, ctr[d].astype(BF16))
            gi_s[...] = -_dot_nt(dyb, cti[d].astype(BF16))
            _scan_cplx(lr[d], -li[d], gr_s, gi_s, None, None, _scan_order(d, True, s_ctx, s))
            gr, gi = gr_s[...], gi_s[...]
            hpr, hpi = hpr_s[...], hpi_s[...]
            d_lr = jnp.sum(gr * hpr + gi * hpi, axis=0, keepdims=True)
            d_li = jnp.sum(gi * hpr - gr * hpi, axis=0, keepdims=True)
            grb, gib = gr.astype(BF16), gi.astype(BF16)
            du = du + jnp.dot(grb, btr[d].astype(BF16), preferred_element_type=F32)
            du = du + jnp.dot(gib, bti[d].astype(BF16), preferred_element_type=F32)
            d_btr = _dot_tn(grb, ub)
            d_bti = _dot_tn(gib, ub)
            for ref, val in zip((dlr, dli, dbtr, dbti, dctr, dcti), (d_lr, d_li, d_btr, d_bti, d_ctr, d_cti)):
                @pl.when(b == 0)
                def _(ref=ref, val=val):
                    ref[d] = val

                @pl.when(b > 0)
                def _(ref=ref, val=val):
                    ref[d] += val
        du_ref[...] = du

    blk = pl.BlockSpec((None, s, S5_TILE_CH), lambda j, b: (b, 0, j))
    pspecs = _s5_specs(lambda j, b: j)
    params = (lam_r, lam_i, bt_r, bt_i, ct_r, ct_i)
    return pl.pallas_call(
        body,
        name=name,
        grid=(w // S5_TILE_CH, nb),
        in_specs=[blk] + pspecs + [blk],
        out_specs=[blk] + pspecs,
        out_shape=[jax.ShapeDtypeStruct(u.shape, F32)] + [jax.ShapeDtypeStruct(p.shape, F32) for p in params],
        scratch_shapes=[pltpu.VMEM((s, S5_TILE_STATES), F32)] * 6,
        compiler_params=_cparams(("parallel", "arbitrary")),
    )(u, lam_r, lam_i, bt_r, bt_i, ct_r, ct_i, dy)


def small_fwd(name, f, ins, out_shapes):
    n = len(ins)

    def body(*refs):
        for o, r in zip(refs[n:], f([r[...] for r in refs[:n]])):
            o[...] = r

    return pl.pallas_call(
        body, name=name,
        out_shape=[jax.ShapeDtypeStruct(s, F32) for s in out_shapes],
        compiler_params=_cparams(),
    )(*ins)


def small_bwd(name, f, ins, cots):
    n, nc = len(ins), len(cots)

    def body(*refs):
        _, vjp = jax.vjp(f, [r[...] for r in refs[:n]])
        (grads,) = vjp([r[...] for r in refs[n:n + nc]])
        for o, r in zip(refs[n + nc:], grads):
            o[...] = r

    return pl.pallas_call(
        body, name=name,
        out_shape=[jax.ShapeDtypeStruct(a.shape, F32) for a in ins],
        compiler_params=_cparams(),
    )(*ins, *cots)


def _row(x, r):
    return jnp.sum(jnp.where(lax.broadcasted_iota(jnp.int32, x.shape, 0) == r, x, 0.0), axis=0, keepdims=True)


def _col(x, c):
    return jnp.sum(jnp.where(lax.broadcasted_iota(jnp.int32, x.shape, 1) == c, x, 0.0), axis=1, keepdims=True)


def _chunk_at(i, reverse, ncc, nc):
    if not reverse:
        return i
    return jnp.where(i < ncc, ncc - 1 - i, nc - 1 - (i - ncc))


def _tri(n, reverse):
    li = lax.broadcasted_iota(jnp.int32, (n, n), 0)
    si = lax.broadcasted_iota(jnp.int32, (n, n), 1)
    return jnp.where((li <= si) if reverse else (li >= si), 1.0, 0.0)


_HI = lax.Precision.HIGHEST


def _ssd_chunk(xs, bm, cm, dtc, dtr, a_row, a_col, hs, reverse):
    n = bm.shape[0]
    last = 0 if reverse else n - 1
    tri = _tri(n, reverse)
    cum_c = jnp.dot(tri, dtc * -jnp.exp(a_row), precision=_HI, preferred_element_type=F32)
    cum_r = lax.dot_general(dtr * -jnp.exp(a_col), tri, (((1,), (1,)), ((), ())), precision=_HI, preferred_element_type=F32)
    tot_r = _row(cum_c, last)
    bmb, cmb = bm.astype(BF16), cm.astype(BF16)
    cb = _dot_nt(cmb, bmb)
    ys, hn = [], []
    for hd in range(len(xs)):
        cl = _col(cum_c, hd)
        tot = _col(tot_r, hd)
        decay = jnp.exp(jnp.where(tri > 0.0, cl - _row(cum_r, hd), -jnp.inf))
        xd = xs[hd] * _col(dtc, hd)
        y = jnp.dot((cb * decay).astype(BF16), xd.astype(BF16), preferred_element_type=F32)
        y = y + _dot_nt(cmb, hs[hd].astype(BF16)) * jnp.exp(cl)
        hnew = hs[hd] * jnp.exp(tot) + _dot_tn((xd * jnp.exp(tot - cl)).astype(BF16), bmb)
        ys.append(y)
        hn.append(hnew)
    return ys, hn


def _ssd_specs(reverse, ncc, nc, order):
    ch = lambda *g: _chunk_at(order(*g)[1], reverse, ncc, nc)
    b_ = lambda *g: order(*g)[0]
    gn = SSD_GROUPS * SSD_STATE
    return [
        pl.BlockSpec((None, SSD_CHUNK, SSD_W), lambda *g: (b_(*g), ch(*g), 0)),
        pl.BlockSpec((None, SSD_CHUNK, gn), lambda *g: (b_(*g), ch(*g), SSD_W // gn)),
        pl.BlockSpec((None, SSD_CHUNK, gn), lambda *g: (b_(*g), ch(*g), SSD_W // gn + 1)),
        pl.BlockSpec((None, SSD_GROUPS, SSD_CHUNK, SSD_HPG), lambda *g: (b_(*g), 0, ch(*g), 0)),
        pl.BlockSpec((None, SSD_GROUPS, SSD_HPG, SSD_CHUNK), lambda *g: (b_(*g), 0, 0, ch(*g))),
        pl.BlockSpec((SSD_GROUPS, 1, SSD_HPG), lambda *g: (0, 0, 0)),
        pl.BlockSpec((SSD_GROUPS, SSD_HPG, 1), lambda *g: (0, 0, 0)),
    ]


def _ssd_group_inputs(g, x_ref, bm_ref, cm_ref, dtc_ref, dtr_ref, ar_ref, ac_ref):
    p, n = SSD_HEAD_DIM, SSD_STATE
    xs = [x_ref[:, p * (SSD_HPG * g + hd):p * (SSD_HPG * g + hd + 1)] for hd in range(SSD_HPG)]
    return xs, bm_ref[:, n * g:n * (g + 1)], cm_ref[:, n * g:n * (g + 1)], dtc_ref[g], dtr_ref[g], ar_ref[g], ac_ref[g]


def ssd_fwd(name, xbc, dt_col, dt_row, a_row, a_col, reverse, s_ctx):
    nb, s, _ = xbc.shape
    nc, ncc = s // SSD_CHUNK, s_ctx // SSD_CHUNK
    p = SSD_HEAD_DIM

    def body(x_ref, bm_ref, cm_ref, dtc_ref, dtr_ref, ar_ref, ac_ref, y_ref, hst_ref, h_s):
        i = pl.program_id(1)

        @pl.when(i == 0)
        def _():
            h_s[...] = jnp.zeros_like(h_s)

        hst_ref[...] = h_s[...]
        for g in range(SSD_GROUPS):
            xs, bm, cm, dtc, dtr, ar, ac = _ssd_group_inputs(g, x_ref, bm_ref, cm_ref, dtc_ref, dtr_ref, ar_ref, ac_ref)
            ys, hn = _ssd_chunk(xs, bm, cm, dtc, dtr, ar, ac, [h_s[g, hd] for hd in range(SSD_HPG)], reverse)
            for hd in range(SSD_HPG):
                y_ref[:, p * (SSD_HPG * g + hd):p * (SSD_HPG * g + hd + 1)] = ys[hd]
                h_s[g, hd] = hn[hd]

    state = (SSD_GROUPS, SSD_HPG, SSD_HEAD_DIM, SSD_STATE)
    return pl.pallas_call(
        body,
        name=name,
        grid=(nb, nc),
        in_specs=_ssd_specs(reverse, ncc, nc, lambda b, i: (b, i)),
        out_specs=[pl.BlockSpec((None, SSD_CHUNK, SSD_W), lambda b, i: (b, _chunk_at(i, reverse, ncc, nc), 0)),
                   pl.BlockSpec((None, None) + state, lambda b, i: (b, i, 0, 0, 0, 0))],
        out_shape=[jax.ShapeDtypeStruct((nb, s, SSD_W), F32), jax.ShapeDtypeStruct((nb, nc) + state, F32)],
        scratch_shapes=[pltpu.VMEM(state, F32)],
        compiler_params=_cparams(("parallel", "arbitrary")),
    )(xbc, xbc, xbc, dt_col, dt_row, a_row, a_col)


def ssd_bwd(name, xbc, dt_col, dt_row, a_row, a_col, hst, dy, reverse, s_ctx):
    nb, s, _ = xbc.shape
    nc, ncc = s // SSD_CHUNK, s_ctx // SSD_CHUNK
    p, n = SSD_HEAD_DIM, SSD_STATE

    def body(x_ref, bm_ref, cm_ref, dtc_ref, dtr_ref, ar_ref, ac_ref, hst_ref, dy_ref,
             dx_ref, dbm_ref, dcm_ref, ddtc_ref, ddtr_ref, dar_ref, dac_ref, dh_s):
        i = pl.program_id(1)

        @pl.when(i == 0)
        def _():
            dh_s[...] = jnp.zeros_like(dh_s)

        for g in range(SSD_GROUPS):
            xs, bm, cm, dtc, dtr, ar, ac = _ssd_group_inputs(g, x_ref, bm_ref, cm_ref, dtc_ref, dtr_ref, ar_ref, ac_ref)
            hs = [hst_ref[g, hd] for hd in range(SSD_HPG)]
            _, vjp = jax.vjp(functools.partial(_ssd_chunk, reverse=reverse), xs, bm, cm, dtc, dtr, ar, ac, hs)
            dys = [dy_ref[:, p * (SSD_HPG * g + hd):p * (SSD_HPG * g + hd + 1)] for hd in range(SSD_HPG)]
            dxs, dbm, dcm, ddtc, ddtr, dar, dac, dhs = vjp((dys, [dh_s[g, hd] for hd in range(SSD_HPG)]))
            for hd in range(SSD_HPG):
                dx_ref[:, p * (SSD_HPG * g + hd):p * (SSD_HPG * g + hd + 1)] = dxs[hd]
                dh_s[g, hd] = dhs[hd]
            dbm_ref[:, n * g:n * (g + 1)] = dbm
            dcm_ref[:, n * g:n * (g + 1)] = dcm
            ddtc_ref[g] = ddtc
            ddtr_ref[g] = ddtr

            @pl.when(i == 0)
            def _(g=g, dar=dar, dac=dac):
                dar_ref[g] = dar
                dac_ref[g] = dac

            @pl.when(i > 0)
            def _(g=g, dar=dar, dac=dac):
                dar_ref[g] += dar
                dac_ref[g] += dac

    ch = lambda b, i: _chunk_at(nc - 1 - i, reverse, ncc, nc)
    state = (SSD_GROUPS, SSD_HPG, SSD_HEAD_DIM, SSD_STATE)
    gn = SSD_GROUPS * SSD_STATE
    in_specs = _ssd_specs(reverse, ncc, nc, lambda b, i: (b, nc - 1 - i)) + [
        pl.BlockSpec((None, None) + state, lambda b, i: (b, nc - 1 - i, 0, 0, 0, 0)),
        pl.BlockSpec((None, SSD_CHUNK, SSD_W), lambda b, i: (b, ch(b, i), 0))]
    out_specs = [
        pl.BlockSpec((None, SSD_CHUNK, SSD_W), lambda b, i: (b, ch(b, i), 0)),
        pl.BlockSpec((None, SSD_CHUNK, gn), lambda b, i: (b, ch(b, i), 0)),
        pl.BlockSpec((None, SSD_CHUNK, gn), lambda b, i: (b, ch(b, i), 0)),
        pl.BlockSpec((None, SSD_GROUPS, SSD_CHUNK, SSD_HPG), lambda b, i: (b, 0, ch(b, i), 0)),
        pl.BlockSpec((None, SSD_GROUPS, SSD_HPG, SSD_CHUNK), lambda b, i: (b, 0, 0, ch(b, i))),
        pl.BlockSpec((None, SSD_GROUPS, 1, SSD_HPG), lambda b, i: (b, 0, 0, 0)),
        pl.BlockSpec((None, SSD_GROUPS, SSD_HPG, 1), lambda b, i: (b, 0, 0, 0)),
    ]
    out_shape = [
        jax.ShapeDtypeStruct((nb, s, SSD_W), F32),
        jax.ShapeDtypeStruct((nb, s, gn), F32),
        jax.ShapeDtypeStruct((nb, s, gn), F32),
        jax.ShapeDtypeStruct(dt_col.shape, F32),
        jax.ShapeDtypeStruct(dt_row.shape, F32),
        jax.ShapeDtypeStruct((nb, SSD_GROUPS, 1, SSD_HPG), F32),
        jax.ShapeDtypeStruct((nb, SSD_GROUPS, SSD_HPG, 1), F32),
    ]
    return pl.pallas_call(
        body,
        name=name,
        grid=(nb, nc),
        in_specs=in_specs,
        out_specs=out_specs,
        out_shape=out_shape,
        scratch_shapes=[pltpu.VMEM(state, F32)],
        compiler_params=_cparams(("parallel", "arbitrary")),
    )(xbc, xbc, xbc, dt_col, dt_row, a_row, a_col, hst, dy)


HG_TILES = HG_CHUNK // SUBLANES
HG_HEADS_PER_STEP = 2


def _hg_cum_tiles(x_t, reverse):
    row = lax.broadcasted_iota(jnp.int32, x_t[0].shape, 0)
    out = [None] * len(x_t)
    off = None
    for i in (reversed(range(len(x_t))) if reverse else range(len(x_t))):
        c = x_t[i]
        for k in (1, 2, 4):
            keep = (row < SUBLANES - k) if reverse else (row >= k)
            c = jnp.where(keep, c + pltpu.roll(c, SUBLANES - k if reverse else k, axis=0), c)
        out[i] = c if off is None else c + off
        off = _last_row(out[i], reverse)
    return out, off


def _hg_pairs(reverse):
    row = lax.broadcasted_iota(jnp.int32, (SUBLANES, HG_DK), 0)
    rots = []
    for r in range(SUBLANES):
        rots.append(((SUBLANES - r) % SUBLANES, row <= SUBLANES - 1 - r) if reverse else (r, row >= r))
    return [(j, [i for i in range(HG_TILES) if (i <= j if reverse else i >= j)], rots) for j in range(HG_TILES)]


def _rot(x, sh):
    return pltpu.roll(x, sh, axis=0) if sh else x


def _cat(tiles):
    return jnp.concatenate(tiles, axis=0)


def _hg_chunk_fwd(q_t, k_t, lf_t, v_t, st, reverse):
    cum_t, tot = _hg_cum_tiles(lf_t, reverse)
    y_t = [jnp.zeros(v_t[0].shape, F32) for _ in v_t]
    for j, l_tiles, rots in _hg_pairs(reverse):
        for sh, diag_ok in rots:
            k_j, c_j, v_j = _rot(k_t[j], sh), _rot(cum_t[j], sh), _rot(v_t[j], sh)
            for i in l_tiles:
                e = jnp.exp(cum_t[i] - c_j)
                if i == j:
                    e = jnp.where(diag_ok, e, 0.0)
                att = jnp.sum(q_t[i] * (k_j * e), axis=1, keepdims=True)
                y_t[i] = y_t[i] + att * v_j
    q, k, v, cum = _cat(q_t), _cat(k_t), _cat(v_t), _cat(cum_t)
    y_state = _dot_nt((q * jnp.exp(cum)).astype(BF16), st.astype(BF16))
    st_new = st * jnp.exp(tot) + _dot_tn(v.astype(BF16), (k * jnp.exp(tot - cum)).astype(BF16))
    return [y + y_state[SUBLANES * i:SUBLANES * (i + 1)] for i, y in enumerate(y_t)], st_new


def _hg_chunk_bwd(q_t, k_t, lf_t, v_t, st, dy_t, dst_new, reverse):
    nt = len(q_t)
    cum_t, tot = _hg_cum_tiles(lf_t, reverse)
    q, k, v, cum, dy = _cat(q_t), _cat(k_t), _cat(v_t), _cat(cum_t), _cat(dy_t)
    e_cum, e_tot, e_end = jnp.exp(cum), jnp.exp(tot), jnp.exp(tot - cum)
    qt, khat = q * e_cum, k * e_end
    dyb, dsb = dy.astype(BF16), dst_new.astype(BF16)
    dqt = jnp.dot(dyb, st.astype(BF16), preferred_element_type=F32)
    dst = dst_new * e_tot + _dot_tn(dyb, qt.astype(BF16))
    dv = _dot_nt(khat.astype(BF16), dsb)
    dkhat = jnp.dot(v.astype(BF16), dsb, preferred_element_type=F32)
    t1 = dkhat * khat
    dtot = jnp.sum(dst_new * st, axis=0, keepdims=True) * e_tot + jnp.sum(t1, axis=0, keepdims=True)
    rows = lax.broadcasted_iota(jnp.int32, cum.shape, 0)
    last = 0 if reverse else cum.shape[0] - 1
    dcum = dqt * qt - t1 + jnp.where(rows == last, dtot, 0.0)
    tiles = lambda a: [a[SUBLANES * i:SUBLANES * (i + 1)] for i in range(nt)]
    dq_t, dk_t, dv_t, dcum_t = tiles(dqt * e_cum), tiles(dkhat * e_end), tiles(dv), tiles(dcum)
    for j, l_tiles, rots in _hg_pairs(reverse):
        for sh, diag_ok in rots:
            k_j, c_j, v_j = _rot(k_t[j], sh), _rot(cum_t[j], sh), _rot(v_t[j], sh)
            acc_v = acc_k = acc_c = None
            for i in l_tiles:
                e = jnp.exp(cum_t[i] - c_j)
                if i == j:
                    e = jnp.where(diag_ok, e, 0.0)
                ke, qe = k_j * e, q_t[i] * e
                p = q_t[i] * ke
                att = jnp.sum(p, axis=1, keepdims=True)
                datt = jnp.sum(dy_t[i] * v_j, axis=1, keepdims=True)
                g = datt * p
                dq_t[i] = dq_t[i] + datt * ke
                dcum_t[i] = dcum_t[i] + g
                av, ak = att * dy_t[i], datt * qe
                acc_v, acc_k, acc_c = (av, ak, g) if acc_v is None else (acc_v + av, acc_k + ak, acc_c + g)
            back = (SUBLANES - sh) % SUBLANES
            dv_t[j] = dv_t[j] + _rot(acc_v, back)
            dk_t[j] = dk_t[j] + _rot(acc_k, back)
            dcum_t[j] = dcum_t[j] - _rot(acc_c, back)
    dlf_t, _ = _hg_cum_tiles(dcum_t, not reverse)
    return dq_t, dk_t, dlf_t, dv_t, dst


def _hg_super(s_ctx):
    return min(256, s_ctx)


def hg_fwd(name, q, k, lf, v, reverse, s_ctx):
    nb, s, w = q.shape
    nh, dk, sup = w // HG_DK, HG_DK, _hg_super(s_ctx)
    nsup, nsc, cps = s // sup, s_ctx // sup, sup // HG_CHUNK
    hps = HG_HEADS_PER_STEP if nh % HG_HEADS_PER_STEP == 0 else 1

    def body(q_ref, k_ref, lf_ref, v_ref, y_ref, hst_ref, st_s):
        i = pl.program_id(2)

        @pl.when(i == 0)
        def _():
            st_s[...] = jnp.zeros_like(st_s)

        def step(c, sts):
            r0 = pl.multiple_of((cps - 1 - c if reverse else c) * HG_CHUNK, HG_CHUNK)
            new = []
            for hh in range(hps):
                cols = slice(hh * dk, (hh + 1) * dk)
                tile = lambda ref: [ref[pl.ds(r0 + SUBLANES * i, SUBLANES), cols] for i in range(HG_TILES)]
                hst_ref[hh, c] = sts[hh]
                y_t, st_new = _hg_chunk_fwd(tile(q_ref), tile(k_ref), tile(lf_ref), tile(v_ref), sts[hh], reverse)
                for i in range(HG_TILES):
                    y_ref[pl.ds(r0 + SUBLANES * i, SUBLANES), cols] = y_t[i]
                new.append(st_new)
            return tuple(new)

        out = lax.fori_loop(0, cps, step, tuple(st_s[hh] for hh in range(hps)), unroll=2 if cps % 2 == 0 else 1)
        for hh in range(hps):
            st_s[hh] = out[hh]

    blk = pl.BlockSpec((None, sup, hps * dk), lambda b, h, i: (b, _chunk_at(i, reverse, nsc, nsup), h))
    return pl.pallas_call(
        body,
        name=name,
        grid=(nb, nh // hps, nsup),
        in_specs=[blk] * 4,
        out_specs=[blk, pl.BlockSpec((None, hps, cps, dk, dk), lambda b, h, i: (b, h, i, 0, 0))],
        out_shape=[jax.ShapeDtypeStruct(q.shape, F32), jax.ShapeDtypeStruct((nb, nh, s // HG_CHUNK, dk, dk), F32)],
        scratch_shapes=[pltpu.VMEM((hps, dk, dk), F32)],
        compiler_params=_cparams(("parallel", "parallel", "arbitrary")),
    )(q, k, lf, v)


def hg_bwd(name, q, k, lf, v, hst, dy, reverse, s_ctx):
    nb, s, w = q.shape
    nh, dk, sup = w // HG_DK, HG_DK, _hg_super(s_ctx)
    nsup, nsc, cps = s // sup, s_ctx // sup, sup // HG_CHUNK
    hps = HG_HEADS_PER_STEP if nh % HG_HEADS_PER_STEP == 0 else 1

    def body(q_ref, k_ref, lf_ref, v_ref, hst_ref, dy_ref, dq_ref, dk_ref, dlf_ref, dv_ref, dst_s):
        i = pl.program_id(2)

        @pl.when(i == 0)
        def _():
            dst_s[...] = jnp.zeros_like(dst_s)

        def step(cc, dsts):
            c = cps - 1 - cc
            r0 = pl.multiple_of((cps - 1 - c if reverse else c) * HG_CHUNK, HG_CHUNK)
            new = []
            for hh in range(hps):
                cols = slice(hh * dk, (hh + 1) * dk)
                tile = lambda ref: [ref[pl.ds(r0 + SUBLANES * i, SUBLANES), cols] for i in range(HG_TILES)]
                dq_t, dk_t, dlf_t, dv_t, dst_prev = _hg_chunk_bwd(
                    tile(q_ref), tile(k_ref), tile(lf_ref), tile(v_ref), hst_ref[hh, c], tile(dy_ref), dsts[hh], reverse)
                for ref, val in zip((dq_ref, dk_ref, dlf_ref, dv_ref), (dq_t, dk_t, dlf_t, dv_t)):
                    for i in range(HG_TILES):
                        ref[pl.ds(r0 + SUBLANES * i, SUBLANES), cols] = val[i]
                new.append(dst_prev)
            return tuple(new)

        out = lax.fori_loop(0, cps, step, tuple(dst_s[hh] for hh in range(hps)), unroll=2 if cps % 2 == 0 else 1)
        for hh in range(hps):
            dst_s[hh] = out[hh]

    blk = pl.BlockSpec((None, sup, hps * dk), lambda b, h, i: (b, _chunk_at(nsup - 1 - i, reverse, nsc, nsup), h))
    return pl.pallas_call(
        body,
        name=name,
        grid=(nb, nh // hps, nsup),
        in_specs=[blk] * 4 + [pl.BlockSpec((None, hps, cps, dk, dk), lambda b, h, i: (b, h, nsup - 1 - i, 0, 0)), blk],
        out_specs=[blk] * 4,
        out_shape=[jax.ShapeDtypeStruct(q.shape, F32)] * 4,
        scratch_shapes=[pltpu.VMEM((hps, dk, dk), F32)],
        compiler_params=_cparams(("parallel", "parallel", "arbitrary")),
    )(q, k, lf, v, hst, dy)


def f_s5_discretize(ins):
    lam_re, lam_im, log_step, b_re, b_im = ins
    step = jnp.exp(log_step)
    mag = jnp.exp(lam_re * step)
    ar, ai = mag * jnp.cos(lam_im * step), mag * jnp.sin(lam_im * step)
    den = lam_re * lam_re + lam_im * lam_im
    zr = ((ar - 1.0) * lam_re + ai * lam_im) / den
    zi = (ai * lam_re - (ar - 1.0) * lam_im) / den
    return [ar, ai, zr * b_re - zi * b_im, zr * b_im + zi * b_re]


def s5_tiles_of(m):
    g, p, k = m.shape
    gt = S5_TILE_CH // k
    eye = jnp.eye(gt, dtype=m.dtype)
    t = m.reshape(g // gt, gt, p, 1, k) * eye[None, :, None, :, None]
    return t.reshape(g // gt, gt * p, gt * k)


def s5_groups_of(t, g, p, k):
    gt = S5_TILE_CH // k
    eye = jnp.eye(gt, dtype=t.dtype)
    return jnp.sum(t.reshape(g // gt, gt, p, gt, k) * eye[None, :, None, :, None], axis=3).reshape(g, p, k)


def f_lower_bounds(ins):
    (logits,) = ins
    e = jnp.exp(logits - jnp.max(logits, axis=0, keepdims=True))
    p = e / jnp.sum(e, axis=0, keepdims=True)
    n = logits.shape[0]
    li = lax.broadcasted_iota(jnp.int32, (n, n), 0)
    si = lax.broadcasted_iota(jnp.int32, (n, n), 1)
    after_first = jnp.where(jnp.logical_and(si >= 1, si <= li), 1.0, 0.0)
    return [jnp.dot(after_first, p, precision=_HI, preferred_element_type=F32)]


def f_silu(ins):
    return [_silu(ins[0])]


def f_norm_keep(shift_row, scale_row):
    def f(tv, mv, pv):
        return [tv[0], _rms(tv[0], pv[0]) * (1.0 + mv[scale_row]) + mv[shift_row]]
    return f


def f_dt(tv, mv, pv):
    return [_softplus(tv[0] + pv[0])]


def f_even_finish(tv, mv, pv):
    y_f, y_b, xs, z, h_sum, gy = tv
    d_exp, g = pv
    y = _rms((y_f + y_b + d_exp * xs) * _silu(z), g)
    return [y, h_sum * jax.nn.gelu(gy)]


def f_odd_prep(tv, mv, pv):
    q, f_f, f_b = tv
    (lb,) = pv
    outs = [_silu(q)]
    for f in (f_f, f_b):
        outs.append((1.0 - lb) * jax.nn.sigmoid(-f))
        outs.append(jnp.log(lb + (1.0 - lb) * jax.nn.sigmoid(f)))
    return outs


def f_odd_finish(tv, mv, pv):
    o_f, o_b, g, s5y, u = tv
    norm_g, s5_d, glu_w, glu_b = pv
    o = o_f + o_b
    w = o.shape[1]
    hi = lax.broadcasted_iota(jnp.int32, (w, w), 0) // HG_DK
    hj = lax.broadcasted_iota(jnp.int32, (w, w), 1) // HG_DK
    head_mean = jnp.where(hi == hj, 1.0 / HG_DK, 0.0)
    ms = jnp.dot(o * o, head_mean, precision=_HI, preferred_element_type=F32)
    on = o * lax.rsqrt(ms + RMS_EPS) * norm_g * _silu(g)
    y = jax.nn.gelu(s5y + s5_d * u)
    gate = jax.nn.sigmoid(jnp.dot(y.astype(BF16), glu_w.astype(BF16), preferred_element_type=F32) + glu_b)
    return [on, y * gate]


def final_loss(name, s, br, mod, g, target, s_ctx):
    nb, st, d = s.shape
    tb = TOK_BLOCK
    assert s_ctx == tb

    def lossf(sv, bv, gate, gv, tv):
        y = _rms(sv + gate * bv, gv)
        err = jnp.square(y - tv)
        return 0.5 * jnp.sum(jnp.mean(err, axis=-1, keepdims=True), axis=0, keepdims=True)

    def body(s_ref, b_ref, m_ref, g_ref, t_ref, l_ref, ds_ref, db_ref, dm_ref, dg_ref):
        b, t = pl.program_id(0), pl.program_id(1)

        @pl.when(t == 0)
        def _():
            ds_ref[...] = jnp.zeros_like(ds_ref)
            db_ref[...] = jnp.zeros_like(db_ref)
            dm_ref[...] = jnp.zeros_like(dm_ref)
            l_ref[...] = jnp.zeros_like(l_ref)

        @pl.when(jnp.logical_and(b == 0, t == 0))
        def _():
            dg_ref[...] = jnp.zeros_like(dg_ref)

        @pl.when(t > 0)
        def _():
            gate = m_ref[N_MOD - 1:N_MOD, :]
            l, vjp = jax.vjp(lossf, s_ref[...], b_ref[...], gate, g_ref[...], t_ref[...])
            ds, db, dgate, dg, _ = vjp(jnp.ones((1, 1), F32))
            ds_ref[...] = ds
            db_ref[...] = db.astype(db_ref.dtype)
            dg_ref[...] += dg
            l_ref[...] += jnp.broadcast_to(l, l_ref.shape)

            @pl.when(t == 1)
            def _():
                dm_ref[...] = jnp.zeros_like(dm_ref)
                dm_ref[N_MOD - 1:N_MOD, :] = dgate

            @pl.when(t > 1)
            def _():
                dm_ref[N_MOD - 1:N_MOD, :] += dgate

    tok = pl.BlockSpec((None, tb, d), lambda b, t: (b, t, 0))
    modspec = pl.BlockSpec((None, N_MOD, d), _mod_index)
    gspec = pl.BlockSpec((1, d), lambda b, t: (0, 0))
    return pl.pallas_call(
        body,
        name=name,
        grid=(nb, st // tb),
        in_specs=[tok, tok, modspec, gspec, pl.BlockSpec((None, tb, d), lambda b, t: (b, jnp.maximum(t - 1, 0), 0))],
        out_specs=[pl.BlockSpec((None, SUBLANES, 128), lambda b, t: (b, 0, 0)), tok, tok, modspec, gspec],
        out_shape=[jax.ShapeDtypeStruct((nb, SUBLANES, 128), F32), jax.ShapeDtypeStruct(s.shape, F32),
                   jax.ShapeDtypeStruct(s.shape, BF16), jax.ShapeDtypeStruct(mod.shape, F32), jax.ShapeDtypeStruct(g.shape, F32)],
        compiler_params=_cparams(("arbitrary", "arbitrary")),
    )(s, br, mod, g, target)


def adamw(name, w, g, m, v):
    shape = w.shape
    cols = shape[-1] if w.ndim >= 2 else w.size
    rows = w.size // cols
    tr = _pick(rows, (512, 256, 128, 64, 32, 16, 8))

    def body(w_ref, g_ref, m_ref, v_ref, d_ref, nm_ref, nv_ref):
        gv = g_ref[...]
        nm = ADAM_B1 * m_ref[...] + (1.0 - ADAM_B1) * gv
        nv = ADAM_B2 * v_ref[...] + (1.0 - ADAM_B2) * jnp.square(gv)
        m_hat = nm / (1.0 - ADAM_B1 ** ADAM_STEP)
        v_hat = nv / (1.0 - ADAM_B2 ** ADAM_STEP)
        d_ref[...] = -ADAM_LR * (m_hat / (jnp.sqrt(v_hat) + ADAM_EPS) + ADAM_WD * w_ref[...])
        nm_ref[...] = nm
        nv_ref[...] = nv

    spec = pl.BlockSpec((tr, cols), lambda i: (i, 0))
    outs = pl.pallas_call(
        body,
        name=name,
        grid=(rows // tr,),
        in_specs=[spec] * 4,
        out_specs=[spec] * 3,
        out_shape=[jax.ShapeDtypeStruct((rows, cols), F32)] * 3,
        compiler_params=_cparams(("parallel",)),
    )(*(a.reshape(rows, cols) for a in (w, g, m, v)))
    return tuple(o.reshape(shape) for o in outs)


EV_COLS = {"z": (0, 1024), "xbc": (1024, 2560), "dt": (2560, 2592), "gy": (2592, 3616), "u": (3616, 4640)}
OD_COLS = {"q": (0, 768), "ff": (768, 1536), "fb": (1536, 2304), "v": (2304, 3072), "g": (3072, 3840), "u": (3840, 4096)}
EV_OUT_ROWS = ((0, 1024), (1024, 2048))
OD_OUT_ROWS = ((0, 768), (768, 1024))
LANES = 128


def _pad_to_lanes(w):
    n = w.shape[1]
    return w if n % LANES == 0 else jnp.pad(w, ((0, 0), (0, LANES - n % LANES)))


def _layer_weights(l, big):
    j = l // 2
    even = l % 2 == 0
    w_in = big["ev_w_in" if even else "od_w_in"][j]
    w_out = big["ev_w_out" if even else "od_w_out"][j]
    lw = {"in": {}, "out": []}
    for name, (a, b) in (EV_COLS if even else OD_COLS).items():
        w = _pad_to_lanes(w_in[:, a:b])
        lw["in"][name] = (w, w.T)
    for a, b in (EV_OUT_ROWS if even else OD_OUT_ROWS):
        lw["out"].append((w_out[a:b], w_out[a:b].T))
    for name in ("gate", "up", "down"):
        w = big["ffn_w_" + name][l]
        lw[name] = (w, w.T)
    return lw


def _rows2d(a):
    return a.reshape(-1, a.shape[-1])


def _mm3(a, w, name, out_dtype=F32):
    return mm([(_rows2d(a), w)], name, out_dtype).reshape(a.shape[:-1] + (w.shape[1],))


def _wgrad(a, d, name):
    return mm_tn(_rows2d(a), _rows2d(d), name)


def _dgrad(pairs, name, shape3):
    return mm([(_rows2d(d), wt) for d, wt in pairs], name).reshape(shape3[:-1] + (pairs[0][1].shape[1],))


def _dir_dt(dt, d):
    nb, s, _ = dt.shape
    dd = dt[:, :, SSD_HEADS * d:SSD_HEADS * (d + 1)].reshape(nb, s, SSD_GROUPS, SSD_HPG)
    return jnp.transpose(dd, (0, 2, 1, 3)), jnp.transpose(dd, (0, 2, 3, 1))


def _s5_prepare(p, j, tag):
    g_, p_, k_ = S5_GROUPS, S5_STATE, S5_GROUP_CH
    col = lambda t: t.reshape(g_ * p_, 1)
    ins, outs = [], []
    for d in (0, 1):
        i_d = [col(p["s5_lam_re"][j, d]), col(p["s5_lam_im"][j, d]), col(jnp.repeat(p["s5_log_step"][j, d], p_)),
               p["s5_b_re"][j].reshape(g_ * p_, k_), p["s5_b_im"][j].reshape(g_ * p_, k_)]
        ins.append(i_d)
        outs.append(small_fwd(f"{tag}_disc{d}", f_s5_discretize, i_d, [(g_ * p_, 1)] * 2 + [(g_ * p_, k_)] * 2))
    lam_r = jnp.stack([o[0].reshape(1, g_ * p_) for o in outs])
    lam_i = jnp.stack([o[1].reshape(1, g_ * p_) for o in outs])
    bt_r = jnp.stack([s5_tiles_of(o[2].reshape(g_, p_, k_)) for o in outs])
    bt_i = jnp.stack([s5_tiles_of(o[3].reshape(g_, p_, k_)) for o in outs])
    ct_r = jnp.stack([s5_tiles_of(jnp.transpose(p["s5_c_re"][j, d], (0, 2, 1))) for d in (0, 1)])
    ct_i = jnp.stack([s5_tiles_of(jnp.transpose(p["s5_c_im"][j, d], (0, 2, 1))) for d in (0, 1)])
    return ins, (lam_r, lam_i, bt_r, bt_i, ct_r, ct_i)


def _s5_param_grads(ins, grads, tag):
    g_, p_, k_ = S5_GROUPS, S5_STATE, S5_GROUP_CH
    dlr, dli, dbtr, dbti, dctr, dcti = grads
    g_lre, g_lim, g_ls, g_bre, g_bim = [], [], [], 0.0, 0.0
    for d in (0, 1):
        cots = [dlr[d].reshape(g_ * p_, 1), dli[d].reshape(g_ * p_, 1),
                s5_groups_of(dbtr[d], g_, p_, k_).reshape(g_ * p_, k_), s5_groups_of(dbti[d], g_, p_, k_).reshape(g_ * p_, k_)]
        g = small_bwd(f"{tag}_disc_bwd{d}", f_s5_discretize, ins[d], cots)
        g_lre.append(g[0].reshape(g_, p_))
        g_lim.append(g[1].reshape(g_, p_))
        g_ls.append(g[2].reshape(g_, p_).sum(-1))
        g_bre = g_bre + g[3].reshape(g_, p_, k_)
        g_bim = g_bim + g[4].reshape(g_, p_, k_)
    g_cre = jnp.stack([jnp.transpose(s5_groups_of(dctr[d], g_, p_, k_), (0, 2, 1)) for d in (0, 1)])
    g_cim = jnp.stack([jnp.transpose(s5_groups_of(dcti[d], g_, p_, k_), (0, 2, 1)) for d in (0, 1)])
    return jnp.stack(g_lre), jnp.stack(g_lim), jnp.stack(g_ls), g_bre, g_bim, g_cre, g_cim


def _even_mixer_fwd(l, hn, p, lw, s_ctx):
    j = l // 2
    t1 = taps_1d(4, s_ctx, hn.shape[1])
    r = {"hn": hn}
    proj = {n: _mm3(hn, lw["in"][n][0], f"l{l}_proj_{n}") for n in EV_COLS}
    r["z"], r["xbc"], r["gy"], r["u"] = proj["z"], proj["xbc"], proj["gy"], proj["u"]
    r["dtp"] = proj["dt"][:, :, :2 * SSD_HEADS]
    r["xbc_c"] = conv_fwd(f"l{l}_ssd_conv", r["xbc"], p["ssd_conv_w"][j], p["ssd_conv_b"][j][None], t1, "silu")
    r["u_c"] = conv_fwd(f"l{l}_lru_conv", r["u"], p["lru_conv_w"][j], p["lru_conv_b"][j][None], t1, "none")
    r["dt_bias"] = p["ssd_dt_bias"][j].reshape(1, 2 * SSD_HEADS)
    (r["dt"],) = tok_fwd(f"l{l}_dt", f_dt, [r["dtp"]], None, [r["dt_bias"]], [2 * SSD_HEADS], [F32])
    r["ys"], r["hst"], r["dts"], r["alog"] = [], [], [], []
    for d in (0, 1):
        dtc, dtr = _dir_dt(r["dt"], d)
        al = p["ssd_a_log"][j, d].reshape(SSD_GROUPS, SSD_HPG)
        al_r, al_c = al[:, None, :], al[:, :, None]
        y, hst = ssd_fwd(f"l{l}_ssd_fwd{d}", r["xbc_c"], dtc, dtr, al_r, al_c, bool(d), s_ctx)
        r["ys"].append(y)
        r["hst"].append(hst)
        r["dts"].append((dtc, dtr))
        r["alog"].append((al_r, al_c))
    v4 = lambda t: t.reshape(2, LRU_BLOCKS, 1, LRU_BLOCK_W)
    r["lru_p"] = (p["lru_w_a"][j], v4(p["lru_b_a"][j]), p["lru_w_i"][j], v4(p["lru_b_i"][j]), v4(p["lru_lam"][j]))
    r["h_sum"] = lru_fwd(f"l{l}_lru_fwd", r["u_c"], *r["lru_p"], s_ctx)
    r["xs"] = r["xbc_c"][:, :, :SSD_HEADS * SSD_HEAD_DIM]
    r["fin_p"] = [jnp.repeat(p["ssd_d"][j], SSD_HEAD_DIM)[None], p["ssd_norm_g"][j][None]]
    r["fin_in"] = [r["ys"][0], r["ys"][1], r["xs"], r["z"], r["h_sum"], r["gy"]]
    r["o"] = tok_fwd(f"l{l}_even_finish", f_even_finish, r["fin_in"], None, r["fin_p"], [1024, 1024], [BF16, BF16])
    return r


def _even_mixer_bwd(l, r, dox, p, lw, s_ctx, grads):
    j = l // 2
    shape3 = dox.shape
    t1 = taps_1d(4, s_ctx, shape3[1])
    grads["ev_w_out"][j] = jnp.concatenate([_wgrad(o, dox, f"l{l}_dwout{i}") for i, o in enumerate(r["o"])], axis=0)
    do = [_dgrad([(dox, lw["out"][i][1])], f"l{l}_dout{i}", shape3) for i in range(2)]
    (dy, _, dxs, dz, dh_sum, dgy), _, (dd_exp, grads["ssd_norm_g"][j]) = tok_bwd(
        f"l{l}_even_finish_bwd", f_even_finish, r["fin_in"], None, r["fin_p"], do, [F32, F32, F32, BF16, F32, BF16])
    grads["ssd_d"][j] = dd_exp.reshape(SSD_HEADS, SSD_HEAD_DIM).sum(-1)
    du_c, dwa, dba, dwi, dbi, dlam = lru_bwd(f"l{l}_lru_bwd", r["u_c"], *r["lru_p"], dh_sum, s_ctx)
    grads["lru_w_a"][j], grads["lru_w_i"][j] = dwa, dwi
    v2 = lambda t: t.reshape(2, LRU_BLOCKS * LRU_BLOCK_W)
    grads["lru_b_a"][j], grads["lru_b_i"][j], grads["lru_lam"][j] = v2(dba), v2(dbi), v2(dlam)
    dx_sum, dbm_sum, dcm_sum, ddts, dalog = dxs, 0.0, 0.0, [], []
    for d in (0, 1):
        dx, dbm, dcm, ddtc, ddtr, dar, dac = ssd_bwd(
            f"l{l}_ssd_bwd{d}", r["xbc_c"], *r["dts"][d], *r["alog"][d], r["hst"][d], dy, bool(d), s_ctx)
        dx_sum, dbm_sum, dcm_sum = dx_sum + dx, dbm_sum + dbm, dcm_sum + dcm
        ddts.append((jnp.transpose(ddtc, (0, 2, 1, 3)) + jnp.transpose(ddtr, (0, 3, 1, 2))).reshape(shape3[0], shape3[1], SSD_HEADS))
        dalog.append((dar.sum(0)[:, 0, :] + dac.sum(0)[:, :, 0]).reshape(SSD_HEADS))
    grads["ssd_a_log"][j] = jnp.stack(dalog)
    dxbc_c = jnp.concatenate([dx_sum, dbm_sum, dcm_sum], axis=-1)
    (ddtp,), _, (ddt_bias,) = tok_bwd(f"l{l}_dt_bwd", f_dt, [r["dtp"]], None, [r["dt_bias"]], [jnp.concatenate(ddts, axis=-1)], [F32])
    grads["ssd_dt_bias"][j] = ddt_bias.reshape(2, SSD_HEADS)
    dxbc, grads["ssd_conv_w"][j], dcb = conv_bwd(f"l{l}_ssd_conv_bwd", r["xbc"], p["ssd_conv_w"][j], p["ssd_conv_b"][j][None], dxbc_c, t1, "silu", dx_dtype=BF16)
    du, grads["lru_conv_w"][j], dlb = conv_bwd(f"l{l}_lru_conv_bwd", r["u"], p["lru_conv_w"][j], p["lru_conv_b"][j][None], du_c, t1, "none", dx_dtype=BF16)
    grads["ssd_conv_b"][j], grads["lru_conv_b"][j] = dcb[0], dlb[0]
    dproj = {"z": dz, "xbc": dxbc, "dt": _pad_to_lanes(_rows2d(ddtp)).reshape(shape3[:2] + (LANES,)), "gy": dgy, "u": du}
    grads["ev_w_in"][j] = jnp.concatenate(
        [_wgrad(r["hn"], dproj[n], f"l{l}_dwin_{n}")[:, :b - a] for n, (a, b) in EV_COLS.items()], axis=1)
    return _dgrad([(dproj[n], lw["in"][n][1]) for n in EV_COLS], f"l{l}_dhn", shape3)


def _odd_mixer_fwd(l, hn, p, lw, lb_row, s_ctx):
    j = l // 2
    r = {"hn": hn}
    proj = {n: _mm3(hn, lw["in"][n][0], f"l{l}_proj_{n}") for n in OD_COLS}
    r["v"], r["g"], r["u"] = proj["v"], proj["g"], proj["u"]
    r["prep_in"] = [proj["q"], proj["ff"], proj["fb"]]
    r["lb"] = lb_row
    r["prep"] = tok_fwd(f"l{l}_odd_prep", f_odd_prep, r["prep_in"], None, [lb_row], [HG_W] * 5, [F32] * 5)
    qs = r["prep"][0]
    r["os"], r["hst"] = [], []
    for d in (0, 1):
        o, hst = hg_fwd(f"l{l}_hg_fwd{d}", qs, r["prep"][1 + 2 * d], r["prep"][2 + 2 * d], r["v"], bool(d), s_ctx)
        r["os"].append(o)
        r["hst"].append(hst)
    r["s5_ins"], r["s5_p"] = _s5_prepare(p, j, f"l{l}_s5")
    r["s5y"] = s5_fwd(f"l{l}_s5_fwd", r["u"], *r["s5_p"], s_ctx)
    r["fin_p"] = [p["hg_norm_g"][j].reshape(1, HG_W), p["s5_d"][j][None], p["s5_glu_w"][j], p["s5_glu_b"][j][None]]
    r["fin_in"] = [r["os"][0], r["os"][1], r["g"], r["s5y"], r["u"]]
    r["o"] = tok_fwd(f"l{l}_odd_finish", f_odd_finish, r["fin_in"], None, r["fin_p"], [HG_W, S5_W], [BF16, BF16])
    return r


def _odd_mixer_bwd(l, r, dox, p, lw, s_ctx, grads):
    j = l // 2
    shape3 = dox.shape
    grads["od_w_out"][j] = jnp.concatenate([_wgrad(o, dox, f"l{l}_dwout{i}") for i, o in enumerate(r["o"])], axis=0)
    do = [_dgrad([(dox, lw["out"][i][1])], f"l{l}_dout{i}", shape3) for i in range(2)]
    (do_hg, _, dg, ds5y, du_fin), _, (dng, grads["s5_d"][j], grads["s5_glu_w"][j], dglu_b) = tok_bwd(
        f"l{l}_odd_finish_bwd", f_odd_finish, r["fin_in"], None, r["fin_p"], do, [F32, F32, BF16, F32, F32])
    grads["hg_norm_g"][j] = dng.reshape(HG_HEADS, HG_DK)
    grads["s5_d"][j], grads["s5_glu_b"][j] = grads["s5_d"][j][0], dglu_b[0]
    s5g = s5_bwd(f"l{l}_s5_bwd", r["u"], *r["s5_p"], ds5y, s_ctx)
    du = s5g[0] + du_fin
    (grads["s5_lam_re"][j], grads["s5_lam_im"][j], grads["s5_log_step"][j], grads["s5_b_re"][j], grads["s5_b_im"][j],
     grads["s5_c_re"][j], grads["s5_c_im"][j]) = _s5_param_grads(r["s5_ins"], s5g[1:], f"l{l}_s5")
    qs = r["prep"][0]
    dqs, dv, dprep = 0.0, 0.0, [None] * 5
    for d in (0, 1):
        dq, dk, dlf, dvd = hg_bwd(f"l{l}_hg_bwd{d}", qs, r["prep"][1 + 2 * d], r["prep"][2 + 2 * d], r["v"], r["hst"][d], do_hg, bool(d), s_ctx)
        dqs, dv = dqs + dq, dv + dvd
        dprep[1 + 2 * d], dprep[2 + 2 * d] = dk, dlf
    dprep[0] = dqs
    (dq_, dff, dfb), _, (dlb,) = tok_bwd(f"l{l}_odd_prep_bwd", f_odd_prep, r["prep_in"], None, [r["lb"]], dprep, [BF16] * 3)
    dproj = {"q": dq_, "ff": dff, "fb": dfb, "v": dv, "g": dg, "u": du}
    grads["od_w_in"][j] = jnp.concatenate([_wgrad(r["hn"], dproj[n], f"l{l}_dwin_{n}") for n in OD_COLS], axis=1)
    return _dgrad([(dproj[n], lw["in"][n][1]) for n in OD_COLS], f"l{l}_dhn", shape3), dlb


def _ffn_fwd(l, fn, p, lw, s_ctx):
    r = {"fn": fn}
    tg = taps_grid(s_ctx, fn.shape[1], GRID_W)
    r["a"] = _mm3(fn, lw["gate"][0], f"l{l}_ffn_gate")
    r["up"] = _mm3(fn, lw["up"][0], f"l{l}_ffn_up")
    r["cw"], r["cb"] = p["ffn_conv_w"][l].reshape(9, D_FF), p["ffn_conv_b"][l][None]
    r["act"] = conv_fwd(f"l{l}_ffn_conv", r["a"], r["cw"], r["cb"], tg, "silu_mul", mul=r["up"], out_dtype=BF16)
    return r, _mm3(r["act"], lw["down"][0], f"l{l}_ffn_down")


def _ffn_bwd(l, r, dfo, lw, s_ctx, grads):
    shape3 = dfo.shape
    tg = taps_grid(s_ctx, shape3[1], GRID_W)
    grads["ffn_w_down"][l] = _wgrad(r["act"], dfo, f"l{l}_dwdown")
    dact = _dgrad([(dfo, lw["down"][1])], f"l{l}_dact", shape3)
    da, dcw, dcb, dup = conv_bwd(f"l{l}_ffn_conv_bwd", r["a"], r["cw"], r["cb"], dact, tg, "silu_mul", mul=r["up"], dx_dtype=BF16)
    grads["ffn_conv_w"][l], grads["ffn_conv_b"][l] = dcw.reshape(3, 3, D_FF), dcb[0]
    grads["ffn_w_gate"][l] = _wgrad(r["fn"], da, f"l{l}_dwgate")
    grads["ffn_w_up"][l] = _wgrad(r["fn"], dup, f"l{l}_dwup")
    return _dgrad([(da, lw["gate"][1]), (dup, lw["up"][1])], f"l{l}_dfn", shape3)


BIG_WEIGHTS = ("ev_w_in", "ev_w_out", "od_w_in", "od_w_out", "ffn_w_gate", "ffn_w_up", "ffn_w_down")
PER_LAYER = {"norm_mix_g": DEPTH, "norm_ffn_g": DEPTH, "ffn_w_gate": DEPTH, "ffn_w_up": DEPTH, "ffn_conv_w": DEPTH,
             "ffn_conv_b": DEPTH, "ffn_w_down": DEPTH}


def local_step(x, ctx, target, modtabs, p, big, s_ctx=CTX_LEN):
    d_model = x.shape[-1]
    s0 = jnp.concatenate([ctx, x], axis=1)
    lws = [_layer_weights(l, big) for l in range(DEPTH)]
    shapes = {n: v.shape for n, v in {**p, **big}.items()}
    grads = {n: [None] * PER_LAYER.get(n, DEPTH // 2) for n in shapes if n not in ("c_ctx", "w_mod", "b_mod", "final_norm_g", "hg_lb_logits")}
    (lbs,) = small_fwd("lower_bounds", f_lower_bounds, [p["hg_lb_logits"]], [p["hg_lb_logits"].shape])
    tab_a = [modtabs[0]] + [modtabs[l].at[:, N_MOD - 1].set(modtabs[l - 1][:, N_MOD - 1]) for l in range(1, DEPTH)]
    res = []
    s, br = s0, None
    for l in range(DEPTH):
        r = {}
        g_mix, g_ffn = p["norm_mix_g"][l][None], p["norm_ffn_g"][l][None]
        if l == 0:
            (hn,) = tok_fwd("l0_norm", f_norm(0, 1), [s], tab_a[0], [g_mix], [d_model], [BF16])
            r["a_in"] = [s]
        else:
            r["a_in"] = [s, br]
            s, hn = tok_fwd(f"l{l}_resnorm_a", f_resnorm(5, 0, 1), r["a_in"], tab_a[l], [g_mix], [d_model] * 2, [F32, BF16])
        if l % 2 == 0:
            r["mix"] = _even_mixer_fwd(l, hn, p, lws[l], s_ctx)
        else:
            r["mix"] = _odd_mixer_fwd(l, hn, p, lws[l], lbs[l:l + 1], s_ctx)
        ox = mm([(_rows2d(o), w) for o, (w, _) in zip(r["mix"]["o"], lws[l]["out"])], f"l{l}_mix_out").reshape(s.shape)
        r["b_in"] = [s, ox]
        s, fn = tok_fwd(f"l{l}_resnorm_b", f_resnorm(2, 3, 4), r["b_in"], modtabs[l], [g_ffn], [d_model] * 2, [F32, BF16])
        r["ffn"], br = _ffn_fwd(l, fn, p, lws[l], s_ctx)
        res.append(r)

    loss_blk, ds, dbr, dtab_f, dfinal_g = final_loss("final_loss", s, br, modtabs[DEPTH - 1], p["final_norm_g"][None], target, s_ctx)
    grads["final_norm_g"] = dfinal_g[0]
    dmod = [None] * DEPTH
    dtab_next = dtab_f
    dlb = jnp.zeros_like(lbs)
    for l in reversed(range(DEPTH)):
        r = res[l]
        g_mix, g_ffn = p["norm_mix_g"][l][None], p["norm_ffn_g"][l][None]
        dfn = _ffn_bwd(l, r["ffn"], dbr, lws[l], s_ctx, grads)
        (ds, dox), dtab_b, (grads["norm_ffn_g"][l],) = tok_bwd(
            f"l{l}_resnorm_b_bwd", f_resnorm(2, 3, 4), r["b_in"], modtabs[l], [g_ffn], [ds, dfn], [F32, BF16])
        if l % 2 == 0:
            dhn = _even_mixer_bwd(l, r["mix"], dox, p, lws[l], s_ctx, grads)
        else:
            dhn, dlb_l = _odd_mixer_bwd(l, r["mix"], dox, p, lws[l], s_ctx, grads)
            dlb = dlb.at[l:l + 1].set(dlb_l)
        if l == 0:
            (ds,), dtab_a, (dg,) = tok_bwd("l0_norm_bwd", f_norm_keep(0, 1), r["a_in"], tab_a[0], [g_mix], [ds, dhn], [F32])
        else:
            (ds, dbr), dtab_a, (dg,) = tok_bwd(
                f"l{l}_resnorm_a_bwd", f_resnorm(5, 0, 1), r["a_in"], tab_a[l], [g_mix], [ds, dhn], [F32, BF16])
        grads["norm_mix_g"][l] = dg
        dmod[l] = (dtab_a.at[:, N_MOD - 1].set(0.0) + dtab_b).at[:, N_MOD - 1].set(dtab_next[:, N_MOD - 1])
        dtab_next = dtab_a
    (grads["hg_lb_logits"],) = small_bwd("lower_bounds_bwd", f_lower_bounds, [p["hg_lb_logits"]], [dlb])
    out = {}
    for n, g in grads.items():
        if isinstance(g, list):
            g = jnp.stack([t.reshape(shapes[n][1:]) for t in g])
        out[n] = g.reshape(shapes[n])
    return loss_blk[:, 0, 0], ds[:, s_ctx:], dmod, out


WEIGHT_NAMES = (
    "c_ctx", "w_mod", "b_mod", "norm_mix_g", "norm_ffn_g", "final_norm_g", "ev_w_in", "ev_w_out", "ssd_conv_w",
    "ssd_conv_b", "ssd_dt_bias", "ssd_a_log", "ssd_d", "ssd_norm_g", "lru_conv_w", "lru_conv_b", "lru_w_a", "lru_b_a",
    "lru_w_i", "lru_b_i", "lru_lam", "od_w_in", "od_w_out", "hg_lb_logits", "hg_norm_g", "s5_lam_re", "s5_lam_im",
    "s5_log_step", "s5_b_re", "s5_b_im", "s5_c_re", "s5_c_im", "s5_d", "s5_glu_w", "s5_glu_b", "ffn_w_gate", "ffn_w_up",
    "ffn_conv_w", "ffn_conv_b", "ffn_w_down")
INPUT_NAMES = ("x", "c", "ctx") + WEIGHT_NAMES + ("loss_target",) + tuple("m_" + n for n in WEIGHT_NAMES) + tuple("v_" + n for n in WEIGHT_NAMES)
SHARD_AXIS = {"w_mod": 2, "ev_w_in": 2, "ev_w_out": 1, "ssd_conv_w": 2, "lru_conv_w": 2, "lru_b_a": 2, "lru_b_i": 2,
              "lru_lam": 2, "od_w_in": 2, "od_w_out": 1, "s5_d": 1, "s5_glu_w": 1, "s5_glu_b": 1, "ffn_w_gate": 2,
              "ffn_w_up": 2, "ffn_conv_w": 3, "ffn_w_down": 1}
SMALL_SHARDED = tuple(n for n in WEIGHT_NAMES if n in SHARD_AXIS and n not in BIG_WEIGHTS and n != "w_mod")
REPLICATED_LOCAL = tuple(n for n in WEIGHT_NAMES if n not in SHARD_AXIS and n not in ("c_ctx", "b_mod"))
PACK_WIDTH = 1024
MOD_ROWS = 48
CTX_ROW = 32


def _unshard(g8, axis):
    moved = jnp.moveaxis(g8, 0, axis)
    shp = moved.shape
    return moved.reshape(shp[:axis] + (shp[axis] * shp[axis + 1],) + shp[axis + 2:])


def _to_shards(full, axis):
    shp = full.shape
    return jnp.moveaxis(full.reshape(shp[:axis] + (N_DEV, shp[axis] // N_DEV) + shp[axis + 1:]), axis, 0)


def _pack(arrs, dtype, lead=(), row_mult=16):
    flat = jnp.concatenate([a.astype(dtype).reshape(lead + (-1,)) for a in arrs], axis=-1)
    n = flat.shape[-1]
    unit = row_mult * PACK_WIDTH
    padded = -(-n // unit) * unit
    flat = jnp.pad(flat, [(0, 0)] * len(lead) + [(0, padded - n)])
    return flat.reshape(lead + (padded // PACK_WIDTH, PACK_WIDTH))


def _unpack(packed, shapes, lead=()):
    flat = packed.reshape(lead + (-1,))
    out, off = [], 0
    for shp in shapes:
        n = math.prod(shp)
        out.append(flat[..., off:off + n].reshape(lead + tuple(shp)))
        off += n
    return out


def _my_block(full, axis, me):
    loc = full.shape[axis] // N_DEV
    return lax.dynamic_slice_in_dim(full, me * loc, loc, axis)


def kernel(*args):
    a = dict(zip(INPUT_NAMES, args))
    px, py, pc = _my_pos()
    me = 4 * px + 2 * py + pc
    nb = a["x"].shape[0]

    small_names = ("c",) + SMALL_SHARDED
    *big8, small8 = all_gather([a[n].astype(BF16) for n in BIG_WEIGHTS] + [_pack([a[n] for n in small_names], F32)],
                               "gather_weights")
    big = {n: _unshard(g, SHARD_AXIS[n]) for n, g in zip(BIG_WEIGHTS, big8)}
    small = dict(zip(small_names, _unpack(small8, [a[n].shape for n in small_names], (N_DEV,))))
    p = {n: a[n] for n in WEIGHT_NAMES if n not in SHARD_AXIS}
    for n in SMALL_SHARDED:
        p[n] = _unshard(small[n], SHARD_AXIS[n])
    c_all = small["c"].reshape(N_DEV * nb, D_MODEL)

    rows = jnp.concatenate([c_all, a["c_ctx"][None], jnp.zeros((MOD_ROWS - CTX_ROW - 1, D_MODEL), F32)], axis=0)
    (srows,) = small_fwd("mod_silu", f_silu, [rows], [rows.shape])
    wmod2d = jnp.transpose(a["w_mod"], (1, 0, 2)).reshape(D_MODEL, -1).astype(BF16)
    cols = a["w_mod"].shape[2]
    mod_loc = mm([(srows, wmod2d)], "mod_proj")
    mod8 = all_gather([mod_loc], "gather_mod")[0].reshape(N_DEV, MOD_ROWS, DEPTH, cols)
    mod_all = jnp.transpose(mod8, (2, 1, 0, 3)).reshape(DEPTH, MOD_ROWS, N_DEV * cols) + a["b_mod"][:, None, :]
    modtabs = []
    for l in range(DEPTH):
        mine = lax.dynamic_slice_in_dim(mod_all[l], me * nb, nb, 0).reshape(nb, N_MOD, D_MODEL)
        ctx_row = jnp.broadcast_to(mod_all[l, CTX_ROW].reshape(1, N_MOD, D_MODEL), (nb, N_MOD, D_MODEL))
        modtabs.append(jnp.stack([ctx_row, mine], axis=1).reshape(2 * nb, N_MOD, D_MODEL))

    loss_b, grad_x, dmod, grads = local_step(a["x"], a["ctx"], a["loss_target"], modtabs, p, big)

    dm = jnp.stack([t.reshape(nb, 2, N_MOD * D_MODEL) for t in dmod])
    dloc = jnp.concatenate([dm[:, :, 1], jnp.sum(dm[:, :, 0], axis=1, keepdims=True),
                            jnp.zeros((DEPTH, SUBLANES - nb - 1, N_MOD * D_MODEL), F32)], axis=1)
    d8 = all_gather([dloc.reshape(DEPTH * SUBLANES, -1)], "gather_dmod")[0].reshape(N_DEV, DEPTH, SUBLANES, -1)
    d_rows = jnp.transpose(d8[:, :, :nb], (1, 0, 2, 3)).reshape(DEPTH, N_DEV * nb, -1)
    d_ctx = jnp.sum(d8[:, :, nb], axis=0)[:, None]
    d_full = jnp.concatenate([d_rows, d_ctx, jnp.zeros((DEPTH, MOD_ROWS - CTX_ROW - 1, N_MOD * D_MODEL), F32)], axis=1)
    grads["b_mod"] = jnp.sum(d_full, axis=1)
    d_cols = jnp.transpose(_my_block(d_full, 2, me), (1, 0, 2)).reshape(MOD_ROWS, DEPTH * cols)
    g_wmod = mm([(srows.T, d_cols)], "mod_dw")
    g_wmod_local = jnp.transpose(g_wmod.reshape(D_MODEL, DEPTH, cols), (1, 0, 2))
    d_srows_part = mm([(d_cols[CTX_ROW:CTX_ROW + SUBLANES], wmod2d.T)], "mod_dctx")[0]

    reduce_names = REPLICATED_LOCAL + SMALL_SHARDED
    parts = [jnp.sum(loss_b).reshape(1), d_srows_part] + [grads[n] for n in reduce_names]
    packed = _pack(parts, F32, row_mult=SUBLANES * N_DEV)
    own = [_to_shards(grads[n], SHARD_AXIS[n]).astype(BF16).reshape(N_CHIPS, 2, -1, a[n].shape[-1]) for n in BIG_WEIGHTS]
    own.append(packed.reshape(N_CHIPS, 2, -1, PACK_WIDTH))
    names = BIG_WEIGHTS + ("small",)
    from_sibling = sibling_swap(own, "exchange_sibling_grads")
    chip_sums = [pair_sum(o, s, "pair_sum_" + n) for n, o, s in zip(names, own, from_sibling)]
    got = chip_exchange(chip_sums, "exchange_chip_grads")
    eighths = [sum_slots(t, "sum_" + n) for n, t in zip(names, got)]
    (small8,) = all_gather([eighths[-1]], "gather_small_sums")
    totals = _unpack(small8, [(1,), (D_MODEL,)] + [grads[n].shape for n in reduce_names])
    loss = totals[0][0]
    d_srows = jnp.zeros_like(rows).at[CTX_ROW].set(totals[1])
    (d_rows_in,) = small_bwd("mod_silu_bwd", f_silu, [rows], [d_srows])
    g_local = {"c_ctx": d_rows_in[CTX_ROW], "b_mod": grads["b_mod"], "w_mod": g_wmod_local}
    for n, t in zip(reduce_names, totals[2:]):
        g_local[n] = _my_block(t, SHARD_AXIS[n], me) if n in SHARD_AXIS else t
    for n, t in zip(BIG_WEIGHTS, eighths):
        g_local[n] = t.reshape(a[n].shape)

    deltas, new_m, new_v = [], [], []
    for n in WEIGHT_NAMES:
        d, m, v = adamw("adamw_" + n, a[n], g_local[n], a["m_" + n], a["v_" + n])
        deltas.append(d)
        new_m.append(m)
        new_v.append(v)
    return (loss, grad_x, *[g_local[n] for n in WEIGHT_NAMES], *deltas, *new_m, *new_v)
```

```python
import functools
import math

import jax
import jax.numpy as jnp
from jax import lax
from jax.experimental import pallas as pl
from jax.experimental.pallas import tpu as pltpu

F32 = jnp.float32
BF16 = jnp.bfloat16

D_MODEL = 1024
DEPTH = 4
CTX_LEN = 256
SEQ = 2048
S_TOT = CTX_LEN + SEQ
GRID_W = 64
N_MOD = 6
RMS_EPS = 1e-6
N_DEV = 8

SSD_HEADS = 16
SSD_HEAD_DIM = 64
SSD_GROUPS = 2
SSD_HPG = 8
SSD_STATE = 128
SSD_CHUNK = 128
SSD_W = SSD_HEADS * SSD_HEAD_DIM
LRU_BLOCKS = 8
LRU_BLOCK_W = 128
LRU_C = 8.0
HG_W = 768
HG_HEADS = 6
HG_DK = 128
HG_CHUNK = 32
S5_W = 256
S5_GROUPS = 16
S5_GROUP_CH = 16
S5_STATE = 64
D_FF = 2816

ADAM_LR = 0.001
ADAM_B1 = 0.9
ADAM_B2 = 0.999
ADAM_EPS = 1e-08
ADAM_WD = 0.01
ADAM_STEP = 10

TOK_BLOCK = CTX_LEN
SUBLANES = 8
VMEM_LIMIT_BYTES = 56 * 1024 * 1024
MM_BLOCK_BYTES = 8 * 1024 * 1024
MM_TILES = (1408, 1024, 768, 704, 512, 384, 352, 256, 128, 64, 48, 40, 32, 16, 8)
MM_ROW_TILES = (2304, 2048, 1152, 1024, 512, 256, 128, 64, 48, 32, 16, 8)
LANE_TILE = 128


def _cparams(sem=None):
    kw = dict(vmem_limit_bytes=VMEM_LIMIT_BYTES)
    if sem is not None:
        kw["dimension_semantics"] = sem
    return pltpu.CompilerParams(**kw)


def _pick(n, cands):
    for c in cands:
        if n % c == 0:
            return c
    return n


def mm(pairs, name, out_dtype=F32):
    m = pairs[0][0].shape[0]
    n = pairs[0][1].shape[1]
    kdims = [a.shape[1] for a, _ in pairs]
    ktile = None
    if len(pairs) == 1 and kdims[0] > 4096:
        ktile = _pick(kdims[0], (2304, 2048, 1024))
    nk = kdims[0] // ktile if ktile else 1
    col_bytes = sum((ktile or w.shape[0]) * w.dtype.itemsize for _, w in pairs)
    tn = _pick(n, tuple(c for c in MM_TILES if c % LANE_TILE == 0 and c * col_bytes <= MM_BLOCK_BYTES))
    row_bytes = max(sum((ktile or a.shape[1]) * a.dtype.itemsize for a, _ in pairs), tn * 4)
    tm = _pick(m, tuple(c for c in MM_TILES if c * row_bytes <= MM_BLOCK_BYTES))
    npairs = len(pairs)
    if nk > 1:
        assert out_dtype == F32

    def body(*refs):
        o_ref = refs[2 * npairs]
        acc = None
        for i in range(npairs):
            a = refs[2 * i][...].astype(BF16)
            w = refs[2 * i + 1][...].astype(BF16)
            p = jnp.dot(a, w, preferred_element_type=F32)
            acc = p if acc is None else acc + p
        if nk == 1:
            o_ref[...] = acc.astype(out_dtype)
        else:
            k = pl.program_id(2)

            @pl.when(k == 0)
            def _():
                o_ref[...] = acc

            @pl.when(k > 0)
            def _():
                o_ref[...] += acc

    in_specs = []
    args = []
    for a, w in pairs:
        kk = a.shape[1]
        assert w.shape == (kk, n) and a.shape[0] == m, (a.shape, w.shape)
        tk = ktile if ktile else kk
        in_specs.append(pl.BlockSpec((tm, tk), lambda i, j, k: (i, k)))
        in_specs.append(pl.BlockSpec((tk, tn), lambda i, j, k: (k, j)))
        args += [a, w]
    return pl.pallas_call(
        body,
        name=name,
        grid=(m // tm, n // tn, nk),
        in_specs=in_specs,
        out_specs=pl.BlockSpec((tm, tn), lambda i, j, k: (i, j)),
        out_shape=jax.ShapeDtypeStruct((m, n), out_dtype),
        compiler_params=_cparams(("parallel", "parallel", "arbitrary")),
    )(*args)


def mm_tn(a, d, name):
    r, k = a.shape
    n = d.shape[1]
    tr = _pick(r, MM_ROW_TILES)
    lane_ok = lambda c, full: c % LANE_TILE == 0 or c == full
    tk = _pick(k, tuple(c for c in MM_TILES if lane_ok(c, k) and c * tr * a.dtype.itemsize <= MM_BLOCK_BYTES))
    tn = _pick(n, tuple(c for c in MM_TILES if lane_ok(c, n) and c * tr * d.dtype.itemsize <= MM_BLOCK_BYTES
                        and c * tk * 4 <= MM_BLOCK_BYTES))

    def body(a_ref, d_ref, o_ref):
        acc = lax.dot_general(a_ref[...].astype(BF16), d_ref[...].astype(BF16), (((0,), (0,)), ((), ())),
                              preferred_element_type=F32)
        step = pl.program_id(2)

        @pl.when(step == 0)
        def _():
            o_ref[...] = acc

        @pl.when(step > 0)
        def _():
            o_ref[...] += acc

    return pl.pallas_call(
        body,
        name=name,
        grid=(k // tk, n // tn, r // tr),
        in_specs=[pl.BlockSpec((tr, tk), lambda i, j, s: (s, i)), pl.BlockSpec((tr, tn), lambda i, j, s: (s, j))],
        out_specs=pl.BlockSpec((tk, tn), lambda i, j, s: (i, j)),
        out_shape=jax.ShapeDtypeStruct((k, n), F32),
        compiler_params=_cparams(("parallel", "parallel", "arbitrary")),
    )(a, d)


def _mod_index(b, t):
    return (2 * b + jnp.minimum(t, 1), 0, 0)


def tok_fwd(name, f, toks, mod, params, out_widths, out_dtypes):
    nb, s, _ = toks[0].shape
    nt, nm, npar = len(toks), int(mod is not None), len(params)

    def body(*refs):
        ins, outs = refs[: nt + nm + npar], refs[nt + nm + npar:]
        tv = [r[...].astype(F32) for r in ins[:nt]]
        mv = [ins[nt][k:k + 1, :] for k in range(N_MOD)] if nm else None
        pv = [r[...] for r in ins[nt + nm:]]
        for o, r in zip(outs, f(tv, mv, pv)):
            o[...] = r.astype(o.dtype)

    in_specs = [pl.BlockSpec((None, TOK_BLOCK, t.shape[2]), lambda b, t: (b, t, 0)) for t in toks]
    if nm:
        in_specs.append(pl.BlockSpec((None, N_MOD, mod.shape[2]), _mod_index))
    in_specs += [pl.BlockSpec(p.shape, lambda b, t, nd=p.ndim: (0,) * nd) for p in params]
    return pl.pallas_call(
        body,
        name=name,
        grid=(nb, s // TOK_BLOCK),
        in_specs=in_specs,
        out_specs=[pl.BlockSpec((None, TOK_BLOCK, w), lambda b, t: (b, t, 0)) for w in out_widths],
        out_shape=[jax.ShapeDtypeStruct((nb, s, w), dt) for w, dt in zip(out_widths, out_dtypes)],
        compiler_params=_cparams(("parallel", "parallel")),
    )(*toks, *([mod] if nm else []), *params)


def tok_bwd(name, f, toks, mod, params, cots, dtok_dtypes):
    nb, s, _ = toks[0].shape
    nt, nm, npar, nc = len(toks), int(mod is not None), len(params), len(cots)

    def body(*refs):
        n_in = nt + nm + npar + nc
        ins, outs = refs[:n_in], refs[n_in:]
        b, t = pl.program_id(0), pl.program_id(1)
        tv = [r[...].astype(F32) for r in ins[:nt]]
        mv = [ins[nt][k:k + 1, :] for k in range(N_MOD)] if nm else None
        pv = [r[...] for r in ins[nt + nm: nt + nm + npar]]
        cv = [r[...].astype(F32) for r in ins[nt + nm + npar:]]
        _, vjp = jax.vjp(f, tv, mv, pv)
        dtv, dmv, dpv = vjp(cv)
        for o, r in zip(outs[:nt], dtv):
            o[...] = r.astype(o.dtype)
        if nm:
            dm_ref = outs[nt]

            @pl.when(t <= 1)
            def _():
                for k in range(N_MOD):
                    dm_ref[k:k + 1, :] = dmv[k]

            @pl.when(t > 1)
            def _():
                for k in range(N_MOD):
                    dm_ref[k:k + 1, :] += dmv[k]

        first = jnp.logical_and(b == 0, t == 0)
        for o, r in zip(outs[nt + nm:], dpv):
            @pl.when(first)
            def _(o=o, r=r):
                o[...] = r

            @pl.when(jnp.logical_not(first))
            def _(o=o, r=r):
                o[...] += r

    tok_spec = lambda w: pl.BlockSpec((None, TOK_BLOCK, w), lambda b, t: (b, t, 0))
    in_specs = [tok_spec(t.shape[2]) for t in toks]
    if nm:
        in_specs.append(pl.BlockSpec((None, N_MOD, mod.shape[2]), _mod_index))
    in_specs += [pl.BlockSpec(p.shape, lambda b, t, nd=p.ndim: (0,) * nd) for p in params]
    in_specs += [tok_spec(c.shape[2]) for c in cots]
    out_specs = [tok_spec(t.shape[2]) for t in toks]
    out_shape = [jax.ShapeDtypeStruct(t.shape, dt) for t, dt in zip(toks, dtok_dtypes)]
    if nm:
        out_specs.append(pl.BlockSpec((None, N_MOD, mod.shape[2]), _mod_index))
        out_shape.append(jax.ShapeDtypeStruct(mod.shape, F32))
    out_specs += [pl.BlockSpec(p.shape, lambda b, t, nd=p.ndim: (0,) * nd) for p in params]
    out_shape += [jax.ShapeDtypeStruct(p.shape, F32) for p in params]
    res = pl.pallas_call(
        body,
        name=name,
        grid=(nb, s // TOK_BLOCK),
        in_specs=in_specs,
        out_specs=out_specs,
        out_shape=out_shape,
        compiler_params=_cparams(("arbitrary", "arbitrary")),
    )(*toks, *([mod] if nm else []), *params, *cots)
    return res[:nt], (res[nt] if nm else None), res[nt + nm:]


def _rms(x, g):
    return x * lax.rsqrt(jnp.mean(x * x, axis=-1, keepdims=True) + RMS_EPS) * g


def _silu(x):
    return x * jax.nn.sigmoid(x)


def f_norm(shift_row, scale_row):
    def f(tv, mv, pv):
        return [_rms(tv[0], pv[0]) * (1.0 + mv[scale_row]) + mv[shift_row]]
    return f


def f_resnorm(gate_row, shift_row, scale_row):
    def f(tv, mv, pv):
        s = tv[0] + mv[gate_row] * tv[1]
        return [s, _rms(s, pv[0]) * (1.0 + mv[scale_row]) + mv[shift_row]]
    return f


_ANY = pl.BlockSpec(memory_space=pl.ANY)
_MESH = pl.DeviceIdType.MESH


def _my_pos():
    return lax.axis_index("x"), lax.axis_index("y"), lax.axis_index("c")


def _slot_of(pos):
    return 4 * pos[0] + 2 * pos[1] + pos[2]


def all_gather(xs, name):
    n = len(xs)

    def body(*refs):
        x_refs, out_refs = refs[:n], refs[n:2 * n]
        send_sems, recv_sems, local_sems = refs[2 * n:]
        px, py, pc = _my_pos()
        me, sibling = (px, py, pc), (px, py, 1 - pc)
        chips = [(1 - px, py), (px, 1 - py), (1 - px, 1 - py)]

        def copy(a, k, block, to, from_input=False):
            slot = out_refs[a].at[_slot_of(block)]
            return pltpu.make_async_remote_copy(
                src_ref=x_refs[a] if from_input else slot, dst_ref=slot,
                send_sem=send_sems.at[a, k], recv_sem=recv_sems.at[a, k],
                device_id=to, device_id_type=_MESH)

        mine = [pltpu.make_async_copy(x_refs[a], out_refs[a].at[_slot_of(me)], local_sems.at[a]) for a in range(n)]
        for cp in mine:
            cp.start()
        first = [copy(a, 0, me, sibling, True) for a in range(n)]
        first += [copy(a, 1 + j, me, (*chip, pc), True) for j, chip in enumerate(chips) for a in range(n)]
        for cp in first:
            cp.start()
        passed = []
        for j, chip in enumerate(chips):
            for a in range(n):
                copy(a, 1 + j, (*chip, pc), me).wait_recv()
                passed.append(copy(a, 4 + j, (*chip, pc), sibling))
                passed[-1].start()
        for a in range(n):
            copy(a, 0, sibling, me).wait_recv()
        for j, chip in enumerate(chips):
            for a in range(n):
                copy(a, 4 + j, (*chip, 1 - pc), me).wait_recv()
        for cp in first + passed:
            cp.wait_send()
        for cp in mine:
            cp.wait()

    return pl.pallas_call(
        body,
        name=name,
        out_shape=[jax.ShapeDtypeStruct((N_DEV,) + x.shape, x.dtype) for x in xs],
        in_specs=[_ANY] * n,
        out_specs=[_ANY] * n,
        scratch_shapes=[pltpu.SemaphoreType.DMA((n, 7)), pltpu.SemaphoreType.DMA((n, 7)), pltpu.SemaphoreType.DMA((n,))],
    )(*xs)


def all_to_all(xs, name):
    n = len(xs)

    def body(*refs):
        x_refs, out_refs = refs[:n], refs[n:2 * n]
        send_sems, recv_sems, local_sems = refs[2 * n:]
        px, py, pc = _my_pos()
        me = (px, py, pc)

        def flipped(k):
            kx, ky, kc = (k >> 2) & 1, (k >> 1) & 1, k & 1
            return (1 - px if kx else px, 1 - py if ky else py, 1 - pc if kc else pc)

        def copy(a, k):
            peer = flipped(k)
            return pltpu.make_async_remote_copy(
                src_ref=x_refs[a].at[_slot_of(peer)], dst_ref=out_refs[a].at[_slot_of(me)],
                send_sem=send_sems.at[a, k - 1], recv_sem=recv_sems.at[a, k - 1],
                device_id=peer, device_id_type=_MESH)

        def landing(a, k):
            peer = flipped(k)
            return pltpu.make_async_remote_copy(
                src_ref=x_refs[a].at[_slot_of(me)], dst_ref=out_refs[a].at[_slot_of(peer)],
                send_sem=send_sems.at[a, k - 1], recv_sem=recv_sems.at[a, k - 1],
                device_id=peer, device_id_type=_MESH)

        mine = [pltpu.make_async_copy(x_refs[a].at[_slot_of(me)], out_refs[a].at[_slot_of(me)], local_sems.at[a]) for a in range(n)]
        for cp in mine:
            cp.start()
        copies = [copy(a, k) for a in range(n) for k in range(1, N_DEV)]
        for cp in copies:
            cp.start()
        for a in range(n):
            for k in range(1, N_DEV):
                landing(a, k).wait_recv()
        for cp in copies:
            cp.wait_send()
        for cp in mine:
            cp.wait()

    return pl.pallas_call(
        body,
        name=name,
        out_shape=[jax.ShapeDtypeStruct(x.shape, x.dtype) for x in xs],
        in_specs=[_ANY] * n,
        out_specs=[_ANY] * n,
        scratch_shapes=[pltpu.SemaphoreType.DMA((n, 7)), pltpu.SemaphoreType.DMA((n, 7)), pltpu.SemaphoreType.DMA((n,))],
    )(*xs)


N_CHIPS = 4


def sibling_swap(xs, name):
    n = len(xs)

    def body(*refs):
        x_refs, out_refs = refs[:n], refs[n:2 * n]
        send_sems, recv_sems = refs[2 * n:]
        px, py, pc = _my_pos()
        copies = [pltpu.make_async_remote_copy(
            src_ref=x_refs[a].at[:, 1 - pc], dst_ref=out_refs[a],
            send_sem=send_sems.at[a], recv_sem=recv_sems.at[a],
            device_id=(px, py, 1 - pc), device_id_type=_MESH) for a in range(n)]
        for cp in copies:
            cp.start()
        for cp in copies:
            cp.wait()

    return pl.pallas_call(
        body,
        name=name,
        out_shape=[jax.ShapeDtypeStruct(x.shape[:1] + x.shape[2:], x.dtype) for x in xs],
        in_specs=[_ANY] * n,
        out_specs=[_ANY] * n,
        scratch_shapes=[pltpu.SemaphoreType.DMA((n,)), pltpu.SemaphoreType.DMA((n,))],
    )(*xs)


def chip_exchange(xs, name):
    n = len(xs)

    def body(*refs):
        x_refs, out_refs = refs[:n], refs[n:2 * n]
        send_sems, recv_sems, local_sems = refs[2 * n:]
        px, py, pc = _my_pos()
        my_chip = 2 * px + py

        def peer(k):
            return (1 - px if k & 2 else px, 1 - py if k & 1 else py)

        def copy(a, k, landing):
            qx, qy = peer(k)
            src, dst = (my_chip, 2 * qx + qy) if landing else (2 * qx + qy, my_chip)
            return pltpu.make_async_remote_copy(
                src_ref=x_refs[a].at[src], dst_ref=out_refs[a].at[dst],
                send_sem=send_sems.at[a, k - 1], recv_sem=recv_sems.at[a, k - 1],
                device_id=(qx, qy, pc), device_id_type=_MESH)

        mine = [pltpu.make_async_copy(x_refs[a].at[my_chip], out_refs[a].at[my_chip], local_sems.at[a]) for a in range(n)]
        for cp in mine:
            cp.start()
        copies = [copy(a, k, False) for a in range(n) for k in range(1, N_CHIPS)]
        for cp in copies:
            cp.start()
        for a in range(n):
            for k in range(1, N_CHIPS):
                copy(a, k, True).wait_recv()
        for cp in copies:
            cp.wait_send()
        for cp in mine:
            cp.wait()

    return pl.pallas_call(
        body,
        name=name,
        out_shape=[jax.ShapeDtypeStruct(x.shape, x.dtype) for x in xs],
        in_specs=[_ANY] * n,
        out_specs=[_ANY] * n,
        scratch_shapes=[pltpu.SemaphoreType.DMA((n, N_CHIPS - 1)), pltpu.SemaphoreType.DMA((n, N_CHIPS - 1)),
                        pltpu.SemaphoreType.DMA((n,))],
    )(*xs)


def pair_sum(own, got, name):
    nch, _, r, c = own.shape
    tr = _pick(r, (512, 256, 128, 64, 32, 16))

    def body(own_ref, got_ref, o_ref):
        pc = lax.axis_index("c")
        o_ref[...] = (own_ref[pc].astype(F32) + got_ref[...].astype(F32)).astype(o_ref.dtype)

    return pl.pallas_call(
        body,
        name=name,
        grid=(nch, r // tr),
        in_specs=[pl.BlockSpec((None, 2, tr, c), lambda i, j: (i, 0, j, 0)), pl.BlockSpec((None, tr, c), lambda i, j: (i, j, 0))],
        out_specs=pl.BlockSpec((None, tr, c), lambda i, j: (i, j, 0)),
        out_shape=jax.ShapeDtypeStruct((nch, r, c), own.dtype),
        compiler_params=_cparams(("parallel", "parallel")),
    )(own, got)


def sum_slots(x, name):
    n, r, c = x.shape
    tr = _pick(r, (512, 256, 128, 64, 32, 16, 8))

    def body(x_ref, o_ref):
        acc = x_ref[0].astype(F32)
        for i in range(1, n):
            acc = acc + x_ref[i].astype(F32)
        o_ref[...] = acc

    return pl.pallas_call(
        body,
        name=name,
        grid=(r // tr,),
        in_specs=[pl.BlockSpec((n, tr, c), lambda i: (0, i, 0))],
        out_specs=pl.BlockSpec((tr, c), lambda i: (i, 0)),
        out_shape=jax.ShapeDtypeStruct((r, c), F32),
        compiler_params=_cparams(("parallel",)),
    )(x)


CONV_CH_TILE = 256


def _shift_rows(x, off):
    n = x.shape[0]
    if off % n == 0:
        return x
    return pltpu.roll(x, (-off) % n, axis=0)


def _between(v, lo, hi):
    return jnp.where(v >= lo, 1.0, 0.0) * jnp.where(v < hi, 1.0, 0.0)


def taps_1d(ntaps, s_ctx, s_tot):
    def mask(off):
        def m(t):
            is_ctx = _between(t, 0, s_ctx)
            return is_ctx * _between(t + off, 0, s_ctx) + (1.0 - is_ctx) * _between(t + off, s_ctx, s_tot)
        return m
    return [(j - (ntaps - 1) // 2, mask(j - (ntaps - 1) // 2)) for j in range(ntaps)]


def taps_grid(s_ctx, s_tot, grid_w):
    assert s_ctx % grid_w == 0

    return ("grid", s_ctx, s_tot, grid_w)


def _grid_masks(taps, s):
    _, s_ctx, s_tot, grid_w = taps
    t = lax.broadcasted_iota(jnp.int32, (s, 1), 0)
    is_ctx = _between(t, 0, s_ctx)
    mcol = {dc: is_ctx * _between(t + dc, 0, s_ctx) + (1.0 - is_ctx) * _between(t % grid_w + dc, 0, grid_w) for dc in (-1, 1)}
    mrow = {dr: (1.0 - is_ctx) * _between(t + grid_w * dr, s_ctx, s_tot) for dr in (-1, 1)}
    return mcol, mrow


def _grid_cols(x, mcol):
    return {-1: _shift_rows(x, -1) * mcol[-1], 0: x, 1: _shift_rows(x, 1) * mcol[1]}


def _conv_acc(x, w_ref, b_ref, taps, s):
    acc = jnp.broadcast_to(b_ref[...], x.shape)
    if taps[0] == "grid":
        grid_w = taps[3]
        mcol, mrow = _grid_masks(taps, s)
        xc = _grid_cols(x, mcol)
        for a, dr in enumerate((-1, 0, 1)):
            r = sum(w_ref[3 * a + b:3 * a + b + 1, :] * xc[dc] for b, dc in enumerate((-1, 0, 1)))
            acc = acc + (r if dr == 0 else _shift_rows(r, grid_w * dr) * mrow[dr])
        return acc
    t = lax.broadcasted_iota(jnp.int32, (s, 1), 0)
    for k, (off, m) in enumerate(taps):
        acc = acc + w_ref[k:k + 1, :] * (_shift_rows(x, off) * m(t))
    return acc


def _conv_adjoint(x, dacc, w_ref, taps, s):
    if taps[0] == "grid":
        grid_w = taps[3]
        mcol, mrow = _grid_masks(taps, s)
        xc = _grid_cols(x, mcol)
        dxc = {dc: 0.0 for dc in (-1, 0, 1)}
        dws = []
        for a, dr in enumerate((-1, 0, 1)):
            d_r = dacc if dr == 0 else _shift_rows(dacc * mrow[dr], -grid_w * dr)
            for b, dc in enumerate((-1, 0, 1)):
                dxc[dc] = dxc[dc] + w_ref[3 * a + b:3 * a + b + 1, :] * d_r
                dws.append(jnp.sum(d_r * xc[dc], axis=0, keepdims=True))
        dx = dxc[0] + _shift_rows(dxc[-1] * mcol[-1], 1) + _shift_rows(dxc[1] * mcol[1], -1)
        return dx, dws
    t = lax.broadcasted_iota(jnp.int32, (s, 1), 0)
    dx = jnp.zeros_like(x)
    dws = []
    for k, (off, m) in enumerate(taps):
        dm = dacc * m(t)
        dx = dx + _shift_rows(w_ref[k:k + 1, :] * dm, -off)
        dws.append(jnp.sum(dm * _shift_rows(x, off), axis=0, keepdims=True))
    return dx, dws


def conv_fwd(name, x, w, b, taps, mode, mul=None, out_dtype=F32):
    nb, s, c = x.shape
    ct = _pick(c, (CONV_CH_TILE, 128))
    has_mul = mode == "silu_mul"

    def body(*refs):
        x_ref, w_ref, b_ref = refs[:3]
        o_ref = refs[-1]
        acc = _conv_acc(x_ref[...], w_ref, b_ref, taps, s)
        if mode == "none":
            out = acc
        else:
            out = _silu(acc)
            if has_mul:
                out = out * refs[3][...].astype(F32)
        o_ref[...] = out.astype(o_ref.dtype)

    blk = pl.BlockSpec((None, s, ct), lambda bb, j: (bb, 0, j))
    par = lambda k: pl.BlockSpec((k, ct), lambda bb, j: (0, j))
    return pl.pallas_call(
        body,
        name=name,
        grid=(nb, c // ct),
        in_specs=[blk, par(w.shape[0]), par(1)] + ([blk] if has_mul else []),
        out_specs=blk,
        out_shape=jax.ShapeDtypeStruct(x.shape, out_dtype),
        compiler_params=_cparams(("parallel", "parallel")),
    )(x, w, b, *([mul] if has_mul else []))


def conv_bwd(name, x, w, b, dout, taps, mode, mul=None, dx_dtype=F32):
    nb, s, c = x.shape
    ct = _pick(c, (CONV_CH_TILE, 128))
    has_mul = mode == "silu_mul"
    nk = w.shape[0]

    def body(*refs):
        x_ref, w_ref, b_ref, do_ref = refs[:4]
        n_in = 5 if has_mul else 4
        dx_ref, dw_ref, db_ref = refs[n_in:n_in + 3]
        bb = pl.program_id(1)
        x = x_ref[...]
        dacc = do_ref[...].astype(F32)
        if mode != "none":
            acc = _conv_acc(x, w_ref, b_ref, taps, s)
            sg = jax.nn.sigmoid(acc)
            if has_mul:
                refs[n_in + 3][...] = (dacc * (acc * sg)).astype(refs[n_in + 3].dtype)
                dacc = dacc * refs[4][...].astype(F32)
            dacc = dacc * (sg * (1.0 + acc * (1.0 - sg)))
        dx, dws = _conv_adjoint(x, dacc, w_ref, taps, s)
        dx_ref[...] = dx.astype(dx_ref.dtype)
        db = jnp.sum(dacc, axis=0, keepdims=True)

        @pl.when(bb == 0)
        def _():
            for k in range(nk):
                dw_ref[k:k + 1, :] = dws[k]
            db_ref[...] = db

        @pl.when(bb > 0)
        def _():
            for k in range(nk):
                dw_ref[k:k + 1, :] += dws[k]
            db_ref[...] += db

    blk = pl.BlockSpec((None, s, ct), lambda j, bb: (bb, 0, j))
    par = lambda k: pl.BlockSpec((k, ct), lambda j, bb: (0, j))
    out_specs = [blk, par(nk), par(1)] + ([blk] if has_mul else [])
    out_shape = [jax.ShapeDtypeStruct(x.shape, dx_dtype), jax.ShapeDtypeStruct(w.shape, F32), jax.ShapeDtypeStruct(b.shape, F32)]
    if has_mul:
        out_shape.append(jax.ShapeDtypeStruct(x.shape, dx_dtype))
    return pl.pallas_call(
        body,
        name=name,
        grid=(c // ct, nb),
        in_specs=[blk, par(nk), par(1), blk] + ([blk] if has_mul else []),
        out_specs=out_specs,
        out_shape=out_shape,
        compiler_params=_cparams(("parallel", "arbitrary")),
    )(x, w, b, dout, *([mul] if has_mul else []))


SCAN_UNROLL = 4


def _scan_order(direction, adjoint, s_ctx, s_tot):
    nc, nt = s_ctx // SUBLANES, s_tot // SUBLANES
    if direction == 0:
        return ([(0, nt, 1)], False) if not adjoint else ([(nt - 1, nt, -1)], True)
    if not adjoint:
        return [(nc - 1, nc, -1), (nt - 1, nt - nc, -1)], True
    return [(nc, nt - nc, 1), (0, nc, 1)], False


def _last_row(h, descending):
    row = lax.broadcasted_iota(jnp.int32, h.shape, 0)
    pick = 0 if descending else SUBLANES - 1
    return jnp.sum(jnp.where(row == pick, h, 0.0), axis=0, keepdims=True)


def _prev_rows(h, carry, descending):
    row = lax.broadcasted_iota(jnp.int32, h.shape, 0)
    if descending:
        return jnp.where(row == SUBLANES - 1, carry, pltpu.roll(h, SUBLANES - 1, axis=0))
    return jnp.where(row == 0, carry, pltpu.roll(h, 1, axis=0))


def _scan_real(a_ref, x_ref, h_ref, hp_ref, order):
    ranges, descending = order
    n_rows, width = a_ref.shape
    n_tiles = n_rows // SUBLANES
    row = lax.broadcasted_iota(jnp.int32, (SUBLANES, width), 0)
    unroll = lambda count: SCAN_UNROLL if count % SCAN_UNROLL == 0 else 1

    def run(ac_ref):
        def in_tile(i, _):
            t0 = pl.multiple_of(i * SUBLANES, SUBLANES)
            a = a_ref[pl.ds(t0, SUBLANES), :]
            x = x_ref[pl.ds(t0, SUBLANES), :]
            for k in (1, 2, 4):
                sh = SUBLANES - k if descending else k
                keep = (row < SUBLANES - k) if descending else (row >= k)
                x = jnp.where(keep, a * pltpu.roll(x, sh, axis=0) + x, x)
                a = jnp.where(keep, a * pltpu.roll(a, sh, axis=0), a)
            ac_ref[pl.ds(t0, SUBLANES), :] = a
            x_ref[pl.ds(t0, SUBLANES), :] = x
            return 0

        lax.fori_loop(0, n_tiles, in_tile, 0, unroll=unroll(n_tiles))

        def tile(i, carry):
            t0 = pl.multiple_of(i * SUBLANES, SUBLANES)
            a = ac_ref[pl.ds(t0, SUBLANES), :]
            x = x_ref[pl.ds(t0, SUBLANES), :]
            h = a * carry + x
            if h_ref is not None:
                h_ref[pl.ds(t0, SUBLANES), :] = h
            if hp_ref is not None:
                hp_ref[pl.ds(t0, SUBLANES), :] = _prev_rows(h, carry, descending)
            return _last_row(a, descending) * carry + _last_row(x, descending)

        carry = jnp.zeros((1, width), F32)
        for first, count, step in ranges:
            carry = lax.fori_loop(0, count, lambda j, c, first=first, step=step: tile(first + step * j, c), carry,
                                  unroll=unroll(count))

    pl.run_scoped(run, pltpu.VMEM((n_rows, width), F32))


def _cmul(ar, ai, br, bi):
    return ar * br - ai * bi, ar * bi + ai * br


def _scan_cplx(lr, li, xr_ref, xi_ref, hpr_ref, hpi_ref, order):
    ranges, descending = order
    width = xr_ref.shape[1]
    row = lax.broadcasted_iota(jnp.int32, (SUBLANES, width), 0)
    pw = [(lr, li)]
    for _ in range(SUBLANES - 1):
        pw.append(_cmul(pw[-1][0], pw[-1][1], lr, li))
    pr = jnp.zeros((SUBLANES, width), F32)
    pi = jnp.zeros((SUBLANES, width), F32)
    for r in range(SUBLANES):
        n = SUBLANES - 1 - r if descending else r
        pr = jnp.where(row == r, pw[n][0], pr)
        pi = jnp.where(row == r, pw[n][1], pi)

    n_tiles = xr_ref.shape[0] // SUBLANES
    unroll = lambda count: SCAN_UNROLL if count % SCAN_UNROLL == 0 else 1

    def in_tile(i, _):
        t0 = pl.multiple_of(i * SUBLANES, SUBLANES)
        xr = xr_ref[pl.ds(t0, SUBLANES), :]
        xi = xi_ref[pl.ds(t0, SUBLANES), :]
        for k in (1, 2, 4):
            sh = SUBLANES - k if descending else k
            keep = (row < SUBLANES - k) if descending else (row >= k)
            sr, si = _cmul(pw[k - 1][0], pw[k - 1][1], pltpu.roll(xr, sh, axis=0), pltpu.roll(xi, sh, axis=0))
            xr = jnp.where(keep, xr + sr, xr)
            xi = jnp.where(keep, xi + si, xi)
        xr_ref[pl.ds(t0, SUBLANES), :] = xr
        xi_ref[pl.ds(t0, SUBLANES), :] = xi
        return 0

    lax.fori_loop(0, n_tiles, in_tile, 0, unroll=unroll(n_tiles))
    lam8 = pw[SUBLANES - 1]

    def tile(i, carry):
        cr, ci = carry
        t0 = pl.multiple_of(i * SUBLANES, SUBLANES)
        xr = xr_ref[pl.ds(t0, SUBLANES), :]
        xi = xi_ref[pl.ds(t0, SUBLANES), :]
        hr, hi = _cmul(pr, pi, cr, ci)
        hr, hi = hr + xr, hi + xi
        xr_ref[pl.ds(t0, SUBLANES), :] = hr
        xi_ref[pl.ds(t0, SUBLANES), :] = hi
        if hpr_ref is not None:
            hpr_ref[pl.ds(t0, SUBLANES), :] = _prev_rows(hr, cr, descending)
            hpi_ref[pl.ds(t0, SUBLANES), :] = _prev_rows(hi, ci, descending)
        nr, ni = _cmul(lam8[0], lam8[1], cr, ci)
        return nr + _last_row(xr, descending), ni + _last_row(xi, descending)

    carry = (jnp.zeros((1, width), F32), jnp.zeros((1, width), F32))
    for first, count, step in ranges:
        carry = lax.fori_loop(0, count, lambda j, c, first=first, step=step: tile(first + step * j, c), carry,
                              unroll=unroll(count))


def _log1p_pos(y):
    return jnp.where(y < 0.01, y * (1.0 - y * (0.5 - y * (1.0 / 3.0 - 0.25 * y))), jnp.log(1.0 + y))


def _softplus(x):
    return jnp.maximum(x, 0.0) + _log1p_pos(jnp.exp(-jnp.abs(x)))


def _neg_expm1(z):
    series = -z * (1.0 + z * (0.5 + z * (1.0 / 6.0 + z * (1.0 / 24.0 + z * (1.0 / 120.0)))))
    return jnp.where(z > -0.1, series, 1.0 - jnp.exp(z))


def _lru_gates(u, w_a, b_a, w_i, b_i, lam):
    ub = u.astype(BF16)
    r = jax.nn.sigmoid(jnp.dot(ub, w_a.astype(BF16), preferred_element_type=F32) + b_a)
    i = jax.nn.sigmoid(jnp.dot(ub, w_i.astype(BF16), preferred_element_type=F32) + b_i)
    log_a = (-LRU_C) * _softplus(-lam) * r
    return jnp.exp(log_a), jnp.sqrt(_neg_expm1(2.0 * log_a)) * (i * u)


LRU_PER_STEP = 4
LRU_PER_STEP_BWD = 2


def _lru_specs(per, bw, order):
    w = pl.BlockSpec((2, per, bw, bw), lambda *g: (0, order(*g), 0, 0))
    v = pl.BlockSpec((2, per, 1, bw), lambda *g: (0, order(*g), 0, 0))
    return [w, v, w, v, v]


def lru_fwd(name, u, w_a, b_a, w_i, b_i, lam, s_ctx):
    nb, s, _ = u.shape
    nblk, bw = w_a.shape[1], w_a.shape[2]
    per = min(LRU_PER_STEP, nblk)

    def body(u_ref, wa, ba, wi, bi, lm, o_ref, a_s, x_s, h_s):
        for d in (0, 1):
            for k in range(per):
                cols = slice(k * bw, (k + 1) * bw)
                a, bx = _lru_gates(u_ref[:, cols], wa[d, k], ba[d, k], wi[d, k], bi[d, k], lm[d, k])
                a_s[:, cols] = a
                x_s[:, cols] = bx
            _scan_real(a_s, x_s, h_s, None, _scan_order(d, False, s_ctx, s))
            if d == 0:
                o_ref[...] = h_s[...]
            else:
                o_ref[...] += h_s[...]

    blk = pl.BlockSpec((None, s, per * bw), lambda b, n: (b, 0, n))
    return pl.pallas_call(
        body,
        name=name,
        grid=(nb, nblk // per),
        in_specs=[blk] + _lru_specs(per, bw, lambda b, n: n),
        out_specs=blk,
        out_shape=jax.ShapeDtypeStruct(u.shape, F32),
        scratch_shapes=[pltpu.VMEM((s, per * bw), F32)] * 3,
        compiler_params=_cparams(("parallel", "parallel")),
    )(u, w_a, b_a, w_i, b_i, lam)


def lru_bwd(name, u, w_a, b_a, w_i, b_i, lam, dh, s_ctx):
    nb, s, _ = u.shape
    nblk, bw = w_a.shape[1], w_a.shape[2]
    per = min(LRU_PER_STEP_BWD, nblk)

    def body(u_ref, wa, ba, wi, bi, lm, dh_ref, du_ref, dwa, dba, dwi, dbi, dlm, a_s, x_s, hp_s, wp_s):
        b = pl.program_id(1)
        for d in (0, 1):
            for k in range(per):
                cols = slice(k * bw, (k + 1) * bw)
                a, bx = _lru_gates(u_ref[:, cols], wa[d, k], ba[d, k], wi[d, k], bi[d, k], lm[d, k])
                a_s[:, cols] = a
                x_s[:, cols] = bx
            _scan_real(a_s, x_s, None, hp_s, _scan_order(d, False, s_ctx, s))
            x_s[...] = a_s[...] * dh_ref[...]
            _scan_real(a_s, x_s, None, wp_s, _scan_order(d, True, s_ctx, s))
            for k in range(per):
                cols = slice(k * bw, (k + 1) * bw)
                g = dh_ref[:, cols] + wp_s[:, cols]
                _, vjp = jax.vjp(_lru_gates, u_ref[:, cols], wa[d, k], ba[d, k], wi[d, k], bi[d, k], lm[d, k])
                grads = vjp((g * hp_s[:, cols], g))
                if d == 0:
                    du_ref[:, cols] = grads[0]
                else:
                    du_ref[:, cols] += grads[0]
                for ref, val in zip((dwa, dba, dwi, dbi, dlm), grads[1:]):
                    @pl.when(b == 0)
                    def _(ref=ref, val=val, k=k):
                        ref[d, k] = val

                    @pl.when(b > 0)
                    def _(ref=ref, val=val, k=k):
                        ref[d, k] += val

    blk = pl.BlockSpec((None, s, per * bw), lambda n, b: (b, 0, n))
    pspecs = _lru_specs(per, bw, lambda n, b: n)
    return pl.pallas_call(
        body,
        name=name,
        grid=(nblk // per, nb),
        in_specs=[blk] + pspecs + [blk],
        out_specs=[blk] + pspecs,
        out_shape=[jax.ShapeDtypeStruct(u.shape, F32)] + [jax.ShapeDtypeStruct(p.shape, F32) for p in (w_a, b_a, w_i, b_i, lam)],
        scratch_shapes=[pltpu.VMEM((s, per * bw), F32)] * 4,
        compiler_params=_cparams(("parallel", "arbitrary")),
    )(u, w_a, b_a, w_i, b_i, lam, dh)


S5_TILE_CH = 128
S5_TILE_STATES = S5_TILE_CH // S5_GROUP_CH * S5_STATE


def _dot_nt(a, b):
    return lax.dot_general(a, b, (((1,), (1,)), ((), ())), preferred_element_type=F32)


def _dot_tn(a, b):
    return lax.dot_general(a, b, (((0,), (0,)), ((), ())), preferred_element_type=F32)


def _s5_specs(order):
    lam = pl.BlockSpec((2, 1, S5_TILE_STATES), lambda *g: (0, 0, order(*g)))
    mat = pl.BlockSpec((2, None, S5_TILE_STATES, S5_TILE_CH), lambda *g: (0, order(*g), 0, 0))
    return [lam, lam, mat, mat, mat, mat]


def s5_fwd(name, u, lam_r, lam_i, bt_r, bt_i, ct_r, ct_i, s_ctx):
    nb, s, w = u.shape

    def body(u_ref, lr, li, btr, bti, ctr, cti, o_ref, xr_s, xi_s):
        ub = u_ref[...].astype(BF16)
        for d in (0, 1):
            xr_s[...] = _dot_nt(ub, btr[d].astype(BF16))
            xi_s[...] = _dot_nt(ub, bti[d].astype(BF16))
            _scan_cplx(lr[d], li[d], xr_s, xi_s, None, None, _scan_order(d, False, s_ctx, s))
            y = (jnp.dot(xr_s[...].astype(BF16), ctr[d].astype(BF16), preferred_element_type=F32)
                 - jnp.dot(xi_s[...].astype(BF16), cti[d].astype(BF16), preferred_element_type=F32))
            if d == 0:
                o_ref[...] = y
            else:
                o_ref[...] += y

    blk = pl.BlockSpec((None, s, S5_TILE_CH), lambda b, j: (b, 0, j))
    return pl.pallas_call(
        body,
        name=name,
        grid=(nb, w // S5_TILE_CH),
        in_specs=[blk] + _s5_specs(lambda b, j: j),
        out_specs=blk,
        out_shape=jax.ShapeDtypeStruct(u.shape, F32),
        scratch_shapes=[pltpu.VMEM((s, S5_TILE_STATES), F32)] * 2,
        compiler_params=_cparams(("parallel", "parallel")),
    )(u, lam_r, lam_i, bt_r, bt_i, ct_r, ct_i)


def s5_bwd(name, u, lam_r, lam_i, bt_r, bt_i, ct_r, ct_i, dy, s_ctx):
    nb, s, w = u.shape

    def body(u_ref, lr, li, btr, bti, ctr, cti, dy_ref, du_ref, dlr, dli, dbtr, dbti, dctr, dcti,
             hr_s, hi_s, hpr_s, hpi_s, gr_s, gi_s):
        b = pl.program_id(1)
        ub = u_ref[...].astype(BF16)
        dyb = dy_ref[...].astype(BF16)
        du = jnp.zeros((s, S5_TILE_CH), F32)
        for d in (0, 1):
            hr_s[...] = _dot_nt(ub, btr[d].astype(BF16))
            hi_s[...] = _dot_nt(ub, bti[d].astype(BF16))
            _scan_cplx(lr[d], li[d], hr_s, hi_s, hpr_s, hpi_s, _scan_order(d, False, s_ctx, s))
            d_ctr = _dot_tn(hr_s[...].astype(BF16), dyb)
            d_cti = -_dot_tn(hi_s[...].astype(BF16), dyb)
            gr_s[...] = _dot_nt(dyb, ctr[d].astype(BF16))
            gi_s[...] = -_dot_nt(dyb, cti[d].astype(BF16))
            _scan_cplx(lr[d], -li[d], gr_s, gi_s, None, None, _scan_order(d, True, s_ctx, s))
            gr, gi = gr_s[...], gi_s[...]
            hpr, hpi = hpr_s[...], hpi_s[...]
            d_lr = jnp.sum(gr * hpr + gi * hpi, axis=0, keepdims=True)
            d_li = jnp.sum(gi * hpr - gr * hpi, axis=0, keepdims=True)
            grb, gib = gr.astype(BF16), gi.astype(BF16)
            du = du + jnp.dot(grb, btr[d].astype(BF16), preferred_element_type=F32)
            du = du + jnp.dot(gib, bti[d].astype(BF16), preferred_element_type=F32)
            d_btr = _dot_tn(grb, ub)
            d_bti = _dot_tn(gib, ub)
            for ref, val in zip((dlr, dli, dbtr, dbti, dctr, dcti), (d_lr, d_li, d_btr, d_bti, d_ctr, d_cti)):
                @pl.when(b == 0)
                def _(ref=ref, val=val):
                    ref[d] = val

                @pl.when(b > 0)
                def _(ref=ref, val=val):
                    ref[d] += val
        du_ref[...] = du

    blk = pl.BlockSpec((None, s, S5_TILE_CH), lambda j, b: (b, 0, j))
    pspecs = _s5_specs(lambda j, b: j)
    params = (lam_r, lam_i, bt_r, bt_i, ct_r, ct_i)
    return pl.pallas_call(
        body,
        name=name,
        grid=(w // S5_TILE_CH, nb),
        in_specs=[blk] + pspecs + [blk],
        out_specs=[blk] + pspecs,
        out_shape=[jax.ShapeDtypeStruct(u.shape, F32)] + [jax.ShapeDtypeStruct(p.shape, F32) for p in params],
        scratch_shapes=[pltpu.VMEM((s, S5_TILE_STATES), F32)] * 6,
        compiler_params=_cparams(("parallel", "arbitrary")),
    )(u, lam_r, lam_i, bt_r, bt_i, ct_r, ct_i, dy)


def small_fwd(name, f, ins, out_shapes):
    n = len(ins)

    def body(*refs):
        for o, r in zip(refs[n:], f([r[...] for r in refs[:n]])):
            o[...] = r

    return pl.pallas_call(
        body, name=name,
        out_shape=[jax.ShapeDtypeStruct(s, F32) for s in out_shapes],
        compiler_params=_cparams(),
    )(*ins)


def small_bwd(name, f, ins, cots):
    n, nc = len(ins), len(cots)

    def body(*refs):
        _, vjp = jax.vjp(f, [r[...] for r in refs[:n]])
        (grads,) = vjp([r[...] for r in refs[n:n + nc]])
        for o, r in zip(refs[n + nc:], grads):
            o[...] = r

    return pl.pallas_call(
        body, name=name,
        out_shape=[jax.ShapeDtypeStruct(a.shape, F32) for a in ins],
        compiler_params=_cparams(),
    )(*ins, *cots)


def _row(x, r):
    return jnp.sum(jnp.where(lax.broadcasted_iota(jnp.int32, x.shape, 0) == r, x, 0.0), axis=0, keepdims=True)


def _col(x, c):
    return jnp.sum(jnp.where(lax.broadcasted_iota(jnp.int32, x.shape, 1) == c, x, 0.0), axis=1, keepdims=True)


def _chunk_at(i, reverse, ncc, nc):
    if not reverse:
        return i
    return jnp.where(i < ncc, ncc - 1 - i, nc - 1 - (i - ncc))


def _tri(n, reverse):
    li = lax.broadcasted_iota(jnp.int32, (n, n), 0)
    si = lax.broadcasted_iota(jnp.int32, (n, n), 1)
    return jnp.where((li <= si) if reverse else (li >= si), 1.0, 0.0)


_HI = lax.Precision.HIGHEST


def _ssd_chunk(xs, bm, cm, dtc, dtr, a_row, a_col, hs, reverse):
    n = bm.shape[0]
    last = 0 if reverse else n - 1
    tri = _tri(n, reverse)
    cum_c = jnp.dot(tri, dtc * -jnp.exp(a_row), precision=_HI, preferred_element_type=F32)
    cum_r = lax.dot_general(dtr * -jnp.exp(a_col), tri, (((1,), (1,)), ((), ())), precision=_HI, preferred_element_type=F32)
    tot_r = _row(cum_c, last)
    bmb, cmb = bm.astype(BF16), cm.astype(BF16)
    cb = _dot_nt(cmb, bmb)
    ys, hn = [], []
    for hd in range(len(xs)):
        cl = _col(cum_c, hd)
        tot = _col(tot_r, hd)
        decay = jnp.exp(jnp.where(tri > 0.0, cl - _row(cum_r, hd), -jnp.inf))
        xd = xs[hd] * _col(dtc, hd)
        y = jnp.dot((cb * decay).astype(BF16), xd.astype(BF16), preferred_element_type=F32)
        y = y + _dot_nt(cmb, hs[hd].astype(BF16)) * jnp.exp(cl)
        hnew = hs[hd] * jnp.exp(tot) + _dot_tn((xd * jnp.exp(tot - cl)).astype(BF16), bmb)
        ys.append(y)
        hn.append(hnew)
    return ys, hn


def _ssd_specs(reverse, ncc, nc, order):
    ch = lambda *g: _chunk_at(order(*g)[1], reverse, ncc, nc)
    b_ = lambda *g: order(*g)[0]
    gn = SSD_GROUPS * SSD_STATE
    return [
        pl.BlockSpec((None, SSD_CHUNK, SSD_W), lambda *g: (b_(*g), ch(*g), 0)),
        pl.BlockSpec((None, SSD_CHUNK, gn), lambda *g: (b_(*g), ch(*g), SSD_W // gn)),
        pl.BlockSpec((None, SSD_CHUNK, gn), lambda *g: (b_(*g), ch(*g), SSD_W // gn + 1)),
        pl.BlockSpec((None, SSD_GROUPS, SSD_CHUNK, SSD_HPG), lambda *g: (b_(*g), 0, ch(*g), 0)),
        pl.BlockSpec((None, SSD_GROUPS, SSD_HPG, SSD_CHUNK), lambda *g: (b_(*g), 0, 0, ch(*g))),
        pl.BlockSpec((SSD_GROUPS, 1, SSD_HPG), lambda *g: (0, 0, 0)),
        pl.BlockSpec((SSD_GROUPS, SSD_HPG, 1), lambda *g: (0, 0, 0)),
    ]


def _ssd_group_inputs(g, x_ref, bm_ref, cm_ref, dtc_ref, dtr_ref, ar_ref, ac_ref):
    p, n = SSD_HEAD_DIM, SSD_STATE
    xs = [x_ref[:, p * (SSD_HPG * g + hd):p * (SSD_HPG * g + hd + 1)] for hd in range(SSD_HPG)]
    return xs, bm_ref[:, n * g:n * (g + 1)], cm_ref[:, n * g:n * (g + 1)], dtc_ref[g], dtr_ref[g], ar_ref[g], ac_ref[g]


def ssd_fwd(name, xbc, dt_col, dt_row, a_row, a_col, reverse, s_ctx):
    nb, s, _ = xbc.shape
    nc, ncc = s // SSD_CHUNK, s_ctx // SSD_CHUNK
    p = SSD_HEAD_DIM

    def body(x_ref, bm_ref, cm_ref, dtc_ref, dtr_ref, ar_ref, ac_ref, y_ref, hst_ref, h_s):
        i = pl.program_id(1)

        @pl.when(i == 0)
        def _():
            h_s[...] = jnp.zeros_like(h_s)

        hst_ref[...] = h_s[...]
        for g in range(SSD_GROUPS):
            xs, bm, cm, dtc, dtr, ar, ac = _ssd_group_inputs(g, x_ref, bm_ref, cm_ref, dtc_ref, dtr_ref, ar_ref, ac_ref)
            ys, hn = _ssd_chunk(xs, bm, cm, dtc, dtr, ar, ac, [h_s[g, hd] for hd in range(SSD_HPG)], reverse)
            for hd in range(SSD_HPG):
                y_ref[:, p * (SSD_HPG * g + hd):p * (SSD_HPG * g + hd + 1)] = ys[hd]
                h_s[g, hd] = hn[hd]

    state = (SSD_GROUPS, SSD_HPG, SSD_HEAD_DIM, SSD_STATE)
    return pl.pallas_call(
        body,
        name=name,
        grid=(nb, nc),
        in_specs=_ssd_specs(reverse, ncc, nc, lambda b, i: (b, i)),
        out_specs=[pl.BlockSpec((None, SSD_CHUNK, SSD_W), lambda b, i: (b, _chunk_at(i, reverse, ncc, nc), 0)),
                   pl.BlockSpec((None, None) + state, lambda b, i: (b, i, 0, 0, 0, 0))],
        out_shape=[jax.ShapeDtypeStruct((nb, s, SSD_W), F32), jax.ShapeDtypeStruct((nb, nc) + state, F32)],
        scratch_shapes=[pltpu.VMEM(state, F32)],
        compiler_params=_cparams(("parallel", "arbitrary")),
    )(xbc, xbc, xbc, dt_col, dt_row, a_row, a_col)


def ssd_bwd(name, xbc, dt_col, dt_row, a_row, a_col, hst, dy, reverse, s_ctx):
    nb, s, _ = xbc.shape
    nc, ncc = s // SSD_CHUNK, s_ctx // SSD_CHUNK
    p, n = SSD_HEAD_DIM, SSD_STATE

    def body(x_ref, bm_ref, cm_ref, dtc_ref, dtr_ref, ar_ref, ac_ref, hst_ref, dy_ref,
             dx_ref, dbm_ref, dcm_ref, ddtc_ref, ddtr_ref, dar_ref, dac_ref, dh_s):
        i = pl.program_id(1)

        @pl.when(i == 0)
        def _():
            dh_s[...] = jnp.zeros_like(dh_s)

        for g in range(SSD_GROUPS):
            xs, bm, cm, dtc, dtr, ar, ac = _ssd_group_inputs(g, x_ref, bm_ref, cm_ref, dtc_ref, dtr_ref, ar_ref, ac_ref)
            hs = [hst_ref[g, hd] for hd in range(SSD_HPG)]
            _, vjp = jax.vjp(functools.partial(_ssd_chunk, reverse=reverse), xs, bm, cm, dtc, dtr, ar, ac, hs)
            dys = [dy_ref[:, p * (SSD_HPG * g + hd):p * (SSD_HPG * g + hd + 1)] for hd in range(SSD_HPG)]
            dxs, dbm, dcm, ddtc, ddtr, dar, dac, dhs = vjp((dys, [dh_s[g, hd] for hd in range(SSD_HPG)]))
            for hd in range(SSD_HPG):
                dx_ref[:, p * (SSD_HPG * g + hd):p * (SSD_HPG * g + hd + 1)] = dxs[hd]
                dh_s[g, hd] = dhs[hd]
            dbm_ref[:, n * g:n * (g + 1)] = dbm
            dcm_ref[:, n * g:n * (g + 1)] = dcm
            ddtc_ref[g] = ddtc
            ddtr_ref[g] = ddtr

            @pl.when(i == 0)
            def _(g=g, dar=dar, dac=dac):
                dar_ref[g] = dar
                dac_ref[g] = dac

            @pl.when(i > 0)
            def _(g=g, dar=dar, dac=dac):
                dar_ref[g] += dar
                dac_ref[g] += dac

    ch = lambda b, i: _chunk_at(nc - 1 - i, reverse, ncc, nc)
    state = (SSD_GROUPS, SSD_HPG, SSD_HEAD_DIM, SSD_STATE)
    gn = SSD_GROUPS * SSD_STATE
    in_specs = _ssd_specs(reverse, ncc, nc, lambda b, i: (b, nc - 1 - i)) + [
        pl.BlockSpec((None, None) + state, lambda b, i: (b, nc - 1 - i, 0, 0, 0, 0)),
        pl.BlockSpec((None, SSD_CHUNK, SSD_W), lambda b, i: (b, ch(b, i), 0))]
    out_specs = [
        pl.BlockSpec((None, SSD_CHUNK, SSD_W), lambda b, i: (b, ch(b, i), 0)),
        pl.BlockSpec((None, SSD_CHUNK, gn), lambda b, i: (b, ch(b, i), 0)),
        pl.BlockSpec((None, SSD_CHUNK, gn), lambda b, i: (b, ch(b, i), 0)),
        pl.BlockSpec((None, SSD_GROUPS, SSD_CHUNK, SSD_HPG), lambda b, i: (b, 0, ch(b, i), 0)),
        pl.BlockSpec((None, SSD_GROUPS, SSD_HPG, SSD_CHUNK), lambda b, i: (b, 0, 0, ch(b, i))),
        pl.BlockSpec((None, SSD_GROUPS, 1, SSD_HPG), lambda b, i: (b, 0, 0, 0)),
        pl.BlockSpec((None, SSD_GROUPS, SSD_HPG, 1), lambda b, i: (b, 0, 0, 0)),
    ]
    out_shape = [
        jax.ShapeDtypeStruct((nb, s, SSD_W), F32),
        jax.ShapeDtypeStruct((nb, s, gn), F32),
        jax.ShapeDtypeStruct((nb, s, gn), F32),
        jax.ShapeDtypeStruct(dt_col.shape, F32),
        jax.ShapeDtypeStruct(dt_row.shape, F32),
        jax.ShapeDtypeStruct((nb, SSD_GROUPS, 1, SSD_HPG), F32),
        jax.ShapeDtypeStruct((nb, SSD_GROUPS, SSD_HPG, 1), F32),
    ]
    return pl.pallas_call(
        body,
        name=name,
        grid=(nb, nc),
        in_specs=in_specs,
        out_specs=out_specs,
        out_shape=out_shape,
        scratch_shapes=[pltpu.VMEM(state, F32)],
        compiler_params=_cparams(("parallel", "arbitrary")),
    )(xbc, xbc, xbc, dt_col, dt_row, a_row, a_col, hst, dy)


HG_TILES = HG_CHUNK // SUBLANES
HG_HEADS_PER_STEP = 3


def _hg_cum_tiles(x_t, reverse):
    row = lax.broadcasted_iota(jnp.int32, x_t[0].shape, 0)
    out = [None] * len(x_t)
    off = None
    for i in (reversed(range(len(x_t))) if reverse else range(len(x_t))):
        c = x_t[i]
        for k in (1, 2, 4):
            keep = (row < SUBLANES - k) if reverse else (row >= k)
            c = jnp.where(keep, c + pltpu.roll(c, SUBLANES - k if reverse else k, axis=0), c)
        out[i] = c if off is None else c + off
        off = _last_row(out[i], reverse)
    return out, off


def _hg_pairs(reverse):
    row = lax.broadcasted_iota(jnp.int32, (SUBLANES, HG_DK), 0)
    rots = []
    for r in range(SUBLANES):
        rots.append(((SUBLANES - r) % SUBLANES, row <= SUBLANES - 1 - r) if reverse else (r, row >= r))
    return [(j, [i for i in range(HG_TILES) if (i <= j if reverse else i >= j)], rots) for j in range(HG_TILES)]


def _rot(x, sh):
    return pltpu.roll(x, sh, axis=0) if sh else x


def _cat(tiles):
    return jnp.concatenate(tiles, axis=0)


def _hg_chunk_fwd(q_t, k_t, lf_t, v_t, st, reverse):
    cum_t, tot = _hg_cum_tiles(lf_t, reverse)
    y_t = [jnp.zeros(v_t[0].shape, F32) for _ in v_t]
    for j, l_tiles, rots in _hg_pairs(reverse):
        for sh, diag_ok in rots:
            k_j, c_j, v_j = _rot(k_t[j], sh), _rot(cum_t[j], sh), _rot(v_t[j], sh)
            for i in l_tiles:
                e = jnp.exp(cum_t[i] - c_j)
                if i == j:
                    e = jnp.where(diag_ok, e, 0.0)
                att = jnp.sum(q_t[i] * (k_j * e), axis=1, keepdims=True)
                y_t[i] = y_t[i] + att * v_j
    q, k, v, cum = _cat(q_t), _cat(k_t), _cat(v_t), _cat(cum_t)
    y_state = _dot_nt((q * jnp.exp(cum)).astype(BF16), st.astype(BF16))
    st_new = st * jnp.exp(tot) + _dot_tn(v.astype(BF16), (k * jnp.exp(tot - cum)).astype(BF16))
    return [y + y_state[SUBLANES * i:SUBLANES * (i + 1)] for i, y in enumerate(y_t)], st_new


def _hg_chunk_bwd(q_t, k_t, lf_t, v_t, st, dy_t, dst_new, reverse):
    nt = len(q_t)
    cum_t, tot = _hg_cum_tiles(lf_t, reverse)
    q, k, v, cum, dy = _cat(q_t), _cat(k_t), _cat(v_t), _cat(cum_t), _cat(dy_t)
    e_cum, e_tot, e_end = jnp.exp(cum), jnp.exp(tot), jnp.exp(tot - cum)
    qt, khat = q * e_cum, k * e_end
    dyb, dsb = dy.astype(BF16), dst_new.astype(BF16)
    dqt = jnp.dot(dyb, st.astype(BF16), preferred_element_type=F32)
    dst = dst_new * e_tot + _dot_tn(dyb, qt.astype(BF16))
    dv = _dot_nt(khat.astype(BF16), dsb)
    dkhat = jnp.dot(v.astype(BF16), dsb, preferred_element_type=F32)
    t1 = dkhat * khat
    dtot = jnp.sum(dst_new * st, axis=0, keepdims=True) * e_tot + jnp.sum(t1, axis=0, keepdims=True)
    rows = lax.broadcasted_iota(jnp.int32, cum.shape, 0)
    last = 0 if reverse else cum.shape[0] - 1
    dcum = dqt * qt - t1 + jnp.where(rows == last, dtot, 0.0)
    tiles = lambda a: [a[SUBLANES * i:SUBLANES * (i + 1)] for i in range(nt)]
    dq_t, dk_t, dv_t, dcum_t = tiles(dqt * e_cum), tiles(dkhat * e_end), tiles(dv), tiles(dcum)
    for j, l_tiles, rots in _hg_pairs(reverse):
        for sh, diag_ok in rots:
            k_j, c_j, v_j = _rot(k_t[j], sh), _rot(cum_t[j], sh), _rot(v_t[j], sh)
            acc_v = acc_k = acc_c = None
            for i in l_tiles:
                e = jnp.exp(cum_t[i] - c_j)
                if i == j:
                    e = jnp.where(diag_ok, e, 0.0)
                ke, qe = k_j * e, q_t[i] * e
                p = q_t[i] * ke
                att = jnp.sum(p, axis=1, keepdims=True)
                datt = jnp.sum(dy_t[i] * v_j, axis=1, keepdims=True)
                g = datt * p
                dq_t[i] = dq_t[i] + datt * ke
                dcum_t[i] = dcum_t[i] + g
                av, ak = att * dy_t[i], datt * qe
                acc_v, acc_k, acc_c = (av, ak, g) if acc_v is None else (acc_v + av, acc_k + ak, acc_c + g)
            back = (SUBLANES - sh) % SUBLANES
            dv_t[j] = dv_t[j] + _rot(acc_v, back)
            dk_t[j] = dk_t[j] + _rot(acc_k, back)
            dcum_t[j] = dcum_t[j] - _rot(acc_c, back)
    dlf_t, _ = _hg_cum_tiles(dcum_t, not reverse)
    return dq_t, dk_t, dlf_t, dv_t, dst


def _hg_super(s_ctx):
    return min(256, s_ctx)


def hg_fwd(name, q, k, lf, v, reverse, s_ctx):
    nb, s, w = q.shape
    nh, dk, sup = w // HG_DK, HG_DK, _hg_super(s_ctx)
    nsup, nsc, cps = s // sup, s_ctx // sup, sup // HG_CHUNK
    hps = HG_HEADS_PER_STEP if nh % HG_HEADS_PER_STEP == 0 else 1

    def body(q_ref, k_ref, lf_ref, v_ref, y_ref, hst_ref, st_s):
        i = pl.program_id(2)

        @pl.when(i == 0)
        def _():
            st_s[...] = jnp.zeros_like(st_s)

        def step(c, sts):
            r0 = pl.multiple_of((cps - 1 - c if reverse else c) * HG_CHUNK, HG_CHUNK)
            new = []
            for hh in range(hps):
                cols = slice(hh * dk, (hh + 1) * dk)
                tile = lambda ref: [ref[pl.ds(r0 + SUBLANES * i, SUBLANES), cols] for i in range(HG_TILES)]
                hst_ref[hh, c] = sts[hh]
                y_t, st_new = _hg_chunk_fwd(tile(q_ref), tile(k_ref), tile(lf_ref), tile(v_ref), sts[hh], reverse)
                for i in range(HG_TILES):
                    y_ref[pl.ds(r0 + SUBLANES * i, SUBLANES), cols] = y_t[i]
                new.append(st_new)
            return tuple(new)

        out = lax.fori_loop(0, cps, step, tuple(st_s[hh] for hh in range(hps)), unroll=2 if cps % 2 == 0 else 1)
        for hh in range(hps):
            st_s[hh] = out[hh]

    blk = pl.BlockSpec((None, sup, hps * dk), lambda b, h, i: (b, _chunk_at(i, reverse, nsc, nsup), h))
    return pl.pallas_call(
        body,
        name=name,
        grid=(nb, nh // hps, nsup),
        in_specs=[blk] * 4,
        out_specs=[blk, pl.BlockSpec((None, hps, cps, dk, dk), lambda b, h, i: (b, h, i, 0, 0))],
        out_shape=[jax.ShapeDtypeStruct(q.shape, F32), jax.ShapeDtypeStruct((nb, nh, s // HG_CHUNK, dk, dk), F32)],
        scratch_shapes=[pltpu.VMEM((hps, dk, dk), F32)],
        compiler_params=_cparams(("parallel", "parallel", "arbitrary")),
    )(q, k, lf, v)


def hg_bwd(name, q, k, lf, v, hst, dy, reverse, s_ctx):
    nb, s, w = q.shape
    nh, dk, sup = w // HG_DK, HG_DK, _hg_super(s_ctx)
    nsup, nsc, cps = s // sup, s_ctx // sup, sup // HG_CHUNK
    hps = HG_HEADS_PER_STEP if nh % HG_HEADS_PER_STEP == 0 else 1

    def body(q_ref, k_ref, lf_ref, v_ref, hst_ref, dy_ref, dq_ref, dk_ref, dlf_ref, dv_ref, dst_s):
        i = pl.program_id(2)

        @pl.when(i == 0)
        def _():
            dst_s[...] = jnp.zeros_like(dst_s)

        def step(cc, dsts):
            c = cps - 1 - cc
            r0 = pl.multiple_of((cps - 1 - c if reverse else c) * HG_CHUNK, HG_CHUNK)
            new = []
            for hh in range(hps):
                cols = slice(hh * dk, (hh + 1) * dk)
                tile = lambda ref: [ref[pl.ds(r0 + SUBLANES * i, SUBLANES), cols] for i in range(HG_TILES)]
                dq_t, dk_t, dlf_t, dv_t, dst_prev = _hg_chunk_bwd(
                    tile(q_ref), tile(k_ref), tile(lf_ref), tile(v_ref), hst_ref[hh, c], tile(dy_ref), dsts[hh], reverse)
                for ref, val in zip((dq_ref, dk_ref, dlf_ref, dv_ref), (dq_t, dk_t, dlf_t, dv_t)):
                    for i in range(HG_TILES):
                        ref[pl.ds(r0 + SUBLANES * i, SUBLANES), cols] = val[i]
                new.append(dst_prev)
            return tuple(new)

        out = lax.fori_loop(0, cps, step, tuple(dst_s[hh] for hh in range(hps)), unroll=2 if cps % 2 == 0 else 1)
        for hh in range(hps):
            dst_s[hh] = out[hh]

    blk = pl.BlockSpec((None, sup, hps * dk), lambda b, h, i: (b, _chunk_at(nsup - 1 - i, reverse, nsc, nsup), h))
    return pl.pallas_call(
        body,
        name=name,
        grid=(nb, nh // hps, nsup),
        in_specs=[blk] * 4 + [pl.BlockSpec((None, hps, cps, dk, dk), lambda b, h, i: (b, h, nsup - 1 - i, 0, 0)), blk],
        out_specs=[blk] * 4,
        out_shape=[jax.ShapeDtypeStruct(q.shape, F32)] * 4,
        scratch_shapes=[pltpu.VMEM((hps, dk, dk), F32)],
        compiler_params=_cparams(("parallel", "parallel", "arbitrary")),
    )(q, k, lf, v, hst, dy)


def f_s5_discretize(ins):
    lam_re, lam_im, log_step, b_re, b_im = ins
    step = jnp.exp(log_step)
    mag = jnp.exp(lam_re * step)
    ar, ai = mag * jnp.cos(lam_im * step), mag * jnp.sin(lam_im * step)
    den = lam_re * lam_re + lam_im * lam_im
    zr = ((ar - 1.0) * lam_re + ai * lam_im) / den
    zi = (ai * lam_re - (ar - 1.0) * lam_im) / den
    return [ar, ai, zr * b_re - zi * b_im, zr * b_im + zi * b_re]


def s5_tiles_of(m):
    g, p, k = m.shape
    gt = S5_TILE_CH // k
    eye = jnp.eye(gt, dtype=m.dtype)
    t = m.reshape(g // gt, gt, p, 1, k) * eye[None, :, None, :, None]
    return t.reshape(g // gt, gt * p, gt * k)


def s5_groups_of(t, g, p, k):
    gt = S5_TILE_CH // k
    eye = jnp.eye(gt, dtype=t.dtype)
    return jnp.sum(t.reshape(g // gt, gt, p, gt, k) * eye[None, :, None, :, None], axis=3).reshape(g, p, k)


def f_lower_bounds(ins):
    (logits,) = ins
    e = jnp.exp(logits - jnp.max(logits, axis=0, keepdims=True))
    p = e / jnp.sum(e, axis=0, keepdims=True)
    n = logits.shape[0]
    li = lax.broadcasted_iota(jnp.int32, (n, n), 0)
    si = lax.broadcasted_iota(jnp.int32, (n, n), 1)
    after_first = jnp.where(jnp.logical_and(si >= 1, si <= li), 1.0, 0.0)
    return [jnp.dot(after_first, p, precision=_HI, preferred_element_type=F32)]


def f_silu(ins):
    return [_silu(ins[0])]


def f_norm_keep(shift_row, scale_row):
    def f(tv, mv, pv):
        return [tv[0], _rms(tv[0], pv[0]) * (1.0 + mv[scale_row]) + mv[shift_row]]
    return f


def f_dt(tv, mv, pv):
    return [_softplus(tv[0] + pv[0])]


def f_even_finish(tv, mv, pv):
    y_f, y_b, xs, z, h_sum, gy = tv
    d_exp, g = pv
    y = _rms((y_f + y_b + d_exp * xs) * _silu(z), g)
    return [y, h_sum * jax.nn.gelu(gy)]


def f_odd_prep(tv, mv, pv):
    q, f_f, f_b = tv
    (lb,) = pv
    outs = [_silu(q)]
    for f in (f_f, f_b):
        outs.append((1.0 - lb) * jax.nn.sigmoid(-f))
        outs.append(jnp.log(lb + (1.0 - lb) * jax.nn.sigmoid(f)))
    return outs


def f_odd_finish(tv, mv, pv):
    o_f, o_b, g, s5y, u = tv
    norm_g, s5_d, glu_w, glu_b = pv
    o = o_f + o_b
    w = o.shape[1]
    hi = lax.broadcasted_iota(jnp.int32, (w, w), 0) // HG_DK
    hj = lax.broadcasted_iota(jnp.int32, (w, w), 1) // HG_DK
    head_mean = jnp.where(hi == hj, 1.0 / HG_DK, 0.0)
    ms = jnp.dot(o * o, head_mean, precision=_HI, preferred_element_type=F32)
    on = o * lax.rsqrt(ms + RMS_EPS) * norm_g * _silu(g)
    y = jax.nn.gelu(s5y + s5_d * u)
    gate = jax.nn.sigmoid(jnp.dot(y.astype(BF16), glu_w.astype(BF16), preferred_element_type=F32) + glu_b)
    return [on, y * gate]


def final_loss(name, s, br, mod, g, target, s_ctx):
    nb, st, d = s.shape
    tb = TOK_BLOCK
    assert s_ctx == tb

    def lossf(sv, bv, gate, gv, tv):
        y = _rms(sv + gate * bv, gv)
        err = jnp.square(y - tv)
        return 0.5 * jnp.sum(jnp.mean(err, axis=-1, keepdims=True), axis=0, keepdims=True)

    def body(s_ref, b_ref, m_ref, g_ref, t_ref, l_ref, ds_ref, db_ref, dm_ref, dg_ref):
        b, t = pl.program_id(0), pl.program_id(1)

        @pl.when(t == 0)
        def _():
            ds_ref[...] = jnp.zeros_like(ds_ref)
            db_ref[...] = jnp.zeros_like(db_ref)
            dm_ref[...] = jnp.zeros_like(dm_ref)
            l_ref[...] = jnp.zeros_like(l_ref)

        @pl.when(jnp.logical_and(b == 0, t == 0))
        def _():
            dg_ref[...] = jnp.zeros_like(dg_ref)

        @pl.when(t > 0)
        def _():
            gate = m_ref[N_MOD - 1:N_MOD, :]
            l, vjp = jax.vjp(lossf, s_ref[...], b_ref[...], gate, g_ref[...], t_ref[...])
            ds, db, dgate, dg, _ = vjp(jnp.ones((1, 1), F32))
            ds_ref[...] = ds
            db_ref[...] = db.astype(db_ref.dtype)
            dg_ref[...] += dg
            l_ref[...] += jnp.broadcast_to(l, l_ref.shape)

            @pl.when(t == 1)
            def _():
                dm_ref[...] = jnp.zeros_like(dm_ref)
                dm_ref[N_MOD - 1:N_MOD, :] = dgate

            @pl.when(t > 1)
            def _():
                dm_ref[N_MOD - 1:N_MOD, :] += dgate

    tok = pl.BlockSpec((None, tb, d), lambda b, t: (b, t, 0))
    modspec = pl.BlockSpec((None, N_MOD, d), _mod_index)
    gspec = pl.BlockSpec((1, d), lambda b, t: (0, 0))
    return pl.pallas_call(
        body,
        name=name,
        grid=(nb, st // tb),
        in_specs=[tok, tok, modspec, gspec, pl.BlockSpec((None, tb, d), lambda b, t: (b, jnp.maximum(t - 1, 0), 0))],
        out_specs=[pl.BlockSpec((None, SUBLANES, 128), lambda b, t: (b, 0, 0)), tok, tok, modspec, gspec],
        out_shape=[jax.ShapeDtypeStruct((nb, SUBLANES, 128), F32), jax.ShapeDtypeStruct(s.shape, F32),
                   jax.ShapeDtypeStruct(s.shape, BF16), jax.ShapeDtypeStruct(mod.shape, F32), jax.ShapeDtypeStruct(g.shape, F32)],
        compiler_params=_cparams(("arbitrary", "arbitrary")),
    )(s, br, mod, g, target)


def adamw(name, w, g, m, v):
    shape = w.shape
    cols = shape[-1] if w.ndim >= 2 else w.size
    rows = w.size // cols
    tr = _pick(rows, (512, 256, 128, 64, 32, 16, 8))

    def body(w_ref, g_ref, m_ref, v_ref, d_ref, nm_ref, nv_ref):
        gv = g_ref[...]
        nm = ADAM_B1 * m_ref[...] + (1.0 - ADAM_B1) * gv
        nv = ADAM_B2 * v_ref[...] + (1.0 - ADAM_B2) * jnp.square(gv)
        m_hat = nm / (1.0 - ADAM_B1 ** ADAM_STEP)
        v_hat = nv / (1.0 - ADAM_B2 ** ADAM_STEP)
        d_ref[...] = -ADAM_LR * (m_hat / (jnp.sqrt(v_hat) + ADAM_EPS) + ADAM_WD * w_ref[...])
        nm_ref[...] = nm
        nv_ref[...] = nv

    spec = pl.BlockSpec((tr, cols), lambda i: (i, 0))
    outs = pl.pallas_call(
        body,
        name=name,
        grid=(rows // tr,),
        in_specs=[spec] * 4,
        out_specs=[spec] * 3,
        out_shape=[jax.ShapeDtypeStruct((rows, cols), F32)] * 3,
        compiler_params=_cparams(("parallel",)),
    )(*(a.reshape(rows, cols) for a in (w, g, m, v)))
    return tuple(o.reshape(shape) for o in outs)


EV_COLS = {"z": (0, 1024), "xbc": (1024, 2560), "dt": (2560, 2592), "gy": (2592, 3616), "u": (3616, 4640)}
OD_COLS = {"q": (0, 768), "ff": (768, 1536), "fb": (1536, 2304), "v": (2304, 3072), "g": (3072, 3840), "u": (3840, 4096)}
EV_OUT_ROWS = ((0, 1024), (1024, 2048))
OD_OUT_ROWS = ((0, 768), (768, 1024))
LANES = 128


def _pad_to_lanes(w):
    n = w.shape[1]
    return w if n % LANES == 0 else jnp.pad(w, ((0, 0), (0, LANES - n % LANES)))


def _layer_weights(l, big):
    j = l // 2
    even = l % 2 == 0
    w_in = big["ev_w_in" if even else "od_w_in"][j]
    w_out = big["ev_w_out" if even else "od_w_out"][j]
    lw = {"in": {}, "out": []}
    for name, (a, b) in (EV_COLS if even else OD_COLS).items():
        w = _pad_to_lanes(w_in[:, a:b])
        lw["in"][name] = (w, w.T)
    for a, b in (EV_OUT_ROWS if even else OD_OUT_ROWS):
        lw["out"].append((w_out[a:b], w_out[a:b].T))
    for name in ("gate", "up", "down"):
        w = big["ffn_w_" + name][l]
        lw[name] = (w, w.T)
    return lw


def _rows2d(a):
    return a.reshape(-1, a.shape[-1])


def _mm3(a, w, name, out_dtype=F32):
    return mm([(_rows2d(a), w)], name, out_dtype).reshape(a.shape[:-1] + (w.shape[1],))


def _wgrad(a, d, name):
    return mm_tn(_rows2d(a), _rows2d(d), name)


def _dgrad(pairs, name, shape3):
    return mm([(_rows2d(d), wt) for d, wt in pairs], name).reshape(shape3[:-1] + (pairs[0][1].shape[1],))


def _dir_dt(dt, d):
    nb, s, _ = dt.shape
    dd = dt[:, :, SSD_HEADS * d:SSD_HEADS * (d + 1)].reshape(nb, s, SSD_GROUPS, SSD_HPG)
    return jnp.transpose(dd, (0, 2, 1, 3)), jnp.transpose(dd, (0, 2, 3, 1))


def _s5_prepare(p, j, tag):
    g_, p_, k_ = S5_GROUPS, S5_STATE, S5_GROUP_CH
    col = lambda t: t.reshape(g_ * p_, 1)
    ins, outs = [], []
    for d in (0, 1):
        i_d = [col(p["s5_lam_re"][j, d]), col(p["s5_lam_im"][j, d]), col(jnp.repeat(p["s5_log_step"][j, d], p_)),
               p["s5_b_re"][j].reshape(g_ * p_, k_), p["s5_b_im"][j].reshape(g_ * p_, k_)]
        ins.append(i_d)
        outs.append(small_fwd(f"{tag}_disc{d}", f_s5_discretize, i_d, [(g_ * p_, 1)] * 2 + [(g_ * p_, k_)] * 2))
    lam_r = jnp.stack([o[0].reshape(1, g_ * p_) for o in outs])
    lam_i = jnp.stack([o[1].reshape(1, g_ * p_) for o in outs])
    bt_r = jnp.stack([s5_tiles_of(o[2].reshape(g_, p_, k_)) for o in outs])
    bt_i = jnp.stack([s5_tiles_of(o[3].reshape(g_, p_, k_)) for o in outs])
    ct_r = jnp.stack([s5_tiles_of(jnp.transpose(p["s5_c_re"][j, d], (0, 2, 1))) for d in (0, 1)])
    ct_i = jnp.stack([s5_tiles_of(jnp.transpose(p["s5_c_im"][j, d], (0, 2, 1))) for d in (0, 1)])
    return ins, (lam_r, lam_i, bt_r, bt_i, ct_r, ct_i)


def _s5_param_grads(ins, grads, tag):
    g_, p_, k_ = S5_GROUPS, S5_STATE, S5_GROUP_CH
    dlr, dli, dbtr, dbti, dctr, dcti = grads
    g_lre, g_lim, g_ls, g_bre, g_bim = [], [], [], 0.0, 0.0
    for d in (0, 1):
        cots = [dlr[d].reshape(g_ * p_, 1), dli[d].reshape(g_ * p_, 1),
                s5_groups_of(dbtr[d], g_, p_, k_).reshape(g_ * p_, k_), s5_groups_of(dbti[d], g_, p_, k_).reshape(g_ * p_, k_)]
        g = small_bwd(f"{tag}_disc_bwd{d}", f_s5_discretize, ins[d], cots)
        g_lre.append(g[0].reshape(g_, p_))
        g_lim.append(g[1].reshape(g_, p_))
        g_ls.append(g[2].reshape(g_, p_).sum(-1))
        g_bre = g_bre + g[3].reshape(g_, p_, k_)
        g_bim = g_bim + g[4].reshape(g_, p_, k_)
    g_cre = jnp.stack([jnp.transpose(s5_groups_of(dctr[d], g_, p_, k_), (0, 2, 1)) for d in (0, 1)])
    g_cim = jnp.stack([jnp.transpose(s5_groups_of(dcti[d], g_, p_, k_), (0, 2, 1)) for d in (0, 1)])
    return jnp.stack(g_lre), jnp.stack(g_lim), jnp.stack(g_ls), g_bre, g_bim, g_cre, g_cim


def _even_mixer_fwd(l, hn, p, lw, s_ctx):
    j = l // 2
    t1 = taps_1d(4, s_ctx, hn.shape[1])
    r = {"hn": hn}
    proj = {n: _mm3(hn, lw["in"][n][0], f"l{l}_proj_{n}") for n in EV_COLS}
    r["z"], r["xbc"], r["gy"], r["u"] = proj["z"], proj["xbc"], proj["gy"], proj["u"]
    r["dtp"] = proj["dt"][:, :, :2 * SSD_HEADS]
    r["xbc_c"] = conv_fwd(f"l{l}_ssd_conv", r["xbc"], p["ssd_conv_w"][j], p["ssd_conv_b"][j][None], t1, "silu")
    r["u_c"] = conv_fwd(f"l{l}_lru_conv", r["u"], p["lru_conv_w"][j], p["lru_conv_b"][j][None], t1, "none")
    r["dt_bias"] = p["ssd_dt_bias"][j].reshape(1, 2 * SSD_HEADS)
    (r["dt"],) = tok_fwd(f"l{l}_dt", f_dt, [r["dtp"]], None, [r["dt_bias"]], [2 * SSD_HEADS], [F32])
    r["ys"], r["hst"], r["dts"], r["alog"] = [], [], [], []
    for d in (0, 1):
        dtc, dtr = _dir_dt(r["dt"], d)
        al = p["ssd_a_log"][j, d].reshape(SSD_GROUPS, SSD_HPG)
        al_r, al_c = al[:, None, :], al[:, :, None]
        y, hst = ssd_fwd(f"l{l}_ssd_fwd{d}", r["xbc_c"], dtc, dtr, al_r, al_c, bool(d), s_ctx)
        r["ys"].append(y)
        r["hst"].append(hst)
        r["dts"].append((dtc, dtr))
        r["alog"].append((al_r, al_c))
    v4 = lambda t: t.reshape(2, LRU_BLOCKS, 1, LRU_BLOCK_W)
    r["lru_p"] = (p["lru_w_a"][j], v4(p["lru_b_a"][j]), p["lru_w_i"][j], v4(p["lru_b_i"][j]), v4(p["lru_lam"][j]))
    r["h_sum"] = lru_fwd(f"l{l}_lru_fwd", r["u_c"], *r["lru_p"], s_ctx)
    r["xs"] = r["xbc_c"][:, :, :SSD_HEADS * SSD_HEAD_DIM]
    r["fin_p"] = [jnp.repeat(p["ssd_d"][j], SSD_HEAD_DIM)[None], p["ssd_norm_g"][j][None]]
    r["fin_in"] = [r["ys"][0], r["ys"][1], r["xs"], r["z"], r["h_sum"], r["gy"]]
    r["o"] = tok_fwd(f"l{l}_even_finish", f_even_finish, r["fin_in"], None, r["fin_p"], [1024, 1024], [BF16, BF16])
    return r


def _even_mixer_bwd(l, r, dox, p, lw, s_ctx, grads):
    j = l // 2
    shape3 = dox.shape
    t1 = taps_1d(4, s_ctx, shape3[1])
    grads["ev_w_out"][j] = jnp.concatenate([_wgrad(o, dox, f"l{l}_dwout{i}") for i, o in enumerate(r["o"])], axis=0)
    do = [_dgrad([(dox, lw["out"][i][1])], f"l{l}_dout{i}", shape3) for i in range(2)]
    (dy, _, dxs, dz, dh_sum, dgy), _, (dd_exp, grads["ssd_norm_g"][j]) = tok_bwd(
        f"l{l}_even_finish_bwd", f_even_finish, r["fin_in"], None, r["fin_p"], do, [F32, F32, F32, BF16, F32, BF16])
    grads["ssd_d"][j] = dd_exp.reshape(SSD_HEADS, SSD_HEAD_DIM).sum(-1)
    du_c, dwa, dba, dwi, dbi, dlam = lru_bwd(f"l{l}_lru_bwd", r["u_c"], *r["lru_p"], dh_sum, s_ctx)
    grads["lru_w_a"][j], grads["lru_w_i"][j] = dwa, dwi
    v2 = lambda t: t.reshape(2, LRU_BLOCKS * LRU_BLOCK_W)
    grads["lru_b_a"][j], grads["lru_b_i"][j], grads["lru_lam"][j] = v2(dba), v2(dbi), v2(dlam)
    dx_sum, dbm_sum, dcm_sum, ddts, dalog = dxs, 0.0, 0.0, [], []
    for d in (0, 1):
        dx, dbm, dcm, ddtc, ddtr, dar, dac = ssd_bwd(
            f"l{l}_ssd_bwd{d}", r["xbc_c"], *r["dts"][d], *r["alog"][d], r["hst"][d], dy, bool(d), s_ctx)
        dx_sum, dbm_sum, dcm_sum = dx_sum + dx, dbm_sum + dbm, dcm_sum + dcm
        ddts.append((jnp.transpose(ddtc, (0, 2, 1, 3)) + jnp.transpose(ddtr, (0, 3, 1, 2))).reshape(shape3[0], shape3[1], SSD_HEADS))
        dalog.append((dar.sum(0)[:, 0, :] + dac.sum(0)[:, :, 0]).reshape(SSD_HEADS))
    grads["ssd_a_log"][j] = jnp.stack(dalog)
    dxbc_c = jnp.concatenate([dx_sum, dbm_sum, dcm_sum], axis=-1)
    (ddtp,), _, (ddt_bias,) = tok_bwd(f"l{l}_dt_bwd", f_dt, [r["dtp"]], None, [r["dt_bias"]], [jnp.concatenate(ddts, axis=-1)], [F32])
    grads["ssd_dt_bias"][j] = ddt_bias.reshape(2, SSD_HEADS)
    dxbc, grads["ssd_conv_w"][j], dcb = conv_bwd(f"l{l}_ssd_conv_bwd", r["xbc"], p["ssd_conv_w"][j], p["ssd_conv_b"][j][None], dxbc_c, t1, "silu", dx_dtype=BF16)
    du, grads["lru_conv_w"][j], dlb = conv_bwd(f"l{l}_lru_conv_bwd", r["u"], p["lru_conv_w"][j], p["lru_conv_b"][j][None], du_c, t1, "none", dx_dtype=BF16)
    grads["ssd_conv_b"][j], grads["lru_conv_b"][j] = dcb[0], dlb[0]
    dproj = {"z": dz, "xbc": dxbc, "dt": _pad_to_lanes(_rows2d(ddtp)).reshape(shape3[:2] + (LANES,)), "gy": dgy, "u": du}
    grads["ev_w_in"][j] = jnp.concatenate(
        [_wgrad(r["hn"], dproj[n], f"l{l}_dwin_{n}")[:, :b - a] for n, (a, b) in EV_COLS.items()], axis=1)
    return _dgrad([(dproj[n], lw["in"][n][1]) for n in EV_COLS], f"l{l}_dhn", shape3)


def _odd_mixer_fwd(l, hn, p, lw, lb_row, s_ctx):
    j = l // 2
    r = {"hn": hn}
    proj = {n: _mm3(hn, lw["in"][n][0], f"l{l}_proj_{n}") for n in OD_COLS}
    r["v"], r["g"], r["u"] = proj["v"], proj["g"], proj["u"]
    r["prep_in"] = [proj["q"], proj["ff"], proj["fb"]]
    r["lb"] = lb_row
    r["prep"] = tok_fwd(f"l{l}_odd_prep", f_odd_prep, r["prep_in"], None, [lb_row], [HG_W] * 5, [F32] * 5)
    qs = r["prep"][0]
    r["os"], r["hst"] = [], []
    for d in (0, 1):
        o, hst = hg_fwd(f"l{l}_hg_fwd{d}", qs, r["prep"][1 + 2 * d], r["prep"][2 + 2 * d], r["v"], bool(d), s_ctx)
        r["os"].append(o)
        r["hst"].append(hst)
    r["s5_ins"], r["s5_p"] = _s5_prepare(p, j, f"l{l}_s5")
    r["s5y"] = s5_fwd(f"l{l}_s5_fwd", r["u"], *r["s5_p"], s_ctx)
    r["fin_p"] = [p["hg_norm_g"][j].reshape(1, HG_W), p["s5_d"][j][None], p["s5_glu_w"][j], p["s5_glu_b"][j][None]]
    r["fin_in"] = [r["os"][0], r["os"][1], r["g"], r["s5y"], r["u"]]
    r["o"] = tok_fwd(f"l{l}_odd_finish", f_odd_finish, r["fin_in"], None, r["fin_p"], [HG_W, S5_W], [BF16, BF16])
    return r


def _odd_mixer_bwd(l, r, dox, p, lw, s_ctx, grads):
    j = l // 2
    shape3 = dox.shape
    grads["od_w_out"][j] = jnp.concatenate([_wgrad(o, dox, f"l{l}_dwout{i}") for i, o in enumerate(r["o"])], axis=0)
    do = [_dgrad([(dox, lw["out"][i][1])], f"l{l}_dout{i}", shape3) for i in range(2)]
    (do_hg, _, dg, ds5y, du_fin), _, (dng, grads["s5_d"][j], grads["s5_glu_w"][j], dglu_b) = tok_bwd(
        f"l{l}_odd_finish_bwd", f_odd_finish, r["fin_in"], None, r["fin_p"], do, [F32, F32, BF16, F32, F32])
    grads["hg_norm_g"][j] = dng.reshape(HG_HEADS, HG_DK)
    grads["s5_d"][j], grads["s5_glu_b"][j] = grads["s5_d"][j][0], dglu_b[0]
    s5g = s5_bwd(f"l{l}_s5_bwd", r["u"], *r["s5_p"], ds5y, s_ctx)
    du = s5g[0] + du_fin
    (grads["s5_lam_re"][j], grads["s5_lam_im"][j], grads["s5_log_step"][j], grads["s5_b_re"][j], grads["s5_b_im"][j],
     grads["s5_c_re"][j], grads["s5_c_im"][j]) = _s5_param_grads(r["s5_ins"], s5g[1:], f"l{l}_s5")
    qs = r["prep"][0]
    dqs, dv, dprep = 0.0, 0.0, [None] * 5
    for d in (0, 1):
        dq, dk, dlf, dvd = hg_bwd(f"l{l}_hg_bwd{d}", qs, r["prep"][1 + 2 * d], r["prep"][2 + 2 * d], r["v"], r["hst"][d], do_hg, bool(d), s_ctx)
        dqs, dv = dqs + dq, dv + dvd
        dprep[1 + 2 * d], dprep[2 + 2 * d] = dk, dlf
    dprep[0] = dqs
    (dq_, dff, dfb), _, (dlb,) = tok_bwd(f"l{l}_odd_prep_bwd", f_odd_prep, r["prep_in"], None, [r["lb"]], dprep, [BF16] * 3)
    dproj = {"q": dq_, "ff": dff, "fb": dfb, "v": dv, "g": dg, "u": du}
    grads["od_w_in"][j] = jnp.concatenate([_wgrad(r["hn"], dproj[n], f"l{l}_dwin_{n}") for n in OD_COLS], axis=1)
    return _dgrad([(dproj[n], lw["in"][n][1]) for n in OD_COLS], f"l{l}_dhn", shape3), dlb


def _ffn_fwd(l, fn, p, lw, s_ctx):
    r = {"fn": fn}
    tg = taps_grid(s_ctx, fn.shape[1], GRID_W)
    r["a"] = _mm3(fn, lw["gate"][0], f"l{l}_ffn_gate")
    r["up"] = _mm3(fn, lw["up"][0], f"l{l}_ffn_up")
    r["cw"], r["cb"] = p["ffn_conv_w"][l].reshape(9, D_FF), p["ffn_conv_b"][l][None]
    r["act"] = conv_fwd(f"l{l}_ffn_conv", r["a"], r["cw"], r["cb"], tg, "silu_mul", mul=r["up"], out_dtype=BF16)
    return r, _mm3(r["act"], lw["down"][0], f"l{l}_ffn_down")


def _ffn_bwd(l, r, dfo, lw, s_ctx, grads):
    shape3 = dfo.shape
    tg = taps_grid(s_ctx, shape3[1], GRID_W)
    grads["ffn_w_down"][l] = _wgrad(r["act"], dfo, f"l{l}_dwdown")
    dact = _dgrad([(dfo, lw["down"][1])], f"l{l}_dact", shape3)
    da, dcw, dcb, dup = conv_bwd(f"l{l}_ffn_conv_bwd", r["a"], r["cw"], r["cb"], dact, tg, "silu_mul", mul=r["up"], dx_dtype=BF16)
    grads["ffn_conv_w"][l], grads["ffn_conv_b"][l] = dcw.reshape(3, 3, D_FF), dcb[0]
    grads["ffn_w_gate"][l] = _wgrad(r["fn"], da, f"l{l}_dwgate")
    grads["ffn_w_up"][l] = _wgrad(r["fn"], dup, f"l{l}_dwup")
    return _dgrad([(da, lw["gate"][1]), (dup, lw["up"][1])], f"l{l}_dfn", shape3)


BIG_WEIGHTS = ("ev_w_in", "ev_w_out", "od_w_in", "od_w_out", "ffn_w_gate", "ffn_w_up", "ffn_w_down")
PER_LAYER = {"norm_mix_g": DEPTH, "norm_ffn_g": DEPTH, "ffn_w_gate": DEPTH, "ffn_w_up": DEPTH, "ffn_conv_w": DEPTH,
             "ffn_conv_b": DEPTH, "ffn_w_down": DEPTH}


def local_step(x, ctx, target, modtabs, p, big, s_ctx=CTX_LEN):
    d_model = x.shape[-1]
    s0 = jnp.concatenate([ctx, x], axis=1)
    lws = [_layer_weights(l, big) for l in range(DEPTH)]
    shapes = {n: v.shape for n, v in {**p, **big}.items()}
    grads = {n: [None] * PER_LAYER.get(n, DEPTH // 2) for n in shapes if n not in ("c_ctx", "w_mod", "b_mod", "final_norm_g", "hg_lb_logits")}
    (lbs,) = small_fwd("lower_bounds", f_lower_bounds, [p["hg_lb_logits"]], [p["hg_lb_logits"].shape])
    tab_a = [modtabs[0]] + [modtabs[l].at[:, N_MOD - 1].set(modtabs[l - 1][:, N_MOD - 1]) for l in range(1, DEPTH)]
    res = []
    s, br = s0, None
    for l in range(DEPTH):
        r = {}
        g_mix, g_ffn = p["norm_mix_g"][l][None], p["norm_ffn_g"][l][None]
        if l == 0:
            (hn,) = tok_fwd("l0_norm", f_norm(0, 1), [s], tab_a[0], [g_mix], [d_model], [BF16])
            r["a_in"] = [s]
        else:
            r["a_in"] = [s, br]
            s, hn = tok_fwd(f"l{l}_resnorm_a", f_resnorm(5, 0, 1), r["a_in"], tab_a[l], [g_mix], [d_model] * 2, [F32, BF16])
        if l % 2 == 0:
            r["mix"] = _even_mixer_fwd(l, hn, p, lws[l], s_ctx)
        else:
            r["mix"] = _odd_mixer_fwd(l, hn, p, lws[l], lbs[l:l + 1], s_ctx)
        ox = mm([(_rows2d(o), w) for o, (w, _) in zip(r["mix"]["o"], lws[l]["out"])], f"l{l}_mix_out").reshape(s.shape)
        r["b_in"] = [s, ox]
        s, fn = tok_fwd(f"l{l}_resnorm_b", f_resnorm(2, 3, 4), r["b_in"], modtabs[l], [g_ffn], [d_model] * 2, [F32, BF16])
        r["ffn"], br = _ffn_fwd(l, fn, p, lws[l], s_ctx)
        res.append(r)

    loss_blk, ds, dbr, dtab_f, dfinal_g = final_loss("final_loss", s, br, modtabs[DEPTH - 1], p["final_norm_g"][None], target, s_ctx)
    grads["final_norm_g"] = dfinal_g[0]
    dmod = [None] * DEPTH
    dtab_next = dtab_f
    dlb = jnp.zeros_like(lbs)
    for l in reversed(range(DEPTH)):
        r = res[l]
        g_mix, g_ffn = p["norm_mix_g"][l][None], p["norm_ffn_g"][l][None]
        dfn = _ffn_bwd(l, r["ffn"], dbr, lws[l], s_ctx, grads)
        (ds, dox), dtab_b, (grads["norm_ffn_g"][l],) = tok_bwd(
            f"l{l}_resnorm_b_bwd", f_resnorm(2, 3, 4), r["b_in"], modtabs[l], [g_ffn], [ds, dfn], [F32, BF16])
        if l % 2 == 0:
            dhn = _even_mixer_bwd(l, r["mix"], dox, p, lws[l], s_ctx, grads)
        else:
            dhn, dlb_l = _odd_mixer_bwd(l, r["mix"], dox, p, lws[l], s_ctx, grads)
            dlb = dlb.at[l:l + 1].set(dlb_l)
        if l == 0:
            (ds,), dtab_a, (dg,) = tok_bwd("l0_norm_bwd", f_norm_keep(0, 1), r["a_in"], tab_a[0], [g_mix], [ds, dhn], [F32])
        else:
            (ds, dbr), dtab_a, (dg,) = tok_bwd(
                f"l{l}_resnorm_a_bwd", f_resnorm(5, 0, 1), r["a_in"], tab_a[l], [g_mix], [ds, dhn], [F32, BF16])
        grads["norm_mix_g"][l] = dg
        dmod[l] = (dtab_a.at[:, N_MOD - 1].set(0.0) + dtab_b).at[:, N_MOD - 1].set(dtab_next[:, N_MOD - 1])
        dtab_next = dtab_a
    (grads["hg_lb_logits"],) = small_bwd("lower_bounds_bwd", f_lower_bounds, [p["hg_lb_logits"]], [dlb])
    out = {}
    for n, g in grads.items():
        if isinstance(g, list):
            g = jnp.stack([t.reshape(shapes[n][1:]) for t in g])
        out[n] = g.reshape(shapes[n])
    return loss_blk[:, 0, 0], ds[:, s_ctx:], dmod, out


WEIGHT_NAMES = (
    "c_ctx", "w_mod", "b_mod", "norm_mix_g", "norm_ffn_g", "final_norm_g", "ev_w_in", "ev_w_out", "ssd_conv_w",
    "ssd_conv_b", "ssd_dt_bias", "ssd_a_log", "ssd_d", "ssd_norm_g", "lru_conv_w", "lru_conv_b", "lru_w_a", "lru_b_a",
    "lru_w_i", "lru_b_i", "lru_lam", "od_w_in", "od_w_out", "hg_lb_logits", "hg_norm_g", "s5_lam_re", "s5_lam_im",
    "s5_log_step", "s5_b_re", "s5_b_im", "s5_c_re", "s5_c_im", "s5_d", "s5_glu_w", "s5_glu_b", "ffn_w_gate", "ffn_w_up",
    "ffn_conv_w", "ffn_conv_b", "ffn_w_down")
INPUT_NAMES = ("x", "c", "ctx") + WEIGHT_NAMES + ("loss_target",) + tuple("m_" + n for n in WEIGHT_NAMES) + tuple("v_" + n for n in WEIGHT_NAMES)
SHARD_AXIS = {"w_mod": 2, "ev_w_in": 2, "ev_w_out": 1, "ssd_conv_w": 2, "lru_conv_w": 2, "lru_b_a": 2, "lru_b_i": 2,
              "lru_lam": 2, "od_w_in": 2, "od_w_out": 1, "s5_d": 1, "s5_glu_w": 1, "s5_glu_b": 1, "ffn_w_gate": 2,
              "ffn_w_up": 2, "ffn_conv_w": 3, "ffn_w_down": 1}
SMALL_SHARDED = tuple(n for n in WEIGHT_NAMES if n in SHARD_AXIS and n not in BIG_WEIGHTS and n != "w_mod")
REPLICATED_LOCAL = tuple(n for n in WEIGHT_NAMES if n not in SHARD_AXIS and n not in ("c_ctx", "b_mod"))
PACK_WIDTH = 1024
MOD_ROWS = 48
CTX_ROW = 32


def _unshard(g8, axis):
    moved = jnp.moveaxis(g8, 0, axis)
    shp = moved.shape
    return moved.reshape(shp[:axis] + (shp[axis] * shp[axis + 1],) + shp[axis + 2:])


def _to_shards(full, axis):
    shp = full.shape
    return jnp.moveaxis(full.reshape(shp[:axis] + (N_DEV, shp[axis] // N_DEV) + shp[axis + 1:]), axis, 0)


def _pack(arrs, dtype, lead=(), row_mult=16):
    flat = jnp.concatenate([a.astype(dtype).reshape(lead + (-1,)) for a in arrs], axis=-1)
    n = flat.shape[-1]
    unit = row_mult * PACK_WIDTH
    padded = -(-n // unit) * unit
    flat = jnp.pad(flat, [(0, 0)] * len(lead) + [(0, padded - n)])
    return flat.reshape(lead + (padded // PACK_WIDTH, PACK_WIDTH))


def _unpack(packed, shapes, lead=()):
    flat = packed.reshape(lead + (-1,))
    out, off = [], 0
    for shp in shapes:
        n = math.prod(shp)
        out.append(flat[..., off:off + n].reshape(lead + tuple(shp)))
        off += n
    return out


def _my_block(full, axis, me):
    loc = full.shape[axis] // N_DEV
    return lax.dynamic_slice_in_dim(full, me * loc, loc, axis)


def kernel(*args):
    a = dict(zip(INPUT_NAMES, args))
    px, py, pc = _my_pos()
    me = 4 * px + 2 * py + pc
    nb = a["x"].shape[0]

    small_names = ("c",) + SMALL_SHARDED
    *big8, small8 = all_gather([a[n].astype(BF16) for n in BIG_WEIGHTS] + [_pack([a[n] for n in small_names], F32)],
                               "gather_weights")
    big = {n: _unshard(g, SHARD_AXIS[n]) for n, g in zip(BIG_WEIGHTS, big8)}
    small = dict(zip(small_names, _unpack(small8, [a[n].shape for n in small_names], (N_DEV,))))
    p = {n: a[n] for n in WEIGHT_NAMES if n not in SHARD_AXIS}
    for n in SMALL_SHARDED:
        p[n] = _unshard(small[n], SHARD_AXIS[n])
    c_all = small["c"].reshape(N_DEV * nb, D_MODEL)

    rows = jnp.concatenate([c_all, a["c_ctx"][None], jnp.zeros((MOD_ROWS - CTX_ROW - 1, D_MODEL), F32)], axis=0)
    (srows,) = small_fwd("mod_silu", f_silu, [rows], [rows.shape])
    wmod2d = jnp.transpose(a["w_mod"], (1, 0, 2)).reshape(D_MODEL, -1).astype(BF16)
    cols = a["w_mod"].shape[2]
    mod_loc = mm([(srows, wmod2d)], "mod_proj")
    mod8 = all_gather([mod_loc], "gather_mod")[0].reshape(N_DEV, MOD_ROWS, DEPTH, cols)
    mod_all = jnp.transpose(mod8, (2, 1, 0, 3)).reshape(DEPTH, MOD_ROWS, N_DEV * cols) + a["b_mod"][:, None, :]
    modtabs = []
    for l in range(DEPTH):
        mine = lax.dynamic_slice_in_dim(mod_all[l], me * nb, nb, 0).reshape(nb, N_MOD, D_MODEL)
        ctx_row = jnp.broadcast_to(mod_all[l, CTX_ROW].reshape(1, N_MOD, D_MODEL), (nb, N_MOD, D_MODEL))
        modtabs.append(jnp.stack([ctx_row, mine], axis=1).reshape(2 * nb, N_MOD, D_MODEL))

    loss_b, grad_x, dmod, grads = local_step(a["x"], a["ctx"], a["loss_target"], modtabs, p, big)

    dm = jnp.stack([t.reshape(nb, 2, N_MOD * D_MODEL) for t in dmod])
    dloc = jnp.concatenate([dm[:, :, 1], jnp.sum(dm[:, :, 0], axis=1, keepdims=True),
                            jnp.zeros((DEPTH, SUBLANES - nb - 1, N_MOD * D_MODEL), F32)], axis=1)
    d8 = all_gather([dloc.reshape(DEPTH * SUBLANES, -1)], "gather_dmod")[0].reshape(N_DEV, DEPTH, SUBLANES, -1)
    d_rows = jnp.transpose(d8[:, :, :nb], (1, 0, 2, 3)).reshape(DEPTH, N_DEV * nb, -1)
    d_ctx = jnp.sum(d8[:, :, nb], axis=0)[:, None]
    d_full = jnp.concatenate([d_rows, d_ctx, jnp.zeros((DEPTH, MOD_ROWS - CTX_ROW - 1, N_MOD * D_MODEL), F32)], axis=1)
    grads["b_mod"] = jnp.sum(d_full, axis=1)
    d_cols = jnp.transpose(_my_block(d_full, 2, me), (1, 0, 2)).reshape(MOD_ROWS, DEPTH * cols)
    g_wmod = mm([(srows.T, d_cols)], "mod_dw")
    g_wmod_local = jnp.transpose(g_wmod.reshape(D_MODEL, DEPTH, cols), (1, 0, 2))
    d_srows_part = mm([(d_cols[CTX_ROW:CTX_ROW + SUBLANES], wmod2d.T)], "mod_dctx")[0]

    reduce_names = REPLICATED_LOCAL + SMALL_SHARDED
    parts = [jnp.sum(loss_b).reshape(1), d_srows_part] + [grads[n] for n in reduce_names]
    packed = _pack(parts, F32, row_mult=SUBLANES * N_DEV)
    own = [_to_shards(grads[n], SHARD_AXIS[n]).astype(BF16).reshape(N_CHIPS, 2, -1, a[n].shape[-1]) for n in BIG_WEIGHTS]
    own.append(packed.reshape(N_CHIPS, 2, -1, PACK_WIDTH))
    names = BIG_WEIGHTS + ("small",)
    from_sibling = sibling_swap(own, "exchange_sibling_grads")
    chip_sums = [pair_sum(o, s, "pair_sum_" + n) for n, o, s in zip(names, own, from_sibling)]
    got = chip_exchange(chip_sums, "exchange_chip_grads")
    eighths = [sum_slots(t, "sum_" + n) for n, t in zip(names, got)]
    (small8,) = all_gather([eighths[-1]], "gather_small_sums")
    totals = _unpack(small8, [(1,), (D_MODEL,)] + [grads[n].shape for n in reduce_names])
    loss = totals[0][0]
    d_srows = jnp.zeros_like(rows).at[CTX_ROW].set(totals[1])
    (d_rows_in,) = small_bwd("mod_silu_bwd", f_silu, [rows], [d_srows])
    g_local = {"c_ctx": d_rows_in[CTX_ROW], "b_mod": grads["b_mod"], "w_mod": g_wmod_local}
    for n, t in zip(reduce_names, totals[2:]):
        g_local[n] = _my_block(t, SHARD_AXIS[n], me) if n in SHARD_AXIS else t
    for n, t in zip(BIG_WEIGHTS, eighths):
        g_local[n] = t.reshape(a[n].shape)

    deltas, new_m, new_v = [], [], []
    for n in WEIGHT_NAMES:
        d, m, v = adamw("adamw_" + n, a[n], g_local[n], a["m_" + n], a["v_" + n])
        deltas.append(d)
        new_m.append(m)
        new_v.append(v)
    return (loss, grad_x, *[g_local[n] for n in WEIGHT_NAMES], *deltas, *new_m, *new_v)
```

```python
import functools
import math

import jax
import jax.numpy as jnp
from jax import lax
from jax.experimental import pallas as pl
from jax.experimental.pallas import tpu as pltpu

F32 = jnp.float32
BF16 = jnp.bfloat16

D_MODEL = 1024
DEPTH = 4
CTX_LEN = 256
SEQ = 2048
S_TOT = CTX_LEN + SEQ
GRID_W = 64
N_MOD = 6
RMS_EPS = 1e-6
N_DEV = 8

SSD_HEADS = 16
SSD_HEAD_DIM = 64
SSD_GROUPS = 2
SSD_HPG = 8
SSD_STATE = 128
SSD_CHUNK = 128
SSD_W = SSD_HEADS * SSD_HEAD_DIM
LRU_BLOCKS = 8
LRU_BLOCK_W = 128
LRU_C = 8.0
HG_W = 768
HG_HEADS = 6
HG_DK = 128
HG_CHUNK = 32
S5_W = 256
S5_GROUPS = 16
S5_GROUP_CH = 16
S5_STATE = 64
D_FF = 2816

ADAM_LR = 0.001
ADAM_B1 = 0.9
ADAM_B2 = 0.999
ADAM_EPS = 1e-08
ADAM_WD = 0.01
ADAM_STEP = 10

TOK_BLOCK = CTX_LEN
SUBLANES = 8
VMEM_LIMIT_BYTES = 56 * 1024 * 1024
MM_BLOCK_BYTES = 8 * 1024 * 1024
MM_TILES = (1408, 1024, 768, 704, 512, 384, 352, 256, 128, 64, 48, 40, 32, 16, 8)
MM_ROW_TILES = (2304, 2048, 1152, 1024, 512, 256, 128, 64, 48, 32, 16, 8)
LANE_TILE = 128


def _cparams(sem=None):
    kw = dict(vmem_limit_bytes=VMEM_LIMIT_BYTES)
    if sem is not None:
        kw["dimension_semantics"] = sem
    return pltpu.CompilerParams(**kw)


def _pick(n, cands):
    for c in cands:
        if n % c == 0:
            return c
    return n


def mm(pairs, name, out_dtype=F32):
    m = pairs[0][0].shape[0]
    n = pairs[0][1].shape[1]
    kdims = [a.shape[1] for a, _ in pairs]
    ktile = None
    if len(pairs) == 1 and kdims[0] > 4096:
        ktile = _pick(kdims[0], (2304, 2048, 1024))
    nk = kdims[0] // ktile if ktile else 1
    col_bytes = sum((ktile or w.shape[0]) * w.dtype.itemsize for _, w in pairs)
    tn = _pick(n, tuple(c for c in MM_TILES if c % LANE_TILE == 0 and c * col_bytes <= MM_BLOCK_BYTES))
    row_bytes = max(sum((ktile or a.shape[1]) * a.dtype.itemsize for a, _ in pairs), tn * 4)
    tm = _pick(m, tuple(c for c in MM_TILES if c * row_bytes <= MM_BLOCK_BYTES))
    npairs = len(pairs)
    if nk > 1:
        assert out_dtype == F32

    def body(*refs):
        o_ref = refs[2 * npairs]
        acc = None
        for i in range(npairs):
            a = refs[2 * i][...].astype(BF16)
            w = refs[2 * i + 1][...].astype(BF16)
            p = jnp.dot(a, w, preferred_element_type=F32)
            acc = p if acc is None else acc + p
        if nk == 1:
            o_ref[...] = acc.astype(out_dtype)
        else:
            k = pl.program_id(2)

            @pl.when(k == 0)
            def _():
                o_ref[...] = acc

            @pl.when(k > 0)
            def _():
                o_ref[...] += acc

    in_specs = []
    args = []
    for a, w in pairs:
        kk = a.shape[1]
        assert w.shape == (kk, n) and a.shape[0] == m, (a.shape, w.shape)
        tk = ktile if ktile else kk
        in_specs.append(pl.BlockSpec((tm, tk), lambda i, j, k: (i, k)))
        in_specs.append(pl.BlockSpec((tk, tn), lambda i, j, k: (k, j)))
        args += [a, w]
    return pl.pallas_call(
        body,
        name=name,
        grid=(m // tm, n // tn, nk),
        in_specs=in_specs,
        out_specs=pl.BlockSpec((tm, tn), lambda i, j, k: (i, j)),
        out_shape=jax.ShapeDtypeStruct((m, n), out_dtype),
        compiler_params=_cparams(("parallel", "parallel", "arbitrary")),
    )(*args)


def mm_tn(a, d, name):
    r, k = a.shape
    n = d.shape[1]
    tr = _pick(r, MM_ROW_TILES)
    lane_ok = lambda c, full: c % LANE_TILE == 0 or c == full
    tk = _pick(k, tuple(c for c in MM_TILES if lane_ok(c, k) and c * tr * a.dtype.itemsize <= MM_BLOCK_BYTES))
    tn = _pick(n, tuple(c for c in MM_TILES if lane_ok(c, n) and c * tr * d.dtype.itemsize <= MM_BLOCK_BYTES
                        and c * tk * 4 <= MM_BLOCK_BYTES))

    def body(a_ref, d_ref, o_ref):
        acc = lax.dot_general(a_ref[...].astype(BF16), d_ref[...].astype(BF16), (((0,), (0,)), ((), ())),
                              preferred_element_type=F32)
        step = pl.program_id(2)

        @pl.when(step == 0)
        def _():
            o_ref[...] = acc

        @pl.when(step > 0)
        def _():
            o_ref[...] += acc

    return pl.pallas_call(
        body,
        name=name,
        grid=(k // tk, n // tn, r // tr),
        in_specs=[pl.BlockSpec((tr, tk), lambda i, j, s: (s, i)), pl.BlockSpec((tr, tn), lambda i, j, s: (s, j))],
        out_specs=pl.BlockSpec((tk, tn), lambda i, j, s: (i, j)),
        out_shape=jax.ShapeDtypeStruct((k, n), F32),
        compiler_params=_cparams(("parallel", "parallel", "arbitrary")),
    )(a, d)


def _mod_index(b, t):
    return (2 * b + jnp.minimum(t, 1), 0, 0)


def tok_fwd(name, f, toks, mod, params, out_widths, out_dtypes):
    nb, s, _ = toks[0].shape
    nt, nm, npar = len(toks), int(mod is not None), len(params)

    def body(*refs):
        ins, outs = refs[: nt + nm + npar], refs[nt + nm + npar:]
        tv = [r[...].astype(F32) for r in ins[:nt]]
        mv = [ins[nt][k:k + 1, :] for k in range(N_MOD)] if nm else None
        pv = [r[...] for r in ins[nt + nm:]]
        for o, r in zip(outs, f(tv, mv, pv)):
            o[...] = r.astype(o.dtype)

    in_specs = [pl.BlockSpec((None, TOK_BLOCK, t.shape[2]), lambda b, t: (b, t, 0)) for t in toks]
    if nm:
        in_specs.append(pl.BlockSpec((None, N_MOD, mod.shape[2]), _mod_index))
    in_specs += [pl.BlockSpec(p.shape, lambda b, t, nd=p.ndim: (0,) * nd) for p in params]
    return pl.pallas_call(
        body,
        name=name,
        grid=(nb, s // TOK_BLOCK),
        in_specs=in_specs,
        out_specs=[pl.BlockSpec((None, TOK_BLOCK, w), lambda b, t: (b, t, 0)) for w in out_widths],
        out_shape=[jax.ShapeDtypeStruct((nb, s, w), dt) for w, dt in zip(out_widths, out_dtypes)],
        compiler_params=_cparams(("parallel", "parallel")),
    )(*toks, *([mod] if nm else []), *params)


def tok_bwd(name, f, toks, mod, params, cots, dtok_dtypes):
    nb, s, _ = toks[0].shape
    nt, nm, npar, nc = len(toks), int(mod is not None), len(params), len(cots)

    def body(*refs):
        n_in = nt + nm + npar + nc
        ins, outs = refs[:n_in], refs[n_in:]
        b, t = pl.program_id(0), pl.program_id(1)
        tv = [r[...].astype(F32) for r in ins[:nt]]
        mv = [ins[nt][k:k + 1, :] for k in range(N_MOD)] if nm else None
        pv = [r[...] for r in ins[nt + nm: nt + nm + npar]]
        cv = [r[...].astype(F32) for r in ins[nt + nm + npar:]]
        _, vjp = jax.vjp(f, tv, mv, pv)
        dtv, dmv, dpv = vjp(cv)
        for o, r in zip(outs[:nt], dtv):
            o[...] = r.astype(o.dtype)
        if nm:
            dm_ref = outs[nt]

            @pl.when(t <= 1)
            def _():
                for k in range(N_MOD):
                    dm_ref[k:k + 1, :] = dmv[k]

            @pl.when(t > 1)
            def _():
                for k in range(N_MOD):
                    dm_ref[k:k + 1, :] += dmv[k]

        first = jnp.logical_and(b == 0, t == 0)
        for o, r in zip(outs[nt + nm:], dpv):
            @pl.when(first)
            def _(o=o, r=r):
                o[...] = r

            @pl.when(jnp.logical_not(first))
            def _(o=o, r=r):
                o[...] += r

    tok_spec = lambda w: pl.BlockSpec((None, TOK_BLOCK, w), lambda b, t: (b, t, 0))
    in_specs = [tok_spec(t.shape[2]) for t in toks]
    if nm:
        in_specs.append(pl.BlockSpec((None, N_MOD, mod.shape[2]), _mod_index))
    in_specs += [pl.BlockSpec(p.shape, lambda b, t, nd=p.ndim: (0,) * nd) for p in params]
    in_specs += [tok_spec(c.shape[2]) for c in cots]
    out_specs = [tok_spec(t.shape[2]) for t in toks]
    out_shape = [jax.ShapeDtypeStruct(t.shape, dt) for t, dt in zip(toks, dtok_dtypes)]
    if nm:
        out_specs.append(pl.BlockSpec((None, N_MOD, mod.shape[2]), _mod_index))
        out_shape.append(jax.ShapeDtypeStruct(mod.shape, F32))
    out_specs += [pl.BlockSpec(p.shape, lambda b, t, nd=p.ndim: (0,) * nd) for p in params]
    out_shape += [jax.ShapeDtypeStruct(p.shape, F32) for p in params]
    res = pl.pallas_call(
        body,
        name=name,
        grid=(nb, s // TOK_BLOCK),
        in_specs=in_specs,
        out_specs=out_specs,
        out_shape=out_shape,
        compiler_params=_cparams(("arbitrary", "arbitrary")),
    )(*toks, *([mod] if nm else []), *params, *cots)
    return res[:nt], (res[nt] if nm else None), res[nt + nm:]


def _rms(x, g):
    return x * lax.rsqrt(jnp.mean(x * x, axis=-1, keepdims=True) + RMS_EPS) * g


def _silu(x):
    return x * jax.nn.sigmoid(x)


def f_norm(shift_row, scale_row):
    def f(tv, mv, pv):
        return [_rms(tv[0], pv[0]) * (1.0 + mv[scale_row]) + mv[shift_row]]
    return f


def f_resnorm(gate_row, shift_row, scale_row):
    def f(tv, mv, pv):
        s = tv[0] + mv[gate_row] * tv[1]
        return [s, _rms(s, pv[0]) * (1.0 + mv[scale_row]) + mv[shift_row]]
    return f


_ANY = pl.BlockSpec(memory_space=pl.ANY)
_MESH = pl.DeviceIdType.MESH


def _my_pos():
    return lax.axis_index("x"), lax.axis_index("y"), lax.axis_index("c")


def _slot_of(pos):
    return 4 * pos[0] + 2 * pos[1] + pos[2]


def all_gather(xs, name):
    n = len(xs)

    def body(*refs):
        x_refs, out_refs = refs[:n], refs[n:2 * n]
        send_sems, recv_sems, local_sems = refs[2 * n:]
        px, py, pc = _my_pos()
        me, sibling = (px, py, pc), (px, py, 1 - pc)
        chips = [(1 - px, py), (px, 1 - py), (1 - px, 1 - py)]

        def copy(a, k, block, to, from_input=False):
            slot = out_refs[a].at[_slot_of(block)]
            return pltpu.make_async_remote_copy(
                src_ref=x_refs[a] if from_input else slot, dst_ref=slot,
                send_sem=send_sems.at[a, k], recv_sem=recv_sems.at[a, k],
                device_id=to, device_id_type=_MESH)

        mine = [pltpu.make_async_copy(x_refs[a], out_refs[a].at[_slot_of(me)], local_sems.at[a]) for a in range(n)]
        for cp in mine:
            cp.start()
        first = [copy(a, 0, me, sibling, True) for a in range(n)]
        first += [copy(a, 1 + j, me, (*chip, pc), True) for j, chip in enumerate(chips) for a in range(n)]
        for cp in first:
            cp.start()
        passed = []
        for j, chip in enumerate(chips):
            for a in range(n):
                copy(a, 1 + j, (*chip, pc), me).wait_recv()
                passed.append(copy(a, 4 + j, (*chip, pc), sibling))
                passed[-1].start()
        for a in range(n):
            copy(a, 0, sibling, me).wait_recv()
        for j, chip in enumerate(chips):
            for a in range(n):
                copy(a, 4 + j, (*chip, 1 - pc), me).wait_recv()
        for cp in first + passed:
            cp.wait_send()
        for cp in mine:
            cp.wait()

    return pl.pallas_call(
        body,
        name=name,
        out_shape=[jax.ShapeDtypeStruct((N_DEV,) + x.shape, x.dtype) for x in xs],
        in_specs=[_ANY] * n,
        out_specs=[_ANY] * n,
        scratch_shapes=[pltpu.SemaphoreType.DMA((n, 7)), pltpu.SemaphoreType.DMA((n, 7)), pltpu.SemaphoreType.DMA((n,))],
    )(*xs)


def all_to_all(xs, name):
    n = len(xs)

    def body(*refs):
        x_refs, out_refs = refs[:n], refs[n:2 * n]
        send_sems, recv_sems, local_sems = refs[2 * n:]
        px, py, pc = _my_pos()
        me = (px, py, pc)

        def flipped(k):
            kx, ky, kc = (k >> 2) & 1, (k >> 1) & 1, k & 1
            return (1 - px if kx else px, 1 - py if ky else py, 1 - pc if kc else pc)

        def copy(a, k):
            peer = flipped(k)
            return pltpu.make_async_remote_copy(
                src_ref=x_refs[a].at[_slot_of(peer)], dst_ref=out_refs[a].at[_slot_of(me)],
                send_sem=send_sems.at[a, k - 1], recv_sem=recv_sems.at[a, k - 1],
                device_id=peer, device_id_type=_MESH)

        def landing(a, k):
            peer = flipped(k)
            return pltpu.make_async_remote_copy(
                src_ref=x_refs[a].at[_slot_of(me)], dst_ref=out_refs[a].at[_slot_of(peer)],
                send_sem=send_sems.at[a, k - 1], recv_sem=recv_sems.at[a, k - 1],
                device_id=peer, device_id_type=_MESH)

        mine = [pltpu.make_async_copy(x_refs[a].at[_slot_of(me)], out_refs[a].at[_slot_of(me)], local_sems.at[a]) for a in range(n)]
        for cp in mine:
            cp.start()
        copies = [copy(a, k) for a in range(n) for k in range(1, N_DEV)]
        for cp in copies:
            cp.start()
        for a in range(n):
            for k in range(1, N_DEV):
                landing(a, k).wait_recv()
        for cp in copies:
            cp.wait_send()
        for cp in mine:
            cp.wait()

    return pl.pallas_call(
        body,
        name=name,
        out_shape=[jax.ShapeDtypeStruct(x.shape, x.dtype) for x in xs],
        in_specs=[_ANY] * n,
        out_specs=[_ANY] * n,
        scratch_shapes=[pltpu.SemaphoreType.DMA((n, 7)), pltpu.SemaphoreType.DMA((n, 7)), pltpu.SemaphoreType.DMA((n,))],
    )(*xs)


N_CHIPS = 4


def sibling_swap(xs, name):
    n = len(xs)

    def body(*refs):
        x_refs, out_refs = refs[:n], refs[n:2 * n]
        send_sems, recv_sems = refs[2 * n:]
        px, py, pc = _my_pos()
        copies = [pltpu.make_async_remote_copy(
            src_ref=x_refs[a].at[:, 1 - pc], dst_ref=out_refs[a],
            send_sem=send_sems.at[a], recv_sem=recv_sems.at[a],
            device_id=(px, py, 1 - pc), device_id_type=_MESH) for a in range(n)]
        for cp in copies:
            cp.start()
        for cp in copies:
            cp.wait()

    return pl.pallas_call(
        body,
        name=name,
        out_shape=[jax.ShapeDtypeStruct(x.shape[:1] + x.shape[2:], x.dtype) for x in xs],
        in_specs=[_ANY] * n,
        out_specs=[_ANY] * n,
        scratch_shapes=[pltpu.SemaphoreType.DMA((n,)), pltpu.SemaphoreType.DMA((n,))],
    )(*xs)


def chip_exchange(xs, name):
    n = len(xs)

    def body(*refs):
        x_refs, out_refs = refs[:n], refs[n:2 * n]
        send_sems, recv_sems, local_sems = refs[2 * n:]
        px, py, pc = _my_pos()
        my_chip = 2 * px + py

        def peer(k):
            return (1 - px if k & 2 else px, 1 - py if k & 1 else py)

        def copy(a, k, landing):
            qx, qy = peer(k)
            src, dst = (my_chip, 2 * qx + qy) if landing else (2 * qx + qy, my_chip)
            return pltpu.make_async_remote_copy(
                src_ref=x_refs[a].at[src], dst_ref=out_refs[a].at[dst],
                send_sem=send_sems.at[a, k - 1], recv_sem=recv_sems.at[a, k - 1],
                device_id=(qx, qy, pc), device_id_type=_MESH)

        mine = [pltpu.make_async_copy(x_refs[a].at[my_chip], out_refs[a].at[my_chip], local_sems.at[a]) for a in range(n)]
        for cp in mine:
            cp.start()
        copies = [copy(a, k, False) for a in range(n) for k in range(1, N_CHIPS)]
        for cp in copies:
            cp.start()
        for a in range(n):
            for k in range(1, N_CHIPS):
                copy(a, k, True).wait_recv()
        for cp in copies:
            cp.wait_send()
        for cp in mine:
            cp.wait()

    return pl.pallas_call(
        body,
        name=name,
        out_shape=[jax.ShapeDtypeStruct(x.shape, x.dtype) for x in xs],
        in_specs=[_ANY] * n,
        out_specs=[_ANY] * n,
        scratch_shapes=[pltpu.SemaphoreType.DMA((n, N_CHIPS - 1)), pltpu.SemaphoreType.DMA((n, N_CHIPS - 1)),
                        pltpu.SemaphoreType.DMA((n,))],
    )(*xs)


def pair_sum(own, got, name):
    nch, _, r, c = own.shape
    tr = _pick(r, (512, 256, 128, 64, 32, 16))

    def body(own_ref, got_ref, o_ref):
        pc = lax.axis_index("c")
        o_ref[...] = (own_ref[pc].astype(F32) + got_ref[...].astype(F32)).astype(o_ref.dtype)

    return pl.pallas_call(
        body,
        name=name,
        grid=(nch, r // tr),
        in_specs=[pl.BlockSpec((None, 2, tr, c), lambda i, j: (i, 0, j, 0)), pl.BlockSpec((None, tr, c), lambda i, j: (i, j, 0))],
        out_specs=pl.BlockSpec((None, tr, c), lambda i, j: (i, j, 0)),
        out_shape=jax.ShapeDtypeStruct((nch, r, c), own.dtype),
        compiler_params=_cparams(("parallel", "parallel")),
    )(own, got)


def sum_slots(x, name):
    n, r, c = x.shape
    tr = _pick(r, (512, 256, 128, 64, 32, 16, 8))

    def body(x_ref, o_ref):
        acc = x_ref[0].astype(F32)
        for i in range(1, n):
            acc = acc + x_ref[i].astype(F32)
        o_ref[...] = acc

    return pl.pallas_call(
        body,
        name=name,
        grid=(r // tr,),
        in_specs=[pl.BlockSpec((n, tr, c), lambda i: (0, i, 0))],
        out_specs=pl.BlockSpec((tr, c), lambda i: (i, 0)),
        out_shape=jax.ShapeDtypeStruct((r, c), F32),
        compiler_params=_cparams(("parallel",)),
    )(x)


CONV_CH_TILE = 256


def _shift_rows(x, off):
    n = x.shape[0]
    if off % n == 0:
        return x
    return pltpu.roll(x, (-off) % n, axis=0)


def _between(v, lo, hi):
    return jnp.where(v >= lo, 1.0, 0.0) * jnp.where(v < hi, 1.0, 0.0)


def taps_1d(ntaps, s_ctx, s_tot):
    def mask(off):
        def m(t):
            is_ctx = _between(t, 0, s_ctx)
            return is_ctx * _between(t + off, 0, s_ctx) + (1.0 - is_ctx) * _between(t + off, s_ctx, s_tot)
        return m
    return [(j - (ntaps - 1) // 2, mask(j - (ntaps - 1) // 2)) for j in range(ntaps)]


def taps_grid(s_ctx, s_tot, grid_w):
    assert s_ctx % grid_w == 0

    return ("grid", s_ctx, s_tot, grid_w)


def _grid_masks(taps, s):
    _, s_ctx, s_tot, grid_w = taps
    t = lax.broadcasted_iota(jnp.int32, (s, 1), 0)
    is_ctx = _between(t, 0, s_ctx)
    mcol = {dc: is_ctx * _between(t + dc, 0, s_ctx) + (1.0 - is_ctx) * _between(t % grid_w + dc, 0, grid_w) for dc in (-1, 1)}
    mrow = {dr: (1.0 - is_ctx) * _between(t + grid_w * dr, s_ctx, s_tot) for dr in (-1, 1)}
    return mcol, mrow


def _grid_cols(x, mcol):
    return {-1: _shift_rows(x, -1) * mcol[-1], 0: x, 1: _shift_rows(x, 1) * mcol[1]}


def _conv_acc(x, w_ref, b_ref, taps, s):
    acc = jnp.broadcast_to(b_ref[...], x.shape)
    if taps[0] == "grid":
        grid_w = taps[3]
        mcol, mrow = _grid_masks(taps, s)
        xc = _grid_cols(x, mcol)
        for a, dr in enumerate((-1, 0, 1)):
            r = sum(w_ref[3 * a + b:3 * a + b + 1, :] * xc[dc] for b, dc in enumerate((-1, 0, 1)))
            acc = acc + (r if dr == 0 else _shift_rows(r, grid_w * dr) * mrow[dr])
        return acc
    t = lax.broadcasted_iota(jnp.int32, (s, 1), 0)
    for k, (off, m) in enumerate(taps):
        acc = acc + w_ref[k:k + 1, :] * (_shift_rows(x, off) * m(t))
    return acc


def _conv_adjoint(x, dacc, w_ref, taps, s):
    if taps[0] == "grid":
        grid_w = taps[3]
        mcol, mrow = _grid_masks(taps, s)
        xc = _grid_cols(x, mcol)
        dxc = {dc: 0.0 for dc in (-1, 0, 1)}
        dws = []
        for a, dr in enumerate((-1, 0, 1)):
            d_r = dacc if dr == 0 else _shift_rows(dacc * mrow[dr], -grid_w * dr)
            for b, dc in enumerate((-1, 0, 1)):
                dxc[dc] = dxc[dc] + w_ref[3 * a + b:3 * a + b + 1, :] * d_r
                dws.append(jnp.sum(d_r * xc[dc], axis=0, keepdims=True))
        dx = dxc[0] + _shift_rows(dxc[-1] * mcol[-1], 1) + _shift_rows(dxc[1] * mcol[1], -1)
        return dx, dws
    t = lax.broadcasted_iota(jnp.int32, (s, 1), 0)
    dx = jnp.zeros_like(x)
    dws = []
    for k, (off, m) in enumerate(taps):
        dm = dacc * m(t)
        dx = dx + _shift_rows(w_ref[k:k + 1, :] * dm, -off)
        dws.append(jnp.sum(dm * _shift_rows(x, off), axis=0, keepdims=True))
    return dx, dws


def conv_fwd(name, x, w, b, taps, mode, mul=None, out_dtype=F32):
    nb, s, c = x.shape
    ct = _pick(c, (CONV_CH_TILE, 128))
    has_mul = mode == "silu_mul"

    def body(*refs):
        x_ref, w_ref, b_ref = refs[:3]
        o_ref = refs[-1]
        acc = _conv_acc(x_ref[...], w_ref, b_ref, taps, s)
        if mode == "none":
            out = acc
        else:
            out = _silu(acc)
            if has_mul:
                out = out * refs[3][...].astype(F32)
        o_ref[...] = out.astype(o_ref.dtype)

    blk = pl.BlockSpec((None, s, ct), lambda bb, j: (bb, 0, j))
    par = lambda k: pl.BlockSpec((k, ct), lambda bb, j: (0, j))
    return pl.pallas_call(
        body,
        name=name,
        grid=(nb, c // ct),
        in_specs=[blk, par(w.shape[0]), par(1)] + ([blk] if has_mul else []),
        out_specs=blk,
        out_shape=jax.ShapeDtypeStruct(x.shape, out_dtype),
        compiler_params=_cparams(("parallel", "parallel")),
    )(x, w, b, *([mul] if has_mul else []))


def conv_bwd(name, x, w, b, dout, taps, mode, mul=None, dx_dtype=F32):
    nb, s, c = x.shape
    ct = _pick(c, (CONV_CH_TILE, 128))
    has_mul = mode == "silu_mul"
    nk = w.shape[0]

    def body(*refs):
        x_ref, w_ref, b_ref, do_ref = refs[:4]
        n_in = 5 if has_mul else 4
        dx_ref, dw_ref, db_ref = refs[n_in:n_in + 3]
        bb = pl.program_id(1)
        x = x_ref[...]
        dacc = do_ref[...].astype(F32)
        if mode != "none":
            acc = _conv_acc(x, w_ref, b_ref, taps, s)
            sg = jax.nn.sigmoid(acc)
            if has_mul:
                refs[n_in + 3][...] = (dacc * (acc * sg)).astype(refs[n_in + 3].dtype)
                dacc = dacc * refs[4][...].astype(F32)
            dacc = dacc * (sg * (1.0 + acc * (1.0 - sg)))
        dx, dws = _conv_adjoint(x, dacc, w_ref, taps, s)
        dx_ref[...] = dx.astype(dx_ref.dtype)
        db = jnp.sum(dacc, axis=0, keepdims=True)

        @pl.when(bb == 0)
        def _():
            for k in range(nk):
                dw_ref[k:k + 1, :] = dws[k]
            db_ref[...] = db

        @pl.when(bb > 0)
        def _():
            for k in range(nk):
                dw_ref[k:k + 1, :] += dws[k]
            db_ref[...] += db

    blk = pl.BlockSpec((None, s, ct), lambda j, bb: (bb, 0, j))
    par = lambda k: pl.BlockSpec((k, ct), lambda j, bb: (0, j))
    out_specs = [blk, par(nk), par(1)] + ([blk] if has_mul else [])
    out_shape = [jax.ShapeDtypeStruct(x.shape, dx_dtype), jax.ShapeDtypeStruct(w.shape, F32), jax.ShapeDtypeStruct(b.shape, F32)]
    if has_mul:
        out_shape.append(jax.ShapeDtypeStruct(x.shape, dx_dtype))
    return pl.pallas_call(
        body,
        name=name,
        grid=(c // ct, nb),
        in_specs=[blk, par(nk), par(1), blk] + ([blk] if has_mul else []),
        out_specs=out_specs,
        out_shape=out_shape,
        compiler_params=_cparams(("parallel", "arbitrary")),
    )(x, w, b, dout, *([mul] if has_mul else []))


SCAN_UNROLL = 4


def _scan_order(direction, adjoint, s_ctx, s_tot):
    nc, nt = s_ctx // SUBLANES, s_tot // SUBLANES
    if direction == 0:
        return ([(0, nt, 1)], False) if not adjoint else ([(nt - 1, nt, -1)], True)
    if not adjoint:
        return [(nc - 1, nc, -1), (nt - 1, nt - nc, -1)], True
    return [(nc, nt - nc, 1), (0, nc, 1)], False


def _last_row(h, descending):
    row = lax.broadcasted_iota(jnp.int32, h.shape, 0)
    pick = 0 if descending else SUBLANES - 1
    return jnp.sum(jnp.where(row == pick, h, 0.0), axis=0, keepdims=True)


def _prev_rows(h, carry, descending):
    row = lax.broadcasted_iota(jnp.int32, h.shape, 0)
    if descending:
        return jnp.where(row == SUBLANES - 1, carry, pltpu.roll(h, SUBLANES - 1, axis=0))
    return jnp.where(row == 0, carry, pltpu.roll(h, 1, axis=0))


def _scan_real(a_ref, x_ref, h_ref, hp_ref, order):
    ranges, descending = order
    width = a_ref.shape[1]
    row = lax.broadcasted_iota(jnp.int32, (SUBLANES, width), 0)

    def tile(i, carry):
        t0 = pl.multiple_of(i * SUBLANES, SUBLANES)
        a = a_ref[pl.ds(t0, SUBLANES), :]
        x = x_ref[pl.ds(t0, SUBLANES), :]
        for k in (1, 2, 4):
            sh = SUBLANES - k if descending else k
            keep = (row < SUBLANES - k) if descending else (row >= k)
            x = jnp.where(keep, a * pltpu.roll(x, sh, axis=0) + x, x)
            a = jnp.where(keep, a * pltpu.roll(a, sh, axis=0), a)
        h = a * carry + x
        if h_ref is not None:
            h_ref[pl.ds(t0, SUBLANES), :] = h
        if hp_ref is not None:
            hp_ref[pl.ds(t0, SUBLANES), :] = _prev_rows(h, carry, descending)
        return _last_row(h, descending)

    carry = jnp.zeros((1, width), F32)
    for first, count, step in ranges:
        carry = lax.fori_loop(0, count, lambda j, c, first=first, step=step: tile(first + step * j, c), carry,
                              unroll=SCAN_UNROLL if count % SCAN_UNROLL == 0 else 1)


def _cmul(ar, ai, br, bi):
    return ar * br - ai * bi, ar * bi + ai * br


def _scan_cplx(lr, li, xr_ref, xi_ref, hpr_ref, hpi_ref, order):
    ranges, descending = order
    width = xr_ref.shape[1]
    row = lax.broadcasted_iota(jnp.int32, (SUBLANES, width), 0)
    pw = [(lr, li)]
    for _ in range(SUBLANES - 1):
        pw.append(_cmul(pw[-1][0], pw[-1][1], lr, li))
    pr = jnp.zeros((SUBLANES, width), F32)
    pi = jnp.zeros((SUBLANES, width), F32)
    for r in range(SUBLANES):
        n = SUBLANES - 1 - r if descending else r
        pr = jnp.where(row == r, pw[n][0], pr)
        pi = jnp.where(row == r, pw[n][1], pi)

    def tile(i, carry):
        cr, ci = carry
        t0 = pl.multiple_of(i * SUBLANES, SUBLANES)
        xr = xr_ref[pl.ds(t0, SUBLANES), :]
        xi = xi_ref[pl.ds(t0, SUBLANES), :]
        for k in (1, 2, 4):
            sh = SUBLANES - k if descending else k
            keep = (row < SUBLANES - k) if descending else (row >= k)
            sr, si = _cmul(pw[k - 1][0], pw[k - 1][1], pltpu.roll(xr, sh, axis=0), pltpu.roll(xi, sh, axis=0))
            xr = jnp.where(keep, xr + sr, xr)
            xi = jnp.where(keep, xi + si, xi)
        hr, hi = _cmul(pr, pi, cr, ci)
        hr, hi = hr + xr, hi + xi
        xr_ref[pl.ds(t0, SUBLANES), :] = hr
        xi_ref[pl.ds(t0, SUBLANES), :] = hi
        if hpr_ref is not None:
            hpr_ref[pl.ds(t0, SUBLANES), :] = _prev_rows(hr, cr, descending)
            hpi_ref[pl.ds(t0, SUBLANES), :] = _prev_rows(hi, ci, descending)
        return _last_row(hr, descending), _last_row(hi, descending)

    carry = (jnp.zeros((1, width), F32), jnp.zeros((1, width), F32))
    for first, count, step in ranges:
        carry = lax.fori_loop(0, count, lambda j, c, first=first, step=step: tile(first + step * j, c), carry,
                              unroll=SCAN_UNROLL if count % SCAN_UNROLL == 0 else 1)


def _log1p_pos(y):
    return jnp.where(y < 0.01, y * (1.0 - y * (0.5 - y * (1.0 / 3.0 - 0.25 * y))), jnp.log(1.0 + y))


def _softplus(x):
    return jnp.maximum(x, 0.0) + _log1p_pos(jnp.exp(-jnp.abs(x)))


def _neg_expm1(z):
    series = -z * (1.0 + z * (0.5 + z * (1.0 / 6.0 + z * (1.0 / 24.0 + z * (1.0 / 120.0)))))
    return jnp.where(z > -0.1, series, 1.0 - jnp.exp(z))


def _lru_gates(u, w_a, b_a, w_i, b_i, lam):
    ub = u.astype(BF16)
    r = jax.nn.sigmoid(jnp.dot(ub, w_a.astype(BF16), preferred_element_type=F32) + b_a)
    i = jax.nn.sigmoid(jnp.dot(ub, w_i.astype(BF16), preferred_element_type=F32) + b_i)
    log_a = (-LRU_C) * _softplus(-lam) * r
    return jnp.exp(log_a), jnp.sqrt(_neg_expm1(2.0 * log_a)) * (i * u)


LRU_PER_STEP = 4
LRU_PER_STEP_BWD = 2


def _lru_specs(per, bw, order):
    w = pl.BlockSpec((2, per, bw, bw), lambda *g: (0, order(*g), 0, 0))
    v = pl.BlockSpec((2, per, 1, bw), lambda *g: (0, order(*g), 0, 0))
    return [w, v, w, v, v]


def lru_fwd(name, u, w_a, b_a, w_i, b_i, lam, s_ctx):
    nb, s, _ = u.shape
    nblk, bw = w_a.shape[1], w_a.shape[2]
    per = min(LRU_PER_STEP, nblk)

    def body(u_ref, wa, ba, wi, bi, lm, o_ref, a_s, x_s, h_s):
        for d in (0, 1):
            for k in range(per):
                cols = slice(k * bw, (k + 1) * bw)
                a, bx = _lru_gates(u_ref[:, cols], wa[d, k], ba[d, k], wi[d, k], bi[d, k], lm[d, k])
                a_s[:, cols] = a
                x_s[:, cols] = bx
            _scan_real(a_s, x_s, h_s, None, _scan_order(d, False, s_ctx, s))
            if d == 0:
                o_ref[...] = h_s[...]
            else:
                o_ref[...] += h_s[...]

    blk = pl.BlockSpec((None, s, per * bw), lambda b, n: (b, 0, n))
    return pl.pallas_call(
        body,
        name=name,
        grid=(nb, nblk // per),
        in_specs=[blk] + _lru_specs(per, bw, lambda b, n: n),
        out_specs=blk,
        out_shape=jax.ShapeDtypeStruct(u.shape, F32),
        scratch_shapes=[pltpu.VMEM((s, per * bw), F32)] * 3,
        compiler_params=_cparams(("parallel", "parallel")),
    )(u, w_a, b_a, w_i, b_i, lam)


def lru_bwd(name, u, w_a, b_a, w_i, b_i, lam, dh, s_ctx):
    nb, s, _ = u.shape
    nblk, bw = w_a.shape[1], w_a.shape[2]
    per = min(LRU_PER_STEP_BWD, nblk)

    def body(u_ref, wa, ba, wi, bi, lm, dh_ref, du_ref, dwa, dba, dwi, dbi, dlm, a_s, x_s, hp_s, wp_s):
        b = pl.program_id(1)
        for d in (0, 1):
            for k in range(per):
                cols = slice(k * bw, (k + 1) * bw)
                a, bx = _lru_gates(u_ref[:, cols], wa[d, k], ba[d, k], wi[d, k], bi[d, k], lm[d, k])
                a_s[:, cols] = a
                x_s[:, cols] = bx
            _scan_real(a_s, x_s, None, hp_s, _scan_order(d, False, s_ctx, s))
            x_s[...] = a_s[...] * dh_ref[...]
            _scan_real(a_s, x_s, None, wp_s, _scan_order(d, True, s_ctx, s))
            for k in range(per):
                cols = slice(k * bw, (k + 1) * bw)
                g = dh_ref[:, cols] + wp_s[:, cols]
                _, vjp = jax.vjp(_lru_gates, u_ref[:, cols], wa[d, k], ba[d, k], wi[d, k], bi[d, k], lm[d, k])
                grads = vjp((g * hp_s[:, cols], g))
                if d == 0:
                    du_ref[:, cols] = grads[0]
                else:
                    du_ref[:, cols] += grads[0]
                for ref, val in zip((dwa, dba, dwi, dbi, dlm), grads[1:]):
                    @pl.when(b == 0)
                    def _(ref=ref, val=val, k=k):
                        ref[d, k] = val

                    @pl.when(b > 0)
                    def _(ref=ref, val=val, k=k):
                        ref[d, k] += val

    blk = pl.BlockSpec((None, s, per * bw), lambda n, b: (b, 0, n))
    pspecs = _lru_specs(per, bw, lambda n, b: n)
    return pl.pallas_call(
        body,
        name=name,
        grid=(nblk // per, nb),
        in_specs=[blk] + pspecs + [blk],
        out_specs=[blk] + pspecs,
        out_shape=[jax.ShapeDtypeStruct(u.shape, F32)] + [jax.ShapeDtypeStruct(p.shape, F32) for p in (w_a, b_a, w_i, b_i, lam)],
        scratch_shapes=[pltpu.VMEM((s, per * bw), F32)] * 4,
        compiler_params=_cparams(("parallel", "arbitrary")),
    )(u, w_a, b_a, w_i, b_i, lam, dh)


S5_TILE_CH = 128
S5_TILE_STATES = S5_TILE_CH // S5_GROUP_CH * S5_STATE


def _dot_nt(a, b):
    return lax.dot_general(a, b, (((1,), (1,)), ((), ())), preferred_element_type=F32)


def _dot_tn(a, b):
    return lax.dot_general(a, b, (((0,), (0,)), ((), ())), preferred_element_type=F32)


def _s5_specs(order):
    lam = pl.BlockSpec((2, 1, S5_TILE_STATES), lambda *g: (0, 0, order(*g)))
    mat = pl.BlockSpec((2, None, S5_TILE_STATES, S5_TILE_CH), lambda *g: (0, order(*g), 0, 0))
    return [lam, lam, mat, mat, mat, mat]


def s5_fwd(name, u, lam_r, lam_i, bt_r, bt_i, ct_r, ct_i, s_ctx):
    nb, s, w = u.shape

    def body(u_ref, lr, li, btr, bti, ctr, cti, o_ref, xr_s, xi_s):
        ub = u_ref[...].astype(BF16)
        for d in (0, 1):
            xr_s[...] = _dot_nt(ub, btr[d].astype(BF16))
            xi_s[...] = _dot_nt(ub, bti[d].astype(BF16))
            _scan_cplx(lr[d], li[d], xr_s, xi_s, None, None, _scan_order(d, False, s_ctx, s))
            y = (jnp.dot(xr_s[...].astype(BF16), ctr[d].astype(BF16), preferred_element_type=F32)
                 - jnp.dot(xi_s[...].astype(BF16), cti[d].astype(BF16), preferred_element_type=F32))
            if d == 0:
                o_ref[...] = y
            else:
                o_ref[...] += y

    blk = pl.BlockSpec((None, s, S5_TILE_CH), lambda b, j: (b, 0, j))
    return pl.pallas_call(
        body,
        name=name,
        grid=(nb, w // S5_TILE_CH),
        in_specs=[blk] + _s5_specs(lambda b, j: j),
        out_specs=blk,
        out_shape=jax.ShapeDtypeStruct(u.shape, F32),
        scratch_shapes=[pltpu.VMEM((s, S5_TILE_STATES), F32)] * 2,
        compiler_params=_cparams(("parallel", "parallel")),
    )(u, lam_r, lam_i, bt_r, bt_i, ct_r, ct_i)


def s5_bwd(name, u, lam_r, lam_i, bt_r, bt_i, ct_r, ct_i, dy, s_ctx):
    nb, s, w = u.shape

    def body(u_ref, lr, li, btr, bti, ctr, cti, dy_ref, du_ref, dlr, dli, dbtr, dbti, dctr, dcti,
             hr_s, hi_s, hpr_s, hpi_s, gr_s, gi_s):
        b = pl.program_id(1)
        ub = u_ref[...].astype(BF16)
        dyb = dy_ref[...].astype(BF16)
        du = jnp.zeros((s, S5_TILE_CH), F32)
        for d in (0, 1):
            hr_s[...] = _dot_nt(ub, btr[d].astype(BF16))
            hi_s[...] = _dot_nt(ub, bti[d].astype(BF16))
            _scan_cplx(lr[d], li[d], hr_s, hi_s, hpr_s, hpi_s, _scan_order(d, False, s_ctx, s))
            d_ctr = _dot_tn(hr_s[...].astype(BF16), dyb)
            d_cti = -_dot_tn(hi_s[...].astype(BF16), dyb)
            gr_s[...] = _dot_nt(dyb, ctr[d].astype(BF16))
            gi_s[...] = -_dot_nt(dyb, cti[d].astype(BF16))
            _scan_cplx(lr[d], -li[d], gr_s, gi_s, None, None, _scan_order(d, True, s_ctx, s))
            gr, gi = gr_s[...], gi_s[...]
            hpr, hpi = hpr_s[...], hpi_s[...]
            d_lr = jnp.sum(gr * hpr + gi * hpi, axis=0, keepdims=True)
            d_li = jnp.sum(gi * hpr - gr * hpi, axis=0, keepdims=True)
            grb, gib = gr.astype(BF16), gi.astype(BF16)
            du = du + jnp.dot(grb, btr[d].astype(BF16), preferred_element_type=F32)
            du = du + jnp.dot(gib, bti[d].astype(BF16), preferred_element_type=F32)
            d_btr = _dot_tn(grb, ub)
            d_bti = _dot_tn(gib, ub)
            for ref, val in zip((dlr, dli, dbtr, dbti, dctr, dcti), (d_lr, d_li, d_btr, d_bti, d_ctr, d_cti)):
                @pl.when(b == 0)
                def _(ref=ref, val=val):
                    ref[d] = val

                @pl.when(b > 0)
                def _(ref=ref, val=val):
                    ref[d] += val
        du_ref[...] = du

    blk = pl.BlockSpec((None, s, S5_TILE_CH), lambda j, b: (b, 0, j))
    pspecs = _s5_specs(lambda j, b: j)
    params = (lam_r, lam_i, bt_r, bt_i, ct_r, ct_i)
    return pl.pallas_call(
        body,
        name=name,
        grid=(w // S5_TILE_CH, nb),
        in_specs=[blk] + pspecs + [blk],
        out_specs=[blk] + pspecs,
        out_shape=[jax.ShapeDtypeStruct(u.shape, F32)] + [jax.ShapeDtypeStruct(p.shape, F32) for p in params],
        scratch_shapes=[pltpu.VMEM((s, S5_TILE_STATES), F32)] * 6,
        compiler_params=_cparams(("parallel", "arbitrary")),
    )(u, lam_r, lam_i, bt_r, bt_i, ct_r, ct_i, dy)


def small_fwd(name, f, ins, out_shapes):
    n = len(ins)

    def body(*refs):
        for o, r in zip(refs[n:], f([r[...] for r in refs[:n]])):
            o[...] = r

    return pl.pallas_call(
        body, name=name,
        out_shape=[jax.ShapeDtypeStruct(s, F32) for s in out_shapes],
        compiler_params=_cparams(),
    )(*ins)


def small_bwd(name, f, ins, cots):
    n, nc = len(ins), len(cots)

    def body(*refs):
        _, vjp = jax.vjp(f, [r[...] for r in refs[:n]])
        (grads,) = vjp([r[...] for r in refs[n:n + nc]])
        for o, r in zip(refs[n + nc:], grads):
            o[...] = r

    return pl.pallas_call(
        body, name=name,
        out_shape=[jax.ShapeDtypeStruct(a.shape, F32) for a in ins],
        compiler_params=_cparams(),
    )(*ins, *cots)


def _row(x, r):
    return jnp.sum(jnp.where(lax.broadcasted_iota(jnp.int32, x.shape, 0) == r, x, 0.0), axis=0, keepdims=True)


def _col(x, c):
    return jnp.sum(jnp.where(lax.broadcasted_iota(jnp.int32, x.shape, 1) == c, x, 0.0), axis=1, keepdims=True)


def _chunk_at(i, reverse, ncc, nc):
    if not reverse:
        return i
    return jnp.where(i < ncc, ncc - 1 - i, nc - 1 - (i - ncc))


def _tri(n, reverse):
    li = lax.broadcasted_iota(jnp.int32, (n, n), 0)
    si = lax.broadcasted_iota(jnp.int32, (n, n), 1)
    return jnp.where((li <= si) if reverse else (li >= si), 1.0, 0.0)


_HI = lax.Precision.HIGHEST


def _ssd_chunk(xs, bm, cm, dtc, dtr, a_row, a_col, hs, reverse):
    n = bm.shape[0]
    last = 0 if reverse else n - 1
    tri = _tri(n, reverse)
    cum_c = jnp.dot(tri, dtc * -jnp.exp(a_row), precision=_HI, preferred_element_type=F32)
    cum_r = lax.dot_general(dtr * -jnp.exp(a_col), tri, (((1,), (1,)), ((), ())), precision=_HI, preferred_element_type=F32)
    tot_r = _row(cum_c, last)
    bmb, cmb = bm.astype(BF16), cm.astype(BF16)
    cb = _dot_nt(cmb, bmb)
    ys, hn = [], []
    for hd in range(len(xs)):
        cl = _col(cum_c, hd)
        tot = _col(tot_r, hd)
        decay = jnp.exp(jnp.where(tri > 0.0, cl - _row(cum_r, hd), -jnp.inf))
        xd = xs[hd] * _col(dtc, hd)
        y = jnp.dot((cb * decay).astype(BF16), xd.astype(BF16), preferred_element_type=F32)
        y = y + _dot_nt(cmb, hs[hd].astype(BF16)) * jnp.exp(cl)
        hnew = hs[hd] * jnp.exp(tot) + _dot_tn((xd * jnp.exp(tot - cl)).astype(BF16), bmb)
        ys.append(y)
        hn.append(hnew)
    return ys, hn


def _ssd_specs(reverse, ncc, nc, order):
    ch = lambda *g: _chunk_at(order(*g)[1], reverse, ncc, nc)
    b_ = lambda *g: order(*g)[0]
    gn = SSD_GROUPS * SSD_STATE
    return [
        pl.BlockSpec((None, SSD_CHUNK, SSD_W), lambda *g: (b_(*g), ch(*g), 0)),
        pl.BlockSpec((None, SSD_CHUNK, gn), lambda *g: (b_(*g), ch(*g), SSD_W // gn)),
        pl.BlockSpec((None, SSD_CHUNK, gn), lambda *g: (b_(*g), ch(*g), SSD_W // gn + 1)),
        pl.BlockSpec((None, SSD_GROUPS, SSD_CHUNK, SSD_HPG), lambda *g: (b_(*g), 0, ch(*g), 0)),
        pl.BlockSpec((None, SSD_GROUPS, SSD_HPG, SSD_CHUNK), lambda *g: (b_(*g), 0, 0, ch(*g))),
        pl.BlockSpec((SSD_GROUPS, 1, SSD_HPG), lambda *g: (0, 0, 0)),
        pl.BlockSpec((SSD_GROUPS, SSD_HPG, 1), lambda *g: (0, 0, 0)),
    ]


def _ssd_group_inputs(g, x_ref, bm_ref, cm_ref, dtc_ref, dtr_ref, ar_ref, ac_ref):
    p, n = SSD_HEAD_DIM, SSD_STATE
    xs = [x_ref[:, p * (SSD_HPG * g + hd):p * (SSD_HPG * g + hd + 1)] for hd in range(SSD_HPG)]
    return xs, bm_ref[:, n * g:n * (g + 1)], cm_ref[:, n * g:n * (g + 1)], dtc_ref[g], dtr_ref[g], ar_ref[g], ac_ref[g]


def ssd_fwd(name, xbc, dt_col, dt_row, a_row, a_col, reverse, s_ctx):
    nb, s, _ = xbc.shape
    nc, ncc = s // SSD_CHUNK, s_ctx // SSD_CHUNK
    p = SSD_HEAD_DIM

    def body(x_ref, bm_ref, cm_ref, dtc_ref, dtr_ref, ar_ref, ac_ref, y_ref, hst_ref, h_s):
        i = pl.program_id(1)

        @pl.when(i == 0)
        def _():
            h_s[...] = jnp.zeros_like(h_s)

        hst_ref[...] = h_s[...]
        for g in range(SSD_GROUPS):
            xs, bm, cm, dtc, dtr, ar, ac = _ssd_group_inputs(g, x_ref, bm_ref, cm_ref, dtc_ref, dtr_ref, ar_ref, ac_ref)
            ys, hn = _ssd_chunk(xs, bm, cm, dtc, dtr, ar, ac, [h_s[g, hd] for hd in range(SSD_HPG)], reverse)
            for hd in range(SSD_HPG):
                y_ref[:, p * (SSD_HPG * g + hd):p * (SSD_HPG * g + hd + 1)] = ys[hd]
                h_s[g, hd] = hn[hd]

    state = (SSD_GROUPS, SSD_HPG, SSD_HEAD_DIM, SSD_STATE)
    return pl.pallas_call(
        body,
        name=name,
        grid=(nb, nc),
        in_specs=_ssd_specs(reverse, ncc, nc, lambda b, i: (b, i)),
        out_specs=[pl.BlockSpec((None, SSD_CHUNK, SSD_W), lambda b, i: (b, _chunk_at(i, reverse, ncc, nc), 0)),
                   pl.BlockSpec((None, None) + state, lambda b, i: (b, i, 0, 0, 0, 0))],
        out_shape=[jax.ShapeDtypeStruct((nb, s, SSD_W), F32), jax.ShapeDtypeStruct((nb, nc) + state, F32)],
        scratch_shapes=[pltpu.VMEM(state, F32)],
        compiler_params=_cparams(("parallel", "arbitrary")),
    )(xbc, xbc, xbc, dt_col, dt_row, a_row, a_col)


def ssd_bwd(name, xbc, dt_col, dt_row, a_row, a_col, hst, dy, reverse, s_ctx):
    nb, s, _ = xbc.shape
    nc, ncc = s // SSD_CHUNK, s_ctx // SSD_CHUNK
    p, n = SSD_HEAD_DIM, SSD_STATE

    def body(x_ref, bm_ref, cm_ref, dtc_ref, dtr_ref, ar_ref, ac_ref, hst_ref, dy_ref,
             dx_ref, dbm_ref, dcm_ref, ddtc_ref, ddtr_ref, dar_ref, dac_ref, dh_s):
        i = pl.program_id(1)

        @pl.when(i == 0)
        def _():
            dh_s[...] = jnp.zeros_like(dh_s)

        for g in range(SSD_GROUPS):
            xs, bm, cm, dtc, dtr, ar, ac = _ssd_group_inputs(g, x_ref, bm_ref, cm_ref, dtc_ref, dtr_ref, ar_ref, ac_ref)
            hs = [hst_ref[g, hd] for hd in range(SSD_HPG)]
            _, vjp = jax.vjp(functools.partial(_ssd_chunk, reverse=reverse), xs, bm, cm, dtc, dtr, ar, ac, hs)
            dys = [dy_ref[:, p * (SSD_HPG * g + hd):p * (SSD_HPG * g + hd + 1)] for hd in range(SSD_HPG)]
            dxs, dbm, dcm, ddtc, ddtr, dar, dac, dhs = vjp((dys, [dh_s[g, hd] for hd in range(SSD_HPG)]))
            for hd in range(SSD_HPG):
                dx_ref[:, p * (SSD_HPG * g + hd):p * (SSD_HPG * g + hd + 1)] = dxs[hd]
                dh_s[g, hd] = dhs[hd]
            dbm_ref[:, n * g:n * (g + 1)] = dbm
            dcm_ref[:, n * g:n * (g + 1)] = dcm
            ddtc_ref[g] = ddtc
            ddtr_ref[g] = ddtr

            @pl.when(i == 0)
            def _(g=g, dar=dar, dac=dac):
                dar_ref[g] = dar
                dac_ref[g] = dac

            @pl.when(i > 0)
            def _(g=g, dar=dar, dac=dac):
                dar_ref[g] += dar
                dac_ref[g] += dac

    ch = lambda b, i: _chunk_at(nc - 1 - i, reverse, ncc, nc)
    state = (SSD_GROUPS, SSD_HPG, SSD_HEAD_DIM, SSD_STATE)
    gn = SSD_GROUPS * SSD_STATE
    in_specs = _ssd_specs(reverse, ncc, nc, lambda b, i: (b, nc - 1 - i)) + [
        pl.BlockSpec((None, None) + state, lambda b, i: (b, nc - 1 - i, 0, 0, 0, 0)),
        pl.BlockSpec((None, SSD_CHUNK, SSD_W), lambda b, i: (b, ch(b, i), 0))]
    out_specs = [
        pl.BlockSpec((None, SSD_CHUNK, SSD_W), lambda b, i: (b, ch(b, i), 0)),
        pl.BlockSpec((None, SSD_CHUNK, gn), lambda b, i: (b, ch(b, i), 0)),
        pl.BlockSpec((None, SSD_CHUNK, gn), lambda b, i: (b, ch(b, i), 0)),
        pl.BlockSpec((None, SSD_GROUPS, SSD_CHUNK, SSD_HPG), lambda b, i: (b, 0, ch(b, i), 0)),
        pl.BlockSpec((None, SSD_GROUPS, SSD_HPG, SSD_CHUNK), lambda b, i: (b, 0, 0, ch(b, i))),
        pl.BlockSpec((None, SSD_GROUPS, 1, SSD_HPG), lambda b, i: (b, 0, 0, 0)),
        pl.BlockSpec((None, SSD_GROUPS, SSD_HPG, 1), lambda b, i: (b, 0, 0, 0)),
    ]
    out_shape = [
        jax.ShapeDtypeStruct((nb, s, SSD_W), F32),
        jax.ShapeDtypeStruct((nb, s, gn), F32),
        jax.ShapeDtypeStruct((nb, s, gn), F32),
        jax.ShapeDtypeStruct(dt_col.shape, F32),
        jax.ShapeDtypeStruct(dt_row.shape, F32),
        jax.ShapeDtypeStruct((nb, SSD_GROUPS, 1, SSD_HPG), F32),
        jax.ShapeDtypeStruct((nb, SSD_GROUPS, SSD_HPG, 1), F32),
    ]
    return pl.pallas_call(
        body,
        name=name,
        grid=(nb, nc),
        in_specs=in_specs,
        out_specs=out_specs,
        out_shape=out_shape,
        scratch_shapes=[pltpu.VMEM(state, F32)],
        compiler_params=_cparams(("parallel", "arbitrary")),
    )(xbc, xbc, xbc, dt_col, dt_row, a_row, a_col, hst, dy)


HG_TILES = HG_CHUNK // SUBLANES
HG_HEADS_PER_STEP = 3


def _hg_cum_tiles(x_t, reverse):
    row = lax.broadcasted_iota(jnp.int32, x_t[0].shape, 0)
    out = [None] * len(x_t)
    off = None
    for i in (reversed(range(len(x_t))) if reverse else range(len(x_t))):
        c = x_t[i]
        for k in (1, 2, 4):
            keep = (row < SUBLANES - k) if reverse else (row >= k)
            c = jnp.where(keep, c + pltpu.roll(c, SUBLANES - k if reverse else k, axis=0), c)
        out[i] = c if off is None else c + off
        off = _last_row(out[i], reverse)
    return out, off


def _hg_pairs(reverse):
    row = lax.broadcasted_iota(jnp.int32, (SUBLANES, HG_DK), 0)
    rots = []
    for r in range(SUBLANES):
        rots.append(((SUBLANES - r) % SUBLANES, row <= SUBLANES - 1 - r) if reverse else (r, row >= r))
    return [(j, [i for i in range(HG_TILES) if (i <= j if reverse else i >= j)], rots) for j in range(HG_TILES)]


def _rot(x, sh):
    return pltpu.roll(x, sh, axis=0) if sh else x


def _cat(tiles):
    return jnp.concatenate(tiles, axis=0)


def _hg_chunk_fwd(q_t, k_t, lf_t, v_t, st, reverse):
    cum_t, tot = _hg_cum_tiles(lf_t, reverse)
    y_t = [jnp.zeros(v_t[0].shape, F32) for _ in v_t]
    for j, l_tiles, rots in _hg_pairs(reverse):
        for sh, diag_ok in rots:
            k_j, c_j, v_j = _rot(k_t[j], sh), _rot(cum_t[j], sh), _rot(v_t[j], sh)
            for i in l_tiles:
                e = jnp.exp(cum_t[i] - c_j)
                if i == j:
                    e = jnp.where(diag_ok, e, 0.0)
                att = jnp.sum(q_t[i] * (k_j * e), axis=1, keepdims=True)
                y_t[i] = y_t[i] + att * v_j
    q, k, v, cum = _cat(q_t), _cat(k_t), _cat(v_t), _cat(cum_t)
    y_state = _dot_nt((q * jnp.exp(cum)).astype(BF16), st.astype(BF16))
    st_new = st * jnp.exp(tot) + _dot_tn(v.astype(BF16), (k * jnp.exp(tot - cum)).astype(BF16))
    return [y + y_state[SUBLANES * i:SUBLANES * (i + 1)] for i, y in enumerate(y_t)], st_new


def _hg_chunk_bwd(q_t, k_t, lf_t, v_t, st, dy_t, dst_new, reverse):
    nt = len(q_t)
    cum_t, tot = _hg_cum_tiles(lf_t, reverse)
    q, k, v, cum, dy = _cat(q_t), _cat(k_t), _cat(v_t), _cat(cum_t), _cat(dy_t)
    e_cum, e_tot, e_end = jnp.exp(cum), jnp.exp(tot), jnp.exp(tot - cum)
    qt, khat = q * e_cum, k * e_end
    dyb, dsb = dy.astype(BF16), dst_new.astype(BF16)
    dqt = jnp.dot(dyb, st.astype(BF16), preferred_element_type=F32)
    dst = dst_new * e_tot + _dot_tn(dyb, qt.astype(BF16))
    dv = _dot_nt(khat.astype(BF16), dsb)
    dkhat = jnp.dot(v.astype(BF16), dsb, preferred_element_type=F32)
    t1 = dkhat * khat
    dtot = jnp.sum(dst_new * st, axis=0, keepdims=True) * e_tot + jnp.sum(t1, axis=0, keepdims=True)
    rows = lax.broadcasted_iota(jnp.int32, cum.shape, 0)
    last = 0 if reverse else cum.shape[0] - 1
    dcum = dqt * qt - t1 + jnp.where(rows == last, dtot, 0.0)
    tiles = lambda a: [a[SUBLANES * i:SUBLANES * (i + 1)] for i in range(nt)]
    dq_t, dk_t, dv_t, dcum_t = tiles(dqt * e_cum), tiles(dkhat * e_end), tiles(dv), tiles(dcum)
    for j, l_tiles, rots in _hg_pairs(reverse):
        for sh, diag_ok in rots:
            k_j, c_j, v_j = _rot(k_t[j], sh), _rot(cum_t[j], sh), _rot(v_t[j], sh)
            acc_v = acc_k = acc_c = None
            for i in l_tiles:
                e = jnp.exp(cum_t[i] - c_j)
                if i == j:
                    e = jnp.where(diag_ok, e, 0.0)
                ke, qe = k_j * e, q_t[i] * e
                p = q_t[i] * ke
                att = jnp.sum(p, axis=1, keepdims=True)
                datt = jnp.sum(dy_t[i] * v_j, axis=1, keepdims=True)
                g = datt * p
                dq_t[i] = dq_t[i] + datt * ke
                dcum_t[i] = dcum_t[i] + g
                av, ak = att * dy_t[i], datt * qe
                acc_v, acc_k, acc_c = (av, ak, g) if acc_v is None else (acc_v + av, acc_k + ak, acc_c + g)
            back = (SUBLANES - sh) % SUBLANES
            dv_t[j] = dv_t[j] + _rot(acc_v, back)
            dk_t[j] = dk_t[j] + _rot(acc_k, back)
            dcum_t[j] = dcum_t[j] - _rot(acc_c, back)
    dlf_t, _ = _hg_cum_tiles(dcum_t, not reverse)
    return dq_t, dk_t, dlf_t, dv_t, dst


def _hg_super(s_ctx):
    return min(256, s_ctx)


def hg_fwd(name, q, k, lf, v, reverse, s_ctx):
    nb, s, w = q.shape
    nh, dk, sup = w // HG_DK, HG_DK, _hg_super(s_ctx)
    nsup, nsc, cps = s // sup, s_ctx // sup, sup // HG_CHUNK
    hps = HG_HEADS_PER_STEP if nh % HG_HEADS_PER_STEP == 0 else 1

    def body(q_ref, k_ref, lf_ref, v_ref, y_ref, hst_ref, st_s):
        i = pl.program_id(2)

        @pl.when(i == 0)
        def _():
            st_s[...] = jnp.zeros_like(st_s)

        def step(c, sts):
            r0 = pl.multiple_of((cps - 1 - c if reverse else c) * HG_CHUNK, HG_CHUNK)
            new = []
            for hh in range(hps):
                cols = slice(hh * dk, (hh + 1) * dk)
                tile = lambda ref: [ref[pl.ds(r0 + SUBLANES * i, SUBLANES), cols] for i in range(HG_TILES)]
                hst_ref[hh, c] = sts[hh]
                y_t, st_new = _hg_chunk_fwd(tile(q_ref), tile(k_ref), tile(lf_ref), tile(v_ref), sts[hh], reverse)
                for i in range(HG_TILES):
                    y_ref[pl.ds(r0 + SUBLANES * i, SUBLANES), cols] = y_t[i]
                new.append(st_new)
            return tuple(new)

        out = lax.fori_loop(0, cps, step, tuple(st_s[hh] for hh in range(hps)), unroll=2 if cps % 2 == 0 else 1)
        for hh in range(hps):
            st_s[hh] = out[hh]

    blk = pl.BlockSpec((None, sup, hps * dk), lambda b, h, i: (b, _chunk_at(i, reverse, nsc, nsup), h))
    return pl.pallas_call(
        body,
        name=name,
        grid=(nb, nh // hps, nsup),
        in_specs=[blk] * 4,
        out_specs=[blk, pl.BlockSpec((None, hps, cps, dk, dk), lambda b, h, i: (b, h, i, 0, 0))],
        out_shape=[jax.ShapeDtypeStruct(q.shape, F32), jax.ShapeDtypeStruct((nb, nh, s // HG_CHUNK, dk, dk), F32)],
        scratch_shapes=[pltpu.VMEM((hps, dk, dk), F32)],
        compiler_params=_cparams(("parallel", "parallel", "arbitrary")),
    )(q, k, lf, v)


def hg_bwd(name, q, k, lf, v, hst, dy, reverse, s_ctx):
    nb, s, w = q.shape
    nh, dk, sup = w // HG_DK, HG_DK, _hg_super(s_ctx)
    nsup, nsc, cps = s // sup, s_ctx // sup, sup // HG_CHUNK
    hps = HG_HEADS_PER_STEP if nh % HG_HEADS_PER_STEP == 0 else 1

    def body(q_ref, k_ref, lf_ref, v_ref, hst_ref, dy_ref, dq_ref, dk_ref, dlf_ref, dv_ref, dst_s):
        i = pl.program_id(2)

        @pl.when(i == 0)
        def _():
            dst_s[...] = jnp.zeros_like(dst_s)

        def step(cc, dsts):
            c = cps - 1 - cc
            r0 = pl.multiple_of((cps - 1 - c if reverse else c) * HG_CHUNK, HG_CHUNK)
            new = []
            for hh in range(hps):
                cols = slice(hh * dk, (hh + 1) * dk)
                tile = lambda ref: [ref[pl.ds(r0 + SUBLANES * i, SUBLANES), cols] for i in range(HG_TILES)]
                dq_t, dk_t, dlf_t, dv_t, dst_prev = _hg_chunk_bwd(
                    tile(q_ref), tile(k_ref), tile(lf_ref), tile(v_ref), hst_ref[hh, c], tile(dy_ref), dsts[hh], reverse)
                for ref, val in zip((dq_ref, dk_ref, dlf_ref, dv_ref), (dq_t, dk_t, dlf_t, dv_t)):
                    for i in range(HG_TILES):
                        ref[pl.ds(r0 + SUBLANES * i, SUBLANES), cols] = val[i]
                new.append(dst_prev)
            return tuple(new)

        out = lax.fori_loop(0, cps, step, tuple(dst_s[hh] for hh in range(hps)), unroll=2 if cps % 2 == 0 else 1)
        for hh in range(hps):
            dst_s[hh] = out[hh]

    blk = pl.BlockSpec((None, sup, hps * dk), lambda b, h, i: (b, _chunk_at(nsup - 1 - i, reverse, nsc, nsup), h))
    return pl.pallas_call(
        body,
        name=name,
        grid=(nb, nh // hps, nsup),
        in_specs=[blk] * 4 + [pl.BlockSpec((None, hps, cps, dk, dk), lambda b, h, i: (b, h, nsup - 1 - i, 0, 0)), blk],
        out_specs=[blk] * 4,
        out_shape=[jax.ShapeDtypeStruct(q.shape, F32)] * 4,
        scratch_shapes=[pltpu.VMEM((hps, dk, dk), F32)],
        compiler_params=_cparams(("parallel", "parallel", "arbitrary")),
    )(q, k, lf, v, hst, dy)


def f_s5_discretize(ins):
    lam_re, lam_im, log_step, b_re, b_im = ins
    step = jnp.exp(log_step)
    mag = jnp.exp(lam_re * step)
    ar, ai = mag * jnp.cos(lam_im * step), mag * jnp.sin(lam_im * step)
    den = lam_re * lam_re + lam_im * lam_im
    zr = ((ar - 1.0) * lam_re + ai * lam_im) / den
    zi = (ai * lam_re - (ar - 1.0) * lam_im) / den
    return [ar, ai, zr * b_re - zi * b_im, zr * b_im + zi * b_re]


def s5_tiles_of(m):
    g, p, k = m.shape
    gt = S5_TILE_CH // k
    eye = jnp.eye(gt, dtype=m.dtype)
    t = m.reshape(g // gt, gt, p, 1, k) * eye[None, :, None, :, None]
    return t.reshape(g // gt, gt * p, gt * k)


def s5_groups_of(t, g, p, k):
    gt = S5_TILE_CH // k
    eye = jnp.eye(gt, dtype=t.dtype)
    return jnp.sum(t.reshape(g // gt, gt, p, gt, k) * eye[None, :, None, :, None], axis=3).reshape(g, p, k)


def f_lower_bounds(ins):
    (logits,) = ins
    e = jnp.exp(logits - jnp.max(logits, axis=0, keepdims=True))
    p = e / jnp.sum(e, axis=0, keepdims=True)
    n = logits.shape[0]
    li = lax.broadcasted_iota(jnp.int32, (n, n), 0)
    si = lax.broadcasted_iota(jnp.int32, (n, n), 1)
    after_first = jnp.where(jnp.logical_and(si >= 1, si <= li), 1.0, 0.0)
    return [jnp.dot(after_first, p, precision=_HI, preferred_element_type=F32)]


def f_silu(ins):
    return [_silu(ins[0])]


def f_norm_keep(shift_row, scale_row):
    def f(tv, mv, pv):
        return [tv[0], _rms(tv[0], pv[0]) * (1.0 + mv[scale_row]) + mv[shift_row]]
    return f


def f_dt(tv, mv, pv):
    return [_softplus(tv[0] + pv[0])]


def f_even_finish(tv, mv, pv):
    y_f, y_b, xs, z, h_sum, gy = tv
    d_exp, g = pv
    y = _rms((y_f + y_b + d_exp * xs) * _silu(z), g)
    return [y, h_sum * jax.nn.gelu(gy)]


def f_odd_prep(tv, mv, pv):
    q, f_f, f_b = tv
    (lb,) = pv
    outs = [_silu(q)]
    for f in (f_f, f_b):
        outs.append((1.0 - lb) * jax.nn.sigmoid(-f))
        outs.append(jnp.log(lb + (1.0 - lb) * jax.nn.sigmoid(f)))
    return outs


def f_odd_finish(tv, mv, pv):
    o_f, o_b, g, s5y, u = tv
    norm_g, s5_d, glu_w, glu_b = pv
    o = o_f + o_b
    w = o.shape[1]
    hi = lax.broadcasted_iota(jnp.int32, (w, w), 0) // HG_DK
    hj = lax.broadcasted_iota(jnp.int32, (w, w), 1) // HG_DK
    head_mean = jnp.where(hi == hj, 1.0 / HG_DK, 0.0)
    ms = jnp.dot(o * o, head_mean, precision=_HI, preferred_element_type=F32)
    on = o * lax.rsqrt(ms + RMS_EPS) * norm_g * _silu(g)
    y = jax.nn.gelu(s5y + s5_d * u)
    gate = jax.nn.sigmoid(jnp.dot(y.astype(BF16), glu_w.astype(BF16), preferred_element_type=F32) + glu_b)
    return [on, y * gate]


def final_loss(name, s, br, mod, g, target, s_ctx):
    nb, st, d = s.shape
    tb = TOK_BLOCK
    assert s_ctx == tb

    def lossf(sv, bv, gate, gv, tv):
        y = _rms(sv + gate * bv, gv)
        err = jnp.square(y - tv)
        return 0.5 * jnp.sum(jnp.mean(err, axis=-1, keepdims=True), axis=0, keepdims=True)

    def body(s_ref, b_ref, m_ref, g_ref, t_ref, l_ref, ds_ref, db_ref, dm_ref, dg_ref):
        b, t = pl.program_id(0), pl.program_id(1)

        @pl.when(t == 0)
        def _():
            ds_ref[...] = jnp.zeros_like(ds_ref)
            db_ref[...] = jnp.zeros_like(db_ref)
            dm_ref[...] = jnp.zeros_like(dm_ref)
            l_ref[...] = jnp.zeros_like(l_ref)

        @pl.when(jnp.logical_and(b == 0, t == 0))
        def _():
            dg_ref[...] = jnp.zeros_like(dg_ref)

        @pl.when(t > 0)
        def _():
            gate = m_ref[N_MOD - 1:N_MOD, :]
            l, vjp = jax.vjp(lossf, s_ref[...], b_ref[...], gate, g_ref[...], t_ref[...])
            ds, db, dgate, dg, _ = vjp(jnp.ones((1, 1), F32))
            ds_ref[...] = ds
            db_ref[...] = db.astype(db_ref.dtype)
            dg_ref[...] += dg
            l_ref[...] += jnp.broadcast_to(l, l_ref.shape)

            @pl.when(t == 1)
            def _():
                dm_ref[...] = jnp.zeros_like(dm_ref)
                dm_ref[N_MOD - 1:N_MOD, :] = dgate

            @pl.when(t > 1)
            def _():
                dm_ref[N_MOD - 1:N_MOD, :] += dgate

    tok = pl.BlockSpec((None, tb, d), lambda b, t: (b, t, 0))
    modspec = pl.BlockSpec((None, N_MOD, d), _mod_index)
    gspec = pl.BlockSpec((1, d), lambda b, t: (0, 0))
    return pl.pallas_call(
        body,
        name=name,
        grid=(nb, st // tb),
        in_specs=[tok, tok, modspec, gspec, pl.BlockSpec((None, tb, d), lambda b, t: (b, jnp.maximum(t - 1, 0), 0))],
        out_specs=[pl.BlockSpec((None, SUBLANES, 128), lambda b, t: (b, 0, 0)), tok, tok, modspec, gspec],
        out_shape=[jax.ShapeDtypeStruct((nb, SUBLANES, 128), F32), jax.ShapeDtypeStruct(s.shape, F32),
                   jax.ShapeDtypeStruct(s.shape, BF16), jax.ShapeDtypeStruct(mod.shape, F32), jax.ShapeDtypeStruct(g.shape, F32)],
        compiler_params=_cparams(("arbitrary", "arbitrary")),
    )(s, br, mod, g, target)


def adamw(name, w, g, m, v):
    shape = w.shape
    cols = shape[-1] if w.ndim >= 2 else w.size
    rows = w.size // cols
    tr = _pick(rows, (512, 256, 128, 64, 32, 16, 8))

    def body(w_ref, g_ref, m_ref, v_ref, d_ref, nm_ref, nv_ref):
        gv = g_ref[...]
        nm = ADAM_B1 * m_ref[...] + (1.0 - ADAM_B1) * gv
        nv = ADAM_B2 * v_ref[...] + (1.0 - ADAM_B2) * jnp.square(gv)
        m_hat = nm / (1.0 - ADAM_B1 ** ADAM_STEP)
        v_hat = nv / (1.0 - ADAM_B2 ** ADAM_STEP)
        d_ref[...] = -ADAM_LR * (m_hat / (jnp.sqrt(v_hat) + ADAM_EPS) + ADAM_WD * w_ref[...])
        nm_ref[...] = nm
        nv_ref[...] = nv

    spec = pl.BlockSpec((tr, cols), lambda i: (i, 0))
    outs = pl.pallas_call(
        body,
        name=name,
        grid=(rows // tr,),
        in_specs=[spec] * 4,
        out_specs=[spec] * 3,
        out_shape=[jax.ShapeDtypeStruct((rows, cols), F32)] * 3,
        compiler_params=_cparams(("parallel",)),
    )(*(a.reshape(rows, cols) for a in (w, g, m, v)))
    return tuple(o.reshape(shape) for o in outs)


EV_COLS = {"z": (0, 1024), "xbc": (1024, 2560), "dt": (2560, 2592), "gy": (2592, 3616), "u": (3616, 4640)}
OD_COLS = {"q": (0, 768), "ff": (768, 1536), "fb": (1536, 2304), "v": (2304, 3072), "g": (3072, 3840), "u": (3840, 4096)}
EV_OUT_ROWS = ((0, 1024), (1024, 2048))
OD_OUT_ROWS = ((0, 768), (768, 1024))
LANES = 128


def _pad_to_lanes(w):
    n = w.shape[1]
    return w if n % LANES == 0 else jnp.pad(w, ((0, 0), (0, LANES - n % LANES)))


def _layer_weights(l, big):
    j = l // 2
    even = l % 2 == 0
    w_in = big["ev_w_in" if even else "od_w_in"][j]
    w_out = big["ev_w_out" if even else "od_w_out"][j]
    lw = {"in": {}, "out": []}
    for name, (a, b) in (EV_COLS if even else OD_COLS).items():
        w = _pad_to_lanes(w_in[:, a:b])
        lw["in"][name] = (w, w.T)
    for a, b in (EV_OUT_ROWS if even else OD_OUT_ROWS):
        lw["out"].append((w_out[a:b], w_out[a:b].T))
    for name in ("gate", "up", "down"):
        w = big["ffn_w_" + name][l]
        lw[name] = (w, w.T)
    return lw


def _rows2d(a):
    return a.reshape(-1, a.shape[-1])


def _mm3(a, w, name, out_dtype=F32):
    return mm([(_rows2d(a), w)], name, out_dtype).reshape(a.shape[:-1] + (w.shape[1],))


def _wgrad(a, d, name):
    return mm_tn(_rows2d(a), _rows2d(d), name)


def _dgrad(pairs, name, shape3):
    return mm([(_rows2d(d), wt) for d, wt in pairs], name).reshape(shape3[:-1] + (pairs[0][1].shape[1],))


def _dir_dt(dt, d):
    nb, s, _ = dt.shape
    dd = dt[:, :, SSD_HEADS * d:SSD_HEADS * (d + 1)].reshape(nb, s, SSD_GROUPS, SSD_HPG)
    return jnp.transpose(dd, (0, 2, 1, 3)), jnp.transpose(dd, (0, 2, 3, 1))


def _s5_prepare(p, j, tag):
    g_, p_, k_ = S5_GROUPS, S5_STATE, S5_GROUP_CH
    col = lambda t: t.reshape(g_ * p_, 1)
    ins, outs = [], []
    for d in (0, 1):
        i_d = [col(p["s5_lam_re"][j, d]), col(p["s5_lam_im"][j, d]), col(jnp.repeat(p["s5_log_step"][j, d], p_)),
               p["s5_b_re"][j].reshape(g_ * p_, k_), p["s5_b_im"][j].reshape(g_ * p_, k_)]
        ins.append(i_d)
        outs.append(small_fwd(f"{tag}_disc{d}", f_s5_discretize, i_d, [(g_ * p_, 1)] * 2 + [(g_ * p_, k_)] * 2))
    lam_r = jnp.stack([o[0].reshape(1, g_ * p_) for o in outs])
    lam_i = jnp.stack([o[1].reshape(1, g_ * p_) for o in outs])
    bt_r = jnp.stack([s5_tiles_of(o[2].reshape(g_, p_, k_)) for o in outs])
    bt_i = jnp.stack([s5_tiles_of(o[3].reshape(g_, p_, k_)) for o in outs])
    ct_r = jnp.stack([s5_tiles_of(jnp.transpose(p["s5_c_re"][j, d], (0, 2, 1))) for d in (0, 1)])
    ct_i = jnp.stack([s5_tiles_of(jnp.transpose(p["s5_c_im"][j, d], (0, 2, 1))) for d in (0, 1)])
    return ins, (lam_r, lam_i, bt_r, bt_i, ct_r, ct_i)


def _s5_param_grads(ins, grads, tag):
    g_, p_, k_ = S5_GROUPS, S5_STATE, S5_GROUP_CH
    dlr, dli, dbtr, dbti, dctr, dcti = grads
    g_lre, g_lim, g_ls, g_bre, g_bim = [], [], [], 0.0, 0.0
    for d in (0, 1):
        cots = [dlr[d].reshape(g_ * p_, 1), dli[d].reshape(g_ * p_, 1),
                s5_groups_of(dbtr[d], g_, p_, k_).reshape(g_ * p_, k_), s5_groups_of(dbti[d], g_, p_, k_).reshape(g_ * p_, k_)]
        g = small_bwd(f"{tag}_disc_bwd{d}", f_s5_discretize, ins[d], cots)
        g_lre.append(g[0].reshape(g_, p_))
        g_lim.append(g[1].reshape(g_, p_))
        g_ls.append(g[2].reshape(g_, p_).sum(-1))
        g_bre = g_bre + g[3].reshape(g_, p_, k_)
        g_bim = g_bim + g[4].reshape(g_, p_, k_)
    g_cre = jnp.stack([jnp.transpose(s5_groups_of(dctr[d], g_, p_, k_), (0, 2, 1)) for d in (0, 1)])
    g_cim = jnp.stack([jnp.transpose(s5_groups_of(dcti[d], g_, p_, k_), (0, 2, 1)) for d in (0, 1)])
    return jnp.stack(g_lre), jnp.stack(g_lim), jnp.stack(g_ls), g_bre, g_bim, g_cre, g_cim


def _even_mixer_fwd(l, hn, p, lw, s_ctx):
    j = l // 2
    t1 = taps_1d(4, s_ctx, hn.shape[1])
    r = {"hn": hn}
    proj = {n: _mm3(hn, lw["in"][n][0], f"l{l}_proj_{n}") for n in EV_COLS}
    r["z"], r["xbc"], r["gy"], r["u"] = proj["z"], proj["xbc"], proj["gy"], proj["u"]
    r["dtp"] = proj["dt"][:, :, :2 * SSD_HEADS]
    r["xbc_c"] = conv_fwd(f"l{l}_ssd_conv", r["xbc"], p["ssd_conv_w"][j], p["ssd_conv_b"][j][None], t1, "silu")
    r["u_c"] = conv_fwd(f"l{l}_lru_conv", r["u"], p["lru_conv_w"][j], p["lru_conv_b"][j][None], t1, "none")
    r["dt_bias"] = p["ssd_dt_bias"][j].reshape(1, 2 * SSD_HEADS)
    (r["dt"],) = tok_fwd(f"l{l}_dt", f_dt, [r["dtp"]], None, [r["dt_bias"]], [2 * SSD_HEADS], [F32])
    r["ys"], r["hst"], r["dts"], r["alog"] = [], [], [], []
    for d in (0, 1):
        dtc, dtr = _dir_dt(r["dt"], d)
        al = p["ssd_a_log"][j, d].reshape(SSD_GROUPS, SSD_HPG)
        al_r, al_c = al[:, None, :], al[:, :, None]
        y, hst = ssd_fwd(f"l{l}_ssd_fwd{d}", r["xbc_c"], dtc, dtr, al_r, al_c, bool(d), s_ctx)
        r["ys"].append(y)
        r["hst"].append(hst)
        r["dts"].append((dtc, dtr))
        r["alog"].append((al_r, al_c))
    v4 = lambda t: t.reshape(2, LRU_BLOCKS, 1, LRU_BLOCK_W)
    r["lru_p"] = (p["lru_w_a"][j], v4(p["lru_b_a"][j]), p["lru_w_i"][j], v4(p["lru_b_i"][j]), v4(p["lru_lam"][j]))
    r["h_sum"] = lru_fwd(f"l{l}_lru_fwd", r["u_c"], *r["lru_p"], s_ctx)
    r["xs"] = r["xbc_c"][:, :, :SSD_HEADS * SSD_HEAD_DIM]
    r["fin_p"] = [jnp.repeat(p["ssd_d"][j], SSD_HEAD_DIM)[None], p["ssd_norm_g"][j][None]]
    r["fin_in"] = [r["ys"][0], r["ys"][1], r["xs"], r["z"], r["h_sum"], r["gy"]]
    r["o"] = tok_fwd(f"l{l}_even_finish", f_even_finish, r["fin_in"], None, r["fin_p"], [1024, 1024], [BF16, BF16])
    return r


def _even_mixer_bwd(l, r, dox, p, lw, s_ctx, grads):
    j = l // 2
    shape3 = dox.shape
    t1 = taps_1d(4, s_ctx, shape3[1])
    grads["ev_w_out"][j] = jnp.concatenate([_wgrad(o, dox, f"l{l}_dwout{i}") for i, o in enumerate(r["o"])], axis=0)
    do = [_dgrad([(dox, lw["out"][i][1])], f"l{l}_dout{i}", shape3) for i in range(2)]
    (dy, _, dxs, dz, dh_sum, dgy), _, (dd_exp, grads["ssd_norm_g"][j]) = tok_bwd(
        f"l{l}_even_finish_bwd", f_even_finish, r["fin_in"], None, r["fin_p"], do, [F32, F32, F32, BF16, F32, BF16])
    grads["ssd_d"][j] = dd_exp.reshape(SSD_HEADS, SSD_HEAD_DIM).sum(-1)
    du_c, dwa, dba, dwi, dbi, dlam = lru_bwd(f"l{l}_lru_bwd", r["u_c"], *r["lru_p"], dh_sum, s_ctx)
    grads["lru_w_a"][j], grads["lru_w_i"][j] = dwa, dwi
    v2 = lambda t: t.reshape(2, LRU_BLOCKS * LRU_BLOCK_W)
    grads["lru_b_a"][j], grads["lru_b_i"][j], grads["lru_lam"][j] = v2(dba), v2(dbi), v2(dlam)
    dx_sum, dbm_sum, dcm_sum, ddts, dalog = dxs, 0.0, 0.0, [], []
    for d in (0, 1):
        dx, dbm, dcm, ddtc, ddtr, dar, dac = ssd_bwd(
            f"l{l}_ssd_bwd{d}", r["xbc_c"], *r["dts"][d], *r["alog"][d], r["hst"][d], dy, bool(d), s_ctx)
        dx_sum, dbm_sum, dcm_sum = dx_sum + dx, dbm_sum + dbm, dcm_sum + dcm
        ddts.append((jnp.transpose(ddtc, (0, 2, 1, 3)) + jnp.transpose(ddtr, (0, 3, 1, 2))).reshape(shape3[0], shape3[1], SSD_HEADS))
        dalog.append((dar.sum(0)[:, 0, :] + dac.sum(0)[:, :, 0]).reshape(SSD_HEADS))
    grads["ssd_a_log"][j] = jnp.stack(dalog)
    dxbc_c = jnp.concatenate([dx_sum, dbm_sum, dcm_sum], axis=-1)
    (ddtp,), _, (ddt_bias,) = tok_bwd(f"l{l}_dt_bwd", f_dt, [r["dtp"]], None, [r["dt_bias"]], [jnp.concatenate(ddts, axis=-1)], [F32])
    grads["ssd_dt_bias"][j] = ddt_bias.reshape(2, SSD_HEADS)
    dxbc, grads["ssd_conv_w"][j], dcb = conv_bwd(f"l{l}_ssd_conv_bwd", r["xbc"], p["ssd_conv_w"][j], p["ssd_conv_b"][j][None], dxbc_c, t1, "silu", dx_dtype=BF16)
    du, grads["lru_conv_w"][j], dlb = conv_bwd(f"l{l}_lru_conv_bwd", r["u"], p["lru_conv_w"][j], p["lru_conv_b"][j][None], du_c, t1, "none", dx_dtype=BF16)
    grads["ssd_conv_b"][j], grads["lru_conv_b"][j] = dcb[0], dlb[0]
    dproj = {"z": dz, "xbc": dxbc, "dt": _pad_to_lanes(_rows2d(ddtp)).reshape(shape3[:2] + (LANES,)), "gy": dgy, "u": du}
    grads["ev_w_in"][j] = jnp.concatenate(
        [_wgrad(r["hn"], dproj[n], f"l{l}_dwin_{n}")[:, :b - a] for n, (a, b) in EV_COLS.items()], axis=1)
    return _dgrad([(dproj[n], lw["in"][n][1]) for n in EV_COLS], f"l{l}_dhn", shape3)


def _odd_mixer_fwd(l, hn, p, lw, lb_row, s_ctx):
    j = l // 2
    r = {"hn": hn}
    proj = {n: _mm3(hn, lw["in"][n][0], f"l{l}_proj_{n}") for n in OD_COLS}
    r["v"], r["g"], r["u"] = proj["v"], proj["g"], proj["u"]
    r["prep_in"] = [proj["q"], proj["ff"], proj["fb"]]
    r["lb"] = lb_row
    r["prep"] = tok_fwd(f"l{l}_odd_prep", f_odd_prep, r["prep_in"], None, [lb_row], [HG_W] * 5, [F32] * 5)
    qs = r["prep"][0]
    r["os"], r["hst"] = [], []
    for d in (0, 1):
        o, hst = hg_fwd(f"l{l}_hg_fwd{d}", qs, r["prep"][1 + 2 * d], r["prep"][2 + 2 * d], r["v"], bool(d), s_ctx)
        r["os"].append(o)
        r["hst"].append(hst)
    r["s5_ins"], r["s5_p"] = _s5_prepare(p, j, f"l{l}_s5")
    r["s5y"] = s5_fwd(f"l{l}_s5_fwd", r["u"], *r["s5_p"], s_ctx)
    r["fin_p"] = [p["hg_norm_g"][j].reshape(1, HG_W), p["s5_d"][j][None], p["s5_glu_w"][j], p["s5_glu_b"][j][None]]
    r["fin_in"] = [r["os"][0], r["os"][1], r["g"], r["s5y"], r["u"]]
    r["o"] = tok_fwd(f"l{l}_odd_finish", f_odd_finish, r["fin_in"], None, r["fin_p"], [HG_W, S5_W], [BF16, BF16])
    return r


def _odd_mixer_bwd(l, r, dox, p, lw, s_ctx, grads):
    j = l // 2
    shape3 = dox.shape
    grads["od_w_out"][j] = jnp.concatenate([_wgrad(o, dox, f"l{l}_dwout{i}") for i, o in enumerate(r["o"])], axis=0)
    do = [_dgrad([(dox, lw["out"][i][1])], f"l{l}_dout{i}", shape3) for i in range(2)]
    (do_hg, _, dg, ds5y, du_fin), _, (dng, grads["s5_d"][j], grads["s5_glu_w"][j], dglu_b) = tok_bwd(
        f"l{l}_odd_finish_bwd", f_odd_finish, r["fin_in"], None, r["fin_p"], do, [F32, F32, BF16, F32, F32])
    grads["hg_norm_g"][j] = dng.reshape(HG_HEADS, HG_DK)
    grads["s5_d"][j], grads["s5_glu_b"][j] = grads["s5_d"][j][0], dglu_b[0]
    s5g = s5_bwd(f"l{l}_s5_bwd", r["u"], *r["s5_p"], ds5y, s_ctx)
    du = s5g[0] + du_fin
    (grads["s5_lam_re"][j], grads["s5_lam_im"][j], grads["s5_log_step"][j], grads["s5_b_re"][j], grads["s5_b_im"][j],
     grads["s5_c_re"][j], grads["s5_c_im"][j]) = _s5_param_grads(r["s5_ins"], s5g[1:], f"l{l}_s5")
    qs = r["prep"][0]
    dqs, dv, dprep = 0.0, 0.0, [None] * 5
    for d in (0, 1):
        dq, dk, dlf, dvd = hg_bwd(f"l{l}_hg_bwd{d}", qs, r["prep"][1 + 2 * d], r["prep"][2 + 2 * d], r["v"], r["hst"][d], do_hg, bool(d), s_ctx)
        dqs, dv = dqs + dq, dv + dvd
        dprep[1 + 2 * d], dprep[2 + 2 * d] = dk, dlf
    dprep[0] = dqs
    (dq_, dff, dfb), _, (dlb,) = tok_bwd(f"l{l}_odd_prep_bwd", f_odd_prep, r["prep_in"], None, [r["lb"]], dprep, [BF16] * 3)
    dproj = {"q": dq_, "ff": dff, "fb": dfb, "v": dv, "g": dg, "u": du}
    grads["od_w_in"][j] = jnp.concatenate([_wgrad(r["hn"], dproj[n], f"l{l}_dwin_{n}") for n in OD_COLS], axis=1)
    return _dgrad([(dproj[n], lw["in"][n][1]) for n in OD_COLS], f"l{l}_dhn", shape3), dlb


def _ffn_fwd(l, fn, p, lw, s_ctx):
    r = {"fn": fn}
    tg = taps_grid(s_ctx, fn.shape[1], GRID_W)
    r["a"] = _mm3(fn, lw["gate"][0], f"l{l}_ffn_gate")
    r["up"] = _mm3(fn, lw["up"][0], f"l{l}_ffn_up")
    r["cw"], r["cb"] = p["ffn_conv_w"][l].reshape(9, D_FF), p["ffn_conv_b"][l][None]
    r["act"] = conv_fwd(f"l{l}_ffn_conv", r["a"], r["cw"], r["cb"], tg, "silu_mul", mul=r["up"], out_dtype=BF16)
    return r, _mm3(r["act"], lw["down"][0], f"l{l}_ffn_down")


def _ffn_bwd(l, r, dfo, lw, s_ctx, grads):
    shape3 = dfo.shape
    tg = taps_grid(s_ctx, shape3[1], GRID_W)
    grads["ffn_w_down"][l] = _wgrad(r["act"], dfo, f"l{l}_dwdown")
    dact = _dgrad([(dfo, lw["down"][1])], f"l{l}_dact", shape3)
    da, dcw, dcb, dup = conv_bwd(f"l{l}_ffn_conv_bwd", r["a"], r["cw"], r["cb"], dact, tg, "silu_mul", mul=r["up"], dx_dtype=BF16)
    grads["ffn_conv_w"][l], grads["ffn_conv_b"][l] = dcw.reshape(3, 3, D_FF), dcb[0]
    grads["ffn_w_gate"][l] = _wgrad(r["fn"], da, f"l{l}_dwgate")
    grads["ffn_w_up"][l] = _wgrad(r["fn"], dup, f"l{l}_dwup")
    return _dgrad([(da, lw["gate"][1]), (dup, lw["up"][1])], f"l{l}_dfn", shape3)


BIG_WEIGHTS = ("ev_w_in", "ev_w_out", "od_w_in", "od_w_out", "ffn_w_gate", "ffn_w_up", "ffn_w_down")
PER_LAYER = {"norm_mix_g": DEPTH, "norm_ffn_g": DEPTH, "ffn_w_gate": DEPTH, "ffn_w_up": DEPTH, "ffn_conv_w": DEPTH,
             "ffn_conv_b": DEPTH, "ffn_w_down": DEPTH}


def local_step(x, ctx, target, modtabs, p, big, s_ctx=CTX_LEN):
    d_model = x.shape[-1]
    s0 = jnp.concatenate([ctx, x], axis=1)
    lws = [_layer_weights(l, big) for l in range(DEPTH)]
    shapes = {n: v.shape for n, v in {**p, **big}.items()}
    grads = {n: [None] * PER_LAYER.get(n, DEPTH // 2) for n in shapes if n not in ("c_ctx", "w_mod", "b_mod", "final_norm_g", "hg_lb_logits")}
    (lbs,) = small_fwd("lower_bounds", f_lower_bounds, [p["hg_lb_logits"]], [p["hg_lb_logits"].shape])
    tab_a = [modtabs[0]] + [modtabs[l].at[:, N_MOD - 1].set(modtabs[l - 1][:, N_MOD - 1]) for l in range(1, DEPTH)]
    res = []
    s, br = s0, None
    for l in range(DEPTH):
        r = {}
        g_mix, g_ffn = p["norm_mix_g"][l][None], p["norm_ffn_g"][l][None]
        if l == 0:
            (hn,) = tok_fwd("l0_norm", f_norm(0, 1), [s], tab_a[0], [g_mix], [d_model], [BF16])
            r["a_in"] = [s]
        else:
            r["a_in"] = [s, br]
            s, hn = tok_fwd(f"l{l}_resnorm_a", f_resnorm(5, 0, 1), r["a_in"], tab_a[l], [g_mix], [d_model] * 2, [F32, BF16])
        if l % 2 == 0:
            r["mix"] = _even_mixer_fwd(l, hn, p, lws[l], s_ctx)
        else:
            r["mix"] = _odd_mixer_fwd(l, hn, p, lws[l], lbs[l:l + 1], s_ctx)
        ox = mm([(_rows2d(o), w) for o, (w, _) in zip(r["mix"]["o"], lws[l]["out"])], f"l{l}_mix_out").reshape(s.shape)
        r["b_in"] = [s, ox]
        s, fn = tok_fwd(f"l{l}_resnorm_b", f_resnorm(2, 3, 4), r["b_in"], modtabs[l], [g_ffn], [d_model] * 2, [F32, BF16])
        r["ffn"], br = _ffn_fwd(l, fn, p, lws[l], s_ctx)
        res.append(r)

    loss_blk, ds, dbr, dtab_f, dfinal_g = final_loss("final_loss", s, br, modtabs[DEPTH - 1], p["final_norm_g"][None], target, s_ctx)
    grads["final_norm_g"] = dfinal_g[0]
    dmod = [None] * DEPTH
    dtab_next = dtab_f
    dlb = jnp.zeros_like(lbs)
    for l in reversed(range(DEPTH)):
        r = res[l]
        g_mix, g_ffn = p["norm_mix_g"][l][None], p["norm_ffn_g"][l][None]
        dfn = _ffn_bwd(l, r["ffn"], dbr, lws[l], s_ctx, grads)
        (ds, dox), dtab_b, (grads["norm_ffn_g"][l],) = tok_bwd(
            f"l{l}_resnorm_b_bwd", f_resnorm(2, 3, 4), r["b_in"], modtabs[l], [g_ffn], [ds, dfn], [F32, BF16])
        if l % 2 == 0:
            dhn = _even_mixer_bwd(l, r["mix"], dox, p, lws[l], s_ctx, grads)
        else:
            dhn, dlb_l = _odd_mixer_bwd(l, r["mix"], dox, p, lws[l], s_ctx, grads)
            dlb = dlb.at[l:l + 1].set(dlb_l)
        if l == 0:
            (ds,), dtab_a, (dg,) = tok_bwd("l0_norm_bwd", f_norm_keep(0, 1), r["a_in"], tab_a[0], [g_mix], [ds, dhn], [F32])
        else:
            (ds, dbr), dtab_a, (dg,) = tok_bwd(
                f"l{l}_resnorm_a_bwd", f_resnorm(5, 0, 1), r["a_in"], tab_a[l], [g_mix], [ds, dhn], [F32, BF16])
        grads["norm_mix_g"][l] = dg
        dmod[l] = (dtab_a.at[:, N_MOD - 1].set(0.0) + dtab_b).at[:, N_MOD - 1].set(dtab_next[:, N_MOD - 1])
        dtab_next = dtab_a
    (grads["hg_lb_logits"],) = small_bwd("lower_bounds_bwd", f_lower_bounds, [p["hg_lb_logits"]], [dlb])
    out = {}
    for n, g in grads.items():
        if isinstance(g, list):
            g = jnp.stack([t.reshape(shapes[n][1:]) for t in g])
        out[n] = g.reshape(shapes[n])
    return loss_blk[:, 0, 0], ds[:, s_ctx:], dmod, out


WEIGHT_NAMES = (
    "c_ctx", "w_mod", "b_mod", "norm_mix_g", "norm_ffn_g", "final_norm_g", "ev_w_in", "ev_w_out", "ssd_conv_w",
    "ssd_conv_b", "ssd_dt_bias", "ssd_a_log", "ssd_d", "ssd_norm_g", "lru_conv_w", "lru_conv_b", "lru_w_a", "lru_b_a",
    "lru_w_i", "lru_b_i", "lru_lam", "od_w_in", "od_w_out", "hg_lb_logits", "hg_norm_g", "s5_lam_re", "s5_lam_im",
    "s5_log_step", "s5_b_re", "s5_b_im", "s5_c_re", "s5_c_im", "s5_d", "s5_glu_w", "s5_glu_b", "ffn_w_gate", "ffn_w_up",
    "ffn_conv_w", "ffn_conv_b", "ffn_w_down")
INPUT_NAMES = ("x", "c", "ctx") + WEIGHT_NAMES + ("loss_target",) + tuple("m_" + n for n in WEIGHT_NAMES) + tuple("v_" + n for n in WEIGHT_NAMES)
SHARD_AXIS = {"w_mod": 2, "ev_w_in": 2, "ev_w_out": 1, "ssd_conv_w": 2, "lru_conv_w": 2, "lru_b_a": 2, "lru_b_i": 2,
              "lru_lam": 2, "od_w_in": 2, "od_w_out": 1, "s5_d": 1, "s5_glu_w": 1, "s5_glu_b": 1, "ffn_w_gate": 2,
              "ffn_w_up": 2, "ffn_conv_w": 3, "ffn_w_down": 1}
SMALL_SHARDED = tuple(n for n in WEIGHT_NAMES if n in SHARD_AXIS and n not in BIG_WEIGHTS and n != "w_mod")
REPLICATED_LOCAL = tuple(n for n in WEIGHT_NAMES if n not in SHARD_AXIS and n not in ("c_ctx", "b_mod"))
PACK_WIDTH = 1024
MOD_ROWS = 48
CTX_ROW = 32


def _unshard(g8, axis):
    moved = jnp.moveaxis(g8, 0, axis)
    shp = moved.shape
    return moved.reshape(shp[:axis] + (shp[axis] * shp[axis + 1],) + shp[axis + 2:])


def _to_shards(full, axis):
    shp = full.shape
    return jnp.moveaxis(full.reshape(shp[:axis] + (N_DEV, shp[axis] // N_DEV) + shp[axis + 1:]), axis, 0)


def _pack(arrs, dtype, lead=(), row_mult=16):
    flat = jnp.concatenate([a.astype(dtype).reshape(lead + (-1,)) for a in arrs], axis=-1)
    n = flat.shape[-1]
    unit = row_mult * PACK_WIDTH
    padded = -(-n // unit) * unit
    flat = jnp.pad(flat, [(0, 0)] * len(lead) + [(0, padded - n)])
    return flat.reshape(lead + (padded // PACK_WIDTH, PACK_WIDTH))


def _unpack(packed, shapes, lead=()):
    flat = packed.reshape(lead + (-1,))
    out, off = [], 0
    for shp in shapes:
        n = math.prod(shp)
        out.append(flat[..., off:off + n].reshape(lead + tuple(shp)))
        off += n
    return out


def _my_block(full, axis, me):
    loc = full.shape[axis] // N_DEV
    return lax.dynamic_slice_in_dim(full, me * loc, loc, axis)


def kernel(*args):
    a = dict(zip(INPUT_NAMES, args))
    px, py, pc = _my_pos()
    me = 4 * px + 2 * py + pc
    nb = a["x"].shape[0]

    small_names = ("c",) + SMALL_SHARDED
    *big8, small8 = all_gather([a[n].astype(BF16) for n in BIG_WEIGHTS] + [_pack([a[n] for n in small_names], F32)],
                               "gather_weights")
    big = {n: _unshard(g, SHARD_AXIS[n]) for n, g in zip(BIG_WEIGHTS, big8)}
    small = dict(zip(small_names, _unpack(small8, [a[n].shape for n in small_names], (N_DEV,))))
    p = {n: a[n] for n in WEIGHT_NAMES if n not in SHARD_AXIS}
    for n in SMALL_SHARDED:
        p[n] = _unshard(small[n], SHARD_AXIS[n])
    c_all = small["c"].reshape(N_DEV * nb, D_MODEL)

    rows = jnp.concatenate([c_all, a["c_ctx"][None], jnp.zeros((MOD_ROWS - CTX_ROW - 1, D_MODEL), F32)], axis=0)
    (srows,) = small_fwd("mod_silu", f_silu, [rows], [rows.shape])
    wmod2d = jnp.transpose(a["w_mod"], (1, 0, 2)).reshape(D_MODEL, -1).astype(BF16)
    cols = a["w_mod"].shape[2]
    mod_loc = mm([(srows, wmod2d)], "mod_proj")
    mod8 = all_gather([mod_loc], "gather_mod")[0].reshape(N_DEV, MOD_ROWS, DEPTH, cols)
    mod_all = jnp.transpose(mod8, (2, 1, 0, 3)).reshape(DEPTH, MOD_ROWS, N_DEV * cols) + a["b_mod"][:, None, :]
    modtabs = []
    for l in range(DEPTH):
        mine = lax.dynamic_slice_in_dim(mod_all[l], me * nb, nb, 0).reshape(nb, N_MOD, D_MODEL)
        ctx_row = jnp.broadcast_to(mod_all[l, CTX_ROW].reshape(1, N_MOD, D_MODEL), (nb, N_MOD, D_MODEL))
        modtabs.append(jnp.stack([ctx_row, mine], axis=1).reshape(2 * nb, N_MOD, D_MODEL))

    loss_b, grad_x, dmod, grads = local_step(a["x"], a["ctx"], a["loss_target"], modtabs, p, big)

    dm = jnp.stack([t.reshape(nb, 2, N_MOD * D_MODEL) for t in dmod])
    dloc = jnp.concatenate([dm[:, :, 1], jnp.sum(dm[:, :, 0], axis=1, keepdims=True),
                            jnp.zeros((DEPTH, SUBLANES - nb - 1, N_MOD * D_MODEL), F32)], axis=1)
    d8 = all_gather([dloc.reshape(DEPTH * SUBLANES, -1)], "gather_dmod")[0].reshape(N_DEV, DEPTH, SUBLANES, -1)
    d_rows = jnp.transpose(d8[:, :, :nb], (1, 0, 2, 3)).reshape(DEPTH, N_DEV * nb, -1)
    d_ctx = jnp.sum(d8[:, :, nb], axis=0)[:, None]
    d_full = jnp.concatenate([d_rows, d_ctx, jnp.zeros((DEPTH, MOD_ROWS - CTX_ROW - 1, N_MOD * D_MODEL), F32)], axis=1)
    grads["b_mod"] = jnp.sum(d_full, axis=1)
    d_cols = jnp.transpose(_my_block(d_full, 2, me), (1, 0, 2)).reshape(MOD_ROWS, DEPTH * cols)
    g_wmod = mm([(srows.T, d_cols)], "mod_dw")
    g_wmod_local = jnp.transpose(g_wmod.reshape(D_MODEL, DEPTH, cols), (1, 0, 2))
    d_srows_part = mm([(d_cols[CTX_ROW:CTX_ROW + SUBLANES], wmod2d.T)], "mod_dctx")[0]

    reduce_names = REPLICATED_LOCAL + SMALL_SHARDED
    parts = [jnp.sum(loss_b).reshape(1), d_srows_part] + [grads[n] for n in reduce_names]
    packed = _pack(parts, F32, row_mult=SUBLANES * N_DEV)
    own = [_to_shards(grads[n], SHARD_AXIS[n]).astype(BF16).reshape(N_CHIPS, 2, -1, a[n].shape[-1]) for n in BIG_WEIGHTS]
    own.append(packed.reshape(N_CHIPS, 2, -1, PACK_WIDTH))
    names = BIG_WEIGHTS + ("small",)
    from_sibling = sibling_swap(own, "exchange_sibling_grads")
    chip_sums = [pair_sum(o, s, "pair_sum_" + n) for n, o, s in zip(names, own, from_sibling)]
    got = chip_exchange(chip_sums, "exchange_chip_grads")
    eighths = [sum_slots(t, "sum_" + n) for n, t in zip(names, got)]
    (small8,) = all_gather([eighths[-1]], "gather_small_sums")
    totals = _unpack(small8, [(1,), (D_MODEL,)] + [grads[n].shape for n in reduce_names])
    loss = totals[0][0]
    d_srows = jnp.zeros_like(rows).at[CTX_ROW].set(totals[1])
    (d_rows_in,) = small_bwd("mod_silu_bwd", f_silu, [rows], [d_srows])
    g_local = {"c_ctx": d_rows_in[CTX_ROW], "b_mod": grads["b_mod"], "w_mod": g_wmod_local}
    for n, t in zip(reduce_names, totals[2:]):
        g_local[n] = _my_block(t, SHARD_AXIS[n], me) if n in SHARD_AXIS else t
    for n, t in zip(BIG_WEIGHTS, eighths):
        g_local[n] = t.reshape(a[n].shape)

    deltas, new_m, new_v = [], [], []
    for n in WEIGHT_NAMES:
        d, m, v = adamw("adamw_" + n, a[n], g_local[n], a["m_" + n], a["v_" + n])
        deltas.append(d)
        new_m.append(m)
        new_v.append(v)
    return (loss, grad_x, *[g_local[n] for n in WEIGHT_NAMES], *deltas, *new_m, *new_v)
```
